```python
import jax, jax.numpy as jnp
from jax import lax
import numpy as np

D_MODEL = 1024
BATCH = 8
SEQ = 2048
DEPTH = 1
DEC_BATCH = 128
DEC_SEQ = 8
PAST_LEN = 8192
PAGE_SIZE = 128

f32 = jnp.float32
MIX_WIDTH = D_MODEL
GLA_WIDTH = MIX_WIDTH // 2
GLA_HEADS = 4
GLA_DV = GLA_WIDTH // GLA_HEADS
GLA_DK = GLA_DV // 2
GLA_GATE_RANK = 16
GLA_TAU = 16.0
GLA_CHUNK = 64
SWA_WIDTH = MIX_WIDTH - GLA_WIDTH
HEAD_DIM = 64
N_Q_HEADS = SWA_WIDTH // HEAD_DIM
N_KV_HEADS = 2
GQA_GROUP = N_Q_HEADS // N_KV_HEADS
WINDOW = 128
ROPE_THETA = 10000.0
D_FF = 4 * D_MODEL
EPS = 1e-6
IN_SPLITS = (GLA_HEADS * GLA_DK, GLA_HEADS * GLA_DK, GLA_WIDTH, GLA_WIDTH, GLA_GATE_RANK,
             N_Q_HEADS * HEAD_DIM, N_KV_HEADS * HEAD_DIM, N_KV_HEADS * HEAD_DIM)
IN_WIDTH = sum(IN_SPLITS)
SPLIT_POINTS = tuple(int(p) for p in np.cumsum(IN_SPLITS)[:-1])

kernel_name = "hymba_gla_swa_sink_adaln_decoder_step"


def rms_norm(x, w):
    xf = x.astype(f32)
    y = xf * lax.rsqrt(jnp.mean(xf * xf, axis=-1, keepdims=True) + EPS) * w.astype(f32)
    return y.astype(x.dtype)


def rope(x, pos):
    half = HEAD_DIM // 2
    inv = jnp.power(ROPE_THETA, -jnp.arange(half, dtype=f32) * 2.0 / HEAD_DIM)
    ang = pos.astype(f32)[:, None] * inv[None, :]
    cos = jnp.cos(ang)[:, None, :]
    sin = jnp.sin(ang)[:, None, :]
    xf = x.astype(f32)
    x1, x2 = xf[..., :half], xf[..., half:]
    return jnp.concatenate([x1 * cos - x2 * sin, x2 * cos + x1 * sin], axis=-1).astype(x.dtype)


def gla_recurrence(q, k, v, log_a, s0):
    b, l, h, _ = q.shape
    dv = v.shape[-1]
    c = min(GLA_CHUNK, l)
    n = -(-l // c)
    pad = n * c - l
    if pad:
        padw = ((0, 0), (0, pad), (0, 0), (0, 0))
        q, k, v, log_a = [jnp.pad(t, padw) for t in (q, k, v, log_a)]

    def to_chunks(t):
        return t.astype(f32).reshape(b, n, c, h, t.shape[-1]).transpose(1, 0, 3, 2, 4)

    qc, kc, vc, ac = to_chunks(q), to_chunks(k), to_chunks(v), to_chunks(log_a)
    causal = jnp.tril(jnp.ones((c, c), bool))[:, :, None]

    def step(s, inp):
        qi, ki, vi, ai = inp
        bcum = jnp.cumsum(ai, axis=-2)
        diff = bcum[..., :, None, :] - bcum[..., None, :, :]
        decay = jnp.exp(jnp.where(causal, diff, -jnp.inf))
        attn = jnp.einsum('bhtd,bhsd,bhtsd->bhts', qi, ki, decay)
        o = (jnp.einsum('bhts,bhsv->bhtv', attn, vi)
             + jnp.einsum('bhtd,bhdv->bhtv', qi * jnp.exp(bcum), s))
        blast = bcum[..., -1:, :]
        s_new = (jnp.exp(blast)[..., 0, :, None] * s
                 + jnp.einsum('bhsd,bhsv->bhdv', ki * jnp.exp(blast - bcum), vi))
        return s_new, o

    s_fin, oc = lax.scan(step, s0.astype(f32), (qc, kc, vc, ac))
    o = oc.transpose(1, 0, 3, 2, 4).reshape(b, n * c, h, dv)[:, :l]
    return o, s_fin


def sink_attention(q, k, v, mask, sinks):
    s = jnp.einsum('...qhgd,...khd->...hgqk', q.astype(f32), k.astype(f32)) * (HEAD_DIM ** -0.5)
    s = jnp.where(mask, s, -jnp.inf)
    sink = sinks.astype(f32).reshape(N_KV_HEADS, GQA_GROUP, 1, 1)
    m = jnp.maximum(jnp.max(s, axis=-1, keepdims=True), sink)
    p = jnp.exp(s - m)
    denom = jnp.sum(p, axis=-1, keepdims=True) + jnp.exp(sink - m)
    return jnp.einsum('...hgqk,...khd->...qhgd', p / denom, v.astype(f32))


def swa_banded(q, k, v, sinks):
    b, t = q.shape[:2]
    nb = t // WINDOW
    qb = q.reshape(b, nb, WINDOW, N_KV_HEADS, GQA_GROUP, HEAD_DIM)

    def band(x):
        xb = x.reshape(b, nb, WINDOW, N_KV_HEADS, HEAD_DIM)
        prev = jnp.pad(xb, ((0, 0), (1, 0), (0, 0), (0, 0), (0, 0)))[:, :-1]
        return jnp.concatenate([prev, xb], axis=2)

    blk = jnp.arange(nb)[:, None]
    qpos = blk * WINDOW + jnp.arange(WINDOW)[None, :]
    kpos = (blk - 1) * WINDOW + jnp.arange(2 * WINDOW)[None, :]
    d = qpos[:, :, None] - kpos[:, None, :]
    mask = (d >= 0) & (d < WINDOW) & (kpos[:, None, :] >= 0)
    mask = mask[:, None, None]
    o = sink_attention(qb, band(k), band(v), mask, sinks)
    return o.reshape(b, t, SWA_WIDTH)


def swa_with_buffer(q, k, v, past_k, past_v, sinks):
    b, t = q.shape[:2]
    n_past = past_k.shape[1]
    keys = jnp.concatenate([past_k.astype(k.dtype), k], axis=1)
    vals = jnp.concatenate([past_v.astype(v.dtype), v], axis=1)
    qpos = jnp.arange(t)
    kpos = jnp.arange(n_past + t) - n_past
    d = qpos[:, None] - kpos[None, :]
    mask = ((d >= 0) & (d < WINDOW))[None, None]
    o = sink_attention(q.reshape(b, t, N_KV_HEADS, GQA_GROUP, HEAD_DIM), keys, vals, mask, sinks)
    return o.reshape(b, t, SWA_WIDTH), keys[:, -WINDOW:], vals[:, -WINDOW:]


def decoder_layer(x, c, pos, gla_state, past_k, past_v,
                  w_ada, b_ada, norm1_w, norm2_w, w_in, w_gate_up, b_gate, gla_norm_w,
                  q_norm_w, k_norm_w, sinks, w_out, w_ff1, w_ff2):
    bsz, t, _ = x.shape
    mod = jnp.einsum('bd,de->be', jax.nn.silu(c), w_ada) + b_ada
    sh1, sc1, g1, sh2, sc2, g2 = jnp.split(mod[:, None, :], 6, axis=-1)

    hn = rms_norm(x, norm1_w) * (1 + sc1) + sh1
    proj = jnp.einsum('btd,de->bte', hn, w_in)
    gq, gk, gv, gr, glr, sq, sk, sv = jnp.split(proj, SPLIT_POINTS, axis=-1)

    gq = gq.reshape(bsz, t, GLA_HEADS, GLA_DK) * (GLA_DK ** -0.5)
    gk = gk.reshape(bsz, t, GLA_HEADS, GLA_DK)
    gv = gv.reshape(bsz, t, GLA_HEADS, GLA_DV)
    gate_logit = (jnp.einsum('btr,re->bte', glr, w_gate_up) + b_gate).astype(f32)
    log_a = (jax.nn.log_sigmoid(gate_logit) / GLA_TAU).reshape(bsz, t, GLA_HEADS, GLA_DK)
    if gla_state is None:
        gla_state = jnp.zeros((bsz, GLA_HEADS, GLA_DK, GLA_DV), f32)
    go, gla_new = gla_recurrence(gq, gk, gv, log_a, gla_state)
    go = rms_norm(go, gla_norm_w) * jax.nn.silu(gr.reshape(bsz, t, GLA_HEADS, GLA_DV).astype(f32))
    go = go.reshape(bsz, t, GLA_WIDTH).astype(x.dtype)

    sq = rope(rms_norm(sq.reshape(bsz, t, N_Q_HEADS, HEAD_DIM), q_norm_w), pos)
    sk = rope(rms_norm(sk.reshape(bsz, t, N_KV_HEADS, HEAD_DIM), k_norm_w), pos)
    sv = sv.reshape(bsz, t, N_KV_HEADS, HEAD_DIM)
    if past_k is None:
        so = swa_banded(sq, sk, sv, sinks)
        k_keep, v_keep = sk[:, -WINDOW:], sv[:, -WINDOW:]
    else:
        so, k_keep, v_keep = swa_with_buffer(sq, sk, sv, past_k, past_v, sinks)

    mixed = jnp.einsum('bte,ed->btd', jnp.concatenate([go, so.astype(x.dtype)], axis=-1), w_out)
    h = x + g1 * mixed

    hn2 = rms_norm(h, norm2_w) * (1 + sc2) + sh2
    ff = jnp.einsum('btf,fd->btd', jnp.square(jax.nn.relu(jnp.einsum('btd,df->btf', hn2, w_ff1))), w_ff2)
    y = h + g2 * ff
    return y, gla_new, k_keep, v_keep


def setup_inputs(seed: int = 0) -> dict:
    key = jax.random.key(seed)
    ks = jax.random.split(key, 24)
    nrm = jax.random.normal
    return {
        "x_prompt": nrm(ks[0], (BATCH, SEQ, D_MODEL), f32),
        "x_sample": nrm(ks[1], (DEC_BATCH, DEC_SEQ, D_MODEL), f32),
        "state_gla": nrm(ks[2], (DEPTH, DEC_BATCH, GLA_HEADS, GLA_DK, GLA_DV), f32),
        "cache_swa_k": nrm(ks[3], (DEPTH, DEC_BATCH, WINDOW, N_KV_HEADS, HEAD_DIM), f32),
        "cache_swa_v": nrm(ks[4], (DEPTH, DEC_BATCH, WINDOW, N_KV_HEADS, HEAD_DIM), f32),
        "c_prompt": nrm(ks[5], (BATCH, D_MODEL), f32),
        "c_sample": nrm(ks[6], (DEC_BATCH, D_MODEL), f32),
        "w_ada": nrm(ks[7], (DEPTH, D_MODEL, 6 * D_MODEL), f32) * (0.5 * D_MODEL ** -0.5),
        "b_ada": nrm(ks[8], (DEPTH, 6 * D_MODEL), f32) * 0.01,
        "norm1_w": 1.0 + 0.1 * nrm(ks[9], (DEPTH, D_MODEL), f32),
        "norm2_w": 1.0 + 0.1 * nrm(ks[10], (DEPTH, D_MODEL), f32),
        "w_in": nrm(ks[11], (DEPTH, D_MODEL, IN_WIDTH), f32) * (D_MODEL ** -0.5),
        "w_gate_up": nrm(ks[12], (DEPTH, GLA_GATE_RANK, GLA_HEADS * GLA_DK), f32) * (GLA_GATE_RANK ** -0.5),
        "b_gate": nrm(ks[13], (DEPTH, GLA_HEADS * GLA_DK), f32) * 0.1,
        "gla_norm_w": 1.0 + 0.1 * nrm(ks[14], (DEPTH, GLA_DV), f32),
        "q_norm_w": 1.0 + 0.1 * nrm(ks[15], (DEPTH, HEAD_DIM), f32),
        "k_norm_w": 1.0 + 0.1 * nrm(ks[16], (DEPTH, HEAD_DIM), f32),
        "sinks": nrm(ks[17], (DEPTH, N_Q_HEADS), f32) * 0.5,
        "w_out": nrm(ks[18], (DEPTH, MIX_WIDTH, D_MODEL), f32) * (MIX_WIDTH ** -0.5),
        "w_ff1": nrm(ks[19], (DEPTH, D_MODEL, D_FF), f32) * (D_MODEL ** -0.5),
        "w_ff2": nrm(ks[20], (DEPTH, D_FF, D_MODEL), f32) * (D_FF ** -0.5),
    }


def reference(x_prompt, x_sample, state_gla, cache_swa_k, cache_swa_v, c_prompt, c_sample,
              w_ada, b_ada, norm1_w, norm2_w, w_in, w_gate_up, b_gate, gla_norm_w,
              q_norm_w, k_norm_w, sinks, w_out, w_ff1, w_ff2):
    pos_prompt = jnp.arange(x_prompt.shape[1])
    pos_sample = PAST_LEN + jnp.arange(x_sample.shape[1])
    yp, ys = x_prompt, x_sample
    gp_l, kp_l, vp_l, gs_l, ks_l, vs_l = [], [], [], [], [], []
    for l in range(DEPTH):
        lw = (w_ada[l], b_ada[l], norm1_w[l], norm2_w[l], w_in[l], w_gate_up[l], b_gate[l],
              gla_norm_w[l], q_norm_w[l], k_norm_w[l], sinks[l], w_out[l], w_ff1[l], w_ff2[l])
        yp, gp, kp, vp = decoder_layer(yp, c_prompt, pos_prompt, None, None, None, *lw)
        ys, gs, kq, vq = decoder_layer(ys, c_sample, pos_sample, state_gla[l],
                                       cache_swa_k[l], cache_swa_v[l], *lw)
        gp_l.append(gp); kp_l.append(kp); vp_l.append(vp)
        gs_l.append(gs); ks_l.append(kq); vs_l.append(vq)
    return (yp, ys,
            jnp.stack(gp_l), jnp.stack(kp_l), jnp.stack(vp_l),
            jnp.stack(gs_l), jnp.stack(ks_l), jnp.stack(vs_l))
```

```python
import functools

import jax
import jax.numpy as jnp
import numpy as np
from jax import lax
from jax.experimental import pallas as pl
from jax.experimental.pallas import tpu as pltpu

f32 = jnp.float32
bf16 = jnp.bfloat16

D_MODEL = 1024
GLA_HEADS = 4
GLA_DK = 64
GLA_DV = 128
GLA_QK = GLA_HEADS * GLA_DK
GLA_WIDTH = GLA_HEADS * GLA_DV
GLA_GATE_RANK = 16
GLA_TAU = 16.0
HEAD_DIM = 64
N_Q_HEADS = 8
N_KV_HEADS = 2
SWA_Q = N_Q_HEADS * HEAD_DIM
SWA_KV = N_KV_HEADS * HEAD_DIM
WINDOW = 128
ROPE_THETA = 10000.0
PAST_LEN = 8192
D_FF = 4 * D_MODEL
EPS = 1e-6
LANES = 128
GLA_CHUNK = 128
VMEM_LIMIT = 56 * 1024 * 1024

_SEG = {}
_off = 0
for _name, _w in (("gq", GLA_QK), ("gk", GLA_QK), ("gv", GLA_WIDTH), ("gr", GLA_WIDTH),
                  ("sq", SWA_Q), ("sk", SWA_KV), ("sv", SWA_KV), ("glr", LANES)):
    _SEG[_name] = (_off, _off + _w)
    _off += _w
IN_WIDTH_PADDED = _off


def _dot(a, b):
    return jnp.dot(a, b, preferred_element_type=f32)


def _dot_nt(a, b):
    return lax.dot_general(a, b, (((1,), (1,)), ((), ())), preferred_element_type=f32)


def _dot_tn(a, b):
    return lax.dot_general(a, b, (((0,), (0,)), ((), ())), preferred_element_type=f32)


def _sigmoid(x):
    return 1.0 / (1.0 + jnp.exp(-x))


def _mod_kernel(c_ref, w_ref, b_ref, o_ref):
    c = c_ref[...]
    s = (c * _sigmoid(c)).astype(bf16)
    o_ref[...] = _dot(s, w_ref[...].astype(bf16)) + b_ref[...]


def _modulation(c_all, w_ada, b_ada):
    m = c_all.shape[0]
    n = w_ada.shape[1]
    bn = 1536
    return pl.pallas_call(
        _mod_kernel,
        grid=(n // bn,),
        in_specs=[pl.BlockSpec((m, D_MODEL), lambda j: (0, 0)),
                  pl.BlockSpec((D_MODEL, bn), lambda j: (0, j)),
                  pl.BlockSpec((1, bn), lambda j: (0, j))],
        out_specs=pl.BlockSpec((m, bn), lambda j: (0, j)),
        out_shape=jax.ShapeDtypeStruct((m, n), f32),
        compiler_params=pltpu.CompilerParams(vmem_limit_bytes=VMEM_LIMIT),
        name="adaln_mod",
    )(c_all, w_ada, b_ada.reshape(1, n))


def _group_rms(x, bd_ref, w_ref):
    ssq = _dot((x * x).astype(bf16), bd_ref[...])
    return x * lax.rsqrt(ssq * (1.0 / HEAD_DIM) + EPS) * w_ref[...]


def _rope(x, cos, sin_signed, low_half):
    partner = jnp.where(low_half, pltpu.roll(x, LANES - 32, axis=1), pltpu.roll(x, 32, axis=1))
    return x * cos + partner * sin_signed


def _inproj_kernel(x_ref, sh_ref, sc_ref, n1_ref, win_ref, wgu_ref, bg_ref, qnw_ref, knw_ref,
                   bdq_ref, bdk_ref, cos_ref, sin_ref,
                   gq_ref, gk_ref, gv_ref, gr_ref, la_ref, sq_ref, sk_ref, sv_ref):
    nb, t, d = x_ref.shape
    m = nb * t
    x = x_ref[...]
    ms = jnp.mean(x * x, axis=-1, keepdims=True)
    hn = x * lax.rsqrt(ms + EPS) * n1_ref[...]
    hn = hn * (1.0 + sc_ref[...]) + sh_ref[...]
    hb = hn.reshape(m, d).astype(bf16)

    def seg(name):
        a, b = _SEG[name]
        return _dot(hb, win_ref[:, a:b])

    gq_ref[...] = seg("gq") * (GLA_DK ** -0.5)
    gk_ref[...] = seg("gk")
    gv_ref[...] = seg("gv")
    gr_ref[...] = seg("gr")
    glr = seg("glr").astype(bf16)
    g = _dot(glr, wgu_ref[...]) + bg_ref[...]
    log_sig = jnp.minimum(g, 0.0) - jnp.log1p(jnp.exp(-jnp.abs(g)))
    la_ref[...] = log_sig * (1.0 / GLA_TAU)

    cos = cos_ref[...]
    sin = sin_ref[...]
    lane = lax.broadcasted_iota(jnp.int32, (m, LANES), 1)
    low_half = (lane & 32) == 0
    sq = _group_rms(seg("sq"), bdq_ref, qnw_ref)
    for c in range(SWA_Q // LANES):
        blk = _rope(sq[:, c * LANES:(c + 1) * LANES], cos, sin, low_half)
        sq_ref[:, c * LANES:(c + 1) * LANES] = blk * (HEAD_DIM ** -0.5)
    sk = _group_rms(seg("sk"), bdk_ref, knw_ref)
    sk_ref[...] = _rope(sk, cos, sin, low_half)
    sv_ref[...] = seg("sv")


def _inproj(x, sh, sc, n1, win, wgu, bg, qnw, knw, bdq, bdk, cos_t, sin_t, nb, tb):
    bsz, t, d = x.shape
    m = nb * tb
    grid = (bsz // nb, t // tb)
    tok = bsz * t
    nt = t // tb

    def full(shape):
        return pl.BlockSpec(shape, lambda i, j: (0,) * len(shape))

    def out(width):
        return pl.BlockSpec((m, width), lambda i, j: (i * nt + j, 0))

    widths = (GLA_QK, GLA_QK, GLA_WIDTH, GLA_WIDTH, GLA_QK, SWA_Q, SWA_KV, SWA_KV)
    return pl.pallas_call(
        _inproj_kernel,
        grid=grid,
        in_specs=[pl.BlockSpec((nb, tb, d), lambda i, j: (i, j, 0)),
                  pl.BlockSpec((nb, 1, d), lambda i, j: (i, 0, 0)),
                  pl.BlockSpec((nb, 1, d), lambda i, j: (i, 0, 0)),
                  full((1, 1, d)),
                  full((d, IN_WIDTH_PADDED)),
                  full((LANES, GLA_QK)),
                  full((1, GLA_QK)),
                  full((1, SWA_Q)),
                  full((1, SWA_KV)),
                  full((SWA_Q, SWA_Q)),
                  full((SWA_KV, SWA_KV)),
                  pl.BlockSpec((m, LANES), lambda i, j: (j, 0)),
                  pl.BlockSpec((m, LANES), lambda i, j: (j, 0))],
        out_specs=[out(w) for w in widths],
        out_shape=[jax.ShapeDtypeStruct((tok, w), f32) for w in widths],
        compiler_params=pltpu.CompilerParams(
            dimension_semantics=("parallel", "parallel"), vmem_limit_bytes=VMEM_LIMIT),
        name="inproj",
    )(x, sh, sc, n1, win, wgu, bg, qnw, knw, bdq, bdk, cos_t, sin_t)


def _gla_constants(chunk, seq):
    t = np.arange(chunk)
    u = t[None, :]
    levels = []
    m = 1
    while m < seq:
        levels.append(m)
        m *= 2
    mq, mk, masks = [], [], [np.eye(chunk)]
    for m in levels:
        start = (t - t % m)[:, None]
        end = (t - t % m + m - 1)[:, None]
        upper = (t % (2 * m) >= m)[:, None]
        lower = (t % (2 * m) < m)[:, None]
        mq.append(upper & (start <= u) & (u <= t[:, None]))
        mk.append(lower & (t[:, None] < u) & (u <= end))
        same = (t[:, None] // (2 * m)) == (t[None, :] // (2 * m))
        masks.append(same & upper & lower.T)
    s0 = (t - t % seq)[:, None]
    s1 = (t - t % seq + seq - 1)[:, None]
    mq.append((s0 <= u) & (u <= t[:, None]))
    mk.append((t[:, None] < u) & (u <= s1))
    mall = np.concatenate(mq[:-1] + mk[:-1] + [mq[-1], mk[-1]], axis=0).astype(np.float32)
    return len(levels), jnp.asarray(mall, bf16), jnp.asarray(np.stack(masks).astype(np.float32))


def _stack_heads(xb, lane_head):
    zero = jnp.zeros_like(xb)
    return jnp.concatenate([jnp.where(lane_head == h, xb, zero) for h in range(GLA_HEADS)], axis=0)


def _gla_chunk(q_ref, k_ref, v_ref, la_ref, mall_ref, masks_ref, n_levels):
    c = q_ref.shape[0]
    la = la_ref[...]
    hi = la.astype(bf16)
    lo = (la - hi.astype(f32)).astype(bf16)
    mall = mall_ref[...]
    x = jnp.exp(_dot(mall, hi) + _dot(mall, lo))
    q = q_ref[...]
    k = k_ref[...]
    vb = v_ref[...].astype(bf16)
    lane_head = lax.broadcasted_iota(jnp.int32, (c, GLA_QK), 1) >> 6
    attn = [None] * GLA_HEADS
    for lvl in range(n_levels + 1):
        if lvl == 0:
            qt, kt = q, k
        else:
            qt = q * x[(lvl - 1) * c:lvl * c]
            kt = k * x[(n_levels + lvl - 1) * c:(n_levels + lvl) * c]
        r = _dot_nt(_stack_heads(qt.astype(bf16), lane_head), kt.astype(bf16))
        mk = masks_ref[lvl]
        for h in range(GLA_HEADS):
            rm = r[h * c:(h + 1) * c] * mk
            attn[h] = rm if attn[h] is None else attn[h] + rm
    a = jnp.concatenate(attn, axis=0).astype(bf16)
    av = _dot(a, vb)
    o_intra = [av[h * c:(h + 1) * c, h * GLA_DV:(h + 1) * GLA_DV] for h in range(GLA_HEADS)]
    xq = x[2 * n_levels * c:(2 * n_levels + 1) * c]
    xk = x[(2 * n_levels + 1) * c:(2 * n_levels + 2) * c]
    qf = _stack_heads((q * xq).astype(bf16), lane_head)
    kf = (k * xk).astype(bf16)
    return o_intra, qf, kf, vb, hi, lo


def _state_update(s_old, kf, vb, hi, lo):
    p = _dot_tn(kf, vb)
    row_head = lax.broadcasted_iota(jnp.int32, (GLA_QK, GLA_DV), 0) >> 6
    upd = jnp.zeros((GLA_QK, GLA_DV), f32)
    for h in range(GLA_HEADS):
        upd = upd + jnp.where(row_head == h, p[:, h * GLA_DV:(h + 1) * GLA_DV], 0.0)
    ones = jnp.ones((hi.shape[0], GLA_DV), bf16)
    total = _dot_tn(hi, ones) + _dot_tn(lo, ones)
    return jnp.exp(total) * s_old + upd


def _gla_finish(o, r, gnw):
    ms = jnp.mean(o * o, axis=-1, keepdims=True)
    return o * lax.rsqrt(ms + EPS) * gnw * (r * _sigmoid(r))


def _gla_prompt_kernel(q_ref, k_ref, v_ref, la_ref, r_ref, mall_ref, masks_ref, gnw_ref,
                       go_ref, sout_ref, state_ref, *, n_levels):
    j = pl.program_id(1)
    c = q_ref.shape[0]

    @pl.when(j == 0)
    def _():
        state_ref[...] = jnp.zeros_like(state_ref)

    o_intra, qf, kf, vb, hi, lo = _gla_chunk(q_ref, k_ref, v_ref, la_ref, mall_ref, masks_ref, n_levels)
    s_old = state_ref[...]
    o_inter = _dot(qf, s_old.astype(bf16))
    gnw = gnw_ref[...]
    for h in range(GLA_HEADS):
        o = o_intra[h] + o_inter[h * c:(h + 1) * c]
        sl = slice(h * GLA_DV, (h + 1) * GLA_DV)
        go_ref[:, sl] = _gla_finish(o, r_ref[:, sl], gnw)
    state_ref[...] = _state_update(s_old, kf, vb, hi, lo)

    @pl.when(j == pl.num_programs(1) - 1)
    def _():
        sout_ref[0] = state_ref[...]


def _gla_prompt(gq, gk, gv, la, gr, gnw, bsz, t):
    c = GLA_CHUNK
    n_levels, mall, masks = _gla_constants(c, c)
    nt = t // c

    def tok(width):
        return pl.BlockSpec((c, width), lambda b, j: (b * nt + j, 0))

    return pl.pallas_call(
        functools.partial(_gla_prompt_kernel, n_levels=n_levels),
        grid=(bsz, nt),
        in_specs=[tok(GLA_QK), tok(GLA_QK), tok(GLA_WIDTH), tok(GLA_QK), tok(GLA_WIDTH),
                  pl.BlockSpec(mall.shape, lambda b, j: (0, 0)),
                  pl.BlockSpec(masks.shape, lambda b, j: (0, 0, 0)),
                  pl.BlockSpec((1, GLA_DV), lambda b, j: (0, 0))],
        out_specs=[tok(GLA_WIDTH),
                   pl.BlockSpec((1, GLA_QK, GLA_DV), lambda b, j: (b, 0, 0))],
        out_shape=[jax.ShapeDtypeStruct((bsz * t, GLA_WIDTH), f32),
                   jax.ShapeDtypeStruct((bsz, GLA_QK, GLA_DV), f32)],
        scratch_shapes=[pltpu.VMEM((GLA_QK, GLA_DV), f32)],
        compiler_params=pltpu.CompilerParams(
            dimension_semantics=("parallel", "arbitrary"), vmem_limit_bytes=VMEM_LIMIT),
        name="gla_prompt",
    )(gq, gk, gv, la, gr, mall, masks, gnw)


def _gla_sample_kernel(q_ref, k_ref, v_ref, la_ref, r_ref, s0_ref, mall_ref, masks_ref, gnw_ref,
                       go_ref, sout_ref, *, n_levels, seq):
    c = q_ref.shape[0]
    o_intra, qf, kf, vb, hi, lo = _gla_chunk(q_ref, k_ref, v_ref, la_ref, mall_ref, masks_ref, n_levels)
    gnw = gnw_ref[...]
    inter = [[] for _ in range(GLA_HEADS)]
    for b in range(c // seq):
        rows = slice(b * seq, (b + 1) * seq)
        s_old = s0_ref[b]
        qb = jnp.concatenate([qf[h * c + b * seq:h * c + (b + 1) * seq] for h in range(GLA_HEADS)], axis=0)
        oi = _dot(qb, s_old.astype(bf16))
        for h in range(GLA_HEADS):
            inter[h].append(oi[h * seq:(h + 1) * seq])
        sout_ref[b] = _state_update(s_old, kf[rows], vb[rows], hi[rows], lo[rows])
    for h in range(GLA_HEADS):
        o = o_intra[h] + jnp.concatenate(inter[h], axis=0)
        sl = slice(h * GLA_DV, (h + 1) * GLA_DV)
        go_ref[:, sl] = _gla_finish(o, r_ref[:, sl], gnw)


def _gla_sample(gq, gk, gv, la, gr, state, gnw, bsz, t):
    c = GLA_CHUNK
    nseq = c // t
    n_levels, mall, masks = _gla_constants(c, t)

    def tok(width):
        return pl.BlockSpec((c, width), lambda i: (i, 0))

    return pl.pallas_call(
        functools.partial(_gla_sample_kernel, n_levels=n_levels, seq=t),
        grid=(bsz // nseq,),
        in_specs=[tok(GLA_QK), tok(GLA_QK), tok(GLA_WIDTH), tok(GLA_QK), tok(GLA_WIDTH),
                  pl.BlockSpec((nseq, GLA_QK, GLA_DV), lambda i: (i, 0, 0)),
                  pl.BlockSpec(mall.shape, lambda i: (0, 0)),
                  pl.BlockSpec(masks.shape, lambda i: (0, 0, 0)),
                  pl.BlockSpec((1, GLA_DV), lambda i: (0, 0))],
        out_specs=[tok(GLA_WIDTH),
                   pl.BlockSpec((nseq, GLA_QK, GLA_DV), lambda i: (i, 0, 0))],
        out_shape=[jax.ShapeDtypeStruct((bsz * t, GLA_WIDTH), f32),
                   jax.ShapeDtypeStruct((bsz, GLA_QK, GLA_DV), f32)],
        compiler_params=pltpu.CompilerParams(
            dimension_semantics=("parallel",), vmem_limit_bytes=VMEM_LIMIT),
        name="gla_sample",
    )(gq, gk, gv, la, gr, state, mall, masks, gnw)


def _dup_heads(x, low_lanes):
    rolled = pltpu.roll(x, HEAD_DIM, axis=1)
    return jnp.where(low_lanes, x, rolled), jnp.where(low_lanes, rolled, x)


def _sink_attention(q_ref, kcat, vcat, mask, sink_ref, o_ref, rows):
    nk = kcat.shape[0]
    low_q = (lax.broadcasted_iota(jnp.int32, (rows, LANES), 1) < HEAD_DIM)
    low_k = (lax.broadcasted_iota(jnp.int32, (nk, LANES), 1) < HEAD_DIM)
    k2 = _dup_heads(kcat, low_k)
    v2 = _dup_heads(vcat, low_k)
    pairs_per_group = (N_Q_HEADS // N_KV_HEADS) // 2
    for g in range(N_KV_HEADS):
        kb = k2[g].astype(bf16)
        vb = v2[g].astype(bf16)
        stack = []
        for p in range(pairs_per_group):
            pair = g * pairs_per_group + p
            qp = q_ref[:, pair * LANES:(pair + 1) * LANES].astype(bf16)
            zero = jnp.zeros_like(qp)
            stack.append(jnp.where(low_q, qp, zero))
            stack.append(jnp.where(low_q, zero, qp))
        s_all = _dot_nt(jnp.concatenate(stack, axis=0), kb)
        outs = []
        for hh in range(2 * pairs_per_group):
            head = g * 2 * pairs_per_group + hh
            sink = sink_ref[head]
            s = jnp.where(mask, s_all[hh * rows:(hh + 1) * rows], -1e30)
            mx = jnp.maximum(jnp.max(s, axis=-1, keepdims=True), sink)
            p = jnp.exp(s - mx)
            denom = jnp.sum(p, axis=-1, keepdims=True) + jnp.exp(sink - mx)
            outs.append(_dot(p.astype(bf16), vb) / denom)
        for p in range(pairs_per_group):
            pair = g * pairs_per_group + p
            o_ref[:, pair * LANES:(pair + 1) * LANES] = jnp.where(low_q, outs[2 * p], outs[2 * p + 1])


def _swa_prompt_kernel(sink_ref, q_ref, kp_ref, kc_ref, vp_ref, vc_ref, o_ref):
    i = pl.program_id(1)
    w = q_ref.shape[0]
    kcat = jnp.concatenate([kp_ref[...], kc_ref[...]], axis=0)
    vcat = jnp.concatenate([vp_ref[...], vc_ref[...]], axis=0)
    tq = lax.broadcasted_iota(jnp.int32, (w, 2 * w), 0)
    tk = lax.broadcasted_iota(jnp.int32, (w, 2 * w), 1)
    first_key = jnp.where(i > 0, 0, w)
    rel = tk - tq
    mask = (rel > 0) & (rel <= w) & (tk >= first_key)
    _sink_attention(q_ref, kcat, vcat, mask, sink_ref, o_ref, w)


def _swa_prompt(sq, sk, sv, sinks, bsz, t):
    w = WINDOW
    nb = t // w
    cur = lambda b, i, s: (b * nb + i, 0)
    prev = lambda b, i, s: (b * nb + jnp.maximum(i - 1, 0), 0)
    return pl.pallas_call(
        _swa_prompt_kernel,
        grid_spec=pltpu.PrefetchScalarGridSpec(
            num_scalar_prefetch=1,
            grid=(bsz, nb),
            in_specs=[pl.BlockSpec((w, SWA_Q), cur),
                      pl.BlockSpec((w, SWA_KV), prev), pl.BlockSpec((w, SWA_KV), cur),
                      pl.BlockSpec((w, SWA_KV), prev), pl.BlockSpec((w, SWA_KV), cur)],
            out_specs=pl.BlockSpec((w, SWA_Q), cur)),
        out_shape=jax.ShapeDtypeStruct((bsz * t, SWA_Q), f32),
        compiler_params=pltpu.CompilerParams(
            dimension_semantics=("parallel", "parallel"), vmem_limit_bytes=VMEM_LIMIT),
        name="swa_prompt",
    )(sinks, sq, sk, sk, sv, sv)


def _swa_sample_kernel(sink_ref, q_ref, kn_ref, vn_ref, pk_ref, pv_ref, o_ref, ko_ref, vo_ref, *, seq):
    nseq = pk_ref.shape[0]
    w = pk_ref.shape[1]
    nk = w + seq
    rows = lax.broadcasted_iota(jnp.int32, (seq, nk), 0)
    cols = lax.broadcasted_iota(jnp.int32, (seq, nk), 1)
    mask = (cols > rows) & (cols <= rows + w)
    for b in range(nseq):
        tok = pl.ds(b * seq, seq)
        kcat = jnp.concatenate([pk_ref[b], kn_ref[tok, :]], axis=0)
        vcat = jnp.concatenate([pv_ref[b], vn_ref[tok, :]], axis=0)
        ko_ref[b] = kcat[seq:]
        vo_ref[b] = vcat[seq:]
        _sink_attention(q_ref.at[tok, :], kcat, vcat, mask, sink_ref, o_ref.at[tok, :], seq)


def _swa_sample(sq, sk, sv, past_k, past_v, sinks, bsz, t):
    nseq = 16
    w = past_k.shape[1]
    tok = lambda width: pl.BlockSpec((nseq * t, width), lambda i, s: (i, 0))
    cache = pl.BlockSpec((nseq, w, SWA_KV), lambda i, s: (i, 0, 0))
    return pl.pallas_call(
        functools.partial(_swa_sample_kernel, seq=t),
        grid_spec=pltpu.PrefetchScalarGridSpec(
            num_scalar_prefetch=1,
            grid=(bsz // nseq,),
            in_specs=[tok(SWA_Q), tok(SWA_KV), tok(SWA_KV), cache, cache],
            out_specs=[tok(SWA_Q), cache, cache]),
        out_shape=[jax.ShapeDtypeStruct((bsz * t, SWA_Q), f32),
                   jax.ShapeDtypeStruct((bsz, w, SWA_KV), f32),
                   jax.ShapeDtypeStruct((bsz, w, SWA_KV), f32)],
        compiler_params=pltpu.CompilerParams(
            dimension_semantics=("parallel",), vmem_limit_bytes=VMEM_LIMIT),
        name="swa_sample",
    )(sinks, sq, sk, sv, past_k, past_v)


def _out_ffn_kernel(x_ref, go_ref, so_ref, g1_ref, sh2_ref, sc2_ref, g2_ref, n2_ref,
                    wout_ref, w1_ref, w2_ref, y_ref):
    nb, t, d = x_ref.shape
    m = nb * t
    mixed = (_dot(go_ref[...].astype(bf16), wout_ref[:GLA_WIDTH, :])
             + _dot(so_ref[...].astype(bf16), wout_ref[GLA_WIDTH:, :]))
    h = x_ref[...] + g1_ref[...] * mixed.reshape(nb, t, d)
    ms = jnp.mean(h * h, axis=-1, keepdims=True)
    hn = h * lax.rsqrt(ms + EPS) * n2_ref[...]
    hn = hn * (1.0 + sc2_ref[...]) + sh2_ref[...]
    hb = hn.reshape(m, d).astype(bf16)
    fc = 1024
    ff = None
    for c in range(D_FF // fc):
        a = jnp.maximum(_dot(hb, w1_ref[:, c * fc:(c + 1) * fc]), 0.0)
        part = _dot((a * a).astype(bf16), w2_ref[c * fc:(c + 1) * fc, :])
        ff = part if ff is None else ff + part
    y_ref[...] = h + g2_ref[...] * ff.reshape(nb, t, d)


def _out_ffn(x, go, so, g1, sh2, sc2, g2, n2, wout, w1, w2, nb, tb):
    bsz, t, d = x.shape
    m = nb * tb
    nt = t // tb

    def full(shape):
        return pl.BlockSpec(shape, lambda i, j: (0,) * len(shape), pipeline_mode=pl.Buffered(1))

    modspec = pl.BlockSpec((nb, 1, d), lambda i, j: (i, 0, 0))
    xspec = pl.BlockSpec((nb, tb, d), lambda i, j: (i, j, 0))
    return pl.pallas_call(
        _out_ffn_kernel,
        grid=(bsz // nb, nt),
        in_specs=[xspec,
                  pl.BlockSpec((m, GLA_WIDTH), lambda i, j: (i * nt + j, 0)),
                  pl.BlockSpec((m, SWA_Q), lambda i, j: (i * nt + j, 0)),
                  modspec, modspec, modspec, modspec,
                  pl.BlockSpec((1, 1, d), lambda i, j: (0, 0, 0)),
                  full((d, d)), full((d, D_FF)), full((D_FF, d))],
        out_specs=xspec,
        out_shape=jax.ShapeDtypeStruct((bsz, t, d), f32),
        compiler_params=pltpu.CompilerParams(
            dimension_semantics=("parallel", "parallel"), vmem_limit_bytes=VMEM_LIMIT),
        name="out_ffn",
    )(x, go, so, g1, sh2, sc2, g2, n2, wout, w1, w2)


def _rope_tables(pos):
    half = HEAD_DIM // 2
    inv = jnp.power(ROPE_THETA, -jnp.arange(half, dtype=f32) * 2.0 / HEAD_DIM)
    ang = pos.astype(f32)[:, None] * inv[None, :]
    cos = jnp.cos(ang)
    sin = jnp.sin(ang)
    reps = LANES // HEAD_DIM
    return (jnp.tile(jnp.concatenate([cos, cos], axis=-1), (1, reps)),
            jnp.tile(jnp.concatenate([-sin, sin], axis=-1), (1, reps)))


def _block_diag_ones(n, blk):
    idx = np.arange(n) // blk
    return jnp.asarray((idx[:, None] == idx[None, :]).astype(np.float32), bf16)


def _layer_weights(w_in, w_gate_up, b_gate, q_norm_w, k_norm_w, w_out, w_ff1, w_ff2):
    splits = np.cumsum([GLA_QK, GLA_QK, GLA_WIDTH, GLA_WIDTH, GLA_GATE_RANK, SWA_Q, SWA_KV])
    gq, gk, gv, gr, glr, sq, sk, sv = jnp.split(w_in, [int(s) for s in splits], axis=1)
    glr = jnp.pad(glr, ((0, 0), (0, LANES - GLA_GATE_RANK)))
    win = jnp.concatenate([gq, gk, gv, gr, sq, sk, sv, glr], axis=1).astype(bf16)
    wgu = jnp.pad(w_gate_up, ((0, LANES - GLA_GATE_RANK), (0, 0))).astype(bf16)
    return dict(
        win=win, wgu=wgu, bg=b_gate.reshape(1, GLA_QK),
        qnw=jnp.tile(q_norm_w, N_Q_HEADS).reshape(1, SWA_Q),
        knw=jnp.tile(k_norm_w, N_KV_HEADS).reshape(1, SWA_KV),
        wout=w_out.astype(bf16), w1=w_ff1.astype(bf16), w2=w_ff2.astype(bf16))


def _decoder_layer(x, mod, pos, state, past_k, past_v, lw, n1, n2, gnw, sinks, bdq, bdk, nb, tb):
    bsz, t, d = x.shape
    sh1, sc1, g1, sh2, sc2, g2 = [mod[:, None, i * d:(i + 1) * d] for i in range(6)]
    cos_t, sin_t = _rope_tables(pos)
    if nb > 1:
        cos_t = jnp.tile(cos_t, (nb, 1))
        sin_t = jnp.tile(sin_t, (nb, 1))
    gq, gk, gv, gr, la, sq, sk, sv = _inproj(
        x, sh1, sc1, n1.reshape(1, 1, d), lw["win"], lw["wgu"], lw["bg"], lw["qnw"], lw["knw"],
        bdq, bdk, cos_t, sin_t, nb, tb)
    if state is None:
        go, s_new = _gla_prompt(gq, gk, gv, la, gr, gnw, bsz, t)
        so = _swa_prompt(sq, sk, sv, sinks, bsz, t)
        k_keep = sk.reshape(bsz, t, SWA_KV)[:, -WINDOW:]
        v_keep = sv.reshape(bsz, t, SWA_KV)[:, -WINDOW:]
    else:
        go, s_new = _gla_sample(gq, gk, gv, la, gr, state.reshape(bsz, GLA_QK, GLA_DV), gnw, bsz, t)
        so, k_keep, v_keep = _swa_sample(
            sq, sk, sv, past_k.reshape(bsz, WINDOW, SWA_KV), past_v.reshape(bsz, WINDOW, SWA_KV),
            sinks, bsz, t)
    y = _out_ffn(x, go, so, g1, sh2, sc2, g2, n2.reshape(1, 1, d), lw["wout"], lw["w1"], lw["w2"], nb, tb)
    return (y, s_new.reshape(bsz, GLA_HEADS, GLA_DK, GLA_DV),
            k_keep.reshape(bsz, WINDOW, N_KV_HEADS, HEAD_DIM),
            v_keep.reshape(bsz, WINDOW, N_KV_HEADS, HEAD_DIM))


def kernel(x_prompt, x_sample, state_gla, cache_swa_k, cache_swa_v, c_prompt, c_sample, w_ada, b_ada, norm1_w, norm2_w, w_in, w_gate_up, b_gate, gla_norm_w, q_norm_w, k_norm_w, sinks, w_out, w_ff1, w_ff2):
    depth = w_ada.shape[0]
    bp, tp, _ = x_prompt.shape
    bs, ts, _ = x_sample.shape
    pos_p = jnp.arange(tp)
    pos_s = PAST_LEN + jnp.arange(ts)
    bdq = _block_diag_ones(SWA_Q, HEAD_DIM)
    bdk = _block_diag_ones(SWA_KV, HEAD_DIM)
    c_all = jnp.concatenate([c_prompt, c_sample], axis=0)
    yp, ys = x_prompt, x_sample
    outs = [[] for _ in range(6)]
    for l in range(depth):
        mod = _modulation(c_all, w_ada[l], b_ada[l])
        lw = _layer_weights(w_in[l], w_gate_up[l], b_gate[l], q_norm_w[l], k_norm_w[l],
                            w_out[l], w_ff1[l], w_ff2[l])
        gnw = gla_norm_w[l].reshape(1, GLA_DV)
        common = (lw, norm1_w[l], norm2_w[l], gnw, sinks[l], bdq, bdk)
        yp, gp, kp, vp = _decoder_layer(yp, mod[:bp], pos_p, None, None, None, *common, nb=1, tb=512)
        ys, gs, kq, vq = _decoder_layer(ys, mod[bp:], pos_s, state_gla[l], cache_swa_k[l],
                                        cache_swa_v[l], *common, nb=256 // ts, tb=ts)
        for lst, val in zip(outs, (gp, kp, vp, gs, kq, vq)):
            lst.append(val)
    return (yp, ys) + tuple(jnp.stack(o) for o in outs)
```

```python
import functools

import jax
import jax.numpy as jnp
import numpy as np
from jax import lax
from jax.experimental import pallas as pl
from jax.experimental.pallas import tpu as pltpu

f32 = jnp.float32
bf16 = jnp.bfloat16

D_MODEL = 1024
GLA_HEADS = 4
GLA_DK = 64
GLA_DV = 128
GLA_QK = GLA_HEADS * GLA_DK
GLA_WIDTH = GLA_HEADS * GLA_DV
GLA_GATE_RANK = 16
GLA_TAU = 16.0
LOG2E = 1.4426950408889634
HEAD_DIM = 64
N_Q_HEADS = 8
N_KV_HEADS = 2
SWA_Q = N_Q_HEADS * HEAD_DIM
SWA_KV = N_KV_HEADS * HEAD_DIM
WINDOW = 128
ROPE_THETA = 10000.0
PAST_LEN = 8192
D_FF = 4 * D_MODEL
EPS = 1e-6
LANES = 128
GLA_CHUNK = 128
VMEM_LIMIT = 56 * 1024 * 1024

_SEG = {}
_off = 0
for _name, _w in (("gq", GLA_QK), ("gk", GLA_QK), ("gv", GLA_WIDTH), ("gr", GLA_WIDTH),
                  ("sq", SWA_Q), ("sk", SWA_KV), ("sv", SWA_KV), ("glr", LANES)):
    _SEG[_name] = (_off, _off + _w)
    _off += _w
IN_WIDTH_PADDED = _off


def _dot(a, b):
    return jnp.dot(a, b, preferred_element_type=f32)


def _dot_nt(a, b):
    return lax.dot_general(a, b, (((1,), (1,)), ((), ())), preferred_element_type=f32)


def _dot_tn(a, b):
    return lax.dot_general(a, b, (((0,), (0,)), ((), ())), preferred_element_type=f32)


def _sigmoid(x):
    return 1.0 / (1.0 + jnp.exp(-x))


def _mod_kernel(c_ref, w_ref, b_ref, o_ref):
    c = c_ref[...]
    s = (c * _sigmoid(c)).astype(bf16)
    o_ref[...] = _dot(s, w_ref[...].astype(bf16)) + b_ref[...]


def _modulation(c_all, w_ada, b_ada):
    m = c_all.shape[0]
    n = w_ada.shape[1]
    bn = 1536
    return pl.pallas_call(
        _mod_kernel,
        grid=(n // bn,),
        in_specs=[pl.BlockSpec((m, D_MODEL), lambda j: (0, 0)),
                  pl.BlockSpec((D_MODEL, bn), lambda j: (0, j)),
                  pl.BlockSpec((1, bn), lambda j: (0, j))],
        out_specs=pl.BlockSpec((m, bn), lambda j: (0, j)),
        out_shape=jax.ShapeDtypeStruct((m, n), f32),
        compiler_params=pltpu.CompilerParams(vmem_limit_bytes=VMEM_LIMIT),
        name="adaln_mod",
    )(c_all, w_ada, b_ada.reshape(1, n))


def _group_rms(x, bd_ref, w_ref):
    ssq = _dot((x * x).astype(bf16), bd_ref[...])
    return x * lax.rsqrt(ssq * (1.0 / HEAD_DIM) + EPS) * w_ref[...]


def _rope(x, cos, sin_signed, low_half):
    partner = jnp.where(low_half, pltpu.roll(x, LANES - 32, axis=1), pltpu.roll(x, 32, axis=1))
    return x * cos + partner * sin_signed


def _inproj_kernel(x_ref, sh_ref, sc_ref, n1_ref, win_ref, wgu_ref, bg_ref, qnw_ref, knw_ref,
                   bdq_ref, bdk_ref, cos_ref, sin_ref,
                   gq_ref, gk_ref, gv_ref, gr_ref, la_ref, sq_ref, sk_ref, sv_ref):
    nb, t, d = x_ref.shape
    m = nb * t
    x = x_ref[...]
    ms = jnp.mean(x * x, axis=-1, keepdims=True)
    hn = x * lax.rsqrt(ms + EPS) * n1_ref[...]
    hn = hn * (1.0 + sc_ref[...]) + sh_ref[...]
    hb = hn.reshape(m, d).astype(bf16)

    def seg(name):
        a, b = _SEG[name]
        return _dot(hb, win_ref[:, a:b])

    gq_ref[...] = seg("gq") * (GLA_DK ** -0.5)
    gk_ref[...] = seg("gk")
    gv_ref[...] = seg("gv")
    gr_ref[...] = seg("gr")
    glr = seg("glr").astype(bf16)
    g = _dot(glr, wgu_ref[...]) + bg_ref[...]
    log_sig = jnp.minimum(g, 0.0) - jnp.log1p(jnp.exp(-jnp.abs(g)))
    la_ref[...] = log_sig * (LOG2E / GLA_TAU)

    cos = cos_ref[...]
    sin = sin_ref[...]
    lane = lax.broadcasted_iota(jnp.int32, (m, LANES), 1)
    low_half = (lane & 32) == 0
    sq = _group_rms(seg("sq"), bdq_ref, qnw_ref)
    for c in range(SWA_Q // LANES):
        blk = _rope(sq[:, c * LANES:(c + 1) * LANES], cos, sin, low_half)
        sq_ref[:, c * LANES:(c + 1) * LANES] = blk * (HEAD_DIM ** -0.5)
    sk = _group_rms(seg("sk"), bdk_ref, knw_ref)
    sk_ref[...] = _rope(sk, cos, sin, low_half)
    sv_ref[...] = seg("sv")


def _inproj(x, sh, sc, n1, win, wgu, bg, qnw, knw, bdq, bdk, cos_t, sin_t, nb, tb):
    bsz, t, d = x.shape
    m = nb * tb
    grid = (bsz // nb, t // tb)
    tok = bsz * t
    nt = t // tb

    def full(shape):
        return pl.BlockSpec(shape, lambda i, j: (0,) * len(shape))

    def out(width):
        return pl.BlockSpec((m, width), lambda i, j: (i * nt + j, 0))

    widths = (GLA_QK, GLA_QK, GLA_WIDTH, GLA_WIDTH, GLA_QK, SWA_Q, SWA_KV, SWA_KV)
    return pl.pallas_call(
        _inproj_kernel,
        grid=grid,
        in_specs=[pl.BlockSpec((nb, tb, d), lambda i, j: (i, j, 0)),
                  pl.BlockSpec((nb, 1, d), lambda i, j: (i, 0, 0)),
                  pl.BlockSpec((nb, 1, d), lambda i, j: (i, 0, 0)),
                  full((1, 1, d)),
                  full((d, IN_WIDTH_PADDED)),
                  full((LANES, GLA_QK)),
                  full((1, GLA_QK)),
                  full((1, SWA_Q)),
                  full((1, SWA_KV)),
                  full((SWA_Q, SWA_Q)),
                  full((SWA_KV, SWA_KV)),
                  pl.BlockSpec((m, LANES), lambda i, j: (j, 0)),
                  pl.BlockSpec((m, LANES), lambda i, j: (j, 0))],
        out_specs=[out(w) for w in widths],
        out_shape=[jax.ShapeDtypeStruct((tok, w), f32) for w in widths],
        compiler_params=pltpu.CompilerParams(
            dimension_semantics=("parallel", "parallel"), vmem_limit_bytes=VMEM_LIMIT),
        name="inproj",
    )(x, sh, sc, n1, win, wgu, bg, qnw, knw, bdq, bdk, cos_t, sin_t)


def _gla_constants(chunk, seq):
    t = np.arange(chunk)
    levels = []
    m = 1
    while m < seq:
        levels.append(m)
        m *= 2
    masks = [np.eye(chunk, dtype=bool)]
    for m in levels:
        upper = (t % (2 * m) >= m)[:, None]
        lower = (t % (2 * m) < m)[None, :]
        same = (t[:, None] // (2 * m)) == (t[None, :] // (2 * m))
        masks.append(same & upper & lower)
    tri = t[None, :] <= t[:, None]
    return (tuple(levels), jnp.asarray(tri.astype(np.float32), bf16),
            jnp.asarray(np.stack(masks).astype(np.float32)))


def _stack_heads(xb, lane_head, axis):
    zero = jnp.zeros_like(xb)
    return jnp.concatenate([jnp.where(lane_head == h, xb, zero) for h in range(GLA_HEADS)], axis=axis)


def _block_sums(la, bcum, m, row, rolls):
    c, n = la.shape
    if m == 1:
        return la, None
    if m < 8:
        def rolled(shift):
            if shift not in rolls:
                rolls[shift] = pltpu.roll(la, shift % c, axis=0)
            return rolls[shift]
        pos = row & (m - 1)
        pre = la
        suf = None
        for j in range(1, m):
            pre = pre + jnp.where(pos >= j, rolled(j), 0.0)
            term = jnp.where(pos < m - j, rolled(-j), 0.0)
            suf = term if suf is None else suf + term
        return pre, suf
    before, last = [], []
    for i in range(c // m):
        before.append(jnp.zeros((m, n), f32) if i == 0
                      else jnp.broadcast_to(bcum[i * m - 1:i * m], (m, n)))
        last.append(jnp.broadcast_to(bcum[(i + 1) * m - 1:(i + 1) * m], (m, n)))
    if len(before) == 1:
        return bcum - before[0], last[0] - bcum
    return bcum - jnp.concatenate(before, axis=0), jnp.concatenate(last, axis=0) - bcum


def _gla_chunk(q_ref, k_ref, v_ref, la_ref, tri_ref, masks_ref, levels, seq):
    c = q_ref.shape[0]
    la = la_ref[...]
    hi = la.astype(bf16)
    lo = (la - hi.astype(f32)).astype(bf16)
    tri = tri_ref[...]
    bcum = _dot(tri, hi) + _dot(tri, lo)
    q = q_ref[...]
    k = k_ref[...]
    vb = v_ref[...].astype(bf16)
    row = lax.broadcasted_iota(jnp.int32, (c, GLA_QK), 0)
    lane_head = lax.broadcasted_iota(jnp.int32, (c, GLA_QK), 1) >> 6
    rolls = {}
    attn = None
    for lvl, m in enumerate((0,) + levels):
        if m == 0:
            qt, kt = q, k
        else:
            pre, suf = _block_sums(la, bcum, m, row, rolls)
            qt = q * jnp.exp2(pre)
            kt = k if suf is None else k * jnp.exp2(suf)
        r = _dot_nt(qt.astype(bf16), _stack_heads(kt.astype(bf16), lane_head, 0))
        mk = masks_ref[lvl]
        parts = [r[:, h * c:(h + 1) * c] * mk for h in range(GLA_HEADS)]
        attn = parts if attn is None else [a + p for a, p in zip(attn, parts)]
    a = jnp.concatenate(attn, axis=1).astype(bf16)
    v_head = lax.broadcasted_iota(jnp.int32, (c, GLA_WIDTH), 1) >> 7
    vbd = _stack_heads(vb, v_head, 0)
    o_intra = _dot(a, vbd)
    pre, suf = _block_sums(la, bcum, seq, row, rolls)
    qf = (q * jnp.exp2(pre)).astype(bf16)
    kf = (k * jnp.exp2(suf)).astype(bf16)
    return o_intra, qf, kf, vb, bcum, hi, lo


def _gla_finish(o, r_ref, gnw, go_ref):
    for h in range(GLA_HEADS):
        sl = slice(h * GLA_DV, (h + 1) * GLA_DV)
        oh = o[:, sl]
        r = r_ref[:, sl]
        ms = jnp.mean(oh * oh, axis=-1, keepdims=True)
        go_ref[:, sl] = oh * lax.rsqrt(ms + EPS) * gnw * (r * _sigmoid(r))


def _gla_prompt_kernel(q_ref, k_ref, v_ref, la_ref, r_ref, tri_ref, masks_ref, gnw_ref,
                       go_ref, sout_ref, state_ref, *, levels):
    j = pl.program_id(1)
    c = q_ref.shape[0]

    @pl.when(j == 0)
    def _():
        state_ref[...] = jnp.zeros_like(state_ref)

    o_intra, qf, kf, vb, bcum, _, _ = _gla_chunk(q_ref, k_ref, v_ref, la_ref, tri_ref, masks_ref, levels, c)
    st = state_ref[...]
    lane_head = lax.broadcasted_iota(jnp.int32, (GLA_DV, GLA_QK), 1) >> 6
    sbd_t = _stack_heads(st.astype(bf16), lane_head, 0)
    o = o_intra + _dot_nt(qf, sbd_t)
    _gla_finish(o, r_ref, gnw_ref[...], go_ref)
    full = _dot_tn(vb, kf)
    upd = None
    for h in range(GLA_HEADS):
        term = jnp.where(lane_head == h, full[h * GLA_DV:(h + 1) * GLA_DV], 0.0)
        upd = term if upd is None else upd + term
    state_ref[...] = jnp.exp2(bcum[c - 1:c]) * st + upd

    @pl.when(j == pl.num_programs(1) - 1)
    def _():
        sout_ref[0] = state_ref[...]


def _gla_prompt(gq, gk, gv, la, gr, gnw, bsz, t):
    c = GLA_CHUNK
    levels, tri, masks = _gla_constants(c, c)
    nt = t // c

    def tok(width):
        return pl.BlockSpec((c, width), lambda b, j: (b * nt + j, 0))

    return pl.pallas_call(
        functools.partial(_gla_prompt_kernel, levels=levels),
        grid=(bsz, nt),
        in_specs=[tok(GLA_QK), tok(GLA_QK), tok(GLA_WIDTH), tok(GLA_QK), tok(GLA_WIDTH),
                  pl.BlockSpec(tri.shape, lambda b, j: (0, 0)),
                  pl.BlockSpec(masks.shape, lambda b, j: (0, 0, 0)),
                  pl.BlockSpec((1, GLA_DV), lambda b, j: (0, 0))],
        out_specs=[tok(GLA_WIDTH),
                   pl.BlockSpec((1, GLA_DV, GLA_QK), lambda b, j: (b, 0, 0))],
        out_shape=[jax.ShapeDtypeStruct((bsz * t, GLA_WIDTH), f32),
                   jax.ShapeDtypeStruct((bsz, GLA_DV, GLA_QK), f32)],
        scratch_shapes=[pltpu.VMEM((GLA_DV, GLA_QK), f32)],
        compiler_params=pltpu.CompilerParams(
            dimension_semantics=("parallel", "arbitrary"), vmem_limit_bytes=VMEM_LIMIT),
        name="gla_prompt",
    )(gq, gk, gv, la, gr, tri, masks, gnw)


def _gla_sample_kernel(q_ref, k_ref, v_ref, la_ref, r_ref, s0_ref, tri_ref, masks_ref, gnw_ref,
                       go_ref, sout_ref, *, levels, seq):
    c = q_ref.shape[0]
    o_intra, qf, kf, vb, _, hi, lo = _gla_chunk(q_ref, k_ref, v_ref, la_ref, tri_ref, masks_ref, levels, seq)
    row_head = lax.broadcasted_iota(jnp.int32, (GLA_QK, GLA_DV), 0) >> 6
    ones = jnp.ones((seq, GLA_DV), bf16)
    inter = []
    for b in range(c // seq):
        rows = slice(b * seq, (b + 1) * seq)
        s_old = s0_ref[b]
        sbd = _stack_heads(s_old.astype(bf16), row_head, 1)
        inter.append(_dot(qf[rows], sbd))
        p = _dot_tn(kf[rows], vb[rows])
        upd = None
        for h in range(GLA_HEADS):
            term = jnp.where(row_head == h, p[:, h * GLA_DV:(h + 1) * GLA_DV], 0.0)
            upd = term if upd is None else upd + term
        total = _dot_tn(hi[rows], ones) + _dot_tn(lo[rows], ones)
        sout_ref[b] = jnp.exp2(total) * s_old + upd
    o = o_intra + jnp.concatenate(inter, axis=0)
    _gla_finish(o, r_ref, gnw_ref[...], go_ref)


def _gla_sample(gq, gk, gv, la, gr, state, gnw, bsz, t):
    c = GLA_CHUNK
    nseq = c // t
    levels, tri, masks = _gla_constants(c, t)

    def tok(width):
        return pl.BlockSpec((c, width), lambda i: (i, 0))

    return pl.pallas_call(
        functools.partial(_gla_sample_kernel, levels=levels, seq=t),
        grid=(bsz // nseq,),
        in_specs=[tok(GLA_QK), tok(GLA_QK), tok(GLA_WIDTH), tok(GLA_QK), tok(GLA_WIDTH),
                  pl.BlockSpec((nseq, GLA_QK, GLA_DV), lambda i: (i, 0, 0)),
                  pl.BlockSpec(tri.shape, lambda i: (0, 0)),
                  pl.BlockSpec(masks.shape, lambda i: (0, 0, 0)),
                  pl.BlockSpec((1, GLA_DV), lambda i: (0, 0))],
        out_specs=[tok(GLA_WIDTH),
                   pl.BlockSpec((nseq, GLA_QK, GLA_DV), lambda i: (i, 0, 0))],
        out_shape=[jax.ShapeDtypeStruct((bsz * t, GLA_WIDTH), f32),
                   jax.ShapeDtypeStruct((bsz, GLA_QK, GLA_DV), f32)],
        compiler_params=pltpu.CompilerParams(
            dimension_semantics=("parallel",), vmem_limit_bytes=VMEM_LIMIT),
        name="gla_sample",
    )(gq, gk, gv, la, gr, state, tri, masks, gnw)


def _dup_heads(x, low_lanes):
    rolled = pltpu.roll(x, HEAD_DIM, axis=1)
    return jnp.where(low_lanes, x, rolled), jnp.where(low_lanes, rolled, x)


def _sink_attention(q_ref, kcat, vcat, mask, sink_ref, o_ref, rows):
    nk = kcat.shape[0]
    low_q = (lax.broadcasted_iota(jnp.int32, (rows, LANES), 1) < HEAD_DIM)
    low_k = (lax.broadcasted_iota(jnp.int32, (nk, LANES), 1) < HEAD_DIM)
    k2 = _dup_heads(kcat, low_k)
    v2 = _dup_heads(vcat, low_k)
    pairs_per_group = (N_Q_HEADS // N_KV_HEADS) // 2
    for g in range(N_KV_HEADS):
        kb = k2[g].astype(bf16)
        vb = v2[g].astype(bf16)
        stack = []
        for p in range(pairs_per_group):
            pair = g * pairs_per_group + p
            qp = q_ref[:, pair * LANES:(pair + 1) * LANES].astype(bf16)
            zero = jnp.zeros_like(qp)
            stack.append(jnp.where(low_q, qp, zero))
            stack.append(jnp.where(low_q, zero, qp))
        s_all = _dot_nt(jnp.concatenate(stack, axis=0), kb)
        outs = []
        for hh in range(2 * pairs_per_group):
            head = g * 2 * pairs_per_group + hh
            sink = sink_ref[head]
            s = jnp.where(mask, s_all[hh * rows:(hh + 1) * rows], -1e30)
            mx = jnp.maximum(jnp.max(s, axis=-1, keepdims=True), sink)
            p = jnp.exp(s - mx)
            denom = jnp.sum(p, axis=-1, keepdims=True) + jnp.exp(sink - mx)
            outs.append(_dot(p.astype(bf16), vb) / denom)
        for p in range(pairs_per_group):
            pair = g * pairs_per_group + p
            o_ref[:, pair * LANES:(pair + 1) * LANES] = jnp.where(low_q, outs[2 * p], outs[2 * p + 1])


def _swa_prompt_kernel(sink_ref, q_ref, kp_ref, kc_ref, vp_ref, vc_ref, o_ref):
    i = pl.program_id(1)
    w = q_ref.shape[0]
    kcat = jnp.concatenate([kp_ref[...], kc_ref[...]], axis=0)
    vcat = jnp.concatenate([vp_ref[...], vc_ref[...]], axis=0)
    tq = lax.broadcasted_iota(jnp.int32, (w, 2 * w), 0)
    tk = lax.broadcasted_iota(jnp.int32, (w, 2 * w), 1)
    first_key = jnp.where(i > 0, 0, w)
    rel = tk - tq
    mask = (rel > 0) & (rel <= w) & (tk >= first_key)
    _sink_attention(q_ref, kcat, vcat, mask, sink_ref, o_ref, w)


def _swa_prompt(sq, sk, sv, sinks, bsz, t):
    w = WINDOW
    nb = t // w
    cur = lambda b, i, s: (b * nb + i, 0)
    prev = lambda b, i, s: (b * nb + jnp.maximum(i - 1, 0), 0)
    return pl.pallas_call(
        _swa_prompt_kernel,
        grid_spec=pltpu.PrefetchScalarGridSpec(
            num_scalar_prefetch=1,
            grid=(bsz, nb),
            in_specs=[pl.BlockSpec((w, SWA_Q), cur),
                      pl.BlockSpec((w, SWA_KV), prev), pl.BlockSpec((w, SWA_KV), cur),
                      pl.BlockSpec((w, SWA_KV), prev), pl.BlockSpec((w, SWA_KV), cur)],
            out_specs=pl.BlockSpec((w, SWA_Q), cur)),
        out_shape=jax.ShapeDtypeStruct((bsz * t, SWA_Q), f32),
        compiler_params=pltpu.CompilerParams(
            dimension_semantics=("parallel", "parallel"), vmem_limit_bytes=VMEM_LIMIT),
        name="swa_prompt",
    )(sinks, sq, sk, sk, sv, sv)


def _swa_sample_kernel(sink_ref, q_ref, kn_ref, vn_ref, pk_ref, pv_ref, o_ref, ko_ref, vo_ref, *, seq):
    nseq = pk_ref.shape[0]
    w = pk_ref.shape[1]
    nk = w + seq
    rows = lax.broadcasted_iota(jnp.int32, (seq, nk), 0)
    cols = lax.broadcasted_iota(jnp.int32, (seq, nk), 1)
    mask = (cols > rows) & (cols <= rows + w)
    for b in range(nseq):
        tok = pl.ds(b * seq, seq)
        kcat = jnp.concatenate([pk_ref[b], kn_ref[tok, :]], axis=0)
        vcat = jnp.concatenate([pv_ref[b], vn_ref[tok, :]], axis=0)
        ko_ref[b] = kcat[seq:]
        vo_ref[b] = vcat[seq:]
        _sink_attention(q_ref.at[tok, :], kcat, vcat, mask, sink_ref, o_ref.at[tok, :], seq)


def _swa_sample(sq, sk, sv, past_k, past_v, sinks, bsz, t):
    nseq = 16
    w = past_k.shape[1]
    tok = lambda width: pl.BlockSpec((nseq * t, width), lambda i, s: (i, 0))
    cache = pl.BlockSpec((nseq, w, SWA_KV), lambda i, s: (i, 0, 0))
    return pl.pallas_call(
        functools.partial(_swa_sample_kernel, seq=t),
        grid_spec=pltpu.PrefetchScalarGridSpec(
            num_scalar_prefetch=1,
            grid=(bsz // nseq,),
            in_specs=[tok(SWA_Q), tok(SWA_KV), tok(SWA_KV), cache, cache],
            out_specs=[tok(SWA_Q), cache, cache]),
        out_shape=[jax.ShapeDtypeStruct((bsz * t, SWA_Q), f32),
                   jax.ShapeDtypeStruct((bsz, w, SWA_KV), f32),
                   jax.ShapeDtypeStruct((bsz, w, SWA_KV), f32)],
        compiler_params=pltpu.CompilerParams(
            dimension_semantics=("parallel",), vmem_limit_bytes=VMEM_LIMIT),
        name="swa_sample",
    )(sinks, sq, sk, sv, past_k, past_v)


def _out_ffn_kernel(x_ref, go_ref, so_ref, g1_ref, sh2_ref, sc2_ref, g2_ref, n2_ref,
                    wout_ref, w1_ref, w2_ref, y_ref):
    nb, t, d = x_ref.shape
    m = nb * t
    mixed = (_dot(go_ref[...].astype(bf16), wout_ref[:GLA_WIDTH, :])
             + _dot(so_ref[...].astype(bf16), wout_ref[GLA_WIDTH:, :]))
    h = x_ref[...] + g1_ref[...] * mixed.reshape(nb, t, d)
    ms = jnp.mean(h * h, axis=-1, keepdims=True)
    hn = h * lax.rsqrt(ms + EPS) * n2_ref[...]
    hn = hn * (1.0 + sc2_ref[...]) + sh2_ref[...]
    hb = hn.reshape(m, d).astype(bf16)
    fc = 1024
    ff = None
    for c in range(D_FF // fc):
        a = jnp.maximum(_dot(hb, w1_ref[:, c * fc:(c + 1) * fc]), 0.0)
        part = _dot((a * a).astype(bf16), w2_ref[c * fc:(c + 1) * fc, :])
        ff = part if ff is None else ff + part
    y_ref[...] = h + g2_ref[...] * ff.reshape(nb, t, d)


def _out_ffn(x, go, so, g1, sh2, sc2, g2, n2, wout, w1, w2, nb, tb):
    bsz, t, d = x.shape
    m = nb * tb
    nt = t // tb

    def full(shape):
        return pl.BlockSpec(shape, lambda i, j: (0,) * len(shape), pipeline_mode=pl.Buffered(1))

    modspec = pl.BlockSpec((nb, 1, d), lambda i, j: (i, 0, 0))
    xspec = pl.BlockSpec((nb, tb, d), lambda i, j: (i, j, 0))
    return pl.pallas_call(
        _out_ffn_kernel,
        grid=(bsz // nb, nt),
        in_specs=[xspec,
                  pl.BlockSpec((m, GLA_WIDTH), lambda i, j: (i * nt + j, 0)),
                  pl.BlockSpec((m, SWA_Q), lambda i, j: (i * nt + j, 0)),
                  modspec, modspec, modspec, modspec,
                  pl.BlockSpec((1, 1, d), lambda i, j: (0, 0, 0)),
                  full((d, d)), full((d, D_FF)), full((D_FF, d))],
        out_specs=xspec,
        out_shape=jax.ShapeDtypeStruct((bsz, t, d), f32),
        compiler_params=pltpu.CompilerParams(
            dimension_semantics=("parallel", "parallel"), vmem_limit_bytes=VMEM_LIMIT),
        name="out_ffn",
    )(x, go, so, g1, sh2, sc2, g2, n2, wout, w1, w2)


def _rope_tables(pos):
    half = HEAD_DIM // 2
    inv = jnp.power(ROPE_THETA, -jnp.arange(half, dtype=f32) * 2.0 / HEAD_DIM)
    ang = pos.astype(f32)[:, None] * inv[None, :]
    cos = jnp.cos(ang)
    sin = jnp.sin(ang)
    reps = LANES // HEAD_DIM
    return (jnp.tile(jnp.concatenate([cos, cos], axis=-1), (1, reps)),
            jnp.tile(jnp.concatenate([-sin, sin], axis=-1), (1, reps)))


def _block_diag_ones(n, blk):
    idx = np.arange(n) // blk
    return jnp.asarray((idx[:, None] == idx[None, :]).astype(np.float32), bf16)


def _layer_weights(w_in, w_gate_up, b_gate, q_norm_w, k_norm_w, w_out, w_ff1, w_ff2):
    splits = np.cumsum([GLA_QK, GLA_QK, GLA_WIDTH, GLA_WIDTH, GLA_GATE_RANK, SWA_Q, SWA_KV])
    gq, gk, gv, gr, glr, sq, sk, sv = jnp.split(w_in, [int(s) for s in splits], axis=1)
    glr = jnp.pad(glr, ((0, 0), (0, LANES - GLA_GATE_RANK)))
    win = jnp.concatenate([gq, gk, gv, gr, sq, sk, sv, glr], axis=1).astype(bf16)
    wgu = jnp.pad(w_gate_up, ((0, LANES - GLA_GATE_RANK), (0, 0))).astype(bf16)
    return dict(
        win=win, wgu=wgu, bg=b_gate.reshape(1, GLA_QK),
        qnw=jnp.tile(q_norm_w, N_Q_HEADS).reshape(1, SWA_Q),
        knw=jnp.tile(k_norm_w, N_KV_HEADS).reshape(1, SWA_KV),
        wout=w_out.astype(bf16), w1=w_ff1.astype(bf16), w2=w_ff2.astype(bf16))


def _decoder_layer(x, mod, pos, state, past_k, past_v, lw, n1, n2, gnw, sinks, bdq, bdk, nb, tb):
    bsz, t, d = x.shape
    sh1, sc1, g1, sh2, sc2, g2 = [mod[:, None, i * d:(i + 1) * d] for i in range(6)]
    cos_t, sin_t = _rope_tables(pos)
    if nb > 1:
        cos_t = jnp.tile(cos_t, (nb, 1))
        sin_t = jnp.tile(sin_t, (nb, 1))
    gq, gk, gv, gr, la, sq, sk, sv = _inproj(
        x, sh1, sc1, n1.reshape(1, 1, d), lw["win"], lw["wgu"], lw["bg"], lw["qnw"], lw["knw"],
        bdq, bdk, cos_t, sin_t, nb, tb)
    if state is None:
        go, s_t = _gla_prompt(gq, gk, gv, la, gr, gnw, bsz, t)
        s_new = s_t.reshape(bsz, GLA_DV, GLA_HEADS, GLA_DK).transpose(0, 2, 3, 1)
        so = _swa_prompt(sq, sk, sv, sinks, bsz, t)
        k_keep = sk.reshape(bsz, t, SWA_KV)[:, -WINDOW:]
        v_keep = sv.reshape(bsz, t, SWA_KV)[:, -WINDOW:]
    else:
        go, s_new = _gla_sample(gq, gk, gv, la, gr, state.reshape(bsz, GLA_QK, GLA_DV), gnw, bsz, t)
        so, k_keep, v_keep = _swa_sample(
            sq, sk, sv, past_k.reshape(bsz, WINDOW, SWA_KV), past_v.reshape(bsz, WINDOW, SWA_KV),
            sinks, bsz, t)
    y = _out_ffn(x, go, so, g1, sh2, sc2, g2, n2.reshape(1, 1, d), lw["wout"], lw["w1"], lw["w2"], nb, tb)
    return (y, s_new.reshape(bsz, GLA_HEADS, GLA_DK, GLA_DV),
            k_keep.reshape(bsz, WINDOW, N_KV_HEADS, HEAD_DIM),
            v_keep.reshape(bsz, WINDOW, N_KV_HEADS, HEAD_DIM))


def kernel(x_prompt, x_sample, state_gla, cache_swa_k, cache_swa_v, c_prompt, c_sample, w_ada, b_ada, norm1_w, norm2_w, w_in, w_gate_up, b_gate, gla_norm_w, q_norm_w, k_norm_w, sinks, w_out, w_ff1, w_ff2):
    depth = w_ada.shape[0]
    bp, tp, _ = x_prompt.shape
    bs, ts, _ = x_sample.shape
    pos_p = jnp.arange(tp)
    pos_s = PAST_LEN + jnp.arange(ts)
    bdq = _block_diag_ones(SWA_Q, HEAD_DIM)
    bdk = _block_diag_ones(SWA_KV, HEAD_DIM)
    c_all = jnp.concatenate([c_prompt, c_sample], axis=0)
    yp, ys = x_prompt, x_sample
    outs = [[] for _ in range(6)]
    for l in range(depth):
        mod = _modulation(c_all, w_ada[l], b_ada[l])
        lw = _layer_weights(w_in[l], w_gate_up[l], b_gate[l], q_norm_w[l], k_norm_w[l],
                            w_out[l], w_ff1[l], w_ff2[l])
        gnw = gla_norm_w[l].reshape(1, GLA_DV)
        common = (lw, norm1_w[l], norm2_w[l], gnw, sinks[l], bdq, bdk)
        yp, gp, kp, vp = _decoder_layer(yp, mod[:bp], pos_p, None, None, None, *common, nb=1, tb=512)
        ys, gs, kq, vq = _decoder_layer(ys, mod[bp:], pos_s, state_gla[l], cache_swa_k[l],
                                        cache_swa_v[l], *common, nb=256 // ts, tb=ts)
        for lst, val in zip(outs, (gp, kp, vp, gs, kq, vq)):
            lst.append(val)
    return (yp, ys) + tuple(jnp.stack(o) for o in outs)
```

```python
import functools

import jax
import jax.numpy as jnp
import numpy as np
from jax import lax
from jax.experimental import pallas as pl
from jax.experimental.pallas import tpu as pltpu

f32 = jnp.float32
bf16 = jnp.bfloat16

D_MODEL = 1024
GLA_HEADS = 4
GLA_DK = 64
GLA_DV = 128
GLA_QK = GLA_HEADS * GLA_DK
GLA_WIDTH = GLA_HEADS * GLA_DV
GLA_GATE_RANK = 16
GLA_TAU = 16.0
LOG2E = 1.4426950408889634
HEAD_DIM = 64
N_Q_HEADS = 8
N_KV_HEADS = 2
SWA_Q = N_Q_HEADS * HEAD_DIM
SWA_KV = N_KV_HEADS * HEAD_DIM
SWA_KV2 = 2 * SWA_KV
WINDOW = 128
ROPE_THETA = 10000.0
PAST_LEN = 8192
D_FF = 4 * D_MODEL
EPS = 1e-6
LANES = 128
GLA_CHUNK = 128
TOKEN_BLOCK = 512
SAMPLE_ROWS = 256
VMEM_LIMIT = 56 * 1024 * 1024

_SEG = {}
_off = 0
for _name, _w in (("gq", GLA_QK), ("gk", GLA_QK), ("gv", GLA_WIDTH), ("gr", GLA_WIDTH),
                  ("sq", SWA_Q), ("sk", SWA_KV), ("sv", SWA_KV), ("glr", LANES)):
    _SEG[_name] = (_off, _off + _w)
    _off += _w
IN_WIDTH_PADDED = _off


def _dot(a, b):
    return jnp.dot(a, b, preferred_element_type=f32)


def _dot_nt(a, b):
    return lax.dot_general(a, b, (((1,), (1,)), ((), ())), preferred_element_type=f32)


def _dot_tn(a, b):
    return lax.dot_general(a, b, (((0,), (0,)), ((), ())), preferred_element_type=f32)


def _sigmoid(x):
    return 1.0 / (1.0 + jnp.exp(-x))


def _mod_kernel(c_ref, w_ref, b_ref, o_ref):
    c = c_ref[...]
    s = (c * _sigmoid(c)).astype(bf16)
    o_ref[...] = _dot(s, w_ref[...].astype(bf16)) + b_ref[...]


def _modulation(c_all, w_ada, b_ada):
    m = c_all.shape[0]
    n = w_ada.shape[1]
    bn = 1536
    return pl.pallas_call(
        _mod_kernel,
        grid=(n // bn,),
        in_specs=[pl.BlockSpec((m, D_MODEL), lambda j: (0, 0)),
                  pl.BlockSpec((D_MODEL, bn), lambda j: (0, j)),
                  pl.BlockSpec((1, bn), lambda j: (0, j))],
        out_specs=pl.BlockSpec((m, bn), lambda j: (0, j)),
        out_shape=jax.ShapeDtypeStruct((m, n), f32),
        compiler_params=pltpu.CompilerParams(vmem_limit_bytes=VMEM_LIMIT),
        name="adaln_mod",
    )(c_all, w_ada, b_ada.reshape(1, n))


def _group_rms(x, bd_ref, w_ref):
    ssq = _dot((x * x).astype(bf16), bd_ref[...])
    return x * lax.rsqrt(ssq * (1.0 / HEAD_DIM) + EPS) * w_ref[...]


def _rope(x, cos, sin_signed, low_half):
    partner = jnp.where(low_half, pltpu.roll(x, LANES - 32, axis=1), pltpu.roll(x, 32, axis=1))
    return x * cos + partner * sin_signed


def _dup_heads(x, low_lanes):
    rolled = pltpu.roll(x, HEAD_DIM, axis=1)
    return jnp.where(low_lanes, x, rolled), jnp.where(low_lanes, rolled, x)


def _inproj_kernel(x_ref, sh_ref, sc_ref, n1_ref, win_ref, wgu_ref, bg_ref, qnw_ref, knw_ref,
                   bdq_ref, bdk_ref, cos_ref, sin_ref,
                   gq_ref, gk_ref, gv_ref, gr_ref, la_ref, sq_ref, k2_ref, v2_ref, kk_ref, vk_ref):
    nb, t, d = x_ref.shape
    m = nb * t
    tail = kk_ref.shape[0]
    x = x_ref[...]
    ms = jnp.mean(x * x, axis=-1, keepdims=True)
    hn = x * lax.rsqrt(ms + EPS) * n1_ref[...]
    hn = hn * (1.0 + sc_ref[...]) + sh_ref[...]
    hb = hn.reshape(m, d).astype(bf16)

    def seg(name):
        a, b = _SEG[name]
        return _dot(hb, win_ref[:, a:b])

    gq_ref[...] = seg("gq") * (GLA_DK ** -0.5)
    gk_ref[...] = seg("gk")
    gv_ref[...] = seg("gv").astype(gv_ref.dtype)
    gr_ref[...] = seg("gr").astype(gr_ref.dtype)
    glr = seg("glr").astype(bf16)
    g = _dot(glr, wgu_ref[...]) + bg_ref[...]
    log_sig = jnp.minimum(g, 0.0) - jnp.log1p(jnp.exp(-jnp.abs(g)))
    la_ref[...] = log_sig * (LOG2E / GLA_TAU)

    cos = cos_ref[...]
    sin = sin_ref[...]
    lane = lax.broadcasted_iota(jnp.int32, (m, LANES), 1)
    low_half = (lane & 32) == 0
    low_lanes = lane < HEAD_DIM
    sq = _group_rms(seg("sq"), bdq_ref, qnw_ref)
    for c in range(SWA_Q // LANES):
        blk = _rope(sq[:, c * LANES:(c + 1) * LANES], cos, sin, low_half)
        sq_ref[:, c * LANES:(c + 1) * LANES] = (blk * (HEAD_DIM ** -0.5)).astype(sq_ref.dtype)
    sk = _rope(_group_rms(seg("sk"), bdk_ref, knw_ref), cos, sin, low_half)
    sv = seg("sv")
    kk_ref[...] = sk[m - tail:]
    vk_ref[...] = sv[m - tail:]
    for src, dst in ((sk, k2_ref), (sv, v2_ref)):
        d0, d1 = _dup_heads(src, low_lanes)
        dst[:, :LANES] = d0.astype(dst.dtype)
        dst[:, LANES:] = d1.astype(dst.dtype)


def _inproj(x, sh, sc, n1, win, wgu, bg, qnw, knw, bdq, bdk, cos_t, sin_t, nb, tb, tail, act):
    bsz, t, d = x.shape
    m = nb * tb
    nt = t // tb
    grid = (bsz // nb, nt)
    tok = bsz * t

    def full(shape):
        return pl.BlockSpec(shape, lambda i, j: (0,) * len(shape))

    def out(width):
        return pl.BlockSpec((m, width), lambda i, j: (i * nt + j, 0))

    tail_spec = pl.BlockSpec((tail, SWA_KV), lambda i, j: (i, 0))
    outs = ((GLA_QK, f32), (GLA_QK, f32), (GLA_WIDTH, act), (GLA_WIDTH, act), (GLA_QK, f32),
            (SWA_Q, act), (SWA_KV2, bf16), (SWA_KV2, bf16))
    ntail = (bsz // nb) * tail
    return pl.pallas_call(
        _inproj_kernel,
        grid=grid,
        in_specs=[pl.BlockSpec((nb, tb, d), lambda i, j: (i, j, 0)),
                  pl.BlockSpec((nb, 1, d), lambda i, j: (i, 0, 0)),
                  pl.BlockSpec((nb, 1, d), lambda i, j: (i, 0, 0)),
                  full((1, 1, d)),
                  full((d, IN_WIDTH_PADDED)),
                  full((LANES, GLA_QK)),
                  full((1, GLA_QK)),
                  full((1, SWA_Q)),
                  full((1, SWA_KV)),
                  full((SWA_Q, SWA_Q)),
                  full((SWA_KV, SWA_KV)),
                  pl.BlockSpec((m, LANES), lambda i, j: (j, 0)),
                  pl.BlockSpec((m, LANES), lambda i, j: (j, 0))],
        out_specs=[out(w) for w, _ in outs] + [tail_spec, tail_spec],
        out_shape=[jax.ShapeDtypeStruct((tok, w), dt) for w, dt in outs]
        + [jax.ShapeDtypeStruct((ntail, SWA_KV), f32)] * 2,
        compiler_params=pltpu.CompilerParams(
            dimension_semantics=("parallel", "arbitrary"), vmem_limit_bytes=VMEM_LIMIT),
        name="inproj",
    )(x, sh, sc, n1, win, wgu, bg, qnw, knw, bdq, bdk, cos_t, sin_t)


def _gla_constants(chunk, seq):
    t = np.arange(chunk)
    levels = []
    m = 1
    while m < seq:
        levels.append(m)
        m *= 2
    masks = [np.eye(chunk, dtype=bool)]
    for m in levels:
        upper = (t % (2 * m) >= m)[:, None]
        lower = (t % (2 * m) < m)[None, :]
        same = (t[:, None] // (2 * m)) == (t[None, :] // (2 * m))
        masks.append(same & upper & lower)
    tri = t[None, :] <= t[:, None]
    return (tuple(levels), jnp.asarray(tri.astype(np.float32), bf16),
            jnp.asarray(np.stack(masks).astype(np.float32)))


def _stack_heads(xb, lane_head, axis):
    zero = jnp.zeros_like(xb)
    return jnp.concatenate([jnp.where(lane_head == h, xb, zero) for h in range(GLA_HEADS)], axis=axis)


def _block_sums(la, bcum, m, row, rolls):
    c, n = la.shape
    if m == 1:
        return la, None
    if m < 8:
        def rolled(shift):
            if shift not in rolls:
                rolls[shift] = pltpu.roll(la, shift % c, axis=0)
            return rolls[shift]
        pos = row & (m - 1)
        pre = la
        suf = None
        for j in range(1, m):
            pre = pre + jnp.where(pos >= j, rolled(j), 0.0)
            term = jnp.where(pos < m - j, rolled(-j), 0.0)
            suf = term if suf is None else suf + term
        return pre, suf
    before, last = [], []
    for i in range(c // m):
        before.append(jnp.zeros((m, n), f32) if i == 0
                      else jnp.broadcast_to(bcum[i * m - 1:i * m], (m, n)))
        last.append(jnp.broadcast_to(bcum[(i + 1) * m - 1:(i + 1) * m], (m, n)))
    if len(before) == 1:
        return bcum - before[0], last[0] - bcum
    return bcum - jnp.concatenate(before, axis=0), jnp.concatenate(last, axis=0) - bcum


def _gla_chunk(q_ref, k_ref, v_ref, la_ref, tri_ref, masks_ref, levels, seq):
    c = q_ref.shape[0]
    la = la_ref[...]
    hi = la.astype(bf16)
    lo = (la - hi.astype(f32)).astype(bf16)
    tri = tri_ref[...]
    bcum = _dot(tri, hi) + _dot(tri, lo)
    q = q_ref[...]
    k = k_ref[...]
    vb = v_ref[...].astype(bf16)
    row = lax.broadcasted_iota(jnp.int32, (c, GLA_QK), 0)
    lane_head = lax.broadcasted_iota(jnp.int32, (c, GLA_QK), 1) >> 6
    rolls = {}
    attn = None
    for lvl, m in enumerate((0,) + levels):
        if m == 0:
            qt, kt = q, k
        else:
            pre, suf = _block_sums(la, bcum, m, row, rolls)
            qt = q * jnp.exp2(pre)
            kt = k if suf is None else k * jnp.exp2(suf)
        r = _dot_nt(qt.astype(bf16), _stack_heads(kt.astype(bf16), lane_head, 0))
        mk = masks_ref[lvl]
        parts = [r[:, h * c:(h + 1) * c] * mk for h in range(GLA_HEADS)]
        attn = parts if attn is None else [a + p for a, p in zip(attn, parts)]
    a = jnp.concatenate(attn, axis=1).astype(bf16)
    v_head = lax.broadcasted_iota(jnp.int32, (c, GLA_WIDTH), 1) >> 7
    vbd = _stack_heads(vb, v_head, 0)
    o_intra = _dot(a, vbd)
    pre, suf = _block_sums(la, bcum, seq, row, rolls)
    qf = (q * jnp.exp2(pre)).astype(bf16)
    kf = (k * jnp.exp2(suf)).astype(bf16)
    return o_intra, qf, kf, vb, bcum, hi, lo


def _gla_finish(o, r_ref, gnw, go_ref):
    for h in range(GLA_HEADS):
        sl = slice(h * GLA_DV, (h + 1) * GLA_DV)
        oh = o[:, sl]
        r = r_ref[:, sl].astype(f32)
        ms = jnp.mean(oh * oh, axis=-1, keepdims=True)
        go_ref[:, sl] = (oh * lax.rsqrt(ms + EPS) * gnw * (r * _sigmoid(r))).astype(go_ref.dtype)


def _gla_prompt_chunk(st, q_ref, k_ref, v_ref, la_ref, r_ref, tri_ref, masks_ref, gnw, go_ref, levels):
    c = q_ref.shape[0]
    o_intra, qf, kf, vb, bcum, _, _ = _gla_chunk(q_ref, k_ref, v_ref, la_ref, tri_ref, masks_ref, levels, c)
    lane_head = lax.broadcasted_iota(jnp.int32, (GLA_DV, GLA_QK), 1) >> 6
    sbd_t = _stack_heads(st.astype(bf16), lane_head, 0)
    o = o_intra + _dot_nt(qf, sbd_t)
    _gla_finish(o, r_ref, gnw, go_ref)
    full = _dot_tn(vb, kf)
    upd = None
    for h in range(GLA_HEADS):
        term = jnp.where(lane_head == h, full[h * GLA_DV:(h + 1) * GLA_DV], 0.0)
        upd = term if upd is None else upd + term
    return jnp.exp2(bcum[c - 1:c]) * st + upd


def _gla_sample_kernel(q_ref, k_ref, v_ref, la_ref, r_ref, s0_ref, tri_ref, masks_ref, gnw_ref,
                       go_ref, sout_ref, *, levels, seq):
    c = q_ref.shape[0]
    o_intra, qf, kf, vb, _, hi, lo = _gla_chunk(q_ref, k_ref, v_ref, la_ref, tri_ref, masks_ref, levels, seq)
    row_head = lax.broadcasted_iota(jnp.int32, (GLA_QK, GLA_DV), 0) >> 6
    ones = jnp.ones((seq, GLA_DV), bf16)
    inter = []
    for b in range(c // seq):
        rows = slice(b * seq, (b + 1) * seq)
        s_old = s0_ref[b]
        sbd = _stack_heads(s_old.astype(bf16), row_head, 1)
        inter.append(_dot(qf[rows], sbd))
        p = _dot_tn(kf[rows], vb[rows])
        upd = None
        for h in range(GLA_HEADS):
            term = jnp.where(row_head == h, p[:, h * GLA_DV:(h + 1) * GLA_DV], 0.0)
            upd = term if upd is None else upd + term
        total = _dot_tn(hi[rows], ones) + _dot_tn(lo[rows], ones)
        sout_ref[b] = jnp.exp2(total) * s_old + upd
    o = o_intra + jnp.concatenate(inter, axis=0)
    _gla_finish(o, r_ref, gnw_ref[...], go_ref)


def _gla_sample(gq, gk, gv, la, gr, state, gnw, bsz, t):
    c = GLA_CHUNK
    nseq = c // t
    levels, tri, masks = _gla_constants(c, t)

    def tok(width):
        return pl.BlockSpec((c, width), lambda i: (i, 0))

    return pl.pallas_call(
        functools.partial(_gla_sample_kernel, levels=levels, seq=t),
        grid=(bsz // nseq,),
        in_specs=[tok(GLA_QK), tok(GLA_QK), tok(GLA_WIDTH), tok(GLA_QK), tok(GLA_WIDTH),
                  pl.BlockSpec((nseq, GLA_QK, GLA_DV), lambda i: (i, 0, 0)),
                  pl.BlockSpec(tri.shape, lambda i: (0, 0)),
                  pl.BlockSpec(masks.shape, lambda i: (0, 0, 0)),
                  pl.BlockSpec((1, GLA_DV), lambda i: (0, 0))],
        out_specs=[tok(GLA_WIDTH),
                   pl.BlockSpec((nseq, GLA_QK, GLA_DV), lambda i: (i, 0, 0))],
        out_shape=[jax.ShapeDtypeStruct((bsz * t, GLA_WIDTH), f32),
                   jax.ShapeDtypeStruct((bsz, GLA_QK, GLA_DV), f32)],
        compiler_params=pltpu.CompilerParams(
            dimension_semantics=("parallel",), vmem_limit_bytes=VMEM_LIMIT),
        name="gla_sample",
    )(gq, gk, gv, la, gr, state, tri, masks, gnw)


def _sink_attention(q_ref, kdup, vdup, mask, sink_ref, o_ref, rows):
    low_q = (lax.broadcasted_iota(jnp.int32, (rows, LANES), 1) < HEAD_DIM)
    pairs_per_group = (N_Q_HEADS // N_KV_HEADS) // 2
    for g in range(N_KV_HEADS):
        stack = []
        for p in range(pairs_per_group):
            pair = g * pairs_per_group + p
            qp = q_ref[:, pair * LANES:(pair + 1) * LANES].astype(bf16)
            zero = jnp.zeros_like(qp)
            stack.append(jnp.where(low_q, qp, zero))
            stack.append(jnp.where(low_q, zero, qp))
        s_all = _dot_nt(jnp.concatenate(stack, axis=0), kdup[g])
        outs = []
        for hh in range(2 * pairs_per_group):
            head = g * 2 * pairs_per_group + hh
            sink = sink_ref[head]
            s = jnp.where(mask, s_all[hh * rows:(hh + 1) * rows], -1e30)
            mx = jnp.maximum(jnp.max(s, axis=-1, keepdims=True), sink)
            p = jnp.exp(s - mx)
            denom = jnp.sum(p, axis=-1, keepdims=True) + jnp.exp(sink - mx)
            outs.append(_dot(p.astype(bf16), vdup[g]) / denom)
        for p in range(pairs_per_group):
            pair = g * pairs_per_group + p
            o_ref[:, pair * LANES:(pair + 1) * LANES] = jnp.where(
                low_q, outs[2 * p], outs[2 * p + 1]).astype(o_ref.dtype)


def _swa_prompt_block(q_ref, k_prev, k_cur, v_prev, v_cur, first, sink_ref, o_ref):
    w = q_ref.shape[0]
    kdup = [jnp.concatenate([k_prev[:, g * LANES:(g + 1) * LANES], k_cur[:, g * LANES:(g + 1) * LANES]], axis=0)
            for g in range(N_KV_HEADS)]
    vdup = [jnp.concatenate([v_prev[:, g * LANES:(g + 1) * LANES], v_cur[:, g * LANES:(g + 1) * LANES]], axis=0)
            for g in range(N_KV_HEADS)]
    tq = lax.broadcasted_iota(jnp.int32, (w, 2 * w), 0)
    tk = lax.broadcasted_iota(jnp.int32, (w, 2 * w), 1)
    first_key = jnp.where(first, w, 0)
    rel = tk - tq
    mask = (rel > 0) & (rel <= w) & (tk >= first_key)
    _sink_attention(q_ref, kdup, vdup, mask, sink_ref, o_ref, w)


def _swa_sample_kernel(sink_ref, q_ref, kn_ref, vn_ref, pk_ref, pv_ref, o_ref, ko_ref, vo_ref, *, seq):
    nseq = pk_ref.shape[0]
    w = pk_ref.shape[1]
    nk = w + seq
    rows = lax.broadcasted_iota(jnp.int32, (seq, nk), 0)
    cols = lax.broadcasted_iota(jnp.int32, (seq, nk), 1)
    mask = (cols > rows) & (cols <= rows + w)
    low_k = lax.broadcasted_iota(jnp.int32, (nk, LANES), 1) < HEAD_DIM
    for b in range(nseq):
        tok = pl.ds(b * seq, seq)
        kcat = jnp.concatenate([pk_ref[b], kn_ref[tok, :]], axis=0)
        vcat = jnp.concatenate([pv_ref[b], vn_ref[tok, :]], axis=0)
        ko_ref[b] = kcat[seq:]
        vo_ref[b] = vcat[seq:]
        kdup = [d.astype(bf16) for d in _dup_heads(kcat, low_k)]
        vdup = [d.astype(bf16) for d in _dup_heads(vcat, low_k)]
        _sink_attention(q_ref.at[tok, :], kdup, vdup, mask, sink_ref, o_ref.at[tok, :], seq)


def _swa_sample(sq, sk, sv, past_k, past_v, sinks, bsz, t):
    nseq = 16
    w = past_k.shape[1]
    tok = lambda width: pl.BlockSpec((nseq * t, width), lambda i, s: (i, 0))
    cache = pl.BlockSpec((nseq, w, SWA_KV), lambda i, s: (i, 0, 0))
    return pl.pallas_call(
        functools.partial(_swa_sample_kernel, seq=t),
        grid_spec=pltpu.PrefetchScalarGridSpec(
            num_scalar_prefetch=1,
            grid=(bsz // nseq,),
            in_specs=[tok(SWA_Q), tok(SWA_KV), tok(SWA_KV), cache, cache],
            out_specs=[tok(SWA_Q), cache, cache]),
        out_shape=[jax.ShapeDtypeStruct((bsz * t, SWA_Q), f32),
                   jax.ShapeDtypeStruct((bsz, w, SWA_KV), f32),
                   jax.ShapeDtypeStruct((bsz, w, SWA_KV), f32)],
        compiler_params=pltpu.CompilerParams(
            dimension_semantics=("parallel",), vmem_limit_bytes=VMEM_LIMIT),
        name="swa_sample",
    )(sinks, sq, sk, sv, past_k, past_v)


FFN_CHUNK = 1024


def _out_proj_norm(x, go, so, g1, sh2, sc2, n2, wout_ref):
    nb, t, d = x.shape
    mixed = _dot(go, wout_ref[:GLA_WIDTH, :]) + _dot(so, wout_ref[GLA_WIDTH:, :])
    h = x + g1 * mixed.reshape(nb, t, d)
    ms = jnp.mean(h * h, axis=-1, keepdims=True)
    hn = h * lax.rsqrt(ms + EPS) * n2
    hn = hn * (1.0 + sc2) + sh2
    return h, hn.reshape(nb * t, d).astype(bf16)


def _ffn_piece(hb, w1_ref, w2_ref, c):
    cols = slice(c * FFN_CHUNK, (c + 1) * FFN_CHUNK)
    a = jnp.maximum(_dot(hb, w1_ref[:, cols]), 0.0)
    return _dot((a * a).astype(bf16), w2_ref[cols, :])


def _out_ffn_block(x, go, so, g1, sh2, sc2, g2, n2, wout_ref, w1_ref, w2_ref):
    h, hb = _out_proj_norm(x, go, so, g1, sh2, sc2, n2, wout_ref)
    ff = _ffn_piece(hb, w1_ref, w2_ref, 0)
    for c in range(1, D_FF // FFN_CHUNK):
        ff = ff + _ffn_piece(hb, w1_ref, w2_ref, c)
    return h + g2 * ff.reshape(h.shape)


def _out_ffn_kernel(x_ref, go_ref, so_ref, g1_ref, sh2_ref, sc2_ref, g2_ref, n2_ref,
                    wout_ref, w1_ref, w2_ref, y_ref):
    y_ref[...] = _out_ffn_block(x_ref[...], go_ref[...].astype(bf16), so_ref[...].astype(bf16),
                                g1_ref[...], sh2_ref[...],
                                sc2_ref[...], g2_ref[...], n2_ref[...], wout_ref, w1_ref, w2_ref)


def _resident(shape):
    return pl.BlockSpec(shape, lambda *_: (0,) * len(shape), pipeline_mode=pl.Buffered(1))


def _out_ffn(x, go, so, g1, sh2, sc2, g2, n2, wout, w1, w2, nb, tb):
    bsz, t, d = x.shape
    m = nb * tb
    nt = t // tb
    modspec = pl.BlockSpec((nb, 1, d), lambda i, j: (i, 0, 0))
    xspec = pl.BlockSpec((nb, tb, d), lambda i, j: (i, j, 0))
    return pl.pallas_call(
        _out_ffn_kernel,
        grid=(bsz // nb, nt),
        in_specs=[xspec,
                  pl.BlockSpec((m, GLA_WIDTH), lambda i, j: (i * nt + j, 0)),
                  pl.BlockSpec((m, SWA_Q), lambda i, j: (i * nt + j, 0)),
                  modspec, modspec, modspec, modspec,
                  pl.BlockSpec((1, 1, d), lambda i, j: (0, 0, 0)),
                  _resident((d, d)), _resident((d, D_FF)), _resident((D_FF, d))],
        out_specs=xspec,
        out_shape=jax.ShapeDtypeStruct((bsz, t, d), f32),
        compiler_params=pltpu.CompilerParams(
            dimension_semantics=("parallel", "parallel"), vmem_limit_bytes=VMEM_LIMIT),
        name="out_ffn",
    )(x, go, so, g1, sh2, sc2, g2, n2, wout, w1, w2)


def _prompt_kernel(sink_ref,
                   q_ref, k_ref, v_ref, la_ref, r_ref, sq_ref, k2_ref, v2_ref, k2p_ref, v2p_ref,
                   tri_ref, masks_ref, gnw_ref,
                   x_ref, g1_ref, sh2_ref, sc2_ref, g2_ref, n2_ref, wout_ref, w1_ref, w2_ref,
                   y_ref, sout_ref, state_ref, mix_ref, *, levels, nt, nsteps):
    s = pl.program_id(0)
    slot = s % 2
    c = GLA_CHUNK
    nchunk = q_ref.shape[0] // c

    @pl.when(s == 0)
    def _():
        mix_ref[1] = jnp.zeros(mix_ref.shape[1:], mix_ref.dtype)
        state_ref[...] = jnp.zeros_like(state_ref)

    j = jnp.minimum(s, nsteps - 1) % nt
    st_in = state_ref[...]
    st = jnp.where(j == 0, 0.0, st_in)
    gnw = gnw_ref[...]
    out = mix_ref.at[slot]
    for ci in range(nchunk):
        rows = pl.ds(ci * c, c)
        st = _gla_prompt_chunk(st, q_ref.at[rows], k_ref.at[rows], v_ref.at[rows], la_ref.at[rows],
                               r_ref.at[rows], tri_ref, masks_ref, gnw,
                               out.at[rows, pl.ds(0, GLA_WIDTH)], levels)
    for ci in range(nchunk):
        rows = pl.ds(ci * c, c)
        prev = pl.ds((ci - 1) * c, c)
        k_prev = k2p_ref[...] if ci == 0 else k2_ref[prev, :]
        v_prev = v2p_ref[...] if ci == 0 else v2_ref[prev, :]
        first = (j == 0) if ci == 0 else False
        _swa_prompt_block(sq_ref.at[rows], k_prev, k2_ref[rows, :], v_prev, v2_ref[rows, :], first,
                          sink_ref, out.at[rows, pl.ds(GLA_WIDTH, SWA_Q)])
    st = jnp.where(s < nsteps, st, st_in)
    state_ref[...] = st
    sout_ref[0] = st
    mix = mix_ref[1 - slot]
    y_ref[...] = _out_ffn_block(x_ref[...], mix[:, :GLA_WIDTH], mix[:, GLA_WIDTH:], g1_ref[...],
                                sh2_ref[...], sc2_ref[...], g2_ref[...], n2_ref[...],
                                wout_ref, w1_ref, w2_ref)


def _prompt_mix_ffn(x, gq, gk, gv, la, gr, sq, k2, v2, gnw, sinks, g1, sh2, sc2, g2, n2, wout, w1, w2):
    bsz, t, d = x.shape
    tb = TOKEN_BLOCK
    nt = t // tb
    nsteps = bsz * nt
    levels, tri, masks = _gla_constants(GLA_CHUNK, GLA_CHUNK)
    per_blk = tb // WINDOW

    def mix_blk(s):
        return jnp.minimum(s, nsteps - 1)

    def ffn_blk(s):
        return jnp.maximum(s - 1, 0)

    def tok(width):
        return pl.BlockSpec((tb, width), lambda s, _: (mix_blk(s), 0))

    prev = pl.BlockSpec((WINDOW, SWA_KV2), lambda s, _: (jnp.maximum(mix_blk(s) * per_blk - 1, 0), 0))
    xspec = pl.BlockSpec((1, tb, d), lambda s, _: (ffn_blk(s) // nt, ffn_blk(s) % nt, 0))
    modspec = pl.BlockSpec((1, 1, d), lambda s, _: (ffn_blk(s) // nt, 0, 0))
    return pl.pallas_call(
        functools.partial(_prompt_kernel, levels=levels, nt=nt, nsteps=nsteps),
        grid_spec=pltpu.PrefetchScalarGridSpec(
            num_scalar_prefetch=1,
            grid=(nsteps + 1,),
            in_specs=[tok(GLA_QK), tok(GLA_QK), tok(GLA_WIDTH), tok(GLA_QK), tok(GLA_WIDTH),
                      tok(SWA_Q), tok(SWA_KV2), tok(SWA_KV2), prev, prev,
                      _resident(tri.shape), _resident(masks.shape), _resident((1, GLA_DV)),
                      xspec, modspec, modspec, modspec, modspec, _resident((1, 1, d)),
                      _resident((d, d)), _resident((d, D_FF)), _resident((D_FF, d))],
            out_specs=[xspec,
                       pl.BlockSpec((1, GLA_DV, GLA_QK), lambda s, _: (mix_blk(s) // nt, 0, 0))],
            scratch_shapes=[pltpu.VMEM((GLA_DV, GLA_QK), f32),
                            pltpu.VMEM((2, tb, GLA_WIDTH + SWA_Q), bf16)]),
        out_shape=[jax.ShapeDtypeStruct((bsz, t, d), f32),
                   jax.ShapeDtypeStruct((bsz, GLA_DV, GLA_QK), f32)],
        compiler_params=pltpu.CompilerParams(
            dimension_semantics=("arbitrary",), vmem_limit_bytes=VMEM_LIMIT),
        name="prompt_mix_ffn",
    )(sinks, gq, gk, gv, la, gr, sq, k2, v2, k2, v2, tri, masks, gnw,
      x, g1, sh2, sc2, g2, n2, wout, w1, w2)


def _rope_tables(pos):
    half = HEAD_DIM // 2
    inv = jnp.power(ROPE_THETA, -jnp.arange(half, dtype=f32) * 2.0 / HEAD_DIM)
    ang = pos.astype(f32)[:, None] * inv[None, :]
    cos = jnp.cos(ang)
    sin = jnp.sin(ang)
    reps = LANES // HEAD_DIM
    return (jnp.tile(jnp.concatenate([cos, cos], axis=-1), (1, reps)),
            jnp.tile(jnp.concatenate([-sin, sin], axis=-1), (1, reps)))


def _block_diag_ones(n, blk):
    idx = np.arange(n) // blk
    return jnp.asarray((idx[:, None] == idx[None, :]).astype(np.float32), bf16)


def _layer_weights(w_in, w_gate_up, b_gate, q_norm_w, k_norm_w, w_out, w_ff1, w_ff2):
    splits = np.cumsum([GLA_QK, GLA_QK, GLA_WIDTH, GLA_WIDTH, GLA_GATE_RANK, SWA_Q, SWA_KV])
    gq, gk, gv, gr, glr, sq, sk, sv = jnp.split(w_in, [int(s) for s in splits], axis=1)
    glr = jnp.pad(glr, ((0, 0), (0, LANES - GLA_GATE_RANK)))
    win = jnp.concatenate([gq, gk, gv, gr, sq, sk, sv, glr], axis=1).astype(bf16)
    wgu = jnp.pad(w_gate_up, ((0, LANES - GLA_GATE_RANK), (0, 0))).astype(bf16)
    return dict(
        win=win, wgu=wgu, bg=b_gate.reshape(1, GLA_QK),
        qnw=jnp.tile(q_norm_w, N_Q_HEADS).reshape(1, SWA_Q),
        knw=jnp.tile(k_norm_w, N_KV_HEADS).reshape(1, SWA_KV),
        wout=w_out.astype(bf16), w1=w_ff1.astype(bf16), w2=w_ff2.astype(bf16))


def _decoder_layer(x, mod, pos, state, past_k, past_v, lw, n1, n2, gnw, sinks, bdq, bdk):
    bsz, t, d = x.shape
    sh1, sc1, g1, sh2, sc2, g2 = [mod[:, None, i * d:(i + 1) * d] for i in range(6)]
    cos_t, sin_t = _rope_tables(pos)
    prompt = state is None
    if prompt:
        nb, tb, tail, act = 1, TOKEN_BLOCK, WINDOW, bf16
    else:
        nb, tb, tail, act = SAMPLE_ROWS // t, t, SAMPLE_ROWS, f32
        cos_t = jnp.tile(cos_t, (nb, 1))
        sin_t = jnp.tile(sin_t, (nb, 1))
    n2 = n2.reshape(1, 1, d)
    gq, gk, gv, gr, la, sq, k2, v2, kk, vk = _inproj(
        x, sh1, sc1, n1.reshape(1, 1, d), lw["win"], lw["wgu"], lw["bg"], lw["qnw"], lw["knw"],
        bdq, bdk, cos_t, sin_t, nb, tb, tail, act)
    if prompt:
        y, s_t = _prompt_mix_ffn(x, gq, gk, gv, la, gr, sq, k2, v2, gnw, sinks, g1, sh2, sc2, g2, n2,
                                 lw["wout"], lw["w1"], lw["w2"])
        s_new = s_t.reshape(bsz, GLA_DV, GLA_HEADS, GLA_DK).transpose(0, 2, 3, 1)
        k_keep, v_keep = kk, vk
    else:
        go, s_new = _gla_sample(gq, gk, gv, la, gr, state.reshape(bsz, GLA_QK, GLA_DV), gnw, bsz, t)
        so, k_keep, v_keep = _swa_sample(
            sq, kk, vk, past_k.reshape(bsz, WINDOW, SWA_KV), past_v.reshape(bsz, WINDOW, SWA_KV),
            sinks, bsz, t)
        y = _out_ffn(x, go, so, g1, sh2, sc2, g2, n2, lw["wout"], lw["w1"], lw["w2"], nb, tb)
    return (y, s_new.reshape(bsz, GLA_HEADS, GLA_DK, GLA_DV),
            k_keep.reshape(bsz, WINDOW, N_KV_HEADS, HEAD_DIM),
            v_keep.reshape(bsz, WINDOW, N_KV_HEADS, HEAD_DIM))


def kernel(x_prompt, x_sample, state_gla, cache_swa_k, cache_swa_v, c_prompt, c_sample, w_ada, b_ada, norm1_w, norm2_w, w_in, w_gate_up, b_gate, gla_norm_w, q_norm_w, k_norm_w, sinks, w_out, w_ff1, w_ff2):
    depth = w_ada.shape[0]
    bp, tp, _ = x_prompt.shape
    bs, ts, _ = x_sample.shape
    pos_p = jnp.arange(tp)
    pos_s = PAST_LEN + jnp.arange(ts)
    bdq = _block_diag_ones(SWA_Q, HEAD_DIM)
    bdk = _block_diag_ones(SWA_KV, HEAD_DIM)
    c_all = jnp.concatenate([c_prompt, c_sample], axis=0)
    yp, ys = x_prompt, x_sample
    outs = [[] for _ in range(6)]
    for l in range(depth):
        mod = _modulation(c_all, w_ada[l], b_ada[l])
        lw = _layer_weights(w_in[l], w_gate_up[l], b_gate[l], q_norm_w[l], k_norm_w[l],
                            w_out[l], w_ff1[l], w_ff2[l])
        gnw = gla_norm_w[l].reshape(1, GLA_DV)
        common = (lw, norm1_w[l], norm2_w[l], gnw, sinks[l], bdq, bdk)
        yp, gp, kp, vp = _decoder_layer(yp, mod[:bp], pos_p, None, None, None, *common)
        ys, gs, kq, vq = _decoder_layer(ys, mod[bp:], pos_s, state_gla[l], cache_swa_k[l],
                                        cache_swa_v[l], *common)
        for lst, val in zip(outs, (gp, kp, vp, gs, kq, vq)):
            lst.append(val)
    return (yp, ys) + tuple(jnp.stack(o) for o in outs)
```

```python
import functools

import jax
import jax.numpy as jnp
import numpy as np
from jax import lax
from jax.experimental import pallas as pl
from jax.experimental.pallas import tpu as pltpu

f32 = jnp.float32
bf16 = jnp.bfloat16

D_MODEL = 1024
GLA_HEADS = 4
GLA_DK = 64
GLA_DV = 128
GLA_QK = GLA_HEADS * GLA_DK
GLA_WIDTH = GLA_HEADS * GLA_DV
GLA_GATE_RANK = 16
GLA_TAU = 16.0
LOG2E = 1.4426950408889634
HEAD_DIM = 64
N_Q_HEADS = 8
N_KV_HEADS = 2
SWA_Q = N_Q_HEADS * HEAD_DIM
SWA_KV = N_KV_HEADS * HEAD_DIM
SWA_KV2 = 2 * SWA_KV
WINDOW = 128
ROPE_THETA = 10000.0
PAST_LEN = 8192
D_FF = 4 * D_MODEL
EPS = 1e-6
LANES = 128
GLA_CHUNK = 128
TOKEN_BLOCK = 512
SAMPLE_ROWS = 256
PROMPT_INPROJ_GROUPS = 2
VMEM_LIMIT = 56 * 1024 * 1024

_SEG = {}
_off = 0
for _name, _w in (("gq", GLA_QK), ("gk", GLA_QK), ("gv", GLA_WIDTH), ("gr", GLA_WIDTH),
                  ("sq", SWA_Q), ("sk", SWA_KV), ("sv", SWA_KV), ("glr", LANES)):
    _SEG[_name] = (_off, _off + _w)
    _off += _w
IN_WIDTH_PADDED = _off


def _dot(a, b):
    return jnp.dot(a, b, preferred_element_type=f32)


def _dot_nt(a, b):
    return lax.dot_general(a, b, (((1,), (1,)), ((), ())), preferred_element_type=f32)


def _dot_tn(a, b):
    return lax.dot_general(a, b, (((0,), (0,)), ((), ())), preferred_element_type=f32)


def _sigmoid(x):
    return 1.0 / (1.0 + jnp.exp(-x))


def _mod_kernel(c_ref, w_ref, b_ref, o_ref):
    c = c_ref[...]
    s = (c * _sigmoid(c)).astype(bf16)
    o_ref[...] = _dot(s, w_ref[...].astype(bf16)) + b_ref[...]


def _modulation(c_all, w_ada, b_ada):
    m = c_all.shape[0]
    n = w_ada.shape[1]
    bn = 1536
    return pl.pallas_call(
        _mod_kernel,
        grid=(n // bn,),
        in_specs=[pl.BlockSpec((m, D_MODEL), lambda j: (0, 0)),
                  pl.BlockSpec((D_MODEL, bn), lambda j: (0, j)),
                  pl.BlockSpec((1, bn), lambda j: (0, j))],
        out_specs=pl.BlockSpec((m, bn), lambda j: (0, j)),
        out_shape=jax.ShapeDtypeStruct((m, n), f32),
        compiler_params=pltpu.CompilerParams(vmem_limit_bytes=VMEM_LIMIT),
        name="adaln_mod",
    )(c_all, w_ada, b_ada.reshape(1, n))


def _group_rms(x, bd_ref, w_ref):
    ssq = _dot((x * x).astype(bf16), bd_ref[...])
    return x * lax.rsqrt(ssq * (1.0 / HEAD_DIM) + EPS) * w_ref[...]


def _rope(x, cos, sin_signed, low_half):
    partner = jnp.where(low_half, pltpu.roll(x, LANES - 32, axis=1), pltpu.roll(x, 32, axis=1))
    return x * cos + partner * sin_signed


def _dup_heads(x, low_lanes):
    rolled = pltpu.roll(x, HEAD_DIM, axis=1)
    return jnp.where(low_lanes, x, rolled), jnp.where(low_lanes, rolled, x)


def _inproj_kernel(x_ref, sh_ref, sc_ref, n1_ref, win_ref, wgu_ref, bg_ref, qnw_ref, knw_ref,
                   bdq_ref, bdk_ref, cos_ref, sin_ref,
                   gq_ref, gk_ref, gv_ref, gr_ref, la_ref, sq_ref, k2_ref, v2_ref, kk_ref, vk_ref,
                   *, groups):
    nb, t, d = x_ref.shape
    m = nb * t
    tail = kk_ref.shape[0]
    mg = m // groups
    lane = lax.broadcasted_iota(jnp.int32, (mg, LANES), 1)
    low_half = (lane & 32) == 0
    low_lanes = lane < HEAD_DIM
    for grp in range(groups):
        rows = slice(grp * mg, (grp + 1) * mg)
        if nb == 1:
            x = x_ref[:, rows, :]
            sc, sh = sc_ref[...], sh_ref[...]
        else:
            seqs = slice(grp * (nb // groups), (grp + 1) * (nb // groups))
            x = x_ref[seqs]
            sc, sh = sc_ref[seqs], sh_ref[seqs]
        ms = jnp.mean(x * x, axis=-1, keepdims=True)
        hn = x * lax.rsqrt(ms + EPS) * n1_ref[...]
        hn = hn * (1.0 + sc) + sh
        hb = hn.reshape(mg, d).astype(bf16)

        def seg(name):
            a, b = _SEG[name]
            return _dot(hb, win_ref[:, a:b])

        cos = cos_ref[rows, :]
        sin = sin_ref[rows, :]
        sq = _group_rms(seg("sq"), bdq_ref, qnw_ref)
        for c in range(SWA_Q // LANES):
            blk = _rope(sq[:, c * LANES:(c + 1) * LANES], cos, sin, low_half)
            sq_ref[rows, c * LANES:(c + 1) * LANES] = (blk * (HEAD_DIM ** -0.5)).astype(sq_ref.dtype)
        sk = _rope(_group_rms(seg("sk"), bdk_ref, knw_ref), cos, sin, low_half)
        sv = seg("sv")
        keep = (grp + 1) * mg - (m - tail)
        if keep > 0:
            keep = min(keep, mg)
            dst_rows = slice((grp + 1) * mg - keep - (m - tail), (grp + 1) * mg - (m - tail))
            kk_ref[dst_rows, :] = sk[mg - keep:]
            vk_ref[dst_rows, :] = sv[mg - keep:]
        for src, dst in ((sk, k2_ref), (sv, v2_ref)):
            d0, d1 = _dup_heads(src, low_lanes)
            dst[rows, :LANES] = d0.astype(dst.dtype)
            dst[rows, LANES:] = d1.astype(dst.dtype)
        glr = seg("glr").astype(bf16)
        g = _dot(glr, wgu_ref[...]) + bg_ref[...]
        log_sig = jnp.minimum(g, 0.0) - jnp.log1p(jnp.exp(-jnp.abs(g)))
        la_ref[rows, :] = log_sig * (LOG2E / GLA_TAU)
        gq_ref[rows, :] = seg("gq") * (GLA_DK ** -0.5)
        gk_ref[rows, :] = seg("gk")
        gv_ref[rows, :] = seg("gv").astype(gv_ref.dtype)
        gr_ref[rows, :] = seg("gr").astype(gr_ref.dtype)


def _inproj(x, sh, sc, n1, win, wgu, bg, qnw, knw, bdq, bdk, cos_t, sin_t, nb, tb, tail, act, groups):
    bsz, t, d = x.shape
    m = nb * tb
    nt = t // tb
    grid = (bsz // nb, nt)
    tok = bsz * t

    def full(shape):
        return pl.BlockSpec(shape, lambda i, j: (0,) * len(shape))

    def out(width):
        return pl.BlockSpec((m, width), lambda i, j: (i * nt + j, 0))

    tail_spec = pl.BlockSpec((tail, SWA_KV), lambda i, j: (i, 0))
    outs = ((GLA_QK, f32), (GLA_QK, f32), (GLA_WIDTH, act), (GLA_WIDTH, act), (GLA_QK, f32),
            (SWA_Q, act), (SWA_KV2, bf16), (SWA_KV2, bf16))
    ntail = (bsz // nb) * tail
    return pl.pallas_call(
        functools.partial(_inproj_kernel, groups=groups),
        grid=grid,
        in_specs=[pl.BlockSpec((nb, tb, d), lambda i, j: (i, j, 0)),
                  pl.BlockSpec((nb, 1, d), lambda i, j: (i, 0, 0)),
                  pl.BlockSpec((nb, 1, d), lambda i, j: (i, 0, 0)),
                  full((1, 1, d)),
                  full((d, IN_WIDTH_PADDED)),
                  full((LANES, GLA_QK)),
                  full((1, GLA_QK)),
                  full((1, SWA_Q)),
                  full((1, SWA_KV)),
                  full((SWA_Q, SWA_Q)),
                  full((SWA_KV, SWA_KV)),
                  pl.BlockSpec((m, LANES), lambda i, j: (j, 0)),
                  pl.BlockSpec((m, LANES), lambda i, j: (j, 0))],
        out_specs=[out(w) for w, _ in outs] + [tail_spec, tail_spec],
        out_shape=[jax.ShapeDtypeStruct((tok, w), dt) for w, dt in outs]
        + [jax.ShapeDtypeStruct((ntail, SWA_KV), f32)] * 2,
        compiler_params=pltpu.CompilerParams(
            dimension_semantics=("parallel", "arbitrary"), vmem_limit_bytes=VMEM_LIMIT),
        name="inproj",
    )(x, sh, sc, n1, win, wgu, bg, qnw, knw, bdq, bdk, cos_t, sin_t)


def _gla_constants(chunk, seq):
    t = np.arange(chunk)
    levels = []
    m = 1
    while m < seq:
        levels.append(m)
        m *= 2
    masks = [np.eye(chunk, dtype=bool)]
    for m in levels:
        upper = (t % (2 * m) >= m)[:, None]
        lower = (t % (2 * m) < m)[None, :]
        same = (t[:, None] // (2 * m)) == (t[None, :] // (2 * m))
        masks.append(same & upper & lower)
    tri = t[None, :] <= t[:, None]
    return (tuple(levels), jnp.asarray(tri.astype(np.float32), bf16),
            jnp.asarray(np.stack(masks).astype(np.float32)))


def _stack_heads(xb, lane_head, axis):
    zero = jnp.zeros_like(xb)
    return jnp.concatenate([jnp.where(lane_head == h, xb, zero) for h in range(GLA_HEADS)], axis=axis)


def _block_sums(la, bcum, m, row, rolls):
    c, n = la.shape
    if m == 1:
        return la, None
    if m < 8:
        def rolled(shift):
            if shift not in rolls:
                rolls[shift] = pltpu.roll(la, shift % c, axis=0)
            return rolls[shift]
        pos = row & (m - 1)
        pre = la
        suf = None
        for j in range(1, m):
            pre = pre + jnp.where(pos >= j, rolled(j), 0.0)
            term = jnp.where(pos < m - j, rolled(-j), 0.0)
            suf = term if suf is None else suf + term
        return pre, suf
    before, last = [], []
    for i in range(c // m):
        before.append(jnp.zeros((m, n), f32) if i == 0
                      else jnp.broadcast_to(bcum[i * m - 1:i * m], (m, n)))
        last.append(jnp.broadcast_to(bcum[(i + 1) * m - 1:(i + 1) * m], (m, n)))
    if len(before) == 1:
        return bcum - before[0], last[0] - bcum
    return bcum - jnp.concatenate(before, axis=0), jnp.concatenate(last, axis=0) - bcum


def _gla_scores(q_ref, k_ref, v_ref, la_ref, tri_ref, masks_ref, levels, seq):
    c = q_ref.shape[0]
    la = la_ref[...]
    hi = la.astype(bf16)
    lo = (la - hi.astype(f32)).astype(bf16)
    tri = tri_ref[...]
    bcum = _dot(tri, hi) + _dot(tri, lo)
    q = q_ref[...]
    k = k_ref[...]
    vb = v_ref[...].astype(bf16)
    row = lax.broadcasted_iota(jnp.int32, (c, GLA_QK), 0)
    lane_head = lax.broadcasted_iota(jnp.int32, (c, GLA_QK), 1) >> 6
    rolls = {}
    attn = None
    for lvl, m in enumerate((0,) + levels):
        if m == 0:
            qt, kt = q, k
        else:
            pre, suf = _block_sums(la, bcum, m, row, rolls)
            qt = q * jnp.exp2(pre)
            kt = k if suf is None else k * jnp.exp2(suf)
        r = _dot_nt(qt.astype(bf16), _stack_heads(kt.astype(bf16), lane_head, 0))
        mk = masks_ref[lvl]
        parts = [r[:, h * c:(h + 1) * c] * mk for h in range(GLA_HEADS)]
        attn = parts if attn is None else [a + p for a, p in zip(attn, parts)]
    a = jnp.concatenate(attn, axis=1).astype(bf16)
    pre, suf = _block_sums(la, bcum, seq, row, rolls)
    qf = (q * jnp.exp2(pre)).astype(bf16)
    kf = (k * jnp.exp2(suf)).astype(bf16)
    return a, qf, kf, vb, bcum, hi, lo


def _gla_values(a, vb):
    v_head = lax.broadcasted_iota(jnp.int32, vb.shape, 1) >> 7
    return _dot(a, _stack_heads(vb, v_head, 0))


def _gla_finish(o, r_ref, gnw, go_ref):
    for h in range(GLA_HEADS):
        sl = slice(h * GLA_DV, (h + 1) * GLA_DV)
        oh = o[:, sl]
        r = r_ref[:, sl].astype(f32)
        ms = jnp.mean(oh * oh, axis=-1, keepdims=True)
        go_ref[:, sl] = (oh * lax.rsqrt(ms + EPS) * gnw * (r * _sigmoid(r))).astype(go_ref.dtype)


def _gla_prompt_block(st, q_ref, k_ref, v_ref, la_ref, r_ref, tri_ref, masks_ref, gnw, go_ref, levels):
    c = GLA_CHUNK
    nchunk = q_ref.shape[0] // c
    lane_head = lax.broadcasted_iota(jnp.int32, (GLA_DV, GLA_QK), 1) >> 6
    chunks = []
    for ci in range(nchunk):
        rows = pl.ds(ci * c, c)
        chunks.append(_gla_scores(q_ref.at[rows], k_ref.at[rows], v_ref.at[rows], la_ref.at[rows],
                                  tri_ref, masks_ref, levels, c))
    intra, upds, decays = [], [], []
    for a, _, kf, vb, bcum, _, _ in chunks:
        intra.append(_gla_values(a, vb))
        full = _dot_tn(vb, kf)
        upd = None
        for h in range(GLA_HEADS):
            term = jnp.where(lane_head == h, full[h * GLA_DV:(h + 1) * GLA_DV], 0.0)
            upd = term if upd is None else upd + term
        upds.append(upd)
        decays.append(jnp.exp2(bcum[c - 1:c]))
    for ci in range(nchunk):
        rows = pl.ds(ci * c, c)
        sbd_t = _stack_heads(st.astype(bf16), lane_head, 0)
        o = intra[ci] + _dot_nt(chunks[ci][1], sbd_t)
        _gla_finish(o, r_ref.at[rows], gnw, go_ref.at[rows])
        st = decays[ci] * st + upds[ci]
    return st


def _gla_sample_kernel(q_ref, k_ref, v_ref, la_ref, r_ref, s0_ref, tri_ref, masks_ref, gnw_ref,
                       go_ref, sout_ref, *, levels, seq):
    c = q_ref.shape[0]
    a, qf, kf, vb, _, hi, lo = _gla_scores(q_ref, k_ref, v_ref, la_ref, tri_ref, masks_ref, levels, seq)
    o_intra = _gla_values(a, vb)
    row_head = lax.broadcasted_iota(jnp.int32, (GLA_QK, GLA_DV), 0) >> 6
    ones = jnp.ones((seq, GLA_DV), bf16)
    inter = []
    for b in range(c // seq):
        rows = slice(b * seq, (b + 1) * seq)
        s_old = s0_ref[b]
        sbd = _stack_heads(s_old.astype(bf16), row_head, 1)
        inter.append(_dot(qf[rows], sbd))
        p = _dot_tn(kf[rows], vb[rows])
        upd = None
        for h in range(GLA_HEADS):
            term = jnp.where(row_head == h, p[:, h * GLA_DV:(h + 1) * GLA_DV], 0.0)
            upd = term if upd is None else upd + term
        total = _dot_tn(hi[rows], ones) + _dot_tn(lo[rows], ones)
        sout_ref[b] = jnp.exp2(total) * s_old + upd
    o = o_intra + jnp.concatenate(inter, axis=0)
    _gla_finish(o, r_ref, gnw_ref[...], go_ref)


def _gla_sample(gq, gk, gv, la, gr, state, gnw, bsz, t):
    c = GLA_CHUNK
    nseq = c // t
    levels, tri, masks = _gla_constants(c, t)

    def tok(width):
        return pl.BlockSpec((c, width), lambda i: (i, 0))

    return pl.pallas_call(
        functools.partial(_gla_sample_kernel, levels=levels, seq=t),
        grid=(bsz // nseq,),
        in_specs=[tok(GLA_QK), tok(GLA_QK), tok(GLA_WIDTH), tok(GLA_QK), tok(GLA_WIDTH),
                  pl.BlockSpec((nseq, GLA_QK, GLA_DV), lambda i: (i, 0, 0)),
                  pl.BlockSpec(tri.shape, lambda i: (0, 0)),
                  pl.BlockSpec(masks.shape, lambda i: (0, 0, 0)),
                  pl.BlockSpec((1, GLA_DV), lambda i: (0, 0))],
        out_specs=[tok(GLA_WIDTH),
                   pl.BlockSpec((nseq, GLA_QK, GLA_DV), lambda i: (i, 0, 0))],
        out_shape=[jax.ShapeDtypeStruct((bsz * t, GLA_WIDTH), f32),
                   jax.ShapeDtypeStruct((bsz, GLA_QK, GLA_DV), f32)],
        compiler_params=pltpu.CompilerParams(
            dimension_semantics=("parallel",), vmem_limit_bytes=VMEM_LIMIT),
        name="gla_sample",
    )(gq, gk, gv, la, gr, state, tri, masks, gnw)


def _sink_attention(q_ref, rows, kdups, vdups, mask, sink_ref, o_ref):
    n = len(kdups)
    heads_per_group = N_Q_HEADS // N_KV_HEADS
    low_q = lax.broadcasted_iota(jnp.int32, (rows, LANES), 1) < HEAD_DIM
    scores = []
    for i in range(n):
        per_group = []
        for g in range(N_KV_HEADS):
            stack = []
            for p in range(heads_per_group // 2):
                pair = g * (heads_per_group // 2) + p
                qp = q_ref[i * rows:(i + 1) * rows, pair * LANES:(pair + 1) * LANES].astype(bf16)
                zero = jnp.zeros_like(qp)
                stack.append(jnp.where(low_q, qp, zero))
                stack.append(jnp.where(low_q, zero, qp))
            per_group.append(_dot_nt(jnp.concatenate(stack, axis=0), kdups[i][g]))
        scores.append(per_group)
    probs, inv = [], []
    for head in range(N_Q_HEADS):
        g, hh = divmod(head, heads_per_group)
        parts = [scores[i][g][hh * rows:(hh + 1) * rows] for i in range(n)]
        s = parts[0] if n == 1 else jnp.concatenate(parts, axis=0)
        sink = sink_ref[head]
        s = jnp.where(mask, s, -1e30)
        mx = jnp.maximum(jnp.max(s, axis=-1, keepdims=True), sink)
        p = jnp.exp(s - mx)
        inv.append(1.0 / (jnp.sum(p, axis=-1, keepdims=True) + jnp.exp(sink - mx)))
        probs.append(p.astype(bf16))
    outs = []
    for i in range(n):
        per_group = []
        for g in range(N_KV_HEADS):
            p_i = jnp.concatenate([probs[g * heads_per_group + hh][i * rows:(i + 1) * rows]
                                   for hh in range(heads_per_group)], axis=0)
            per_group.append(_dot(p_i, vdups[i][g]))
        outs.append(per_group)
    low_all = lax.broadcasted_iota(jnp.int32, (n * rows, LANES), 1) < HEAD_DIM
    for pair in range(N_Q_HEADS // 2):
        halves = []
        for head in (2 * pair, 2 * pair + 1):
            g, hh = divmod(head, heads_per_group)
            parts = [outs[i][g][hh * rows:(hh + 1) * rows] for i in range(n)]
            o = parts[0] if n == 1 else jnp.concatenate(parts, axis=0)
            halves.append(o * inv[head])
        o_ref[:, pair * LANES:(pair + 1) * LANES] = jnp.where(low_all, halves[0], halves[1]).astype(o_ref.dtype)


def _swa_prompt_blocks(q_ref, k_before, k2_ref, v_before, v2_ref, first, sink_ref, o_ref):
    w = WINDOW
    nblk = q_ref.shape[0] // w

    def dup(before, ref):
        blocks = [before] + [ref[i * w:(i + 1) * w, :] for i in range(nblk)]
        return [[jnp.concatenate([blocks[i][:, g * LANES:(g + 1) * LANES],
                                  blocks[i + 1][:, g * LANES:(g + 1) * LANES]], axis=0)
                 for g in range(N_KV_HEADS)] for i in range(nblk)]

    row = lax.broadcasted_iota(jnp.int32, (nblk * w, 2 * w), 0)
    tk = lax.broadcasted_iota(jnp.int32, (nblk * w, 2 * w), 1)
    rel = tk - (row & (w - 1))
    first_key = jnp.where(row < w, jnp.where(first, w, 0), 0)
    mask = (rel > 0) & (rel <= w) & (tk >= first_key)
    _sink_attention(q_ref, w, dup(k_before, k2_ref), dup(v_before, v2_ref), mask, sink_ref, o_ref)


def _swa_sample_kernel(sink_ref, q_ref, kn_ref, vn_ref, pk_ref, pv_ref, o_ref, ko_ref, vo_ref, *, seq):
    nseq = pk_ref.shape[0]
    w = pk_ref.shape[1]
    nk = w + seq
    rows = lax.broadcasted_iota(jnp.int32, (nseq * seq, nk), 0) & (seq - 1)
    cols = lax.broadcasted_iota(jnp.int32, (nseq * seq, nk), 1)
    mask = (cols > rows) & (cols <= rows + w)
    low_k = lax.broadcasted_iota(jnp.int32, (nk, LANES), 1) < HEAD_DIM
    kdups, vdups = [], []
    for b in range(nseq):
        tok = pl.ds(b * seq, seq)
        kcat = jnp.concatenate([pk_ref[b], kn_ref[tok, :]], axis=0)
        vcat = jnp.concatenate([pv_ref[b], vn_ref[tok, :]], axis=0)
        ko_ref[b] = kcat[seq:]
        vo_ref[b] = vcat[seq:]
        kdups.append([d.astype(bf16) for d in _dup_heads(kcat, low_k)])
        vdups.append([d.astype(bf16) for d in _dup_heads(vcat, low_k)])
    _sink_attention(q_ref, seq, kdups, vdups, mask, sink_ref, o_ref)


def _swa_sample(sq, sk, sv, past_k, past_v, sinks, bsz, t):
    nseq = 16
    w = past_k.shape[1]
    tok = lambda width: pl.BlockSpec((nseq * t, width), lambda i, s: (i, 0))
    cache = pl.BlockSpec((nseq, w, SWA_KV), lambda i, s: (i, 0, 0))
    return pl.pallas_call(
        functools.partial(_swa_sample_kernel, seq=t),
        grid_spec=pltpu.PrefetchScalarGridSpec(
            num_scalar_prefetch=1,
            grid=(bsz // nseq,),
            in_specs=[tok(SWA_Q), tok(SWA_KV), tok(SWA_KV), cache, cache],
            out_specs=[tok(SWA_Q), cache, cache]),
        out_shape=[jax.ShapeDtypeStruct((bsz * t, SWA_Q), f32),
                   jax.ShapeDtypeStruct((bsz, w, SWA_KV), f32),
                   jax.ShapeDtypeStruct((bsz, w, SWA_KV), f32)],
        compiler_params=pltpu.CompilerParams(
            dimension_semantics=("parallel",), vmem_limit_bytes=VMEM_LIMIT),
        name="swa_sample",
    )(sinks, sq, sk, sv, past_k, past_v)


FFN_CHUNK = 1024


def _out_proj_norm(x, go, so, g1, sh2, sc2, n2, wout_ref):
    nb, t, d = x.shape
    mixed = _dot(go, wout_ref[:GLA_WIDTH, :]) + _dot(so, wout_ref[GLA_WIDTH:, :])
    h = x + g1 * mixed.reshape(nb, t, d)
    ms = jnp.mean(h * h, axis=-1, keepdims=True)
    hn = h * lax.rsqrt(ms + EPS) * n2
    hn = hn * (1.0 + sc2) + sh2
    return h, hn.reshape(nb * t, d).astype(bf16)


def _ffn_piece(hb, w1_ref, w2_ref, c):
    cols = slice(c * FFN_CHUNK, (c + 1) * FFN_CHUNK)
    a = jnp.maximum(_dot(hb, w1_ref[:, cols]), 0.0)
    return _dot((a * a).astype(bf16), w2_ref[cols, :])


def _out_ffn_block(x, go, so, g1, sh2, sc2, g2, n2, wout_ref, w1_ref, w2_ref):
    h, hb = _out_proj_norm(x, go, so, g1, sh2, sc2, n2, wout_ref)
    ff = _ffn_piece(hb, w1_ref, w2_ref, 0)
    for c in range(1, D_FF // FFN_CHUNK):
        ff = ff + _ffn_piece(hb, w1_ref, w2_ref, c)
    return h + g2 * ff.reshape(h.shape)


def _out_ffn_kernel(x_ref, go_ref, so_ref, g1_ref, sh2_ref, sc2_ref, g2_ref, n2_ref,
                    wout_ref, w1_ref, w2_ref, y_ref):
    y_ref[...] = _out_ffn_block(x_ref[...], go_ref[...].astype(bf16), so_ref[...].astype(bf16),
                                g1_ref[...], sh2_ref[...],
                                sc2_ref[...], g2_ref[...], n2_ref[...], wout_ref, w1_ref, w2_ref)


def _resident(shape):
    return pl.BlockSpec(shape, lambda *_: (0,) * len(shape), pipeline_mode=pl.Buffered(1))


def _out_ffn(x, go, so, g1, sh2, sc2, g2, n2, wout, w1, w2, nb, tb):
    bsz, t, d = x.shape
    m = nb * tb
    nt = t // tb
    modspec = pl.BlockSpec((nb, 1, d), lambda i, j: (i, 0, 0))
    xspec = pl.BlockSpec((nb, tb, d), lambda i, j: (i, j, 0))
    return pl.pallas_call(
        _out_ffn_kernel,
        grid=(bsz // nb, nt),
        in_specs=[xspec,
                  pl.BlockSpec((m, GLA_WIDTH), lambda i, j: (i * nt + j, 0)),
                  pl.BlockSpec((m, SWA_Q), lambda i, j: (i * nt + j, 0)),
                  modspec, modspec, modspec, modspec,
                  pl.BlockSpec((1, 1, d), lambda i, j: (0, 0, 0)),
                  _resident((d, d)), _resident((d, D_FF)), _resident((D_FF, d))],
        out_specs=xspec,
        out_shape=jax.ShapeDtypeStruct((bsz, t, d), f32),
        compiler_params=pltpu.CompilerParams(
            dimension_semantics=("parallel", "parallel"), vmem_limit_bytes=VMEM_LIMIT),
        name="out_ffn",
    )(x, go, so, g1, sh2, sc2, g2, n2, wout, w1, w2)


def _prompt_kernel(sink_ref,
                   q_ref, k_ref, v_ref, la_ref, r_ref, sq_ref, k2_ref, v2_ref, k2p_ref, v2p_ref,
                   tri_ref, masks_ref, gnw_ref,
                   x_ref, g1_ref, sh2_ref, sc2_ref, g2_ref, n2_ref, wout_ref, w1_ref, w2_ref,
                   y_ref, sout_ref, state_ref, mix_ref, *, levels, nt, nsteps):
    s = pl.program_id(0)
    slot = s % 2
    c = GLA_CHUNK
    nchunk = q_ref.shape[0] // c

    @pl.when(s == 0)
    def _():
        mix_ref[1] = jnp.zeros(mix_ref.shape[1:], mix_ref.dtype)
        state_ref[...] = jnp.zeros_like(state_ref)

    j = jnp.minimum(s, nsteps - 1) % nt
    st_in = state_ref[...]
    st = jnp.where(j == 0, 0.0, st_in)
    gnw = gnw_ref[...]
    out = mix_ref.at[slot]
    st = _gla_prompt_block(st, q_ref, k_ref, v_ref, la_ref, r_ref, tri_ref, masks_ref, gnw,
                           out.at[:, pl.ds(0, GLA_WIDTH)], levels)
    _swa_prompt_blocks(sq_ref, k2p_ref[...], k2_ref, v2p_ref[...], v2_ref, j == 0, sink_ref,
                       out.at[:, pl.ds(GLA_WIDTH, SWA_Q)])
    st = jnp.where(s < nsteps, st, st_in)
    state_ref[...] = st
    sout_ref[0] = st
    mix = mix_ref[1 - slot]
    y_ref[...] = _out_ffn_block(x_ref[...], mix[:, :GLA_WIDTH], mix[:, GLA_WIDTH:], g1_ref[...],
                                sh2_ref[...], sc2_ref[...], g2_ref[...], n2_ref[...],
                                wout_ref, w1_ref, w2_ref)


def _prompt_mix_ffn(x, gq, gk, gv, la, gr, sq, k2, v2, gnw, sinks, g1, sh2, sc2, g2, n2, wout, w1, w2):
    bsz, t, d = x.shape
    tb = TOKEN_BLOCK
    nt = t // tb
    nsteps = bsz * nt
    levels, tri, masks = _gla_constants(GLA_CHUNK, GLA_CHUNK)
    per_blk = tb // WINDOW

    def mix_blk(s):
        return jnp.minimum(s, nsteps - 1)

    def ffn_blk(s):
        return jnp.maximum(s - 1, 0)

    def tok(width):
        return pl.BlockSpec((tb, width), lambda s, _: (mix_blk(s), 0))

    prev = pl.BlockSpec((WINDOW, SWA_KV2), lambda s, _: (jnp.maximum(mix_blk(s) * per_blk - 1, 0), 0))
    xspec = pl.BlockSpec((1, tb, d), lambda s, _: (ffn_blk(s) // nt, ffn_blk(s) % nt, 0))
    modspec = pl.BlockSpec((1, 1, d), lambda s, _: (ffn_blk(s) // nt, 0, 0))
    return pl.pallas_call(
        functools.partial(_prompt_kernel, levels=levels, nt=nt, nsteps=nsteps),
        grid_spec=pltpu.PrefetchScalarGridSpec(
            num_scalar_prefetch=1,
            grid=(nsteps + 1,),
            in_specs=[tok(GLA_QK), tok(GLA_QK), tok(GLA_WIDTH), tok(GLA_QK), tok(GLA_WIDTH),
                      tok(SWA_Q), tok(SWA_KV2), tok(SWA_KV2), prev, prev,
                      _resident(tri.shape), _resident(masks.shape), _resident((1, GLA_DV)),
                      xspec, modspec, modspec, modspec, modspec, _resident((1, 1, d)),
                      _resident((d, d)), _resident((d, D_FF)), _resident((D_FF, d))],
            out_specs=[xspec,
                       pl.BlockSpec((1, GLA_DV, GLA_QK), lambda s, _: (mix_blk(s) // nt, 0, 0))],
            scratch_shapes=[pltpu.VMEM((GLA_DV, GLA_QK), f32),
                            pltpu.VMEM((2, tb, GLA_WIDTH + SWA_Q), bf16)]),
        out_shape=[jax.ShapeDtypeStruct((bsz, t, d), f32),
                   jax.ShapeDtypeStruct((bsz, GLA_DV, GLA_QK), f32)],
        compiler_params=pltpu.CompilerParams(
            dimension_semantics=("arbitrary",), vmem_limit_bytes=VMEM_LIMIT),
        name="prompt_mix_ffn",
    )(sinks, gq, gk, gv, la, gr, sq, k2, v2, k2, v2, tri, masks, gnw,
      x, g1, sh2, sc2, g2, n2, wout, w1, w2)


def _rope_tables(pos):
    half = HEAD_DIM // 2
    inv = jnp.power(ROPE_THETA, -jnp.arange(half, dtype=f32) * 2.0 / HEAD_DIM)
    ang = pos.astype(f32)[:, None] * inv[None, :]
    cos = jnp.cos(ang)
    sin = jnp.sin(ang)
    reps = LANES // HEAD_DIM
    return (jnp.tile(jnp.concatenate([cos, cos], axis=-1), (1, reps)),
            jnp.tile(jnp.concatenate([-sin, sin], axis=-1), (1, reps)))


def _block_diag_ones(n, blk):
    idx = np.arange(n) // blk
    return jnp.asarray((idx[:, None] == idx[None, :]).astype(np.float32), bf16)


def _layer_weights(w_in, w_gate_up, b_gate, q_norm_w, k_norm_w, w_out, w_ff1, w_ff2):
    splits = np.cumsum([GLA_QK, GLA_QK, GLA_WIDTH, GLA_WIDTH, GLA_GATE_RANK, SWA_Q, SWA_KV])
    gq, gk, gv, gr, glr, sq, sk, sv = jnp.split(w_in, [int(s) for s in splits], axis=1)
    glr = jnp.pad(glr, ((0, 0), (0, LANES - GLA_GATE_RANK)))
    win = jnp.concatenate([gq, gk, gv, gr, sq, sk, sv, glr], axis=1).astype(bf16)
    wgu = jnp.pad(w_gate_up, ((0, LANES - GLA_GATE_RANK), (0, 0))).astype(bf16)
    return dict(
        win=win, wgu=wgu, bg=b_gate.reshape(1, GLA_QK),
        qnw=jnp.tile(q_norm_w, N_Q_HEADS).reshape(1, SWA_Q),
        knw=jnp.tile(k_norm_w, N_KV_HEADS).reshape(1, SWA_KV),
        wout=w_out.astype(bf16), w1=w_ff1.astype(bf16), w2=w_ff2.astype(bf16))


def _decoder_layer(x, mod, pos, state, past_k, past_v, lw, n1, n2, gnw, sinks, bdq, bdk):
    bsz, t, d = x.shape
    sh1, sc1, g1, sh2, sc2, g2 = [mod[:, None, i * d:(i + 1) * d] for i in range(6)]
    cos_t, sin_t = _rope_tables(pos)
    prompt = state is None
    if prompt:
        nb, tb, tail, act, groups = 1, TOKEN_BLOCK, WINDOW, bf16, PROMPT_INPROJ_GROUPS
    else:
        nb, tb, tail, act, groups = SAMPLE_ROWS // t, t, SAMPLE_ROWS, f32, 1
        cos_t = jnp.tile(cos_t, (nb, 1))
        sin_t = jnp.tile(sin_t, (nb, 1))
    n2 = n2.reshape(1, 1, d)
    gq, gk, gv, gr, la, sq, k2, v2, kk, vk = _inproj(
        x, sh1, sc1, n1.reshape(1, 1, d), lw["win"], lw["wgu"], lw["bg"], lw["qnw"], lw["knw"],
        bdq, bdk, cos_t, sin_t, nb, tb, tail, act, groups)
    if prompt:
        y, s_t = _prompt_mix_ffn(x, gq, gk, gv, la, gr, sq, k2, v2, gnw, sinks, g1, sh2, sc2, g2, n2,
                                 lw["wout"], lw["w1"], lw["w2"])
        s_new = s_t.reshape(bsz, GLA_DV, GLA_HEADS, GLA_DK).transpose(0, 2, 3, 1)
        k_keep, v_keep = kk, vk
    else:
        go, s_new = _gla_sample(gq, gk, gv, la, gr, state.reshape(bsz, GLA_QK, GLA_DV), gnw, bsz, t)
        so, k_keep, v_keep = _swa_sample(
            sq, kk, vk, past_k.reshape(bsz, WINDOW, SWA_KV), past_v.reshape(bsz, WINDOW, SWA_KV),
            sinks, bsz, t)
        y = _out_ffn(x, go, so, g1, sh2, sc2, g2, n2, lw["wout"], lw["w1"], lw["w2"], nb, tb)
    return (y, s_new.reshape(bsz, GLA_HEADS, GLA_DK, GLA_DV),
            k_keep.reshape(bsz, WINDOW, N_KV_HEADS, HEAD_DIM),
            v_keep.reshape(bsz, WINDOW, N_KV_HEADS, HEAD_DIM))


def kernel(x_prompt, x_sample, state_gla, cache_swa_k, cache_swa_v, c_prompt, c_sample, w_ada, b_ada, norm1_w, norm2_w, w_in, w_gate_up, b_gate, gla_norm_w, q_norm_w, k_norm_w, sinks, w_out, w_ff1, w_ff2):
    depth = w_ada.shape[0]
    bp, tp, _ = x_prompt.shape
    bs, ts, _ = x_sample.shape
    pos_p = jnp.arange(tp)
    pos_s = PAST_LEN + jnp.arange(ts)
    bdq = _block_diag_ones(SWA_Q, HEAD_DIM)
    bdk = _block_diag_ones(SWA_KV, HEAD_DIM)
    c_all = jnp.concatenate([c_prompt, c_sample], axis=0)
    yp, ys = x_prompt, x_sample
    outs = [[] for _ in range(6)]
    for l in range(depth):
        mod = _modulation(c_all, w_ada[l], b_ada[l])
        lw = _layer_weights(w_in[l], w_gate_up[l], b_gate[l], q_norm_w[l], k_norm_w[l],
                            w_out[l], w_ff1[l], w_ff2[l])
        gnw = gla_norm_w[l].reshape(1, GLA_DV)
        common = (lw, norm1_w[l], norm2_w[l], gnw, sinks[l], bdq, bdk)
        yp, gp, kp, vp = _decoder_layer(yp, mod[:bp], pos_p, None, None, None, *common)
        ys, gs, kq, vq = _decoder_layer(ys, mod[bp:], pos_s, state_gla[l], cache_swa_k[l],
                                        cache_swa_v[l], *common)
        for lst, val in zip(outs, (gp, kp, vp, gs, kq, vq)):
            lst.append(val)
    return (yp, ys) + tuple(jnp.stack(o) for o in outs)
```

```python
import functools

import jax
import jax.numpy as jnp
import numpy as np
from jax import lax
from jax.experimental import pallas as pl
from jax.experimental.pallas import tpu as pltpu

f32 = jnp.float32
bf16 = jnp.bfloat16

D_MODEL = 1024
GLA_HEADS = 4
GLA_DK = 64
GLA_DV = 128
GLA_QK = GLA_HEADS * GLA_DK
GLA_WIDTH = GLA_HEADS * GLA_DV
GLA_GATE_RANK = 16
GLA_TAU = 16.0
LOG2E = 1.4426950408889634
HEAD_DIM = 64
N_Q_HEADS = 8
N_KV_HEADS = 2
SWA_Q = N_Q_HEADS * HEAD_DIM
SWA_KV = N_KV_HEADS * HEAD_DIM
SWA_KV2 = 2 * SWA_KV
WINDOW = 128
ROPE_THETA = 10000.0
PAST_LEN = 8192
D_FF = 4 * D_MODEL
EPS = 1e-6
LANES = 128
GLA_CHUNK = 128
TOKEN_BLOCK = 512
SAMPLE_ROWS = 256
PROMPT_INPROJ_GROUPS = 2
VMEM_LIMIT = 56 * 1024 * 1024

_SEG = {}
_off = 0
for _name, _w in (("gq", GLA_QK), ("gk", GLA_QK), ("gv", GLA_WIDTH), ("gr", GLA_WIDTH),
                  ("sq", SWA_Q), ("sk", SWA_KV), ("sv", SWA_KV), ("glr", LANES)):
    _SEG[_name] = (_off, _off + _w)
    _off += _w
IN_WIDTH_PADDED = _off


def _dot(a, b):
    return jnp.dot(a, b, preferred_element_type=f32)


def _dot_nt(a, b):
    return lax.dot_general(a, b, (((1,), (1,)), ((), ())), preferred_element_type=f32)


def _dot_tn(a, b):
    return lax.dot_general(a, b, (((0,), (0,)), ((), ())), preferred_element_type=f32)


def _sigmoid(x):
    return 1.0 / (1.0 + jnp.exp(-x))


def _mod_kernel(c_ref, w_ref, b_ref, o_ref):
    c = c_ref[...]
    s = (c * _sigmoid(c)).astype(bf16)
    res = _dot(s, w_ref[...].astype(bf16)) + b_ref[...]
    for r in range(res.shape[0]):
        o_ref[r] = res[r:r + 1, :]


def _modulation(c_all, w_ada, b_ada):
    m = c_all.shape[0]
    n = w_ada.shape[1]
    bn = 1536
    return pl.pallas_call(
        _mod_kernel,
        grid=(n // bn,),
        in_specs=[pl.BlockSpec((m, D_MODEL), lambda j: (0, 0)),
                  pl.BlockSpec((D_MODEL, bn), lambda j: (0, j)),
                  pl.BlockSpec((1, bn), lambda j: (0, j))],
        out_specs=pl.BlockSpec((m, 1, bn), lambda j: (0, 0, j)),
        out_shape=jax.ShapeDtypeStruct((m, 1, n), f32),
        compiler_params=pltpu.CompilerParams(vmem_limit_bytes=VMEM_LIMIT),
        name="adaln_mod",
    )(c_all, w_ada, b_ada.reshape(1, n))


def _group_rms(x, bd_ref, w_ref):
    ssq = _dot((x * x).astype(bf16), bd_ref[...])
    return x * lax.rsqrt(ssq * (1.0 / HEAD_DIM) + EPS) * w_ref[...]


def _rope(x, cos, sin_signed, low_half):
    partner = jnp.where(low_half, pltpu.roll(x, LANES - 32, axis=1), pltpu.roll(x, 32, axis=1))
    return x * cos + partner * sin_signed


def _dup_heads(x, low_lanes):
    rolled = pltpu.roll(x, HEAD_DIM, axis=1)
    return jnp.where(low_lanes, x, rolled), jnp.where(low_lanes, rolled, x)


def _inproj_kernel(x_ref, sh_ref, sc_ref, n1_ref, win_ref, wgu_ref, bg_ref, qnw_ref, knw_ref,
                   bdq_ref, bdk_ref, cos_ref, sin_ref,
                   gq_ref, gk_ref, gv_ref, gr_ref, la_ref, sq_ref, k2_ref, v2_ref, kk_ref, vk_ref,
                   *, groups):
    nb, t, d = x_ref.shape
    m = nb * t
    tail = kk_ref.shape[0]
    mg = m // groups
    lane = lax.broadcasted_iota(jnp.int32, (mg, LANES), 1)
    low_half = (lane & 32) == 0
    low_lanes = lane < HEAD_DIM
    for grp in range(groups):
        rows = slice(grp * mg, (grp + 1) * mg)
        if nb == 1:
            x = x_ref[:, rows, :]
            sc, sh = sc_ref[...], sh_ref[...]
        else:
            seqs = slice(grp * (nb // groups), (grp + 1) * (nb // groups))
            x = x_ref[seqs]
            sc, sh = sc_ref[seqs], sh_ref[seqs]
        ms = jnp.mean(x * x, axis=-1, keepdims=True)
        hn = x * lax.rsqrt(ms + EPS) * n1_ref[...]
        hn = hn * (1.0 + sc) + sh
        hb = hn.reshape(mg, d).astype(bf16)

        def seg(name):
            a, b = _SEG[name]
            return _dot(hb, win_ref[:, a:b])

        cos = cos_ref[rows, :]
        sin = sin_ref[rows, :]
        sq = _group_rms(seg("sq"), bdq_ref, qnw_ref)
        for c in range(SWA_Q // LANES):
            blk = _rope(sq[:, c * LANES:(c + 1) * LANES], cos, sin, low_half)
            sq_ref[rows, c * LANES:(c + 1) * LANES] = (blk * (HEAD_DIM ** -0.5)).astype(sq_ref.dtype)
        sk = _rope(_group_rms(seg("sk"), bdk_ref, knw_ref), cos, sin, low_half)
        sv = seg("sv")
        keep = (grp + 1) * mg - (m - tail)
        if keep > 0:
            keep = min(keep, mg)
            dst_rows = slice((grp + 1) * mg - keep - (m - tail), (grp + 1) * mg - (m - tail))
            kk_ref[dst_rows, :] = sk[mg - keep:]
            vk_ref[dst_rows, :] = sv[mg - keep:]
        for src, dst in ((sk, k2_ref), (sv, v2_ref)):
            d0, d1 = _dup_heads(src, low_lanes)
            dst[rows, :LANES] = d0.astype(dst.dtype)
            dst[rows, LANES:] = d1.astype(dst.dtype)
        glr = seg("glr").astype(bf16)
        g = _dot(glr, wgu_ref[...]) + bg_ref[...]
        log_sig = jnp.minimum(g, 0.0) - jnp.log1p(jnp.exp(-jnp.abs(g)))
        la_ref[rows, :] = log_sig * (LOG2E / GLA_TAU)
        gq_ref[rows, :] = seg("gq") * (GLA_DK ** -0.5)
        gk_ref[rows, :] = seg("gk")
        gv_ref[rows, :] = seg("gv").astype(gv_ref.dtype)
        gr_ref[rows, :] = seg("gr").astype(gr_ref.dtype)


MOD_SHIFT1, MOD_SCALE1, MOD_GATE1, MOD_SHIFT2, MOD_SCALE2, MOD_GATE2 = range(6)


def _mod_spec(nb, row0, chunk, batch_block):
    return pl.BlockSpec((nb, 1, D_MODEL), lambda *ids: (row0 // nb + batch_block(*ids), 0, chunk))


def _inproj(x, mod, row0, n1, win, wgu, bg, qnw, knw, bdq, bdk, cos_t, sin_t, nb, tb, tail, act, groups):
    bsz, t, d = x.shape
    m = nb * tb
    nt = t // tb
    grid = (bsz // nb, nt)
    tok = bsz * t

    def full(shape):
        return pl.BlockSpec(shape, lambda i, j: (0,) * len(shape))

    def out(width):
        return pl.BlockSpec((m, width), lambda i, j: (i * nt + j, 0))

    tail_spec = pl.BlockSpec((tail, SWA_KV), lambda i, j: (i, 0))
    outs = ((GLA_QK, f32), (GLA_QK, f32), (GLA_WIDTH, act), (GLA_WIDTH, act), (GLA_QK, f32),
            (SWA_Q, act), (SWA_KV2, bf16), (SWA_KV2, bf16))
    ntail = (bsz // nb) * tail
    return pl.pallas_call(
        functools.partial(_inproj_kernel, groups=groups),
        grid=grid,
        in_specs=[pl.BlockSpec((nb, tb, d), lambda i, j: (i, j, 0)),
                  _mod_spec(nb, row0, MOD_SHIFT1, lambda i, j: i),
                  _mod_spec(nb, row0, MOD_SCALE1, lambda i, j: i),
                  full((1, 1, d)),
                  full((d, IN_WIDTH_PADDED)),
                  full((LANES, GLA_QK)),
                  full((1, GLA_QK)),
                  full((1, SWA_Q)),
                  full((1, SWA_KV)),
                  full((SWA_Q, SWA_Q)),
                  full((SWA_KV, SWA_KV)),
                  pl.BlockSpec((m, LANES), lambda i, j: (j, 0)),
                  pl.BlockSpec((m, LANES), lambda i, j: (j, 0))],
        out_specs=[out(w) for w, _ in outs] + [tail_spec, tail_spec],
        out_shape=[jax.ShapeDtypeStruct((tok, w), dt) for w, dt in outs]
        + [jax.ShapeDtypeStruct((ntail, SWA_KV), f32)] * 2,
        compiler_params=pltpu.CompilerParams(
            dimension_semantics=("parallel", "arbitrary"), vmem_limit_bytes=VMEM_LIMIT),
        name="inproj",
    )(x, mod, mod, n1, win, wgu, bg, qnw, knw, bdq, bdk, cos_t, sin_t)


def _gla_constants(chunk, seq):
    t = np.arange(chunk)
    levels = []
    m = 1
    while m < seq:
        levels.append(m)
        m *= 2
    masks = [np.eye(chunk, dtype=bool)]
    for m in levels:
        upper = (t % (2 * m) >= m)[:, None]
        lower = (t % (2 * m) < m)[None, :]
        same = (t[:, None] // (2 * m)) == (t[None, :] // (2 * m))
        masks.append(same & upper & lower)
    tri = t[None, :] <= t[:, None]
    return (tuple(levels), jnp.asarray(tri.astype(np.float32), bf16),
            jnp.asarray(np.stack(masks).astype(np.float32)))


def _stack_heads(xb, lane_head, axis):
    zero = jnp.zeros_like(xb)
    return jnp.concatenate([jnp.where(lane_head == h, xb, zero) for h in range(GLA_HEADS)], axis=axis)


def _block_sums(la, bcum, m, row, rolls):
    c, n = la.shape
    if m == 1:
        return la, None
    if m < 8:
        def rolled(shift):
            if shift not in rolls:
                rolls[shift] = pltpu.roll(la, shift % c, axis=0)
            return rolls[shift]
        pos = row & (m - 1)
        pre = la
        suf = None
        for j in range(1, m):
            pre = pre + jnp.where(pos >= j, rolled(j), 0.0)
            term = jnp.where(pos < m - j, rolled(-j), 0.0)
            suf = term if suf is None else suf + term
        return pre, suf
    before, last = [], []
    for i in range(c // m):
        before.append(jnp.zeros((m, n), f32) if i == 0
                      else jnp.broadcast_to(bcum[i * m - 1:i * m], (m, n)))
        last.append(jnp.broadcast_to(bcum[(i + 1) * m - 1:(i + 1) * m], (m, n)))
    if len(before) == 1:
        return bcum - before[0], last[0] - bcum
    return bcum - jnp.concatenate(before, axis=0), jnp.concatenate(last, axis=0) - bcum


def _gla_scores(q_ref, k_ref, v_ref, la_ref, tri_ref, masks_ref, levels, seq):
    c = q_ref.shape[0]
    la = la_ref[...]
    hi = la.astype(bf16)
    lo = (la - hi.astype(f32)).astype(bf16)
    tri = tri_ref[...]
    bcum = _dot(tri, hi) + _dot(tri, lo)
    q = q_ref[...]
    k = k_ref[...]
    vb = v_ref[...].astype(bf16)
    row = lax.broadcasted_iota(jnp.int32, (c, GLA_QK), 0)
    lane_head = lax.broadcasted_iota(jnp.int32, (c, GLA_QK), 1) >> 6
    rolls = {}
    attn = None
    for lvl, m in enumerate((0,) + levels):
        if m == 0:
            qt, kt = q, k
        else:
            pre, suf = _block_sums(la, bcum, m, row, rolls)
            qt = q * jnp.exp2(pre)
            kt = k if suf is None else k * jnp.exp2(suf)
        r = _dot_nt(qt.astype(bf16), _stack_heads(kt.astype(bf16), lane_head, 0))
        mk = masks_ref[lvl]
        parts = [r[:, h * c:(h + 1) * c] * mk for h in range(GLA_HEADS)]
        attn = parts if attn is None else [a + p for a, p in zip(attn, parts)]
    a = jnp.concatenate(attn, axis=1).astype(bf16)
    pre, suf = _block_sums(la, bcum, seq, row, rolls)
    qf = (q * jnp.exp2(pre)).astype(bf16)
    kf = (k * jnp.exp2(suf)).astype(bf16)
    return a, qf, kf, vb, bcum, hi, lo


def _gla_values(a, vb):
    v_head = lax.broadcasted_iota(jnp.int32, vb.shape, 1) >> 7
    return _dot(a, _stack_heads(vb, v_head, 0))


def _gla_finish(o, r_ref, gnw, go_ref):
    for h in range(GLA_HEADS):
        sl = slice(h * GLA_DV, (h + 1) * GLA_DV)
        oh = o[:, sl]
        r = r_ref[:, sl].astype(f32)
        ms = jnp.mean(oh * oh, axis=-1, keepdims=True)
        go_ref[:, sl] = (oh * lax.rsqrt(ms + EPS) * gnw * (r * _sigmoid(r))).astype(go_ref.dtype)


def _gla_prompt_block(st, q_ref, k_ref, v_ref, la_ref, r_ref, tri_ref, masks_ref, gnw, go_ref, levels):
    c = GLA_CHUNK
    nchunk = q_ref.shape[0] // c
    lane_head = lax.broadcasted_iota(jnp.int32, (GLA_DV, GLA_QK), 1) >> 6
    chunks = []
    for ci in range(nchunk):
        rows = pl.ds(ci * c, c)
        chunks.append(_gla_scores(q_ref.at[rows], k_ref.at[rows], v_ref.at[rows], la_ref.at[rows],
                                  tri_ref, masks_ref, levels, c))
    intra, upds, decays = [], [], []
    for a, _, kf, vb, bcum, _, _ in chunks:
        intra.append(_gla_values(a, vb))
        full = _dot_tn(vb, kf)
        upd = None
        for h in range(GLA_HEADS):
            term = jnp.where(lane_head == h, full[h * GLA_DV:(h + 1) * GLA_DV], 0.0)
            upd = term if upd is None else upd + term
        upds.append(upd)
        decays.append(jnp.exp2(bcum[c - 1:c]))
    for ci in range(nchunk):
        rows = pl.ds(ci * c, c)
        sbd_t = _stack_heads(st.astype(bf16), lane_head, 0)
        o = intra[ci] + _dot_nt(chunks[ci][1], sbd_t)
        _gla_finish(o, r_ref.at[rows], gnw, go_ref.at[rows])
        st = decays[ci] * st + upds[ci]
    return st


def _gla_sample_kernel(q_ref, k_ref, v_ref, la_ref, r_ref, s0_ref, tri_ref, masks_ref, gnw_ref,
                       go_ref, sout_ref, *, levels, seq):
    c = q_ref.shape[0]
    a, qf, kf, vb, _, hi, lo = _gla_scores(q_ref, k_ref, v_ref, la_ref, tri_ref, masks_ref, levels, seq)
    o_intra = _gla_values(a, vb)
    row_head = lax.broadcasted_iota(jnp.int32, (GLA_QK, GLA_DV), 0) >> 6
    ones = jnp.ones((seq, GLA_DV), bf16)
    inter = []
    for b in range(c // seq):
        rows = slice(b * seq, (b + 1) * seq)
        s_old = s0_ref[b]
        sbd = _stack_heads(s_old.astype(bf16), row_head, 1)
        inter.append(_dot(qf[rows], sbd))
        p = _dot_tn(kf[rows], vb[rows])
        upd = None
        for h in range(GLA_HEADS):
            term = jnp.where(row_head == h, p[:, h * GLA_DV:(h + 1) * GLA_DV], 0.0)
            upd = term if upd is None else upd + term
        total = _dot_tn(hi[rows], ones) + _dot_tn(lo[rows], ones)
        sout_ref[b] = jnp.exp2(total) * s_old + upd
    o = o_intra + jnp.concatenate(inter, axis=0)
    _gla_finish(o, r_ref, gnw_ref[...], go_ref)


def _gla_sample(gq, gk, gv, la, gr, state, gnw, bsz, t):
    c = GLA_CHUNK
    nseq = c // t
    levels, tri, masks = _gla_constants(c, t)

    def tok(width):
        return pl.BlockSpec((c, width), lambda i: (i, 0))

    return pl.pallas_call(
        functools.partial(_gla_sample_kernel, levels=levels, seq=t),
        grid=(bsz // nseq,),
        in_specs=[tok(GLA_QK), tok(GLA_QK), tok(GLA_WIDTH), tok(GLA_QK), tok(GLA_WIDTH),
                  pl.BlockSpec((nseq, GLA_QK, GLA_DV), lambda i: (i, 0, 0)),
                  pl.BlockSpec(tri.shape, lambda i: (0, 0)),
                  pl.BlockSpec(masks.shape, lambda i: (0, 0, 0)),
                  pl.BlockSpec((1, GLA_DV), lambda i: (0, 0))],
        out_specs=[tok(GLA_WIDTH),
                   pl.BlockSpec((nseq, GLA_QK, GLA_DV), lambda i: (i, 0, 0))],
        out_shape=[jax.ShapeDtypeStruct((bsz * t, GLA_WIDTH), f32),
                   jax.ShapeDtypeStruct((bsz, GLA_QK, GLA_DV), f32)],
        compiler_params=pltpu.CompilerParams(
            dimension_semantics=("parallel",), vmem_limit_bytes=VMEM_LIMIT),
        name="gla_sample",
    )(gq, gk, gv, la, gr, state, tri, masks, gnw)


def _sink_attention(q_ref, rows, kdups, vdups, mask, sink_ref, o_ref):
    n = len(kdups)
    heads_per_group = N_Q_HEADS // N_KV_HEADS
    low_q = lax.broadcasted_iota(jnp.int32, (rows, LANES), 1) < HEAD_DIM
    scores = []
    for i in range(n):
        per_group = []
        for g in range(N_KV_HEADS):
            stack = []
            for p in range(heads_per_group // 2):
                pair = g * (heads_per_group // 2) + p
                qp = q_ref[i * rows:(i + 1) * rows, pair * LANES:(pair + 1) * LANES].astype(bf16)
                zero = jnp.zeros_like(qp)
                stack.append(jnp.where(low_q, qp, zero))
                stack.append(jnp.where(low_q, zero, qp))
            per_group.append(_dot_nt(jnp.concatenate(stack, axis=0), kdups[i][g]))
        scores.append(per_group)
    probs, inv = [], []
    for head in range(N_Q_HEADS):
        g, hh = divmod(head, heads_per_group)
        parts = [scores[i][g][hh * rows:(hh + 1) * rows] for i in range(n)]
        s = parts[0] if n == 1 else jnp.concatenate(parts, axis=0)
        sink = sink_ref[head]
        s = jnp.where(mask, s, -1e30)
        mx = jnp.maximum(jnp.max(s, axis=-1, keepdims=True), sink)
        p = jnp.exp(s - mx)
        inv.append(1.0 / (jnp.sum(p, axis=-1, keepdims=True) + jnp.exp(sink - mx)))
        probs.append(p.astype(bf16))
    outs = []
    for i in range(n):
        per_group = []
        for g in range(N_KV_HEADS):
            p_i = jnp.concatenate([probs[g * heads_per_group + hh][i * rows:(i + 1) * rows]
                                   for hh in range(heads_per_group)], axis=0)
            per_group.append(_dot(p_i, vdups[i][g]))
        outs.append(per_group)
    low_all = lax.broadcasted_iota(jnp.int32, (n * rows, LANES), 1) < HEAD_DIM
    for pair in range(N_Q_HEADS // 2):
        halves = []
        for head in (2 * pair, 2 * pair + 1):
            g, hh = divmod(head, heads_per_group)
            parts = [outs[i][g][hh * rows:(hh + 1) * rows] for i in range(n)]
            o = parts[0] if n == 1 else jnp.concatenate(parts, axis=0)
            halves.append(o * inv[head])
        o_ref[:, pair * LANES:(pair + 1) * LANES] = jnp.where(low_all, halves[0], halves[1]).astype(o_ref.dtype)


def _swa_prompt_blocks(q_ref, k_before, k2_ref, v_before, v2_ref, first, sink_ref, o_ref):
    w = WINDOW
    nblk = q_ref.shape[0] // w

    def dup(before, ref):
        blocks = [before] + [ref[i * w:(i + 1) * w, :] for i in range(nblk)]
        return [[jnp.concatenate([blocks[i][:, g * LANES:(g + 1) * LANES],
                                  blocks[i + 1][:, g * LANES:(g + 1) * LANES]], axis=0)
                 for g in range(N_KV_HEADS)] for i in range(nblk)]

    row = lax.broadcasted_iota(jnp.int32, (nblk * w, 2 * w), 0)
    tk = lax.broadcasted_iota(jnp.int32, (nblk * w, 2 * w), 1)
    rel = tk - (row & (w - 1))
    first_key = jnp.where(row < w, jnp.where(first, w, 0), 0)
    mask = (rel > 0) & (rel <= w) & (tk >= first_key)
    _sink_attention(q_ref, w, dup(k_before, k2_ref), dup(v_before, v2_ref), mask, sink_ref, o_ref)


def _swa_sample_kernel(sink_ref, q_ref, kn_ref, vn_ref, pk_ref, pv_ref, o_ref, ko_ref, vo_ref, *, seq):
    nseq, _, _, w = pk_ref.shape
    rows = nseq * seq
    heads_per_group = N_Q_HEADS // N_KV_HEADS
    lane = lax.broadcasted_iota(jnp.int32, (rows, LANES), 1)
    r_id = lax.broadcasted_iota(jnp.int32, (rows, LANES), 0)
    low = lane < HEAD_DIM
    pos = r_id & (seq - 1)
    mask_old = lane > pos
    seq_shift = seq.bit_length() - 1
    mask_new = ((r_id >> seq_shift) == (lane >> seq_shift)) & ((lane & (seq - 1)) <= pos)
    kn = kn_ref[...]
    vn = vn_ref[...]
    kn_dup = [d.astype(bf16) for d in _dup_heads(kn, low)]
    vn_dup = [d.astype(bf16) for d in _dup_heads(vn, low)]

    pad = jnp.zeros((w - seq, LANES), f32)
    tail_lanes = lax.broadcasted_iota(jnp.int32, (HEAD_DIM, w), 1) >= w - seq
    kt_dup, vt_dup = [], []
    for b in range(nseq):
        tok = slice(b * seq, (b + 1) * seq)
        per_k, per_v = [], []
        for new, old_ref, out_ref, per in ((kn, pk_ref, ko_ref, per_k), (vn, pv_ref, vo_ref, per_v)):
            new_t = jnp.concatenate([pad, new[tok]], axis=0).T
            for g in range(N_KV_HEADS):
                old = old_ref[b, g]
                out_ref[b, g] = jnp.where(tail_lanes, new_t[g * HEAD_DIM:(g + 1) * HEAD_DIM],
                                          pltpu.roll(old, w - seq, axis=1))
                ob = old.astype(bf16)
                per.append(jnp.concatenate([ob, ob], axis=0))
        kt_dup.append(per_k)
        vt_dup.append(per_v)

    lhs = []
    for g in range(N_KV_HEADS):
        stack = []
        for p in range(heads_per_group // 2):
            pair = g * (heads_per_group // 2) + p
            qp = q_ref[:, pair * LANES:(pair + 1) * LANES].astype(bf16)
            zero = jnp.zeros_like(qp)
            stack.append(jnp.where(low, qp, zero))
            stack.append(jnp.where(low, zero, qp))
        lhs.append(jnp.concatenate(stack, axis=0))
    s_new = [_dot_nt(lhs[g], kn_dup[g]) for g in range(N_KV_HEADS)]
    s_old = []
    for b in range(nseq):
        per = []
        for g in range(N_KV_HEADS):
            qb = jnp.concatenate([lhs[g][hh * rows + b * seq:hh * rows + (b + 1) * seq]
                                  for hh in range(heads_per_group)], axis=0)
            per.append(_dot(qb, kt_dup[b][g]))
        s_old.append(per)

    p_old, p_new, inv = [], [], []
    for head in range(N_Q_HEADS):
        g, hh = divmod(head, heads_per_group)
        so = jnp.concatenate([s_old[b][g][hh * seq:(hh + 1) * seq] for b in range(nseq)], axis=0)
        so = jnp.where(mask_old, so, -1e30)
        sn = jnp.where(mask_new, s_new[g][hh * rows:(hh + 1) * rows], -1e30)
        sink = sink_ref[head]
        mx = jnp.maximum(jnp.maximum(jnp.max(so, axis=-1, keepdims=True),
                                     jnp.max(sn, axis=-1, keepdims=True)), sink)
        po = jnp.exp(so - mx)
        pn = jnp.exp(sn - mx)
        inv.append(1.0 / (jnp.sum(po, axis=-1, keepdims=True) + jnp.sum(pn, axis=-1, keepdims=True)
                          + jnp.exp(sink - mx)))
        p_old.append(po.astype(bf16))
        p_new.append(pn.astype(bf16))

    o_new = [_dot(jnp.concatenate([p_new[g * heads_per_group + hh] for hh in range(heads_per_group)], axis=0),
                  vn_dup[g]) for g in range(N_KV_HEADS)]
    o_old = []
    for b in range(nseq):
        per = []
        for g in range(N_KV_HEADS):
            pb = jnp.concatenate([p_old[g * heads_per_group + hh][b * seq:(b + 1) * seq]
                                  for hh in range(heads_per_group)], axis=0)
            per.append(_dot_nt(pb, vt_dup[b][g]))
        o_old.append(per)

    for pair in range(N_Q_HEADS // 2):
        halves = []
        for head in (2 * pair, 2 * pair + 1):
            g, hh = divmod(head, heads_per_group)
            old = jnp.concatenate([o_old[b][g][hh * seq:(hh + 1) * seq] for b in range(nseq)], axis=0)
            halves.append((old + o_new[g][hh * rows:(hh + 1) * rows]) * inv[head])
        o_ref[:, pair * LANES:(pair + 1) * LANES] = jnp.where(low, halves[0], halves[1])


def _swa_sample(sq, sk, sv, past_kt, past_vt, sinks, bsz, t):
    nseq = SAMPLE_ROWS // 2 // t
    assert nseq * t == LANES
    tok = lambda width: pl.BlockSpec((nseq * t, width), lambda i, s: (i, 0))
    cache = pl.BlockSpec((nseq,) + past_kt.shape[1:], lambda i, s: (i, 0, 0, 0))
    return pl.pallas_call(
        functools.partial(_swa_sample_kernel, seq=t),
        grid_spec=pltpu.PrefetchScalarGridSpec(
            num_scalar_prefetch=1,
            grid=(bsz // nseq,),
            in_specs=[tok(SWA_Q), tok(SWA_KV), tok(SWA_KV), cache, cache],
            out_specs=[tok(SWA_Q), cache, cache]),
        out_shape=[jax.ShapeDtypeStruct((bsz * t, SWA_Q), f32),
                   jax.ShapeDtypeStruct(past_kt.shape, f32),
                   jax.ShapeDtypeStruct(past_vt.shape, f32)],
        compiler_params=pltpu.CompilerParams(
            dimension_semantics=("parallel",), vmem_limit_bytes=VMEM_LIMIT),
        name="swa_sample",
    )(sinks, sq, sk, sv, past_kt, past_vt)


FFN_CHUNK = 1024


def _out_proj_norm(x, go, so, g1, sh2, sc2, n2, wout_ref):
    nb, t, d = x.shape
    mixed = _dot(go, wout_ref[:GLA_WIDTH, :]) + _dot(so, wout_ref[GLA_WIDTH:, :])
    h = x + g1 * mixed.reshape(nb, t, d)
    ms = jnp.mean(h * h, axis=-1, keepdims=True)
    hn = h * lax.rsqrt(ms + EPS) * n2
    hn = hn * (1.0 + sc2) + sh2
    return h, hn.reshape(nb * t, d).astype(bf16)


def _ffn_piece(hb, w1_ref, w2_ref, c):
    cols = slice(c * FFN_CHUNK, (c + 1) * FFN_CHUNK)
    a = jnp.maximum(_dot(hb, w1_ref[:, cols]), 0.0)
    return _dot((a * a).astype(bf16), w2_ref[cols, :])


def _out_ffn_block(x, go, so, g1, sh2, sc2, g2, n2, wout_ref, w1_ref, w2_ref):
    h, hb = _out_proj_norm(x, go, so, g1, sh2, sc2, n2, wout_ref)
    ff = _ffn_piece(hb, w1_ref, w2_ref, 0)
    for c in range(1, D_FF // FFN_CHUNK):
        ff = ff + _ffn_piece(hb, w1_ref, w2_ref, c)
    return h + g2 * ff.reshape(h.shape)


def _out_ffn_kernel(x_ref, go_ref, so_ref, g1_ref, sh2_ref, sc2_ref, g2_ref, n2_ref,
                    wout_ref, w1_ref, w2_ref, y_ref):
    y_ref[...] = _out_ffn_block(x_ref[...], go_ref[...].astype(bf16), so_ref[...].astype(bf16),
                                g1_ref[...], sh2_ref[...],
                                sc2_ref[...], g2_ref[...], n2_ref[...], wout_ref, w1_ref, w2_ref)


def _resident(shape):
    return pl.BlockSpec(shape, lambda *_: (0,) * len(shape), pipeline_mode=pl.Buffered(1))


def _out_ffn(x, go, so, mod, row0, n2, wout, w1, w2, nb, tb):
    bsz, t, d = x.shape
    m = nb * tb
    nt = t // tb
    mods = [_mod_spec(nb, row0, chunk, lambda i, j: i)
            for chunk in (MOD_GATE1, MOD_SHIFT2, MOD_SCALE2, MOD_GATE2)]
    xspec = pl.BlockSpec((nb, tb, d), lambda i, j: (i, j, 0))
    return pl.pallas_call(
        _out_ffn_kernel,
        grid=(bsz // nb, nt),
        in_specs=[xspec,
                  pl.BlockSpec((m, GLA_WIDTH), lambda i, j: (i * nt + j, 0)),
                  pl.BlockSpec((m, SWA_Q), lambda i, j: (i * nt + j, 0)),
                  *mods,
                  pl.BlockSpec((1, 1, d), lambda i, j: (0, 0, 0)),
                  _resident((d, d)), _resident((d, D_FF)), _resident((D_FF, d))],
        out_specs=xspec,
        out_shape=jax.ShapeDtypeStruct((bsz, t, d), f32),
        compiler_params=pltpu.CompilerParams(
            dimension_semantics=("parallel", "parallel"), vmem_limit_bytes=VMEM_LIMIT),
        name="out_ffn",
    )(x, go, so, mod, mod, mod, mod, n2, wout, w1, w2)


def _prompt_kernel(sink_ref,
                   q_ref, k_ref, v_ref, la_ref, r_ref, sq_ref, k2_ref, v2_ref, k2p_ref, v2p_ref,
                   tri_ref, masks_ref, gnw_ref,
                   x_ref, g1_ref, sh2_ref, sc2_ref, g2_ref, n2_ref, wout_ref, w1_ref, w2_ref,
                   y_ref, sout_ref, state_ref, mix_ref, *, levels, nt, nsteps):
    s = pl.program_id(0)
    slot = s % 2
    c = GLA_CHUNK
    nchunk = q_ref.shape[0] // c

    @pl.when(s == 0)
    def _():
        mix_ref[1] = jnp.zeros(mix_ref.shape[1:], mix_ref.dtype)
        state_ref[...] = jnp.zeros_like(state_ref)

    j = jnp.minimum(s, nsteps - 1) % nt
    st_in = state_ref[...]
    st = jnp.where(j == 0, 0.0, st_in)
    gnw = gnw_ref[...]
    out = mix_ref.at[slot]
    st = _gla_prompt_block(st, q_ref, k_ref, v_ref, la_ref, r_ref, tri_ref, masks_ref, gnw,
                           out.at[:, pl.ds(0, GLA_WIDTH)], levels)
    _swa_prompt_blocks(sq_ref, k2p_ref[...], k2_ref, v2p_ref[...], v2_ref, j == 0, sink_ref,
                       out.at[:, pl.ds(GLA_WIDTH, SWA_Q)])
    st = jnp.where(s < nsteps, st, st_in)
    state_ref[...] = st
    sout_ref[0] = st
    mix = mix_ref[1 - slot]
    y_ref[...] = _out_ffn_block(x_ref[...], mix[:, :GLA_WIDTH], mix[:, GLA_WIDTH:], g1_ref[...],
                                sh2_ref[...], sc2_ref[...], g2_ref[...], n2_ref[...],
                                wout_ref, w1_ref, w2_ref)


def _prompt_mix_ffn(x, gq, gk, gv, la, gr, sq, k2, v2, gnw, sinks, mod, row0, n2, wout, w1, w2):
    bsz, t, d = x.shape
    tb = TOKEN_BLOCK
    nt = t // tb
    nsteps = bsz * nt
    levels, tri, masks = _gla_constants(GLA_CHUNK, GLA_CHUNK)
    per_blk = tb // WINDOW

    def mix_blk(s):
        return jnp.minimum(s, nsteps - 1)

    def ffn_blk(s):
        return jnp.maximum(s - 1, 0)

    def tok(width):
        return pl.BlockSpec((tb, width), lambda s, _: (mix_blk(s), 0))

    prev = pl.BlockSpec((WINDOW, SWA_KV2), lambda s, _: (jnp.maximum(mix_blk(s) * per_blk - 1, 0), 0))
    xspec = pl.BlockSpec((1, tb, d), lambda s, _: (ffn_blk(s) // nt, ffn_blk(s) % nt, 0))
    mods = [_mod_spec(1, row0, chunk, lambda s, _: ffn_blk(s) // nt)
            for chunk in (MOD_GATE1, MOD_SHIFT2, MOD_SCALE2, MOD_GATE2)]
    return pl.pallas_call(
        functools.partial(_prompt_kernel, levels=levels, nt=nt, nsteps=nsteps),
        grid_spec=pltpu.PrefetchScalarGridSpec(
            num_scalar_prefetch=1,
            grid=(nsteps + 1,),
            in_specs=[tok(GLA_QK), tok(GLA_QK), tok(GLA_WIDTH), tok(GLA_QK), tok(GLA_WIDTH),
                      tok(SWA_Q), tok(SWA_KV2), tok(SWA_KV2), prev, prev,
                      _resident(tri.shape), _resident(masks.shape), _resident((1, GLA_DV)),
                      xspec, *mods, _resident((1, 1, d)),
                      _resident((d, d)), _resident((d, D_FF)), _resident((D_FF, d))],
            out_specs=[xspec,
                       pl.BlockSpec((1, GLA_DV, GLA_QK), lambda s, _: (mix_blk(s) // nt, 0, 0))],
            scratch_shapes=[pltpu.VMEM((GLA_DV, GLA_QK), f32),
                            pltpu.VMEM((2, tb, GLA_WIDTH + SWA_Q), bf16)]),
        out_shape=[jax.ShapeDtypeStruct((bsz, t, d), f32),
                   jax.ShapeDtypeStruct((bsz, GLA_DV, GLA_QK), f32)],
        compiler_params=pltpu.CompilerParams(
            dimension_semantics=("arbitrary",), vmem_limit_bytes=VMEM_LIMIT),
        name="prompt_mix_ffn",
    )(sinks, gq, gk, gv, la, gr, sq, k2, v2, k2, v2, tri, masks, gnw,
      x, mod, mod, mod, mod, n2, wout, w1, w2)


def _rope_tables(pos):
    half = HEAD_DIM // 2
    inv = jnp.power(ROPE_THETA, -jnp.arange(half, dtype=f32) * 2.0 / HEAD_DIM)
    ang = pos.astype(f32)[:, None] * inv[None, :]
    cos = jnp.cos(ang)
    sin = jnp.sin(ang)
    reps = LANES // HEAD_DIM
    return (jnp.tile(jnp.concatenate([cos, cos], axis=-1), (1, reps)),
            jnp.tile(jnp.concatenate([-sin, sin], axis=-1), (1, reps)))


def _block_diag_ones(n, blk):
    idx = np.arange(n) // blk
    return jnp.asarray((idx[:, None] == idx[None, :]).astype(np.float32), bf16)


def _layer_weights(w_in, w_gate_up, b_gate, q_norm_w, k_norm_w, w_out, w_ff1, w_ff2):
    splits = np.cumsum([GLA_QK, GLA_QK, GLA_WIDTH, GLA_WIDTH, GLA_GATE_RANK, SWA_Q, SWA_KV])
    gq, gk, gv, gr, glr, sq, sk, sv = jnp.split(w_in, [int(s) for s in splits], axis=1)
    glr = jnp.pad(glr, ((0, 0), (0, LANES - GLA_GATE_RANK)))
    win = jnp.concatenate([gq, gk, gv, gr, sq, sk, sv, glr], axis=1).astype(bf16)
    wgu = jnp.pad(w_gate_up, ((0, LANES - GLA_GATE_RANK), (0, 0))).astype(bf16)
    return dict(
        win=win, wgu=wgu, bg=b_gate.reshape(1, GLA_QK),
        qnw=jnp.tile(q_norm_w, N_Q_HEADS).reshape(1, SWA_Q),
        knw=jnp.tile(k_norm_w, N_KV_HEADS).reshape(1, SWA_KV),
        wout=w_out.astype(bf16), w1=w_ff1.astype(bf16), w2=w_ff2.astype(bf16))


def _decoder_layer(x, mod, row0, pos, state, past_k, past_v, lw, n1, n2, gnw, sinks, bdq, bdk):
    bsz, t, d = x.shape
    cos_t, sin_t = _rope_tables(pos)
    prompt = state is None
    if prompt:
        nb, tb, tail, act, groups = 1, TOKEN_BLOCK, WINDOW, bf16, PROMPT_INPROJ_GROUPS
    else:
        nb, tb, tail, act, groups = SAMPLE_ROWS // t, t, SAMPLE_ROWS, f32, 1
        cos_t = jnp.tile(cos_t, (nb, 1))
        sin_t = jnp.tile(sin_t, (nb, 1))
    n2 = n2.reshape(1, 1, d)
    gq, gk, gv, gr, la, sq, k2, v2, kk, vk = _inproj(
        x, mod, row0, n1.reshape(1, 1, d), lw["win"], lw["wgu"], lw["bg"], lw["qnw"], lw["knw"],
        bdq, bdk, cos_t, sin_t, nb, tb, tail, act, groups)
    if prompt:
        y, s_t = _prompt_mix_ffn(x, gq, gk, gv, la, gr, sq, k2, v2, gnw, sinks, mod, row0, n2,
                                 lw["wout"], lw["w1"], lw["w2"])
        s_new = s_t.reshape(bsz, GLA_DV, GLA_HEADS, GLA_DK).transpose(0, 2, 3, 1)
        k_keep, v_keep = kk, vk
    else:
        go, s_new = _gla_sample(gq, gk, gv, la, gr, state.reshape(bsz, GLA_QK, GLA_DV), gnw, bsz, t)
        so, kt, vt = _swa_sample(sq, kk, vk, past_k.transpose(0, 2, 3, 1), past_v.transpose(0, 2, 3, 1),
                                 sinks, bsz, t)
        k_keep, v_keep = kt.transpose(0, 3, 1, 2), vt.transpose(0, 3, 1, 2)
        y = _out_ffn(x, go, so, mod, row0, n2, lw["wout"], lw["w1"], lw["w2"], nb, tb)
    return (y, s_new.reshape(bsz, GLA_HEADS, GLA_DK, GLA_DV),
            k_keep.reshape(bsz, WINDOW, N_KV_HEADS, HEAD_DIM),
            v_keep.reshape(bsz, WINDOW, N_KV_HEADS, HEAD_DIM))


def kernel(x_prompt, x_sample, state_gla, cache_swa_k, cache_swa_v, c_prompt, c_sample, w_ada, b_ada, norm1_w, norm2_w, w_in, w_gate_up, b_gate, gla_norm_w, q_norm_w, k_norm_w, sinks, w_out, w_ff1, w_ff2):
    depth = w_ada.shape[0]
    bp, tp, _ = x_prompt.shape
    bs, ts, _ = x_sample.shape
    pos_p = jnp.arange(tp)
    pos_s = PAST_LEN + jnp.arange(ts)
    bdq = _block_diag_ones(SWA_Q, HEAD_DIM)
    bdk = _block_diag_ones(SWA_KV, HEAD_DIM)
    c_all = jnp.concatenate([c_sample, c_prompt], axis=0)
    yp, ys = x_prompt, x_sample
    outs = [[] for _ in range(6)]
    for l in range(depth):
        mod = _modulation(c_all, w_ada[l], b_ada[l])
        lw = _layer_weights(w_in[l], w_gate_up[l], b_gate[l], q_norm_w[l], k_norm_w[l],
                            w_out[l], w_ff1[l], w_ff2[l])
        gnw = gla_norm_w[l].reshape(1, GLA_DV)
        common = (lw, norm1_w[l], norm2_w[l], gnw, sinks[l], bdq, bdk)
        yp, gp, kp, vp = _decoder_layer(yp, mod, bs, pos_p, None, None, None, *common)
        ys, gs, kq, vq = _decoder_layer(ys, mod, 0, pos_s, state_gla[l], cache_swa_k[l],
                                        cache_swa_v[l], *common)
        for lst, val in zip(outs, (gp, kp, vp, gs, kq, vq)):
            lst.append(val)
    return (yp, ys) + tuple(jnp.stack(o) for o in outs)
```

```python
import functools

import jax
import jax.numpy as jnp
import numpy as np
from jax import lax
from jax.experimental import pallas as pl
from jax.experimental.pallas import tpu as pltpu

f32 = jnp.float32
bf16 = jnp.bfloat16

D_MODEL = 1024
GLA_HEADS = 4
GLA_DK = 64
GLA_DV = 128
GLA_QK = GLA_HEADS * GLA_DK
GLA_WIDTH = GLA_HEADS * GLA_DV
GLA_GATE_RANK = 16
GLA_TAU = 16.0
LOG2E = 1.4426950408889634
HEAD_DIM = 64
N_Q_HEADS = 8
N_KV_HEADS = 2
SWA_Q = N_Q_HEADS * HEAD_DIM
SWA_KV = N_KV_HEADS * HEAD_DIM
SWA_KV2 = 2 * SWA_KV
WINDOW = 128
ROPE_THETA = 10000.0
PAST_LEN = 8192
D_FF = 4 * D_MODEL
EPS = 1e-6
LANES = 128
GLA_CHUNK = 128
TOKEN_BLOCK = 512
SAMPLE_ROWS = 256
PROMPT_INPROJ_GROUPS = 2
VMEM_LIMIT = 56 * 1024 * 1024

_SEG = {}
_off = 0
for _name, _w in (("gq", GLA_QK), ("gk", GLA_QK), ("gv", GLA_WIDTH), ("gr", GLA_WIDTH),
                  ("sq", SWA_Q), ("sk", SWA_KV), ("sv", SWA_KV), ("glr", LANES)):
    _SEG[_name] = (_off, _off + _w)
    _off += _w
IN_WIDTH_PADDED = _off


def _dot(a, b):
    return jnp.dot(a, b, preferred_element_type=f32)


def _dot_nt(a, b):
    return lax.dot_general(a, b, (((1,), (1,)), ((), ())), preferred_element_type=f32)


def _dot_tn(a, b):
    return lax.dot_general(a, b, (((0,), (0,)), ((), ())), preferred_element_type=f32)


def _sigmoid(x):
    return 1.0 / (1.0 + jnp.exp(-x))


def _mod_kernel(c_ref, w_ref, b_ref, o_ref):
    c = c_ref[...]
    s = (c * _sigmoid(c)).astype(bf16)
    res = _dot(s, w_ref[...].astype(bf16)) + b_ref[...]
    for r in range(res.shape[0]):
        o_ref[r] = res[r:r + 1, :]


def _modulation(c_all, w_ada, b_ada):
    m = c_all.shape[0]
    n = w_ada.shape[1]
    bn = 1536
    return pl.pallas_call(
        _mod_kernel,
        grid=(n // bn,),
        in_specs=[pl.BlockSpec((m, D_MODEL), lambda j: (0, 0)),
                  pl.BlockSpec((D_MODEL, bn), lambda j: (0, j)),
                  pl.BlockSpec((1, bn), lambda j: (0, j))],
        out_specs=pl.BlockSpec((m, 1, bn), lambda j: (0, 0, j)),
        out_shape=jax.ShapeDtypeStruct((m, 1, n), f32),
        compiler_params=pltpu.CompilerParams(vmem_limit_bytes=VMEM_LIMIT),
        name="adaln_mod",
    )(c_all, w_ada, b_ada.reshape(1, n))


def _group_rms(x, bd_ref, w_ref):
    ssq = _dot((x * x).astype(bf16), bd_ref[...])
    return x * lax.rsqrt(ssq * (1.0 / HEAD_DIM) + EPS) * w_ref[...]


def _rope(x, cos, sin_signed, low_half):
    partner = jnp.where(low_half, pltpu.roll(x, LANES - 32, axis=1), pltpu.roll(x, 32, axis=1))
    return x * cos + partner * sin_signed


def _dup_heads(x, low_lanes):
    rolled = pltpu.roll(x, HEAD_DIM, axis=1)
    return jnp.where(low_lanes, x, rolled), jnp.where(low_lanes, rolled, x)


def _inproj_kernel(x_ref, sh_ref, sc_ref, n1_ref, win_ref, wgu_ref, bg_ref, qnw_ref, knw_ref,
                   bdq_ref, bdk_ref, cos_ref, sin_ref,
                   gq_ref, gk_ref, gv_ref, gr_ref, la_ref, sq_ref, k2_ref, v2_ref, kk_ref, vk_ref,
                   *, groups):
    nb, t, d = x_ref.shape
    m = nb * t
    tail = kk_ref.shape[0]
    mg = m // groups
    lane = lax.broadcasted_iota(jnp.int32, (mg, LANES), 1)
    low_half = (lane & 32) == 0
    low_lanes = lane < HEAD_DIM
    for grp in range(groups):
        rows = slice(grp * mg, (grp + 1) * mg)
        if nb == 1:
            x = x_ref[:, rows, :]
            sc, sh = sc_ref[...], sh_ref[...]
        else:
            seqs = slice(grp * (nb // groups), (grp + 1) * (nb // groups))
            x = x_ref[seqs]
            sc, sh = sc_ref[seqs], sh_ref[seqs]
        ms = jnp.mean(x * x, axis=-1, keepdims=True)
        hn = x * lax.rsqrt(ms + EPS) * n1_ref[...]
        hn = hn * (1.0 + sc) + sh
        hb = hn.reshape(mg, d).astype(bf16)

        def seg(name):
            a, b = _SEG[name]
            return _dot(hb, win_ref[:, a:b])

        cos = cos_ref[rows, :]
        sin = sin_ref[rows, :]
        sq = _group_rms(seg("sq"), bdq_ref, qnw_ref)
        for c in range(SWA_Q // LANES):
            blk = _rope(sq[:, c * LANES:(c + 1) * LANES], cos, sin, low_half)
            sq_ref[rows, c * LANES:(c + 1) * LANES] = (blk * (HEAD_DIM ** -0.5)).astype(sq_ref.dtype)
        sk = _rope(_group_rms(seg("sk"), bdk_ref, knw_ref), cos, sin, low_half)
        sv = seg("sv")
        keep = (grp + 1) * mg - (m - tail)
        if keep > 0:
            keep = min(keep, mg)
            dst_rows = slice((grp + 1) * mg - keep - (m - tail), (grp + 1) * mg - (m - tail))
            kk_ref[dst_rows, :] = sk[mg - keep:]
            vk_ref[dst_rows, :] = sv[mg - keep:]
        for src, dst in ((sk, k2_ref), (sv, v2_ref)):
            d0, d1 = _dup_heads(src, low_lanes)
            dst[rows, :LANES] = d0.astype(dst.dtype)
            dst[rows, LANES:] = d1.astype(dst.dtype)
        glr = seg("glr").astype(bf16)
        g = _dot(glr, wgu_ref[...]) + bg_ref[...]
        log_sig = jnp.minimum(g, 0.0) - jnp.log1p(jnp.exp(-jnp.abs(g)))
        la_ref[rows, :] = log_sig * (LOG2E / GLA_TAU)
        gq_ref[rows, :] = seg("gq") * (GLA_DK ** -0.5)
        gk_ref[rows, :] = seg("gk")
        gv_ref[rows, :] = seg("gv").astype(gv_ref.dtype)
        gr_ref[rows, :] = seg("gr").astype(gr_ref.dtype)


MOD_SHIFT1, MOD_SCALE1, MOD_GATE1, MOD_SHIFT2, MOD_SCALE2, MOD_GATE2 = range(6)


def _mod_spec(nb, row0, chunk, batch_block):
    return pl.BlockSpec((nb, 1, D_MODEL), lambda *ids: (row0 // nb + batch_block(*ids), 0, chunk))


def _inproj(x, mod, row0, n1, win, wgu, bg, qnw, knw, bdq, bdk, cos_t, sin_t, nb, tb, tail, act, groups):
    bsz, t, d = x.shape
    m = nb * tb
    nt = t // tb
    grid = (bsz // nb, nt)
    tok = bsz * t

    def full(shape):
        return pl.BlockSpec(shape, lambda i, j: (0,) * len(shape))

    def out(width):
        return pl.BlockSpec((m, width), lambda i, j: (i * nt + j, 0))

    tail_spec = pl.BlockSpec((tail, SWA_KV), lambda i, j: (i, 0))
    outs = ((GLA_QK, f32), (GLA_QK, f32), (GLA_WIDTH, act), (GLA_WIDTH, act), (GLA_QK, f32),
            (SWA_Q, act), (SWA_KV2, bf16), (SWA_KV2, bf16))
    ntail = (bsz // nb) * tail
    return pl.pallas_call(
        functools.partial(_inproj_kernel, groups=groups),
        grid=grid,
        in_specs=[pl.BlockSpec((nb, tb, d), lambda i, j: (i, j, 0)),
                  _mod_spec(nb, row0, MOD_SHIFT1, lambda i, j: i),
                  _mod_spec(nb, row0, MOD_SCALE1, lambda i, j: i),
                  full((1, 1, d)),
                  full((d, IN_WIDTH_PADDED)),
                  full((LANES, GLA_QK)),
                  full((1, GLA_QK)),
                  full((1, SWA_Q)),
                  full((1, SWA_KV)),
                  full((SWA_Q, SWA_Q)),
                  full((SWA_KV, SWA_KV)),
                  pl.BlockSpec((m, LANES), lambda i, j: (j, 0)),
                  pl.BlockSpec((m, LANES), lambda i, j: (j, 0))],
        out_specs=[out(w) for w, _ in outs] + [tail_spec, tail_spec],
        out_shape=[jax.ShapeDtypeStruct((tok, w), dt) for w, dt in outs]
        + [jax.ShapeDtypeStruct((ntail, SWA_KV), f32)] * 2,
        compiler_params=pltpu.CompilerParams(
            dimension_semantics=("parallel", "arbitrary"), vmem_limit_bytes=VMEM_LIMIT),
        name="inproj",
    )(x, mod, mod, n1, win, wgu, bg, qnw, knw, bdq, bdk, cos_t, sin_t)


def _gla_constants(chunk, seq):
    t = np.arange(chunk)
    levels = []
    m = 1
    while m < seq:
        levels.append(m)
        m *= 2
    masks = [np.eye(chunk, dtype=bool)]
    for m in levels:
        upper = (t % (2 * m) >= m)[:, None]
        lower = (t % (2 * m) < m)[None, :]
        same = (t[:, None] // (2 * m)) == (t[None, :] // (2 * m))
        masks.append(same & upper & lower)
    tri = t[None, :] <= t[:, None]
    return (tuple(levels), jnp.asarray(tri.astype(np.float32), bf16),
            jnp.asarray(np.stack(masks).astype(np.float32)))


def _stack_heads(xb, lane_head, axis):
    zero = jnp.zeros_like(xb)
    return jnp.concatenate([jnp.where(lane_head == h, xb, zero) for h in range(GLA_HEADS)], axis=axis)


def _block_sums(la, bcum, m, row, rolls):
    c, n = la.shape
    if m == 1:
        return la, None
    if m < 8:
        def rolled(shift):
            if shift not in rolls:
                rolls[shift] = pltpu.roll(la, shift % c, axis=0)
            return rolls[shift]
        pos = row & (m - 1)
        pre = la
        suf = None
        for j in range(1, m):
            pre = pre + jnp.where(pos >= j, rolled(j), 0.0)
            term = jnp.where(pos < m - j, rolled(-j), 0.0)
            suf = term if suf is None else suf + term
        return pre, suf
    before, last = [], []
    for i in range(c // m):
        before.append(jnp.zeros((m, n), f32) if i == 0
                      else jnp.broadcast_to(bcum[i * m - 1:i * m], (m, n)))
        last.append(jnp.broadcast_to(bcum[(i + 1) * m - 1:(i + 1) * m], (m, n)))
    if len(before) == 1:
        return bcum - before[0], last[0] - bcum
    return bcum - jnp.concatenate(before, axis=0), jnp.concatenate(last, axis=0) - bcum


def _gla_scores(q_ref, k_ref, v_ref, la_ref, tri_ref, masks_ref, levels, seq):
    c = q_ref.shape[0]
    la = la_ref[...]
    hi = la.astype(bf16)
    lo = (la - hi.astype(f32)).astype(bf16)
    tri = tri_ref[...]
    bcum = _dot(tri, hi) + _dot(tri, lo)
    q = q_ref[...]
    k = k_ref[...]
    vb = v_ref[...].astype(bf16)
    row = lax.broadcasted_iota(jnp.int32, (c, GLA_QK), 0)
    lane_head = lax.broadcasted_iota(jnp.int32, (c, GLA_QK), 1) >> 6
    rolls = {}
    attn = None
    for lvl, m in enumerate((0,) + levels):
        if m == 0:
            qt, kt = q, k
        else:
            pre, suf = _block_sums(la, bcum, m, row, rolls)
            qt = q * jnp.exp2(pre)
            kt = k if suf is None else k * jnp.exp2(suf)
        r = _dot_nt(qt.astype(bf16), _stack_heads(kt.astype(bf16), lane_head, 0))
        mk = masks_ref[lvl]
        parts = [r[:, h * c:(h + 1) * c] * mk for h in range(GLA_HEADS)]
        attn = parts if attn is None else [a + p for a, p in zip(attn, parts)]
    a = jnp.concatenate(attn, axis=1).astype(bf16)
    pre, suf = _block_sums(la, bcum, seq, row, rolls)
    qf = (q * jnp.exp2(pre)).astype(bf16)
    kf = (k * jnp.exp2(suf)).astype(bf16)
    return a, qf, kf, vb, bcum, hi, lo


def _gla_values(a, vb):
    v_head = lax.broadcasted_iota(jnp.int32, vb.shape, 1) >> 7
    return _dot(a, _stack_heads(vb, v_head, 0))


def _gla_finish(o, r_ref, gnw, go_ref):
    for h in range(GLA_HEADS):
        sl = slice(h * GLA_DV, (h + 1) * GLA_DV)
        oh = o[:, sl]
        r = r_ref[:, sl].astype(f32)
        ms = jnp.mean(oh * oh, axis=-1, keepdims=True)
        go_ref[:, sl] = (oh * lax.rsqrt(ms + EPS) * gnw * (r * _sigmoid(r))).astype(go_ref.dtype)


def _gla_prompt_block(st, q_ref, k_ref, v_ref, la_ref, r_ref, tri_ref, masks_ref, gnw, go_ref, levels,
                      between=lambda: None):
    c = GLA_CHUNK
    nchunk = q_ref.shape[0] // c
    lane_head = lax.broadcasted_iota(jnp.int32, (GLA_DV, GLA_QK), 1) >> 6
    chunks = []
    for ci in range(nchunk):
        rows = pl.ds(ci * c, c)
        chunks.append(_gla_scores(q_ref.at[rows], k_ref.at[rows], v_ref.at[rows], la_ref.at[rows],
                                  tri_ref, masks_ref, levels, c))
        between()
    intra, upds, decays = [], [], []
    for a, _, kf, vb, bcum, _, _ in chunks:
        intra.append(_gla_values(a, vb))
        full = _dot_tn(vb, kf)
        upd = None
        for h in range(GLA_HEADS):
            term = jnp.where(lane_head == h, full[h * GLA_DV:(h + 1) * GLA_DV], 0.0)
            upd = term if upd is None else upd + term
        upds.append(upd)
        decays.append(jnp.exp2(bcum[c - 1:c]))
    between()
    for ci in range(nchunk):
        rows = pl.ds(ci * c, c)
        sbd_t = _stack_heads(st.astype(bf16), lane_head, 0)
        o = intra[ci] + _dot_nt(chunks[ci][1], sbd_t)
        _gla_finish(o, r_ref.at[rows], gnw, go_ref.at[rows])
        st = decays[ci] * st + upds[ci]
    return st


def _gla_sample_kernel(q_ref, k_ref, v_ref, la_ref, r_ref, s0_ref, tri_ref, masks_ref, gnw_ref,
                       go_ref, sout_ref, *, levels, seq):
    c = q_ref.shape[0]
    a, qf, kf, vb, _, hi, lo = _gla_scores(q_ref, k_ref, v_ref, la_ref, tri_ref, masks_ref, levels, seq)
    o_intra = _gla_values(a, vb)
    row_head = lax.broadcasted_iota(jnp.int32, (GLA_QK, GLA_DV), 0) >> 6
    ones = jnp.ones((seq, GLA_DV), bf16)
    inter = []
    for b in range(c // seq):
        rows = slice(b * seq, (b + 1) * seq)
        s_old = s0_ref[b]
        sbd = _stack_heads(s_old.astype(bf16), row_head, 1)
        inter.append(_dot(qf[rows], sbd))
        p = _dot_tn(kf[rows], vb[rows])
        upd = None
        for h in range(GLA_HEADS):
            term = jnp.where(row_head == h, p[:, h * GLA_DV:(h + 1) * GLA_DV], 0.0)
            upd = term if upd is None else upd + term
        total = _dot_tn(hi[rows], ones) + _dot_tn(lo[rows], ones)
        sout_ref[b] = jnp.exp2(total) * s_old + upd
    o = o_intra + jnp.concatenate(inter, axis=0)
    _gla_finish(o, r_ref, gnw_ref[...], go_ref)


def _gla_sample(gq, gk, gv, la, gr, state, gnw, bsz, t):
    c = GLA_CHUNK
    nseq = c // t
    levels, tri, masks = _gla_constants(c, t)

    def tok(width):
        return pl.BlockSpec((c, width), lambda i: (i, 0))

    return pl.pallas_call(
        functools.partial(_gla_sample_kernel, levels=levels, seq=t),
        grid=(bsz // nseq,),
        in_specs=[tok(GLA_QK), tok(GLA_QK), tok(GLA_WIDTH), tok(GLA_QK), tok(GLA_WIDTH),
                  pl.BlockSpec((nseq, GLA_QK, GLA_DV), lambda i: (i, 0, 0)),
                  pl.BlockSpec(tri.shape, lambda i: (0, 0)),
                  pl.BlockSpec(masks.shape, lambda i: (0, 0, 0)),
                  pl.BlockSpec((1, GLA_DV), lambda i: (0, 0))],
        out_specs=[tok(GLA_WIDTH),
                   pl.BlockSpec((nseq, GLA_QK, GLA_DV), lambda i: (i, 0, 0))],
        out_shape=[jax.ShapeDtypeStruct((bsz * t, GLA_WIDTH), f32),
                   jax.ShapeDtypeStruct((bsz, GLA_QK, GLA_DV), f32)],
        compiler_params=pltpu.CompilerParams(
            dimension_semantics=("parallel",), vmem_limit_bytes=VMEM_LIMIT),
        name="gla_sample",
    )(gq, gk, gv, la, gr, state, tri, masks, gnw)


def _sink_attention(q_ref, rows, kdups, vdups, mask, sink_ref, o_ref, between=lambda: None):
    n = len(kdups)
    heads_per_group = N_Q_HEADS // N_KV_HEADS
    low_q = lax.broadcasted_iota(jnp.int32, (rows, LANES), 1) < HEAD_DIM
    scores = []
    for i in range(n):
        per_group = []
        for g in range(N_KV_HEADS):
            stack = []
            for p in range(heads_per_group // 2):
                pair = g * (heads_per_group // 2) + p
                qp = q_ref[i * rows:(i + 1) * rows, pair * LANES:(pair + 1) * LANES].astype(bf16)
                zero = jnp.zeros_like(qp)
                stack.append(jnp.where(low_q, qp, zero))
                stack.append(jnp.where(low_q, zero, qp))
            per_group.append(_dot_nt(jnp.concatenate(stack, axis=0), kdups[i][g]))
        scores.append(per_group)
    between()
    probs, inv = [], []
    for head in range(N_Q_HEADS):
        g, hh = divmod(head, heads_per_group)
        parts = [scores[i][g][hh * rows:(hh + 1) * rows] for i in range(n)]
        s = parts[0] if n == 1 else jnp.concatenate(parts, axis=0)
        sink = sink_ref[head]
        s = jnp.where(mask, s, -1e30)
        mx = jnp.maximum(jnp.max(s, axis=-1, keepdims=True), sink)
        p = jnp.exp(s - mx)
        inv.append(1.0 / (jnp.sum(p, axis=-1, keepdims=True) + jnp.exp(sink - mx)))
        probs.append(p.astype(bf16))
    between()
    outs = []
    for i in range(n):
        per_group = []
        for g in range(N_KV_HEADS):
            p_i = jnp.concatenate([probs[g * heads_per_group + hh][i * rows:(i + 1) * rows]
                                   for hh in range(heads_per_group)], axis=0)
            per_group.append(_dot(p_i, vdups[i][g]))
        outs.append(per_group)
    low_all = lax.broadcasted_iota(jnp.int32, (n * rows, LANES), 1) < HEAD_DIM
    for pair in range(N_Q_HEADS // 2):
        halves = []
        for head in (2 * pair, 2 * pair + 1):
            g, hh = divmod(head, heads_per_group)
            parts = [outs[i][g][hh * rows:(hh + 1) * rows] for i in range(n)]
            o = parts[0] if n == 1 else jnp.concatenate(parts, axis=0)
            halves.append(o * inv[head])
        o_ref[:, pair * LANES:(pair + 1) * LANES] = jnp.where(low_all, halves[0], halves[1]).astype(o_ref.dtype)


def _swa_prompt_blocks(q_ref, k_before, k2_ref, v_before, v2_ref, first, sink_ref, o_ref,
                       between=lambda: None):
    w = WINDOW
    nblk = q_ref.shape[0] // w

    def dup(before, ref):
        blocks = [before] + [ref[i * w:(i + 1) * w, :] for i in range(nblk)]
        return [[jnp.concatenate([blocks[i][:, g * LANES:(g + 1) * LANES],
                                  blocks[i + 1][:, g * LANES:(g + 1) * LANES]], axis=0)
                 for g in range(N_KV_HEADS)] for i in range(nblk)]

    row = lax.broadcasted_iota(jnp.int32, (nblk * w, 2 * w), 0)
    tk = lax.broadcasted_iota(jnp.int32, (nblk * w, 2 * w), 1)
    rel = tk - (row & (w - 1))
    first_key = jnp.where(row < w, jnp.where(first, w, 0), 0)
    mask = (rel > 0) & (rel <= w) & (tk >= first_key)
    _sink_attention(q_ref, w, dup(k_before, k2_ref), dup(v_before, v2_ref), mask, sink_ref, o_ref, between)


def _swa_sample_kernel(sink_ref, q_ref, kn_ref, vn_ref, pk_ref, pv_ref, o_ref, ko_ref, vo_ref, *, seq):
    nseq, _, _, w = pk_ref.shape
    rows = nseq * seq
    heads_per_group = N_Q_HEADS // N_KV_HEADS
    lane = lax.broadcasted_iota(jnp.int32, (rows, LANES), 1)
    r_id = lax.broadcasted_iota(jnp.int32, (rows, LANES), 0)
    low = lane < HEAD_DIM
    pos = r_id & (seq - 1)
    mask_old = lane > pos
    seq_shift = seq.bit_length() - 1
    mask_new = ((r_id >> seq_shift) == (lane >> seq_shift)) & ((lane & (seq - 1)) <= pos)
    kn = kn_ref[...]
    vn = vn_ref[...]
    kn_dup = [d.astype(bf16) for d in _dup_heads(kn, low)]
    vn_dup = [d.astype(bf16) for d in _dup_heads(vn, low)]

    pad = jnp.zeros((w - seq, LANES), f32)
    tail_lanes = lax.broadcasted_iota(jnp.int32, (HEAD_DIM, w), 1) >= w - seq
    kt_dup, vt_dup = [], []
    for b in range(nseq):
        tok = slice(b * seq, (b + 1) * seq)
        per_k, per_v = [], []
        for new, old_ref, out_ref, per in ((kn, pk_ref, ko_ref, per_k), (vn, pv_ref, vo_ref, per_v)):
            new_t = jnp.concatenate([pad, new[tok]], axis=0).T
            for g in range(N_KV_HEADS):
                old = old_ref[b, g]
                out_ref[b, g] = jnp.where(tail_lanes, new_t[g * HEAD_DIM:(g + 1) * HEAD_DIM],
                                          pltpu.roll(old, w - seq, axis=1))
                ob = old.astype(bf16)
                per.append(jnp.concatenate([ob, ob], axis=0))
        kt_dup.append(per_k)
        vt_dup.append(per_v)

    lhs = []
    for g in range(N_KV_HEADS):
        stack = []
        for p in range(heads_per_group // 2):
            pair = g * (heads_per_group // 2) + p
            qp = q_ref[:, pair * LANES:(pair + 1) * LANES].astype(bf16)
            zero = jnp.zeros_like(qp)
            stack.append(jnp.where(low, qp, zero))
            stack.append(jnp.where(low, zero, qp))
        lhs.append(jnp.concatenate(stack, axis=0))
    s_new = [_dot_nt(lhs[g], kn_dup[g]) for g in range(N_KV_HEADS)]
    s_old = []
    for b in range(nseq):
        per = []
        for g in range(N_KV_HEADS):
            qb = jnp.concatenate([lhs[g][hh * rows + b * seq:hh * rows + (b + 1) * seq]
                                  for hh in range(heads_per_group)], axis=0)
            per.append(_dot(qb, kt_dup[b][g]))
        s_old.append(per)

    p_old, p_new, inv = [], [], []
    for head in range(N_Q_HEADS):
        g, hh = divmod(head, heads_per_group)
        so = jnp.concatenate([s_old[b][g][hh * seq:(hh + 1) * seq] for b in range(nseq)], axis=0)
        so = jnp.where(mask_old, so, -1e30)
        sn = jnp.where(mask_new, s_new[g][hh * rows:(hh + 1) * rows], -1e30)
        sink = sink_ref[head]
        mx = jnp.maximum(jnp.maximum(jnp.max(so, axis=-1, keepdims=True),
                                     jnp.max(sn, axis=-1, keepdims=True)), sink)
        po = jnp.exp(so - mx)
        pn = jnp.exp(sn - mx)
        inv.append(1.0 / (jnp.sum(po, axis=-1, keepdims=True) + jnp.sum(pn, axis=-1, keepdims=True)
                          + jnp.exp(sink - mx)))
        p_old.append(po.astype(bf16))
        p_new.append(pn.astype(bf16))

    o_new = [_dot(jnp.concatenate([p_new[g * heads_per_group + hh] for hh in range(heads_per_group)], axis=0),
                  vn_dup[g]) for g in range(N_KV_HEADS)]
    o_old = []
    for b in range(nseq):
        per = []
        for g in range(N_KV_HEADS):
            pb = jnp.concatenate([p_old[g * heads_per_group + hh][b * seq:(b + 1) * seq]
                                  for hh in range(heads_per_group)], axis=0)
            per.append(_dot_nt(pb, vt_dup[b][g]))
        o_old.append(per)

    for pair in range(N_Q_HEADS // 2):
        halves = []
        for head in (2 * pair, 2 * pair + 1):
            g, hh = divmod(head, heads_per_group)
            old = jnp.concatenate([o_old[b][g][hh * seq:(hh + 1) * seq] for b in range(nseq)], axis=0)
            halves.append((old + o_new[g][hh * rows:(hh + 1) * rows]) * inv[head])
        o_ref[:, pair * LANES:(pair + 1) * LANES] = jnp.where(low, halves[0], halves[1])


def _swa_sample(sq, sk, sv, past_kt, past_vt, sinks, bsz, t):
    nseq = SAMPLE_ROWS // 2 // t
    assert nseq * t == LANES
    tok = lambda width: pl.BlockSpec((nseq * t, width), lambda i, s: (i, 0))
    cache = pl.BlockSpec((nseq,) + past_kt.shape[1:], lambda i, s: (i, 0, 0, 0))
    return pl.pallas_call(
        functools.partial(_swa_sample_kernel, seq=t),
        grid_spec=pltpu.PrefetchScalarGridSpec(
            num_scalar_prefetch=1,
            grid=(bsz // nseq,),
            in_specs=[tok(SWA_Q), tok(SWA_KV), tok(SWA_KV), cache, cache],
            out_specs=[tok(SWA_Q), cache, cache]),
        out_shape=[jax.ShapeDtypeStruct((bsz * t, SWA_Q), f32),
                   jax.ShapeDtypeStruct(past_kt.shape, f32),
                   jax.ShapeDtypeStruct(past_vt.shape, f32)],
        compiler_params=pltpu.CompilerParams(
            dimension_semantics=("parallel",), vmem_limit_bytes=VMEM_LIMIT),
        name="swa_sample",
    )(sinks, sq, sk, sv, past_kt, past_vt)


FFN_CHUNK = 1024
FFN_CHUNK_FUSED = 512


def _out_proj_norm(x, go, so, g1, sh2, sc2, n2, wout_ref):
    nb, t, d = x.shape
    mixed = _dot(go, wout_ref[:GLA_WIDTH, :]) + _dot(so, wout_ref[GLA_WIDTH:, :])
    h = x + g1 * mixed.reshape(nb, t, d)
    ms = jnp.mean(h * h, axis=-1, keepdims=True)
    hn = h * lax.rsqrt(ms + EPS) * n2
    hn = hn * (1.0 + sc2) + sh2
    return h, hn.reshape(nb * t, d).astype(bf16)


def _ffn_piece(hb, w1_ref, w2_ref, c, width):
    cols = slice(c * width, (c + 1) * width)
    a = jnp.maximum(_dot(hb, w1_ref[:, cols]), 0.0)
    return _dot((a * a).astype(bf16), w2_ref[cols, :])


def _out_ffn_block(x, go, so, g1, sh2, sc2, g2, n2, wout_ref, w1_ref, w2_ref):
    h, hb = _out_proj_norm(x, go, so, g1, sh2, sc2, n2, wout_ref)
    ff = _ffn_piece(hb, w1_ref, w2_ref, 0, FFN_CHUNK)
    for c in range(1, D_FF // FFN_CHUNK):
        ff = ff + _ffn_piece(hb, w1_ref, w2_ref, c, FFN_CHUNK)
    return h + g2 * ff.reshape(h.shape)


def _out_ffn_kernel(x_ref, go_ref, so_ref, g1_ref, sh2_ref, sc2_ref, g2_ref, n2_ref,
                    wout_ref, w1_ref, w2_ref, y_ref):
    y_ref[...] = _out_ffn_block(x_ref[...], go_ref[...].astype(bf16), so_ref[...].astype(bf16),
                                g1_ref[...], sh2_ref[...],
                                sc2_ref[...], g2_ref[...], n2_ref[...], wout_ref, w1_ref, w2_ref)


def _resident(shape):
    return pl.BlockSpec(shape, lambda *_: (0,) * len(shape), pipeline_mode=pl.Buffered(1))


def _out_ffn(x, go, so, mod, row0, n2, wout, w1, w2, nb, tb):
    bsz, t, d = x.shape
    m = nb * tb
    nt = t // tb
    mods = [_mod_spec(nb, row0, chunk, lambda i, j: i)
            for chunk in (MOD_GATE1, MOD_SHIFT2, MOD_SCALE2, MOD_GATE2)]
    xspec = pl.BlockSpec((nb, tb, d), lambda i, j: (i, j, 0))
    return pl.pallas_call(
        _out_ffn_kernel,
        grid=(bsz // nb, nt),
        in_specs=[xspec,
                  pl.BlockSpec((m, GLA_WIDTH), lambda i, j: (i * nt + j, 0)),
                  pl.BlockSpec((m, SWA_Q), lambda i, j: (i * nt + j, 0)),
                  *mods,
                  pl.BlockSpec((1, 1, d), lambda i, j: (0, 0, 0)),
                  _resident((d, d)), _resident((d, D_FF)), _resident((D_FF, d))],
        out_specs=xspec,
        out_shape=jax.ShapeDtypeStruct((bsz, t, d), f32),
        compiler_params=pltpu.CompilerParams(
            dimension_semantics=("parallel", "parallel"), vmem_limit_bytes=VMEM_LIMIT),
        name="out_ffn",
    )(x, go, so, mod, mod, mod, mod, n2, wout, w1, w2)


def _prompt_kernel(sink_ref,
                   q_ref, k_ref, v_ref, la_ref, r_ref, sq_ref, k2_ref, v2_ref, k2p_ref, v2p_ref,
                   tri_ref, masks_ref, gnw_ref,
                   x_ref, g1_ref, sh2_ref, sc2_ref, g2_ref, n2_ref, wout_ref, w1_ref, w2_ref,
                   y_ref, sout_ref, state_ref, mix_ref, *, levels, nt, nsteps):
    s = pl.program_id(0)
    slot = s % 2
    c = GLA_CHUNK
    nchunk = q_ref.shape[0] // c

    @pl.when(s == 0)
    def _():
        mix_ref[1] = jnp.zeros(mix_ref.shape[1:], mix_ref.dtype)
        state_ref[...] = jnp.zeros_like(state_ref)

    j = jnp.minimum(s, nsteps - 1) % nt
    st_in = state_ref[...]
    st = jnp.where(j == 0, 0.0, st_in)
    gnw = gnw_ref[...]
    out = mix_ref.at[slot]
    mix = mix_ref[1 - slot]
    h, hb = _out_proj_norm(x_ref[...], mix[:, :GLA_WIDTH], mix[:, GLA_WIDTH:], g1_ref[...],
                           sh2_ref[...], sc2_ref[...], n2_ref[...], wout_ref)
    todo = list(range(D_FF // FFN_CHUNK_FUSED))
    parts = []

    def ffn_piece():
        if todo:
            parts.append(_ffn_piece(hb, w1_ref, w2_ref, todo.pop(0), FFN_CHUNK_FUSED))

    st = _gla_prompt_block(st, q_ref, k_ref, v_ref, la_ref, r_ref, tri_ref, masks_ref, gnw,
                           out.at[:, pl.ds(0, GLA_WIDTH)], levels, ffn_piece)
    _swa_prompt_blocks(sq_ref, k2p_ref[...], k2_ref, v2p_ref[...], v2_ref, j == 0, sink_ref,
                       out.at[:, pl.ds(GLA_WIDTH, SWA_Q)], ffn_piece)
    while todo:
        ffn_piece()
    st = jnp.where(s < nsteps, st, st_in)
    state_ref[...] = st
    sout_ref[0] = st
    y_ref[...] = h + g2_ref[...] * sum(parts[1:], parts[0]).reshape(h.shape)


def _prompt_mix_ffn(x, gq, gk, gv, la, gr, sq, k2, v2, gnw, sinks, mod, row0, n2, wout, w1, w2):
    bsz, t, d = x.shape
    tb = TOKEN_BLOCK
    nt = t // tb
    nsteps = bsz * nt
    levels, tri, masks = _gla_constants(GLA_CHUNK, GLA_CHUNK)
    per_blk = tb // WINDOW

    def mix_blk(s):
        return jnp.minimum(s, nsteps - 1)

    def ffn_blk(s):
        return jnp.maximum(s - 1, 0)

    def tok(width):
        return pl.BlockSpec((tb, width), lambda s, _: (mix_blk(s), 0))

    prev = pl.BlockSpec((WINDOW, SWA_KV2), lambda s, _: (jnp.maximum(mix_blk(s) * per_blk - 1, 0), 0))
    xspec = pl.BlockSpec((1, tb, d), lambda s, _: (ffn_blk(s) // nt, ffn_blk(s) % nt, 0))
    mods = [_mod_spec(1, row0, chunk, lambda s, _: ffn_blk(s) // nt)
            for chunk in (MOD_GATE1, MOD_SHIFT2, MOD_SCALE2, MOD_GATE2)]
    return pl.pallas_call(
        functools.partial(_prompt_kernel, levels=levels, nt=nt, nsteps=nsteps),
        grid_spec=pltpu.PrefetchScalarGridSpec(
            num_scalar_prefetch=1,
            grid=(nsteps + 1,),
            in_specs=[tok(GLA_QK), tok(GLA_QK), tok(GLA_WIDTH), tok(GLA_QK), tok(GLA_WIDTH),
                      tok(SWA_Q), tok(SWA_KV2), tok(SWA_KV2), prev, prev,
                      _resident(tri.shape), _resident(masks.shape), _resident((1, GLA_DV)),
                      xspec, *mods, _resident((1, 1, d)),
                      _resident((d, d)), _resident((d, D_FF)), _resident((D_FF, d))],
            out_specs=[xspec,
                       pl.BlockSpec((1, GLA_DV, GLA_QK), lambda s, _: (mix_blk(s) // nt, 0, 0))],
            scratch_shapes=[pltpu.VMEM((GLA_DV, GLA_QK), f32),
                            pltpu.VMEM((2, tb, GLA_WIDTH + SWA_Q), bf16)]),
        out_shape=[jax.ShapeDtypeStruct((bsz, t, d), f32),
                   jax.ShapeDtypeStruct((bsz, GLA_DV, GLA_QK), f32)],
        compiler_params=pltpu.CompilerParams(
            dimension_semantics=("arbitrary",), vmem_limit_bytes=VMEM_LIMIT),
        name="prompt_mix_ffn",
    )(sinks, gq, gk, gv, la, gr, sq, k2, v2, k2, v2, tri, masks, gnw,
      x, mod, mod, mod, mod, n2, wout, w1, w2)


def _rope_tables(pos):
    half = HEAD_DIM // 2
    inv = jnp.power(ROPE_THETA, -jnp.arange(half, dtype=f32) * 2.0 / HEAD_DIM)
    ang = pos.astype(f32)[:, None] * inv[None, :]
    cos = jnp.cos(ang)
    sin = jnp.sin(ang)
    reps = LANES // HEAD_DIM
    return (jnp.tile(jnp.concatenate([cos, cos], axis=-1), (1, reps)),
            jnp.tile(jnp.concatenate([-sin, sin], axis=-1), (1, reps)))


def _block_diag_ones(n, blk):
    idx = np.arange(n) // blk
    return jnp.asarray((idx[:, None] == idx[None, :]).astype(np.float32), bf16)


def _layer_weights(w_in, w_gate_up, b_gate, q_norm_w, k_norm_w, w_out, w_ff1, w_ff2):
    splits = np.cumsum([GLA_QK, GLA_QK, GLA_WIDTH, GLA_WIDTH, GLA_GATE_RANK, SWA_Q, SWA_KV])
    gq, gk, gv, gr, glr, sq, sk, sv = jnp.split(w_in, [int(s) for s in splits], axis=1)
    glr = jnp.pad(glr, ((0, 0), (0, LANES - GLA_GATE_RANK)))
    win = jnp.concatenate([gq, gk, gv, gr, sq, sk, sv, glr], axis=1).astype(bf16)
    wgu = jnp.pad(w_gate_up, ((0, LANES - GLA_GATE_RANK), (0, 0))).astype(bf16)
    return dict(
        win=win, wgu=wgu, bg=b_gate.reshape(1, GLA_QK),
        qnw=jnp.tile(q_norm_w, N_Q_HEADS).reshape(1, SWA_Q),
        knw=jnp.tile(k_norm_w, N_KV_HEADS).reshape(1, SWA_KV),
        wout=w_out.astype(bf16), w1=w_ff1.astype(bf16), w2=w_ff2.astype(bf16))


def _decoder_layer(x, mod, row0, pos, state, past_k, past_v, lw, n1, n2, gnw, sinks, bdq, bdk):
    bsz, t, d = x.shape
    cos_t, sin_t = _rope_tables(pos)
    prompt = state is None
    if prompt:
        nb, tb, tail, act, groups = 1, TOKEN_BLOCK, WINDOW, bf16, PROMPT_INPROJ_GROUPS
    else:
        nb, tb, tail, act, groups = SAMPLE_ROWS // t, t, SAMPLE_ROWS, f32, 1
        cos_t = jnp.tile(cos_t, (nb, 1))
        sin_t = jnp.tile(sin_t, (nb, 1))
    n2 = n2.reshape(1, 1, d)
    gq, gk, gv, gr, la, sq, k2, v2, kk, vk = _inproj(
        x, mod, row0, n1.reshape(1, 1, d), lw["win"], lw["wgu"], lw["bg"], lw["qnw"], lw["knw"],
        bdq, bdk, cos_t, sin_t, nb, tb, tail, act, groups)
    if prompt:
        y, s_t = _prompt_mix_ffn(x, gq, gk, gv, la, gr, sq, k2, v2, gnw, sinks, mod, row0, n2,
                                 lw["wout"], lw["w1"], lw["w2"])
        s_new = s_t.reshape(bsz, GLA_DV, GLA_HEADS, GLA_DK).transpose(0, 2, 3, 1)
        k_keep, v_keep = kk, vk
    else:
        go, s_new = _gla_sample(gq, gk, gv, la, gr, state.reshape(bsz, GLA_QK, GLA_DV), gnw, bsz, t)
        so, kt, vt = _swa_sample(sq, kk, vk, past_k.transpose(0, 2, 3, 1), past_v.transpose(0, 2, 3, 1),
                                 sinks, bsz, t)
        k_keep, v_keep = kt.transpose(0, 3, 1, 2), vt.transpose(0, 3, 1, 2)
        y = _out_ffn(x, go, so, mod, row0, n2, lw["wout"], lw["w1"], lw["w2"], nb, tb)
    return (y, s_new.reshape(bsz, GLA_HEADS, GLA_DK, GLA_DV),
            k_keep.reshape(bsz, WINDOW, N_KV_HEADS, HEAD_DIM),
            v_keep.reshape(bsz, WINDOW, N_KV_HEADS, HEAD_DIM))


def kernel(x_prompt, x_sample, state_gla, cache_swa_k, cache_swa_v, c_prompt, c_sample, w_ada, b_ada, norm1_w, norm2_w, w_in, w_gate_up, b_gate, gla_norm_w, q_norm_w, k_norm_w, sinks, w_out, w_ff1, w_ff2):
    depth = w_ada.shape[0]
    bp, tp, _ = x_prompt.shape
    bs, ts, _ = x_sample.shape
    pos_p = jnp.arange(tp)
    pos_s = PAST_LEN + jnp.arange(ts)
    bdq = _block_diag_ones(SWA_Q, HEAD_DIM)
    bdk = _block_diag_ones(SWA_KV, HEAD_DIM)
    c_all = jnp.concatenate([c_sample, c_prompt], axis=0)
    yp, ys = x_prompt, x_sample
    outs = [[] for _ in range(6)]
    for l in range(depth):
        mod = _modulation(c_all, w_ada[l], b_ada[l])
        lw = _layer_weights(w_in[l], w_gate_up[l], b_gate[l], q_norm_w[l], k_norm_w[l],
                            w_out[l], w_ff1[l], w_ff2[l])
        gnw = gla_norm_w[l].reshape(1, GLA_DV)
        common = (lw, norm1_w[l], norm2_w[l], gnw, sinks[l], bdq, bdk)
        yp, gp, kp, vp = _decoder_layer(yp, mod, bs, pos_p, None, None, None, *common)
        ys, gs, kq, vq = _decoder_layer(ys, mod, 0, pos_s, state_gla[l], cache_swa_k[l],
                                        cache_swa_v[l], *common)
        for lst, val in zip(outs, (gp, kp, vp, gs, kq, vq)):
            lst.append(val)
    return (yp, ys) + tuple(jnp.stack(o) for o in outs)
```

```python
import functools

import jax
import jax.numpy as jnp
import numpy as np
from jax import lax
from jax.experimental import pallas as pl
from jax.experimental.pallas import tpu as pltpu

f32 = jnp.float32
bf16 = jnp.bfloat16

D_MODEL = 1024
GLA_HEADS = 4
GLA_DK = 64
GLA_DV = 128
GLA_QK = GLA_HEADS * GLA_DK
GLA_WIDTH = GLA_HEADS * GLA_DV
GLA_GATE_RANK = 16
GLA_TAU = 16.0
LOG2E = 1.4426950408889634
HEAD_DIM = 64
N_Q_HEADS = 8
N_KV_HEADS = 2
SWA_Q = N_Q_HEADS * HEAD_DIM
SWA_KV = N_KV_HEADS * HEAD_DIM
SWA_KV2 = 2 * SWA_KV
WINDOW = 128
ROPE_THETA = 10000.0
PAST_LEN = 8192
D_FF = 4 * D_MODEL
EPS = 1e-6
LANES = 128
GLA_CHUNK = 128
PROMPT_GLA_CHUNK = 128
TOKEN_BLOCK = 512
SAMPLE_ROWS = 256
PROMPT_INPROJ_GROUPS = 2
VMEM_LIMIT = 56 * 1024 * 1024

_SEG = {}
_off = 0
for _name, _w in (("gq", GLA_QK), ("gk", GLA_QK), ("gv", GLA_WIDTH), ("gr", GLA_WIDTH),
                  ("sq", SWA_Q), ("sk", SWA_KV), ("sv", SWA_KV), ("glr", LANES)):
    _SEG[_name] = (_off, _off + _w)
    _off += _w
IN_WIDTH_PADDED = _off


def _dot(a, b):
    return jnp.dot(a, b, preferred_element_type=f32)


def _dot_nt(a, b):
    return lax.dot_general(a, b, (((1,), (1,)), ((), ())), preferred_element_type=f32)


def _dot_tn(a, b):
    return lax.dot_general(a, b, (((0,), (0,)), ((), ())), preferred_element_type=f32)


def _sigmoid(x):
    return 1.0 / (1.0 + jnp.exp(-x))


def _mod_kernel(c_ref, w_ref, b_ref, o_ref):
    c = c_ref[...]
    s = (c * _sigmoid(c)).astype(bf16)
    res = _dot(s, w_ref[...].astype(bf16)) + b_ref[...]
    for r in range(res.shape[0]):
        o_ref[r] = res[r:r + 1, :]


def _modulation(c_all, w_ada, b_ada):
    m = c_all.shape[0]
    n = w_ada.shape[1]
    bn = 1536
    return pl.pallas_call(
        _mod_kernel,
        grid=(n // bn,),
        in_specs=[pl.BlockSpec((m, D_MODEL), lambda j: (0, 0)),
                  pl.BlockSpec((D_MODEL, bn), lambda j: (0, j)),
                  pl.BlockSpec((1, bn), lambda j: (0, j))],
        out_specs=pl.BlockSpec((m, 1, bn), lambda j: (0, 0, j)),
        out_shape=jax.ShapeDtypeStruct((m, 1, n), f32),
        compiler_params=pltpu.CompilerParams(vmem_limit_bytes=VMEM_LIMIT),
        name="adaln_mod",
    )(c_all, w_ada, b_ada.reshape(1, n))


def _group_rms(x, bd_ref, w_ref):
    ssq = _dot((x * x).astype(bf16), bd_ref[...])
    return x * lax.rsqrt(ssq * (1.0 / HEAD_DIM) + EPS) * w_ref[...]


def _rope(x, cos, sin_signed, low_half):
    partner = jnp.where(low_half, pltpu.roll(x, LANES - 32, axis=1), pltpu.roll(x, 32, axis=1))
    return x * cos + partner * sin_signed


def _dup_heads(x, low_lanes):
    rolled = pltpu.roll(x, HEAD_DIM, axis=1)
    return jnp.where(low_lanes, x, rolled), jnp.where(low_lanes, rolled, x)


def _inproj_kernel(x_ref, sh_ref, sc_ref, n1_ref, win_ref, wgu_ref, bg_ref, qnw_ref, knw_ref,
                   bdq_ref, bdk_ref, cos_ref, sin_ref,
                   gq_ref, gk_ref, gv_ref, gr_ref, la_ref, sq_ref, k2_ref, v2_ref, kk_ref, vk_ref,
                   *, groups):
    nb, t, d = x_ref.shape
    m = nb * t
    tail = kk_ref.shape[0]
    mg = m // groups
    lane = lax.broadcasted_iota(jnp.int32, (mg, LANES), 1)
    low_half = (lane & 32) == 0
    low_lanes = lane < HEAD_DIM
    for grp in range(groups):
        rows = slice(grp * mg, (grp + 1) * mg)
        if nb == 1:
            x = x_ref[:, rows, :]
            sc, sh = sc_ref[...], sh_ref[...]
        else:
            seqs = slice(grp * (nb // groups), (grp + 1) * (nb // groups))
            x = x_ref[seqs]
            sc, sh = sc_ref[seqs], sh_ref[seqs]
        ms = jnp.mean(x * x, axis=-1, keepdims=True)
        hn = x * lax.rsqrt(ms + EPS) * n1_ref[...]
        hn = hn * (1.0 + sc) + sh
        hb = hn.reshape(mg, d).astype(bf16)

        def seg(name):
            a, b = _SEG[name]
            return _dot(hb, win_ref[:, a:b])

        cos = cos_ref[rows, :]
        sin = sin_ref[rows, :]
        sq = _group_rms(seg("sq"), bdq_ref, qnw_ref)
        for c in range(SWA_Q // LANES):
            blk = _rope(sq[:, c * LANES:(c + 1) * LANES], cos, sin, low_half)
            sq_ref[rows, c * LANES:(c + 1) * LANES] = (blk * (HEAD_DIM ** -0.5)).astype(sq_ref.dtype)
        sk = _rope(_group_rms(seg("sk"), bdk_ref, knw_ref), cos, sin, low_half)
        sv = seg("sv")
        keep = (grp + 1) * mg - (m - tail)
        if keep > 0:
            keep = min(keep, mg)
            dst_rows = slice((grp + 1) * mg - keep - (m - tail), (grp + 1) * mg - (m - tail))
            kk_ref[dst_rows, :] = sk[mg - keep:]
            vk_ref[dst_rows, :] = sv[mg - keep:]
        for src, dst in ((sk, k2_ref), (sv, v2_ref)):
            d0, d1 = _dup_heads(src, low_lanes)
            dst[rows, :LANES] = d0.astype(dst.dtype)
            dst[rows, LANES:] = d1.astype(dst.dtype)
        glr = seg("glr").astype(bf16)
        g = _dot(glr, wgu_ref[...]) + bg_ref[...]
        log_sig = jnp.minimum(g, 0.0) - jnp.log1p(jnp.exp(-jnp.abs(g)))
        la_ref[rows, :] = log_sig * (LOG2E / GLA_TAU)
        gq_ref[rows, :] = seg("gq") * (GLA_DK ** -0.5)
        gk_ref[rows, :] = seg("gk")
        gv_ref[rows, :] = seg("gv").astype(gv_ref.dtype)
        gr_ref[rows, :] = seg("gr").astype(gr_ref.dtype)


MOD_SHIFT1, MOD_SCALE1, MOD_GATE1, MOD_SHIFT2, MOD_SCALE2, MOD_GATE2 = range(6)


def _mod_spec(nb, row0, chunk, batch_block):
    return pl.BlockSpec((nb, 1, D_MODEL), lambda *ids: (row0 // nb + batch_block(*ids), 0, chunk))


def _inproj(x, mod, row0, n1, win, wgu, bg, qnw, knw, bdq, bdk, cos_t, sin_t, nb, tb, tail, act, groups):
    bsz, t, d = x.shape
    m = nb * tb
    nt = t // tb
    grid = (bsz // nb, nt)
    tok = bsz * t

    def full(shape):
        return pl.BlockSpec(shape, lambda i, j: (0,) * len(shape))

    def out(width):
        return pl.BlockSpec((m, width), lambda i, j: (i * nt + j, 0))

    tail_spec = pl.BlockSpec((tail, SWA_KV), lambda i, j: (i, 0))
    outs = ((GLA_QK, f32), (GLA_QK, f32), (GLA_WIDTH, act), (GLA_WIDTH, act), (GLA_QK, f32),
            (SWA_Q, act), (SWA_KV2, bf16), (SWA_KV2, bf16))
    ntail = (bsz // nb) * tail
    return pl.pallas_call(
        functools.partial(_inproj_kernel, groups=groups),
        grid=grid,
        in_specs=[pl.BlockSpec((nb, tb, d), lambda i, j: (i, j, 0)),
                  _mod_spec(nb, row0, MOD_SHIFT1, lambda i, j: i),
                  _mod_spec(nb, row0, MOD_SCALE1, lambda i, j: i),
                  full((1, 1, d)),
                  full((d, IN_WIDTH_PADDED)),
                  full((LANES, GLA_QK)),
                  full((1, GLA_QK)),
                  full((1, SWA_Q)),
                  full((1, SWA_KV)),
                  full((SWA_Q, SWA_Q)),
                  full((SWA_KV, SWA_KV)),
                  pl.BlockSpec((m, LANES), lambda i, j: (j, 0)),
                  pl.BlockSpec((m, LANES), lambda i, j: (j, 0))],
        out_specs=[out(w) for w, _ in outs] + [tail_spec, tail_spec],
        out_shape=[jax.ShapeDtypeStruct((tok, w), dt) for w, dt in outs]
        + [jax.ShapeDtypeStruct((ntail, SWA_KV), f32)] * 2,
        compiler_params=pltpu.CompilerParams(
            dimension_semantics=("parallel", "arbitrary"), vmem_limit_bytes=VMEM_LIMIT),
        name="inproj",
    )(x, mod, mod, n1, win, wgu, bg, qnw, knw, bdq, bdk, cos_t, sin_t)


def _gla_constants(chunk, seq):
    t = np.arange(chunk)
    levels = []
    m = 1
    while m < seq:
        levels.append(m)
        m *= 2
    masks = [np.eye(chunk, dtype=bool)]
    for m in levels:
        upper = (t % (2 * m) >= m)[:, None]
        lower = (t % (2 * m) < m)[None, :]
        same = (t[:, None] // (2 * m)) == (t[None, :] // (2 * m))
        masks.append(same & upper & lower)
    tri = t[None, :] <= t[:, None]
    tiled = np.tile(np.stack(masks).astype(np.float32), (1, 1, GLA_HEADS))
    return tuple(levels), jnp.asarray(tri.astype(np.float32), bf16), jnp.asarray(tiled)


def _stack_heads(xb, lane_head, axis):
    zero = jnp.zeros_like(xb)
    return jnp.concatenate([jnp.where(lane_head == h, xb, zero) for h in range(GLA_HEADS)], axis=axis)


def _block_sums(la, bcum, m, row, rolls):
    c, n = la.shape
    if m == 1:
        return la, None
    if m < 8:
        def rolled(shift):
            if shift not in rolls:
                rolls[shift] = pltpu.roll(la, shift % c, axis=0)
            return rolls[shift]
        pos = row & (m - 1)
        pre = la
        suf = None
        for j in range(1, m):
            pre = pre + jnp.where(pos >= j, rolled(j), 0.0)
            term = jnp.where(pos < m - j, rolled(-j), 0.0)
            suf = term if suf is None else suf + term
        return pre, suf
    before, last = [], []
    for i in range(c // m):
        before.append(jnp.zeros((m, n), f32) if i == 0
                      else jnp.broadcast_to(bcum[i * m - 1:i * m], (m, n)))
        last.append(jnp.broadcast_to(bcum[(i + 1) * m - 1:(i + 1) * m], (m, n)))
    if len(before) == 1:
        return bcum - before[0], last[0] - bcum
    return bcum - jnp.concatenate(before, axis=0), jnp.concatenate(last, axis=0) - bcum


def _gla_scores(q_ref, k_ref, v_ref, la_ref, tri_ref, masks_ref, levels, seq):
    c = q_ref.shape[0]
    la = la_ref[...]
    hi = la.astype(bf16)
    lo = (la - hi.astype(f32)).astype(bf16)
    tri = tri_ref[...]
    bcum = _dot(tri, hi) + _dot(tri, lo)
    q = q_ref[...]
    k = k_ref[...]
    vb = v_ref[...].astype(bf16)
    row = lax.broadcasted_iota(jnp.int32, (c, GLA_QK), 0)
    lane_head = lax.broadcasted_iota(jnp.int32, (c, GLA_QK), 1) >> 6
    rolls = {}
    attn = None
    for lvl, m in enumerate((0,) + levels):
        if m == 0:
            qt, kt = q, k
        else:
            pre, suf = _block_sums(la, bcum, m, row, rolls)
            qt = q * jnp.exp2(pre)
            kt = k if suf is None else k * jnp.exp2(suf)
        r = _dot_nt(qt.astype(bf16), _stack_heads(kt.astype(bf16), lane_head, 0))
        rm = r * masks_ref[lvl]
        attn = rm if attn is None else attn + rm
    a = attn.astype(bf16)
    pre, suf = _block_sums(la, bcum, seq, row, rolls)
    qf = (q * jnp.exp2(pre)).astype(bf16)
    kf = (k * jnp.exp2(suf)).astype(bf16)
    return a, qf, kf, vb, bcum, hi, lo


def _gla_values(a, vb):
    v_head = lax.broadcasted_iota(jnp.int32, vb.shape, 1) >> 7
    return _dot(a, _stack_heads(vb, v_head, 0))


def _gla_finish(o, r_ref, gnw, go_ref):
    for h in range(GLA_HEADS):
        sl = slice(h * GLA_DV, (h + 1) * GLA_DV)
        oh = o[:, sl]
        r = r_ref[:, sl].astype(f32)
        ms = jnp.mean(oh * oh, axis=-1, keepdims=True)
        go_ref[:, sl] = (oh * lax.rsqrt(ms + EPS) * gnw * (r * _sigmoid(r))).astype(go_ref.dtype)


def _gla_prompt_block(st, q_ref, k_ref, v_ref, la_ref, r_ref, tri_ref, masks_ref, gnw, go_ref, levels,
                      between=lambda: None):
    c = PROMPT_GLA_CHUNK
    nchunk = q_ref.shape[0] // c
    lane_head = lax.broadcasted_iota(jnp.int32, (GLA_DV, GLA_QK), 1) >> 6
    chunks = []
    for ci in range(nchunk):
        rows = pl.ds(ci * c, c)
        chunks.append(_gla_scores(q_ref.at[rows], k_ref.at[rows], v_ref.at[rows], la_ref.at[rows],
                                  tri_ref, masks_ref, levels, c))
        between()
    intra, upds, decays = [], [], []
    for a, _, kf, vb, bcum, _, _ in chunks:
        intra.append(_gla_values(a, vb))
        full = _dot_tn(vb, kf)
        upd = None
        for h in range(GLA_HEADS):
            term = jnp.where(lane_head == h, full[h * GLA_DV:(h + 1) * GLA_DV], 0.0)
            upd = term if upd is None else upd + term
        upds.append(upd)
        decays.append(jnp.exp2(bcum[c - 1:c]))
    between()
    for ci in range(nchunk):
        rows = pl.ds(ci * c, c)
        sbd_t = _stack_heads(st.astype(bf16), lane_head, 0)
        o = intra[ci] + _dot_nt(chunks[ci][1], sbd_t)
        _gla_finish(o, r_ref.at[rows], gnw, go_ref.at[rows])
        st = decays[ci] * st + upds[ci]
    return st


def _gla_sample_kernel(q_ref, k_ref, v_ref, la_ref, r_ref, s0_ref, tri_ref, masks_ref, gnw_ref,
                       go_ref, sout_ref, *, levels, seq):
    c = q_ref.shape[0]
    a, qf, kf, vb, _, hi, lo = _gla_scores(q_ref, k_ref, v_ref, la_ref, tri_ref, masks_ref, levels, seq)
    o_intra = _gla_values(a, vb)
    row_head = lax.broadcasted_iota(jnp.int32, (GLA_QK, GLA_DV), 0) >> 6
    ones = jnp.ones((seq, GLA_DV), bf16)
    inter = []
    for b in range(c // seq):
        rows = slice(b * seq, (b + 1) * seq)
        s_old = s0_ref[b]
        sbd = _stack_heads(s_old.astype(bf16), row_head, 1)
        inter.append(_dot(qf[rows], sbd))
        p = _dot_tn(kf[rows], vb[rows])
        upd = None
        for h in range(GLA_HEADS):
            term = jnp.where(row_head == h, p[:, h * GLA_DV:(h + 1) * GLA_DV], 0.0)
            upd = term if upd is None else upd + term
        total = _dot_tn(hi[rows], ones) + _dot_tn(lo[rows], ones)
        sout_ref[b] = jnp.exp2(total) * s_old + upd
    o = o_intra + jnp.concatenate(inter, axis=0)
    _gla_finish(o, r_ref, gnw_ref[...], go_ref)


def _gla_sample(gq, gk, gv, la, gr, state, gnw, bsz, t):
    c = GLA_CHUNK
    nseq = c // t
    levels, tri, masks = _gla_constants(c, t)

    def tok(width):
        return pl.BlockSpec((c, width), lambda i: (i, 0))

    return pl.pallas_call(
        functools.partial(_gla_sample_kernel, levels=levels, seq=t),
        grid=(bsz // nseq,),
        in_specs=[tok(GLA_QK), tok(GLA_QK), tok(GLA_WIDTH), tok(GLA_QK), tok(GLA_WIDTH),
                  pl.BlockSpec((nseq, GLA_QK, GLA_DV), lambda i: (i, 0, 0)),
                  pl.BlockSpec(tri.shape, lambda i: (0, 0)),
                  pl.BlockSpec(masks.shape, lambda i: (0, 0, 0)),
                  pl.BlockSpec((1, GLA_DV), lambda i: (0, 0))],
        out_specs=[tok(GLA_WIDTH),
                   pl.BlockSpec((nseq, GLA_QK, GLA_DV), lambda i: (i, 0, 0))],
        out_shape=[jax.ShapeDtypeStruct((bsz * t, GLA_WIDTH), f32),
                   jax.ShapeDtypeStruct((bsz, GLA_QK, GLA_DV), f32)],
        compiler_params=pltpu.CompilerParams(
            dimension_semantics=("parallel",), vmem_limit_bytes=VMEM_LIMIT),
        name="gla_sample",
    )(gq, gk, gv, la, gr, state, tri, masks, gnw)


def _sink_attention(q_ref, rows, kdups, vdups, mask, sink_ref, o_ref, between=lambda: None):
    n = len(kdups)
    heads_per_group = N_Q_HEADS // N_KV_HEADS
    low_q = lax.broadcasted_iota(jnp.int32, (rows, LANES), 1) < HEAD_DIM
    scores = []
    for i in range(n):
        per_group = []
        for g in range(N_KV_HEADS):
            stack = []
            for p in range(heads_per_group // 2):
                pair = g * (heads_per_group // 2) + p
                qp = q_ref[i * rows:(i + 1) * rows, pair * LANES:(pair + 1) * LANES].astype(bf16)
                zero = jnp.zeros_like(qp)
                stack.append(jnp.where(low_q, qp, zero))
                stack.append(jnp.where(low_q, zero, qp))
            per_group.append(_dot_nt(jnp.concatenate(stack, axis=0), kdups[i][g]))
        scores.append(per_group)
    between()
    probs, inv = [], []
    for head in range(N_Q_HEADS):
        g, hh = divmod(head, heads_per_group)
        parts = [scores[i][g][hh * rows:(hh + 1) * rows] for i in range(n)]
        s = parts[0] if n == 1 else jnp.concatenate(parts, axis=0)
        sink = sink_ref[head]
        s = jnp.where(mask, s, -1e30)
        mx = jnp.maximum(jnp.max(s, axis=-1, keepdims=True), sink)
        p = jnp.exp(s - mx)
        inv.append(1.0 / (jnp.sum(p, axis=-1, keepdims=True) + jnp.exp(sink - mx)))
        probs.append(p.astype(bf16))
    between()
    outs = []
    for i in range(n):
        per_group = []
        for g in range(N_KV_HEADS):
            p_i = jnp.concatenate([probs[g * heads_per_group + hh][i * rows:(i + 1) * rows]
                                   for hh in range(heads_per_group)], axis=0)
            per_group.append(_dot(p_i, vdups[i][g]))
        outs.append(per_group)
    low_all = lax.broadcasted_iota(jnp.int32, (n * rows, LANES), 1) < HEAD_DIM
    for pair in range(N_Q_HEADS // 2):
        halves = []
        for head in (2 * pair, 2 * pair + 1):
            g, hh = divmod(head, heads_per_group)
            parts = [outs[i][g][hh * rows:(hh + 1) * rows] for i in range(n)]
            o = parts[0] if n == 1 else jnp.concatenate(parts, axis=0)
            halves.append(o * inv[head])
        o_ref[:, pair * LANES:(pair + 1) * LANES] = jnp.where(low_all, halves[0], halves[1]).astype(o_ref.dtype)


def _swa_prompt_blocks(q_ref, k_before, k2_ref, v_before, v2_ref, first, sink_ref, o_ref,
                       between=lambda: None):
    w = WINDOW
    nblk = q_ref.shape[0] // w

    def dup(before, ref):
        blocks = [before] + [ref[i * w:(i + 1) * w, :] for i in range(nblk)]
        return [[jnp.concatenate([blocks[i][:, g * LANES:(g + 1) * LANES],
                                  blocks[i + 1][:, g * LANES:(g + 1) * LANES]], axis=0)
                 for g in range(N_KV_HEADS)] for i in range(nblk)]

    row = lax.broadcasted_iota(jnp.int32, (nblk * w, 2 * w), 0)
    tk = lax.broadcasted_iota(jnp.int32, (nblk * w, 2 * w), 1)
    rel = tk - (row & (w - 1))
    first_key = jnp.where(row < w, jnp.where(first, w, 0), 0)
    mask = (rel > 0) & (rel <= w) & (tk >= first_key)
    _sink_attention(q_ref, w, dup(k_before, k2_ref), dup(v_before, v2_ref), mask, sink_ref, o_ref, between)


def _swa_sample_kernel(sink_ref, q_ref, kn_ref, vn_ref, pk_ref, pv_ref, o_ref, ko_ref, vo_ref, *, seq):
    nseq, _, _, w = pk_ref.shape
    rows = nseq * seq
    heads_per_group = N_Q_HEADS // N_KV_HEADS
    lane = lax.broadcasted_iota(jnp.int32, (rows, LANES), 1)
    r_id = lax.broadcasted_iota(jnp.int32, (rows, LANES), 0)
    low = lane < HEAD_DIM
    pos = r_id & (seq - 1)
    mask_old = lane > pos
    seq_shift = seq.bit_length() - 1
    mask_new = ((r_id >> seq_shift) == (lane >> seq_shift)) & ((lane & (seq - 1)) <= pos)
    kn = kn_ref[...]
    vn = vn_ref[...]
    kn_dup = [d.astype(bf16) for d in _dup_heads(kn, low)]
    vn_dup = [d.astype(bf16) for d in _dup_heads(vn, low)]

    pad = jnp.zeros((w - seq, LANES), f32)
    tail_lanes = lax.broadcasted_iota(jnp.int32, (HEAD_DIM, w), 1) >= w - seq
    kt_dup, vt_dup = [], []
    for b in range(nseq):
        tok = slice(b * seq, (b + 1) * seq)
        per_k, per_v = [], []
        for new, old_ref, out_ref, per in ((kn, pk_ref, ko_ref, per_k), (vn, pv_ref, vo_ref, per_v)):
            new_t = jnp.concatenate([pad, new[tok]], axis=0).T
            for g in range(N_KV_HEADS):
                old = old_ref[b, g]
                out_ref[b, g] = jnp.where(tail_lanes, new_t[g * HEAD_DIM:(g + 1) * HEAD_DIM],
                                          pltpu.roll(old, w - seq, axis=1))
                ob = old.astype(bf16)
                per.append(jnp.concatenate([ob, ob], axis=0))
        kt_dup.append(per_k)
        vt_dup.append(per_v)

    lhs = []
    for g in range(N_KV_HEADS):
        stack = []
        for p in range(heads_per_group // 2):
            pair = g * (heads_per_group // 2) + p
            qp = q_ref[:, pair * LANES:(pair + 1) * LANES].astype(bf16)
            zero = jnp.zeros_like(qp)
            stack.append(jnp.where(low, qp, zero))
            stack.append(jnp.where(low, zero, qp))
        lhs.append(jnp.concatenate(stack, axis=0))
    s_new = [_dot_nt(lhs[g], kn_dup[g]) for g in range(N_KV_HEADS)]
    s_old = []
    for b in range(nseq):
        per = []
        for g in range(N_KV_HEADS):
            qb = jnp.concatenate([lhs[g][hh * rows + b * seq:hh * rows + (b + 1) * seq]
                                  for hh in range(heads_per_group)], axis=0)
            per.append(_dot(qb, kt_dup[b][g]))
        s_old.append(per)

    p_old, p_new, inv = [], [], []
    for head in range(N_Q_HEADS):
        g, hh = divmod(head, heads_per_group)
        so = jnp.concatenate([s_old[b][g][hh * seq:(hh + 1) * seq] for b in range(nseq)], axis=0)
        so = jnp.where(mask_old, so, -1e30)
        sn = jnp.where(mask_new, s_new[g][hh * rows:(hh + 1) * rows], -1e30)
        sink = sink_ref[head]
        mx = jnp.maximum(jnp.maximum(jnp.max(so, axis=-1, keepdims=True),
                                     jnp.max(sn, axis=-1, keepdims=True)), sink)
        po = jnp.exp(so - mx)
        pn = jnp.exp(sn - mx)
        inv.append(1.0 / (jnp.sum(po, axis=-1, keepdims=True) + jnp.sum(pn, axis=-1, keepdims=True)
                          + jnp.exp(sink - mx)))
        p_old.append(po.astype(bf16))
        p_new.append(pn.astype(bf16))

    o_new = [_dot(jnp.concatenate([p_new[g * heads_per_group + hh] for hh in range(heads_per_group)], axis=0),
                  vn_dup[g]) for g in range(N_KV_HEADS)]
    o_old = []
    for b in range(nseq):
        per = []
        for g in range(N_KV_HEADS):
            pb = jnp.concatenate([p_old[g * heads_per_group + hh][b * seq:(b + 1) * seq]
                                  for hh in range(heads_per_group)], axis=0)
            per.append(_dot_nt(pb, vt_dup[b][g]))
        o_old.append(per)

    for pair in range(N_Q_HEADS // 2):
        halves = []
        for head in (2 * pair, 2 * pair + 1):
            g, hh = divmod(head, heads_per_group)
            old = jnp.concatenate([o_old[b][g][hh * seq:(hh + 1) * seq] for b in range(nseq)], axis=0)
            halves.append((old + o_new[g][hh * rows:(hh + 1) * rows]) * inv[head])
        o_ref[:, pair * LANES:(pair + 1) * LANES] = jnp.where(low, halves[0], halves[1])


def _swa_sample(sq, sk, sv, past_kt, past_vt, sinks, bsz, t):
    nseq = SAMPLE_ROWS // 2 // t
    assert nseq * t == LANES
    tok = lambda width: pl.BlockSpec((nseq * t, width), lambda i, s: (i, 0))
    cache = pl.BlockSpec((nseq,) + past_kt.shape[1:], lambda i, s: (i, 0, 0, 0))
    return pl.pallas_call(
        functools.partial(_swa_sample_kernel, seq=t),
        grid_spec=pltpu.PrefetchScalarGridSpec(
            num_scalar_prefetch=1,
            grid=(bsz // nseq,),
            in_specs=[tok(SWA_Q), tok(SWA_KV), tok(SWA_KV), cache, cache],
            out_specs=[tok(SWA_Q), cache, cache]),
        out_shape=[jax.ShapeDtypeStruct((bsz * t, SWA_Q), f32),
                   jax.ShapeDtypeStruct(past_kt.shape, f32),
                   jax.ShapeDtypeStruct(past_vt.shape, f32)],
        compiler_params=pltpu.CompilerParams(
            dimension_semantics=("parallel",), vmem_limit_bytes=VMEM_LIMIT),
        name="swa_sample",
    )(sinks, sq, sk, sv, past_kt, past_vt)


FFN_CHUNK = 1024
FFN_CHUNK_FUSED = 512


def _out_proj_norm(x, go, so, g1, sh2, sc2, n2, wout_ref):
    nb, t, d = x.shape
    mixed = _dot(go, wout_ref[:GLA_WIDTH, :]) + _dot(so, wout_ref[GLA_WIDTH:, :])
    h = x + g1 * mixed.reshape(nb, t, d)
    ms = jnp.mean(h * h, axis=-1, keepdims=True)
    hn = h * lax.rsqrt(ms + EPS) * n2
    hn = hn * (1.0 + sc2) + sh2
    return h, hn.reshape(nb * t, d).astype(bf16)


def _ffn_piece(hb, w1_ref, w2_ref, c, width):
    cols = slice(c * width, (c + 1) * width)
    a = jnp.maximum(_dot(hb, w1_ref[:, cols]), 0.0)
    return _dot((a * a).astype(bf16), w2_ref[cols, :])


def _out_ffn_block(x, go, so, g1, sh2, sc2, g2, n2, wout_ref, w1_ref, w2_ref):
    h, hb = _out_proj_norm(x, go, so, g1, sh2, sc2, n2, wout_ref)
    ff = _ffn_piece(hb, w1_ref, w2_ref, 0, FFN_CHUNK)
    for c in range(1, D_FF // FFN_CHUNK):
        ff = ff + _ffn_piece(hb, w1_ref, w2_ref, c, FFN_CHUNK)
    return h + g2 * ff.reshape(h.shape)


def _out_ffn_kernel(x_ref, go_ref, so_ref, g1_ref, sh2_ref, sc2_ref, g2_ref, n2_ref,
                    wout_ref, w1_ref, w2_ref, y_ref):
    y_ref[...] = _out_ffn_block(x_ref[...], go_ref[...].astype(bf16), so_ref[...].astype(bf16),
                                g1_ref[...], sh2_ref[...],
                                sc2_ref[...], g2_ref[...], n2_ref[...], wout_ref, w1_ref, w2_ref)


def _resident(shape):
    return pl.BlockSpec(shape, lambda *_: (0,) * len(shape), pipeline_mode=pl.Buffered(1))


def _out_ffn(x, go, so, mod, row0, n2, wout, w1, w2, nb, tb):
    bsz, t, d = x.shape
    m = nb * tb
    nt = t // tb
    mods = [_mod_spec(nb, row0, chunk, lambda i, j: i)
            for chunk in (MOD_GATE1, MOD_SHIFT2, MOD_SCALE2, MOD_GATE2)]
    xspec = pl.BlockSpec((nb, tb, d), lambda i, j: (i, j, 0))
    return pl.pallas_call(
        _out_ffn_kernel,
        grid=(bsz // nb, nt),
        in_specs=[xspec,
                  pl.BlockSpec((m, GLA_WIDTH), lambda i, j: (i * nt + j, 0)),
                  pl.BlockSpec((m, SWA_Q), lambda i, j: (i * nt + j, 0)),
                  *mods,
                  pl.BlockSpec((1, 1, d), lambda i, j: (0, 0, 0)),
                  _resident((d, d)), _resident((d, D_FF)), _resident((D_FF, d))],
        out_specs=xspec,
        out_shape=jax.ShapeDtypeStruct((bsz, t, d), f32),
        compiler_params=pltpu.CompilerParams(
            dimension_semantics=("parallel", "parallel"), vmem_limit_bytes=VMEM_LIMIT),
        name="out_ffn",
    )(x, go, so, mod, mod, mod, mod, n2, wout, w1, w2)


def _prompt_kernel(sink_ref,
                   q_ref, k_ref, v_ref, la_ref, r_ref, sq_ref, k2_ref, v2_ref, k2p_ref, v2p_ref,
                   tri_ref, masks_ref, gnw_ref,
                   x_ref, g1_ref, sh2_ref, sc2_ref, g2_ref, n2_ref, wout_ref, w1_ref, w2_ref,
                   y_ref, sout_ref, state_ref, mix_ref, *, levels, nt, nsteps):
    s = pl.program_id(0)
    slot = s % 2

    def ffn_pieces():
        mix = mix_ref[1 - slot]
        h, hb = _out_proj_norm(x_ref[...], mix[:, :GLA_WIDTH], mix[:, GLA_WIDTH:], g1_ref[...],
                               sh2_ref[...], sc2_ref[...], n2_ref[...], wout_ref)
        todo = list(range(D_FF // FFN_CHUNK_FUSED))
        parts = []

        def piece():
            if todo:
                parts.append(_ffn_piece(hb, w1_ref, w2_ref, todo.pop(0), FFN_CHUNK_FUSED))

        def finish():
            while todo:
                piece()
            y_ref[...] = h + g2_ref[...] * sum(parts[1:], parts[0]).reshape(h.shape)

        return piece, finish

    def mixers(between):
        j = s % nt
        st = jnp.where(j == 0, 0.0, state_ref[...])
        out = mix_ref.at[slot]
        st = _gla_prompt_block(st, q_ref, k_ref, v_ref, la_ref, r_ref, tri_ref, masks_ref, gnw_ref[...],
                               out.at[:, pl.ds(0, GLA_WIDTH)], levels, between)
        _swa_prompt_blocks(sq_ref, k2p_ref[...], k2_ref, v2p_ref[...], v2_ref, j == 0, sink_ref,
                           out.at[:, pl.ds(GLA_WIDTH, SWA_Q)], between)
        state_ref[...] = st
        sout_ref[0] = st

    @pl.when(s == 0)
    def _():
        state_ref[...] = jnp.zeros_like(state_ref)
        mixers(lambda: None)

    @pl.when((s > 0) & (s < nsteps))
    def _():
        piece, finish = ffn_pieces()
        mixers(piece)
        finish()

    @pl.when(s == nsteps)
    def _():
        sout_ref[0] = state_ref[...]
        _, finish = ffn_pieces()
        finish()


def _prompt_mix_ffn(x, gq, gk, gv, la, gr, sq, k2, v2, gnw, sinks, mod, row0, n2, wout, w1, w2):
    bsz, t, d = x.shape
    tb = TOKEN_BLOCK
    nt = t // tb
    nsteps = bsz * nt
    levels, tri, masks = _gla_constants(PROMPT_GLA_CHUNK, PROMPT_GLA_CHUNK)
    per_blk = tb // WINDOW

    def mix_blk(s):
        return jnp.minimum(s, nsteps - 1)

    def ffn_blk(s):
        return jnp.maximum(s - 1, 0)

    def tok(width):
        return pl.BlockSpec((tb, width), lambda s, _: (mix_blk(s), 0))

    prev = pl.BlockSpec((WINDOW, SWA_KV2), lambda s, _: (jnp.maximum(mix_blk(s) * per_blk - 1, 0), 0))
    xspec = pl.BlockSpec((1, tb, d), lambda s, _: (ffn_blk(s) // nt, ffn_blk(s) % nt, 0))
    mods = [_mod_spec(1, row0, chunk, lambda s, _: ffn_blk(s) // nt)
            for chunk in (MOD_GATE1, MOD_SHIFT2, MOD_SCALE2, MOD_GATE2)]
    return pl.pallas_call(
        functools.partial(_prompt_kernel, levels=levels, nt=nt, nsteps=nsteps),
        grid_spec=pltpu.PrefetchScalarGridSpec(
            num_scalar_prefetch=1,
            grid=(nsteps + 1,),
            in_specs=[tok(GLA_QK), tok(GLA_QK), tok(GLA_WIDTH), tok(GLA_QK), tok(GLA_WIDTH),
                      tok(SWA_Q), tok(SWA_KV2), tok(SWA_KV2), prev, prev,
                      _resident(tri.shape), _resident(masks.shape), _resident((1, GLA_DV)),
                      xspec, *mods, _resident((1, 1, d)),
                      _resident((d, d)), _resident((d, D_FF)), _resident((D_FF, d))],
            out_specs=[xspec,
                       pl.BlockSpec((1, GLA_DV, GLA_QK), lambda s, _: (mix_blk(s) // nt, 0, 0))],
            scratch_shapes=[pltpu.VMEM((GLA_DV, GLA_QK), f32),
                            pltpu.VMEM((2, tb, GLA_WIDTH + SWA_Q), bf16)]),
        out_shape=[jax.ShapeDtypeStruct((bsz, t, d), f32),
                   jax.ShapeDtypeStruct((bsz, GLA_DV, GLA_QK), f32)],
        compiler_params=pltpu.CompilerParams(
            dimension_semantics=("arbitrary",), vmem_limit_bytes=VMEM_LIMIT),
        name="prompt_mix_ffn",
    )(sinks, gq, gk, gv, la, gr, sq, k2, v2, k2, v2, tri, masks, gnw,
      x, mod, mod, mod, mod, n2, wout, w1, w2)


def _rope_tables(pos):
    half = HEAD_DIM // 2
    inv = jnp.power(ROPE_THETA, -jnp.arange(half, dtype=f32) * 2.0 / HEAD_DIM)
    ang = pos.astype(f32)[:, None] * inv[None, :]
    cos = jnp.cos(ang)
    sin = jnp.sin(ang)
    reps = LANES // HEAD_DIM
    return (jnp.tile(jnp.concatenate([cos, cos], axis=-1), (1, reps)),
            jnp.tile(jnp.concatenate([-sin, sin], axis=-1), (1, reps)))


def _block_diag_ones(n, blk):
    idx = np.arange(n) // blk
    return jnp.asarray((idx[:, None] == idx[None, :]).astype(np.float32), bf16)


def _layer_weights(w_in, w_gate_up, b_gate, q_norm_w, k_norm_w, w_out, w_ff1, w_ff2):
    splits = np.cumsum([GLA_QK, GLA_QK, GLA_WIDTH, GLA_WIDTH, GLA_GATE_RANK, SWA_Q, SWA_KV])
    gq, gk, gv, gr, glr, sq, sk, sv = jnp.split(w_in, [int(s) for s in splits], axis=1)
    glr = jnp.pad(glr, ((0, 0), (0, LANES - GLA_GATE_RANK)))
    win = jnp.concatenate([gq, gk, gv, gr, sq, sk, sv, glr], axis=1).astype(bf16)
    wgu = jnp.pad(w_gate_up, ((0, LANES - GLA_GATE_RANK), (0, 0))).astype(bf16)
    return dict(
        win=win, wgu=wgu, bg=b_gate.reshape(1, GLA_QK),
        qnw=jnp.tile(q_norm_w, N_Q_HEADS).reshape(1, SWA_Q),
        knw=jnp.tile(k_norm_w, N_KV_HEADS).reshape(1, SWA_KV),
        wout=w_out.astype(bf16), w1=w_ff1.astype(bf16), w2=w_ff2.astype(bf16))


def _decoder_layer(x, mod, row0, pos, state, past_k, past_v, lw, n1, n2, gnw, sinks, bdq, bdk):
    bsz, t, d = x.shape
    cos_t, sin_t = _rope_tables(pos)
    prompt = state is None
    if prompt:
        nb, tb, tail, act, groups = 1, TOKEN_BLOCK, WINDOW, bf16, PROMPT_INPROJ_GROUPS
    else:
        nb, tb, tail, act, groups = SAMPLE_ROWS // t, t, SAMPLE_ROWS, f32, 1
        cos_t = jnp.tile(cos_t, (nb, 1))
        sin_t = jnp.tile(sin_t, (nb, 1))
    n2 = n2.reshape(1, 1, d)
    gq, gk, gv, gr, la, sq, k2, v2, kk, vk = _inproj(
        x, mod, row0, n1.reshape(1, 1, d), lw["win"], lw["wgu"], lw["bg"], lw["qnw"], lw["knw"],
        bdq, bdk, cos_t, sin_t, nb, tb, tail, act, groups)
    if prompt:
        y, s_t = _prompt_mix_ffn(x, gq, gk, gv, la, gr, sq, k2, v2, gnw, sinks, mod, row0, n2,
                                 lw["wout"], lw["w1"], lw["w2"])
        s_new = s_t.reshape(bsz, GLA_DV, GLA_HEADS, GLA_DK).transpose(0, 2, 3, 1)
        k_keep, v_keep = kk, vk
    else:
        go, s_new = _gla_sample(gq, gk, gv, la, gr, state.reshape(bsz, GLA_QK, GLA_DV), gnw, bsz, t)
        so, kt, vt = _swa_sample(sq, kk, vk, past_k.transpose(0, 2, 3, 1), past_v.transpose(0, 2, 3, 1),
                                 sinks, bsz, t)
        k_keep, v_keep = kt.transpose(0, 3, 1, 2), vt.transpose(0, 3, 1, 2)
        y = _out_ffn(x, go, so, mod, row0, n2, lw["wout"], lw["w1"], lw["w2"], nb, tb)
    return (y, s_new.reshape(bsz, GLA_HEADS, GLA_DK, GLA_DV),
            k_keep.reshape(bsz, WINDOW, N_KV_HEADS, HEAD_DIM),
            v_keep.reshape(bsz, WINDOW, N_KV_HEADS, HEAD_DIM))


def kernel(x_prompt, x_sample, state_gla, cache_swa_k, cache_swa_v, c_prompt, c_sample, w_ada, b_ada, norm1_w, norm2_w, w_in, w_gate_up, b_gate, gla_norm_w, q_norm_w, k_norm_w, sinks, w_out, w_ff1, w_ff2):
    depth = w_ada.shape[0]
    bp, tp, _ = x_prompt.shape
    bs, ts, _ = x_sample.shape
    pos_p = jnp.arange(tp)
    pos_s = PAST_LEN + jnp.arange(ts)
    bdq = _block_diag_ones(SWA_Q, HEAD_DIM)
    bdk = _block_diag_ones(SWA_KV, HEAD_DIM)
    c_all = jnp.concatenate([c_sample, c_prompt], axis=0)
    yp, ys = x_prompt, x_sample
    outs = [[] for _ in range(6)]
    for l in range(depth):
        mod = _modulation(c_all, w_ada[l], b_ada[l])
        lw = _layer_weights(w_in[l], w_gate_up[l], b_gate[l], q_norm_w[l], k_norm_w[l],
                            w_out[l], w_ff1[l], w_ff2[l])
        gnw = gla_norm_w[l].reshape(1, GLA_DV)
        common = (lw, norm1_w[l], norm2_w[l], gnw, sinks[l], bdq, bdk)
        yp, gp, kp, vp = _decoder_layer(yp, mod, bs, pos_p, None, None, None, *common)
        ys, gs, kq, vq = _decoder_layer(ys, mod, 0, pos_s, state_gla[l], cache_swa_k[l],
                                        cache_swa_v[l], *common)
        for lst, val in zip(outs, (gp, kp, vp, gs, kq, vq)):
            lst.append(val)
    return (yp, ys) + tuple(jnp.stack(o) for o in outs)
```

```python
import functools

import jax
import jax.numpy as jnp
import numpy as np
from jax import lax
from jax.experimental import pallas as pl
from jax.experimental.pallas import tpu as pltpu

f32 = jnp.float32
bf16 = jnp.bfloat16

D_MODEL = 1024
GLA_HEADS = 4
GLA_DK = 64
GLA_DV = 128
GLA_QK = GLA_HEADS * GLA_DK
GLA_WIDTH = GLA_HEADS * GLA_DV
GLA_GATE_RANK = 16
GLA_TAU = 16.0
LOG2E = 1.4426950408889634
HEAD_DIM = 64
N_Q_HEADS = 8
N_KV_HEADS = 2
SWA_Q = N_Q_HEADS * HEAD_DIM
SWA_KV = N_KV_HEADS * HEAD_DIM
SWA_KV2 = 2 * SWA_KV
WINDOW = 128
ROPE_THETA = 10000.0
PAST_LEN = 8192
D_FF = 4 * D_MODEL
EPS = 1e-6
LANES = 128
GLA_CHUNK = 128
PROMPT_GLA_CHUNK = 128
TOKEN_BLOCK = 512
SAMPLE_ROWS = 256
PROMPT_INPROJ_GROUPS = 2
VMEM_LIMIT = 56 * 1024 * 1024

_SEG = {}
_off = 0
for _name, _w in (("gq", GLA_QK), ("gk", GLA_QK), ("gv", GLA_WIDTH), ("gr", GLA_WIDTH),
                  ("sq", SWA_Q), ("sk", SWA_KV), ("sv", SWA_KV), ("glr", LANES)):
    _SEG[_name] = (_off, _off + _w)
    _off += _w
IN_WIDTH_PADDED = _off


def _dot(a, b):
    return jnp.dot(a, b, preferred_element_type=f32)


def _dot_nt(a, b):
    return lax.dot_general(a, b, (((1,), (1,)), ((), ())), preferred_element_type=f32)


def _dot_tn(a, b):
    return lax.dot_general(a, b, (((0,), (0,)), ((), ())), preferred_element_type=f32)


def _sigmoid(x):
    return 1.0 / (1.0 + jnp.exp(-x))


def _mod_kernel(c_ref, w_ref, b_ref, o_ref):
    c = c_ref[...]
    s = (c * _sigmoid(c)).astype(bf16)
    res = _dot(s, w_ref[...].astype(bf16)) + b_ref[...]
    for r in range(res.shape[0]):
        o_ref[r] = res[r:r + 1, :]


def _modulation(c_all, w_ada, b_ada):
    m = c_all.shape[0]
    n = w_ada.shape[1]
    bn = 1536
    return pl.pallas_call(
        _mod_kernel,
        grid=(n // bn,),
        in_specs=[pl.BlockSpec((m, D_MODEL), lambda j: (0, 0)),
                  pl.BlockSpec((D_MODEL, bn), lambda j: (0, j)),
                  pl.BlockSpec((1, bn), lambda j: (0, j))],
        out_specs=pl.BlockSpec((m, 1, bn), lambda j: (0, 0, j)),
        out_shape=jax.ShapeDtypeStruct((m, 1, n), f32),
        compiler_params=pltpu.CompilerParams(vmem_limit_bytes=VMEM_LIMIT),
        name="adaln_mod",
    )(c_all, w_ada, b_ada.reshape(1, n))


def _group_rms(x, bd_ref, w_ref):
    ssq = _dot((x * x).astype(bf16), bd_ref[...])
    return x * lax.rsqrt(ssq * (1.0 / HEAD_DIM) + EPS) * w_ref[...]


def _rope(x, cos, sin_signed, low_half):
    partner = jnp.where(low_half, pltpu.roll(x, LANES - 32, axis=1), pltpu.roll(x, 32, axis=1))
    return x * cos + partner * sin_signed


def _dup_heads(x, low_lanes):
    rolled = pltpu.roll(x, HEAD_DIM, axis=1)
    return jnp.where(low_lanes, x, rolled), jnp.where(low_lanes, rolled, x)


def _inproj_kernel(x_ref, sh_ref, sc_ref, n1_ref, win_ref, wgu_ref, bg_ref, qnw_ref, knw_ref,
                   bdq_ref, bdk_ref, cos_ref, sin_ref, *rest, groups, ncast):
    slabs, rest = rest[:ncast], rest[ncast:]
    gq_ref, gk_ref, gv_ref, gr_ref, la_ref, sq_ref, k2_ref, v2_ref, kk_ref, vk_ref = rest[:10]
    for src, dst in zip(slabs, rest[10:]):
        dst[...] = src[...].astype(dst.dtype)
    nb, t, d = x_ref.shape
    m = nb * t
    tail = kk_ref.shape[0]
    mg = m // groups
    lane = lax.broadcasted_iota(jnp.int32, (mg, LANES), 1)
    low_half = (lane & 32) == 0
    low_lanes = lane < HEAD_DIM
    for grp in range(groups):
        rows = slice(grp * mg, (grp + 1) * mg)
        if nb == 1:
            x = x_ref[:, rows, :]
            sc, sh = sc_ref[...], sh_ref[...]
        else:
            seqs = slice(grp * (nb // groups), (grp + 1) * (nb // groups))
            x = x_ref[seqs]
            sc, sh = sc_ref[seqs], sh_ref[seqs]
        ms = jnp.mean(x * x, axis=-1, keepdims=True)
        hn = x * lax.rsqrt(ms + EPS) * n1_ref[...]
        hn = hn * (1.0 + sc) + sh
        hb = hn.reshape(mg, d).astype(bf16)

        def seg(name):
            a, b = _SEG[name]
            return _dot(hb, win_ref[:, a:b])

        cos = cos_ref[rows, :]
        sin = sin_ref[rows, :]
        sq = _group_rms(seg("sq"), bdq_ref, qnw_ref)
        for c in range(SWA_Q // LANES):
            blk = _rope(sq[:, c * LANES:(c + 1) * LANES], cos, sin, low_half)
            sq_ref[rows, c * LANES:(c + 1) * LANES] = (blk * (LOG2E * HEAD_DIM ** -0.5)).astype(sq_ref.dtype)
        sk = _rope(_group_rms(seg("sk"), bdk_ref, knw_ref), cos, sin, low_half)
        sv = seg("sv")
        keep = (grp + 1) * mg - (m - tail)
        if keep > 0:
            keep = min(keep, mg)
            dst_rows = slice((grp + 1) * mg - keep - (m - tail), (grp + 1) * mg - (m - tail))
            kk_ref[dst_rows, :] = sk[mg - keep:]
            vk_ref[dst_rows, :] = sv[mg - keep:]
        for src, dst in ((sk, k2_ref), (sv, v2_ref)):
            d0, d1 = _dup_heads(src, low_lanes)
            dst[rows, :LANES] = d0.astype(dst.dtype)
            dst[rows, LANES:] = d1.astype(dst.dtype)
        glr = seg("glr").astype(bf16)
        g = _dot(glr, wgu_ref[...]) + bg_ref[...]
        log_sig = jnp.minimum(g, 0.0) - jnp.log1p(jnp.exp(-jnp.abs(g)))
        la_ref[rows, :] = log_sig * (LOG2E / GLA_TAU)
        gq_ref[rows, :] = seg("gq") * (GLA_DK ** -0.5)
        gk_ref[rows, :] = seg("gk")
        gv_ref[rows, :] = seg("gv").astype(gv_ref.dtype)
        gr_ref[rows, :] = seg("gr").astype(gr_ref.dtype)


MOD_SHIFT1, MOD_SCALE1, MOD_GATE1, MOD_SHIFT2, MOD_SCALE2, MOD_GATE2 = range(6)


def _mod_spec(nb, row0, chunk, batch_block):
    return pl.BlockSpec((nb, 1, D_MODEL), lambda *ids: (row0 // nb + batch_block(*ids), 0, chunk))


def _inproj(x, mod, row0, n1, win, wgu, bg, qnw, knw, bdq, bdk, cos_t, sin_t, nb, tb, tail, act, groups,
            to_bf16=()):
    bsz, t, d = x.shape
    m = nb * tb
    nt = t // tb
    grid = (bsz // nb, nt)
    tok = bsz * t
    nsteps = grid[0] * grid[1]
    slab_specs = [pl.BlockSpec((w.shape[0] // nsteps, w.shape[1]), lambda i, j: (i * nt + j, 0))
                  for w in to_bf16]

    def full(shape):
        return pl.BlockSpec(shape, lambda i, j: (0,) * len(shape))

    def out(width):
        return pl.BlockSpec((m, width), lambda i, j: (i * nt + j, 0))

    tail_spec = pl.BlockSpec((tail, SWA_KV), lambda i, j: (i, 0))
    outs = ((GLA_QK, f32), (GLA_QK, f32), (GLA_WIDTH, act), (GLA_WIDTH, act), (GLA_QK, f32),
            (SWA_Q, act), (SWA_KV2, bf16), (SWA_KV2, bf16))
    ntail = (bsz // nb) * tail
    return pl.pallas_call(
        functools.partial(_inproj_kernel, groups=groups, ncast=len(to_bf16)),
        grid=grid,
        in_specs=[pl.BlockSpec((nb, tb, d), lambda i, j: (i, j, 0)),
                  _mod_spec(nb, row0, MOD_SHIFT1, lambda i, j: i),
                  _mod_spec(nb, row0, MOD_SCALE1, lambda i, j: i),
                  full((1, 1, d)),
                  full((d, IN_WIDTH_PADDED)),
                  full((LANES, GLA_QK)),
                  full((1, GLA_QK)),
                  full((1, SWA_Q)),
                  full((1, SWA_KV)),
                  full((SWA_Q, SWA_Q)),
                  full((SWA_KV, SWA_KV)),
                  pl.BlockSpec((m, LANES), lambda i, j: (j, 0)),
                  pl.BlockSpec((m, LANES), lambda i, j: (j, 0))] + slab_specs,
        out_specs=[out(w) for w, _ in outs] + [tail_spec, tail_spec] + slab_specs,
        out_shape=[jax.ShapeDtypeStruct((tok, w), dt) for w, dt in outs]
        + [jax.ShapeDtypeStruct((ntail, SWA_KV), f32)] * 2
        + [jax.ShapeDtypeStruct(w.shape, bf16) for w in to_bf16],
        compiler_params=pltpu.CompilerParams(
            dimension_semantics=("parallel", "arbitrary"), vmem_limit_bytes=VMEM_LIMIT),
        name="inproj",
    )(x, mod, mod, n1, win, wgu, bg, qnw, knw, bdq, bdk, cos_t, sin_t, *to_bf16)


def _gla_constants(chunk, seq):
    t = np.arange(chunk)
    levels = []
    m = 1
    while m < seq:
        levels.append(m)
        m *= 2
    masks = [np.eye(chunk, dtype=bool)]
    for m in levels:
        upper = (t % (2 * m) >= m)[:, None]
        lower = (t % (2 * m) < m)[None, :]
        same = (t[:, None] // (2 * m)) == (t[None, :] // (2 * m))
        masks.append(same & upper & lower)
    tri = t[None, :] <= t[:, None]
    tiled = np.tile(np.stack(masks).astype(np.float32), (1, 1, GLA_HEADS))
    return tuple(levels), jnp.asarray(tri.astype(np.float32), bf16), jnp.asarray(tiled)


def _stack_heads(xb, lane_head, axis):
    zero = jnp.zeros_like(xb)
    return jnp.concatenate([jnp.where(lane_head == h, xb, zero) for h in range(GLA_HEADS)], axis=axis)


def _block_sums(la, bcum, m, row, rolls):
    c, n = la.shape
    if m == 1:
        return la, None
    if m < 8:
        def rolled(shift):
            if shift not in rolls:
                rolls[shift] = pltpu.roll(la, shift % c, axis=0)
            return rolls[shift]
        pos = row & (m - 1)
        pre = la
        suf = None
        for j in range(1, m):
            pre = pre + jnp.where(pos >= j, rolled(j), 0.0)
            term = jnp.where(pos < m - j, rolled(-j), 0.0)
            suf = term if suf is None else suf + term
        return pre, suf
    before, last = [], []
    for i in range(c // m):
        before.append(jnp.zeros((m, n), f32) if i == 0
                      else jnp.broadcast_to(bcum[i * m - 1:i * m], (m, n)))
        last.append(jnp.broadcast_to(bcum[(i + 1) * m - 1:(i + 1) * m], (m, n)))
    if len(before) == 1:
        return bcum - before[0], last[0] - bcum
    return bcum - jnp.concatenate(before, axis=0), jnp.concatenate(last, axis=0) - bcum


def _gla_scores(q_ref, k_ref, v_ref, la_ref, tri_ref, masks_ref, levels, seq):
    c = q_ref.shape[0]
    la = la_ref[...]
    hi = la.astype(bf16)
    lo = (la - hi.astype(f32)).astype(bf16)
    tri = tri_ref[...]
    bcum = _dot(tri, hi) + _dot(tri, lo)
    q = q_ref[...]
    k = k_ref[...]
    vb = v_ref[...].astype(bf16)
    row = lax.broadcasted_iota(jnp.int32, (c, GLA_QK), 0)
    lane_head = lax.broadcasted_iota(jnp.int32, (c, GLA_QK), 1) >> 6
    rolls = {}
    attn = None
    for lvl, m in enumerate((0,) + levels):
        if m == 0:
            qt, kt = q, k
        else:
            pre, suf = _block_sums(la, bcum, m, row, rolls)
            qt = q * jnp.exp2(pre)
            kt = k if suf is None else k * jnp.exp2(suf)
        r = _dot_nt(qt.astype(bf16), _stack_heads(kt.astype(bf16), lane_head, 0))
        rm = r * masks_ref[lvl]
        attn = rm if attn is None else attn + rm
    a = attn.astype(bf16)
    pre, suf = _block_sums(la, bcum, seq, row, rolls)
    qf = (q * jnp.exp2(pre)).astype(bf16)
    kf = (k * jnp.exp2(suf)).astype(bf16)
    return a, qf, kf, vb, bcum, hi, lo


def _gla_values(a, vb):
    v_head = lax.broadcasted_iota(jnp.int32, vb.shape, 1) >> 7
    return _dot(a, _stack_heads(vb, v_head, 0))


def _gla_finish(o, r_ref, gnw, go_ref):
    for h in range(GLA_HEADS):
        sl = slice(h * GLA_DV, (h + 1) * GLA_DV)
        oh = o[:, sl]
        r = r_ref[:, sl].astype(f32)
        ms = jnp.mean(oh * oh, axis=-1, keepdims=True)
        go_ref[:, sl] = (oh * lax.rsqrt(ms + EPS) * gnw * (r * _sigmoid(r))).astype(go_ref.dtype)


def _gla_prompt_block(st, q_ref, k_ref, v_ref, la_ref, r_ref, tri_ref, masks_ref, gnw, go_ref, levels,
                      between=lambda: None):
    c = PROMPT_GLA_CHUNK
    nchunk = q_ref.shape[0] // c
    lane_head = lax.broadcasted_iota(jnp.int32, (GLA_DV, GLA_QK), 1) >> 6
    chunks = []
    for ci in range(nchunk):
        rows = pl.ds(ci * c, c)
        chunks.append(_gla_scores(q_ref.at[rows], k_ref.at[rows], v_ref.at[rows], la_ref.at[rows],
                                  tri_ref, masks_ref, levels, c))
        between()
    intra, upds, decays = [], [], []
    for a, _, kf, vb, bcum, _, _ in chunks:
        intra.append(_gla_values(a, vb))
        full = _dot_tn(vb, kf)
        upd = None
        for h in range(GLA_HEADS):
            term = jnp.where(lane_head == h, full[h * GLA_DV:(h + 1) * GLA_DV], 0.0)
            upd = term if upd is None else upd + term
        upds.append(upd)
        decays.append(jnp.exp2(bcum[c - 1:c]))
    between()
    for ci in range(nchunk):
        rows = pl.ds(ci * c, c)
        sbd_t = _stack_heads(st.astype(bf16), lane_head, 0)
        o = intra[ci] + _dot_nt(chunks[ci][1], sbd_t)
        _gla_finish(o, r_ref.at[rows], gnw, go_ref.at[rows])
        st = decays[ci] * st + upds[ci]
    return st


def _gla_sample_kernel(q_ref, k_ref, v_ref, la_ref, r_ref, s0_ref, tri_ref, masks_ref, gnw_ref,
                       go_ref, sout_ref, *, levels, seq):
    c = q_ref.shape[0]
    a, qf, kf, vb, _, hi, lo = _gla_scores(q_ref, k_ref, v_ref, la_ref, tri_ref, masks_ref, levels, seq)
    o_intra = _gla_values(a, vb)
    row_head = lax.broadcasted_iota(jnp.int32, (GLA_QK, GLA_DV), 0) >> 6
    ones = jnp.ones((seq, GLA_DV), bf16)
    inter = []
    for b in range(c // seq):
        rows = slice(b * seq, (b + 1) * seq)
        s_old = s0_ref[b]
        sbd = _stack_heads(s_old.astype(bf16), row_head, 1)
        inter.append(_dot(qf[rows], sbd))
        p = _dot_tn(kf[rows], vb[rows])
        upd = None
        for h in range(GLA_HEADS):
            term = jnp.where(row_head == h, p[:, h * GLA_DV:(h + 1) * GLA_DV], 0.0)
            upd = term if upd is None else upd + term
        total = _dot_tn(hi[rows], ones) + _dot_tn(lo[rows], ones)
        sout_ref[b] = jnp.exp2(total) * s_old + upd
    o = o_intra + jnp.concatenate(inter, axis=0)
    _gla_finish(o, r_ref, gnw_ref[...], go_ref)


def _gla_sample(gq, gk, gv, la, gr, state, gnw, bsz, t):
    c = GLA_CHUNK
    nseq = c // t
    levels, tri, masks = _gla_constants(c, t)

    def tok(width):
        return pl.BlockSpec((c, width), lambda i: (i, 0))

    return pl.pallas_call(
        functools.partial(_gla_sample_kernel, levels=levels, seq=t),
        grid=(bsz // nseq,),
        in_specs=[tok(GLA_QK), tok(GLA_QK), tok(GLA_WIDTH), tok(GLA_QK), tok(GLA_WIDTH),
                  pl.BlockSpec((nseq, GLA_QK, GLA_DV), lambda i: (i, 0, 0)),
                  pl.BlockSpec(tri.shape, lambda i: (0, 0)),
                  pl.BlockSpec(masks.shape, lambda i: (0, 0, 0)),
                  pl.BlockSpec((1, GLA_DV), lambda i: (0, 0))],
        out_specs=[tok(GLA_WIDTH),
                   pl.BlockSpec((nseq, GLA_QK, GLA_DV), lambda i: (i, 0, 0))],
        out_shape=[jax.ShapeDtypeStruct((bsz * t, GLA_WIDTH), f32),
                   jax.ShapeDtypeStruct((bsz, GLA_QK, GLA_DV), f32)],
        compiler_params=pltpu.CompilerParams(
            dimension_semantics=("parallel",), vmem_limit_bytes=VMEM_LIMIT),
        name="gla_sample",
    )(gq, gk, gv, la, gr, state, tri, masks, gnw)


def _sink_attention(q_ref, rows, kdups, vdups, mask, sink_ref, o_ref, between=lambda: None):
    n = len(kdups)
    heads_per_group = N_Q_HEADS // N_KV_HEADS
    low_q = lax.broadcasted_iota(jnp.int32, (rows, LANES), 1) < HEAD_DIM
    scores = []
    for i in range(n):
        per_group = []
        for g in range(N_KV_HEADS):
            stack = []
            for p in range(heads_per_group // 2):
                pair = g * (heads_per_group // 2) + p
                qp = q_ref[i * rows:(i + 1) * rows, pair * LANES:(pair + 1) * LANES].astype(bf16)
                zero = jnp.zeros_like(qp)
                stack.append(jnp.where(low_q, qp, zero))
                stack.append(jnp.where(low_q, zero, qp))
            per_group.append(_dot_nt(jnp.concatenate(stack, axis=0), kdups[i][g]))
        scores.append(per_group)
    between()
    probs, inv = [], []
    for head in range(N_Q_HEADS):
        g, hh = divmod(head, heads_per_group)
        parts = [scores[i][g][hh * rows:(hh + 1) * rows] for i in range(n)]
        s = parts[0] if n == 1 else jnp.concatenate(parts, axis=0)
        sink = sink_ref[head] * LOG2E
        s = jnp.where(mask, s, -1e30)
        mx = jnp.maximum(jnp.max(s, axis=-1, keepdims=True), sink)
        p = jnp.exp2(s - mx)
        inv.append(1.0 / (jnp.sum(p, axis=-1, keepdims=True) + jnp.exp2(sink - mx)))
        probs.append(p.astype(bf16))
    between()
    outs = []
    for i in range(n):
        per_group = []
        for g in range(N_KV_HEADS):
            p_i = jnp.concatenate([probs[g * heads_per_group + hh][i * rows:(i + 1) * rows]
                                   for hh in range(heads_per_group)], axis=0)
            per_group.append(_dot(p_i, vdups[i][g]))
        outs.append(per_group)
    low_all = lax.broadcasted_iota(jnp.int32, (n * rows, LANES), 1) < HEAD_DIM
    for pair in range(N_Q_HEADS // 2):
        halves = []
        for head in (2 * pair, 2 * pair + 1):
            g, hh = divmod(head, heads_per_group)
            parts = [outs[i][g][hh * rows:(hh + 1) * rows] for i in range(n)]
            o = parts[0] if n == 1 else jnp.concatenate(parts, axis=0)
            halves.append(o * inv[head])
        o_ref[:, pair * LANES:(pair + 1) * LANES] = jnp.where(low_all, halves[0], halves[1]).astype(o_ref.dtype)


def _swa_prompt_blocks(q_ref, k_before, k2_ref, v_before, v2_ref, first, sink_ref, o_ref,
                       between=lambda: None):
    w = WINDOW
    nblk = q_ref.shape[0] // w

    def dup(before, ref):
        blocks = [before] + [ref[i * w:(i + 1) * w, :] for i in range(nblk)]
        return [[jnp.concatenate([blocks[i][:, g * LANES:(g + 1) * LANES],
                                  blocks[i + 1][:, g * LANES:(g + 1) * LANES]], axis=0)
                 for g in range(N_KV_HEADS)] for i in range(nblk)]

    row = lax.broadcasted_iota(jnp.int32, (nblk * w, 2 * w), 0)
    tk = lax.broadcasted_iota(jnp.int32, (nblk * w, 2 * w), 1)
    rel = tk - (row & (w - 1))
    first_key = jnp.where(row < w, jnp.where(first, w, 0), 0)
    mask = (rel > 0) & (rel <= w) & (tk >= first_key)
    _sink_attention(q_ref, w, dup(k_before, k2_ref), dup(v_before, v2_ref), mask, sink_ref, o_ref, between)


def _swa_sample_kernel(sink_ref, q_ref, kn_ref, vn_ref, pk_ref, pv_ref, o_ref, ko_ref, vo_ref, *, seq):
    nseq, _, _, w = pk_ref.shape
    rows = nseq * seq
    heads_per_group = N_Q_HEADS // N_KV_HEADS
    lane = lax.broadcasted_iota(jnp.int32, (rows, LANES), 1)
    r_id = lax.broadcasted_iota(jnp.int32, (rows, LANES), 0)
    low = lane < HEAD_DIM
    pos = r_id & (seq - 1)
    mask_old = lane > pos
    seq_shift = seq.bit_length() - 1
    mask_new = ((r_id >> seq_shift) == (lane >> seq_shift)) & ((lane & (seq - 1)) <= pos)
    kn = kn_ref[...]
    vn = vn_ref[...]
    kn_dup = [d.astype(bf16) for d in _dup_heads(kn, low)]
    vn_dup = [d.astype(bf16) for d in _dup_heads(vn, low)]

    pad = jnp.zeros((w - seq, LANES), f32)
    tail_lanes = lax.broadcasted_iota(jnp.int32, (HEAD_DIM, w), 1) >= w - seq
    kt_dup, vt_dup = [], []
    for b in range(nseq):
        tok = slice(b * seq, (b + 1) * seq)
        per_k, per_v = [], []
        for new, old_ref, out_ref, per in ((kn, pk_ref, ko_ref, per_k), (vn, pv_ref, vo_ref, per_v)):
            new_t = jnp.concatenate([pad, new[tok]], axis=0).T
            for g in range(N_KV_HEADS):
                old = old_ref[b, g]
                out_ref[b, g] = jnp.where(tail_lanes, new_t[g * HEAD_DIM:(g + 1) * HEAD_DIM],
                                          pltpu.roll(old, w - seq, axis=1))
                ob = old.astype(bf16)
                per.append(jnp.concatenate([ob, ob], axis=0))
        kt_dup.append(per_k)
        vt_dup.append(per_v)

    lhs = []
    for g in range(N_KV_HEADS):
        stack = []
        for p in range(heads_per_group // 2):
            pair = g * (heads_per_group // 2) + p
            qp = q_ref[:, pair * LANES:(pair + 1) * LANES].astype(bf16)
            zero = jnp.zeros_like(qp)
            stack.append(jnp.where(low, qp, zero))
            stack.append(jnp.where(low, zero, qp))
        lhs.append(jnp.concatenate(stack, axis=0))
    s_new = [_dot_nt(lhs[g], kn_dup[g]) for g in range(N_KV_HEADS)]
    s_old = []
    for b in range(nseq):
        per = []
        for g in range(N_KV_HEADS):
            qb = jnp.concatenate([lhs[g][hh * rows + b * seq:hh * rows + (b + 1) * seq]
                                  for hh in range(heads_per_group)], axis=0)
            per.append(_dot(qb, kt_dup[b][g]))
        s_old.append(per)

    p_old, p_new, inv = [], [], []
    for head in range(N_Q_HEADS):
        g, hh = divmod(head, heads_per_group)
        so = jnp.concatenate([s_old[b][g][hh * seq:(hh + 1) * seq] for b in range(nseq)], axis=0)
        so = jnp.where(mask_old, so, -1e30)
        sn = jnp.where(mask_new, s_new[g][hh * rows:(hh + 1) * rows], -1e30)
        sink = sink_ref[head] * LOG2E
        mx = jnp.maximum(jnp.maximum(jnp.max(so, axis=-1, keepdims=True),
                                     jnp.max(sn, axis=-1, keepdims=True)), sink)
        po = jnp.exp2(so - mx)
        pn = jnp.exp2(sn - mx)
        inv.append(1.0 / (jnp.sum(po, axis=-1, keepdims=True) + jnp.sum(pn, axis=-1, keepdims=True)
                          + jnp.exp2(sink - mx)))
        p_old.append(po.astype(bf16))
        p_new.append(pn.astype(bf16))

    o_new = [_dot(jnp.concatenate([p_new[g * heads_per_group + hh] for hh in range(heads_per_group)], axis=0),
                  vn_dup[g]) for g in range(N_KV_HEADS)]
    o_old = []
    for b in range(nseq):
        per = []
        for g in range(N_KV_HEADS):
            pb = jnp.concatenate([p_old[g * heads_per_group + hh][b * seq:(b + 1) * seq]
                                  for hh in range(heads_per_group)], axis=0)
            per.append(_dot_nt(pb, vt_dup[b][g]))
        o_old.append(per)

    for pair in range(N_Q_HEADS // 2):
        halves = []
        for head in (2 * pair, 2 * pair + 1):
            g, hh = divmod(head, heads_per_group)
            old = jnp.concatenate([o_old[b][g][hh * seq:(hh + 1) * seq] for b in range(nseq)], axis=0)
            halves.append((old + o_new[g][hh * rows:(hh + 1) * rows]) * inv[head])
        o_ref[:, pair * LANES:(pair + 1) * LANES] = jnp.where(low, halves[0], halves[1])


def _swa_sample(sq, sk, sv, past_kt, past_vt, sinks, bsz, t):
    nseq = SAMPLE_ROWS // 2 // t
    assert nseq * t == LANES
    tok = lambda width: pl.BlockSpec((nseq * t, width), lambda i, s: (i, 0))
    cache = pl.BlockSpec((nseq,) + past_kt.shape[1:], lambda i, s: (i, 0, 0, 0))
    return pl.pallas_call(
        functools.partial(_swa_sample_kernel, seq=t),
        grid_spec=pltpu.PrefetchScalarGridSpec(
            num_scalar_prefetch=1,
            grid=(bsz // nseq,),
            in_specs=[tok(SWA_Q), tok(SWA_KV), tok(SWA_KV), cache, cache],
            out_specs=[tok(SWA_Q), cache, cache]),
        out_shape=[jax.ShapeDtypeStruct((bsz * t, SWA_Q), f32),
                   jax.ShapeDtypeStruct(past_kt.shape, f32),
                   jax.ShapeDtypeStruct(past_vt.shape, f32)],
        compiler_params=pltpu.CompilerParams(
            dimension_semantics=("parallel",), vmem_limit_bytes=VMEM_LIMIT),
        name="swa_sample",
    )(sinks, sq, sk, sv, past_kt, past_vt)


FFN_CHUNK = 1024
FFN_CHUNK_FUSED = 512


def _out_proj_norm(x, go, so, g1, sh2, sc2, n2, wout_ref):
    nb, t, d = x.shape
    mixed = _dot(go, wout_ref[:GLA_WIDTH, :]) + _dot(so, wout_ref[GLA_WIDTH:, :])
    h = x + g1 * mixed.reshape(nb, t, d)
    ms = jnp.mean(h * h, axis=-1, keepdims=True)
    hn = h * lax.rsqrt(ms + EPS) * n2
    hn = hn * (1.0 + sc2) + sh2
    return h, hn.reshape(nb * t, d).astype(bf16)


def _ffn_piece(hb, w1_ref, w2_ref, c, width):
    cols = slice(c * width, (c + 1) * width)
    a = jnp.maximum(_dot(hb, w1_ref[:, cols]), 0.0)
    return _dot((a * a).astype(bf16), w2_ref[cols, :])


def _out_ffn_block(x, go, so, g1, sh2, sc2, g2, n2, wout_ref, w1_ref, w2_ref):
    h, hb = _out_proj_norm(x, go, so, g1, sh2, sc2, n2, wout_ref)
    ff = _ffn_piece(hb, w1_ref, w2_ref, 0, FFN_CHUNK)
    for c in range(1, D_FF // FFN_CHUNK):
        ff = ff + _ffn_piece(hb, w1_ref, w2_ref, c, FFN_CHUNK)
    return h + g2 * ff.reshape(h.shape)


def _out_ffn_kernel(x_ref, go_ref, so_ref, g1_ref, sh2_ref, sc2_ref, g2_ref, n2_ref,
                    wout_ref, w1_ref, w2_ref, y_ref):
    y_ref[...] = _out_ffn_block(x_ref[...], go_ref[...].astype(bf16), so_ref[...].astype(bf16),
                                g1_ref[...], sh2_ref[...],
                                sc2_ref[...], g2_ref[...], n2_ref[...], wout_ref, w1_ref, w2_ref)


def _resident(shape):
    return pl.BlockSpec(shape, lambda *_: (0,) * len(shape), pipeline_mode=pl.Buffered(1))


def _out_ffn(x, go, so, mod, row0, n2, wout, w1, w2, nb, tb):
    bsz, t, d = x.shape
    m = nb * tb
    nt = t // tb
    mods = [_mod_spec(nb, row0, chunk, lambda i, j: i)
            for chunk in (MOD_GATE1, MOD_SHIFT2, MOD_SCALE2, MOD_GATE2)]
    xspec = pl.BlockSpec((nb, tb, d), lambda i, j: (i, j, 0))
    return pl.pallas_call(
        _out_ffn_kernel,
        grid=(bsz // nb, nt),
        in_specs=[xspec,
                  pl.BlockSpec((m, GLA_WIDTH), lambda i, j: (i * nt + j, 0)),
                  pl.BlockSpec((m, SWA_Q), lambda i, j: (i * nt + j, 0)),
                  *mods,
                  pl.BlockSpec((1, 1, d), lambda i, j: (0, 0, 0)),
                  _resident((d, d)), _resident((d, D_FF)), _resident((D_FF, d))],
        out_specs=xspec,
        out_shape=jax.ShapeDtypeStruct((bsz, t, d), f32),
        compiler_params=pltpu.CompilerParams(
            dimension_semantics=("parallel", "parallel"), vmem_limit_bytes=VMEM_LIMIT),
        name="out_ffn",
    )(x, go, so, mod, mod, mod, mod, n2, wout, w1, w2)


def _prompt_kernel(sink_ref,
                   q_ref, k_ref, v_ref, la_ref, r_ref, sq_ref, k2_ref, v2_ref, k2p_ref, v2p_ref,
                   tri_ref, masks_ref, gnw_ref,
                   x_ref, g1_ref, sh2_ref, sc2_ref, g2_ref, n2_ref, wout_ref, w1_ref, w2_ref,
                   y_ref, sout_ref, state_ref, mix_ref, *, levels, nt, nsteps):
    s = pl.program_id(0)
    slot = s % 2

    @pl.when(s == 0)
    def _():
        mix_ref[1] = jnp.zeros(mix_ref.shape[1:], mix_ref.dtype)
        state_ref[...] = jnp.zeros_like(state_ref)

    j = jnp.minimum(s, nsteps - 1) % nt
    st_in = state_ref[...]
    st = jnp.where(j == 0, 0.0, st_in)
    gnw = gnw_ref[...]
    out = mix_ref.at[slot]
    mix = mix_ref[1 - slot]
    h, hb = _out_proj_norm(x_ref[...], mix[:, :GLA_WIDTH], mix[:, GLA_WIDTH:], g1_ref[...],
                           sh2_ref[...], sc2_ref[...], n2_ref[...], wout_ref)
    todo = list(range(D_FF // FFN_CHUNK_FUSED))
    parts = []

    def ffn_piece():
        if todo:
            parts.append(_ffn_piece(hb, w1_ref, w2_ref, todo.pop(0), FFN_CHUNK_FUSED))

    st = _gla_prompt_block(st, q_ref, k_ref, v_ref, la_ref, r_ref, tri_ref, masks_ref, gnw,
                           out.at[:, pl.ds(0, GLA_WIDTH)], levels, ffn_piece)
    _swa_prompt_blocks(sq_ref, k2p_ref[...], k2_ref, v2p_ref[...], v2_ref, j == 0, sink_ref,
                       out.at[:, pl.ds(GLA_WIDTH, SWA_Q)], ffn_piece)
    while todo:
        ffn_piece()
    st = jnp.where(s < nsteps, st, st_in)
    state_ref[...] = st
    sout_ref[0] = st
    y_ref[...] = h + g2_ref[...] * sum(parts[1:], parts[0]).reshape(h.shape)


def _prompt_mix_ffn(x, gq, gk, gv, la, gr, sq, k2, v2, gnw, sinks, mod, row0, n2, wout, w1, w2):
    bsz, t, d = x.shape
    tb = TOKEN_BLOCK
    nt = t // tb
    nsteps = bsz * nt
    levels, tri, masks = _gla_constants(PROMPT_GLA_CHUNK, PROMPT_GLA_CHUNK)
    per_blk = tb // WINDOW

    def mix_blk(s):
        return jnp.minimum(s, nsteps - 1)

    def ffn_blk(s):
        return jnp.maximum(s - 1, 0)

    def tok(width):
        return pl.BlockSpec((tb, width), lambda s, _: (mix_blk(s), 0))

    prev = pl.BlockSpec((WINDOW, SWA_KV2), lambda s, _: (jnp.maximum(mix_blk(s) * per_blk - 1, 0), 0))
    xspec = pl.BlockSpec((1, tb, d), lambda s, _: (ffn_blk(s) // nt, ffn_blk(s) % nt, 0))
    mods = [_mod_spec(1, row0, chunk, lambda s, _: ffn_blk(s) // nt)
            for chunk in (MOD_GATE1, MOD_SHIFT2, MOD_SCALE2, MOD_GATE2)]
    return pl.pallas_call(
        functools.partial(_prompt_kernel, levels=levels, nt=nt, nsteps=nsteps),
        grid_spec=pltpu.PrefetchScalarGridSpec(
            num_scalar_prefetch=1,
            grid=(nsteps + 1,),
            in_specs=[tok(GLA_QK), tok(GLA_QK), tok(GLA_WIDTH), tok(GLA_QK), tok(GLA_WIDTH),
                      tok(SWA_Q), tok(SWA_KV2), tok(SWA_KV2), prev, prev,
                      _resident(tri.shape), _resident(masks.shape), _resident((1, GLA_DV)),
                      xspec, *mods, _resident((1, 1, d)),
                      _resident((d, d)), _resident((d, D_FF)), _resident((D_FF, d))],
            out_specs=[xspec,
                       pl.BlockSpec((1, GLA_DV, GLA_QK), lambda s, _: (mix_blk(s) // nt, 0, 0))],
            scratch_shapes=[pltpu.VMEM((GLA_DV, GLA_QK), f32),
                            pltpu.VMEM((2, tb, GLA_WIDTH + SWA_Q), bf16)]),
        out_shape=[jax.ShapeDtypeStruct((bsz, t, d), f32),
                   jax.ShapeDtypeStruct((bsz, GLA_DV, GLA_QK), f32)],
        compiler_params=pltpu.CompilerParams(
            dimension_semantics=("arbitrary",), vmem_limit_bytes=VMEM_LIMIT),
        name="prompt_mix_ffn",
    )(sinks, gq, gk, gv, la, gr, sq, k2, v2, k2, v2, tri, masks, gnw,
      x, mod, mod, mod, mod, n2, wout, w1, w2)


def _rope_tables(pos):
    half = HEAD_DIM // 2
    inv = jnp.power(ROPE_THETA, -jnp.arange(half, dtype=f32) * 2.0 / HEAD_DIM)
    ang = pos.astype(f32)[:, None] * inv[None, :]
    cos = jnp.cos(ang)
    sin = jnp.sin(ang)
    reps = LANES // HEAD_DIM
    return (jnp.tile(jnp.concatenate([cos, cos], axis=-1), (1, reps)),
            jnp.tile(jnp.concatenate([-sin, sin], axis=-1), (1, reps)))


def _block_diag_ones(n, blk):
    idx = np.arange(n) // blk
    return jnp.asarray((idx[:, None] == idx[None, :]).astype(np.float32), bf16)


def _layer_weights(w_in, w_gate_up, b_gate, q_norm_w, k_norm_w, w_out, w_ff1, w_ff2):
    splits = np.cumsum([GLA_QK, GLA_QK, GLA_WIDTH, GLA_WIDTH, GLA_GATE_RANK, SWA_Q, SWA_KV])
    gq, gk, gv, gr, glr, sq, sk, sv = jnp.split(w_in, [int(s) for s in splits], axis=1)
    glr = jnp.pad(glr, ((0, 0), (0, LANES - GLA_GATE_RANK)))
    win = jnp.concatenate([gq, gk, gv, gr, sq, sk, sv, glr], axis=1).astype(bf16)
    wgu = jnp.pad(w_gate_up, ((0, LANES - GLA_GATE_RANK), (0, 0))).astype(bf16)
    return dict(
        win=win, wgu=wgu, bg=b_gate.reshape(1, GLA_QK),
        qnw=jnp.tile(q_norm_w, N_Q_HEADS).reshape(1, SWA_Q),
        knw=jnp.tile(k_norm_w, N_KV_HEADS).reshape(1, SWA_KV),
        ffn_f32=(w_out, w_ff1, w_ff2))


def _decoder_layer(x, mod, row0, pos, state, past_k, past_v, lw, n1, n2, gnw, sinks, bdq, bdk, ffn_w):
    bsz, t, d = x.shape
    cos_t, sin_t = _rope_tables(pos)
    prompt = state is None
    if prompt:
        nb, tb, tail, act, groups = 1, TOKEN_BLOCK, WINDOW, bf16, PROMPT_INPROJ_GROUPS
    else:
        nb, tb, tail, act, groups = SAMPLE_ROWS // t, t, SAMPLE_ROWS, f32, 1
        cos_t = jnp.tile(cos_t, (nb, 1))
        sin_t = jnp.tile(sin_t, (nb, 1))
    n2 = n2.reshape(1, 1, d)
    gq, gk, gv, gr, la, sq, k2, v2, kk, vk, *made = _inproj(
        x, mod, row0, n1.reshape(1, 1, d), lw["win"], lw["wgu"], lw["bg"], lw["qnw"], lw["knw"],
        bdq, bdk, cos_t, sin_t, nb, tb, tail, act, groups, lw["ffn_f32"] if ffn_w is None else ())
    if ffn_w is None:
        ffn_w = tuple(made)
    if prompt:
        y, s_t = _prompt_mix_ffn(x, gq, gk, gv, la, gr, sq, k2, v2, gnw, sinks, mod, row0, n2, *ffn_w)
        s_new = s_t.reshape(bsz, GLA_DV, GLA_HEADS, GLA_DK).transpose(0, 2, 3, 1)
        k_keep, v_keep = kk, vk
    else:
        go, s_new = _gla_sample(gq, gk, gv, la, gr, state.reshape(bsz, GLA_QK, GLA_DV), gnw, bsz, t)
        so, kt, vt = _swa_sample(sq, kk, vk, past_k.transpose(0, 2, 3, 1), past_v.transpose(0, 2, 3, 1),
                                 sinks, bsz, t)
        k_keep, v_keep = kt.transpose(0, 3, 1, 2), vt.transpose(0, 3, 1, 2)
        y = _out_ffn(x, go, so, mod, row0, n2, *ffn_w, nb, tb)
    return (y, s_new.reshape(bsz, GLA_HEADS, GLA_DK, GLA_DV),
            k_keep.reshape(bsz, WINDOW, N_KV_HEADS, HEAD_DIM),
            v_keep.reshape(bsz, WINDOW, N_KV_HEADS, HEAD_DIM), ffn_w)


def kernel(x_prompt, x_sample, state_gla, cache_swa_k, cache_swa_v, c_prompt, c_sample, w_ada, b_ada, norm1_w, norm2_w, w_in, w_gate_up, b_gate, gla_norm_w, q_norm_w, k_norm_w, sinks, w_out, w_ff1, w_ff2):
    depth = w_ada.shape[0]
    bp, tp, _ = x_prompt.shape
    bs, ts, _ = x_sample.shape
    pos_p = jnp.arange(tp)
    pos_s = PAST_LEN + jnp.arange(ts)
    bdq = _block_diag_ones(SWA_Q, HEAD_DIM)
    bdk = _block_diag_ones(SWA_KV, HEAD_DIM)
    c_all = jnp.concatenate([c_sample, c_prompt], axis=0)
    yp, ys = x_prompt, x_sample
    outs = [[] for _ in range(6)]
    for l in range(depth):
        mod = _modulation(c_all, w_ada[l], b_ada[l])
        lw = _layer_weights(w_in[l], w_gate_up[l], b_gate[l], q_norm_w[l], k_norm_w[l],
                            w_out[l], w_ff1[l], w_ff2[l])
        gnw = gla_norm_w[l].reshape(1, GLA_DV)
        common = (lw, norm1_w[l], norm2_w[l], gnw, sinks[l], bdq, bdk)
        yp, gp, kp, vp, ffn_w = _decoder_layer(yp, mod, bs, pos_p, None, None, None, *common, None)
        ys, gs, kq, vq, _ = _decoder_layer(ys, mod, 0, pos_s, state_gla[l], cache_swa_k[l],
                                           cache_swa_v[l], *common, ffn_w)
        for lst, val in zip(outs, (gp, kp, vp, gs, kq, vq)):
            lst.append(val)
    return (yp, ys) + tuple(jnp.stack(o) for o in outs)
```

```python
import functools

import jax
import jax.numpy as jnp
import numpy as np
from jax import lax
from jax.experimental import pallas as pl
from jax.experimental.pallas import tpu as pltpu

f32 = jnp.float32
bf16 = jnp.bfloat16

D_MODEL = 1024
GLA_HEADS = 4
GLA_DK = 64
GLA_DV = 128
GLA_QK = GLA_HEADS * GLA_DK
GLA_WIDTH = GLA_HEADS * GLA_DV
GLA_GATE_RANK = 16
GLA_TAU = 16.0
LOG2E = 1.4426950408889634
HEAD_DIM = 64
N_Q_HEADS = 8
N_KV_HEADS = 2
SWA_Q = N_Q_HEADS * HEAD_DIM
SWA_KV = N_KV_HEADS * HEAD_DIM
SWA_KV2 = 2 * SWA_KV
WINDOW = 128
ROPE_THETA = 10000.0
PAST_LEN = 8192
D_FF = 4 * D_MODEL
EPS = 1e-6
LANES = 128
GLA_CHUNK = 128
PROMPT_GLA_CHUNK = 128
TOKEN_BLOCK = 512
FUSED_TOKEN_BLOCK = 512
SAMPLE_ROWS = 256
PROMPT_INPROJ_GROUPS = 2
VMEM_LIMIT = 56 * 1024 * 1024

_SEG = {}
_off = 0
for _name, _w in (("gq", GLA_QK), ("gk", GLA_QK), ("gv", GLA_WIDTH), ("gr", GLA_WIDTH),
                  ("sq", SWA_Q), ("sk", SWA_KV), ("sv", SWA_KV), ("glr", LANES)):
    _SEG[_name] = (_off, _off + _w)
    _off += _w
IN_WIDTH_PADDED = _off


def _dot(a, b):
    return jnp.dot(a, b, preferred_element_type=f32)


def _dot_nt(a, b):
    return lax.dot_general(a, b, (((1,), (1,)), ((), ())), preferred_element_type=f32)


def _dot_tn(a, b):
    return lax.dot_general(a, b, (((0,), (0,)), ((), ())), preferred_element_type=f32)


def _sigmoid(x):
    return 1.0 / (1.0 + jnp.exp(-x))


MOD_W_SLABS = 4


def _mod_kernel(c_ref, *refs):
    w_refs, b_ref, o_ref = refs[:-2], refs[-2], refs[-1]
    c = c_ref[...]
    s = (c * _sigmoid(c)).astype(bf16)
    kb = w_refs[0].shape[0]
    res = b_ref[...]
    for i, w_ref in enumerate(w_refs):
        res = res + _dot(s[:, i * kb:(i + 1) * kb], w_ref[...].astype(bf16))
    for r in range(res.shape[0]):
        o_ref[r] = res[r:r + 1, :]


def _modulation(c_all, w_ada, b_ada):
    m = c_all.shape[0]
    n = w_ada.shape[1]
    bn = 1536
    kb = D_MODEL // MOD_W_SLABS
    slabs = [pl.BlockSpec((kb, bn), lambda j, i=i: (i, j)) for i in range(MOD_W_SLABS)]
    return pl.pallas_call(
        _mod_kernel,
        grid=(n // bn,),
        in_specs=[pl.BlockSpec((m, D_MODEL), lambda j: (0, 0))] + slabs
        + [pl.BlockSpec((1, bn), lambda j: (0, j))],
        out_specs=pl.BlockSpec((m, 1, bn), lambda j: (0, 0, j)),
        out_shape=jax.ShapeDtypeStruct((m, 1, n), f32),
        compiler_params=pltpu.CompilerParams(vmem_limit_bytes=VMEM_LIMIT),
        name="adaln_mod",
    )(c_all, *([w_ada] * MOD_W_SLABS), b_ada.reshape(1, n))


def _group_rms(x, bd_ref, w_ref):
    ssq = _dot((x * x).astype(bf16), bd_ref[...])
    return x * lax.rsqrt(ssq * (1.0 / HEAD_DIM) + EPS) * w_ref[...]


def _rope(x, cos, sin_signed, low_half):
    partner = jnp.where(low_half, pltpu.roll(x, LANES - 32, axis=1), pltpu.roll(x, 32, axis=1))
    return x * cos + partner * sin_signed


def _dup_heads(x, low_lanes):
    rolled = pltpu.roll(x, HEAD_DIM, axis=1)
    return jnp.where(low_lanes, x, rolled), jnp.where(low_lanes, rolled, x)


def _inproj_kernel(x_ref, sh_ref, sc_ref, n1_ref, win_ref, wgu_ref, bg_ref, qnw_ref, knw_ref,
                   bdq_ref, bdk_ref, cos_ref, sin_ref, *rest, groups, ncast):
    slabs, rest = rest[:ncast], rest[ncast:]
    gq_ref, gk_ref, gv_ref, gr_ref, la_ref, sq_ref, k2_ref, v2_ref, kk_ref, vk_ref = rest[:10]
    for src, dst in zip(slabs, rest[10:]):
        dst[...] = src[...].astype(dst.dtype)
    nb, t, d = x_ref.shape
    m = nb * t
    tail = kk_ref.shape[0]
    mg = m // groups
    lane = lax.broadcasted_iota(jnp.int32, (mg, LANES), 1)
    low_half = (lane & 32) == 0
    low_lanes = lane < HEAD_DIM
    for grp in range(groups):
        rows = slice(grp * mg, (grp + 1) * mg)
        if nb == 1:
            x = x_ref[:, rows, :]
            sc, sh = sc_ref[...], sh_ref[...]
        else:
            seqs = slice(grp * (nb // groups), (grp + 1) * (nb // groups))
            x = x_ref[seqs]
            sc, sh = sc_ref[seqs], sh_ref[seqs]
        ms = jnp.mean(x * x, axis=-1, keepdims=True)
        hn = x * lax.rsqrt(ms + EPS) * n1_ref[...]
        hn = hn * (1.0 + sc) + sh
        hb = hn.reshape(mg, d).astype(bf16)

        def seg(name):
            a, b = _SEG[name]
            return _dot(hb, win_ref[:, a:b])

        cos = cos_ref[rows, :]
        sin = sin_ref[rows, :]
        sq = _group_rms(seg("sq"), bdq_ref, qnw_ref)
        for c in range(SWA_Q // LANES):
            blk = _rope(sq[:, c * LANES:(c + 1) * LANES], cos, sin, low_half)
            sq_ref[rows, c * LANES:(c + 1) * LANES] = (blk * (LOG2E * HEAD_DIM ** -0.5)).astype(sq_ref.dtype)
        sk = _rope(_group_rms(seg("sk"), bdk_ref, knw_ref), cos, sin, low_half)
        sv = seg("sv")
        keep = (grp + 1) * mg - (m - tail)
        if keep > 0:
            keep = min(keep, mg)
            dst_rows = slice((grp + 1) * mg - keep - (m - tail), (grp + 1) * mg - (m - tail))
            kk_ref[dst_rows, :] = sk[mg - keep:]
            vk_ref[dst_rows, :] = sv[mg - keep:]
        for src, dst in ((sk, k2_ref), (sv, v2_ref)):
            d0, d1 = _dup_heads(src, low_lanes)
            dst[rows, :LANES] = d0.astype(dst.dtype)
            dst[rows, LANES:] = d1.astype(dst.dtype)
        glr = seg("glr").astype(bf16)
        g = _dot(glr, wgu_ref[...]) + bg_ref[...]
        log_sig = jnp.minimum(g, 0.0) - jnp.log1p(jnp.exp(-jnp.abs(g)))
        la_ref[rows, :] = log_sig * (LOG2E / GLA_TAU)
        gq_ref[rows, :] = seg("gq") * (GLA_DK ** -0.5)
        gk_ref[rows, :] = seg("gk")
        gv_ref[rows, :] = seg("gv").astype(gv_ref.dtype)
        gr_ref[rows, :] = seg("gr").astype(gr_ref.dtype)


MOD_SHIFT1, MOD_SCALE1, MOD_GATE1, MOD_SHIFT2, MOD_SCALE2, MOD_GATE2 = range(6)


def _mod_spec(nb, row0, chunk, batch_block):
    return pl.BlockSpec((nb, 1, D_MODEL), lambda *ids: (row0 // nb + batch_block(*ids), 0, chunk))


def _inproj(x, mod, row0, n1, win, wgu, bg, qnw, knw, bdq, bdk, cos_t, sin_t, nb, tb, tail, act, groups,
            to_bf16=()):
    bsz, t, d = x.shape
    m = nb * tb
    nt = t // tb
    grid = (bsz // nb, nt)
    tok = bsz * t
    nsteps = grid[0] * grid[1]
    slab_specs = [pl.BlockSpec((w.shape[0] // nsteps, w.shape[1]), lambda i, j: (i * nt + j, 0))
                  for w in to_bf16]

    def full(shape):
        return pl.BlockSpec(shape, lambda i, j: (0,) * len(shape))

    def out(width):
        return pl.BlockSpec((m, width), lambda i, j: (i * nt + j, 0))

    tail_spec = pl.BlockSpec((tail, SWA_KV), lambda i, j: (i, 0))
    outs = ((GLA_QK, f32), (GLA_QK, f32), (GLA_WIDTH, act), (GLA_WIDTH, act), (GLA_QK, f32),
            (SWA_Q, act), (SWA_KV2, bf16), (SWA_KV2, bf16))
    ntail = (bsz // nb) * tail
    return pl.pallas_call(
        functools.partial(_inproj_kernel, groups=groups, ncast=len(to_bf16)),
        grid=grid,
        in_specs=[pl.BlockSpec((nb, tb, d), lambda i, j: (i, j, 0)),
                  _mod_spec(nb, row0, MOD_SHIFT1, lambda i, j: i),
                  _mod_spec(nb, row0, MOD_SCALE1, lambda i, j: i),
                  full((1, 1, d)),
                  full((d, IN_WIDTH_PADDED)),
                  full((LANES, GLA_QK)),
                  full((1, GLA_QK)),
                  full((1, SWA_Q)),
                  full((1, SWA_KV)),
                  full((SWA_Q, SWA_Q)),
                  full((SWA_KV, SWA_KV)),
                  pl.BlockSpec((m, LANES), lambda i, j: (j, 0)),
                  pl.BlockSpec((m, LANES), lambda i, j: (j, 0))] + slab_specs,
        out_specs=[out(w) for w, _ in outs] + [tail_spec, tail_spec] + slab_specs,
        out_shape=[jax.ShapeDtypeStruct((tok, w), dt) for w, dt in outs]
        + [jax.ShapeDtypeStruct((ntail, SWA_KV), f32)] * 2
        + [jax.ShapeDtypeStruct(w.shape, bf16) for w in to_bf16],
        compiler_params=pltpu.CompilerParams(
            dimension_semantics=("parallel", "arbitrary"), vmem_limit_bytes=VMEM_LIMIT),
        name="inproj",
    )(x, mod, mod, n1, win, wgu, bg, qnw, knw, bdq, bdk, cos_t, sin_t, *to_bf16)


def _gla_constants(chunk, seq):
    t = np.arange(chunk)
    levels = []
    m = 1
    while m < seq:
        levels.append(m)
        m *= 2
    masks = [np.eye(chunk, dtype=bool)]
    for m in levels:
        upper = (t % (2 * m) >= m)[:, None]
        lower = (t % (2 * m) < m)[None, :]
        same = (t[:, None] // (2 * m)) == (t[None, :] // (2 * m))
        masks.append(same & upper & lower)
    tri = t[None, :] <= t[:, None]
    tiled = np.tile(np.stack(masks).astype(np.float32), (1, 1, GLA_HEADS))
    return tuple(levels), jnp.asarray(tri.astype(np.float32), bf16), jnp.asarray(tiled)


def _stack_heads(xb, lane_head, axis):
    zero = jnp.zeros_like(xb)
    return jnp.concatenate([jnp.where(lane_head == h, xb, zero) for h in range(GLA_HEADS)], axis=axis)


def _block_sums(la, bcum, m, row, rolls):
    c, n = la.shape
    if m == 1:
        return la, None
    if m < 8:
        def rolled(shift):
            if shift not in rolls:
                rolls[shift] = pltpu.roll(la, shift % c, axis=0)
            return rolls[shift]
        pos = row & (m - 1)
        pre = la
        suf = None
        for j in range(1, m):
            pre = pre + jnp.where(pos >= j, rolled(j), 0.0)
            term = jnp.where(pos < m - j, rolled(-j), 0.0)
            suf = term if suf is None else suf + term
        return pre, suf
    before, last = [], []
    for i in range(c // m):
        before.append(jnp.zeros((m, n), f32) if i == 0
                      else jnp.broadcast_to(bcum[i * m - 1:i * m], (m, n)))
        last.append(jnp.broadcast_to(bcum[(i + 1) * m - 1:(i + 1) * m], (m, n)))
    if len(before) == 1:
        return bcum - before[0], last[0] - bcum
    return bcum - jnp.concatenate(before, axis=0), jnp.concatenate(last, axis=0) - bcum


def _gla_scores(q_ref, k_ref, v_ref, la_ref, tri_ref, masks_ref, levels, seq):
    c = q_ref.shape[0]
    la = la_ref[...]
    hi = la.astype(bf16)
    lo = (la - hi.astype(f32)).astype(bf16)
    tri = tri_ref[...]
    bcum = _dot(tri, hi) + _dot(tri, lo)
    q = q_ref[...]
    k = k_ref[...]
    vb = v_ref[...].astype(bf16)
    row = lax.broadcasted_iota(jnp.int32, (c, GLA_QK), 0)
    lane_head = lax.broadcasted_iota(jnp.int32, (c, GLA_QK), 1) >> 6
    rolls = {}
    attn = None
    for lvl, m in enumerate((0,) + levels):
        if m == 0:
            qt, kt = q, k
        else:
            pre, suf = _block_sums(la, bcum, m, row, rolls)
            qt = q * jnp.exp2(pre)
            kt = k if suf is None else k * jnp.exp2(suf)
        r = _dot_nt(qt.astype(bf16), _stack_heads(kt.astype(bf16), lane_head, 0))
        rm = r * masks_ref[lvl]
        attn = rm if attn is None else attn + rm
    a = attn.astype(bf16)
    pre, suf = _block_sums(la, bcum, seq, row, rolls)
    qf = (q * jnp.exp2(pre)).astype(bf16)
    kf = (k * jnp.exp2(suf)).astype(bf16)
    return a, qf, kf, vb, bcum, hi, lo


def _gla_values(a, vb):
    v_head = lax.broadcasted_iota(jnp.int32, vb.shape, 1) >> 7
    return _dot(a, _stack_heads(vb, v_head, 0))


def _gla_finish(o, r_ref, gnw, go_ref):
    for h in range(GLA_HEADS):
        sl = slice(h * GLA_DV, (h + 1) * GLA_DV)
        oh = o[:, sl]
        r = r_ref[:, sl].astype(f32)
        ms = jnp.mean(oh * oh, axis=-1, keepdims=True)
        go_ref[:, sl] = (oh * lax.rsqrt(ms + EPS) * gnw * (r * _sigmoid(r))).astype(go_ref.dtype)


def _gla_prompt_block(st, q_ref, k_ref, v_ref, la_ref, r_ref, tri_ref, masks_ref, gnw, go_ref, levels,
                      between=lambda: None):
    c = PROMPT_GLA_CHUNK
    nchunk = q_ref.shape[0] // c
    lane_head = lax.broadcasted_iota(jnp.int32, (GLA_DV, GLA_QK), 1) >> 6
    chunks = []
    for ci in range(nchunk):
        rows = pl.ds(ci * c, c)
        chunks.append(_gla_scores(q_ref.at[rows], k_ref.at[rows], v_ref.at[rows], la_ref.at[rows],
                                  tri_ref, masks_ref, levels, c))
        between()
    intra, upds, decays = [], [], []
    for a, _, kf, vb, bcum, _, _ in chunks:
        intra.append(_gla_values(a, vb))
        full = _dot_tn(vb, kf)
        upd = None
        for h in range(GLA_HEADS):
            term = jnp.where(lane_head == h, full[h * GLA_DV:(h + 1) * GLA_DV], 0.0)
            upd = term if upd is None else upd + term
        upds.append(upd)
        decays.append(jnp.exp2(bcum[c - 1:c]))
    between()
    for ci in range(nchunk):
        rows = pl.ds(ci * c, c)
        sbd_t = _stack_heads(st.astype(bf16), lane_head, 0)
        o = intra[ci] + _dot_nt(chunks[ci][1], sbd_t)
        _gla_finish(o, r_ref.at[rows], gnw, go_ref.at[rows])
        st = decays[ci] * st + upds[ci]
    return st


def _gla_sample_kernel(q_ref, k_ref, v_ref, la_ref, r_ref, s0_ref, tri_ref, masks_ref, gnw_ref,
                       go_ref, sout_ref, *, levels, seq):
    c = q_ref.shape[0]
    a, qf, kf, vb, _, hi, lo = _gla_scores(q_ref, k_ref, v_ref, la_ref, tri_ref, masks_ref, levels, seq)
    o_intra = _gla_values(a, vb)
    row_head = lax.broadcasted_iota(jnp.int32, (GLA_QK, GLA_DV), 0) >> 6
    ones = jnp.ones((seq, GLA_DV), bf16)
    inter = []
    for b in range(c // seq):
        rows = slice(b * seq, (b + 1) * seq)
        s_old = s0_ref[b]
        sbd = _stack_heads(s_old.astype(bf16), row_head, 1)
        inter.append(_dot(qf[rows], sbd))
        p = _dot_tn(kf[rows], vb[rows])
        upd = None
        for h in range(GLA_HEADS):
            term = jnp.where(row_head == h, p[:, h * GLA_DV:(h + 1) * GLA_DV], 0.0)
            upd = term if upd is None else upd + term
        total = _dot_tn(hi[rows], ones) + _dot_tn(lo[rows], ones)
        sout_ref[b] = jnp.exp2(total) * s_old + upd
    o = o_intra + jnp.concatenate(inter, axis=0)
    _gla_finish(o, r_ref, gnw_ref[...], go_ref)


def _gla_sample(gq, gk, gv, la, gr, state, gnw, bsz, t):
    c = GLA_CHUNK
    nseq = c // t
    levels, tri, masks = _gla_constants(c, t)

    def tok(width):
        return pl.BlockSpec((c, width), lambda i: (i, 0))

    return pl.pallas_call(
        functools.partial(_gla_sample_kernel, levels=levels, seq=t),
        grid=(bsz // nseq,),
        in_specs=[tok(GLA_QK), tok(GLA_QK), tok(GLA_WIDTH), tok(GLA_QK), tok(GLA_WIDTH),
                  pl.BlockSpec((nseq, GLA_QK, GLA_DV), lambda i: (i, 0, 0)),
                  pl.BlockSpec(tri.shape, lambda i: (0, 0)),
                  pl.BlockSpec(masks.shape, lambda i: (0, 0, 0)),
                  pl.BlockSpec((1, GLA_DV), lambda i: (0, 0))],
        out_specs=[tok(GLA_WIDTH),
                   pl.BlockSpec((nseq, GLA_QK, GLA_DV), lambda i: (i, 0, 0))],
        out_shape=[jax.ShapeDtypeStruct((bsz * t, GLA_WIDTH), f32),
                   jax.ShapeDtypeStruct((bsz, GLA_QK, GLA_DV), f32)],
        compiler_params=pltpu.CompilerParams(
            dimension_semantics=("parallel",), vmem_limit_bytes=VMEM_LIMIT),
        name="gla_sample",
    )(gq, gk, gv, la, gr, state, tri, masks, gnw)


def _sink_attention(q_ref, rows, kdups, vdups, mask, sink_ref, o_ref, between=lambda: None):
    n = len(kdups)
    heads_per_group = N_Q_HEADS // N_KV_HEADS
    low_q = lax.broadcasted_iota(jnp.int32, (rows, LANES), 1) < HEAD_DIM
    scores = []
    for i in range(n):
        per_group = []
        for g in range(N_KV_HEADS):
            stack = []
            for p in range(heads_per_group // 2):
                pair = g * (heads_per_group // 2) + p
                qp = q_ref[i * rows:(i + 1) * rows, pair * LANES:(pair + 1) * LANES].astype(bf16)
                zero = jnp.zeros_like(qp)
                stack.append(jnp.where(low_q, qp, zero))
                stack.append(jnp.where(low_q, zero, qp))
            per_group.append(_dot_nt(jnp.concatenate(stack, axis=0), kdups[i][g]))
        scores.append(per_group)
    between()
    probs, inv = [], []
    for head in range(N_Q_HEADS):
        g, hh = divmod(head, heads_per_group)
        parts = [scores[i][g][hh * rows:(hh + 1) * rows] for i in range(n)]
        s = parts[0] if n == 1 else jnp.concatenate(parts, axis=0)
        sink = sink_ref[head] * LOG2E
        s = jnp.where(mask, s, -1e30)
        mx = jnp.maximum(jnp.max(s, axis=-1, keepdims=True), sink)
        p = jnp.exp2(s - mx)
        inv.append(1.0 / (jnp.sum(p, axis=-1, keepdims=True) + jnp.exp2(sink - mx)))
        probs.append(p.astype(bf16))
    between()
    outs = []
    for i in range(n):
        per_group = []
        for g in range(N_KV_HEADS):
            p_i = jnp.concatenate([probs[g * heads_per_group + hh][i * rows:(i + 1) * rows]
                                   for hh in range(heads_per_group)], axis=0)
            per_group.append(_dot(p_i, vdups[i][g]))
        outs.append(per_group)
    low_all = lax.broadcasted_iota(jnp.int32, (n * rows, LANES), 1) < HEAD_DIM
    for pair in range(N_Q_HEADS // 2):
        halves = []
        for head in (2 * pair, 2 * pair + 1):
            g, hh = divmod(head, heads_per_group)
            parts = [outs[i][g][hh * rows:(hh + 1) * rows] for i in range(n)]
            o = parts[0] if n == 1 else jnp.concatenate(parts, axis=0)
            halves.append(o * inv[head])
        o_ref[:, pair * LANES:(pair + 1) * LANES] = jnp.where(low_all, halves[0], halves[1]).astype(o_ref.dtype)


def _swa_prompt_blocks(q_ref, k_before, k2_ref, v_before, v2_ref, first, sink_ref, o_ref,
                       between=lambda: None):
    w = WINDOW
    nblk = q_ref.shape[0] // w

    def dup(before, ref):
        blocks = [before] + [ref[i * w:(i + 1) * w, :] for i in range(nblk)]
        return [[jnp.concatenate([blocks[i][:, g * LANES:(g + 1) * LANES],
                                  blocks[i + 1][:, g * LANES:(g + 1) * LANES]], axis=0)
                 for g in range(N_KV_HEADS)] for i in range(nblk)]

    row = lax.broadcasted_iota(jnp.int32, (nblk * w, 2 * w), 0)
    tk = lax.broadcasted_iota(jnp.int32, (nblk * w, 2 * w), 1)
    rel = tk - (row & (w - 1))
    first_key = jnp.where(row < w, jnp.where(first, w, 0), 0)
    mask = (rel > 0) & (rel <= w) & (tk >= first_key)
    _sink_attention(q_ref, w, dup(k_before, k2_ref), dup(v_before, v2_ref), mask, sink_ref, o_ref, between)


def _swa_sample_kernel(sink_ref, q_ref, kn_ref, vn_ref, pk_ref, pv_ref, o_ref, ko_ref, vo_ref, *, seq):
    nseq, _, _, w = pk_ref.shape
    rows = nseq * seq
    heads_per_group = N_Q_HEADS // N_KV_HEADS
    lane = lax.broadcasted_iota(jnp.int32, (rows, LANES), 1)
    r_id = lax.broadcasted_iota(jnp.int32, (rows, LANES), 0)
    low = lane < HEAD_DIM
    pos = r_id & (seq - 1)
    mask_old = lane > pos
    seq_shift = seq.bit_length() - 1
    mask_new = ((r_id >> seq_shift) == (lane >> seq_shift)) & ((lane & (seq - 1)) <= pos)
    kn = kn_ref[...]
    vn = vn_ref[...]
    kn_dup = [d.astype(bf16) for d in _dup_heads(kn, low)]
    vn_dup = [d.astype(bf16) for d in _dup_heads(vn, low)]

    pad = jnp.zeros((w - seq, LANES), f32)
    tail_lanes = lax.broadcasted_iota(jnp.int32, (HEAD_DIM, w), 1) >= w - seq
    kt_dup, vt_dup = [], []
    for b in range(nseq):
        tok = slice(b * seq, (b + 1) * seq)
        per_k, per_v = [], []
        for new, old_ref, out_ref, per in ((kn, pk_ref, ko_ref, per_k), (vn, pv_ref, vo_ref, per_v)):
            new_t = jnp.concatenate([pad, new[tok]], axis=0).T
            for g in range(N_KV_HEADS):
                old = old_ref[b, g]
                out_ref[b, g] = jnp.where(tail_lanes, new_t[g * HEAD_DIM:(g + 1) * HEAD_DIM],
                                          pltpu.roll(old, w - seq, axis=1))
                ob = old.astype(bf16)
                per.append(jnp.concatenate([ob, ob], axis=0))
        kt_dup.append(per_k)
        vt_dup.append(per_v)

    lhs = []
    for g in range(N_KV_HEADS):
        stack = []
        for p in range(heads_per_group // 2):
            pair = g * (heads_per_group // 2) + p
            qp = q_ref[:, pair * LANES:(pair + 1) * LANES].astype(bf16)
            zero = jnp.zeros_like(qp)
            stack.append(jnp.where(low, qp, zero))
            stack.append(jnp.where(low, zero, qp))
        lhs.append(jnp.concatenate(stack, axis=0))
    s_new = [_dot_nt(lhs[g], kn_dup[g]) for g in range(N_KV_HEADS)]
    s_old = []
    for b in range(nseq):
        per = []
        for g in range(N_KV_HEADS):
            qb = jnp.concatenate([lhs[g][hh * rows + b * seq:hh * rows + (b + 1) * seq]
                                  for hh in range(heads_per_group)], axis=0)
            per.append(_dot(qb, kt_dup[b][g]))
        s_old.append(per)

    p_old, p_new, inv = [], [], []
    for head in range(N_Q_HEADS):
        g, hh = divmod(head, heads_per_group)
        so = jnp.concatenate([s_old[b][g][hh * seq:(hh + 1) * seq] for b in range(nseq)], axis=0)
        so = jnp.where(mask_old, so, -1e30)
        sn = jnp.where(mask_new, s_new[g][hh * rows:(hh + 1) * rows], -1e30)
        sink = sink_ref[head] * LOG2E
        mx = jnp.maximum(jnp.maximum(jnp.max(so, axis=-1, keepdims=True),
                                     jnp.max(sn, axis=-1, keepdims=True)), sink)
        po = jnp.exp2(so - mx)
        pn = jnp.exp2(sn - mx)
        inv.append(1.0 / (jnp.sum(po, axis=-1, keepdims=True) + jnp.sum(pn, axis=-1, keepdims=True)
                          + jnp.exp2(sink - mx)))
        p_old.append(po.astype(bf16))
        p_new.append(pn.astype(bf16))

    o_new = [_dot(jnp.concatenate([p_new[g * heads_per_group + hh] for hh in range(heads_per_group)], axis=0),
                  vn_dup[g]) for g in range(N_KV_HEADS)]
    o_old = []
    for b in range(nseq):
        per = []
        for g in range(N_KV_HEADS):
            pb = jnp.concatenate([p_old[g * heads_per_group + hh][b * seq:(b + 1) * seq]
                                  for hh in range(heads_per_group)], axis=0)
            per.append(_dot_nt(pb, vt_dup[b][g]))
        o_old.append(per)

    for pair in range(N_Q_HEADS // 2):
        halves = []
        for head in (2 * pair, 2 * pair + 1):
            g, hh = divmod(head, heads_per_group)
            old = jnp.concatenate([o_old[b][g][hh * seq:(hh + 1) * seq] for b in range(nseq)], axis=0)
            halves.append((old + o_new[g][hh * rows:(hh + 1) * rows]) * inv[head])
        o_ref[:, pair * LANES:(pair + 1) * LANES] = jnp.where(low, halves[0], halves[1])


def _swa_sample(sq, sk, sv, past_kt, past_vt, sinks, bsz, t):
    nseq = SAMPLE_ROWS // 2 // t
    assert nseq * t == LANES
    tok = lambda width: pl.BlockSpec((nseq * t, width), lambda i, s: (i, 0))
    cache = pl.BlockSpec((nseq,) + past_kt.shape[1:], lambda i, s: (i, 0, 0, 0))
    return pl.pallas_call(
        functools.partial(_swa_sample_kernel, seq=t),
        grid_spec=pltpu.PrefetchScalarGridSpec(
            num_scalar_prefetch=1,
            grid=(bsz // nseq,),
            in_specs=[tok(SWA_Q), tok(SWA_KV), tok(SWA_KV), cache, cache],
            out_specs=[tok(SWA_Q), cache, cache]),
        out_shape=[jax.ShapeDtypeStruct((bsz * t, SWA_Q), f32),
                   jax.ShapeDtypeStruct(past_kt.shape, f32),
                   jax.ShapeDtypeStruct(past_vt.shape, f32)],
        compiler_params=pltpu.CompilerParams(
            dimension_semantics=("parallel",), vmem_limit_bytes=VMEM_LIMIT),
        name="swa_sample",
    )(sinks, sq, sk, sv, past_kt, past_vt)


FFN_CHUNK = 1024
FFN_CHUNK_FUSED = 512


def _out_proj_norm(x, go, so, g1, sh2, sc2, n2, wout_ref):
    nb, t, d = x.shape
    mixed = _dot(go, wout_ref[:GLA_WIDTH, :]) + _dot(so, wout_ref[GLA_WIDTH:, :])
    h = x + g1 * mixed.reshape(nb, t, d)
    ms = jnp.mean(h * h, axis=-1, keepdims=True)
    hn = h * lax.rsqrt(ms + EPS) * n2
    hn = hn * (1.0 + sc2) + sh2
    return h, hn.reshape(nb * t, d).astype(bf16)


def _ffn_piece(hb, w1_ref, w2_ref, c, width):
    cols = slice(c * width, (c + 1) * width)
    a = jnp.maximum(_dot(hb, w1_ref[:, cols]), 0.0)
    return _dot((a * a).astype(bf16), w2_ref[cols, :])


def _out_ffn_block(x, go, so, g1, sh2, sc2, g2, n2, wout_ref, w1_ref, w2_ref):
    h, hb = _out_proj_norm(x, go, so, g1, sh2, sc2, n2, wout_ref)
    ff = _ffn_piece(hb, w1_ref, w2_ref, 0, FFN_CHUNK)
    for c in range(1, D_FF // FFN_CHUNK):
        ff = ff + _ffn_piece(hb, w1_ref, w2_ref, c, FFN_CHUNK)
    return h + g2 * ff.reshape(h.shape)


def _out_ffn_kernel(x_ref, go_ref, so_ref, g1_ref, sh2_ref, sc2_ref, g2_ref, n2_ref,
                    wout_ref, w1_ref, w2_ref, y_ref):
    y_ref[...] = _out_ffn_block(x_ref[...], go_ref[...].astype(bf16), so_ref[...].astype(bf16),
                                g1_ref[...], sh2_ref[...],
                                sc2_ref[...], g2_ref[...], n2_ref[...], wout_ref, w1_ref, w2_ref)


def _resident(shape):
    return pl.BlockSpec(shape, lambda *_: (0,) * len(shape), pipeline_mode=pl.Buffered(1))


def _out_ffn(x, go, so, mod, row0, n2, wout, w1, w2, nb, tb):
    bsz, t, d = x.shape
    m = nb * tb
    nt = t // tb
    mods = [_mod_spec(nb, row0, chunk, lambda i, j: i)
            for chunk in (MOD_GATE1, MOD_SHIFT2, MOD_SCALE2, MOD_GATE2)]
    xspec = pl.BlockSpec((nb, tb, d), lambda i, j: (i, j, 0))
    return pl.pallas_call(
        _out_ffn_kernel,
        grid=(bsz // nb, nt),
        in_specs=[xspec,
                  pl.BlockSpec((m, GLA_WIDTH), lambda i, j: (i * nt + j, 0)),
                  pl.BlockSpec((m, SWA_Q), lambda i, j: (i * nt + j, 0)),
                  *mods,
                  pl.BlockSpec((1, 1, d), lambda i, j: (0, 0, 0)),
                  _resident((d, d)), _resident((d, D_FF)), _resident((D_FF, d))],
        out_specs=xspec,
        out_shape=jax.ShapeDtypeStruct((bsz, t, d), f32),
        compiler_params=pltpu.CompilerParams(
            dimension_semantics=("parallel", "parallel"), vmem_limit_bytes=VMEM_LIMIT),
        name="out_ffn",
    )(x, go, so, mod, mod, mod, mod, n2, wout, w1, w2)


def _prompt_kernel(sink_ref,
                   q_ref, k_ref, v_ref, la_ref, r_ref, sq_ref, k2_ref, v2_ref, k2p_ref, v2p_ref,
                   tri_ref, masks_ref, gnw_ref,
                   x_ref, g1_ref, sh2_ref, sc2_ref, g2_ref, n2_ref, wout_ref, w1_ref, w2_ref,
                   y_ref, sout_ref, state_ref, mix_ref, *, levels, nt, nsteps):
    s = pl.program_id(0)
    slot = s % 2

    @pl.when(s == 0)
    def _():
        mix_ref[1] = jnp.zeros(mix_ref.shape[1:], mix_ref.dtype)
        state_ref[...] = jnp.zeros_like(state_ref)

    j = jnp.minimum(s, nsteps - 1) % nt
    st_in = state_ref[...]
    st = jnp.where(j == 0, 0.0, st_in)
    gnw = gnw_ref[...]
    out = mix_ref.at[slot]
    mix = mix_ref[1 - slot]
    h, hb = _out_proj_norm(x_ref[...], mix[:, :GLA_WIDTH], mix[:, GLA_WIDTH:], g1_ref[...],
                           sh2_ref[...], sc2_ref[...], n2_ref[...], wout_ref)
    todo = list(range(D_FF // FFN_CHUNK_FUSED))
    ff = []

    def ffn_piece():
        if todo:
            part = _ffn_piece(hb, w1_ref, w2_ref, todo.pop(0), FFN_CHUNK_FUSED)
            ff[:] = [part if not ff else ff[0] + part]

    st = _gla_prompt_block(st, q_ref, k_ref, v_ref, la_ref, r_ref, tri_ref, masks_ref, gnw,
                           out.at[:, pl.ds(0, GLA_WIDTH)], levels, ffn_piece)
    _swa_prompt_blocks(sq_ref, k2p_ref[...], k2_ref, v2p_ref[...], v2_ref, j == 0, sink_ref,
                       out.at[:, pl.ds(GLA_WIDTH, SWA_Q)], ffn_piece)
    while todo:
        ffn_piece()
    st = jnp.where(s < nsteps, st, st_in)
    state_ref[...] = st
    sout_ref[0] = st
    y_ref[...] = h + g2_ref[...] * ff[0].reshape(h.shape)


def _prompt_mix_ffn(x, gq, gk, gv, la, gr, sq, k2, v2, gnw, sinks, mod, row0, n2, wout, w1, w2):
    bsz, t, d = x.shape
    tb = FUSED_TOKEN_BLOCK
    nt = t // tb
    nsteps = bsz * nt
    levels, tri, masks = _gla_constants(PROMPT_GLA_CHUNK, PROMPT_GLA_CHUNK)
    per_blk = tb // WINDOW

    def mix_blk(s):
        return jnp.minimum(s, nsteps - 1)

    def ffn_blk(s):
        return jnp.maximum(s - 1, 0)

    def tok(width):
        return pl.BlockSpec((tb, width), lambda s, _: (mix_blk(s), 0))

    prev = pl.BlockSpec((WINDOW, SWA_KV2), lambda s, _: (jnp.maximum(mix_blk(s) * per_blk - 1, 0), 0))
    xspec = pl.BlockSpec((1, tb, d), lambda s, _: (ffn_blk(s) // nt, ffn_blk(s) % nt, 0))
    mods = [_mod_spec(1, row0, chunk, lambda s, _: ffn_blk(s) // nt)
            for chunk in (MOD_GATE1, MOD_SHIFT2, MOD_SCALE2, MOD_GATE2)]
    return pl.pallas_call(
        functools.partial(_prompt_kernel, levels=levels, nt=nt, nsteps=nsteps),
        grid_spec=pltpu.PrefetchScalarGridSpec(
            num_scalar_prefetch=1,
            grid=(nsteps + 1,),
            in_specs=[tok(GLA_QK), tok(GLA_QK), tok(GLA_WIDTH), tok(GLA_QK), tok(GLA_WIDTH),
                      tok(SWA_Q), tok(SWA_KV2), tok(SWA_KV2), prev, prev,
                      _resident(tri.shape), _resident(masks.shape), _resident((1, GLA_DV)),
                      xspec, *mods, _resident((1, 1, d)),
                      _resident((d, d)), _resident((d, D_FF)), _resident((D_FF, d))],
            out_specs=[xspec,
                       pl.BlockSpec((1, GLA_DV, GLA_QK), lambda s, _: (mix_blk(s) // nt, 0, 0))],
            scratch_shapes=[pltpu.VMEM((GLA_DV, GLA_QK), f32),
                            pltpu.VMEM((2, tb, GLA_WIDTH + SWA_Q), bf16)]),
        out_shape=[jax.ShapeDtypeStruct((bsz, t, d), f32),
                   jax.ShapeDtypeStruct((bsz, GLA_DV, GLA_QK), f32)],
        compiler_params=pltpu.CompilerParams(
            dimension_semantics=("arbitrary",), vmem_limit_bytes=VMEM_LIMIT),
        name="prompt_mix_ffn",
    )(sinks, gq, gk, gv, la, gr, sq, k2, v2, k2, v2, tri, masks, gnw,
      x, mod, mod, mod, mod, n2, wout, w1, w2)


def _rope_tables(pos):
    half = HEAD_DIM // 2
    inv = jnp.power(ROPE_THETA, -jnp.arange(half, dtype=f32) * 2.0 / HEAD_DIM)
    ang = pos.astype(f32)[:, None] * inv[None, :]
    cos = jnp.cos(ang)
    sin = jnp.sin(ang)
    reps = LANES // HEAD_DIM
    return (jnp.tile(jnp.concatenate([cos, cos], axis=-1), (1, reps)),
            jnp.tile(jnp.concatenate([-sin, sin], axis=-1), (1, reps)))


def _block_diag_ones(n, blk):
    idx = np.arange(n) // blk
    return jnp.asarray((idx[:, None] == idx[None, :]).astype(np.float32), bf16)


def _layer_weights(w_in, w_gate_up, b_gate, q_norm_w, k_norm_w, w_out, w_ff1, w_ff2):
    splits = np.cumsum([GLA_QK, GLA_QK, GLA_WIDTH, GLA_WIDTH, GLA_GATE_RANK, SWA_Q, SWA_KV])
    gq, gk, gv, gr, glr, sq, sk, sv = jnp.split(w_in, [int(s) for s in splits], axis=1)
    glr = jnp.pad(glr, ((0, 0), (0, LANES - GLA_GATE_RANK)))
    win = jnp.concatenate([gq, gk, gv, gr, sq, sk, sv, glr], axis=1).astype(bf16)
    wgu = jnp.pad(w_gate_up, ((0, LANES - GLA_GATE_RANK), (0, 0))).astype(bf16)
    return dict(
        win=win, wgu=wgu, bg=b_gate.reshape(1, GLA_QK),
        qnw=jnp.tile(q_norm_w, N_Q_HEADS).reshape(1, SWA_Q),
        knw=jnp.tile(k_norm_w, N_KV_HEADS).reshape(1, SWA_KV),
        ffn_f32=(w_out, w_ff1, w_ff2))


def _decoder_layer(x, mod, row0, pos, state, past_k, past_v, lw, n1, n2, gnw, sinks, bdq, bdk, ffn_w):
    bsz, t, d = x.shape
    cos_t, sin_t = _rope_tables(pos)
    prompt = state is None
    if prompt:
        nb, tb, tail, act, groups = 1, TOKEN_BLOCK, WINDOW, bf16, PROMPT_INPROJ_GROUPS
    else:
        nb, tb, tail, act, groups = SAMPLE_ROWS // t, t, SAMPLE_ROWS, f32, 1
        cos_t = jnp.tile(cos_t, (nb, 1))
        sin_t = jnp.tile(sin_t, (nb, 1))
    n2 = n2.reshape(1, 1, d)
    gq, gk, gv, gr, la, sq, k2, v2, kk, vk, *made = _inproj(
        x, mod, row0, n1.reshape(1, 1, d), lw["win"], lw["wgu"], lw["bg"], lw["qnw"], lw["knw"],
        bdq, bdk, cos_t, sin_t, nb, tb, tail, act, groups, lw["ffn_f32"] if ffn_w is None else ())
    if ffn_w is None:
        ffn_w = tuple(made)
    if prompt:
        y, s_t = _prompt_mix_ffn(x, gq, gk, gv, la, gr, sq, k2, v2, gnw, sinks, mod, row0, n2, *ffn_w)
        s_new = s_t.reshape(bsz, GLA_DV, GLA_HEADS, GLA_DK).transpose(0, 2, 3, 1)
        k_keep, v_keep = kk, vk
    else:
        go, s_new = _gla_sample(gq, gk, gv, la, gr, state.reshape(bsz, GLA_QK, GLA_DV), gnw, bsz, t)
        so, kt, vt = _swa_sample(sq, kk, vk, past_k.transpose(0, 2, 3, 1), past_v.transpose(0, 2, 3, 1),
                                 sinks, bsz, t)
        k_keep, v_keep = kt.transpose(0, 3, 1, 2), vt.transpose(0, 3, 1, 2)
        y = _out_ffn(x, go, so, mod, row0, n2, *ffn_w, nb, tb)
    return (y, s_new.reshape(bsz, GLA_HEADS, GLA_DK, GLA_DV),
            k_keep.reshape(bsz, WINDOW, N_KV_HEADS, HEAD_DIM),
            v_keep.reshape(bsz, WINDOW, N_KV_HEADS, HEAD_DIM), ffn_w)


def kernel(x_prompt, x_sample, state_gla, cache_swa_k, cache_swa_v, c_prompt, c_sample, w_ada, b_ada, norm1_w, norm2_w, w_in, w_gate_up, b_gate, gla_norm_w, q_norm_w, k_norm_w, sinks, w_out, w_ff1, w_ff2):
    depth = w_ada.shape[0]
    bp, tp, _ = x_prompt.shape
    bs, ts, _ = x_sample.shape
    pos_p = jnp.arange(tp)
    pos_s = PAST_LEN + jnp.arange(ts)
    bdq = _block_diag_ones(SWA_Q, HEAD_DIM)
    bdk = _block_diag_ones(SWA_KV, HEAD_DIM)
    c_all = jnp.concatenate([c_sample, c_prompt], axis=0)
    yp, ys = x_prompt, x_sample
    outs = [[] for _ in range(6)]
    for l in range(depth):
        mod = _modulation(c_all, w_ada[l], b_ada[l])
        lw = _layer_weights(w_in[l], w_gate_up[l], b_gate[l], q_norm_w[l], k_norm_w[l],
                            w_out[l], w_ff1[l], w_ff2[l])
        gnw = gla_norm_w[l].reshape(1, GLA_DV)
        common = (lw, norm1_w[l], norm2_w[l], gnw, sinks[l], bdq, bdk)
        yp, gp, kp, vp, ffn_w = _decoder_layer(yp, mod, bs, pos_p, None, None, None, *common, None)
        ys, gs, kq, vq, _ = _decoder_layer(ys, mod, 0, pos_s, state_gla[l], cache_swa_k[l],
                                           cache_swa_v[l], *common, ffn_w)
        for lst, val in zip(outs, (gp, kp, vp, gs, kq, vq)):
            lst.append(val)
    return (yp, ys) + tuple(jnp.stack(o) for o in outs)
```

```python
import functools

import jax
import jax.numpy as jnp
import numpy as np
from jax import lax
from jax.experimental import pallas as pl
from jax.experimental.pallas import tpu as pltpu

f32 = jnp.float32
bf16 = jnp.bfloat16

D_MODEL = 1024
GLA_HEADS = 4
GLA_DK = 64
GLA_DV = 128
GLA_QK = GLA_HEADS * GLA_DK
GLA_WIDTH = GLA_HEADS * GLA_DV
GLA_GATE_RANK = 16
GLA_TAU = 16.0
LOG2E = 1.4426950408889634
HEAD_DIM = 64
N_Q_HEADS = 8
N_KV_HEADS = 2
SWA_Q = N_Q_HEADS * HEAD_DIM
SWA_KV = N_KV_HEADS * HEAD_DIM
SWA_KV2 = 2 * SWA_KV
WINDOW = 128
ROPE_THETA = 10000.0
PAST_LEN = 8192
D_FF = 4 * D_MODEL
EPS = 1e-6
LANES = 128
GLA_CHUNK = 128
PROMPT_GLA_CHUNK = 128
TOKEN_BLOCK = 512
FUSED_TOKEN_BLOCK = 512
SAMPLE_ROWS = 256
PROMPT_INPROJ_GROUPS = 2
VMEM_LIMIT = 56 * 1024 * 1024

_SEG = {}
_off = 0
for _name, _w in (("gq", GLA_QK), ("gk", GLA_QK), ("gv", GLA_WIDTH), ("gr", GLA_WIDTH),
                  ("sq", SWA_Q), ("sk", SWA_KV), ("sv", SWA_KV), ("glr", LANES)):
    _SEG[_name] = (_off, _off + _w)
    _off += _w
IN_WIDTH_PADDED = _off


def _dot(a, b):
    return jnp.dot(a, b, preferred_element_type=f32)


def _dot_nt(a, b):
    return lax.dot_general(a, b, (((1,), (1,)), ((), ())), preferred_element_type=f32)


def _dot_tn(a, b):
    return lax.dot_general(a, b, (((0,), (0,)), ((), ())), preferred_element_type=f32)


def _sigmoid(x):
    return 1.0 / (1.0 + jnp.exp(-x))


MOD_W_SLABS = 4


def _mod_kernel(c_ref, *refs):
    w_refs, b_ref, o_ref = refs[:-2], refs[-2], refs[-1]
    c = c_ref[...]
    s = (c * _sigmoid(c)).astype(bf16)
    kb = w_refs[0].shape[0]
    res = b_ref[...]
    for i, w_ref in enumerate(w_refs):
        res = res + _dot(s[:, i * kb:(i + 1) * kb], w_ref[...].astype(bf16))
    for r in range(res.shape[0]):
        o_ref[r] = res[r:r + 1, :]


def _modulation(c_all, w_ada, b_ada):
    m = c_all.shape[0]
    n = w_ada.shape[1]
    bn = 1536
    kb = D_MODEL // MOD_W_SLABS
    slabs = [pl.BlockSpec((kb, bn), lambda j, i=i: (i, j)) for i in range(MOD_W_SLABS)]
    return pl.pallas_call(
        _mod_kernel,
        grid=(n // bn,),
        in_specs=[pl.BlockSpec((m, D_MODEL), lambda j: (0, 0))] + slabs
        + [pl.BlockSpec((1, bn), lambda j: (0, j))],
        out_specs=pl.BlockSpec((m, 1, bn), lambda j: (0, 0, j)),
        out_shape=jax.ShapeDtypeStruct((m, 1, n), f32),
        compiler_params=pltpu.CompilerParams(vmem_limit_bytes=VMEM_LIMIT),
        name="adaln_mod",
    )(c_all, *([w_ada] * MOD_W_SLABS), b_ada.reshape(1, n))


def _group_rms(x, bd_ref, w_ref):
    ssq = _dot((x * x).astype(bf16), bd_ref[...])
    return x * lax.rsqrt(ssq * (1.0 / HEAD_DIM) + EPS) * w_ref[...]


def _rope(x, cos, sin_signed, low_half):
    partner = jnp.where(low_half, pltpu.roll(x, LANES - 32, axis=1), pltpu.roll(x, 32, axis=1))
    return x * cos + partner * sin_signed


def _dup_heads(x, low_lanes):
    rolled = pltpu.roll(x, HEAD_DIM, axis=1)
    return jnp.where(low_lanes, x, rolled), jnp.where(low_lanes, rolled, x)


def _inproj_kernel(x_ref, sh_ref, sc_ref, n1_ref, win_ref, wgu_ref, bg_ref, qnw_ref, knw_ref,
                   bdq_ref, bdk_ref, cos_ref, sin_ref, *rest, groups, ncast):
    slabs, rest = rest[:ncast], rest[ncast:]
    gq_ref, gk_ref, gv_ref, gr_ref, la_ref, sq_ref, k2_ref, v2_ref, kk_ref, vk_ref = rest[:10]
    for src, dst in zip(slabs, rest[10:]):
        dst[...] = src[...].astype(dst.dtype)
    nb, t, d = x_ref.shape
    m = nb * t
    tail = kk_ref.shape[0]
    mg = m // groups
    lane = lax.broadcasted_iota(jnp.int32, (mg, LANES), 1)
    low_half = (lane & 32) == 0
    low_lanes = lane < HEAD_DIM
    for grp in range(groups):
        rows = slice(grp * mg, (grp + 1) * mg)
        if nb == 1:
            x = x_ref[:, rows, :]
            sc, sh = sc_ref[...], sh_ref[...]
        else:
            seqs = slice(grp * (nb // groups), (grp + 1) * (nb // groups))
            x = x_ref[seqs]
            sc, sh = sc_ref[seqs], sh_ref[seqs]
        ms = jnp.mean(x * x, axis=-1, keepdims=True)
        hn = x * lax.rsqrt(ms + EPS) * n1_ref[...]
        hn = hn * (1.0 + sc) + sh
        hb = hn.reshape(mg, d).astype(bf16)

        def seg(name):
            a, b = _SEG[name]
            return _dot(hb, win_ref[:, a:b])

        cos = cos_ref[rows, :]
        sin = sin_ref[rows, :]
        sq = _group_rms(seg("sq"), bdq_ref, qnw_ref)
        for c in range(SWA_Q // LANES):
            blk = _rope(sq[:, c * LANES:(c + 1) * LANES], cos, sin, low_half)
            sq_ref[rows, c * LANES:(c + 1) * LANES] = (blk * (LOG2E * HEAD_DIM ** -0.5)).astype(sq_ref.dtype)
        sk = _rope(_group_rms(seg("sk"), bdk_ref, knw_ref), cos, sin, low_half)
        sv = seg("sv")
        keep = (grp + 1) * mg - (m - tail)
        if keep > 0:
            keep = min(keep, mg)
            dst_rows = slice((grp + 1) * mg - keep - (m - tail), (grp + 1) * mg - (m - tail))
            kk_ref[dst_rows, :] = sk[mg - keep:]
            vk_ref[dst_rows, :] = sv[mg - keep:]
        for src, dst in ((sk, k2_ref), (sv, v2_ref)):
            d0, d1 = _dup_heads(src, low_lanes)
            dst[rows, :LANES] = d0.astype(dst.dtype)
            dst[rows, LANES:] = d1.astype(dst.dtype)
        glr = seg("glr").astype(bf16)
        g = _dot(glr, wgu_ref[...]) + bg_ref[...]
        log_sig = jnp.minimum(g, 0.0) - jnp.log1p(jnp.exp(-jnp.abs(g)))
        la_ref[rows, :] = log_sig * (LOG2E / GLA_TAU)
        gq_ref[rows, :] = seg("gq") * (GLA_DK ** -0.5)
        gk_ref[rows, :] = seg("gk")
        gv_ref[rows, :] = seg("gv").astype(gv_ref.dtype)
        gr_ref[rows, :] = seg("gr").astype(gr_ref.dtype)


MOD_SHIFT1, MOD_SCALE1, MOD_GATE1, MOD_SHIFT2, MOD_SCALE2, MOD_GATE2 = range(6)


def _mod_spec(nb, row0, chunk, batch_block):
    return pl.BlockSpec((nb, 1, D_MODEL), lambda *ids: (row0 // nb + batch_block(*ids), 0, chunk))


def _inproj(x, mod, row0, n1, win, wgu, bg, qnw, knw, bdq, bdk, cos_t, sin_t, nb, tb, tail, act, groups,
            to_bf16=()):
    bsz, t, d = x.shape
    m = nb * tb
    nt = t // tb
    grid = (bsz // nb, nt)
    tok = bsz * t
    nsteps = grid[0] * grid[1]
    slab_specs = [pl.BlockSpec((w.shape[0] // nsteps, w.shape[1]), lambda i, j: (i * nt + j, 0))
                  for w in to_bf16]

    def full(shape):
        return pl.BlockSpec(shape, lambda i, j: (0,) * len(shape))

    def out(width):
        return pl.BlockSpec((m, width), lambda i, j: (i * nt + j, 0))

    tail_spec = pl.BlockSpec((tail, SWA_KV), lambda i, j: (i, 0))
    outs = ((GLA_QK, f32), (GLA_QK, f32), (GLA_WIDTH, act), (GLA_WIDTH, act), (GLA_QK, f32),
            (SWA_Q, act), (SWA_KV2, bf16), (SWA_KV2, bf16))
    ntail = (bsz // nb) * tail
    return pl.pallas_call(
        functools.partial(_inproj_kernel, groups=groups, ncast=len(to_bf16)),
        grid=grid,
        in_specs=[pl.BlockSpec((nb, tb, d), lambda i, j: (i, j, 0)),
                  _mod_spec(nb, row0, MOD_SHIFT1, lambda i, j: i),
                  _mod_spec(nb, row0, MOD_SCALE1, lambda i, j: i),
                  full((1, 1, d)),
                  full((d, IN_WIDTH_PADDED)),
                  full((LANES, GLA_QK)),
                  full((1, GLA_QK)),
                  full((1, SWA_Q)),
                  full((1, SWA_KV)),
                  full((SWA_Q, SWA_Q)),
                  full((SWA_KV, SWA_KV)),
                  pl.BlockSpec((m, LANES), lambda i, j: (j, 0)),
                  pl.BlockSpec((m, LANES), lambda i, j: (j, 0))] + slab_specs,
        out_specs=[out(w) for w, _ in outs] + [tail_spec, tail_spec] + slab_specs,
        out_shape=[jax.ShapeDtypeStruct((tok, w), dt) for w, dt in outs]
        + [jax.ShapeDtypeStruct((ntail, SWA_KV), f32)] * 2
        + [jax.ShapeDtypeStruct(w.shape, bf16) for w in to_bf16],
        compiler_params=pltpu.CompilerParams(
            dimension_semantics=("parallel", "arbitrary"), vmem_limit_bytes=VMEM_LIMIT),
        name="inproj",
    )(x, mod, mod, n1, win, wgu, bg, qnw, knw, bdq, bdk, cos_t, sin_t, *to_bf16)


def _gla_constants(chunk, seq):
    t = np.arange(chunk)
    levels = []
    m = 1
    while m < seq:
        levels.append(m)
        m *= 2
    masks = [np.eye(chunk, dtype=bool)]
    for m in levels:
        upper = (t % (2 * m) >= m)[:, None]
        lower = (t % (2 * m) < m)[None, :]
        same = (t[:, None] // (2 * m)) == (t[None, :] // (2 * m))
        masks.append(same & upper & lower)
    tri = t[None, :] <= t[:, None]
    tiled = np.tile(np.stack(masks).astype(np.float32), (1, 1, GLA_HEADS))
    return tuple(levels), jnp.asarray(tri.astype(np.float32), bf16), jnp.asarray(tiled)


def _stack_heads(xb, lane_head, axis):
    zero = jnp.zeros_like(xb)
    return jnp.concatenate([jnp.where(lane_head == h, xb, zero) for h in range(GLA_HEADS)], axis=axis)


def _block_sums(la, bcum, m, row, rolls):
    c, n = la.shape
    if m == 1:
        return la, None
    if m < 8:
        def rolled(shift):
            if shift not in rolls:
                rolls[shift] = pltpu.roll(la, shift % c, axis=0)
            return rolls[shift]
        pos = row & (m - 1)
        pre = la
        suf = None
        for j in range(1, m):
            pre = pre + jnp.where(pos >= j, rolled(j), 0.0)
            term = jnp.where(pos < m - j, rolled(-j), 0.0)
            suf = term if suf is None else suf + term
        return pre, suf
    before, last = [], []
    for i in range(c // m):
        before.append(jnp.zeros((m, n), f32) if i == 0
                      else jnp.broadcast_to(bcum[i * m - 1:i * m], (m, n)))
        last.append(jnp.broadcast_to(bcum[(i + 1) * m - 1:(i + 1) * m], (m, n)))
    if len(before) == 1:
        return bcum - before[0], last[0] - bcum
    return bcum - jnp.concatenate(before, axis=0), jnp.concatenate(last, axis=0) - bcum


def _gla_scores(q_ref, k_ref, v_ref, la_ref, tri_ref, masks_ref, levels, seq):
    c = q_ref.shape[0]
    la = la_ref[...]
    hi = la.astype(bf16)
    lo = (la - hi.astype(f32)).astype(bf16)
    tri = tri_ref[...]
    bcum = _dot(tri, hi) + _dot(tri, lo)
    q = q_ref[...]
    k = k_ref[...]
    vb = v_ref[...].astype(bf16)
    row = lax.broadcasted_iota(jnp.int32, (c, GLA_QK), 0)
    lane_head = lax.broadcasted_iota(jnp.int32, (c, GLA_QK), 1) >> 6
    rolls = {}
    attn = None
    for lvl, m in enumerate((0,) + levels):
        if m == 0:
            qt, kt = q, k
        else:
            pre, suf = _block_sums(la, bcum, m, row, rolls)
            qt = q * jnp.exp2(pre)
            kt = k if suf is None else k * jnp.exp2(suf)
        r = _dot_nt(qt.astype(bf16), _stack_heads(kt.astype(bf16), lane_head, 0))
        rm = r * masks_ref[lvl]
        attn = rm if attn is None else attn + rm
    a = attn.astype(bf16)
    pre, suf = _block_sums(la, bcum, seq, row, rolls)
    qf = (q * jnp.exp2(pre)).astype(bf16)
    kf = (k * jnp.exp2(suf)).astype(bf16)
    return a, qf, kf, vb, bcum, hi, lo


def _gla_values(a, vb):
    v_head = lax.broadcasted_iota(jnp.int32, vb.shape, 1) >> 7
    return _dot(a, _stack_heads(vb, v_head, 0))


def _gla_finish(o, r_ref, gnw, go_ref):
    for h in range(GLA_HEADS):
        sl = slice(h * GLA_DV, (h + 1) * GLA_DV)
        oh = o[:, sl]
        r = r_ref[:, sl].astype(f32)
        ms = jnp.mean(oh * oh, axis=-1, keepdims=True)
        go_ref[:, sl] = (oh * lax.rsqrt(ms + EPS) * gnw * (r * _sigmoid(r))).astype(go_ref.dtype)


def _gla_prompt_block(st, q_ref, k_ref, v_ref, la_ref, r_ref, tri_ref, masks_ref, gnw, go_ref, levels,
                      between=lambda: None):
    c = PROMPT_GLA_CHUNK
    nchunk = q_ref.shape[0] // c
    lane_head = lax.broadcasted_iota(jnp.int32, (GLA_DV, GLA_QK), 1) >> 6
    chunks = []
    for ci in range(nchunk):
        rows = pl.ds(ci * c, c)
        chunks.append(_gla_scores(q_ref.at[rows], k_ref.at[rows], v_ref.at[rows], la_ref.at[rows],
                                  tri_ref, masks_ref, levels, c))
        between()
    intra, upds, decays = [], [], []
    for a, _, kf, vb, bcum, _, _ in chunks:
        intra.append(_gla_values(a, vb))
        full = _dot_tn(vb, kf)
        upd = None
        for h in range(GLA_HEADS):
            term = jnp.where(lane_head == h, full[h * GLA_DV:(h + 1) * GLA_DV], 0.0)
            upd = term if upd is None else upd + term
        upds.append(upd)
        decays.append(jnp.exp2(bcum[c - 1:c]))
    between()
    for ci in range(nchunk):
        rows = pl.ds(ci * c, c)
        sbd_t = _stack_heads(st.astype(bf16), lane_head, 0)
        o = intra[ci] + _dot_nt(chunks[ci][1], sbd_t)
        _gla_finish(o, r_ref.at[rows], gnw, go_ref.at[rows])
        st = decays[ci] * st + upds[ci]
    return st


def _gla_sample_kernel(q_ref, k_ref, v_ref, la_ref, r_ref, s0_ref, tri_ref, masks_ref, gnw_ref,
                       go_ref, sout_ref, *, levels, seq):
    c = q_ref.shape[0]
    a, qf, kf, vb, _, hi, lo = _gla_scores(q_ref, k_ref, v_ref, la_ref, tri_ref, masks_ref, levels, seq)
    o_intra = _gla_values(a, vb)
    row_head = lax.broadcasted_iota(jnp.int32, (GLA_QK, GLA_DV), 0) >> 6
    ones = jnp.ones((seq, GLA_DV), bf16)
    inter = []
    for b in range(c // seq):
        rows = slice(b * seq, (b + 1) * seq)
        s_old = s0_ref[b]
        sbd = _stack_heads(s_old.astype(bf16), row_head, 1)
        inter.append(_dot(qf[rows], sbd))
        p = _dot_tn(kf[rows], vb[rows])
        upd = None
        for h in range(GLA_HEADS):
            term = jnp.where(row_head == h, p[:, h * GLA_DV:(h + 1) * GLA_DV], 0.0)
            upd = term if upd is None else upd + term
        total = _dot_tn(hi[rows], ones) + _dot_tn(lo[rows], ones)
        sout_ref[b] = jnp.exp2(total) * s_old + upd
    o = o_intra + jnp.concatenate(inter, axis=0)
    _gla_finish(o, r_ref, gnw_ref[...], go_ref)


def _gla_sample(gq, gk, gv, la, gr, state, gnw, bsz, t):
    c = GLA_CHUNK
    nseq = c // t
    levels, tri, masks = _gla_constants(c, t)

    def tok(width):
        return pl.BlockSpec((c, width), lambda i: (i, 0))

    return pl.pallas_call(
        functools.partial(_gla_sample_kernel, levels=levels, seq=t),
        grid=(bsz // nseq,),
        in_specs=[tok(GLA_QK), tok(GLA_QK), tok(GLA_WIDTH), tok(GLA_QK), tok(GLA_WIDTH),
                  pl.BlockSpec((nseq, GLA_QK, GLA_DV), lambda i: (i, 0, 0)),
                  pl.BlockSpec(tri.shape, lambda i: (0, 0)),
                  pl.BlockSpec(masks.shape, lambda i: (0, 0, 0)),
                  pl.BlockSpec((1, GLA_DV), lambda i: (0, 0))],
        out_specs=[tok(GLA_WIDTH),
                   pl.BlockSpec((nseq, GLA_QK, GLA_DV), lambda i: (i, 0, 0))],
        out_shape=[jax.ShapeDtypeStruct((bsz * t, GLA_WIDTH), f32),
                   jax.ShapeDtypeStruct((bsz, GLA_QK, GLA_DV), f32)],
        compiler_params=pltpu.CompilerParams(
            dimension_semantics=("parallel",), vmem_limit_bytes=VMEM_LIMIT),
        name="gla_sample",
    )(gq, gk, gv, la, gr, state, tri, masks, gnw)


def _sink_attention(q_ref, rows, kdups, vdups, mask, sink_ref, o_ref, between=lambda: None):
    n = len(kdups)
    heads_per_group = N_Q_HEADS // N_KV_HEADS
    low_q = lax.broadcasted_iota(jnp.int32, (rows, LANES), 1) < HEAD_DIM
    scores = []
    for i in range(n):
        per_group = []
        for g in range(N_KV_HEADS):
            stack = []
            for p in range(heads_per_group // 2):
                pair = g * (heads_per_group // 2) + p
                qp = q_ref[i * rows:(i + 1) * rows, pair * LANES:(pair + 1) * LANES].astype(bf16)
                zero = jnp.zeros_like(qp)
                stack.append(jnp.where(low_q, qp, zero))
                stack.append(jnp.where(low_q, zero, qp))
            per_group.append(_dot_nt(jnp.concatenate(stack, axis=0), kdups[i][g]))
        scores.append(per_group)
    between()
    probs, inv = [], []
    for head in range(N_Q_HEADS):
        g, hh = divmod(head, heads_per_group)
        parts = [scores[i][g][hh * rows:(hh + 1) * rows] for i in range(n)]
        s = parts[0] if n == 1 else jnp.concatenate(parts, axis=0)
        sink = sink_ref[head] * LOG2E
        s = jnp.where(mask, s, -1e30)
        mx = jnp.maximum(jnp.max(s, axis=-1, keepdims=True), sink)
        p = jnp.exp2(s - mx)
        inv.append(1.0 / (jnp.sum(p, axis=-1, keepdims=True) + jnp.exp2(sink - mx)))
        probs.append(p.astype(bf16))
    between()
    outs = []
    for i in range(n):
        per_group = []
        for g in range(N_KV_HEADS):
            p_i = jnp.concatenate([probs[g * heads_per_group + hh][i * rows:(i + 1) * rows]
                                   for hh in range(heads_per_group)], axis=0)
            per_group.append(_dot(p_i, vdups[i][g]))
        outs.append(per_group)
    low_all = lax.broadcasted_iota(jnp.int32, (n * rows, LANES), 1) < HEAD_DIM
    for pair in range(N_Q_HEADS // 2):
        halves = []
        for head in (2 * pair, 2 * pair + 1):
            g, hh = divmod(head, heads_per_group)
            parts = [outs[i][g][hh * rows:(hh + 1) * rows] for i in range(n)]
            o = parts[0] if n == 1 else jnp.concatenate(parts, axis=0)
            halves.append(o * inv[head])
        o_ref[:, pair * LANES:(pair + 1) * LANES] = jnp.where(low_all, halves[0], halves[1]).astype(o_ref.dtype)


def _swa_prompt_blocks(q_ref, k_before, k2_ref, v_before, v2_ref, first, sink_ref, o_ref,
                       between=lambda: None):
    w = WINDOW
    nblk = q_ref.shape[0] // w

    def dup(before, ref):
        blocks = [before] + [ref[i * w:(i + 1) * w, :] for i in range(nblk)]
        return [[jnp.concatenate([blocks[i][:, g * LANES:(g + 1) * LANES],
                                  blocks[i + 1][:, g * LANES:(g + 1) * LANES]], axis=0)
                 for g in range(N_KV_HEADS)] for i in range(nblk)]

    row = lax.broadcasted_iota(jnp.int32, (nblk * w, 2 * w), 0)
    tk = lax.broadcasted_iota(jnp.int32, (nblk * w, 2 * w), 1)
    rel = tk - (row & (w - 1))
    first_key = jnp.where(row < w, jnp.where(first, w, 0), 0)
    mask = (rel > 0) & (rel <= w) & (tk >= first_key)
    _sink_attention(q_ref, w, dup(k_before, k2_ref), dup(v_before, v2_ref), mask, sink_ref, o_ref, between)


def _swa_sample_kernel(sink_ref, q_ref, kn_ref, vn_ref, pk_ref, pv_ref, o_ref, ko_ref, vo_ref, *, seq):
    nseq, _, _, w = pk_ref.shape
    rows = nseq * seq
    heads_per_group = N_Q_HEADS // N_KV_HEADS
    lane = lax.broadcasted_iota(jnp.int32, (rows, LANES), 1)
    r_id = lax.broadcasted_iota(jnp.int32, (rows, LANES), 0)
    low = lane < HEAD_DIM
    pos = r_id & (seq - 1)
    mask_old = lane > pos
    seq_shift = seq.bit_length() - 1
    mask_new = ((r_id >> seq_shift) == (lane >> seq_shift)) & ((lane & (seq - 1)) <= pos)
    kn = kn_ref[...]
    vn = vn_ref[...]
    kn_dup = [d.astype(bf16) for d in _dup_heads(kn, low)]
    vn_dup = [d.astype(bf16) for d in _dup_heads(vn, low)]

    pad = jnp.zeros((w - seq, LANES), f32)
    tail_lanes = lax.broadcasted_iota(jnp.int32, (HEAD_DIM, w), 1) >= w - seq
    kt_dup, vt_dup = [], []
    for b in range(nseq):
        tok = slice(b * seq, (b + 1) * seq)
        per_k, per_v = [], []
        for new, old_ref, out_ref, per in ((kn, pk_ref, ko_ref, per_k), (vn, pv_ref, vo_ref, per_v)):
            new_t = jnp.concatenate([pad, new[tok]], axis=0).T
            for g in range(N_KV_HEADS):
                old = old_ref[b, g]
                out_ref[b, g] = jnp.where(tail_lanes, new_t[g * HEAD_DIM:(g + 1) * HEAD_DIM],
                                          pltpu.roll(old, w - seq, axis=1))
                ob = old.astype(bf16)
                per.append(jnp.concatenate([ob, ob], axis=0))
        kt_dup.append(per_k)
        vt_dup.append(per_v)

    lhs = []
    for g in range(N_KV_HEADS):
        stack = []
        for p in range(heads_per_group // 2):
            pair = g * (heads_per_group // 2) + p
            qp = q_ref[:, pair * LANES:(pair + 1) * LANES].astype(bf16)
            zero = jnp.zeros_like(qp)
            stack.append(jnp.where(low, qp, zero))
            stack.append(jnp.where(low, zero, qp))
        lhs.append(jnp.concatenate(stack, axis=0))
    s_new = [_dot_nt(lhs[g], kn_dup[g]) for g in range(N_KV_HEADS)]
    s_old = []
    for b in range(nseq):
        per = []
        for g in range(N_KV_HEADS):
            qb = jnp.concatenate([lhs[g][hh * rows + b * seq:hh * rows + (b + 1) * seq]
                                  for hh in range(heads_per_group)], axis=0)
            per.append(_dot(qb, kt_dup[b][g]))
        s_old.append(per)

    p_old, p_new, inv = [], [], []
    for head in range(N_Q_HEADS):
        g, hh = divmod(head, heads_per_group)
        so = jnp.concatenate([s_old[b][g][hh * seq:(hh + 1) * seq] for b in range(nseq)], axis=0)
        so = jnp.where(mask_old, so, -1e30)
        sn = jnp.where(mask_new, s_new[g][hh * rows:(hh + 1) * rows], -1e30)
        sink = sink_ref[head] * LOG2E
        mx = jnp.maximum(jnp.maximum(jnp.max(so, axis=-1, keepdims=True),
                                     jnp.max(sn, axis=-1, keepdims=True)), sink)
        po = jnp.exp2(so - mx)
        pn = jnp.exp2(sn - mx)
        inv.append(1.0 / (jnp.sum(po, axis=-1, keepdims=True) + jnp.sum(pn, axis=-1, keepdims=True)
                          + jnp.exp2(sink - mx)))
        p_old.append(po.astype(bf16))
        p_new.append(pn.astype(bf16))

    o_new = [_dot(jnp.concatenate([p_new[g * heads_per_group + hh] for hh in range(heads_per_group)], axis=0),
                  vn_dup[g]) for g in range(N_KV_HEADS)]
    o_old = []
    for b in range(nseq):
        per = []
        for g in range(N_KV_HEADS):
            pb = jnp.concatenate([p_old[g * heads_per_group + hh][b * seq:(b + 1) * seq]
                                  for hh in range(heads_per_group)], axis=0)
            per.append(_dot_nt(pb, vt_dup[b][g]))
        o_old.append(per)

    for pair in range(N_Q_HEADS // 2):
        halves = []
        for head in (2 * pair, 2 * pair + 1):
            g, hh = divmod(head, heads_per_group)
            old = jnp.concatenate([o_old[b][g][hh * seq:(hh + 1) * seq] for b in range(nseq)], axis=0)
            halves.append((old + o_new[g][hh * rows:(hh + 1) * rows]) * inv[head])
        o_ref[:, pair * LANES:(pair + 1) * LANES] = jnp.where(low, halves[0], halves[1])


def _swa_sample(sq, sk, sv, past_kt, past_vt, sinks, bsz, t):
    nseq = SAMPLE_ROWS // 2 // t
    assert nseq * t == LANES
    tok = lambda width: pl.BlockSpec((nseq * t, width), lambda i, s: (i, 0))
    cache = pl.BlockSpec((nseq,) + past_kt.shape[1:], lambda i, s: (i, 0, 0, 0))
    return pl.pallas_call(
        functools.partial(_swa_sample_kernel, seq=t),
        grid_spec=pltpu.PrefetchScalarGridSpec(
            num_scalar_prefetch=1,
            grid=(bsz // nseq,),
            in_specs=[tok(SWA_Q), tok(SWA_KV), tok(SWA_KV), cache, cache],
            out_specs=[tok(SWA_Q), cache, cache]),
        out_shape=[jax.ShapeDtypeStruct((bsz * t, SWA_Q), f32),
                   jax.ShapeDtypeStruct(past_kt.shape, f32),
                   jax.ShapeDtypeStruct(past_vt.shape, f32)],
        compiler_params=pltpu.CompilerParams(
            dimension_semantics=("parallel",), vmem_limit_bytes=VMEM_LIMIT),
        name="swa_sample",
    )(sinks, sq, sk, sv, past_kt, past_vt)


FFN_CHUNK = 512


def _out_proj_norm(streams, n2, wout_ref):
    mix = [s[1] for s in streams]
    mix = mix[0] if len(mix) == 1 else jnp.concatenate(mix, axis=0)
    mixed = _dot(mix[:, :GLA_WIDTH], wout_ref[:GLA_WIDTH, :]) + _dot(mix[:, GLA_WIDTH:], wout_ref[GLA_WIDTH:, :])
    hs, hbs, r0 = [], [], 0
    for x, _, g1, sh2, sc2 in streams:
        nb, t, d = x.shape
        h = x + g1 * mixed[r0:r0 + nb * t].reshape(nb, t, d)
        ms = jnp.mean(h * h, axis=-1, keepdims=True)
        hn = h * lax.rsqrt(ms + EPS) * n2
        hn = hn * (1.0 + sc2) + sh2
        hs.append(h)
        hbs.append(hn.reshape(nb * t, d).astype(bf16))
        r0 += nb * t
    return hs, (hbs[0] if len(hbs) == 1 else jnp.concatenate(hbs, axis=0))


def _ffn_piece(hb, w1_ref, w2_ref, c, width):
    cols = slice(c * width, (c + 1) * width)
    a = jnp.maximum(_dot(hb, w1_ref[:, cols]), 0.0)
    return _dot((a * a).astype(bf16), w2_ref[cols, :])


def _resident(shape):
    return pl.BlockSpec(shape, lambda *_: (0,) * len(shape), pipeline_mode=pl.Buffered(1))


def _prompt_kernel(sink_ref,
                   q_ref, k_ref, v_ref, la_ref, r_ref, sq_ref, k2_ref, v2_ref, k2p_ref, v2p_ref,
                   tri_ref, masks_ref, gnw_ref,
                   x_ref, g1_ref, sh2_ref, sc2_ref, g2_ref,
                   xs_ref, gos_ref, sos_ref, g1s_ref, sh2s_ref, sc2s_ref, g2s_ref,
                   n2_ref, wout_ref, w1_ref, w2_ref,
                   y_ref, ys_ref, sout_ref, state_ref, mix_ref, *, levels, nt, nsteps):
    s = pl.program_id(0)
    slot = s % 2

    @pl.when(s == 0)
    def _():
        mix_ref[1] = jnp.zeros(mix_ref.shape[1:], mix_ref.dtype)
        state_ref[...] = jnp.zeros_like(state_ref)

    j = jnp.minimum(s, nsteps - 1) % nt
    st_in = state_ref[...]
    st = jnp.where(j == 0, 0.0, st_in)
    gnw = gnw_ref[...]
    out = mix_ref.at[slot]
    mix_s = jnp.concatenate([gos_ref[...], sos_ref[...]], axis=1).astype(bf16)
    (h, h_s), hb = _out_proj_norm(
        [(x_ref[...], mix_ref[1 - slot], g1_ref[...], sh2_ref[...], sc2_ref[...]),
         (xs_ref[...], mix_s, g1s_ref[...], sh2s_ref[...], sc2s_ref[...])], n2_ref[...], wout_ref)
    rows_p = h.shape[0] * h.shape[1]
    todo = list(range(D_FF // FFN_CHUNK))
    ff = []

    def ffn_piece():
        if todo:
            part = _ffn_piece(hb, w1_ref, w2_ref, todo.pop(0), FFN_CHUNK)
            ff[:] = [part if not ff else ff[0] + part]

    st = _gla_prompt_block(st, q_ref, k_ref, v_ref, la_ref, r_ref, tri_ref, masks_ref, gnw,
                           out.at[:, pl.ds(0, GLA_WIDTH)], levels, ffn_piece)
    _swa_prompt_blocks(sq_ref, k2p_ref[...], k2_ref, v2p_ref[...], v2_ref, j == 0, sink_ref,
                       out.at[:, pl.ds(GLA_WIDTH, SWA_Q)], ffn_piece)
    while todo:
        ffn_piece()
    st = jnp.where(s < nsteps, st, st_in)
    state_ref[...] = st
    sout_ref[0] = st
    y_ref[...] = h + g2_ref[...] * ff[0][:rows_p].reshape(h.shape)
    ys_ref[...] = h_s + g2s_ref[...] * ff[0][rows_p:].reshape(h_s.shape)


def _prompt_mix_ffn(x, gq, gk, gv, la, gr, sq, k2, v2, gnw, sinks, mod, row0, n2, wout, w1, w2,
                    xs, gos, sos, row0_s):
    bsz, t, d = x.shape
    tb = FUSED_TOKEN_BLOCK
    nt = t // tb
    nsteps = bsz * nt
    levels, tri, masks = _gla_constants(PROMPT_GLA_CHUNK, PROMPT_GLA_CHUNK)
    per_blk = tb // WINDOW
    bs, ts, _ = xs.shape
    nbs = bs // nsteps
    assert nbs * nsteps == bs and (nbs * ts) % 16 == 0

    def mix_blk(s):
        return jnp.minimum(s, nsteps - 1)

    def ffn_blk(s):
        return jnp.maximum(s - 1, 0)

    def tok(width):
        return pl.BlockSpec((tb, width), lambda s, _: (mix_blk(s), 0))

    prev = pl.BlockSpec((WINDOW, SWA_KV2), lambda s, _: (jnp.maximum(mix_blk(s) * per_blk - 1, 0), 0))
    xspec = pl.BlockSpec((1, tb, d), lambda s, _: (ffn_blk(s) // nt, ffn_blk(s) % nt, 0))
    ffn_chunks = (MOD_GATE1, MOD_SHIFT2, MOD_SCALE2, MOD_GATE2)
    mods = [_mod_spec(1, row0, chunk, lambda s, _: ffn_blk(s) // nt) for chunk in ffn_chunks]
    mods_s = [_mod_spec(nbs, row0_s, chunk, lambda s, _: ffn_blk(s)) for chunk in ffn_chunks]
    xs_spec = pl.BlockSpec((nbs, ts, d), lambda s, _: (ffn_blk(s), 0, 0))
    mix_s_spec = pl.BlockSpec((nbs * ts, GLA_WIDTH), lambda s, _: (ffn_blk(s), 0))
    return pl.pallas_call(
        functools.partial(_prompt_kernel, levels=levels, nt=nt, nsteps=nsteps),
        grid_spec=pltpu.PrefetchScalarGridSpec(
            num_scalar_prefetch=1,
            grid=(nsteps + 1,),
            in_specs=[tok(GLA_QK), tok(GLA_QK), tok(GLA_WIDTH), tok(GLA_QK), tok(GLA_WIDTH),
                      tok(SWA_Q), tok(SWA_KV2), tok(SWA_KV2), prev, prev,
                      _resident(tri.shape), _resident(masks.shape), _resident((1, GLA_DV)),
                      xspec, *mods, xs_spec, mix_s_spec, mix_s_spec, *mods_s, _resident((1, 1, d)),
                      _resident((d, d)), _resident((d, D_FF)), _resident((D_FF, d))],
            out_specs=[xspec, xs_spec,
                       pl.BlockSpec((1, GLA_DV, GLA_QK), lambda s, _: (mix_blk(s) // nt, 0, 0))],
            scratch_shapes=[pltpu.VMEM((GLA_DV, GLA_QK), f32),
                            pltpu.VMEM((2, tb, GLA_WIDTH + SWA_Q), bf16)]),
        out_shape=[jax.ShapeDtypeStruct((bsz, t, d), f32),
                   jax.ShapeDtypeStruct(xs.shape, f32),
                   jax.ShapeDtypeStruct((bsz, GLA_DV, GLA_QK), f32)],
        compiler_params=pltpu.CompilerParams(
            dimension_semantics=("arbitrary",), vmem_limit_bytes=VMEM_LIMIT),
        name="prompt_mix_ffn",
    )(sinks, gq, gk, gv, la, gr, sq, k2, v2, k2, v2, tri, masks, gnw,
      x, mod, mod, mod, mod, xs, gos, sos, mod, mod, mod, mod, n2, wout, w1, w2)


def _rope_tables(pos):
    half = HEAD_DIM // 2
    inv = jnp.power(ROPE_THETA, -jnp.arange(half, dtype=f32) * 2.0 / HEAD_DIM)
    ang = pos.astype(f32)[:, None] * inv[None, :]
    cos = jnp.cos(ang)
    sin = jnp.sin(ang)
    reps = LANES // HEAD_DIM
    return (jnp.tile(jnp.concatenate([cos, cos], axis=-1), (1, reps)),
            jnp.tile(jnp.concatenate([-sin, sin], axis=-1), (1, reps)))


def _block_diag_ones(n, blk):
    idx = np.arange(n) // blk
    return jnp.asarray((idx[:, None] == idx[None, :]).astype(np.float32), bf16)


def _layer_weights(w_in, w_gate_up, b_gate, q_norm_w, k_norm_w, w_out, w_ff1, w_ff2):
    splits = np.cumsum([GLA_QK, GLA_QK, GLA_WIDTH, GLA_WIDTH, GLA_GATE_RANK, SWA_Q, SWA_KV])
    gq, gk, gv, gr, glr, sq, sk, sv = jnp.split(w_in, [int(s) for s in splits], axis=1)
    glr = jnp.pad(glr, ((0, 0), (0, LANES - GLA_GATE_RANK)))
    win = jnp.concatenate([gq, gk, gv, gr, sq, sk, sv, glr], axis=1).astype(bf16)
    wgu = jnp.pad(w_gate_up, ((0, LANES - GLA_GATE_RANK), (0, 0))).astype(bf16)
    return dict(
        win=win, wgu=wgu, bg=b_gate.reshape(1, GLA_QK),
        qnw=jnp.tile(q_norm_w, N_Q_HEADS).reshape(1, SWA_Q),
        knw=jnp.tile(k_norm_w, N_KV_HEADS).reshape(1, SWA_KV),
        ffn_f32=(w_out, w_ff1, w_ff2))


def _project(x, mod, row0, pos, lw, n1, bdq, bdk, prompt):
    bsz, t, d = x.shape
    cos_t, sin_t = _rope_tables(pos)
    if prompt:
        nb, tb, tail, act, groups, cast = 1, TOKEN_BLOCK, WINDOW, bf16, PROMPT_INPROJ_GROUPS, lw["ffn_f32"]
    else:
        nb, tb, tail, act, groups, cast = SAMPLE_ROWS // t, t, SAMPLE_ROWS, f32, 1, ()
        cos_t = jnp.tile(cos_t, (nb, 1))
        sin_t = jnp.tile(sin_t, (nb, 1))
    return _inproj(x, mod, row0, n1.reshape(1, 1, d), lw["win"], lw["wgu"], lw["bg"], lw["qnw"], lw["knw"],
                   bdq, bdk, cos_t, sin_t, nb, tb, tail, act, groups, cast)


def _decoder_layers(xp, xs, mod, pos_p, pos_s, state, past_k, past_v, lw, n1, n2, gnw, sinks, bdq, bdk):
    bp, tp, d = xp.shape
    bs, ts, _ = xs.shape
    heads = (N_KV_HEADS, HEAD_DIM)
    gq, gk, gv, gr, la, sq, _, _, kk, vk = _project(xs, mod, 0, pos_s, lw, n1, bdq, bdk, False)
    go_s, state_s = _gla_sample(gq, gk, gv, la, gr, state.reshape(bs, GLA_QK, GLA_DV), gnw, bs, ts)
    so_s, kt, vt = _swa_sample(sq, kk, vk, past_k.transpose(0, 2, 3, 1), past_v.transpose(0, 2, 3, 1),
                               sinks, bs, ts)
    gq, gk, gv, gr, la, sq, k2, v2, kk, vk, *ffn_w = _project(xp, mod, bs, pos_p, lw, n1, bdq, bdk, True)
    yp, ys, s_t = _prompt_mix_ffn(xp, gq, gk, gv, la, gr, sq, k2, v2, gnw, sinks, mod, bs,
                                  n2.reshape(1, 1, d), *ffn_w, xs, go_s, so_s, 0)
    state_p = s_t.reshape(bp, GLA_DV, GLA_HEADS, GLA_DK).transpose(0, 2, 3, 1)
    return (yp, ys, state_p, kk.reshape(bp, WINDOW, *heads), vk.reshape(bp, WINDOW, *heads),
            state_s.reshape(bs, GLA_HEADS, GLA_DK, GLA_DV),
            kt.transpose(0, 3, 1, 2), vt.transpose(0, 3, 1, 2))


def kernel(x_prompt, x_sample, state_gla, cache_swa_k, cache_swa_v, c_prompt, c_sample, w_ada, b_ada, norm1_w, norm2_w, w_in, w_gate_up, b_gate, gla_norm_w, q_norm_w, k_norm_w, sinks, w_out, w_ff1, w_ff2):
    depth = w_ada.shape[0]
    bp, tp, _ = x_prompt.shape
    bs, ts, _ = x_sample.shape
    pos_p = jnp.arange(tp)
    pos_s = PAST_LEN + jnp.arange(ts)
    bdq = _block_diag_ones(SWA_Q, HEAD_DIM)
    bdk = _block_diag_ones(SWA_KV, HEAD_DIM)
    c_all = jnp.concatenate([c_sample, c_prompt], axis=0)
    yp, ys = x_prompt, x_sample
    outs = [[] for _ in range(6)]
    for l in range(depth):
        mod = _modulation(c_all, w_ada[l], b_ada[l])
        lw = _layer_weights(w_in[l], w_gate_up[l], b_gate[l], q_norm_w[l], k_norm_w[l],
                            w_out[l], w_ff1[l], w_ff2[l])
        gnw = gla_norm_w[l].reshape(1, GLA_DV)
        yp, ys, *new = _decoder_layers(yp, ys, mod, pos_p, pos_s, state_gla[l], cache_swa_k[l],
                                       cache_swa_v[l], lw, norm1_w[l], norm2_w[l], gnw, sinks[l], bdq, bdk)
        for lst, val in zip(outs, new):
            lst.append(val)
    return (yp, ys) + tuple(jnp.stack(o) for o in outs)
```

```python
import functools

import jax
import jax.numpy as jnp
import numpy as np
from jax import lax
from jax.experimental import pallas as pl
from jax.experimental.pallas import tpu as pltpu

f32 = jnp.float32
bf16 = jnp.bfloat16

D_MODEL = 1024
GLA_HEADS = 4
GLA_DK = 64
GLA_DV = 128
GLA_QK = GLA_HEADS * GLA_DK
GLA_WIDTH = GLA_HEADS * GLA_DV
GLA_GATE_RANK = 16
GLA_TAU = 16.0
LOG2E = 1.4426950408889634
HEAD_DIM = 64
N_Q_HEADS = 8
N_KV_HEADS = 2
SWA_Q = N_Q_HEADS * HEAD_DIM
SWA_KV = N_KV_HEADS * HEAD_DIM
SWA_KV2 = 2 * SWA_KV
WINDOW = 128
ROPE_THETA = 10000.0
PAST_LEN = 8192
D_FF = 4 * D_MODEL
EPS = 1e-6
LANES = 128
GLA_CHUNK = 128
PROMPT_GLA_CHUNK = 128
TOKEN_BLOCK = 512
FUSED_TOKEN_BLOCK = 512
SAMPLE_ROWS = 256
PROMPT_INPROJ_GROUPS = 2
VMEM_LIMIT = 56 * 1024 * 1024

_SEG = {}
_off = 0
for _name, _w in (("gq", GLA_QK), ("gk", GLA_QK), ("gv", GLA_WIDTH), ("gr", GLA_WIDTH),
                  ("sq", SWA_Q), ("sk", SWA_KV), ("sv", SWA_KV), ("glr", LANES)):
    _SEG[_name] = (_off, _off + _w)
    _off += _w
IN_WIDTH_PADDED = _off


def _dot(a, b):
    return jnp.dot(a, b, preferred_element_type=f32)


def _dot_nt(a, b):
    return lax.dot_general(a, b, (((1,), (1,)), ((), ())), preferred_element_type=f32)


def _dot_tn(a, b):
    return lax.dot_general(a, b, (((0,), (0,)), ((), ())), preferred_element_type=f32)


def _sigmoid(x):
    return 1.0 / (1.0 + jnp.exp(-x))


MOD_W_SLABS = 4


def _mod_kernel(ca_ref, cb_ref, *refs):
    w_refs, b_ref, o_ref = refs[:-2], refs[-2], refs[-1]
    c = jnp.concatenate([ca_ref[...], cb_ref[...]], axis=0)
    s = (c * _sigmoid(c)).astype(bf16)
    kb = w_refs[0].shape[0]
    res = b_ref[...]
    for i, w_ref in enumerate(w_refs):
        res = res + _dot(s[:, i * kb:(i + 1) * kb], w_ref[...].astype(bf16))
    for r in range(res.shape[0]):
        o_ref[r] = res[r:r + 1, :]


def _modulation(c_a, c_b, w_ada, b_ada):
    m = c_a.shape[0] + c_b.shape[0]
    n = w_ada.shape[1]
    bn = 1536
    kb = D_MODEL // MOD_W_SLABS
    slabs = [pl.BlockSpec((kb, bn), lambda j, i=i: (i, j)) for i in range(MOD_W_SLABS)]
    return pl.pallas_call(
        _mod_kernel,
        grid=(n // bn,),
        in_specs=[pl.BlockSpec(c_a.shape, lambda j: (0, 0)), pl.BlockSpec(c_b.shape, lambda j: (0, 0))]
        + slabs + [pl.BlockSpec((1, bn), lambda j: (0, j))],
        out_specs=pl.BlockSpec((m, 1, bn), lambda j: (0, 0, j)),
        out_shape=jax.ShapeDtypeStruct((m, 1, n), f32),
        compiler_params=pltpu.CompilerParams(vmem_limit_bytes=VMEM_LIMIT),
        name="adaln_mod",
    )(c_a, c_b, *([w_ada] * MOD_W_SLABS), b_ada.reshape(1, n))


def _group_rms(x, bd_ref, w_ref):
    ssq = _dot((x * x).astype(bf16), bd_ref[...])
    return x * lax.rsqrt(ssq * (1.0 / HEAD_DIM) + EPS) * w_ref[...]


def _rope(x, cos, sin_signed, low_half):
    partner = jnp.where(low_half, pltpu.roll(x, LANES - 32, axis=1), pltpu.roll(x, 32, axis=1))
    return x * cos + partner * sin_signed


def _dup_heads(x, low_lanes):
    rolled = pltpu.roll(x, HEAD_DIM, axis=1)
    return jnp.where(low_lanes, x, rolled), jnp.where(low_lanes, rolled, x)


def _inproj_kernel(x_ref, sh_ref, sc_ref, n1_ref, win_ref, wgu_ref, bg_ref, qnw_ref, knw_ref,
                   bdq_ref, bdk_ref, cos_ref, sin_ref, *rest, groups, ncast, tail, tail_t):
    slabs, rest = rest[:ncast], rest[ncast:]
    gq_ref, gk_ref, gv_ref, gr_ref, la_ref, sq_ref, k2_ref, v2_ref, kk_ref, vk_ref = rest[:10]
    for src, dst in zip(slabs, rest[10:]):
        dst[...] = src[...].astype(dst.dtype)
    nb, t, d = x_ref.shape
    m = nb * t
    mg = m // groups
    lane = lax.broadcasted_iota(jnp.int32, (mg, LANES), 1)
    low_half = (lane & 32) == 0
    low_lanes = lane < HEAD_DIM
    for grp in range(groups):
        rows = slice(grp * mg, (grp + 1) * mg)
        if nb == 1:
            x = x_ref[:, rows, :]
            sc, sh = sc_ref[...], sh_ref[...]
        else:
            seqs = slice(grp * (nb // groups), (grp + 1) * (nb // groups))
            x = x_ref[seqs]
            sc, sh = sc_ref[seqs], sh_ref[seqs]
        ms = jnp.mean(x * x, axis=-1, keepdims=True)
        hn = x * lax.rsqrt(ms + EPS) * n1_ref[...]
        hn = hn * (1.0 + sc) + sh
        hb = hn.reshape(mg, d).astype(bf16)

        def seg(name):
            a, b = _SEG[name]
            return _dot(hb, win_ref[:, a:b])

        cos = cos_ref[rows, :]
        sin = sin_ref[rows, :]
        sq = _group_rms(seg("sq"), bdq_ref, qnw_ref)
        for c in range(SWA_Q // LANES):
            blk = _rope(sq[:, c * LANES:(c + 1) * LANES], cos, sin, low_half)
            sq_ref[rows, c * LANES:(c + 1) * LANES] = (blk * (LOG2E * HEAD_DIM ** -0.5)).astype(sq_ref.dtype)
        sk = _rope(_group_rms(seg("sk"), bdk_ref, knw_ref), cos, sin, low_half)
        sv = seg("sv")
        keep = (grp + 1) * mg - (m - tail)
        if keep > 0:
            keep = min(keep, mg)
            dst_rows = slice((grp + 1) * mg - keep - (m - tail), (grp + 1) * mg - (m - tail))
            if tail_t:
                assert keep == tail
                kk_ref[...] = sk[mg - keep:].T
                vk_ref[...] = sv[mg - keep:].T
            else:
                kk_ref[dst_rows, :] = sk[mg - keep:]
                vk_ref[dst_rows, :] = sv[mg - keep:]
        for src, dst in ((sk, k2_ref), (sv, v2_ref)):
            d0, d1 = _dup_heads(src, low_lanes)
            dst[rows, :LANES] = d0.astype(dst.dtype)
            dst[rows, LANES:] = d1.astype(dst.dtype)
        glr = seg("glr").astype(bf16)
        g = _dot(glr, wgu_ref[...]) + bg_ref[...]
        log_sig = jnp.minimum(g, 0.0) - jnp.log1p(jnp.exp(-jnp.abs(g)))
        la_ref[rows, :] = log_sig * (LOG2E / GLA_TAU)
        gq_ref[rows, :] = seg("gq") * (GLA_DK ** -0.5)
        gk_ref[rows, :] = seg("gk")
        gv_ref[rows, :] = seg("gv").astype(gv_ref.dtype)
        gr_ref[rows, :] = seg("gr").astype(gr_ref.dtype)


MOD_SHIFT1, MOD_SCALE1, MOD_GATE1, MOD_SHIFT2, MOD_SCALE2, MOD_GATE2 = range(6)


def _mod_spec(nb, row0, chunk, batch_block):
    return pl.BlockSpec((nb, 1, D_MODEL), lambda *ids: (row0 // nb + batch_block(*ids), 0, chunk))


def _inproj(x, mod, row0, n1, win, wgu, bg, qnw, knw, bdq, bdk, cos_t, sin_t, nb, tb, tail, act, groups,
            to_bf16=(), tail_t=False):
    bsz, t, d = x.shape
    m = nb * tb
    nt = t // tb
    grid = (bsz // nb, nt)
    tok = bsz * t
    nsteps = grid[0] * grid[1]
    slab_specs = [pl.BlockSpec((w.shape[0] // nsteps, w.shape[1]), lambda i, j: (i * nt + j, 0))
                  for w in to_bf16]

    def full(shape):
        return pl.BlockSpec(shape, lambda i, j: (0,) * len(shape))

    def out(width):
        return pl.BlockSpec((m, width), lambda i, j: (i * nt + j, 0))

    tail_blk = (SWA_KV, tail) if tail_t else (tail, SWA_KV)
    tail_spec = pl.BlockSpec(tail_blk, lambda i, j: (i, 0))
    tail_shape = jax.ShapeDtypeStruct(((bsz // nb) * tail_blk[0], tail_blk[1]), f32)
    outs = ((GLA_QK, f32), (GLA_QK, f32), (GLA_WIDTH, act), (GLA_WIDTH, act), (GLA_QK, f32),
            (SWA_Q, act), (SWA_KV2, bf16), (SWA_KV2, bf16))
    return pl.pallas_call(
        functools.partial(_inproj_kernel, groups=groups, ncast=len(to_bf16), tail=tail, tail_t=tail_t),
        grid=grid,
        in_specs=[pl.BlockSpec((nb, tb, d), lambda i, j: (i, j, 0)),
                  _mod_spec(nb, row0, MOD_SHIFT1, lambda i, j: i),
                  _mod_spec(nb, row0, MOD_SCALE1, lambda i, j: i),
                  full((1, 1, d)),
                  full((d, IN_WIDTH_PADDED)),
                  full((LANES, GLA_QK)),
                  full((1, GLA_QK)),
                  full((1, SWA_Q)),
                  full((1, SWA_KV)),
                  full((SWA_Q, SWA_Q)),
                  full((SWA_KV, SWA_KV)),
                  pl.BlockSpec((m, LANES), lambda i, j: (j, 0)),
                  pl.BlockSpec((m, LANES), lambda i, j: (j, 0))] + slab_specs,
        out_specs=[out(w) for w, _ in outs] + [tail_spec, tail_spec] + slab_specs,
        out_shape=[jax.ShapeDtypeStruct((tok, w), dt) for w, dt in outs]
        + [tail_shape] * 2
        + [jax.ShapeDtypeStruct(w.shape, bf16) for w in to_bf16],
        compiler_params=pltpu.CompilerParams(
            dimension_semantics=("parallel", "arbitrary"), vmem_limit_bytes=VMEM_LIMIT),
        name="inproj",
    )(x, mod, mod, n1, win, wgu, bg, qnw, knw, bdq, bdk, cos_t, sin_t, *to_bf16)


def _gla_constants(chunk, seq):
    t = np.arange(chunk)
    levels = []
    m = 1
    while m < seq:
        levels.append(m)
        m *= 2
    masks = [np.eye(chunk, dtype=bool)]
    for m in levels:
        upper = (t % (2 * m) >= m)[:, None]
        lower = (t % (2 * m) < m)[None, :]
        same = (t[:, None] // (2 * m)) == (t[None, :] // (2 * m))
        masks.append(same & upper & lower)
    tri = t[None, :] <= t[:, None]
    tiled = np.tile(np.stack(masks).astype(np.float32), (1, 1, GLA_HEADS))
    return tuple(levels), jnp.asarray(tri.astype(np.float32), bf16), jnp.asarray(tiled)


def _stack_heads(xb, lane_head, axis):
    zero = jnp.zeros_like(xb)
    return jnp.concatenate([jnp.where(lane_head == h, xb, zero) for h in range(GLA_HEADS)], axis=axis)


def _block_sums(la, bcum, m, row, rolls):
    c, n = la.shape
    if m == 1:
        return la, None
    if m < 8:
        def rolled(shift):
            if shift not in rolls:
                rolls[shift] = pltpu.roll(la, shift % c, axis=0)
            return rolls[shift]
        pos = row & (m - 1)
        pre = la
        suf = None
        for j in range(1, m):
            pre = pre + jnp.where(pos >= j, rolled(j), 0.0)
            term = jnp.where(pos < m - j, rolled(-j), 0.0)
            suf = term if suf is None else suf + term
        return pre, suf
    before, last = [], []
    for i in range(c // m):
        before.append(jnp.zeros((m, n), f32) if i == 0
                      else jnp.broadcast_to(bcum[i * m - 1:i * m], (m, n)))
        last.append(jnp.broadcast_to(bcum[(i + 1) * m - 1:(i + 1) * m], (m, n)))
    if len(before) == 1:
        return bcum - before[0], last[0] - bcum
    return bcum - jnp.concatenate(before, axis=0), jnp.concatenate(last, axis=0) - bcum


def _gla_scores(q_ref, k_ref, v_ref, la_ref, tri_ref, masks_ref, levels, seq):
    c = q_ref.shape[0]
    la = la_ref[...]
    hi = la.astype(bf16)
    lo = (la - hi.astype(f32)).astype(bf16)
    tri = tri_ref[...]
    bcum = _dot(tri, hi) + _dot(tri, lo)
    q = q_ref[...]
    k = k_ref[...]
    vb = v_ref[...].astype(bf16)
    row = lax.broadcasted_iota(jnp.int32, (c, GLA_QK), 0)
    lane_head = lax.broadcasted_iota(jnp.int32, (c, GLA_QK), 1) >> 6
    rolls = {}
    attn = None
    for lvl, m in enumerate((0,) + levels):
        if m == 0:
            qt, kt = q, k
        else:
            pre, suf = _block_sums(la, bcum, m, row, rolls)
            qt = q * jnp.exp2(pre)
            kt = k if suf is None else k * jnp.exp2(suf)
        r = _dot_nt(qt.astype(bf16), _stack_heads(kt.astype(bf16), lane_head, 0))
        rm = r * masks_ref[lvl]
        attn = rm if attn is None else attn + rm
    a = attn.astype(bf16)
    pre, suf = _block_sums(la, bcum, seq, row, rolls)
    qf = (q * jnp.exp2(pre)).astype(bf16)
    kf = (k * jnp.exp2(suf)).astype(bf16)
    return a, qf, kf, vb, bcum, hi, lo


def _gla_values(a, vb):
    v_head = lax.broadcasted_iota(jnp.int32, vb.shape, 1) >> 7
    return _dot(a, _stack_heads(vb, v_head, 0))


def _gla_finish(o, r_ref, gnw, go_ref):
    for h in range(GLA_HEADS):
        sl = slice(h * GLA_DV, (h + 1) * GLA_DV)
        oh = o[:, sl]
        r = r_ref[:, sl].astype(f32)
        ms = jnp.mean(oh * oh, axis=-1, keepdims=True)
        go_ref[:, sl] = (oh * lax.rsqrt(ms + EPS) * gnw * (r * _sigmoid(r))).astype(go_ref.dtype)


def _gla_prompt_block(st, q_ref, k_ref, v_ref, la_ref, r_ref, tri_ref, masks_ref, gnw, go_ref, levels,
                      between=lambda: None):
    c = PROMPT_GLA_CHUNK
    nchunk = q_ref.shape[0] // c
    lane_head = lax.broadcasted_iota(jnp.int32, (GLA_DV, GLA_QK), 1) >> 6
    chunks = []
    for ci in range(nchunk):
        rows = pl.ds(ci * c, c)
        chunks.append(_gla_scores(q_ref.at[rows], k_ref.at[rows], v_ref.at[rows], la_ref.at[rows],
                                  tri_ref, masks_ref, levels, c))
        between()
    intra, upds, decays = [], [], []
    for a, _, kf, vb, bcum, _, _ in chunks:
        intra.append(_gla_values(a, vb))
        full = _dot_tn(vb, kf)
        upd = None
        for h in range(GLA_HEADS):
            term = jnp.where(lane_head == h, full[h * GLA_DV:(h + 1) * GLA_DV], 0.0)
            upd = term if upd is None else upd + term
        upds.append(upd)
        decays.append(jnp.exp2(bcum[c - 1:c]))
    between()
    for ci in range(nchunk):
        rows = pl.ds(ci * c, c)
        sbd_t = _stack_heads(st.astype(bf16), lane_head, 0)
        o = intra[ci] + _dot_nt(chunks[ci][1], sbd_t)
        _gla_finish(o, r_ref.at[rows], gnw, go_ref.at[rows])
        st = decays[ci] * st + upds[ci]
    return st


def _gla_sample_kernel(q_ref, k_ref, v_ref, la_ref, r_ref, s0_ref, tri_ref, masks_ref, gnw_ref,
                       go_ref, sout_ref, *, levels, seq):
    c = q_ref.shape[0]
    a, qf, kf, vb, _, hi, lo = _gla_scores(q_ref, k_ref, v_ref, la_ref, tri_ref, masks_ref, levels, seq)
    o_intra = _gla_values(a, vb)
    row_head = lax.broadcasted_iota(jnp.int32, (GLA_QK, GLA_DV), 0) >> 6
    ones = jnp.ones((seq, GLA_DV), bf16)
    inter = []
    for b in range(c // seq):
        rows = slice(b * seq, (b + 1) * seq)
        s_old = s0_ref[b]
        sbd = _stack_heads(s_old.astype(bf16), row_head, 1)
        inter.append(_dot(qf[rows], sbd))
        p = _dot_tn(kf[rows], vb[rows])
        upd = None
        for h in range(GLA_HEADS):
            term = jnp.where(row_head == h, p[:, h * GLA_DV:(h + 1) * GLA_DV], 0.0)
            upd = term if upd is None else upd + term
        total = _dot_tn(hi[rows], ones) + _dot_tn(lo[rows], ones)
        sout_ref[b] = jnp.exp2(total) * s_old + upd
    o = o_intra + jnp.concatenate(inter, axis=0)
    _gla_finish(o, r_ref, gnw_ref[...], go_ref)


def _gla_sample(gq, gk, gv, la, gr, state, gnw, bsz, t):
    c = GLA_CHUNK
    nseq = c // t
    levels, tri, masks = _gla_constants(c, t)

    def tok(width):
        return pl.BlockSpec((c, width), lambda i: (i, 0))

    return pl.pallas_call(
        functools.partial(_gla_sample_kernel, levels=levels, seq=t),
        grid=(bsz // nseq,),
        in_specs=[tok(GLA_QK), tok(GLA_QK), tok(GLA_WIDTH), tok(GLA_QK), tok(GLA_WIDTH),
                  pl.BlockSpec((nseq, GLA_QK, GLA_DV), lambda i: (i, 0, 0)),
                  pl.BlockSpec(tri.shape, lambda i: (0, 0)),
                  pl.BlockSpec(masks.shape, lambda i: (0, 0, 0)),
                  pl.BlockSpec((1, GLA_DV), lambda i: (0, 0))],
        out_specs=[tok(GLA_WIDTH),
                   pl.BlockSpec((nseq, GLA_QK, GLA_DV), lambda i: (i, 0, 0))],
        out_shape=[jax.ShapeDtypeStruct((bsz * t, GLA_WIDTH), f32),
                   jax.ShapeDtypeStruct((bsz, GLA_QK, GLA_DV), f32)],
        compiler_params=pltpu.CompilerParams(
            dimension_semantics=("parallel",), vmem_limit_bytes=VMEM_LIMIT),
        name="gla_sample",
    )(gq, gk, gv, la, gr, state, tri, masks, gnw)


def _sink_attention(q_ref, rows, kdups, vdups, mask, sink_ref, o_ref, between=lambda: None):
    n = len(kdups)
    heads_per_group = N_Q_HEADS // N_KV_HEADS
    low_q = lax.broadcasted_iota(jnp.int32, (rows, LANES), 1) < HEAD_DIM
    scores = []
    for i in range(n):
        per_group = []
        for g in range(N_KV_HEADS):
            stack = []
            for p in range(heads_per_group // 2):
                pair = g * (heads_per_group // 2) + p
                qp = q_ref[i * rows:(i + 1) * rows, pair * LANES:(pair + 1) * LANES].astype(bf16)
                zero = jnp.zeros_like(qp)
                stack.append(jnp.where(low_q, qp, zero))
                stack.append(jnp.where(low_q, zero, qp))
            per_group.append(_dot_nt(jnp.concatenate(stack, axis=0), kdups[i][g]))
        scores.append(per_group)
    between()
    probs, inv = [], []
    for head in range(N_Q_HEADS):
        g, hh = divmod(head, heads_per_group)
        parts = [scores[i][g][hh * rows:(hh + 1) * rows] for i in range(n)]
        s = parts[0] if n == 1 else jnp.concatenate(parts, axis=0)
        sink = sink_ref[head] * LOG2E
        s = jnp.where(mask, s, -1e30)
        mx = jnp.maximum(jnp.max(s, axis=-1, keepdims=True), sink)
        p = jnp.exp2(s - mx)
        inv.append(1.0 / (jnp.sum(p, axis=-1, keepdims=True) + jnp.exp2(sink - mx)))
        probs.append(p.astype(bf16))
    between()
    outs = []
    for i in range(n):
        per_group = []
        for g in range(N_KV_HEADS):
            p_i = jnp.concatenate([probs[g * heads_per_group + hh][i * rows:(i + 1) * rows]
                                   for hh in range(heads_per_group)], axis=0)
            per_group.append(_dot(p_i, vdups[i][g]))
        outs.append(per_group)
    low_all = lax.broadcasted_iota(jnp.int32, (n * rows, LANES), 1) < HEAD_DIM
    for pair in range(N_Q_HEADS // 2):
        halves = []
        for head in (2 * pair, 2 * pair + 1):
            g, hh = divmod(head, heads_per_group)
            parts = [outs[i][g][hh * rows:(hh + 1) * rows] for i in range(n)]
            o = parts[0] if n == 1 else jnp.concatenate(parts, axis=0)
            halves.append(o * inv[head])
        o_ref[:, pair * LANES:(pair + 1) * LANES] = jnp.where(low_all, halves[0], halves[1]).astype(o_ref.dtype)


def _swa_prompt_blocks(q_ref, k_before, k2_ref, v_before, v2_ref, first, sink_ref, o_ref,
                       between=lambda: None):
    w = WINDOW
    nblk = q_ref.shape[0] // w

    def dup(before, ref):
        blocks = [before] + [ref[i * w:(i + 1) * w, :] for i in range(nblk)]
        return [[jnp.concatenate([blocks[i][:, g * LANES:(g + 1) * LANES],
                                  blocks[i + 1][:, g * LANES:(g + 1) * LANES]], axis=0)
                 for g in range(N_KV_HEADS)] for i in range(nblk)]

    row = lax.broadcasted_iota(jnp.int32, (nblk * w, 2 * w), 0)
    tk = lax.broadcasted_iota(jnp.int32, (nblk * w, 2 * w), 1)
    rel = tk - (row & (w - 1))
    first_key = jnp.where(row < w, jnp.where(first, w, 0), 0)
    mask = (rel > 0) & (rel <= w) & (tk >= first_key)
    _sink_attention(q_ref, w, dup(k_before, k2_ref), dup(v_before, v2_ref), mask, sink_ref, o_ref, between)


def _swa_sample_kernel(sink_ref, q_ref, kn_ref, vn_ref, pk_ref, pv_ref, o_ref, ko_ref, vo_ref, *, seq):
    nseq, _, _, w = pk_ref.shape
    rows = nseq * seq
    heads_per_group = N_Q_HEADS // N_KV_HEADS
    lane = lax.broadcasted_iota(jnp.int32, (rows, LANES), 1)
    r_id = lax.broadcasted_iota(jnp.int32, (rows, LANES), 0)
    low = lane < HEAD_DIM
    pos = r_id & (seq - 1)
    mask_old = lane > pos
    seq_shift = seq.bit_length() - 1
    mask_new = ((r_id >> seq_shift) == (lane >> seq_shift)) & ((lane & (seq - 1)) <= pos)
    kn = kn_ref[...]
    vn = vn_ref[...]
    kn_dup = [d.astype(bf16) for d in _dup_heads(kn, low)]
    vn_dup = [d.astype(bf16) for d in _dup_heads(vn, low)]

    pad = jnp.zeros((w - seq, LANES), f32)
    tail_lanes = lax.broadcasted_iota(jnp.int32, (HEAD_DIM, w), 1) >= w - seq
    kt_dup, vt_dup = [], []
    for b in range(nseq):
        tok = slice(b * seq, (b + 1) * seq)
        per_k, per_v = [], []
        for new, old_ref, out_ref, per in ((kn, pk_ref, ko_ref, per_k), (vn, pv_ref, vo_ref, per_v)):
            new_t = jnp.concatenate([pad, new[tok]], axis=0).T
            for g in range(N_KV_HEADS):
                old = old_ref[b, g]
                out_ref[b, g] = jnp.where(tail_lanes, new_t[g * HEAD_DIM:(g + 1) * HEAD_DIM],
                                          pltpu.roll(old, w - seq, axis=1))
                ob = old.astype(bf16)
                per.append(jnp.concatenate([ob, ob], axis=0))
        kt_dup.append(per_k)
        vt_dup.append(per_v)

    lhs = []
    for g in range(N_KV_HEADS):
        stack = []
        for p in range(heads_per_group // 2):
            pair = g * (heads_per_group // 2) + p
            qp = q_ref[:, pair * LANES:(pair + 1) * LANES].astype(bf16)
            zero = jnp.zeros_like(qp)
            stack.append(jnp.where(low, qp, zero))
            stack.append(jnp.where(low, zero, qp))
        lhs.append(jnp.concatenate(stack, axis=0))
    s_new = [_dot_nt(lhs[g], kn_dup[g]) for g in range(N_KV_HEADS)]
    s_old = []
    for b in range(nseq):
        per = []
        for g in range(N_KV_HEADS):
            qb = jnp.concatenate([lhs[g][hh * rows + b * seq:hh * rows + (b + 1) * seq]
                                  for hh in range(heads_per_group)], axis=0)
            per.append(_dot(qb, kt_dup[b][g]))
        s_old.append(per)

    p_old, p_new, inv = [], [], []
    for head in range(N_Q_HEADS):
        g, hh = divmod(head, heads_per_group)
        so = jnp.concatenate([s_old[b][g][hh * seq:(hh + 1) * seq] for b in range(nseq)], axis=0)
        so = jnp.where(mask_old, so, -1e30)
        sn = jnp.where(mask_new, s_new[g][hh * rows:(hh + 1) * rows], -1e30)
        sink = sink_ref[head] * LOG2E
        mx = jnp.maximum(jnp.maximum(jnp.max(so, axis=-1, keepdims=True),
                                     jnp.max(sn, axis=-1, keepdims=True)), sink)
        po = jnp.exp2(so - mx)
        pn = jnp.exp2(sn - mx)
        inv.append(1.0 / (jnp.sum(po, axis=-1, keepdims=True) + jnp.sum(pn, axis=-1, keepdims=True)
                          + jnp.exp2(sink - mx)))
        p_old.append(po.astype(bf16))
        p_new.append(pn.astype(bf16))

    o_new = [_dot(jnp.concatenate([p_new[g * heads_per_group + hh] for hh in range(heads_per_group)], axis=0),
                  vn_dup[g]) for g in range(N_KV_HEADS)]
    o_old = []
    for b in range(nseq):
        per = []
        for g in range(N_KV_HEADS):
            pb = jnp.concatenate([p_old[g * heads_per_group + hh][b * seq:(b + 1) * seq]
                                  for hh in range(heads_per_group)], axis=0)
            per.append(_dot_nt(pb, vt_dup[b][g]))
        o_old.append(per)

    for pair in range(N_Q_HEADS // 2):
        halves = []
        for head in (2 * pair, 2 * pair + 1):
            g, hh = divmod(head, heads_per_group)
            old = jnp.concatenate([o_old[b][g][hh * seq:(hh + 1) * seq] for b in range(nseq)], axis=0)
            halves.append((old + o_new[g][hh * rows:(hh + 1) * rows]) * inv[head])
        o_ref[:, pair * LANES:(pair + 1) * LANES] = jnp.where(low, halves[0], halves[1])


def _swa_sample(sq, sk, sv, past_kt, past_vt, sinks, bsz, t):
    nseq = SAMPLE_ROWS // 2 // t
    assert nseq * t == LANES
    tok = lambda width: pl.BlockSpec((nseq * t, width), lambda i, s: (i, 0))
    cache = pl.BlockSpec((nseq,) + past_kt.shape[1:], lambda i, s: (i, 0, 0, 0))
    return pl.pallas_call(
        functools.partial(_swa_sample_kernel, seq=t),
        grid_spec=pltpu.PrefetchScalarGridSpec(
            num_scalar_prefetch=1,
            grid=(bsz // nseq,),
            in_specs=[tok(SWA_Q), tok(SWA_KV), tok(SWA_KV), cache, cache],
            out_specs=[tok(SWA_Q), cache, cache]),
        out_shape=[jax.ShapeDtypeStruct((bsz * t, SWA_Q), f32),
                   jax.ShapeDtypeStruct(past_kt.shape, f32),
                   jax.ShapeDtypeStruct(past_vt.shape, f32)],
        compiler_params=pltpu.CompilerParams(
            dimension_semantics=("parallel",), vmem_limit_bytes=VMEM_LIMIT),
        name="swa_sample",
    )(sinks, sq, sk, sv, past_kt, past_vt)


FFN_CHUNK = 512


def _out_proj_norm(streams, n2, wout_ref):
    mix = [s[1] for s in streams]
    mix = mix[0] if len(mix) == 1 else jnp.concatenate(mix, axis=0)
    mixed = _dot(mix[:, :GLA_WIDTH], wout_ref[:GLA_WIDTH, :]) + _dot(mix[:, GLA_WIDTH:], wout_ref[GLA_WIDTH:, :])
    hs, hbs, r0 = [], [], 0
    for x, _, g1, sh2, sc2 in streams:
        nb, t, d = x.shape
        h = x + g1 * mixed[r0:r0 + nb * t].reshape(nb, t, d)
        ms = jnp.mean(h * h, axis=-1, keepdims=True)
        hn = h * lax.rsqrt(ms + EPS) * n2
        hn = hn * (1.0 + sc2) + sh2
        hs.append(h)
        hbs.append(hn.reshape(nb * t, d).astype(bf16))
        r0 += nb * t
    return hs, (hbs[0] if len(hbs) == 1 else jnp.concatenate(hbs, axis=0))


def _ffn_piece(hb, w1_ref, w2_ref, c, width):
    cols = slice(c * width, (c + 1) * width)
    a = jnp.maximum(_dot(hb, w1_ref[:, cols]), 0.0)
    return _dot((a * a).astype(bf16), w2_ref[cols, :])


def _resident(shape):
    return pl.BlockSpec(shape, lambda *_: (0,) * len(shape), pipeline_mode=pl.Buffered(1))


def _prompt_kernel(sink_ref,
                   q_ref, k_ref, v_ref, la_ref, r_ref, sq_ref, k2_ref, v2_ref, k2p_ref, v2p_ref,
                   tri_ref, masks_ref, gnw_ref,
                   x_ref, g1_ref, sh2_ref, sc2_ref, g2_ref,
                   xs_ref, gos_ref, sos_ref, g1s_ref, sh2s_ref, sc2s_ref, g2s_ref,
                   n2_ref, wout_ref, w1_ref, w2_ref,
                   y_ref, ys_ref, sout_ref, state_ref, mix_ref, *, levels, nt, nsteps):
    s = pl.program_id(0)
    slot = s % 2

    @pl.when(s == 0)
    def _():
        mix_ref[1] = jnp.zeros(mix_ref.shape[1:], mix_ref.dtype)
        state_ref[...] = jnp.zeros_like(state_ref)

    j = jnp.minimum(s, nsteps - 1) % nt
    st_in = state_ref[...]
    st = jnp.where(j == 0, 0.0, st_in)
    gnw = gnw_ref[...]
    out = mix_ref.at[slot]
    mix_s = jnp.concatenate([gos_ref[...], sos_ref[...]], axis=1).astype(bf16)
    (h, h_s), hb = _out_proj_norm(
        [(x_ref[...], mix_ref[1 - slot], g1_ref[...], sh2_ref[...], sc2_ref[...]),
         (xs_ref[...], mix_s, g1s_ref[...], sh2s_ref[...], sc2s_ref[...])], n2_ref[...], wout_ref)
    rows_p = h.shape[0] * h.shape[1]
    todo = list(range(D_FF // FFN_CHUNK))
    ff = []

    def ffn_piece():
        if todo:
            part = _ffn_piece(hb, w1_ref, w2_ref, todo.pop(0), FFN_CHUNK)
            ff[:] = [part if not ff else ff[0] + part]

    st = _gla_prompt_block(st, q_ref, k_ref, v_ref, la_ref, r_ref, tri_ref, masks_ref, gnw,
                           out.at[:, pl.ds(0, GLA_WIDTH)], levels, ffn_piece)
    _swa_prompt_blocks(sq_ref, k2p_ref[...], k2_ref, v2p_ref[...], v2_ref, j == 0, sink_ref,
                       out.at[:, pl.ds(GLA_WIDTH, SWA_Q)], ffn_piece)
    while todo:
        ffn_piece()
    st = jnp.where(s < nsteps, st, st_in)
    state_ref[...] = st
    sout_ref[0] = st.T
    y_ref[...] = h + g2_ref[...] * ff[0][:rows_p].reshape(h.shape)
    ys_ref[...] = h_s + g2s_ref[...] * ff[0][rows_p:].reshape(h_s.shape)


def _prompt_mix_ffn(x, gq, gk, gv, la, gr, sq, k2, v2, gnw, sinks, mod, row0, n2, wout, w1, w2,
                    xs, gos, sos, row0_s):
    bsz, t, d = x.shape
    tb = FUSED_TOKEN_BLOCK
    nt = t // tb
    nsteps = bsz * nt
    levels, tri, masks = _gla_constants(PROMPT_GLA_CHUNK, PROMPT_GLA_CHUNK)
    per_blk = tb // WINDOW
    bs, ts, _ = xs.shape
    nbs = bs // nsteps
    assert nbs * nsteps == bs and (nbs * ts) % 16 == 0

    def mix_blk(s):
        return jnp.minimum(s, nsteps - 1)

    def ffn_blk(s):
        return jnp.maximum(s - 1, 0)

    def tok(width):
        return pl.BlockSpec((tb, width), lambda s, _: (mix_blk(s), 0))

    prev = pl.BlockSpec((WINDOW, SWA_KV2), lambda s, _: (jnp.maximum(mix_blk(s) * per_blk - 1, 0), 0))
    xspec = pl.BlockSpec((1, tb, d), lambda s, _: (ffn_blk(s) // nt, ffn_blk(s) % nt, 0))
    ffn_chunks = (MOD_GATE1, MOD_SHIFT2, MOD_SCALE2, MOD_GATE2)
    mods = [_mod_spec(1, row0, chunk, lambda s, _: ffn_blk(s) // nt) for chunk in ffn_chunks]
    mods_s = [_mod_spec(nbs, row0_s, chunk, lambda s, _: ffn_blk(s)) for chunk in ffn_chunks]
    xs_spec = pl.BlockSpec((nbs, ts, d), lambda s, _: (ffn_blk(s), 0, 0))
    mix_s_spec = pl.BlockSpec((nbs * ts, GLA_WIDTH), lambda s, _: (ffn_blk(s), 0))
    return pl.pallas_call(
        functools.partial(_prompt_kernel, levels=levels, nt=nt, nsteps=nsteps),
        grid_spec=pltpu.PrefetchScalarGridSpec(
            num_scalar_prefetch=1,
            grid=(nsteps + 1,),
            in_specs=[tok(GLA_QK), tok(GLA_QK), tok(GLA_WIDTH), tok(GLA_QK), tok(GLA_WIDTH),
                      tok(SWA_Q), tok(SWA_KV2), tok(SWA_KV2), prev, prev,
                      _resident(tri.shape), _resident(masks.shape), _resident((1, GLA_DV)),
                      xspec, *mods, xs_spec, mix_s_spec, mix_s_spec, *mods_s, _resident((1, 1, d)),
                      _resident((d, d)), _resident((d, D_FF)), _resident((D_FF, d))],
            out_specs=[xspec, xs_spec,
                       pl.BlockSpec((1, GLA_QK, GLA_DV), lambda s, _: (mix_blk(s) // nt, 0, 0))],
            scratch_shapes=[pltpu.VMEM((GLA_DV, GLA_QK), f32),
                            pltpu.VMEM((2, tb, GLA_WIDTH + SWA_Q), bf16)]),
        out_shape=[jax.ShapeDtypeStruct((bsz, t, d), f32),
                   jax.ShapeDtypeStruct(xs.shape, f32),
                   jax.ShapeDtypeStruct((bsz, GLA_QK, GLA_DV), f32)],
        compiler_params=pltpu.CompilerParams(
            dimension_semantics=("arbitrary",), vmem_limit_bytes=VMEM_LIMIT),
        name="prompt_mix_ffn",
    )(sinks, gq, gk, gv, la, gr, sq, k2, v2, k2, v2, tri, masks, gnw,
      x, mod, mod, mod, mod, xs, gos, sos, mod, mod, mod, mod, n2, wout, w1, w2)


def _rope_tables(pos, copies):
    half = HEAD_DIM // 2
    inv = np.power(np.float32(ROPE_THETA), -np.arange(half, dtype=np.float32) * np.float32(2.0 / HEAD_DIM))
    ang = np.asarray(pos, np.float32)[:, None] * inv[None, :].astype(np.float32)
    cos = np.cos(ang).astype(np.float32)
    sin = np.sin(ang).astype(np.float32)
    reps = (copies, LANES // HEAD_DIM)
    return (jnp.asarray(np.tile(np.concatenate([cos, cos], axis=-1), reps)),
            jnp.asarray(np.tile(np.concatenate([-sin, sin], axis=-1), reps)))


def _block_diag_ones(n, blk):
    idx = np.arange(n) // blk
    return jnp.asarray((idx[:, None] == idx[None, :]).astype(np.float32), bf16)


def _layer_weights(w_in, w_gate_up, b_gate, q_norm_w, k_norm_w, w_out, w_ff1, w_ff2):
    splits = np.cumsum([GLA_QK, GLA_QK, GLA_WIDTH, GLA_WIDTH, GLA_GATE_RANK, SWA_Q, SWA_KV])
    gq, gk, gv, gr, glr, sq, sk, sv = jnp.split(w_in, [int(s) for s in splits], axis=1)
    glr = jnp.pad(glr, ((0, 0), (0, LANES - GLA_GATE_RANK)))
    win = jnp.concatenate([gq, gk, gv, gr, sq, sk, sv, glr], axis=1).astype(bf16)
    wgu = jnp.pad(w_gate_up, ((0, LANES - GLA_GATE_RANK), (0, 0))).astype(bf16)
    return dict(
        win=win, wgu=wgu, bg=b_gate.reshape(1, GLA_QK),
        qnw=jnp.tile(q_norm_w, N_Q_HEADS).reshape(1, SWA_Q),
        knw=jnp.tile(k_norm_w, N_KV_HEADS).reshape(1, SWA_KV),
        ffn_f32=(w_out, w_ff1, w_ff2))


def _project(x, mod, row0, pos, lw, n1, bdq, bdk, prompt):
    bsz, t, d = x.shape
    if prompt:
        nb, tb, tail, act, groups, cast = 1, TOKEN_BLOCK, WINDOW, bf16, PROMPT_INPROJ_GROUPS, lw["ffn_f32"]
    else:
        nb, tb, tail, act, groups, cast = SAMPLE_ROWS // t, t, SAMPLE_ROWS, f32, 1, ()
    cos_t, sin_t = _rope_tables(pos, nb)
    return _inproj(x, mod, row0, n1.reshape(1, 1, d), lw["win"], lw["wgu"], lw["bg"], lw["qnw"], lw["knw"],
                   bdq, bdk, cos_t, sin_t, nb, tb, tail, act, groups, cast, tail_t=prompt)


def _decoder_layers(xp, xs, mod, pos_p, pos_s, state, past_k, past_v, lw, n1, n2, gnw, sinks, bdq, bdk):
    bp, tp, d = xp.shape
    bs, ts, _ = xs.shape
    heads = (N_KV_HEADS, HEAD_DIM)
    gq, gk, gv, gr, la, sq, _, _, kk, vk = _project(xs, mod, 0, pos_s, lw, n1, bdq, bdk, False)
    go_s, state_s = _gla_sample(gq, gk, gv, la, gr, state.reshape(bs, GLA_QK, GLA_DV), gnw, bs, ts)
    so_s, kt, vt = _swa_sample(sq, kk, vk, past_k.transpose(0, 2, 3, 1), past_v.transpose(0, 2, 3, 1),
                               sinks, bs, ts)
    gq, gk, gv, gr, la, sq, k2, v2, kk, vk, *ffn_w = _project(xp, mod, bs, pos_p, lw, n1, bdq, bdk, True)
    yp, ys, s_t = _prompt_mix_ffn(xp, gq, gk, gv, la, gr, sq, k2, v2, gnw, sinks, mod, bs,
                                  n2.reshape(1, 1, d), *ffn_w, xs, go_s, so_s, 0)
    state_p = s_t.reshape(bp, GLA_HEADS, GLA_DK, GLA_DV)
    return (yp, ys, state_p,
            kk.reshape(bp, *heads, WINDOW).transpose(0, 3, 1, 2), vk.reshape(bp, *heads, WINDOW).transpose(0, 3, 1, 2),
            state_s.reshape(bs, GLA_HEADS, GLA_DK, GLA_DV),
            kt.transpose(0, 3, 1, 2), vt.transpose(0, 3, 1, 2))


def kernel(x_prompt, x_sample, state_gla, cache_swa_k, cache_swa_v, c_prompt, c_sample, w_ada, b_ada, norm1_w, norm2_w, w_in, w_gate_up, b_gate, gla_norm_w, q_norm_w, k_norm_w, sinks, w_out, w_ff1, w_ff2):
    depth = w_ada.shape[0]
    bp, tp, _ = x_prompt.shape
    bs, ts, _ = x_sample.shape
    pos_p = np.arange(tp)
    pos_s = PAST_LEN + np.arange(ts)
    bdq = _block_diag_ones(SWA_Q, HEAD_DIM)
    bdk = _block_diag_ones(SWA_KV, HEAD_DIM)
    yp, ys = x_prompt, x_sample
    outs = [[] for _ in range(6)]
    for l in range(depth):
        mod = _modulation(c_sample, c_prompt, w_ada[l], b_ada[l])
        lw = _layer_weights(w_in[l], w_gate_up[l], b_gate[l], q_norm_w[l], k_norm_w[l],
                            w_out[l], w_ff1[l], w_ff2[l])
        gnw = gla_norm_w[l].reshape(1, GLA_DV)
        yp, ys, *new = _decoder_layers(yp, ys, mod, pos_p, pos_s, state_gla[l], cache_swa_k[l],
                                       cache_swa_v[l], lw, norm1_w[l], norm2_w[l], gnw, sinks[l], bdq, bdk)
        for lst, val in zip(outs, new):
            lst.append(val)
    return (yp, ys) + tuple(jnp.stack(o) for o in outs)
```

```python
import functools

import jax
import jax.numpy as jnp
import numpy as np
from jax import lax
from jax.experimental import pallas as pl
from jax.experimental.pallas import tpu as pltpu

f32 = jnp.float32
bf16 = jnp.bfloat16

D_MODEL = 1024
GLA_HEADS = 4
GLA_DK = 64
GLA_DV = 128
GLA_QK = GLA_HEADS * GLA_DK
GLA_WIDTH = GLA_HEADS * GLA_DV
GLA_GATE_RANK = 16
GLA_TAU = 16.0
LOG2E = 1.4426950408889634
HEAD_DIM = 64
N_Q_HEADS = 8
N_KV_HEADS = 2
SWA_Q = N_Q_HEADS * HEAD_DIM
SWA_KV = N_KV_HEADS * HEAD_DIM
SWA_KV2 = 2 * SWA_KV
WINDOW = 128
ROPE_THETA = 10000.0
PAST_LEN = 8192
D_FF = 4 * D_MODEL
EPS = 1e-6
LANES = 128
GLA_CHUNK = 128
PROMPT_GLA_CHUNK = 128
TOKEN_BLOCK = 512
FUSED_TOKEN_BLOCK = 512
SAMPLE_ROWS = 256
PROMPT_INPROJ_GROUPS = 2
VMEM_LIMIT = 56 * 1024 * 1024

_SEG = {}
_off = 0
for _name, _w in (("gq", GLA_QK), ("gk", GLA_QK), ("gv", GLA_WIDTH), ("gr", GLA_WIDTH),
                  ("sq", SWA_Q), ("sk", SWA_KV), ("sv", SWA_KV), ("glr", LANES)):
    _SEG[_name] = (_off, _off + _w)
    _off += _w
IN_WIDTH_PADDED = _off


def _dot(a, b):
    return jnp.dot(a, b, preferred_element_type=f32)


def _dot_nt(a, b):
    return lax.dot_general(a, b, (((1,), (1,)), ((), ())), preferred_element_type=f32)


def _dot_tn(a, b):
    return lax.dot_general(a, b, (((0,), (0,)), ((), ())), preferred_element_type=f32)


def _sigmoid(x):
    return 1.0 / (1.0 + jnp.exp(-x))


MOD_W_SLABS = 4


def _mod_kernel(ca_ref, cb_ref, *refs):
    w_refs, b_ref, o_ref = refs[:-2], refs[-2], refs[-1]
    c = jnp.concatenate([ca_ref[...], cb_ref[...]], axis=0)
    s = (c * _sigmoid(c)).astype(bf16)
    kb = w_refs[0].shape[0]
    res = b_ref[...]
    for i, w_ref in enumerate(w_refs):
        res = res + _dot(s[:, i * kb:(i + 1) * kb], w_ref[...].astype(bf16))
    for r in range(res.shape[0]):
        o_ref[r] = res[r:r + 1, :]


def _modulation(c_a, c_b, w_ada, b_ada):
    m = c_a.shape[0] + c_b.shape[0]
    n = w_ada.shape[1]
    bn = 1536
    kb = D_MODEL // MOD_W_SLABS
    slabs = [pl.BlockSpec((kb, bn), lambda j, i=i: (i, j)) for i in range(MOD_W_SLABS)]
    return pl.pallas_call(
        _mod_kernel,
        grid=(n // bn,),
        in_specs=[pl.BlockSpec(c_a.shape, lambda j: (0, 0)), pl.BlockSpec(c_b.shape, lambda j: (0, 0))]
        + slabs + [pl.BlockSpec((1, bn), lambda j: (0, j))],
        out_specs=pl.BlockSpec((m, 1, bn), lambda j: (0, 0, j)),
        out_shape=jax.ShapeDtypeStruct((m, 1, n), f32),
        compiler_params=pltpu.CompilerParams(vmem_limit_bytes=VMEM_LIMIT),
        name="adaln_mod",
    )(c_a, c_b, *([w_ada] * MOD_W_SLABS), b_ada.reshape(1, n))


def _group_rms(x, bd_ref, w_ref):
    ssq = _dot((x * x).astype(bf16), bd_ref[...])
    return x * lax.rsqrt(ssq * (1.0 / HEAD_DIM) + EPS) * w_ref[...]


def _rope(x, cos, sin_signed, low_half):
    partner = jnp.where(low_half, pltpu.roll(x, LANES - 32, axis=1), pltpu.roll(x, 32, axis=1))
    return x * cos + partner * sin_signed


def _dup_heads(x, low_lanes):
    rolled = pltpu.roll(x, HEAD_DIM, axis=1)
    return jnp.where(low_lanes, x, rolled), jnp.where(low_lanes, rolled, x)


def _inproj_kernel(x_ref, sh_ref, sc_ref, n1_ref, win_ref, wgu_ref, bg_ref, qnw_ref, knw_ref,
                   bdq_ref, bdk_ref, cos_ref, sin_ref, *rest, groups, ncast, tail, tail_t):
    slabs, rest = rest[:ncast], rest[ncast:]
    gq_ref, gk_ref, gv_ref, gr_ref, la_ref, sq_ref, k2_ref, v2_ref, kk_ref, vk_ref = rest[:10]
    for src, dst in zip(slabs, rest[10:]):
        dst[...] = src[...].astype(dst.dtype)
    nb, t, d = x_ref.shape
    m = nb * t
    mg = m // groups
    lane = lax.broadcasted_iota(jnp.int32, (mg, LANES), 1)
    low_half = (lane & 32) == 0
    low_lanes = lane < HEAD_DIM
    for grp in range(groups):
        rows = slice(grp * mg, (grp + 1) * mg)
        if nb == 1:
            x = x_ref[:, rows, :]
            sc, sh = sc_ref[...], sh_ref[...]
        else:
            seqs = slice(grp * (nb // groups), (grp + 1) * (nb // groups))
            x = x_ref[seqs]
            sc, sh = sc_ref[seqs], sh_ref[seqs]
        ms = jnp.mean(x * x, axis=-1, keepdims=True)
        hn = x * lax.rsqrt(ms + EPS) * n1_ref[...]
        hn = hn * (1.0 + sc) + sh
        hb = hn.reshape(mg, d).astype(bf16)

        def seg(name):
            a, b = _SEG[name]
            return _dot(hb, win_ref[:, a:b])

        cos = cos_ref[rows, :]
        sin = sin_ref[rows, :]
        sq = _group_rms(seg("sq"), bdq_ref, qnw_ref)
        for c in range(SWA_Q // LANES):
            blk = _rope(sq[:, c * LANES:(c + 1) * LANES], cos, sin, low_half)
            sq_ref[rows, c * LANES:(c + 1) * LANES] = (blk * (LOG2E * HEAD_DIM ** -0.5)).astype(sq_ref.dtype)
        sk = _rope(_group_rms(seg("sk"), bdk_ref, knw_ref), cos, sin, low_half)
        sv = seg("sv")
        keep = (grp + 1) * mg - (m - tail)
        if keep > 0:
            keep = min(keep, mg)
            dst_rows = slice((grp + 1) * mg - keep - (m - tail), (grp + 1) * mg - (m - tail))
            if tail_t:
                assert keep == tail
                kk_ref[...] = sk[mg - keep:].T
                vk_ref[...] = sv[mg - keep:].T
            else:
                kk_ref[dst_rows, :] = sk[mg - keep:]
                vk_ref[dst_rows, :] = sv[mg - keep:]
        for src, dst in ((sk, k2_ref), (sv, v2_ref)):
            d0, d1 = _dup_heads(src, low_lanes)
            dst[rows, :LANES] = d0.astype(dst.dtype)
            dst[rows, LANES:] = d1.astype(dst.dtype)
        glr = seg("glr").astype(bf16)
        g = _dot(glr, wgu_ref[...]) + bg_ref[...]
        log_sig = jnp.minimum(g, 0.0) - jnp.log1p(jnp.exp(-jnp.abs(g)))
        la_ref[rows, :] = log_sig * (LOG2E / GLA_TAU)
        gq_ref[rows, :] = seg("gq") * (GLA_DK ** -0.5)
        gk_ref[rows, :] = seg("gk")
        gv_ref[rows, :] = seg("gv").astype(gv_ref.dtype)
        gr_ref[rows, :] = seg("gr").astype(gr_ref.dtype)


MOD_SHIFT1, MOD_SCALE1, MOD_GATE1, MOD_SHIFT2, MOD_SCALE2, MOD_GATE2 = range(6)


def _mod_spec(nb, row0, chunk, batch_block):
    return pl.BlockSpec((nb, 1, D_MODEL), lambda *ids: (row0 // nb + batch_block(*ids), 0, chunk))


def _inproj(x, mod, row0, n1, win, wgu, bg, qnw, knw, bdq, bdk, cos_t, sin_t, nb, tb, tail, act, groups,
            to_bf16=(), tail_t=False):
    bsz, t, d = x.shape
    m = nb * tb
    nt = t // tb
    grid = (bsz // nb, nt)
    tok = bsz * t
    nsteps = grid[0] * grid[1]
    slab_specs = [pl.BlockSpec((w.shape[0] // nsteps, w.shape[1]), lambda i, j: (i * nt + j, 0))
                  for w in to_bf16]

    def full(shape):
        return pl.BlockSpec(shape, lambda i, j: (0,) * len(shape))

    def out(width):
        return pl.BlockSpec((m, width), lambda i, j: (i * nt + j, 0))

    tail_blk = (SWA_KV, tail) if tail_t else (tail, SWA_KV)
    tail_spec = pl.BlockSpec(tail_blk, lambda i, j: (i, 0))
    tail_shape = jax.ShapeDtypeStruct(((bsz // nb) * tail_blk[0], tail_blk[1]), f32)
    outs = ((GLA_QK, f32), (GLA_QK, f32), (GLA_WIDTH, act), (GLA_WIDTH, act), (GLA_QK, f32),
            (SWA_Q, act), (SWA_KV2, bf16), (SWA_KV2, bf16))
    return pl.pallas_call(
        functools.partial(_inproj_kernel, groups=groups, ncast=len(to_bf16), tail=tail, tail_t=tail_t),
        grid=grid,
        in_specs=[pl.BlockSpec((nb, tb, d), lambda i, j: (i, j, 0)),
                  _mod_spec(nb, row0, MOD_SHIFT1, lambda i, j: i),
                  _mod_spec(nb, row0, MOD_SCALE1, lambda i, j: i),
                  full((1, 1, d)),
                  full((d, IN_WIDTH_PADDED)),
                  full((LANES, GLA_QK)),
                  full((1, GLA_QK)),
                  full((1, SWA_Q)),
                  full((1, SWA_KV)),
                  full((SWA_Q, SWA_Q)),
                  full((SWA_KV, SWA_KV)),
                  pl.BlockSpec((m, LANES), lambda i, j: (j, 0)),
                  pl.BlockSpec((m, LANES), lambda i, j: (j, 0))] + slab_specs,
        out_specs=[out(w) for w, _ in outs] + [tail_spec, tail_spec] + slab_specs,
        out_shape=[jax.ShapeDtypeStruct((tok, w), dt) for w, dt in outs]
        + [tail_shape] * 2
        + [jax.ShapeDtypeStruct(w.shape, bf16) for w in to_bf16],
        compiler_params=pltpu.CompilerParams(
            dimension_semantics=("parallel", "arbitrary"), vmem_limit_bytes=VMEM_LIMIT),
        name="inproj",
    )(x, mod, mod, n1, win, wgu, bg, qnw, knw, bdq, bdk, cos_t, sin_t, *to_bf16)


def _gla_constants(chunk, seq):
    t = np.arange(chunk)
    levels = []
    m = 1
    while m < seq:
        levels.append(m)
        m *= 2
    masks = [np.eye(chunk, dtype=bool)]
    for m in levels:
        upper = (t % (2 * m) >= m)[:, None]
        lower = (t % (2 * m) < m)[None, :]
        same = (t[:, None] // (2 * m)) == (t[None, :] // (2 * m))
        masks.append(same & upper & lower)
    tri = t[None, :] <= t[:, None]
    tiled = np.tile(np.stack(masks).astype(np.float32), (1, 1, GLA_HEADS))
    return tuple(levels), jnp.asarray(tri.astype(np.float32), bf16), jnp.asarray(tiled)


def _stack_heads(xb, lane_head, axis):
    zero = jnp.zeros_like(xb)
    return jnp.concatenate([jnp.where(lane_head == h, xb, zero) for h in range(GLA_HEADS)], axis=axis)


def _block_sums(la, bcum, m, row, rolls):
    c, n = la.shape
    if m == 1:
        return la, None
    if m < 8:
        def rolled(shift):
            if shift not in rolls:
                rolls[shift] = pltpu.roll(la, shift % c, axis=0)
            return rolls[shift]
        pos = row & (m - 1)
        pre = la
        suf = None
        for j in range(1, m):
            pre = pre + jnp.where(pos >= j, rolled(j), 0.0)
            term = jnp.where(pos < m - j, rolled(-j), 0.0)
            suf = term if suf is None else suf + term
        return pre, suf
    before, last = [], []
    for i in range(c // m):
        before.append(jnp.zeros((m, n), f32) if i == 0
                      else jnp.broadcast_to(bcum[i * m - 1:i * m], (m, n)))
        last.append(jnp.broadcast_to(bcum[(i + 1) * m - 1:(i + 1) * m], (m, n)))
    if len(before) == 1:
        return bcum - before[0], last[0] - bcum
    return bcum - jnp.concatenate(before, axis=0), jnp.concatenate(last, axis=0) - bcum


def _gla_scores(q_ref, k_ref, v_ref, la_ref, tri_ref, masks_ref, levels, seq):
    c = q_ref.shape[0]
    la = la_ref[...]
    hi = la.astype(bf16)
    lo = (la - hi.astype(f32)).astype(bf16)
    tri = tri_ref[...]
    bcum = _dot(tri, hi) + _dot(tri, lo)
    q = q_ref[...]
    k = k_ref[...]
    vb = v_ref[...].astype(bf16)
    row = lax.broadcasted_iota(jnp.int32, (c, GLA_QK), 0)
    row_head_t = lax.broadcasted_iota(jnp.int32, (GLA_QK, c), 0) >> 6
    rolls = {}
    attn = None
    for lvl, m in enumerate((0,) + levels):
        if m == 0:
            qt, kt = q, k
        else:
            pre, suf = _block_sums(la, bcum, m, row, rolls)
            qt = q * jnp.exp2(pre)
            kt = k if suf is None else k * jnp.exp2(suf)
        kt_t = kt.T.astype(bf16)
        r = _dot(qt.astype(bf16), _stack_heads(kt_t, row_head_t, 1))
        rm = r * masks_ref[lvl]
        attn = rm if attn is None else attn + rm
    a = attn.astype(bf16)
    pre, suf = _block_sums(la, bcum, seq, row, rolls)
    qf = (q * jnp.exp2(pre)).astype(bf16)
    kf = (k * jnp.exp2(suf)).astype(bf16)
    return a, qf, kf, vb, bcum, hi, lo


def _gla_values(a, vb):
    v_head = lax.broadcasted_iota(jnp.int32, vb.shape, 1) >> 7
    return _dot(a, _stack_heads(vb, v_head, 0))


def _gla_finish(o, r_ref, gnw, go_ref):
    for h in range(GLA_HEADS):
        sl = slice(h * GLA_DV, (h + 1) * GLA_DV)
        oh = o[:, sl]
        r = r_ref[:, sl].astype(f32)
        ms = jnp.mean(oh * oh, axis=-1, keepdims=True)
        go_ref[:, sl] = (oh * lax.rsqrt(ms + EPS) * gnw * (r * _sigmoid(r))).astype(go_ref.dtype)


def _gla_prompt_block(st, q_ref, k_ref, v_ref, la_ref, r_ref, tri_ref, masks_ref, gnw, go_ref, levels,
                      between=lambda: None):
    c = PROMPT_GLA_CHUNK
    nchunk = q_ref.shape[0] // c
    lane_head = lax.broadcasted_iota(jnp.int32, (GLA_DV, GLA_QK), 1) >> 6
    row_head = lax.broadcasted_iota(jnp.int32, (GLA_QK, GLA_DV), 0) >> 6
    chunks = []
    for ci in range(nchunk):
        rows = pl.ds(ci * c, c)
        chunks.append(_gla_scores(q_ref.at[rows], k_ref.at[rows], v_ref.at[rows], la_ref.at[rows],
                                  tri_ref, masks_ref, levels, c))
        between()
    intra, upds, decays = [], [], []
    for a, _, kf, vb, bcum, _, _ in chunks:
        intra.append(_gla_values(a, vb))
        full = _dot_tn(vb, kf)
        upd = None
        for h in range(GLA_HEADS):
            term = jnp.where(lane_head == h, full[h * GLA_DV:(h + 1) * GLA_DV], 0.0)
            upd = term if upd is None else upd + term
        upds.append(upd)
        decays.append(jnp.exp2(bcum[c - 1:c]))
    between()
    for ci in range(nchunk):
        rows = pl.ds(ci * c, c)
        sbd = _stack_heads(st.T.astype(bf16), row_head, 1)
        o = intra[ci] + _dot(chunks[ci][1], sbd)
        _gla_finish(o, r_ref.at[rows], gnw, go_ref.at[rows])
        st = decays[ci] * st + upds[ci]
    return st


def _gla_sample_kernel(q_ref, k_ref, v_ref, la_ref, r_ref, s0_ref, tri_ref, masks_ref, gnw_ref,
                       go_ref, sout_ref, *, levels, seq):
    c = q_ref.shape[0]
    a, qf, kf, vb, _, hi, lo = _gla_scores(q_ref, k_ref, v_ref, la_ref, tri_ref, masks_ref, levels, seq)
    o_intra = _gla_values(a, vb)
    row_head = lax.broadcasted_iota(jnp.int32, (GLA_QK, GLA_DV), 0) >> 6
    ones = jnp.ones((seq, GLA_DV), bf16)
    inter = []
    for b in range(c // seq):
        rows = slice(b * seq, (b + 1) * seq)
        s_old = s0_ref[b]
        sbd = _stack_heads(s_old.astype(bf16), row_head, 1)
        inter.append(_dot(qf[rows], sbd))
        p = _dot_tn(kf[rows], vb[rows])
        upd = None
        for h in range(GLA_HEADS):
            term = jnp.where(row_head == h, p[:, h * GLA_DV:(h + 1) * GLA_DV], 0.0)
            upd = term if upd is None else upd + term
        total = _dot_tn(hi[rows], ones) + _dot_tn(lo[rows], ones)
        sout_ref[b] = jnp.exp2(total) * s_old + upd
    o = o_intra + jnp.concatenate(inter, axis=0)
    _gla_finish(o, r_ref, gnw_ref[...], go_ref)


def _gla_sample(gq, gk, gv, la, gr, state, gnw, bsz, t):
    c = GLA_CHUNK
    nseq = c // t
    levels, tri, masks = _gla_constants(c, t)

    def tok(width):
        return pl.BlockSpec((c, width), lambda i: (i, 0))

    return pl.pallas_call(
        functools.partial(_gla_sample_kernel, levels=levels, seq=t),
        grid=(bsz // nseq,),
        in_specs=[tok(GLA_QK), tok(GLA_QK), tok(GLA_WIDTH), tok(GLA_QK), tok(GLA_WIDTH),
                  pl.BlockSpec((nseq, GLA_QK, GLA_DV), lambda i: (i, 0, 0)),
                  pl.BlockSpec(tri.shape, lambda i: (0, 0)),
                  pl.BlockSpec(masks.shape, lambda i: (0, 0, 0)),
                  pl.BlockSpec((1, GLA_DV), lambda i: (0, 0))],
        out_specs=[tok(GLA_WIDTH),
                   pl.BlockSpec((nseq, GLA_QK, GLA_DV), lambda i: (i, 0, 0))],
        out_shape=[jax.ShapeDtypeStruct((bsz * t, GLA_WIDTH), f32),
                   jax.ShapeDtypeStruct((bsz, GLA_QK, GLA_DV), f32)],
        compiler_params=pltpu.CompilerParams(
            dimension_semantics=("parallel",), vmem_limit_bytes=VMEM_LIMIT),
        name="gla_sample",
    )(gq, gk, gv, la, gr, state, tri, masks, gnw)


def _sink_attention(q_ref, rows, kdups, vdups, mask, sink_ref, o_ref, between=lambda: None):
    n = len(kdups)
    heads_per_group = N_Q_HEADS // N_KV_HEADS
    low_q = lax.broadcasted_iota(jnp.int32, (rows, LANES), 1) < HEAD_DIM
    scores = []
    for i in range(n):
        per_group = []
        for g in range(N_KV_HEADS):
            stack = []
            for p in range(heads_per_group // 2):
                pair = g * (heads_per_group // 2) + p
                qp = q_ref[i * rows:(i + 1) * rows, pair * LANES:(pair + 1) * LANES].astype(bf16)
                zero = jnp.zeros_like(qp)
                stack.append(jnp.where(low_q, qp, zero))
                stack.append(jnp.where(low_q, zero, qp))
            per_group.append(_dot_nt(jnp.concatenate(stack, axis=0), kdups[i][g]))
        scores.append(per_group)
    between()
    probs, inv = [], []
    for head in range(N_Q_HEADS):
        g, hh = divmod(head, heads_per_group)
        parts = [scores[i][g][hh * rows:(hh + 1) * rows] for i in range(n)]
        s = parts[0] if n == 1 else jnp.concatenate(parts, axis=0)
        sink = sink_ref[head] * LOG2E
        s = jnp.where(mask, s, -1e30)
        mx = jnp.maximum(jnp.max(s, axis=-1, keepdims=True), sink)
        p = jnp.exp2(s - mx)
        inv.append(1.0 / (jnp.sum(p, axis=-1, keepdims=True) + jnp.exp2(sink - mx)))
        probs.append(p.astype(bf16))
    between()
    outs = []
    for i in range(n):
        per_group = []
        for g in range(N_KV_HEADS):
            p_i = jnp.concatenate([probs[g * heads_per_group + hh][i * rows:(i + 1) * rows]
                                   for hh in range(heads_per_group)], axis=0)
            per_group.append(_dot(p_i, vdups[i][g]))
        outs.append(per_group)
    low_all = lax.broadcasted_iota(jnp.int32, (n * rows, LANES), 1) < HEAD_DIM
    for pair in range(N_Q_HEADS // 2):
        halves = []
        for head in (2 * pair, 2 * pair + 1):
            g, hh = divmod(head, heads_per_group)
            parts = [outs[i][g][hh * rows:(hh + 1) * rows] for i in range(n)]
            o = parts[0] if n == 1 else jnp.concatenate(parts, axis=0)
            halves.append(o * inv[head])
        o_ref[:, pair * LANES:(pair + 1) * LANES] = jnp.where(low_all, halves[0], halves[1]).astype(o_ref.dtype)


def _swa_prompt_blocks(q_ref, k_before, k2_ref, v_before, v2_ref, first, sink_ref, o_ref,
                       between=lambda: None):
    w = WINDOW
    nblk = q_ref.shape[0] // w

    def dup(before, ref):
        blocks = [before] + [ref[i * w:(i + 1) * w, :] for i in range(nblk)]
        return [[jnp.concatenate([blocks[i][:, g * LANES:(g + 1) * LANES],
                                  blocks[i + 1][:, g * LANES:(g + 1) * LANES]], axis=0)
                 for g in range(N_KV_HEADS)] for i in range(nblk)]

    row = lax.broadcasted_iota(jnp.int32, (nblk * w, 2 * w), 0)
    tk = lax.broadcasted_iota(jnp.int32, (nblk * w, 2 * w), 1)
    rel = tk - (row & (w - 1))
    first_key = jnp.where(row < w, jnp.where(first, w, 0), 0)
    mask = (rel > 0) & (rel <= w) & (tk >= first_key)
    _sink_attention(q_ref, w, dup(k_before, k2_ref), dup(v_before, v2_ref), mask, sink_ref, o_ref, between)


def _swa_sample_kernel(sink_ref, q_ref, kn_ref, vn_ref, pk_ref, pv_ref, o_ref, ko_ref, vo_ref, *, seq):
    nseq, _, _, w = pk_ref.shape
    rows = nseq * seq
    heads_per_group = N_Q_HEADS // N_KV_HEADS
    lane = lax.broadcasted_iota(jnp.int32, (rows, LANES), 1)
    r_id = lax.broadcasted_iota(jnp.int32, (rows, LANES), 0)
    low = lane < HEAD_DIM
    pos = r_id & (seq - 1)
    mask_old = lane > pos
    seq_shift = seq.bit_length() - 1
    mask_new = ((r_id >> seq_shift) == (lane >> seq_shift)) & ((lane & (seq - 1)) <= pos)
    kn = kn_ref[...]
    vn = vn_ref[...]
    kn_dup = [d.astype(bf16) for d in _dup_heads(kn, low)]
    vn_dup = [d.astype(bf16) for d in _dup_heads(vn, low)]

    pad = jnp.zeros((w - seq, LANES), f32)
    tail_lanes = lax.broadcasted_iota(jnp.int32, (HEAD_DIM, w), 1) >= w - seq
    kt_dup, vt_dup = [], []
    for b in range(nseq):
        tok = slice(b * seq, (b + 1) * seq)
        per_k, per_v = [], []
        for new, old_ref, out_ref, per in ((kn, pk_ref, ko_ref, per_k), (vn, pv_ref, vo_ref, per_v)):
            new_t = jnp.concatenate([pad, new[tok]], axis=0).T
            for g in range(N_KV_HEADS):
                old = old_ref[b, g]
                out_ref[b, g] = jnp.where(tail_lanes, new_t[g * HEAD_DIM:(g + 1) * HEAD_DIM],
                                          pltpu.roll(old, w - seq, axis=1))
                ob = old.astype(bf16)
                per.append(jnp.concatenate([ob, ob], axis=0))
        kt_dup.append(per_k)
        vt_dup.append(per_v)

    lhs = []
    for g in range(N_KV_HEADS):
        stack = []
        for p in range(heads_per_group // 2):
            pair = g * (heads_per_group // 2) + p
            qp = q_ref[:, pair * LANES:(pair + 1) * LANES].astype(bf16)
            zero = jnp.zeros_like(qp)
            stack.append(jnp.where(low, qp, zero))
            stack.append(jnp.where(low, zero, qp))
        lhs.append(jnp.concatenate(stack, axis=0))
    s_new = [_dot_nt(lhs[g], kn_dup[g]) for g in range(N_KV_HEADS)]
    s_old = []
    for b in range(nseq):
        per = []
        for g in range(N_KV_HEADS):
            qb = jnp.concatenate([lhs[g][hh * rows + b * seq:hh * rows + (b + 1) * seq]
                                  for hh in range(heads_per_group)], axis=0)
            per.append(_dot(qb, kt_dup[b][g]))
        s_old.append(per)

    p_old, p_new, inv = [], [], []
    for head in range(N_Q_HEADS):
        g, hh = divmod(head, heads_per_group)
        so = jnp.concatenate([s_old[b][g][hh * seq:(hh + 1) * seq] for b in range(nseq)], axis=0)
        so = jnp.where(mask_old, so, -1e30)
        sn = jnp.where(mask_new, s_new[g][hh * rows:(hh + 1) * rows], -1e30)
        sink = sink_ref[head] * LOG2E
        mx = jnp.maximum(jnp.maximum(jnp.max(so, axis=-1, keepdims=True),
                                     jnp.max(sn, axis=-1, keepdims=True)), sink)
        po = jnp.exp2(so - mx)
        pn = jnp.exp2(sn - mx)
        inv.append(1.0 / (jnp.sum(po, axis=-1, keepdims=True) + jnp.sum(pn, axis=-1, keepdims=True)
                          + jnp.exp2(sink - mx)))
        p_old.append(po.astype(bf16))
        p_new.append(pn.astype(bf16))

    o_new = [_dot(jnp.concatenate([p_new[g * heads_per_group + hh] for hh in range(heads_per_group)], axis=0),
                  vn_dup[g]) for g in range(N_KV_HEADS)]
    o_old = []
    for b in range(nseq):
        per = []
        for g in range(N_KV_HEADS):
            pb = jnp.concatenate([p_old[g * heads_per_group + hh][b * seq:(b + 1) * seq]
                                  for hh in range(heads_per_group)], axis=0)
            per.append(_dot_nt(pb, vt_dup[b][g]))
        o_old.append(per)

    for pair in range(N_Q_HEADS // 2):
        halves = []
        for head in (2 * pair, 2 * pair + 1):
            g, hh = divmod(head, heads_per_group)
            old = jnp.concatenate([o_old[b][g][hh * seq:(hh + 1) * seq] for b in range(nseq)], axis=0)
            halves.append((old + o_new[g][hh * rows:(hh + 1) * rows]) * inv[head])
        o_ref[:, pair * LANES:(pair + 1) * LANES] = jnp.where(low, halves[0], halves[1])


def _swa_sample(sq, sk, sv, past_kt, past_vt, sinks, bsz, t):
    nseq = SAMPLE_ROWS // 2 // t
    assert nseq * t == LANES
    tok = lambda width: pl.BlockSpec((nseq * t, width), lambda i, s: (i, 0))
    cache = pl.BlockSpec((nseq,) + past_kt.shape[1:], lambda i, s: (i, 0, 0, 0))
    return pl.pallas_call(
        functools.partial(_swa_sample_kernel, seq=t),
        grid_spec=pltpu.PrefetchScalarGridSpec(
            num_scalar_prefetch=1,
            grid=(bsz // nseq,),
            in_specs=[tok(SWA_Q), tok(SWA_KV), tok(SWA_KV), cache, cache],
            out_specs=[tok(SWA_Q), cache, cache]),
        out_shape=[jax.ShapeDtypeStruct((bsz * t, SWA_Q), f32),
                   jax.ShapeDtypeStruct(past_kt.shape, f32),
                   jax.ShapeDtypeStruct(past_vt.shape, f32)],
        compiler_params=pltpu.CompilerParams(
            dimension_semantics=("parallel",), vmem_limit_bytes=VMEM_LIMIT),
        name="swa_sample",
    )(sinks, sq, sk, sv, past_kt, past_vt)


FFN_CHUNK = 512


def _out_proj_norm(streams, n2, wout_ref):
    mix = [s[1] for s in streams]
    mix = mix[0] if len(mix) == 1 else jnp.concatenate(mix, axis=0)
    mixed = _dot(mix[:, :GLA_WIDTH], wout_ref[:GLA_WIDTH, :]) + _dot(mix[:, GLA_WIDTH:], wout_ref[GLA_WIDTH:, :])
    hs, hbs, r0 = [], [], 0
    for x, _, g1, sh2, sc2 in streams:
        nb, t, d = x.shape
        h = x + g1 * mixed[r0:r0 + nb * t].reshape(nb, t, d)
        ms = jnp.mean(h * h, axis=-1, keepdims=True)
        hn = h * lax.rsqrt(ms + EPS) * n2
        hn = hn * (1.0 + sc2) + sh2
        hs.append(h)
        hbs.append(hn.reshape(nb * t, d).astype(bf16))
        r0 += nb * t
    return hs, (hbs[0] if len(hbs) == 1 else jnp.concatenate(hbs, axis=0))


def _ffn_piece(hb, w1_ref, w2_ref, c, width):
    cols = slice(c * width, (c + 1) * width)
    a = jnp.maximum(_dot(hb, w1_ref[:, cols]), 0.0)
    return _dot((a * a).astype(bf16), w2_ref[cols, :])


def _resident(shape):
    return pl.BlockSpec(shape, lambda *_: (0,) * len(shape), pipeline_mode=pl.Buffered(1))


def _prompt_kernel(sink_ref,
                   q_ref, k_ref, v_ref, la_ref, r_ref, sq_ref, k2_ref, v2_ref, k2p_ref, v2p_ref,
                   tri_ref, masks_ref, gnw_ref,
                   x_ref, g1_ref, sh2_ref, sc2_ref, g2_ref,
                   xs_ref, gos_ref, sos_ref, g1s_ref, sh2s_ref, sc2s_ref, g2s_ref,
                   n2_ref, wout_ref, w1_ref, w2_ref,
                   y_ref, ys_ref, sout_ref, state_ref, mix_ref, *, levels, nt, nsteps):
    s = pl.program_id(0)
    slot = s % 2

    @pl.when(s == 0)
    def _():
        mix_ref[1] = jnp.zeros(mix_ref.shape[1:], mix_ref.dtype)
        state_ref[...] = jnp.zeros_like(state_ref)

    j = jnp.minimum(s, nsteps - 1) % nt
    st_in = state_ref[...]
    st = jnp.where(j == 0, 0.0, st_in)
    gnw = gnw_ref[...]
    out = mix_ref.at[slot]
    mix_s = jnp.concatenate([gos_ref[...], sos_ref[...]], axis=1).astype(bf16)
    (h, h_s), hb = _out_proj_norm(
        [(x_ref[...], mix_ref[1 - slot], g1_ref[...], sh2_ref[...], sc2_ref[...]),
         (xs_ref[...], mix_s, g1s_ref[...], sh2s_ref[...], sc2s_ref[...])], n2_ref[...], wout_ref)
    rows_p = h.shape[0] * h.shape[1]
    todo = list(range(D_FF // FFN_CHUNK))
    ff = []

    def ffn_piece():
        if todo:
            part = _ffn_piece(hb, w1_ref, w2_ref, todo.pop(0), FFN_CHUNK)
            ff[:] = [part if not ff else ff[0] + part]

    st = _gla_prompt_block(st, q_ref, k_ref, v_ref, la_ref, r_ref, tri_ref, masks_ref, gnw,
                           out.at[:, pl.ds(0, GLA_WIDTH)], levels, ffn_piece)
    _swa_prompt_blocks(sq_ref, k2p_ref[...], k2_ref, v2p_ref[...], v2_ref, j == 0, sink_ref,
                       out.at[:, pl.ds(GLA_WIDTH, SWA_Q)], ffn_piece)
    while todo:
        ffn_piece()
    st = jnp.where(s < nsteps, st, st_in)
    state_ref[...] = st
    sout_ref[0] = st.T
    y_ref[...] = h + g2_ref[...] * ff[0][:rows_p].reshape(h.shape)
    ys_ref[...] = h_s + g2s_ref[...] * ff[0][rows_p:].reshape(h_s.shape)


def _prompt_mix_ffn(x, gq, gk, gv, la, gr, sq, k2, v2, gnw, sinks, mod, row0, n2, wout, w1, w2,
                    xs, gos, sos, row0_s):
    bsz, t, d = x.shape
    tb = FUSED_TOKEN_BLOCK
    nt = t // tb
    nsteps = bsz * nt
    levels, tri, masks = _gla_constants(PROMPT_GLA_CHUNK, PROMPT_GLA_CHUNK)
    per_blk = tb // WINDOW
    bs, ts, _ = xs.shape
    nbs = bs // nsteps
    assert nbs * nsteps == bs and (nbs * ts) % 16 == 0

    def mix_blk(s):
        return jnp.minimum(s, nsteps - 1)

    def ffn_blk(s):
        return jnp.maximum(s - 1, 0)

    def tok(width):
        return pl.BlockSpec((tb, width), lambda s, _: (mix_blk(s), 0))

    prev = pl.BlockSpec((WINDOW, SWA_KV2), lambda s, _: (jnp.maximum(mix_blk(s) * per_blk - 1, 0), 0))
    xspec = pl.BlockSpec((1, tb, d), lambda s, _: (ffn_blk(s) // nt, ffn_blk(s) % nt, 0))
    ffn_chunks = (MOD_GATE1, MOD_SHIFT2, MOD_SCALE2, MOD_GATE2)
    mods = [_mod_spec(1, row0, chunk, lambda s, _: ffn_blk(s) // nt) for chunk in ffn_chunks]
    mods_s = [_mod_spec(nbs, row0_s, chunk, lambda s, _: ffn_blk(s)) for chunk in ffn_chunks]
    xs_spec = pl.BlockSpec((nbs, ts, d), lambda s, _: (ffn_blk(s), 0, 0))
    mix_s_spec = pl.BlockSpec((nbs * ts, GLA_WIDTH), lambda s, _: (ffn_blk(s), 0))
    return pl.pallas_call(
        functools.partial(_prompt_kernel, levels=levels, nt=nt, nsteps=nsteps),
        grid_spec=pltpu.PrefetchScalarGridSpec(
            num_scalar_prefetch=1,
            grid=(nsteps + 1,),
            in_specs=[tok(GLA_QK), tok(GLA_QK), tok(GLA_WIDTH), tok(GLA_QK), tok(GLA_WIDTH),
                      tok(SWA_Q), tok(SWA_KV2), tok(SWA_KV2), prev, prev,
                      _resident(tri.shape), _resident(masks.shape), _resident((1, GLA_DV)),
                      xspec, *mods, xs_spec, mix_s_spec, mix_s_spec, *mods_s, _resident((1, 1, d)),
                      _resident((d, d)), _resident((d, D_FF)), _resident((D_FF, d))],
            out_specs=[xspec, xs_spec,
                       pl.BlockSpec((1, GLA_QK, GLA_DV), lambda s, _: (mix_blk(s) // nt, 0, 0))],
            scratch_shapes=[pltpu.VMEM((GLA_DV, GLA_QK), f32),
                            pltpu.VMEM((2, tb, GLA_WIDTH + SWA_Q), bf16)]),
        out_shape=[jax.ShapeDtypeStruct((bsz, t, d), f32),
                   jax.ShapeDtypeStruct(xs.shape, f32),
                   jax.ShapeDtypeStruct((bsz, GLA_QK, GLA_DV), f32)],
        compiler_params=pltpu.CompilerParams(
            dimension_semantics=("arbitrary",), vmem_limit_bytes=VMEM_LIMIT),
        name="prompt_mix_ffn",
    )(sinks, gq, gk, gv, la, gr, sq, k2, v2, k2, v2, tri, masks, gnw,
      x, mod, mod, mod, mod, xs, gos, sos, mod, mod, mod, mod, n2, wout, w1, w2)


def _rope_tables(pos, copies):
    half = HEAD_DIM // 2
    inv = np.power(np.float32(ROPE_THETA), -np.arange(half, dtype=np.float32) * np.float32(2.0 / HEAD_DIM))
    ang = np.asarray(pos, np.float32)[:, None] * inv[None, :].astype(np.float32)
    cos = np.cos(ang).astype(np.float32)
    sin = np.sin(ang).astype(np.float32)
    reps = (copies, LANES // HEAD_DIM)
    return (jnp.asarray(np.tile(np.concatenate([cos, cos], axis=-1), reps)),
            jnp.asarray(np.tile(np.concatenate([-sin, sin], axis=-1), reps)))


def _block_diag_ones(n, blk):
    idx = np.arange(n) // blk
    return jnp.asarray((idx[:, None] == idx[None, :]).astype(np.float32), bf16)


def _layer_weights(w_in, w_gate_up, b_gate, q_norm_w, k_norm_w, w_out, w_ff1, w_ff2):
    splits = np.cumsum([GLA_QK, GLA_QK, GLA_WIDTH, GLA_WIDTH, GLA_GATE_RANK, SWA_Q, SWA_KV])
    gq, gk, gv, gr, glr, sq, sk, sv = jnp.split(w_in, [int(s) for s in splits], axis=1)
    glr = jnp.pad(glr, ((0, 0), (0, LANES - GLA_GATE_RANK)))
    win = jnp.concatenate([gq, gk, gv, gr, sq, sk, sv, glr], axis=1).astype(bf16)
    wgu = jnp.pad(w_gate_up, ((0, LANES - GLA_GATE_RANK), (0, 0))).astype(bf16)
    return dict(
        win=win, wgu=wgu, bg=b_gate.reshape(1, GLA_QK),
        qnw=jnp.tile(q_norm_w, N_Q_HEADS).reshape(1, SWA_Q),
        knw=jnp.tile(k_norm_w, N_KV_HEADS).reshape(1, SWA_KV),
        ffn_f32=(w_out, w_ff1, w_ff2))


def _project(x, mod, row0, pos, lw, n1, bdq, bdk, prompt):
    bsz, t, d = x.shape
    if prompt:
        nb, tb, tail, act, groups, cast = 1, TOKEN_BLOCK, WINDOW, bf16, PROMPT_INPROJ_GROUPS, lw["ffn_f32"]
    else:
        nb, tb, tail, act, groups, cast = SAMPLE_ROWS // t, t, SAMPLE_ROWS, f32, 1, ()
    cos_t, sin_t = _rope_tables(pos, nb)
    return _inproj(x, mod, row0, n1.reshape(1, 1, d), lw["win"], lw["wgu"], lw["bg"], lw["qnw"], lw["knw"],
                   bdq, bdk, cos_t, sin_t, nb, tb, tail, act, groups, cast, tail_t=prompt)


def _decoder_layers(xp, xs, mod, pos_p, pos_s, state, past_k, past_v, lw, n1, n2, gnw, sinks, bdq, bdk):
    bp, tp, d = xp.shape
    bs, ts, _ = xs.shape
    heads = (N_KV_HEADS, HEAD_DIM)
    gq, gk, gv, gr, la, sq, _, _, kk, vk = _project(xs, mod, 0, pos_s, lw, n1, bdq, bdk, False)
    go_s, state_s = _gla_sample(gq, gk, gv, la, gr, state.reshape(bs, GLA_QK, GLA_DV), gnw, bs, ts)
    so_s, kt, vt = _swa_sample(sq, kk, vk, past_k.transpose(0, 2, 3, 1), past_v.transpose(0, 2, 3, 1),
                               sinks, bs, ts)
    gq, gk, gv, gr, la, sq, k2, v2, kk, vk, *ffn_w = _project(xp, mod, bs, pos_p, lw, n1, bdq, bdk, True)
    yp, ys, s_t = _prompt_mix_ffn(xp, gq, gk, gv, la, gr, sq, k2, v2, gnw, sinks, mod, bs,
                                  n2.reshape(1, 1, d), *ffn_w, xs, go_s, so_s, 0)
    state_p = s_t.reshape(bp, GLA_HEADS, GLA_DK, GLA_DV)
    return (yp, ys, state_p,
            kk.reshape(bp, *heads, WINDOW).transpose(0, 3, 1, 2), vk.reshape(bp, *heads, WINDOW).transpose(0, 3, 1, 2),
            state_s.reshape(bs, GLA_HEADS, GLA_DK, GLA_DV),
            kt.transpose(0, 3, 1, 2), vt.transpose(0, 3, 1, 2))


def kernel(x_prompt, x_sample, state_gla, cache_swa_k, cache_swa_v, c_prompt, c_sample, w_ada, b_ada, norm1_w, norm2_w, w_in, w_gate_up, b_gate, gla_norm_w, q_norm_w, k_norm_w, sinks, w_out, w_ff1, w_ff2):
    depth = w_ada.shape[0]
    bp, tp, _ = x_prompt.shape
    bs, ts, _ = x_sample.shape
    pos_p = np.arange(tp)
    pos_s = PAST_LEN + np.arange(ts)
    bdq = _block_diag_ones(SWA_Q, HEAD_DIM)
    bdk = _block_diag_ones(SWA_KV, HEAD_DIM)
    yp, ys = x_prompt, x_sample
    outs = [[] for _ in range(6)]
    for l in range(depth):
        mod = _modulation(c_sample, c_prompt, w_ada[l], b_ada[l])
        lw = _layer_weights(w_in[l], w_gate_up[l], b_gate[l], q_norm_w[l], k_norm_w[l],
                            w_out[l], w_ff1[l], w_ff2[l])
        gnw = gla_norm_w[l].reshape(1, GLA_DV)
        yp, ys, *new = _decoder_layers(yp, ys, mod, pos_p, pos_s, state_gla[l], cache_swa_k[l],
                                       cache_swa_v[l], lw, norm1_w[l], norm2_w[l], gnw, sinks[l], bdq, bdk)
        for lst, val in zip(outs, new):
            lst.append(val)
    return (yp, ys) + tuple(jnp.stack(o) for o in outs)
```

```python
import functools

import jax
import jax.numpy as jnp
import numpy as np
from jax import lax
from jax.experimental import pallas as pl
from jax.experimental.pallas import tpu as pltpu

f32 = jnp.float32
bf16 = jnp.bfloat16

D_MODEL = 1024
GLA_HEADS = 4
GLA_DK = 64
GLA_DV = 128
GLA_QK = GLA_HEADS * GLA_DK
GLA_WIDTH = GLA_HEADS * GLA_DV
GLA_GATE_RANK = 16
GLA_TAU = 16.0
LOG2E = 1.4426950408889634
HEAD_DIM = 64
N_Q_HEADS = 8
N_KV_HEADS = 2
SWA_Q = N_Q_HEADS * HEAD_DIM
SWA_KV = N_KV_HEADS * HEAD_DIM
SWA_KV2 = 2 * SWA_KV
WINDOW = 128
ROPE_THETA = 10000.0
PAST_LEN = 8192
D_FF = 4 * D_MODEL
EPS = 1e-6
LANES = 128
GLA_CHUNK = 128
PROMPT_GLA_CHUNK = 128
TOKEN_BLOCK = 512
FUSED_TOKEN_BLOCK = 512
SAMPLE_ROWS = 256
PROMPT_INPROJ_GROUPS = 2
VMEM_LIMIT = 56 * 1024 * 1024

_SEG = {}
_off = 0
for _name, _w in (("gq", GLA_QK), ("gk", GLA_QK), ("gv", GLA_WIDTH), ("gr", GLA_WIDTH),
                  ("sq", SWA_Q), ("sk", SWA_KV), ("sv", SWA_KV), ("glr", LANES)):
    _SEG[_name] = (_off, _off + _w)
    _off += _w
IN_WIDTH_PADDED = _off


def _dot(a, b):
    return jnp.dot(a, b, preferred_element_type=f32)


def _dot_nt(a, b):
    return lax.dot_general(a, b, (((1,), (1,)), ((), ())), preferred_element_type=f32)


def _dot_tn(a, b):
    return lax.dot_general(a, b, (((0,), (0,)), ((), ())), preferred_element_type=f32)


def _sigmoid(x):
    return 1.0 / (1.0 + jnp.exp(-x))


MOD_W_SLABS = 4


def _mod_kernel(ca_ref, cb_ref, *refs):
    w_refs, b_ref, o_ref = refs[:-2], refs[-2], refs[-1]
    c = jnp.concatenate([ca_ref[...], cb_ref[...]], axis=0)
    s = (c * _sigmoid(c)).astype(bf16)
    kb = w_refs[0].shape[0]
    res = b_ref[...]
    for i, w_ref in enumerate(w_refs):
        res = res + _dot(s[:, i * kb:(i + 1) * kb], w_ref[...].astype(bf16))
    for r in range(res.shape[0]):
        o_ref[r] = res[r:r + 1, :]


def _modulation(c_a, c_b, w_ada, b_ada):
    m = c_a.shape[0] + c_b.shape[0]
    n = w_ada.shape[1]
    bn = 1536
    kb = D_MODEL // MOD_W_SLABS
    slabs = [pl.BlockSpec((kb, bn), lambda j, i=i: (i, j)) for i in range(MOD_W_SLABS)]
    return pl.pallas_call(
        _mod_kernel,
        grid=(n // bn,),
        in_specs=[pl.BlockSpec(c_a.shape, lambda j: (0, 0)), pl.BlockSpec(c_b.shape, lambda j: (0, 0))]
        + slabs + [pl.BlockSpec((1, bn), lambda j: (0, j))],
        out_specs=pl.BlockSpec((m, 1, bn), lambda j: (0, 0, j)),
        out_shape=jax.ShapeDtypeStruct((m, 1, n), f32),
        compiler_params=pltpu.CompilerParams(vmem_limit_bytes=VMEM_LIMIT),
        name="adaln_mod",
    )(c_a, c_b, *([w_ada] * MOD_W_SLABS), b_ada.reshape(1, n))


def _group_rms(x, bd_ref, w_ref):
    ssq = _dot((x * x).astype(bf16), bd_ref[...])
    return x * lax.rsqrt(ssq * (1.0 / HEAD_DIM) + EPS) * w_ref[...]


def _rope(x, cos, sin_signed, low_half):
    partner = jnp.where(low_half, pltpu.roll(x, LANES - 32, axis=1), pltpu.roll(x, 32, axis=1))
    return x * cos + partner * sin_signed


def _dup_heads(x, low_lanes):
    rolled = pltpu.roll(x, HEAD_DIM, axis=1)
    return jnp.where(low_lanes, x, rolled), jnp.where(low_lanes, rolled, x)


def _inproj_kernel(x_ref, sh_ref, sc_ref, n1_ref, win_ref, wgu_ref, bg_ref, qnw_ref, knw_ref,
                   bdq_ref, bdk_ref, cos_ref, sin_ref, *rest, groups, ncast, tail, tail_t):
    slabs, rest = rest[:ncast], rest[ncast:]
    gq_ref, gk_ref, gv_ref, gr_ref, la_ref, sq_ref, k2_ref, v2_ref, kk_ref, vk_ref = rest[:10]
    for src, dst in zip(slabs, rest[10:]):
        dst[...] = src[...].astype(dst.dtype)
    nb, t, d = x_ref.shape
    m = nb * t
    mg = m // groups
    lane = lax.broadcasted_iota(jnp.int32, (mg, LANES), 1)
    low_half = (lane & 32) == 0
    low_lanes = lane < HEAD_DIM
    for grp in range(groups):
        rows = slice(grp * mg, (grp + 1) * mg)
        if nb == 1:
            x = x_ref[:, rows, :]
            sc, sh = sc_ref[...], sh_ref[...]
        else:
            seqs = slice(grp * (nb // groups), (grp + 1) * (nb // groups))
            x = x_ref[seqs]
            sc, sh = sc_ref[seqs], sh_ref[seqs]
        ms = jnp.mean(x * x, axis=-1, keepdims=True)
        hn = x * lax.rsqrt(ms + EPS) * n1_ref[...]
        hn = hn * (1.0 + sc) + sh
        hb = hn.reshape(mg, d).astype(bf16)

        def seg(name):
            a, b = _SEG[name]
            return _dot(hb, win_ref[:, a:b])

        cos = cos_ref[rows, :]
        sin = sin_ref[rows, :]
        sq = _group_rms(seg("sq"), bdq_ref, qnw_ref)
        for c in range(SWA_Q // LANES):
            blk = _rope(sq[:, c * LANES:(c + 1) * LANES], cos, sin, low_half)
            sq_ref[rows, c * LANES:(c + 1) * LANES] = (blk * (LOG2E * HEAD_DIM ** -0.5)).astype(sq_ref.dtype)
        sk = _rope(_group_rms(seg("sk"), bdk_ref, knw_ref), cos, sin, low_half)
        sv = seg("sv")
        keep = (grp + 1) * mg - (m - tail)
        if keep > 0:
            keep = min(keep, mg)
            dst_rows = slice((grp + 1) * mg - keep - (m - tail), (grp + 1) * mg - (m - tail))
            if tail_t:
                assert keep == tail
                kk_ref[...] = sk[mg - keep:].T
                vk_ref[...] = sv[mg - keep:].T
            else:
                kk_ref[dst_rows, :] = sk[mg - keep:]
                vk_ref[dst_rows, :] = sv[mg - keep:]
        for src, dst in ((sk, k2_ref), (sv, v2_ref)):
            d0, d1 = _dup_heads(src, low_lanes)
            dst[rows, :LANES] = d0.astype(dst.dtype)
            dst[rows, LANES:] = d1.astype(dst.dtype)
        glr = seg("glr").astype(bf16)
        g = _dot(glr, wgu_ref[...]) + bg_ref[...]
        log_sig = jnp.minimum(g, 0.0) - jnp.log1p(jnp.exp(-jnp.abs(g)))
        la_ref[rows, :] = log_sig * (LOG2E / GLA_TAU)
        gq_ref[rows, :] = seg("gq") * (GLA_DK ** -0.5)
        gk_ref[rows, :] = seg("gk")
        gv_ref[rows, :] = seg("gv").astype(gv_ref.dtype)
        gr_ref[rows, :] = seg("gr").astype(gr_ref.dtype)


MOD_SHIFT1, MOD_SCALE1, MOD_GATE1, MOD_SHIFT2, MOD_SCALE2, MOD_GATE2 = range(6)


def _mod_spec(nb, row0, chunk, batch_block):
    return pl.BlockSpec((nb, 1, D_MODEL), lambda *ids: (row0 // nb + batch_block(*ids), 0, chunk))


def _inproj(x, mod, row0, n1, win, wgu, bg, qnw, knw, bdq, bdk, cos_t, sin_t, nb, tb, tail, act, groups,
            to_bf16=(), tail_t=False):
    bsz, t, d = x.shape
    m = nb * tb
    nt = t // tb
    grid = (bsz // nb, nt)
    tok = bsz * t
    nsteps = grid[0] * grid[1]
    slab_specs = [pl.BlockSpec((w.shape[0] // nsteps, w.shape[1]), lambda i, j: (i * nt + j, 0))
                  for w in to_bf16]

    def full(shape):
        return pl.BlockSpec(shape, lambda i, j: (0,) * len(shape))

    def out(width):
        return pl.BlockSpec((m, width), lambda i, j: (i * nt + j, 0))

    tail_blk = (SWA_KV, tail) if tail_t else (tail, SWA_KV)
    tail_spec = pl.BlockSpec(tail_blk, lambda i, j: (i, 0))
    tail_shape = jax.ShapeDtypeStruct(((bsz // nb) * tail_blk[0], tail_blk[1]), f32)
    outs = ((GLA_QK, f32), (GLA_QK, f32), (GLA_WIDTH, act), (GLA_WIDTH, act), (GLA_QK, f32),
            (SWA_Q, act), (SWA_KV2, bf16), (SWA_KV2, bf16))
    return pl.pallas_call(
        functools.partial(_inproj_kernel, groups=groups, ncast=len(to_bf16), tail=tail, tail_t=tail_t),
        grid=grid,
        in_specs=[pl.BlockSpec((nb, tb, d), lambda i, j: (i, j, 0)),
                  _mod_spec(nb, row0, MOD_SHIFT1, lambda i, j: i),
                  _mod_spec(nb, row0, MOD_SCALE1, lambda i, j: i),
                  full((1, 1, d)),
                  full((d, IN_WIDTH_PADDED)),
                  full((LANES, GLA_QK)),
                  full((1, GLA_QK)),
                  full((1, SWA_Q)),
                  full((1, SWA_KV)),
                  full((SWA_Q, SWA_Q)),
                  full((SWA_KV, SWA_KV)),
                  pl.BlockSpec((m, LANES), lambda i, j: (j, 0)),
                  pl.BlockSpec((m, LANES), lambda i, j: (j, 0))] + slab_specs,
        out_specs=[out(w) for w, _ in outs] + [tail_spec, tail_spec] + slab_specs,
        out_shape=[jax.ShapeDtypeStruct((tok, w), dt) for w, dt in outs]
        + [tail_shape] * 2
        + [jax.ShapeDtypeStruct(w.shape, bf16) for w in to_bf16],
        compiler_params=pltpu.CompilerParams(
            dimension_semantics=("parallel", "arbitrary"), vmem_limit_bytes=VMEM_LIMIT),
        name="inproj",
    )(x, mod, mod, n1, win, wgu, bg, qnw, knw, bdq, bdk, cos_t, sin_t, *to_bf16)


def _gla_constants(chunk, seq):
    t = np.arange(chunk)
    levels = []
    m = 1
    while m < seq:
        levels.append(m)
        m *= 2
    masks = [np.eye(chunk, dtype=bool)]
    for m in levels:
        upper = (t % (2 * m) >= m)[:, None]
        lower = (t % (2 * m) < m)[None, :]
        same = (t[:, None] // (2 * m)) == (t[None, :] // (2 * m))
        masks.append(same & upper & lower)
    tri = t[None, :] <= t[:, None]
    tiled = np.tile(np.stack(masks).astype(np.float32), (1, 1, GLA_HEADS))
    return tuple(levels), jnp.asarray(tri.astype(np.float32), bf16), jnp.asarray(tiled)


def _stack_heads(xb, lane_head, axis):
    zero = jnp.zeros_like(xb)
    return jnp.concatenate([jnp.where(lane_head == h, xb, zero) for h in range(GLA_HEADS)], axis=axis)


def _block_sums(la, bcum, m, row, rolls):
    c, n = la.shape
    if m == 1:
        return la, None
    if m < 8:
        def rolled(shift):
            if shift not in rolls:
                rolls[shift] = pltpu.roll(la, shift % c, axis=0)
            return rolls[shift]
        pos = row & (m - 1)
        pre = la
        suf = None
        for j in range(1, m):
            pre = pre + jnp.where(pos >= j, rolled(j), 0.0)
            term = jnp.where(pos < m - j, rolled(-j), 0.0)
            suf = term if suf is None else suf + term
        return pre, suf
    before, last = [], []
    for i in range(c // m):
        before.append(jnp.zeros((m, n), f32) if i == 0
                      else jnp.broadcast_to(bcum[i * m - 1:i * m], (m, n)))
        last.append(jnp.broadcast_to(bcum[(i + 1) * m - 1:(i + 1) * m], (m, n)))
    if len(before) == 1:
        return bcum - before[0], last[0] - bcum
    return bcum - jnp.concatenate(before, axis=0), jnp.concatenate(last, axis=0) - bcum


def _gla_scores(q_ref, k_ref, v_ref, la_ref, tri_ref, masks_ref, levels, seq):
    c = q_ref.shape[0]
    la = la_ref[...]
    hi = la.astype(bf16)
    lo = (la - hi.astype(f32)).astype(bf16)
    tri = tri_ref[...]
    bcum = _dot(tri, hi) + _dot(tri, lo)
    q = q_ref[...]
    k = k_ref[...]
    vb = v_ref[...].astype(bf16)
    row = lax.broadcasted_iota(jnp.int32, (c, GLA_QK), 0)
    row_head_t = lax.broadcasted_iota(jnp.int32, (GLA_QK, c), 0) >> 6
    rolls = {}
    attn = None
    for lvl, m in enumerate((0,) + levels):
        if m == 0:
            qt, kt = q, k
        else:
            pre, suf = _block_sums(la, bcum, m, row, rolls)
            qt = q * jnp.exp2(pre)
            kt = k if suf is None else k * jnp.exp2(suf)
        kt_t = kt.T.astype(bf16)
        r = _dot(qt.astype(bf16), _stack_heads(kt_t, row_head_t, 1))
        rm = r * masks_ref[lvl]
        attn = rm if attn is None else attn + rm
    a = attn.astype(bf16)
    pre, suf = _block_sums(la, bcum, seq, row, rolls)
    qf = (q * jnp.exp2(pre)).astype(bf16)
    kf = (k * jnp.exp2(suf)).astype(bf16)
    return a, qf, kf, vb, bcum, hi, lo


def _gla_values(a, vb):
    v_head = lax.broadcasted_iota(jnp.int32, vb.shape, 1) >> 7
    return _dot(a, _stack_heads(vb, v_head, 0))


def _gla_finish(o, r_ref, gnw, go_ref):
    for h in range(GLA_HEADS):
        sl = slice(h * GLA_DV, (h + 1) * GLA_DV)
        oh = o[:, sl]
        r = r_ref[:, sl].astype(f32)
        ms = jnp.mean(oh * oh, axis=-1, keepdims=True)
        go_ref[:, sl] = (oh * lax.rsqrt(ms + EPS) * gnw * (r * _sigmoid(r))).astype(go_ref.dtype)


def _gla_prompt_block(st, q_ref, k_ref, v_ref, la_ref, r_ref, tri_ref, masks_ref, gnw, go_ref, levels,
                      between=lambda: None):
    c = PROMPT_GLA_CHUNK
    nchunk = q_ref.shape[0] // c
    lane_head = lax.broadcasted_iota(jnp.int32, (GLA_DV, GLA_QK), 1) >> 6
    row_head = lax.broadcasted_iota(jnp.int32, (GLA_QK, GLA_DV), 0) >> 6
    chunks = []
    for ci in range(nchunk):
        rows = pl.ds(ci * c, c)
        chunks.append(_gla_scores(q_ref.at[rows], k_ref.at[rows], v_ref.at[rows], la_ref.at[rows],
                                  tri_ref, masks_ref, levels, c))
        between()
    intra, upds, decays = [], [], []
    for a, _, kf, vb, bcum, _, _ in chunks:
        intra.append(_gla_values(a, vb))
        full = _dot_tn(vb, kf)
        upd = None
        for h in range(GLA_HEADS):
            term = jnp.where(lane_head == h, full[h * GLA_DV:(h + 1) * GLA_DV], 0.0)
            upd = term if upd is None else upd + term
        upds.append(upd)
        decays.append(jnp.exp2(bcum[c - 1:c]))
    between()
    for ci in range(nchunk):
        rows = pl.ds(ci * c, c)
        sbd = _stack_heads(st.T.astype(bf16), row_head, 1)
        o = intra[ci] + _dot(chunks[ci][1], sbd)
        _gla_finish(o, r_ref.at[rows], gnw, go_ref.at[rows])
        st = decays[ci] * st + upds[ci]
    return st


def _gla_sample_kernel(q_ref, k_ref, v_ref, la_ref, r_ref, s0_ref, tri_ref, masks_ref, gnw_ref,
                       go_ref, sout_ref, *, levels, seq):
    c = q_ref.shape[0]
    a, qf, kf, vb, _, hi, lo = _gla_scores(q_ref, k_ref, v_ref, la_ref, tri_ref, masks_ref, levels, seq)
    o_intra = _gla_values(a, vb)
    row_head = lax.broadcasted_iota(jnp.int32, (GLA_QK, GLA_DV), 0) >> 6
    ones = jnp.ones((seq, GLA_DV), bf16)
    inter = []
    for b in range(c // seq):
        rows = slice(b * seq, (b + 1) * seq)
        s_old = s0_ref[b]
        sbd = _stack_heads(s_old.astype(bf16), row_head, 1)
        inter.append(_dot(qf[rows], sbd))
        p = _dot_tn(kf[rows], vb[rows])
        upd = None
        for h in range(GLA_HEADS):
            term = jnp.where(row_head == h, p[:, h * GLA_DV:(h + 1) * GLA_DV], 0.0)
            upd = term if upd is None else upd + term
        total = _dot_tn(hi[rows], ones) + _dot_tn(lo[rows], ones)
        sout_ref[b] = jnp.exp2(total) * s_old + upd
    o = o_intra + jnp.concatenate(inter, axis=0)
    _gla_finish(o, r_ref, gnw_ref[...], go_ref)


def _gla_sample(gq, gk, gv, la, gr, state, gnw, bsz, t):
    c = GLA_CHUNK
    nseq = c // t
    levels, tri, masks = _gla_constants(c, t)

    def tok(width):
        return pl.BlockSpec((c, width), lambda i: (i, 0))

    return pl.pallas_call(
        functools.partial(_gla_sample_kernel, levels=levels, seq=t),
        grid=(bsz // nseq,),
        in_specs=[tok(GLA_QK), tok(GLA_QK), tok(GLA_WIDTH), tok(GLA_QK), tok(GLA_WIDTH),
                  pl.BlockSpec((nseq, GLA_QK, GLA_DV), lambda i: (i, 0, 0)),
                  pl.BlockSpec(tri.shape, lambda i: (0, 0)),
                  pl.BlockSpec(masks.shape, lambda i: (0, 0, 0)),
                  pl.BlockSpec((1, GLA_DV), lambda i: (0, 0))],
        out_specs=[tok(GLA_WIDTH),
                   pl.BlockSpec((nseq, GLA_QK, GLA_DV), lambda i: (i, 0, 0))],
        out_shape=[jax.ShapeDtypeStruct((bsz * t, GLA_WIDTH), f32),
                   jax.ShapeDtypeStruct((bsz, GLA_QK, GLA_DV), f32)],
        compiler_params=pltpu.CompilerParams(
            dimension_semantics=("parallel",), vmem_limit_bytes=VMEM_LIMIT),
        name="gla_sample",
    )(gq, gk, gv, la, gr, state, tri, masks, gnw)


def _sink_attention(q_ref, rows, kdups, vdups, mask, sink_ref, o_ref, between=lambda: None):
    n = len(kdups)
    heads_per_group = N_Q_HEADS // N_KV_HEADS
    low_q = lax.broadcasted_iota(jnp.int32, (rows, LANES), 1) < HEAD_DIM
    scores = []
    for i in range(n):
        per_group = []
        for g in range(N_KV_HEADS):
            stack = []
            for p in range(heads_per_group // 2):
                pair = g * (heads_per_group // 2) + p
                qp = q_ref[i * rows:(i + 1) * rows, pair * LANES:(pair + 1) * LANES].astype(bf16)
                zero = jnp.zeros_like(qp)
                stack.append(jnp.where(low_q, qp, zero))
                stack.append(jnp.where(low_q, zero, qp))
            per_group.append(_dot(jnp.concatenate(stack, axis=0), kdups[i][g]))
        scores.append(per_group)
    between()
    probs, inv = [], []
    for head in range(N_Q_HEADS):
        g, hh = divmod(head, heads_per_group)
        parts = [scores[i][g][hh * rows:(hh + 1) * rows] for i in range(n)]
        s = parts[0] if n == 1 else jnp.concatenate(parts, axis=0)
        sink = sink_ref[head] * LOG2E
        s = jnp.where(mask, s, -1e30)
        mx = jnp.maximum(jnp.max(s, axis=-1, keepdims=True), sink)
        p = jnp.exp2(s - mx)
        inv.append(1.0 / (jnp.sum(p, axis=-1, keepdims=True) + jnp.exp2(sink - mx)))
        probs.append(p.astype(bf16))
    between()
    outs = []
    for i in range(n):
        per_group = []
        for g in range(N_KV_HEADS):
            p_i = jnp.concatenate([probs[g * heads_per_group + hh][i * rows:(i + 1) * rows]
                                   for hh in range(heads_per_group)], axis=0)
            per_group.append(_dot(p_i, vdups[i][g]))
        outs.append(per_group)
    low_all = lax.broadcasted_iota(jnp.int32, (n * rows, LANES), 1) < HEAD_DIM
    for pair in range(N_Q_HEADS // 2):
        halves = []
        for head in (2 * pair, 2 * pair + 1):
            g, hh = divmod(head, heads_per_group)
            parts = [outs[i][g][hh * rows:(hh + 1) * rows] for i in range(n)]
            o = parts[0] if n == 1 else jnp.concatenate(parts, axis=0)
            halves.append(o * inv[head])
        o_ref[:, pair * LANES:(pair + 1) * LANES] = jnp.where(low_all, halves[0], halves[1]).astype(o_ref.dtype)


def _swa_prompt_blocks(q_ref, k_before, k2_ref, v_before, v2_ref, first, sink_ref, o_ref,
                       between=lambda: None):
    w = WINDOW
    nblk = q_ref.shape[0] // w

    def dup(before, ref, transposed):
        blocks = [before] + [ref[i * w:(i + 1) * w, :] for i in range(nblk)]
        pieces = [[blk[:, g * LANES:(g + 1) * LANES] for g in range(N_KV_HEADS)] for blk in blocks]
        if transposed:
            pieces = [[p.astype(f32).T.astype(bf16) for p in per_blk] for per_blk in pieces]
        return [[jnp.concatenate([pieces[i][g], pieces[i + 1][g]], axis=1 if transposed else 0)
                 for g in range(N_KV_HEADS)] for i in range(nblk)]

    row = lax.broadcasted_iota(jnp.int32, (nblk * w, 2 * w), 0)
    tk = lax.broadcasted_iota(jnp.int32, (nblk * w, 2 * w), 1)
    rel = tk - (row & (w - 1))
    first_key = jnp.where(row < w, jnp.where(first, w, 0), 0)
    mask = (rel > 0) & (rel <= w) & (tk >= first_key)
    _sink_attention(q_ref, w, dup(k_before, k2_ref, True), dup(v_before, v2_ref, False), mask, sink_ref,
                    o_ref, between)


def _swa_sample_kernel(sink_ref, q_ref, kn_ref, vn_ref, pk_ref, pv_ref, o_ref, ko_ref, vo_ref, *, seq):
    nseq, _, _, w = pk_ref.shape
    rows = nseq * seq
    heads_per_group = N_Q_HEADS // N_KV_HEADS
    lane = lax.broadcasted_iota(jnp.int32, (rows, LANES), 1)
    r_id = lax.broadcasted_iota(jnp.int32, (rows, LANES), 0)
    low = lane < HEAD_DIM
    pos = r_id & (seq - 1)
    mask_old = lane > pos
    seq_shift = seq.bit_length() - 1
    mask_new = ((r_id >> seq_shift) == (lane >> seq_shift)) & ((lane & (seq - 1)) <= pos)
    kn = kn_ref[...]
    vn = vn_ref[...]
    knt_dup = [d.T.astype(bf16) for d in _dup_heads(kn, low)]
    vn_dup = [d.astype(bf16) for d in _dup_heads(vn, low)]

    pad = jnp.zeros((w - seq, LANES), f32)
    tail_lanes = lax.broadcasted_iota(jnp.int32, (HEAD_DIM, w), 1) >= w - seq
    kt_dup, v_dup = [], []
    for b in range(nseq):
        tok = slice(b * seq, (b + 1) * seq)
        per_k, per_v = [], []
        for new, old_ref, out_ref, per in ((kn, pk_ref, ko_ref, per_k), (vn, pv_ref, vo_ref, per_v)):
            new_t = jnp.concatenate([pad, new[tok]], axis=0).T
            for g in range(N_KV_HEADS):
                old = old_ref[b, g]
                out_ref[b, g] = jnp.where(tail_lanes, new_t[g * HEAD_DIM:(g + 1) * HEAD_DIM],
                                          pltpu.roll(old, w - seq, axis=1))
                twice = jnp.concatenate([old, old], axis=0)
                per.append((twice if per is per_k else twice.T).astype(bf16))
        kt_dup.append(per_k)
        v_dup.append(per_v)

    lhs = []
    for g in range(N_KV_HEADS):
        stack = []
        for p in range(heads_per_group // 2):
            pair = g * (heads_per_group // 2) + p
            qp = q_ref[:, pair * LANES:(pair + 1) * LANES].astype(bf16)
            zero = jnp.zeros_like(qp)
            stack.append(jnp.where(low, qp, zero))
            stack.append(jnp.where(low, zero, qp))
        lhs.append(jnp.concatenate(stack, axis=0))
    s_new = [_dot(lhs[g], knt_dup[g]) for g in range(N_KV_HEADS)]
    s_old = []
    for b in range(nseq):
        per = []
        for g in range(N_KV_HEADS):
            qb = jnp.concatenate([lhs[g][hh * rows + b * seq:hh * rows + (b + 1) * seq]
                                  for hh in range(heads_per_group)], axis=0)
            per.append(_dot(qb, kt_dup[b][g]))
        s_old.append(per)

    p_old, p_new, inv = [], [], []
    for head in range(N_Q_HEADS):
        g, hh = divmod(head, heads_per_group)
        so = jnp.concatenate([s_old[b][g][hh * seq:(hh + 1) * seq] for b in range(nseq)], axis=0)
        so = jnp.where(mask_old, so, -1e30)
        sn = jnp.where(mask_new, s_new[g][hh * rows:(hh + 1) * rows], -1e30)
        sink = sink_ref[head] * LOG2E
        mx = jnp.maximum(jnp.maximum(jnp.max(so, axis=-1, keepdims=True),
                                     jnp.max(sn, axis=-1, keepdims=True)), sink)
        po = jnp.exp2(so - mx)
        pn = jnp.exp2(sn - mx)
        inv.append(1.0 / (jnp.sum(po, axis=-1, keepdims=True) + jnp.sum(pn, axis=-1, keepdims=True)
                          + jnp.exp2(sink - mx)))
        p_old.append(po.astype(bf16))
        p_new.append(pn.astype(bf16))

    o_new = [_dot(jnp.concatenate([p_new[g * heads_per_group + hh] for hh in range(heads_per_group)], axis=0),
                  vn_dup[g]) for g in range(N_KV_HEADS)]
    o_old = []
    for b in range(nseq):
        per = []
        for g in range(N_KV_HEADS):
            pb = jnp.concatenate([p_old[g * heads_per_group + hh][b * seq:(b + 1) * seq]
                                  for hh in range(heads_per_group)], axis=0)
            per.append(_dot(pb, v_dup[b][g]))
        o_old.append(per)

    for pair in range(N_Q_HEADS // 2):
        halves = []
        for head in (2 * pair, 2 * pair + 1):
            g, hh = divmod(head, heads_per_group)
            old = jnp.concatenate([o_old[b][g][hh * seq:(hh + 1) * seq] for b in range(nseq)], axis=0)
            halves.append((old + o_new[g][hh * rows:(hh + 1) * rows]) * inv[head])
        o_ref[:, pair * LANES:(pair + 1) * LANES] = jnp.where(low, halves[0], halves[1])


def _swa_sample(sq, sk, sv, past_kt, past_vt, sinks, bsz, t):
    nseq = SAMPLE_ROWS // 2 // t
    assert nseq * t == LANES
    tok = lambda width: pl.BlockSpec((nseq * t, width), lambda i, s: (i, 0))
    cache = pl.BlockSpec((nseq,) + past_kt.shape[1:], lambda i, s: (i, 0, 0, 0))
    return pl.pallas_call(
        functools.partial(_swa_sample_kernel, seq=t),
        grid_spec=pltpu.PrefetchScalarGridSpec(
            num_scalar_prefetch=1,
            grid=(bsz // nseq,),
            in_specs=[tok(SWA_Q), tok(SWA_KV), tok(SWA_KV), cache, cache],
            out_specs=[tok(SWA_Q), cache, cache]),
        out_shape=[jax.ShapeDtypeStruct((bsz * t, SWA_Q), f32),
                   jax.ShapeDtypeStruct(past_kt.shape, f32),
                   jax.ShapeDtypeStruct(past_vt.shape, f32)],
        compiler_params=pltpu.CompilerParams(
            dimension_semantics=("parallel",), vmem_limit_bytes=VMEM_LIMIT),
        name="swa_sample",
    )(sinks, sq, sk, sv, past_kt, past_vt)


FFN_CHUNK = 512


def _out_proj_norm(streams, n2, wout_ref):
    mix = [s[1] for s in streams]
    mix = mix[0] if len(mix) == 1 else jnp.concatenate(mix, axis=0)
    mixed = _dot(mix[:, :GLA_WIDTH], wout_ref[:GLA_WIDTH, :]) + _dot(mix[:, GLA_WIDTH:], wout_ref[GLA_WIDTH:, :])
    hs, hbs, r0 = [], [], 0
    for x, _, g1, sh2, sc2 in streams:
        nb, t, d = x.shape
        h = x + g1 * mixed[r0:r0 + nb * t].reshape(nb, t, d)
        ms = jnp.mean(h * h, axis=-1, keepdims=True)
        hn = h * lax.rsqrt(ms + EPS) * n2
        hn = hn * (1.0 + sc2) + sh2
        hs.append(h)
        hbs.append(hn.reshape(nb * t, d).astype(bf16))
        r0 += nb * t
    return hs, (hbs[0] if len(hbs) == 1 else jnp.concatenate(hbs, axis=0))


def _ffn_piece(hb, w1_ref, w2_ref, c, width):
    cols = slice(c * width, (c + 1) * width)
    a = jnp.maximum(_dot(hb, w1_ref[:, cols]), 0.0)
    return _dot((a * a).astype(bf16), w2_ref[cols, :])


def _resident(shape):
    return pl.BlockSpec(shape, lambda *_: (0,) * len(shape), pipeline_mode=pl.Buffered(1))


def _prompt_kernel(sink_ref,
                   q_ref, k_ref, v_ref, la_ref, r_ref, sq_ref, k2_ref, v2_ref, k2p_ref, v2p_ref,
                   tri_ref, masks_ref, gnw_ref,
                   x_ref, g1_ref, sh2_ref, sc2_ref, g2_ref,
                   xs_ref, gos_ref, sos_ref, g1s_ref, sh2s_ref, sc2s_ref, g2s_ref,
                   n2_ref, wout_ref, w1_ref, w2_ref,
                   y_ref, ys_ref, sout_ref, state_ref, mix_ref, *, levels, nt, nsteps):
    s = pl.program_id(0)
    slot = s % 2

    @pl.when(s == 0)
    def _():
        mix_ref[1] = jnp.zeros(mix_ref.shape[1:], mix_ref.dtype)
        state_ref[...] = jnp.zeros_like(state_ref)

    j = jnp.minimum(s, nsteps - 1) % nt
    st_in = state_ref[...]
    st = jnp.where(j == 0, 0.0, st_in)
    gnw = gnw_ref[...]
    out = mix_ref.at[slot]
    mix_s = jnp.concatenate([gos_ref[...], sos_ref[...]], axis=1).astype(bf16)
    (h, h_s), hb = _out_proj_norm(
        [(x_ref[...], mix_ref[1 - slot], g1_ref[...], sh2_ref[...], sc2_ref[...]),
         (xs_ref[...], mix_s, g1s_ref[...], sh2s_ref[...], sc2s_ref[...])], n2_ref[...], wout_ref)
    rows_p = h.shape[0] * h.shape[1]
    todo = list(range(D_FF // FFN_CHUNK))
    ff = []

    def ffn_piece():
        if todo:
            part = _ffn_piece(hb, w1_ref, w2_ref, todo.pop(0), FFN_CHUNK)
            ff[:] = [part if not ff else ff[0] + part]

    st = _gla_prompt_block(st, q_ref, k_ref, v_ref, la_ref, r_ref, tri_ref, masks_ref, gnw,
                           out.at[:, pl.ds(0, GLA_WIDTH)], levels, ffn_piece)
    _swa_prompt_blocks(sq_ref, k2p_ref[...], k2_ref, v2p_ref[...], v2_ref, j == 0, sink_ref,
                       out.at[:, pl.ds(GLA_WIDTH, SWA_Q)], ffn_piece)
    while todo:
        ffn_piece()
    st = jnp.where(s < nsteps, st, st_in)
    state_ref[...] = st
    sout_ref[0] = st.T
    y_ref[...] = h + g2_ref[...] * ff[0][:rows_p].reshape(h.shape)
    ys_ref[...] = h_s + g2s_ref[...] * ff[0][rows_p:].reshape(h_s.shape)


def _prompt_mix_ffn(x, gq, gk, gv, la, gr, sq, k2, v2, gnw, sinks, mod, row0, n2, wout, w1, w2,
                    xs, gos, sos, row0_s):
    bsz, t, d = x.shape
    tb = FUSED_TOKEN_BLOCK
    nt = t // tb
    nsteps = bsz * nt
    levels, tri, masks = _gla_constants(PROMPT_GLA_CHUNK, PROMPT_GLA_CHUNK)
    per_blk = tb // WINDOW
    bs, ts, _ = xs.shape
    nbs = bs // nsteps
    assert nbs * nsteps == bs and (nbs * ts) % 16 == 0

    def mix_blk(s):
        return jnp.minimum(s, nsteps - 1)

    def ffn_blk(s):
        return jnp.maximum(s - 1, 0)

    def tok(width):
        return pl.BlockSpec((tb, width), lambda s, _: (mix_blk(s), 0))

    prev = pl.BlockSpec((WINDOW, SWA_KV2), lambda s, _: (jnp.maximum(mix_blk(s) * per_blk - 1, 0), 0))
    xspec = pl.BlockSpec((1, tb, d), lambda s, _: (ffn_blk(s) // nt, ffn_blk(s) % nt, 0))
    ffn_chunks = (MOD_GATE1, MOD_SHIFT2, MOD_SCALE2, MOD_GATE2)
    mods = [_mod_spec(1, row0, chunk, lambda s, _: ffn_blk(s) // nt) for chunk in ffn_chunks]
    mods_s = [_mod_spec(nbs, row0_s, chunk, lambda s, _: ffn_blk(s)) for chunk in ffn_chunks]
    xs_spec = pl.BlockSpec((nbs, ts, d), lambda s, _: (ffn_blk(s), 0, 0))
    mix_s_spec = pl.BlockSpec((nbs * ts, GLA_WIDTH), lambda s, _: (ffn_blk(s), 0))
    return pl.pallas_call(
        functools.partial(_prompt_kernel, levels=levels, nt=nt, nsteps=nsteps),
        grid_spec=pltpu.PrefetchScalarGridSpec(
            num_scalar_prefetch=1,
            grid=(nsteps + 1,),
            in_specs=[tok(GLA_QK), tok(GLA_QK), tok(GLA_WIDTH), tok(GLA_QK), tok(GLA_WIDTH),
                      tok(SWA_Q), tok(SWA_KV2), tok(SWA_KV2), prev, prev,
                      _resident(tri.shape), _resident(masks.shape), _resident((1, GLA_DV)),
                      xspec, *mods, xs_spec, mix_s_spec, mix_s_spec, *mods_s, _resident((1, 1, d)),
                      _resident((d, d)), _resident((d, D_FF)), _resident((D_FF, d))],
            out_specs=[xspec, xs_spec,
                       pl.BlockSpec((1, GLA_QK, GLA_DV), lambda s, _: (mix_blk(s) // nt, 0, 0))],
            scratch_shapes=[pltpu.VMEM((GLA_DV, GLA_QK), f32),
                            pltpu.VMEM((2, tb, GLA_WIDTH + SWA_Q), bf16)]),
        out_shape=[jax.ShapeDtypeStruct((bsz, t, d), f32),
                   jax.ShapeDtypeStruct(xs.shape, f32),
                   jax.ShapeDtypeStruct((bsz, GLA_QK, GLA_DV), f32)],
        compiler_params=pltpu.CompilerParams(
            dimension_semantics=("arbitrary",), vmem_limit_bytes=VMEM_LIMIT),
        name="prompt_mix_ffn",
    )(sinks, gq, gk, gv, la, gr, sq, k2, v2, k2, v2, tri, masks, gnw,
      x, mod, mod, mod, mod, xs, gos, sos, mod, mod, mod, mod, n2, wout, w1, w2)


def _rope_tables(pos, copies):
    half = HEAD_DIM // 2
    inv = np.power(np.float32(ROPE_THETA), -np.arange(half, dtype=np.float32) * np.float32(2.0 / HEAD_DIM))
    ang = np.asarray(pos, np.float32)[:, None] * inv[None, :].astype(np.float32)
    cos = np.cos(ang).astype(np.float32)
    sin = np.sin(ang).astype(np.float32)
    reps = (copies, LANES // HEAD_DIM)
    return (jnp.asarray(np.tile(np.concatenate([cos, cos], axis=-1), reps)),
            jnp.asarray(np.tile(np.concatenate([-sin, sin], axis=-1), reps)))


def _block_diag_ones(n, blk):
    idx = np.arange(n) // blk
    return jnp.asarray((idx[:, None] == idx[None, :]).astype(np.float32), bf16)


def _layer_weights(w_in, w_gate_up, b_gate, q_norm_w, k_norm_w, w_out, w_ff1, w_ff2):
    splits = np.cumsum([GLA_QK, GLA_QK, GLA_WIDTH, GLA_WIDTH, GLA_GATE_RANK, SWA_Q, SWA_KV])
    gq, gk, gv, gr, glr, sq, sk, sv = jnp.split(w_in, [int(s) for s in splits], axis=1)
    glr = jnp.pad(glr, ((0, 0), (0, LANES - GLA_GATE_RANK)))
    win = jnp.concatenate([gq, gk, gv, gr, sq, sk, sv, glr], axis=1).astype(bf16)
    wgu = jnp.pad(w_gate_up, ((0, LANES - GLA_GATE_RANK), (0, 0))).astype(bf16)
    return dict(
        win=win, wgu=wgu, bg=b_gate.reshape(1, GLA_QK),
        qnw=jnp.tile(q_norm_w, N_Q_HEADS).reshape(1, SWA_Q),
        knw=jnp.tile(k_norm_w, N_KV_HEADS).reshape(1, SWA_KV),
        ffn_f32=(w_out, w_ff1, w_ff2))


def _project(x, mod, row0, pos, lw, n1, bdq, bdk, prompt):
    bsz, t, d = x.shape
    if prompt:
        nb, tb, tail, act, groups, cast = 1, TOKEN_BLOCK, WINDOW, bf16, PROMPT_INPROJ_GROUPS, lw["ffn_f32"]
    else:
        nb, tb, tail, act, groups, cast = SAMPLE_ROWS // t, t, SAMPLE_ROWS, f32, 1, ()
    cos_t, sin_t = _rope_tables(pos, nb)
    return _inproj(x, mod, row0, n1.reshape(1, 1, d), lw["win"], lw["wgu"], lw["bg"], lw["qnw"], lw["knw"],
                   bdq, bdk, cos_t, sin_t, nb, tb, tail, act, groups, cast, tail_t=prompt)


def _decoder_layers(xp, xs, mod, pos_p, pos_s, state, past_k, past_v, lw, n1, n2, gnw, sinks, bdq, bdk):
    bp, tp, d = xp.shape
    bs, ts, _ = xs.shape
    heads = (N_KV_HEADS, HEAD_DIM)
    gq, gk, gv, gr, la, sq, _, _, kk, vk = _project(xs, mod, 0, pos_s, lw, n1, bdq, bdk, False)
    go_s, state_s = _gla_sample(gq, gk, gv, la, gr, state.reshape(bs, GLA_QK, GLA_DV), gnw, bs, ts)
    so_s, kt, vt = _swa_sample(sq, kk, vk, past_k.transpose(0, 2, 3, 1), past_v.transpose(0, 2, 3, 1),
                               sinks, bs, ts)
    gq, gk, gv, gr, la, sq, k2, v2, kk, vk, *ffn_w = _project(xp, mod, bs, pos_p, lw, n1, bdq, bdk, True)
    yp, ys, s_t = _prompt_mix_ffn(xp, gq, gk, gv, la, gr, sq, k2, v2, gnw, sinks, mod, bs,
                                  n2.reshape(1, 1, d), *ffn_w, xs, go_s, so_s, 0)
    state_p = s_t.reshape(bp, GLA_HEADS, GLA_DK, GLA_DV)
    return (yp, ys, state_p,
            kk.reshape(bp, *heads, WINDOW).transpose(0, 3, 1, 2), vk.reshape(bp, *heads, WINDOW).transpose(0, 3, 1, 2),
            state_s.reshape(bs, GLA_HEADS, GLA_DK, GLA_DV),
            kt.transpose(0, 3, 1, 2), vt.transpose(0, 3, 1, 2))


def kernel(x_prompt, x_sample, state_gla, cache_swa_k, cache_swa_v, c_prompt, c_sample, w_ada, b_ada, norm1_w, norm2_w, w_in, w_gate_up, b_gate, gla_norm_w, q_norm_w, k_norm_w, sinks, w_out, w_ff1, w_ff2):
    depth = w_ada.shape[0]
    bp, tp, _ = x_prompt.shape
    bs, ts, _ = x_sample.shape
    pos_p = np.arange(tp)
    pos_s = PAST_LEN + np.arange(ts)
    bdq = _block_diag_ones(SWA_Q, HEAD_DIM)
    bdk = _block_diag_ones(SWA_KV, HEAD_DIM)
    yp, ys = x_prompt, x_sample
    outs = [[] for _ in range(6)]
    for l in range(depth):
        mod = _modulation(c_sample, c_prompt, w_ada[l], b_ada[l])
        lw = _layer_weights(w_in[l], w_gate_up[l], b_gate[l], q_norm_w[l], k_norm_w[l],
                            w_out[l], w_ff1[l], w_ff2[l])
        gnw = gla_norm_w[l].reshape(1, GLA_DV)
        yp, ys, *new = _decoder_layers(yp, ys, mod, pos_p, pos_s, state_gla[l], cache_swa_k[l],
                                       cache_swa_v[l], lw, norm1_w[l], norm2_w[l], gnw, sinks[l], bdq, bdk)
        for lst, val in zip(outs, new):
            lst.append(val)
    return (yp, ys) + tuple(jnp.stack(o) for o in outs)
```

```python
import functools

import jax
import jax.numpy as jnp
import numpy as np
from jax import lax
from jax.experimental import pallas as pl
from jax.experimental.pallas import tpu as pltpu

f32 = jnp.float32
bf16 = jnp.bfloat16

D_MODEL = 1024
GLA_HEADS = 4
GLA_DK = 64
GLA_DV = 128
GLA_QK = GLA_HEADS * GLA_DK
GLA_WIDTH = GLA_HEADS * GLA_DV
GLA_GATE_RANK = 16
GLA_TAU = 16.0
LOG2E = 1.4426950408889634
HEAD_DIM = 64
N_Q_HEADS = 8
N_KV_HEADS = 2
SWA_Q = N_Q_HEADS * HEAD_DIM
SWA_KV = N_KV_HEADS * HEAD_DIM
SWA_KV2 = 2 * SWA_KV
WINDOW = 128
ROPE_THETA = 10000.0
PAST_LEN = 8192
D_FF = 4 * D_MODEL
EPS = 1e-6
LANES = 128
GLA_CHUNK = 128
PROMPT_GLA_CHUNK = 128
TOKEN_BLOCK = 1024
FUSED_TOKEN_BLOCK = 512
SAMPLE_ROWS = 256
PROMPT_INPROJ_GROUPS = 4
VMEM_LIMIT = 56 * 1024 * 1024

_SEG = {}
_off = 0
for _name, _w in (("gq", GLA_QK), ("gk", GLA_QK), ("gv", GLA_WIDTH), ("gr", GLA_WIDTH),
                  ("sq", SWA_Q), ("sk", SWA_KV), ("sv", SWA_KV), ("glr", LANES)):
    _SEG[_name] = (_off, _off + _w)
    _off += _w
IN_WIDTH_PADDED = _off


def _dot(a, b):
    return jnp.dot(a, b, preferred_element_type=f32)


def _dot_nt(a, b):
    return lax.dot_general(a, b, (((1,), (1,)), ((), ())), preferred_element_type=f32)


def _dot_tn(a, b):
    return lax.dot_general(a, b, (((0,), (0,)), ((), ())), preferred_element_type=f32)


def _sigmoid(x):
    return 1.0 / (1.0 + jnp.exp(-x))


MOD_W_SLABS = 4


def _mod_kernel(ca_ref, cb_ref, *refs):
    w_refs, b_ref, o_ref = refs[:-2], refs[-2], refs[-1]
    c = jnp.concatenate([ca_ref[...], cb_ref[...]], axis=0)
    s = (c * _sigmoid(c)).astype(bf16)
    kb = w_refs[0].shape[0]
    res = b_ref[...]
    for i, w_ref in enumerate(w_refs):
        res = res + _dot(s[:, i * kb:(i + 1) * kb], w_ref[...].astype(bf16))
    for r in range(res.shape[0]):
        o_ref[r] = res[r:r + 1, :]


def _modulation(c_a, c_b, w_ada, b_ada):
    m = c_a.shape[0] + c_b.shape[0]
    n = w_ada.shape[1]
    bn = 1536
    kb = D_MODEL // MOD_W_SLABS
    slabs = [pl.BlockSpec((kb, bn), lambda j, i=i: (i, j)) for i in range(MOD_W_SLABS)]
    return pl.pallas_call(
        _mod_kernel,
        grid=(n // bn,),
        in_specs=[pl.BlockSpec(c_a.shape, lambda j: (0, 0)), pl.BlockSpec(c_b.shape, lambda j: (0, 0))]
        + slabs + [pl.BlockSpec((1, bn), lambda j: (0, j))],
        out_specs=pl.BlockSpec((m, 1, bn), lambda j: (0, 0, j)),
        out_shape=jax.ShapeDtypeStruct((m, 1, n), f32),
        compiler_params=pltpu.CompilerParams(vmem_limit_bytes=VMEM_LIMIT),
        name="adaln_mod",
    )(c_a, c_b, *([w_ada] * MOD_W_SLABS), b_ada.reshape(1, n))


def _group_rms(x, bd_ref, w_ref):
    ssq = _dot((x * x).astype(bf16), bd_ref[...])
    return x * lax.rsqrt(ssq * (1.0 / HEAD_DIM) + EPS) * w_ref[...]


def _rope(x, cos, sin_signed, low_half):
    partner = jnp.where(low_half, pltpu.roll(x, LANES - 32, axis=1), pltpu.roll(x, 32, axis=1))
    return x * cos + partner * sin_signed


def _dup_heads(x, low_lanes):
    rolled = pltpu.roll(x, HEAD_DIM, axis=1)
    return jnp.where(low_lanes, x, rolled), jnp.where(low_lanes, rolled, x)


def _inproj_kernel(x_ref, sh_ref, sc_ref, n1_ref, win_ref, wgu_ref, bg_ref, qnw_ref, knw_ref,
                   bdq_ref, bdk_ref, cos_ref, sin_ref, *rest, groups, ncast, tail, tail_t):
    slabs, rest = rest[:ncast], rest[ncast:]
    gq_ref, gk_ref, gv_ref, gr_ref, la_ref, sq_ref, k2_ref, v2_ref, kk_ref, vk_ref = rest[:10]
    for src, dst in zip(slabs, rest[10:]):
        dst[...] = src[...].astype(dst.dtype)
    nb, t, d = x_ref.shape
    m = nb * t
    mg = m // groups
    lane = lax.broadcasted_iota(jnp.int32, (mg, LANES), 1)
    low_half = (lane & 32) == 0
    low_lanes = lane < HEAD_DIM
    for grp in range(groups):
        rows = slice(grp * mg, (grp + 1) * mg)
        if nb == 1:
            x = x_ref[:, rows, :]
            sc, sh = sc_ref[...], sh_ref[...]
        else:
            seqs = slice(grp * (nb // groups), (grp + 1) * (nb // groups))
            x = x_ref[seqs]
            sc, sh = sc_ref[seqs], sh_ref[seqs]
        ms = jnp.mean(x * x, axis=-1, keepdims=True)
        hn = x * lax.rsqrt(ms + EPS) * n1_ref[...]
        hn = hn * (1.0 + sc) + sh
        hb = hn.reshape(mg, d).astype(bf16)

        def seg(name):
            a, b = _SEG[name]
            return _dot(hb, win_ref[:, a:b])

        cos = cos_ref[rows, :]
        sin = sin_ref[rows, :]
        sq = _group_rms(seg("sq"), bdq_ref, qnw_ref)
        for c in range(SWA_Q // LANES):
            blk = _rope(sq[:, c * LANES:(c + 1) * LANES], cos, sin, low_half)
            sq_ref[rows, c * LANES:(c + 1) * LANES] = (blk * (LOG2E * HEAD_DIM ** -0.5)).astype(sq_ref.dtype)
        sk = _rope(_group_rms(seg("sk"), bdk_ref, knw_ref), cos, sin, low_half)
        sv = seg("sv")
        keep = (grp + 1) * mg - (m - tail)
        if keep > 0:
            keep = min(keep, mg)
            dst_rows = slice((grp + 1) * mg - keep - (m - tail), (grp + 1) * mg - (m - tail))
            if tail_t:
                assert keep == tail
                kk_ref[...] = sk[mg - keep:].T
                vk_ref[...] = sv[mg - keep:].T
            else:
                kk_ref[dst_rows, :] = sk[mg - keep:]
                vk_ref[dst_rows, :] = sv[mg - keep:]
        for src, dst in ((sk, k2_ref), (sv, v2_ref)):
            d0, d1 = _dup_heads(src, low_lanes)
            dst[rows, :LANES] = d0.astype(dst.dtype)
            dst[rows, LANES:] = d1.astype(dst.dtype)
        glr = seg("glr").astype(bf16)
        g = _dot(glr, wgu_ref[...]) + bg_ref[...]
        log_sig = jnp.minimum(g, 0.0) - jnp.log1p(jnp.exp(-jnp.abs(g)))
        la_ref[rows, :] = log_sig * (LOG2E / GLA_TAU)
        gq_ref[rows, :] = seg("gq") * (GLA_DK ** -0.5)
        gk_ref[rows, :] = seg("gk")
        gv_ref[rows, :] = seg("gv").astype(gv_ref.dtype)
        gr_ref[rows, :] = seg("gr").astype(gr_ref.dtype)


MOD_SHIFT1, MOD_SCALE1, MOD_GATE1, MOD_SHIFT2, MOD_SCALE2, MOD_GATE2 = range(6)


def _mod_spec(nb, row0, chunk, batch_block):
    return pl.BlockSpec((nb, 1, D_MODEL), lambda *ids: (row0 // nb + batch_block(*ids), 0, chunk))


def _inproj(x, mod, row0, n1, win, wgu, bg, qnw, knw, bdq, bdk, cos_t, sin_t, nb, tb, tail, act, groups,
            to_bf16=(), tail_t=False):
    bsz, t, d = x.shape
    m = nb * tb
    nt = t // tb
    grid = (bsz // nb, nt)
    tok = bsz * t
    nsteps = grid[0] * grid[1]
    slab_specs = [pl.BlockSpec((w.shape[0] // nsteps, w.shape[1]), lambda i, j: (i * nt + j, 0))
                  for w in to_bf16]

    def full(shape):
        return pl.BlockSpec(shape, lambda i, j: (0,) * len(shape))

    def out(width):
        return pl.BlockSpec((m, width), lambda i, j: (i * nt + j, 0))

    tail_blk = (SWA_KV, tail) if tail_t else (tail, SWA_KV)
    tail_spec = pl.BlockSpec(tail_blk, lambda i, j: (i, 0))
    tail_shape = jax.ShapeDtypeStruct(((bsz // nb) * tail_blk[0], tail_blk[1]), f32)
    outs = ((GLA_QK, f32), (GLA_QK, f32), (GLA_WIDTH, act), (GLA_WIDTH, act), (GLA_QK, f32),
            (SWA_Q, act), (SWA_KV2, bf16), (SWA_KV2, bf16))
    return pl.pallas_call(
        functools.partial(_inproj_kernel, groups=groups, ncast=len(to_bf16), tail=tail, tail_t=tail_t),
        grid=grid,
        in_specs=[pl.BlockSpec((nb, tb, d), lambda i, j: (i, j, 0)),
                  _mod_spec(nb, row0, MOD_SHIFT1, lambda i, j: i),
                  _mod_spec(nb, row0, MOD_SCALE1, lambda i, j: i),
                  full((1, 1, d)),
                  full((d, IN_WIDTH_PADDED)),
                  full((LANES, GLA_QK)),
                  full((1, GLA_QK)),
                  full((1, SWA_Q)),
                  full((1, SWA_KV)),
                  full((SWA_Q, SWA_Q)),
                  full((SWA_KV, SWA_KV)),
                  pl.BlockSpec((m, LANES), lambda i, j: (j, 0)),
                  pl.BlockSpec((m, LANES), lambda i, j: (j, 0))] + slab_specs,
        out_specs=[out(w) for w, _ in outs] + [tail_spec, tail_spec] + slab_specs,
        out_shape=[jax.ShapeDtypeStruct((tok, w), dt) for w, dt in outs]
        + [tail_shape] * 2
        + [jax.ShapeDtypeStruct(w.shape, bf16) for w in to_bf16],
        compiler_params=pltpu.CompilerParams(
            dimension_semantics=("parallel", "arbitrary"), vmem_limit_bytes=VMEM_LIMIT),
        name="inproj",
    )(x, mod, mod, n1, win, wgu, bg, qnw, knw, bdq, bdk, cos_t, sin_t, *to_bf16)


def _gla_constants(chunk, seq):
    t = np.arange(chunk)
    levels = []
    m = 1
    while m < seq:
        levels.append(m)
        m *= 2
    masks = [np.eye(chunk, dtype=bool)]
    for m in levels:
        upper = (t % (2 * m) >= m)[:, None]
        lower = (t % (2 * m) < m)[None, :]
        same = (t[:, None] // (2 * m)) == (t[None, :] // (2 * m))
        masks.append(same & upper & lower)
    tri = t[None, :] <= t[:, None]
    tiled = np.tile(np.stack(masks).astype(np.float32), (1, 1, GLA_HEADS))
    return tuple(levels), jnp.asarray(tri.astype(np.float32), bf16), jnp.asarray(tiled)


def _stack_heads(xb, lane_head, axis):
    zero = jnp.zeros_like(xb)
    return jnp.concatenate([jnp.where(lane_head == h, xb, zero) for h in range(GLA_HEADS)], axis=axis)


def _block_sums(la, bcum, m, row, rolls):
    c, n = la.shape
    if m == 1:
        return la, None
    if m < 8:
        def rolled(shift):
            if shift not in rolls:
                rolls[shift] = pltpu.roll(la, shift % c, axis=0)
            return rolls[shift]
        pos = row & (m - 1)
        pre = la
        suf = None
        for j in range(1, m):
            pre = pre + jnp.where(pos >= j, rolled(j), 0.0)
            term = jnp.where(pos < m - j, rolled(-j), 0.0)
            suf = term if suf is None else suf + term
        return pre, suf
    before, last = [], []
    for i in range(c // m):
        before.append(jnp.zeros((m, n), f32) if i == 0
                      else jnp.broadcast_to(bcum[i * m - 1:i * m], (m, n)))
        last.append(jnp.broadcast_to(bcum[(i + 1) * m - 1:(i + 1) * m], (m, n)))
    if len(before) == 1:
        return bcum - before[0], last[0] - bcum
    return bcum - jnp.concatenate(before, axis=0), jnp.concatenate(last, axis=0) - bcum


def _gla_scores(q_ref, k_ref, v_ref, la_ref, tri_ref, masks_ref, levels, seq):
    c = q_ref.shape[0]
    la = la_ref[...]
    hi = la.astype(bf16)
    lo = (la - hi.astype(f32)).astype(bf16)
    tri = tri_ref[...]
    bcum = _dot(tri, hi) + _dot(tri, lo)
    q = q_ref[...]
    k = k_ref[...]
    vb = v_ref[...].astype(bf16)
    row = lax.broadcasted_iota(jnp.int32, (c, GLA_QK), 0)
    row_head_t = lax.broadcasted_iota(jnp.int32, (GLA_QK, c), 0) >> 6
    rolls = {}
    attn = None
    for lvl, m in enumerate((0,) + levels):
        if m == 0:
            qt, kt = q, k
        else:
            pre, suf = _block_sums(la, bcum, m, row, rolls)
            qt = q * jnp.exp2(pre)
            kt = k if suf is None else k * jnp.exp2(suf)
        kt_t = kt.T.astype(bf16)
        r = _dot(qt.astype(bf16), _stack_heads(kt_t, row_head_t, 1))
        rm = r * masks_ref[lvl]
        attn = rm if attn is None else attn + rm
    a = attn.astype(bf16)
    pre, suf = _block_sums(la, bcum, seq, row, rolls)
    qf = (q * jnp.exp2(pre)).astype(bf16)
    kf = (k * jnp.exp2(suf)).astype(bf16)
    return a, qf, kf, vb, bcum, hi, lo


def _gla_values(a, vb):
    v_head = lax.broadcasted_iota(jnp.int32, vb.shape, 1) >> 7
    return _dot(a, _stack_heads(vb, v_head, 0))


def _gla_finish(o, r_ref, gnw, go_ref):
    for h in range(GLA_HEADS):
        sl = slice(h * GLA_DV, (h + 1) * GLA_DV)
        oh = o[:, sl]
        r = r_ref[:, sl].astype(f32)
        ms = jnp.mean(oh * oh, axis=-1, keepdims=True)
        go_ref[:, sl] = (oh * lax.rsqrt(ms + EPS) * gnw * (r * _sigmoid(r))).astype(go_ref.dtype)


def _gla_prompt_block(st, q_ref, k_ref, v_ref, la_ref, r_ref, tri_ref, masks_ref, gnw, go_ref, levels,
                      between=lambda: None):
    c = PROMPT_GLA_CHUNK
    nchunk = q_ref.shape[0] // c
    lane_head = lax.broadcasted_iota(jnp.int32, (GLA_DV, GLA_QK), 1) >> 6
    row_head = lax.broadcasted_iota(jnp.int32, (GLA_QK, GLA_DV), 0) >> 6
    chunks = []
    for ci in range(nchunk):
        rows = pl.ds(ci * c, c)
        chunks.append(_gla_scores(q_ref.at[rows], k_ref.at[rows], v_ref.at[rows], la_ref.at[rows],
                                  tri_ref, masks_ref, levels, c))
        between()
    intra, upds, decays = [], [], []
    for a, _, kf, vb, bcum, _, _ in chunks:
        intra.append(_gla_values(a, vb))
        full = _dot_tn(vb, kf)
        upd = None
        for h in range(GLA_HEADS):
            term = jnp.where(lane_head == h, full[h * GLA_DV:(h + 1) * GLA_DV], 0.0)
            upd = term if upd is None else upd + term
        upds.append(upd)
        decays.append(jnp.exp2(bcum[c - 1:c]))
    between()
    for ci in range(nchunk):
        rows = pl.ds(ci * c, c)
        sbd = _stack_heads(st.T.astype(bf16), row_head, 1)
        o = intra[ci] + _dot(chunks[ci][1], sbd)
        _gla_finish(o, r_ref.at[rows], gnw, go_ref.at[rows])
        st = decays[ci] * st + upds[ci]
    return st


def _gla_sample_kernel(q_ref, k_ref, v_ref, la_ref, r_ref, s0_ref, tri_ref, masks_ref, gnw_ref,
                       go_ref, sout_ref, *, levels, seq):
    c = q_ref.shape[0]
    a, qf, kf, vb, _, hi, lo = _gla_scores(q_ref, k_ref, v_ref, la_ref, tri_ref, masks_ref, levels, seq)
    o_intra = _gla_values(a, vb)
    row_head = lax.broadcasted_iota(jnp.int32, (GLA_QK, GLA_DV), 0) >> 6
    ones = jnp.ones((seq, GLA_DV), bf16)
    inter = []
    for b in range(c // seq):
        rows = slice(b * seq, (b + 1) * seq)
        s_old = s0_ref[b]
        sbd = _stack_heads(s_old.astype(bf16), row_head, 1)
        inter.append(_dot(qf[rows], sbd))
        p = _dot_tn(kf[rows], vb[rows])
        upd = None
        for h in range(GLA_HEADS):
            term = jnp.where(row_head == h, p[:, h * GLA_DV:(h + 1) * GLA_DV], 0.0)
            upd = term if upd is None else upd + term
        total = _dot_tn(hi[rows], ones) + _dot_tn(lo[rows], ones)
        sout_ref[b] = jnp.exp2(total) * s_old + upd
    o = o_intra + jnp.concatenate(inter, axis=0)
    _gla_finish(o, r_ref, gnw_ref[...], go_ref)


def _gla_sample(gq, gk, gv, la, gr, state, gnw, bsz, t):
    c = GLA_CHUNK
    nseq = c // t
    levels, tri, masks = _gla_constants(c, t)

    def tok(width):
        return pl.BlockSpec((c, width), lambda i: (i, 0))

    return pl.pallas_call(
        functools.partial(_gla_sample_kernel, levels=levels, seq=t),
        grid=(bsz // nseq,),
        in_specs=[tok(GLA_QK), tok(GLA_QK), tok(GLA_WIDTH), tok(GLA_QK), tok(GLA_WIDTH),
                  pl.BlockSpec((nseq, GLA_QK, GLA_DV), lambda i: (i, 0, 0)),
                  pl.BlockSpec(tri.shape, lambda i: (0, 0)),
                  pl.BlockSpec(masks.shape, lambda i: (0, 0, 0)),
                  pl.BlockSpec((1, GLA_DV), lambda i: (0, 0))],
        out_specs=[tok(GLA_WIDTH),
                   pl.BlockSpec((nseq, GLA_QK, GLA_DV), lambda i: (i, 0, 0))],
        out_shape=[jax.ShapeDtypeStruct((bsz * t, GLA_WIDTH), f32),
                   jax.ShapeDtypeStruct((bsz, GLA_QK, GLA_DV), f32)],
        compiler_params=pltpu.CompilerParams(
            dimension_semantics=("parallel",), vmem_limit_bytes=VMEM_LIMIT),
        name="gla_sample",
    )(gq, gk, gv, la, gr, state, tri, masks, gnw)


def _sink_attention(q_ref, rows, kdups, vdups, mask, sink_ref, o_ref, between=lambda: None):
    n = len(kdups)
    heads_per_group = N_Q_HEADS // N_KV_HEADS
    low_q = lax.broadcasted_iota(jnp.int32, (rows, LANES), 1) < HEAD_DIM
    scores = []
    for i in range(n):
        per_group = []
        for g in range(N_KV_HEADS):
            stack = []
            for p in range(heads_per_group // 2):
                pair = g * (heads_per_group // 2) + p
                qp = q_ref[i * rows:(i + 1) * rows, pair * LANES:(pair + 1) * LANES].astype(bf16)
                zero = jnp.zeros_like(qp)
                stack.append(jnp.where(low_q, qp, zero))
                stack.append(jnp.where(low_q, zero, qp))
            per_group.append(_dot_nt(jnp.concatenate(stack, axis=0), kdups[i][g]))
        scores.append(per_group)
    between()
    probs, inv = [], []
    for head in range(N_Q_HEADS):
        g, hh = divmod(head, heads_per_group)
        parts = [scores[i][g][hh * rows:(hh + 1) * rows] for i in range(n)]
        s = parts[0] if n == 1 else jnp.concatenate(parts, axis=0)
        sink = sink_ref[head] * LOG2E
        s = jnp.where(mask, s, -1e30)
        mx = jnp.maximum(jnp.max(s, axis=-1, keepdims=True), sink)
        p = jnp.exp2(s - mx)
        inv.append(1.0 / (jnp.sum(p, axis=-1, keepdims=True) + jnp.exp2(sink - mx)))
        probs.append(p.astype(bf16))
    between()
    outs = []
    for i in range(n):
        per_group = []
        for g in range(N_KV_HEADS):
            p_i = jnp.concatenate([probs[g * heads_per_group + hh][i * rows:(i + 1) * rows]
                                   for hh in range(heads_per_group)], axis=0)
            per_group.append(_dot(p_i, vdups[i][g]))
        outs.append(per_group)
    low_all = lax.broadcasted_iota(jnp.int32, (n * rows, LANES), 1) < HEAD_DIM
    for pair in range(N_Q_HEADS // 2):
        halves = []
        for head in (2 * pair, 2 * pair + 1):
            g, hh = divmod(head, heads_per_group)
            parts = [outs[i][g][hh * rows:(hh + 1) * rows] for i in range(n)]
            o = parts[0] if n == 1 else jnp.concatenate(parts, axis=0)
            halves.append(o * inv[head])
        o_ref[:, pair * LANES:(pair + 1) * LANES] = jnp.where(low_all, halves[0], halves[1]).astype(o_ref.dtype)


def _swa_prompt_blocks(q_ref, k_before, k2_ref, v_before, v2_ref, first, sink_ref, o_ref,
                       between=lambda: None):
    w = WINDOW
    nblk = q_ref.shape[0] // w

    def dup(before, ref):
        blocks = [before] + [ref[i * w:(i + 1) * w, :] for i in range(nblk)]
        return [[jnp.concatenate([blocks[i][:, g * LANES:(g + 1) * LANES],
                                  blocks[i + 1][:, g * LANES:(g + 1) * LANES]], axis=0)
                 for g in range(N_KV_HEADS)] for i in range(nblk)]

    row = lax.broadcasted_iota(jnp.int32, (nblk * w, 2 * w), 0)
    tk = lax.broadcasted_iota(jnp.int32, (nblk * w, 2 * w), 1)
    rel = tk - (row & (w - 1))
    first_key = jnp.where(row < w, jnp.where(first, w, 0), 0)
    mask = (rel > 0) & (rel <= w) & (tk >= first_key)
    _sink_attention(q_ref, w, dup(k_before, k2_ref), dup(v_before, v2_ref), mask, sink_ref, o_ref, between)


def _swa_sample_kernel(sink_ref, q_ref, kn_ref, vn_ref, pk_ref, pv_ref, o_ref, ko_ref, vo_ref, *, seq):
    nseq, _, _, w = pk_ref.shape
    rows = nseq * seq
    heads_per_group = N_Q_HEADS // N_KV_HEADS
    lane = lax.broadcasted_iota(jnp.int32, (rows, LANES), 1)
    r_id = lax.broadcasted_iota(jnp.int32, (rows, LANES), 0)
    low = lane < HEAD_DIM
    pos = r_id & (seq - 1)
    mask_old = lane > pos
    seq_shift = seq.bit_length() - 1
    mask_new = ((r_id >> seq_shift) == (lane >> seq_shift)) & ((lane & (seq - 1)) <= pos)
    kn = kn_ref[...]
    vn = vn_ref[...]
    kn_dup = [d.astype(bf16) for d in _dup_heads(kn, low)]
    vn_dup = [d.astype(bf16) for d in _dup_heads(vn, low)]

    pad = jnp.zeros((w - seq, LANES), f32)
    tail_lanes = lax.broadcasted_iota(jnp.int32, (HEAD_DIM, w), 1) >= w - seq
    kt_dup, vt_dup = [], []
    for b in range(nseq):
        tok = slice(b * seq, (b + 1) * seq)
        per_k, per_v = [], []
        for new, old_ref, out_ref, per in ((kn, pk_ref, ko_ref, per_k), (vn, pv_ref, vo_ref, per_v)):
            new_t = jnp.concatenate([pad, new[tok]], axis=0).T
            for g in range(N_KV_HEADS):
                old = old_ref[b, g]
                out_ref[b, g] = jnp.where(tail_lanes, new_t[g * HEAD_DIM:(g + 1) * HEAD_DIM],
                                          pltpu.roll(old, w - seq, axis=1))
                ob = old.astype(bf16)
                per.append(jnp.concatenate([ob, ob], axis=0))
        kt_dup.append(per_k)
        vt_dup.append(per_v)

    lhs = []
    for g in range(N_KV_HEADS):
        stack = []
        for p in range(heads_per_group // 2):
            pair = g * (heads_per_group // 2) + p
            qp = q_ref[:, pair * LANES:(pair + 1) * LANES].astype(bf16)
            zero = jnp.zeros_like(qp)
            stack.append(jnp.where(low, qp, zero))
            stack.append(jnp.where(low, zero, qp))
        lhs.append(jnp.concatenate(stack, axis=0))
    s_new = [_dot_nt(lhs[g], kn_dup[g]) for g in range(N_KV_HEADS)]
    s_old = []
    for b in range(nseq):
        per = []
        for g in range(N_KV_HEADS):
            qb = jnp.concatenate([lhs[g][hh * rows + b * seq:hh * rows + (b + 1) * seq]
                                  for hh in range(heads_per_group)], axis=0)
            per.append(_dot(qb, kt_dup[b][g]))
        s_old.append(per)

    p_old, p_new, inv = [], [], []
    for head in range(N_Q_HEADS):
        g, hh = divmod(head, heads_per_group)
        so = jnp.concatenate([s_old[b][g][hh * seq:(hh + 1) * seq] for b in range(nseq)], axis=0)
        so = jnp.where(mask_old, so, -1e30)
        sn = jnp.where(mask_new, s_new[g][hh * rows:(hh + 1) * rows], -1e30)
        sink = sink_ref[head] * LOG2E
        mx = jnp.maximum(jnp.maximum(jnp.max(so, axis=-1, keepdims=True),
                                     jnp.max(sn, axis=-1, keepdims=True)), sink)
        po = jnp.exp2(so - mx)
        pn = jnp.exp2(sn - mx)
        inv.append(1.0 / (jnp.sum(po, axis=-1, keepdims=True) + jnp.sum(pn, axis=-1, keepdims=True)
                          + jnp.exp2(sink - mx)))
        p_old.append(po.astype(bf16))
        p_new.append(pn.astype(bf16))

    o_new = [_dot(jnp.concatenate([p_new[g * heads_per_group + hh] for hh in range(heads_per_group)], axis=0),
                  vn_dup[g]) for g in range(N_KV_HEADS)]
    o_old = []
    for b in range(nseq):
        per = []
        for g in range(N_KV_HEADS):
            pb = jnp.concatenate([p_old[g * heads_per_group + hh][b * seq:(b + 1) * seq]
                                  for hh in range(heads_per_group)], axis=0)
            per.append(_dot_nt(pb, vt_dup[b][g]))
        o_old.append(per)

    for pair in range(N_Q_HEADS // 2):
        halves = []
        for head in (2 * pair, 2 * pair + 1):
            g, hh = divmod(head, heads_per_group)
            old = jnp.concatenate([o_old[b][g][hh * seq:(hh + 1) * seq] for b in range(nseq)], axis=0)
            halves.append((old + o_new[g][hh * rows:(hh + 1) * rows]) * inv[head])
        o_ref[:, pair * LANES:(pair + 1) * LANES] = jnp.where(low, halves[0], halves[1])


def _swa_sample(sq, sk, sv, past_kt, past_vt, sinks, bsz, t):
    nseq = SAMPLE_ROWS // 2 // t
    assert nseq * t == LANES
    tok = lambda width: pl.BlockSpec((nseq * t, width), lambda i, s: (i, 0))
    cache = pl.BlockSpec((nseq,) + past_kt.shape[1:], lambda i, s: (i, 0, 0, 0))
    return pl.pallas_call(
        functools.partial(_swa_sample_kernel, seq=t),
        grid_spec=pltpu.PrefetchScalarGridSpec(
            num_scalar_prefetch=1,
            grid=(bsz // nseq,),
            in_specs=[tok(SWA_Q), tok(SWA_KV), tok(SWA_KV), cache, cache],
            out_specs=[tok(SWA_Q), cache, cache]),
        out_shape=[jax.ShapeDtypeStruct((bsz * t, SWA_Q), f32),
                   jax.ShapeDtypeStruct(past_kt.shape, f32),
                   jax.ShapeDtypeStruct(past_vt.shape, f32)],
        compiler_params=pltpu.CompilerParams(
            dimension_semantics=("parallel",), vmem_limit_bytes=VMEM_LIMIT),
        name="swa_sample",
    )(sinks, sq, sk, sv, past_kt, past_vt)


FFN_CHUNK = 512


def _out_proj_norm(streams, n2, wout_ref):
    mix = [s[1] for s in streams]
    mix = mix[0] if len(mix) == 1 else jnp.concatenate(mix, axis=0)
    mixed = _dot(mix[:, :GLA_WIDTH], wout_ref[:GLA_WIDTH, :]) + _dot(mix[:, GLA_WIDTH:], wout_ref[GLA_WIDTH:, :])
    hs, hbs, r0 = [], [], 0
    for x, _, g1, sh2, sc2 in streams:
        nb, t, d = x.shape
        h = x + g1 * mixed[r0:r0 + nb * t].reshape(nb, t, d)
        ms = jnp.mean(h * h, axis=-1, keepdims=True)
        hn = h * lax.rsqrt(ms + EPS) * n2
        hn = hn * (1.0 + sc2) + sh2
        hs.append(h)
        hbs.append(hn.reshape(nb * t, d).astype(bf16))
        r0 += nb * t
    return hs, (hbs[0] if len(hbs) == 1 else jnp.concatenate(hbs, axis=0))


def _ffn_piece(hb, w1_ref, w2_ref, c, width):
    cols = slice(c * width, (c + 1) * width)
    a = jnp.maximum(_dot(hb, w1_ref[:, cols]), 0.0)
    return _dot((a * a).astype(bf16), w2_ref[cols, :])


def _resident(shape):
    return pl.BlockSpec(shape, lambda *_: (0,) * len(shape), pipeline_mode=pl.Buffered(1))


def _prompt_kernel(sink_ref,
                   q_ref, k_ref, v_ref, la_ref, r_ref, sq_ref, k2_ref, v2_ref, k2p_ref, v2p_ref,
                   tri_ref, masks_ref, gnw_ref,
                   x_ref, g1_ref, sh2_ref, sc2_ref, g2_ref,
                   xs_ref, gos_ref, sos_ref, g1s_ref, sh2s_ref, sc2s_ref, g2s_ref,
                   n2_ref, wout_ref, w1_ref, w2_ref,
                   y_ref, ys_ref, sout_ref, state_ref, mix_ref, *, levels, nt, nsteps):
    s = pl.program_id(0)
    slot = s % 2

    @pl.when(s == 0)
    def _():
        mix_ref[1] = jnp.zeros(mix_ref.shape[1:], mix_ref.dtype)
        state_ref[...] = jnp.zeros_like(state_ref)

    j = jnp.minimum(s, nsteps - 1) % nt
    st_in = state_ref[...]
    st = jnp.where(j == 0, 0.0, st_in)
    gnw = gnw_ref[...]
    out = mix_ref.at[slot]
    mix_s = jnp.concatenate([gos_ref[...], sos_ref[...]], axis=1).astype(bf16)
    (h, h_s), hb = _out_proj_norm(
        [(x_ref[...], mix_ref[1 - slot], g1_ref[...], sh2_ref[...], sc2_ref[...]),
         (xs_ref[...], mix_s, g1s_ref[...], sh2s_ref[...], sc2s_ref[...])], n2_ref[...], wout_ref)
    rows_p = h.shape[0] * h.shape[1]
    todo = list(range(D_FF // FFN_CHUNK))
    ff = []

    def ffn_piece():
        if todo:
            part = _ffn_piece(hb, w1_ref, w2_ref, todo.pop(0), FFN_CHUNK)
            ff[:] = [part if not ff else ff[0] + part]

    st = _gla_prompt_block(st, q_ref, k_ref, v_ref, la_ref, r_ref, tri_ref, masks_ref, gnw,
                           out.at[:, pl.ds(0, GLA_WIDTH)], levels, ffn_piece)
    _swa_prompt_blocks(sq_ref, k2p_ref[...], k2_ref, v2p_ref[...], v2_ref, j == 0, sink_ref,
                       out.at[:, pl.ds(GLA_WIDTH, SWA_Q)], ffn_piece)
    while todo:
        ffn_piece()
    st = jnp.where(s < nsteps, st, st_in)
    state_ref[...] = st
    sout_ref[0] = st.T
    y_ref[...] = h + g2_ref[...] * ff[0][:rows_p].reshape(h.shape)
    ys_ref[...] = h_s + g2s_ref[...] * ff[0][rows_p:].reshape(h_s.shape)


def _prompt_mix_ffn(x, gq, gk, gv, la, gr, sq, k2, v2, gnw, sinks, mod, row0, n2, wout, w1, w2,
                    xs, gos, sos, row0_s):
    bsz, t, d = x.shape
    tb = FUSED_TOKEN_BLOCK
    nt = t // tb
    nsteps = bsz * nt
    levels, tri, masks = _gla_constants(PROMPT_GLA_CHUNK, PROMPT_GLA_CHUNK)
    per_blk = tb // WINDOW
    bs, ts, _ = xs.shape
    nbs = bs // nsteps
    assert nbs * nsteps == bs and (nbs * ts) % 16 == 0

    def mix_blk(s):
        return jnp.minimum(s, nsteps - 1)

    def ffn_blk(s):
        return jnp.maximum(s - 1, 0)

    def tok(width):
        return pl.BlockSpec((tb, width), lambda s, _: (mix_blk(s), 0))

    prev = pl.BlockSpec((WINDOW, SWA_KV2), lambda s, _: (jnp.maximum(mix_blk(s) * per_blk - 1, 0), 0))
    xspec = pl.BlockSpec((1, tb, d), lambda s, _: (ffn_blk(s) // nt, ffn_blk(s) % nt, 0))
    ffn_chunks = (MOD_GATE1, MOD_SHIFT2, MOD_SCALE2, MOD_GATE2)
    mods = [_mod_spec(1, row0, chunk, lambda s, _: ffn_blk(s) // nt) for chunk in ffn_chunks]
    mods_s = [_mod_spec(nbs, row0_s, chunk, lambda s, _: ffn_blk(s)) for chunk in ffn_chunks]
    xs_spec = pl.BlockSpec((nbs, ts, d), lambda s, _: (ffn_blk(s), 0, 0))
    mix_s_spec = pl.BlockSpec((nbs * ts, GLA_WIDTH), lambda s, _: (ffn_blk(s), 0))
    return pl.pallas_call(
        functools.partial(_prompt_kernel, levels=levels, nt=nt, nsteps=nsteps),
        grid_spec=pltpu.PrefetchScalarGridSpec(
            num_scalar_prefetch=1,
            grid=(nsteps + 1,),
            in_specs=[tok(GLA_QK), tok(GLA_QK), tok(GLA_WIDTH), tok(GLA_QK), tok(GLA_WIDTH),
                      tok(SWA_Q), tok(SWA_KV2), tok(SWA_KV2), prev, prev,
                      _resident(tri.shape), _resident(masks.shape), _resident((1, GLA_DV)),
                      xspec, *mods, xs_spec, mix_s_spec, mix_s_spec, *mods_s, _resident((1, 1, d)),
                      _resident((d, d)), _resident((d, D_FF)), _resident((D_FF, d))],
            out_specs=[xspec, xs_spec,
                       pl.BlockSpec((1, GLA_QK, GLA_DV), lambda s, _: (mix_blk(s) // nt, 0, 0))],
            scratch_shapes=[pltpu.VMEM((GLA_DV, GLA_QK), f32),
                            pltpu.VMEM((2, tb, GLA_WIDTH + SWA_Q), bf16)]),
        out_shape=[jax.ShapeDtypeStruct((bsz, t, d), f32),
                   jax.ShapeDtypeStruct(xs.shape, f32),
                   jax.ShapeDtypeStruct((bsz, GLA_QK, GLA_DV), f32)],
        compiler_params=pltpu.CompilerParams(
            dimension_semantics=("arbitrary",), vmem_limit_bytes=VMEM_LIMIT),
        name="prompt_mix_ffn",
    )(sinks, gq, gk, gv, la, gr, sq, k2, v2, k2, v2, tri, masks, gnw,
      x, mod, mod, mod, mod, xs, gos, sos, mod, mod, mod, mod, n2, wout, w1, w2)


def _rope_tables(pos, copies):
    half = HEAD_DIM // 2
    inv = np.power(np.float32(ROPE_THETA), -np.arange(half, dtype=np.float32) * np.float32(2.0 / HEAD_DIM))
    ang = np.asarray(pos, np.float32)[:, None] * inv[None, :].astype(np.float32)
    cos = np.cos(ang).astype(np.float32)
    sin = np.sin(ang).astype(np.float32)
    reps = (copies, LANES // HEAD_DIM)
    return (jnp.asarray(np.tile(np.concatenate([cos, cos], axis=-1), reps)),
            jnp.asarray(np.tile(np.concatenate([-sin, sin], axis=-1), reps)))


def _block_diag_ones(n, blk):
    idx = np.arange(n) // blk
    return jnp.asarray((idx[:, None] == idx[None, :]).astype(np.float32), bf16)


def _layer_weights(w_in, w_gate_up, b_gate, q_norm_w, k_norm_w, w_out, w_ff1, w_ff2):
    splits = np.cumsum([GLA_QK, GLA_QK, GLA_WIDTH, GLA_WIDTH, GLA_GATE_RANK, SWA_Q, SWA_KV])
    gq, gk, gv, gr, glr, sq, sk, sv = jnp.split(w_in, [int(s) for s in splits], axis=1)
    glr = jnp.pad(glr, ((0, 0), (0, LANES - GLA_GATE_RANK)))
    win = jnp.concatenate([gq, gk, gv, gr, sq, sk, sv, glr], axis=1).astype(bf16)
    wgu = jnp.pad(w_gate_up, ((0, LANES - GLA_GATE_RANK), (0, 0))).astype(bf16)
    return dict(
        win=win, wgu=wgu, bg=b_gate.reshape(1, GLA_QK),
        qnw=jnp.tile(q_norm_w, N_Q_HEADS).reshape(1, SWA_Q),
        knw=jnp.tile(k_norm_w, N_KV_HEADS).reshape(1, SWA_KV),
        ffn_f32=(w_out, w_ff1, w_ff2))


def _project(x, mod, row0, pos, lw, n1, bdq, bdk, prompt):
    bsz, t, d = x.shape
    if prompt:
        nb, tb, tail, act, groups, cast = 1, TOKEN_BLOCK, WINDOW, bf16, PROMPT_INPROJ_GROUPS, lw["ffn_f32"]
    else:
        nb, tb, tail, act, groups, cast = SAMPLE_ROWS // t, t, SAMPLE_ROWS, f32, 1, ()
    cos_t, sin_t = _rope_tables(pos, nb)
    return _inproj(x, mod, row0, n1.reshape(1, 1, d), lw["win"], lw["wgu"], lw["bg"], lw["qnw"], lw["knw"],
                   bdq, bdk, cos_t, sin_t, nb, tb, tail, act, groups, cast, tail_t=prompt)


def _decoder_layers(xp, xs, mod, pos_p, pos_s, state, past_k, past_v, lw, n1, n2, gnw, sinks, bdq, bdk):
    bp, tp, d = xp.shape
    bs, ts, _ = xs.shape
    heads = (N_KV_HEADS, HEAD_DIM)
    gq, gk, gv, gr, la, sq, _, _, kk, vk = _project(xs, mod, 0, pos_s, lw, n1, bdq, bdk, False)
    go_s, state_s = _gla_sample(gq, gk, gv, la, gr, state.reshape(bs, GLA_QK, GLA_DV), gnw, bs, ts)
    so_s, kt, vt = _swa_sample(sq, kk, vk, past_k.transpose(0, 2, 3, 1), past_v.transpose(0, 2, 3, 1),
                               sinks, bs, ts)
    gq, gk, gv, gr, la, sq, k2, v2, kk, vk, *ffn_w = _project(xp, mod, bs, pos_p, lw, n1, bdq, bdk, True)
    yp, ys, s_t = _prompt_mix_ffn(xp, gq, gk, gv, la, gr, sq, k2, v2, gnw, sinks, mod, bs,
                                  n2.reshape(1, 1, d), *ffn_w, xs, go_s, so_s, 0)
    state_p = s_t.reshape(bp, GLA_HEADS, GLA_DK, GLA_DV)
    return (yp, ys, state_p,
            kk.reshape(bp, *heads, WINDOW).transpose(0, 3, 1, 2), vk.reshape(bp, *heads, WINDOW).transpose(0, 3, 1, 2),
            state_s.reshape(bs, GLA_HEADS, GLA_DK, GLA_DV),
            kt.transpose(0, 3, 1, 2), vt.transpose(0, 3, 1, 2))


def kernel(x_prompt, x_sample, state_gla, cache_swa_k, cache_swa_v, c_prompt, c_sample, w_ada, b_ada, norm1_w, norm2_w, w_in, w_gate_up, b_gate, gla_norm_w, q_norm_w, k_norm_w, sinks, w_out, w_ff1, w_ff2):
    depth = w_ada.shape[0]
    bp, tp, _ = x_prompt.shape
    bs, ts, _ = x_sample.shape
    pos_p = np.arange(tp)
    pos_s = PAST_LEN + np.arange(ts)
    bdq = _block_diag_ones(SWA_Q, HEAD_DIM)
    bdk = _block_diag_ones(SWA_KV, HEAD_DIM)
    yp, ys = x_prompt, x_sample
    outs = [[] for _ in range(6)]
    for l in range(depth):
        mod = _modulation(c_sample, c_prompt, w_ada[l], b_ada[l])
        lw = _layer_weights(w_in[l], w_gate_up[l], b_gate[l], q_norm_w[l], k_norm_w[l],
                            w_out[l], w_ff1[l], w_ff2[l])
        gnw = gla_norm_w[l].reshape(1, GLA_DV)
        yp, ys, *new = _decoder_layers(yp, ys, mod, pos_p, pos_s, state_gla[l], cache_swa_k[l],
                                       cache_swa_v[l], lw, norm1_w[l], norm2_w[l], gnw, sinks[l], bdq, bdk)
        for lst, val in zip(outs, new):
            lst.append(val)
    return (yp, ys) + tuple(jnp.stack(o) for o in outs)
```

```python
import functools

import jax
import jax.numpy as jnp
import numpy as np
from jax import lax
from jax.experimental import pallas as pl
from jax.experimental.pallas import tpu as pltpu

f32 = jnp.float32
bf16 = jnp.bfloat16

D_MODEL = 1024
GLA_HEADS = 4
GLA_DK = 64
GLA_DV = 128
GLA_QK = GLA_HEADS * GLA_DK
GLA_WIDTH = GLA_HEADS * GLA_DV
GLA_GATE_RANK = 16
GLA_TAU = 16.0
LOG2E = 1.4426950408889634
HEAD_DIM = 64
N_Q_HEADS = 8
N_KV_HEADS = 2
SWA_Q = N_Q_HEADS * HEAD_DIM
SWA_KV = N_KV_HEADS * HEAD_DIM
SWA_KV2 = 2 * SWA_KV
WINDOW = 128
ROPE_THETA = 10000.0
PAST_LEN = 8192
D_FF = 4 * D_MODEL
EPS = 1e-6
LANES = 128
GLA_CHUNK = 128
PROMPT_GLA_CHUNK = 128
TOKEN_BLOCK = 1024
FUSED_TOKEN_BLOCK = 512
SAMPLE_ROWS = 256
PROMPT_INPROJ_GROUPS = 2
VMEM_LIMIT = 56 * 1024 * 1024

_SEG = {}
_off = 0
for _name, _w in (("gq", GLA_QK), ("gk", GLA_QK), ("gv", GLA_WIDTH), ("gr", GLA_WIDTH),
                  ("sq", SWA_Q), ("sk", SWA_KV), ("sv", SWA_KV), ("glr", LANES)):
    _SEG[_name] = (_off, _off + _w)
    _off += _w
IN_WIDTH_PADDED = _off


def _dot(a, b):
    return jnp.dot(a, b, preferred_element_type=f32)


def _dot_nt(a, b):
    return lax.dot_general(a, b, (((1,), (1,)), ((), ())), preferred_element_type=f32)


def _dot_tn(a, b):
    return lax.dot_general(a, b, (((0,), (0,)), ((), ())), preferred_element_type=f32)


def _sigmoid(x):
    return 1.0 / (1.0 + jnp.exp(-x))


MOD_W_SLABS = 4


def _mod_kernel(ca_ref, cb_ref, *refs):
    w_refs, b_ref, o_ref = refs[:-2], refs[-2], refs[-1]
    c = jnp.concatenate([ca_ref[...], cb_ref[...]], axis=0)
    s = (c * _sigmoid(c)).astype(bf16)
    kb = w_refs[0].shape[0]
    res = b_ref[...]
    for i, w_ref in enumerate(w_refs):
        res = res + _dot(s[:, i * kb:(i + 1) * kb], w_ref[...].astype(bf16))
    for r in range(res.shape[0]):
        o_ref[r] = res[r:r + 1, :]


def _modulation(c_a, c_b, w_ada, b_ada):
    m = c_a.shape[0] + c_b.shape[0]
    n = w_ada.shape[1]
    bn = 1536
    kb = D_MODEL // MOD_W_SLABS
    slabs = [pl.BlockSpec((kb, bn), lambda j, i=i: (i, j)) for i in range(MOD_W_SLABS)]
    return pl.pallas_call(
        _mod_kernel,
        grid=(n // bn,),
        in_specs=[pl.BlockSpec(c_a.shape, lambda j: (0, 0)), pl.BlockSpec(c_b.shape, lambda j: (0, 0))]
        + slabs + [pl.BlockSpec((1, bn), lambda j: (0, j))],
        out_specs=pl.BlockSpec((m, 1, bn), lambda j: (0, 0, j)),
        out_shape=jax.ShapeDtypeStruct((m, 1, n), f32),
        compiler_params=pltpu.CompilerParams(vmem_limit_bytes=VMEM_LIMIT),
        name="adaln_mod",
    )(c_a, c_b, *([w_ada] * MOD_W_SLABS), b_ada.reshape(1, n))


def _group_rms(x, bd_ref, w_ref):
    ssq = _dot((x * x).astype(bf16), bd_ref[...])
    return x * lax.rsqrt(ssq * (1.0 / HEAD_DIM) + EPS) * w_ref[...]


def _rope(x, cos, sin_signed, low_half):
    partner = jnp.where(low_half, pltpu.roll(x, LANES - 32, axis=1), pltpu.roll(x, 32, axis=1))
    return x * cos + partner * sin_signed


def _dup_heads(x, low_lanes):
    rolled = pltpu.roll(x, HEAD_DIM, axis=1)
    return jnp.where(low_lanes, x, rolled), jnp.where(low_lanes, rolled, x)


def _inproj_kernel(x_ref, sh_ref, sc_ref, n1_ref, win_ref, wgu_ref, bg_ref, qnw_ref, knw_ref,
                   bdq_ref, bdk_ref, cos_ref, sin_ref, *rest, groups, ncast, tail, tail_t):
    slabs, rest = rest[:ncast], rest[ncast:]
    gq_ref, gk_ref, gv_ref, gr_ref, la_ref, sq_ref, k2_ref, v2_ref, kk_ref, vk_ref = rest[:10]
    for src, dst in zip(slabs, rest[10:]):
        dst[...] = src[...].astype(dst.dtype)
    nb, t, d = x_ref.shape
    m = nb * t
    mg = m // groups
    lane = lax.broadcasted_iota(jnp.int32, (mg, LANES), 1)
    low_half = (lane & 32) == 0
    low_lanes = lane < HEAD_DIM
    for grp in range(groups):
        rows = slice(grp * mg, (grp + 1) * mg)
        if nb == 1:
            x = x_ref[:, rows, :]
            sc, sh = sc_ref[...], sh_ref[...]
        else:
            seqs = slice(grp * (nb // groups), (grp + 1) * (nb // groups))
            x = x_ref[seqs]
            sc, sh = sc_ref[seqs], sh_ref[seqs]
        ms = jnp.mean(x * x, axis=-1, keepdims=True)
        hn = x * lax.rsqrt(ms + EPS) * n1_ref[...]
        hn = hn * (1.0 + sc) + sh
        hb = hn.reshape(mg, d).astype(bf16)

        def seg(name):
            a, b = _SEG[name]
            return _dot(hb, win_ref[:, a:b])

        cos = cos_ref[rows, :]
        sin = sin_ref[rows, :]
        sq = _group_rms(seg("sq"), bdq_ref, qnw_ref)
        for c in range(SWA_Q // LANES):
            blk = _rope(sq[:, c * LANES:(c + 1) * LANES], cos, sin, low_half)
            sq_ref[rows, c * LANES:(c + 1) * LANES] = (blk * (LOG2E * HEAD_DIM ** -0.5)).astype(sq_ref.dtype)
        sk = _rope(_group_rms(seg("sk"), bdk_ref, knw_ref), cos, sin, low_half)
        sv = seg("sv")
        keep = (grp + 1) * mg - (m - tail)
        if keep > 0:
            keep = min(keep, mg)
            dst_rows = slice((grp + 1) * mg - keep - (m - tail), (grp + 1) * mg - (m - tail))
            if tail_t:
                assert keep == tail
                kk_ref[...] = sk[mg - keep:].T
                vk_ref[...] = sv[mg - keep:].T
            else:
                kk_ref[dst_rows, :] = sk[mg - keep:]
                vk_ref[dst_rows, :] = sv[mg - keep:]
        for src, dst in ((sk, k2_ref), (sv, v2_ref)):
            d0, d1 = _dup_heads(src, low_lanes)
            dst[rows, :LANES] = d0.astype(dst.dtype)
            dst[rows, LANES:] = d1.astype(dst.dtype)
        glr = seg("glr").astype(bf16)
        g = _dot(glr, wgu_ref[...]) + bg_ref[...]
        log_sig = jnp.minimum(g, 0.0) - jnp.log1p(jnp.exp(-jnp.abs(g)))
        la_ref[rows, :] = log_sig * (LOG2E / GLA_TAU)
        gq_ref[rows, :] = seg("gq") * (GLA_DK ** -0.5)
        gk_ref[rows, :] = seg("gk")
        gv_ref[rows, :] = seg("gv").astype(gv_ref.dtype)
        gr_ref[rows, :] = seg("gr").astype(gr_ref.dtype)


MOD_SHIFT1, MOD_SCALE1, MOD_GATE1, MOD_SHIFT2, MOD_SCALE2, MOD_GATE2 = range(6)


def _mod_spec(nb, row0, chunk, batch_block):
    return pl.BlockSpec((nb, 1, D_MODEL), lambda *ids: (row0 // nb + batch_block(*ids), 0, chunk))


def _inproj(x, mod, row0, n1, win, wgu, bg, qnw, knw, bdq, bdk, cos_t, sin_t, nb, tb, tail, act, groups,
            to_bf16=(), tail_t=False):
    bsz, t, d = x.shape
    m = nb * tb
    nt = t // tb
    grid = (bsz // nb, nt)
    tok = bsz * t
    nsteps = grid[0] * grid[1]
    slab_specs = [pl.BlockSpec((w.shape[0] // nsteps, w.shape[1]), lambda i, j: (i * nt + j, 0))
                  for w in to_bf16]

    def full(shape):
        return pl.BlockSpec(shape, lambda i, j: (0,) * len(shape))

    def out(width):
        return pl.BlockSpec((m, width), lambda i, j: (i * nt + j, 0))

    tail_blk = (SWA_KV, tail) if tail_t else (tail, SWA_KV)
    tail_spec = pl.BlockSpec(tail_blk, lambda i, j: (i, 0))
    tail_shape = jax.ShapeDtypeStruct(((bsz // nb) * tail_blk[0], tail_blk[1]), f32)
    outs = ((GLA_QK, f32), (GLA_QK, f32), (GLA_WIDTH, act), (GLA_WIDTH, act), (GLA_QK, f32),
            (SWA_Q, act), (SWA_KV2, bf16), (SWA_KV2, bf16))
    return pl.pallas_call(
        functools.partial(_inproj_kernel, groups=groups, ncast=len(to_bf16), tail=tail, tail_t=tail_t),
        grid=grid,
        in_specs=[pl.BlockSpec((nb, tb, d), lambda i, j: (i, j, 0)),
                  _mod_spec(nb, row0, MOD_SHIFT1, lambda i, j: i),
                  _mod_spec(nb, row0, MOD_SCALE1, lambda i, j: i),
                  full((1, 1, d)),
                  full((d, IN_WIDTH_PADDED)),
                  full((LANES, GLA_QK)),
                  full((1, GLA_QK)),
                  full((1, SWA_Q)),
                  full((1, SWA_KV)),
                  full((SWA_Q, SWA_Q)),
                  full((SWA_KV, SWA_KV)),
                  pl.BlockSpec((m, LANES), lambda i, j: (j, 0)),
                  pl.BlockSpec((m, LANES), lambda i, j: (j, 0))] + slab_specs,
        out_specs=[out(w) for w, _ in outs] + [tail_spec, tail_spec] + slab_specs,
        out_shape=[jax.ShapeDtypeStruct((tok, w), dt) for w, dt in outs]
        + [tail_shape] * 2
        + [jax.ShapeDtypeStruct(w.shape, bf16) for w in to_bf16],
        compiler_params=pltpu.CompilerParams(
            dimension_semantics=("parallel", "arbitrary"), vmem_limit_bytes=VMEM_LIMIT),
        name="inproj",
    )(x, mod, mod, n1, win, wgu, bg, qnw, knw, bdq, bdk, cos_t, sin_t, *to_bf16)


def _gla_constants(chunk, seq):
    t = np.arange(chunk)
    levels = []
    m = 1
    while m < seq:
        levels.append(m)
        m *= 2
    masks = [np.eye(chunk, dtype=bool)]
    for m in levels:
        upper = (t % (2 * m) >= m)[:, None]
        lower = (t % (2 * m) < m)[None, :]
        same = (t[:, None] // (2 * m)) == (t[None, :] // (2 * m))
        masks.append(same & upper & lower)
    tri = t[None, :] <= t[:, None]
    tiled = np.tile(np.stack(masks).astype(np.float32), (1, 1, GLA_HEADS))
    return tuple(levels), jnp.asarray(tri.astype(np.float32), bf16), jnp.asarray(tiled)


def _stack_heads(xb, lane_head, axis):
    zero = jnp.zeros_like(xb)
    return jnp.concatenate([jnp.where(lane_head == h, xb, zero) for h in range(GLA_HEADS)], axis=axis)


def _block_sums(la, bcum, m, row, rolls):
    c, n = la.shape
    if m == 1:
        return la, None
    if m < 8:
        def rolled(shift):
            if shift not in rolls:
                rolls[shift] = pltpu.roll(la, shift % c, axis=0)
            return rolls[shift]
        pos = row & (m - 1)
        pre = la
        suf = None
        for j in range(1, m):
            pre = pre + jnp.where(pos >= j, rolled(j), 0.0)
            term = jnp.where(pos < m - j, rolled(-j), 0.0)
            suf = term if suf is None else suf + term
        return pre, suf
    before, last = [], []
    for i in range(c // m):
        before.append(jnp.zeros((m, n), f32) if i == 0
                      else jnp.broadcast_to(bcum[i * m - 1:i * m], (m, n)))
        last.append(jnp.broadcast_to(bcum[(i + 1) * m - 1:(i + 1) * m], (m, n)))
    if len(before) == 1:
        return bcum - before[0], last[0] - bcum
    return bcum - jnp.concatenate(before, axis=0), jnp.concatenate(last, axis=0) - bcum


def _gla_scores(q_ref, k_ref, v_ref, la_ref, tri_ref, masks_ref, levels, seq):
    c = q_ref.shape[0]
    la = la_ref[...]
    hi = la.astype(bf16)
    lo = (la - hi.astype(f32)).astype(bf16)
    tri = tri_ref[...]
    bcum = _dot(tri, hi) + _dot(tri, lo)
    q = q_ref[...]
    k = k_ref[...]
    vb = v_ref[...].astype(bf16)
    row = lax.broadcasted_iota(jnp.int32, (c, GLA_QK), 0)
    row_head_t = lax.broadcasted_iota(jnp.int32, (GLA_QK, c), 0) >> 6
    rolls = {}
    attn = None
    for lvl, m in enumerate((0,) + levels):
        if m == 0:
            qt, kt = q, k
        else:
            pre, suf = _block_sums(la, bcum, m, row, rolls)
            qt = q * jnp.exp2(pre)
            kt = k if suf is None else k * jnp.exp2(suf)
        kt_t = kt.T.astype(bf16)
        r = _dot(qt.astype(bf16), _stack_heads(kt_t, row_head_t, 1))
        rm = r * masks_ref[lvl]
        attn = rm if attn is None else attn + rm
    a = attn.astype(bf16)
    pre, suf = _block_sums(la, bcum, seq, row, rolls)
    qf = (q * jnp.exp2(pre)).astype(bf16)
    kf = (k * jnp.exp2(suf)).astype(bf16)
    return a, qf, kf, vb, bcum, hi, lo


def _gla_values(a, vb):
    v_head = lax.broadcasted_iota(jnp.int32, vb.shape, 1) >> 7
    return _dot(a, _stack_heads(vb, v_head, 0))


def _gla_finish(o, r_ref, gnw, go_ref):
    for h in range(GLA_HEADS):
        sl = slice(h * GLA_DV, (h + 1) * GLA_DV)
        oh = o[:, sl]
        r = r_ref[:, sl].astype(f32)
        ms = jnp.mean(oh * oh, axis=-1, keepdims=True)
        go_ref[:, sl] = (oh * lax.rsqrt(ms + EPS) * gnw * (r * _sigmoid(r))).astype(go_ref.dtype)


def _gla_prompt_block(st, q_ref, k_ref, v_ref, la_ref, r_ref, tri_ref, masks_ref, gnw, go_ref, levels,
                      between=lambda: None):
    c = PROMPT_GLA_CHUNK
    nchunk = q_ref.shape[0] // c
    lane_head = lax.broadcasted_iota(jnp.int32, (GLA_DV, GLA_QK), 1) >> 6
    row_head = lax.broadcasted_iota(jnp.int32, (GLA_QK, GLA_DV), 0) >> 6
    chunks = []
    for ci in range(nchunk):
        rows = pl.ds(ci * c, c)
        chunks.append(_gla_scores(q_ref.at[rows], k_ref.at[rows], v_ref.at[rows], la_ref.at[rows],
                                  tri_ref, masks_ref, levels, c))
        between()
    intra, upds, decays = [], [], []
    for a, _, kf, vb, bcum, _, _ in chunks:
        intra.append(_gla_values(a, vb))
        full = _dot_tn(vb, kf)
        upd = None
        for h in range(GLA_HEADS):
            term = jnp.where(lane_head == h, full[h * GLA_DV:(h + 1) * GLA_DV], 0.0)
            upd = term if upd is None else upd + term
        upds.append(upd)
        decays.append(jnp.exp2(bcum[c - 1:c]))
    between()
    for ci in range(nchunk):
        rows = pl.ds(ci * c, c)
        sbd = _stack_heads(st.T.astype(bf16), row_head, 1)
        o = intra[ci] + _dot(chunks[ci][1], sbd)
        _gla_finish(o, r_ref.at[rows], gnw, go_ref.at[rows])
        st = decays[ci] * st + upds[ci]
    return st


def _gla_sample_kernel(q_ref, k_ref, v_ref, la_ref, r_ref, s0_ref, tri_ref, masks_ref, gnw_ref,
                       go_ref, sout_ref, *, levels, seq):
    c = q_ref.shape[0]
    a, qf, kf, vb, _, hi, lo = _gla_scores(q_ref, k_ref, v_ref, la_ref, tri_ref, masks_ref, levels, seq)
    o_intra = _gla_values(a, vb)
    row_head = lax.broadcasted_iota(jnp.int32, (GLA_QK, GLA_DV), 0) >> 6
    ones = jnp.ones((seq, GLA_DV), bf16)
    inter = []
    for b in range(c // seq):
        rows = slice(b * seq, (b + 1) * seq)
        s_old = s0_ref[b]
        sbd = _stack_heads(s_old.astype(bf16), row_head, 1)
        inter.append(_dot(qf[rows], sbd))
        p = _dot_tn(kf[rows], vb[rows])
        upd = None
        for h in range(GLA_HEADS):
            term = jnp.where(row_head == h, p[:, h * GLA_DV:(h + 1) * GLA_DV], 0.0)
            upd = term if upd is None else upd + term
        total = _dot_tn(hi[rows], ones) + _dot_tn(lo[rows], ones)
        sout_ref[b] = jnp.exp2(total) * s_old + upd
    o = o_intra + jnp.concatenate(inter, axis=0)
    _gla_finish(o, r_ref, gnw_ref[...], go_ref)


def _gla_sample(gq, gk, gv, la, gr, state, gnw, bsz, t):
    c = GLA_CHUNK
    nseq = c // t
    levels, tri, masks = _gla_constants(c, t)

    def tok(width):
        return pl.BlockSpec((c, width), lambda i: (i, 0))

    return pl.pallas_call(
        functools.partial(_gla_sample_kernel, levels=levels, seq=t),
        grid=(bsz // nseq,),
        in_specs=[tok(GLA_QK), tok(GLA_QK), tok(GLA_WIDTH), tok(GLA_QK), tok(GLA_WIDTH),
                  pl.BlockSpec((nseq, GLA_QK, GLA_DV), lambda i: (i, 0, 0)),
                  pl.BlockSpec(tri.shape, lambda i: (0, 0)),
                  pl.BlockSpec(masks.shape, lambda i: (0, 0, 0)),
                  pl.BlockSpec((1, GLA_DV), lambda i: (0, 0))],
        out_specs=[tok(GLA_WIDTH),
                   pl.BlockSpec((nseq, GLA_QK, GLA_DV), lambda i: (i, 0, 0))],
        out_shape=[jax.ShapeDtypeStruct((bsz * t, GLA_WIDTH), f32),
                   jax.ShapeDtypeStruct((bsz, GLA_QK, GLA_DV), f32)],
        compiler_params=pltpu.CompilerParams(
            dimension_semantics=("parallel",), vmem_limit_bytes=VMEM_LIMIT),
        name="gla_sample",
    )(gq, gk, gv, la, gr, state, tri, masks, gnw)


def _sink_attention(q_ref, rows, kdups, vdups, mask, sink_ref, o_ref, between=lambda: None):
    n = len(kdups)
    heads_per_group = N_Q_HEADS // N_KV_HEADS
    low_q = lax.broadcasted_iota(jnp.int32, (rows, LANES), 1) < HEAD_DIM
    scores = []
    for i in range(n):
        per_group = []
        for g in range(N_KV_HEADS):
            stack = []
            for p in range(heads_per_group // 2):
                pair = g * (heads_per_group // 2) + p
                qp = q_ref[i * rows:(i + 1) * rows, pair * LANES:(pair + 1) * LANES].astype(bf16)
                zero = jnp.zeros_like(qp)
                stack.append(jnp.where(low_q, qp, zero))
                stack.append(jnp.where(low_q, zero, qp))
            per_group.append(_dot_nt(jnp.concatenate(stack, axis=0), kdups[i][g]))
        scores.append(per_group)
    between()
    probs, inv = [], []
    for head in range(N_Q_HEADS):
        g, hh = divmod(head, heads_per_group)
        parts = [scores[i][g][hh * rows:(hh + 1) * rows] for i in range(n)]
        s = parts[0] if n == 1 else jnp.concatenate(parts, axis=0)
        sink = sink_ref[head] * LOG2E
        s = jnp.where(mask, s, -1e30)
        mx = jnp.maximum(jnp.max(s, axis=-1, keepdims=True), sink)
        p = jnp.exp2(s - mx)
        inv.append(1.0 / (jnp.sum(p, axis=-1, keepdims=True) + jnp.exp2(sink - mx)))
        probs.append(p.astype(bf16))
    between()
    outs = []
    for i in range(n):
        per_group = []
        for g in range(N_KV_HEADS):
            p_i = jnp.concatenate([probs[g * heads_per_group + hh][i * rows:(i + 1) * rows]
                                   for hh in range(heads_per_group)], axis=0)
            per_group.append(_dot(p_i, vdups[i][g]))
        outs.append(per_group)
    low_all = lax.broadcasted_iota(jnp.int32, (n * rows, LANES), 1) < HEAD_DIM
    for pair in range(N_Q_HEADS // 2):
        halves = []
        for head in (2 * pair, 2 * pair + 1):
            g, hh = divmod(head, heads_per_group)
            parts = [outs[i][g][hh * rows:(hh + 1) * rows] for i in range(n)]
            o = parts[0] if n == 1 else jnp.concatenate(parts, axis=0)
            halves.append(o * inv[head])
        o_ref[:, pair * LANES:(pair + 1) * LANES] = jnp.where(low_all, halves[0], halves[1]).astype(o_ref.dtype)


def _swa_prompt_blocks(q_ref, k_before, k2_ref, v_before, v2_ref, first, sink_ref, o_ref,
                       between=lambda: None):
    w = WINDOW
    nblk = q_ref.shape[0] // w

    def dup(before, ref):
        blocks = [before] + [ref[i * w:(i + 1) * w, :] for i in range(nblk)]
        return [[jnp.concatenate([blocks[i][:, g * LANES:(g + 1) * LANES],
                                  blocks[i + 1][:, g * LANES:(g + 1) * LANES]], axis=0)
                 for g in range(N_KV_HEADS)] for i in range(nblk)]

    row = lax.broadcasted_iota(jnp.int32, (nblk * w, 2 * w), 0)
    tk = lax.broadcasted_iota(jnp.int32, (nblk * w, 2 * w), 1)
    rel = tk - (row & (w - 1))
    first_key = jnp.where(row < w, jnp.where(first, w, 0), 0)
    mask = (rel > 0) & (rel <= w) & (tk >= first_key)
    _sink_attention(q_ref, w, dup(k_before, k2_ref), dup(v_before, v2_ref), mask, sink_ref, o_ref, between)


def _swa_sample_kernel(sink_ref, q_ref, kn_ref, vn_ref, pk_ref, pv_ref, o_ref, ko_ref, vo_ref, *, seq):
    nseq, _, _, w = pk_ref.shape
    rows = nseq * seq
    heads_per_group = N_Q_HEADS // N_KV_HEADS
    lane = lax.broadcasted_iota(jnp.int32, (rows, LANES), 1)
    r_id = lax.broadcasted_iota(jnp.int32, (rows, LANES), 0)
    low = lane < HEAD_DIM
    pos = r_id & (seq - 1)
    mask_old = lane > pos
    seq_shift = seq.bit_length() - 1
    mask_new = ((r_id >> seq_shift) == (lane >> seq_shift)) & ((lane & (seq - 1)) <= pos)
    kn = kn_ref[...]
    vn = vn_ref[...]
    kn_dup = [d.astype(bf16) for d in _dup_heads(kn, low)]
    vn_dup = [d.astype(bf16) for d in _dup_heads(vn, low)]

    pad = jnp.zeros((w - seq, LANES), f32)
    tail_lanes = lax.broadcasted_iota(jnp.int32, (HEAD_DIM, w), 1) >= w - seq
    kt_dup, vt_dup = [], []
    for b in range(nseq):
        tok = slice(b * seq, (b + 1) * seq)
        per_k, per_v = [], []
        for new, old_ref, out_ref, per in ((kn, pk_ref, ko_ref, per_k), (vn, pv_ref, vo_ref, per_v)):
            new_t = jnp.concatenate([pad, new[tok]], axis=0).T
            for g in range(N_KV_HEADS):
                old = old_ref[b, g]
                out_ref[b, g] = jnp.where(tail_lanes, new_t[g * HEAD_DIM:(g + 1) * HEAD_DIM],
                                          pltpu.roll(old, w - seq, axis=1))
                ob = old.astype(bf16)
                per.append(jnp.concatenate([ob, ob], axis=0))
        kt_dup.append(per_k)
        vt_dup.append(per_v)

    lhs = []
    for g in range(N_KV_HEADS):
        stack = []
        for p in range(heads_per_group // 2):
            pair = g * (heads_per_group // 2) + p
            qp = q_ref[:, pair * LANES:(pair + 1) * LANES].astype(bf16)
            zero = jnp.zeros_like(qp)
            stack.append(jnp.where(low, qp, zero))
            stack.append(jnp.where(low, zero, qp))
        lhs.append(jnp.concatenate(stack, axis=0))
    s_new = [_dot_nt(lhs[g], kn_dup[g]) for g in range(N_KV_HEADS)]
    s_old = []
    for b in range(nseq):
        per = []
        for g in range(N_KV_HEADS):
            qb = jnp.concatenate([lhs[g][hh * rows + b * seq:hh * rows + (b + 1) * seq]
                                  for hh in range(heads_per_group)], axis=0)
            per.append(_dot(qb, kt_dup[b][g]))
        s_old.append(per)

    p_old, p_new, inv = [], [], []
    for head in range(N_Q_HEADS):
        g, hh = divmod(head, heads_per_group)
        so = jnp.concatenate([s_old[b][g][hh * seq:(hh + 1) * seq] for b in range(nseq)], axis=0)
        so = jnp.where(mask_old, so, -1e30)
        sn = jnp.where(mask_new, s_new[g][hh * rows:(hh + 1) * rows], -1e30)
        sink = sink_ref[head] * LOG2E
        mx = jnp.maximum(jnp.maximum(jnp.max(so, axis=-1, keepdims=True),
                                     jnp.max(sn, axis=-1, keepdims=True)), sink)
        po = jnp.exp2(so - mx)
        pn = jnp.exp2(sn - mx)
        inv.append(1.0 / (jnp.sum(po, axis=-1, keepdims=True) + jnp.sum(pn, axis=-1, keepdims=True)
                          + jnp.exp2(sink - mx)))
        p_old.append(po.astype(bf16))
        p_new.append(pn.astype(bf16))

    o_new = [_dot(jnp.concatenate([p_new[g * heads_per_group + hh] for hh in range(heads_per_group)], axis=0),
                  vn_dup[g]) for g in range(N_KV_HEADS)]
    o_old = []
    for b in range(nseq):
        per = []
        for g in range(N_KV_HEADS):
            pb = jnp.concatenate([p_old[g * heads_per_group + hh][b * seq:(b + 1) * seq]
                                  for hh in range(heads_per_group)], axis=0)
            per.append(_dot_nt(pb, vt_dup[b][g]))
        o_old.append(per)

    for pair in range(N_Q_HEADS // 2):
        halves = []
        for head in (2 * pair, 2 * pair + 1):
            g, hh = divmod(head, heads_per_group)
            old = jnp.concatenate([o_old[b][g][hh * seq:(hh + 1) * seq] for b in range(nseq)], axis=0)
            halves.append((old + o_new[g][hh * rows:(hh + 1) * rows]) * inv[head])
        o_ref[:, pair * LANES:(pair + 1) * LANES] = jnp.where(low, halves[0], halves[1])


def _swa_sample(sq, sk, sv, past_kt, past_vt, sinks, bsz, t):
    nseq = SAMPLE_ROWS // 2 // t
    assert nseq * t == LANES
    tok = lambda width: pl.BlockSpec((nseq * t, width), lambda i, s: (i, 0))
    cache = pl.BlockSpec((nseq,) + past_kt.shape[1:], lambda i, s: (i, 0, 0, 0))
    return pl.pallas_call(
        functools.partial(_swa_sample_kernel, seq=t),
        grid_spec=pltpu.PrefetchScalarGridSpec(
            num_scalar_prefetch=1,
            grid=(bsz // nseq,),
            in_specs=[tok(SWA_Q), tok(SWA_KV), tok(SWA_KV), cache, cache],
            out_specs=[tok(SWA_Q), cache, cache]),
        out_shape=[jax.ShapeDtypeStruct((bsz * t, SWA_Q), f32),
                   jax.ShapeDtypeStruct(past_kt.shape, f32),
                   jax.ShapeDtypeStruct(past_vt.shape, f32)],
        compiler_params=pltpu.CompilerParams(
            dimension_semantics=("parallel",), vmem_limit_bytes=VMEM_LIMIT),
        name="swa_sample",
    )(sinks, sq, sk, sv, past_kt, past_vt)


FFN_CHUNK = 512


def _out_proj_norm(streams, n2, wout_ref):
    mix = [s[1] for s in streams]
    mix = mix[0] if len(mix) == 1 else jnp.concatenate(mix, axis=0)
    mixed = _dot(mix[:, :GLA_WIDTH], wout_ref[:GLA_WIDTH, :]) + _dot(mix[:, GLA_WIDTH:], wout_ref[GLA_WIDTH:, :])
    hs, hbs, r0 = [], [], 0
    for x, _, g1, sh2, sc2 in streams:
        nb, t, d = x.shape
        h = x + g1 * mixed[r0:r0 + nb * t].reshape(nb, t, d)
        ms = jnp.mean(h * h, axis=-1, keepdims=True)
        hn = h * lax.rsqrt(ms + EPS) * n2
        hn = hn * (1.0 + sc2) + sh2
        hs.append(h)
        hbs.append(hn.reshape(nb * t, d).astype(bf16))
        r0 += nb * t
    return hs, (hbs[0] if len(hbs) == 1 else jnp.concatenate(hbs, axis=0))


def _ffn_piece(hb, w1_ref, w2_ref, c, width):
    cols = slice(c * width, (c + 1) * width)
    a = jnp.maximum(_dot(hb, w1_ref[:, cols]), 0.0)
    return _dot((a * a).astype(bf16), w2_ref[cols, :])


def _resident(shape):
    return pl.BlockSpec(shape, lambda *_: (0,) * len(shape), pipeline_mode=pl.Buffered(1))


def _prompt_kernel(sink_ref,
                   q_ref, k_ref, v_ref, la_ref, r_ref, sq_ref, k2_ref, v2_ref, k2p_ref, v2p_ref,
                   tri_ref, masks_ref, gnw_ref,
                   x_ref, g1_ref, sh2_ref, sc2_ref, g2_ref,
                   xs_ref, gos_ref, sos_ref, g1s_ref, sh2s_ref, sc2s_ref, g2s_ref,
                   n2_ref, wout_ref, w1_ref, w2_ref,
                   y_ref, ys_ref, sout_ref, state_ref, mix_ref, *, levels, nt, nsteps):
    s = pl.program_id(0)
    slot = s % 2

    @pl.when(s == 0)
    def _():
        mix_ref[1] = jnp.zeros(mix_ref.shape[1:], mix_ref.dtype)
        state_ref[...] = jnp.zeros_like(state_ref)

    j = jnp.minimum(s, nsteps - 1) % nt
    st_in = state_ref[...]
    st = jnp.where(j == 0, 0.0, st_in)
    gnw = gnw_ref[...]
    out = mix_ref.at[slot]
    mix_s = jnp.concatenate([gos_ref[...], sos_ref[...]], axis=1).astype(bf16)
    (h, h_s), hb = _out_proj_norm(
        [(x_ref[...], mix_ref[1 - slot], g1_ref[...], sh2_ref[...], sc2_ref[...]),
         (xs_ref[...], mix_s, g1s_ref[...], sh2s_ref[...], sc2s_ref[...])], n2_ref[...], wout_ref)
    rows_p = h.shape[0] * h.shape[1]
    todo = list(range(D_FF // FFN_CHUNK))
    ff = []

    def ffn_piece():
        if todo:
            part = _ffn_piece(hb, w1_ref, w2_ref, todo.pop(0), FFN_CHUNK)
            ff[:] = [part if not ff else ff[0] + part]

    st = _gla_prompt_block(st, q_ref, k_ref, v_ref, la_ref, r_ref, tri_ref, masks_ref, gnw,
                           out.at[:, pl.ds(0, GLA_WIDTH)], levels, ffn_piece)
    _swa_prompt_blocks(sq_ref, k2p_ref[...], k2_ref, v2p_ref[...], v2_ref, j == 0, sink_ref,
                       out.at[:, pl.ds(GLA_WIDTH, SWA_Q)], ffn_piece)
    while todo:
        ffn_piece()
    st = jnp.where(s < nsteps, st, st_in)
    state_ref[...] = st
    sout_ref[0] = st.T
    y_ref[...] = h + g2_ref[...] * ff[0][:rows_p].reshape(h.shape)
    ys_ref[...] = h_s + g2s_ref[...] * ff[0][rows_p:].reshape(h_s.shape)


def _prompt_mix_ffn(x, gq, gk, gv, la, gr, sq, k2, v2, gnw, sinks, mod, row0, n2, wout, w1, w2,
                    xs, gos, sos, row0_s):
    bsz, t, d = x.shape
    tb = FUSED_TOKEN_BLOCK
    nt = t // tb
    nsteps = bsz * nt
    levels, tri, masks = _gla_constants(PROMPT_GLA_CHUNK, PROMPT_GLA_CHUNK)
    per_blk = tb // WINDOW
    bs, ts, _ = xs.shape
    nbs = bs // nsteps
    assert nbs * nsteps == bs and (nbs * ts) % 16 == 0

    def mix_blk(s):
        return jnp.minimum(s, nsteps - 1)

    def ffn_blk(s):
        return jnp.maximum(s - 1, 0)

    def tok(width):
        return pl.BlockSpec((tb, width), lambda s, _: (mix_blk(s), 0))

    prev = pl.BlockSpec((WINDOW, SWA_KV2), lambda s, _: (jnp.maximum(mix_blk(s) * per_blk - 1, 0), 0))
    xspec = pl.BlockSpec((1, tb, d), lambda s, _: (ffn_blk(s) // nt, ffn_blk(s) % nt, 0))
    ffn_chunks = (MOD_GATE1, MOD_SHIFT2, MOD_SCALE2, MOD_GATE2)
    mods = [_mod_spec(1, row0, chunk, lambda s, _: ffn_blk(s) // nt) for chunk in ffn_chunks]
    mods_s = [_mod_spec(nbs, row0_s, chunk, lambda s, _: ffn_blk(s)) for chunk in ffn_chunks]
    xs_spec = pl.BlockSpec((nbs, ts, d), lambda s, _: (ffn_blk(s), 0, 0))
    mix_s_spec = pl.BlockSpec((nbs * ts, GLA_WIDTH), lambda s, _: (ffn_blk(s), 0))
    return pl.pallas_call(
        functools.partial(_prompt_kernel, levels=levels, nt=nt, nsteps=nsteps),
        grid_spec=pltpu.PrefetchScalarGridSpec(
            num_scalar_prefetch=1,
            grid=(nsteps + 1,),
            in_specs=[tok(GLA_QK), tok(GLA_QK), tok(GLA_WIDTH), tok(GLA_QK), tok(GLA_WIDTH),
                      tok(SWA_Q), tok(SWA_KV2), tok(SWA_KV2), prev, prev,
                      _resident(tri.shape), _resident(masks.shape), _resident((1, GLA_DV)),
                      xspec, *mods, xs_spec, mix_s_spec, mix_s_spec, *mods_s, _resident((1, 1, d)),
                      _resident((d, d)), _resident((d, D_FF)), _resident((D_FF, d))],
            out_specs=[xspec, xs_spec,
                       pl.BlockSpec((1, GLA_QK, GLA_DV), lambda s, _: (mix_blk(s) // nt, 0, 0))],
            scratch_shapes=[pltpu.VMEM((GLA_DV, GLA_QK), f32),
                            pltpu.VMEM((2, tb, GLA_WIDTH + SWA_Q), bf16)]),
        out_shape=[jax.ShapeDtypeStruct((bsz, t, d), f32),
                   jax.ShapeDtypeStruct(xs.shape, f32),
                   jax.ShapeDtypeStruct((bsz, GLA_QK, GLA_DV), f32)],
        compiler_params=pltpu.CompilerParams(
            dimension_semantics=("arbitrary",), vmem_limit_bytes=VMEM_LIMIT),
        name="prompt_mix_ffn",
    )(sinks, gq, gk, gv, la, gr, sq, k2, v2, k2, v2, tri, masks, gnw,
      x, mod, mod, mod, mod, xs, gos, sos, mod, mod, mod, mod, n2, wout, w1, w2)


def _rope_tables(pos, copies):
    half = HEAD_DIM // 2
    inv = np.power(np.float32(ROPE_THETA), -np.arange(half, dtype=np.float32) * np.float32(2.0 / HEAD_DIM))
    ang = np.asarray(pos, np.float32)[:, None] * inv[None, :].astype(np.float32)
    cos = np.cos(ang).astype(np.float32)
    sin = np.sin(ang).astype(np.float32)
    reps = (copies, LANES // HEAD_DIM)
    return (jnp.asarray(np.tile(np.concatenate([cos, cos], axis=-1), reps)),
            jnp.asarray(np.tile(np.concatenate([-sin, sin], axis=-1), reps)))


def _block_diag_ones(n, blk):
    idx = np.arange(n) // blk
    return jnp.asarray((idx[:, None] == idx[None, :]).astype(np.float32), bf16)


def _layer_weights(w_in, w_gate_up, b_gate, q_norm_w, k_norm_w, w_out, w_ff1, w_ff2):
    splits = np.cumsum([GLA_QK, GLA_QK, GLA_WIDTH, GLA_WIDTH, GLA_GATE_RANK, SWA_Q, SWA_KV])
    gq, gk, gv, gr, glr, sq, sk, sv = jnp.split(w_in, [int(s) for s in splits], axis=1)
    glr = jnp.pad(glr, ((0, 0), (0, LANES - GLA_GATE_RANK)))
    win = jnp.concatenate([gq, gk, gv, gr, sq, sk, sv, glr], axis=1).astype(bf16)
    wgu = jnp.pad(w_gate_up, ((0, LANES - GLA_GATE_RANK), (0, 0))).astype(bf16)
    return dict(
        win=win, wgu=wgu, bg=b_gate.reshape(1, GLA_QK),
        qnw=jnp.tile(q_norm_w, N_Q_HEADS).reshape(1, SWA_Q),
        knw=jnp.tile(k_norm_w, N_KV_HEADS).reshape(1, SWA_KV),
        ffn_f32=(w_out, w_ff1, w_ff2))


def _project(x, mod, row0, pos, lw, n1, bdq, bdk, prompt):
    bsz, t, d = x.shape
    if prompt:
        nb, tb, tail, act, groups, cast = 1, TOKEN_BLOCK, WINDOW, bf16, PROMPT_INPROJ_GROUPS, lw["ffn_f32"]
    else:
        nb, tb, tail, act, groups, cast = SAMPLE_ROWS // t, t, SAMPLE_ROWS, f32, 1, ()
    cos_t, sin_t = _rope_tables(pos, nb)
    return _inproj(x, mod, row0, n1.reshape(1, 1, d), lw["win"], lw["wgu"], lw["bg"], lw["qnw"], lw["knw"],
                   bdq, bdk, cos_t, sin_t, nb, tb, tail, act, groups, cast, tail_t=prompt)


def _decoder_layers(xp, xs, mod, pos_p, pos_s, state, past_k, past_v, lw, n1, n2, gnw, sinks, bdq, bdk):
    bp, tp, d = xp.shape
    bs, ts, _ = xs.shape
    heads = (N_KV_HEADS, HEAD_DIM)
    gq, gk, gv, gr, la, sq, _, _, kk, vk = _project(xs, mod, 0, pos_s, lw, n1, bdq, bdk, False)
    go_s, state_s = _gla_sample(gq, gk, gv, la, gr, state.reshape(bs, GLA_QK, GLA_DV), gnw, bs, ts)
    so_s, kt, vt = _swa_sample(sq, kk, vk, past_k.transpose(0, 2, 3, 1), past_v.transpose(0, 2, 3, 1),
                               sinks, bs, ts)
    gq, gk, gv, gr, la, sq, k2, v2, kk, vk, *ffn_w = _project(xp, mod, bs, pos_p, lw, n1, bdq, bdk, True)
    yp, ys, s_t = _prompt_mix_ffn(xp, gq, gk, gv, la, gr, sq, k2, v2, gnw, sinks, mod, bs,
                                  n2.reshape(1, 1, d), *ffn_w, xs, go_s, so_s, 0)
    state_p = s_t.reshape(bp, GLA_HEADS, GLA_DK, GLA_DV)
    return (yp, ys, state_p,
            kk.reshape(bp, *heads, WINDOW).transpose(0, 3, 1, 2), vk.reshape(bp, *heads, WINDOW).transpose(0, 3, 1, 2),
            state_s.reshape(bs, GLA_HEADS, GLA_DK, GLA_DV),
            kt.transpose(0, 3, 1, 2), vt.transpose(0, 3, 1, 2))


def kernel(x_prompt, x_sample, state_gla, cache_swa_k, cache_swa_v, c_prompt, c_sample, w_ada, b_ada, norm1_w, norm2_w, w_in, w_gate_up, b_gate, gla_norm_w, q_norm_w, k_norm_w, sinks, w_out, w_ff1, w_ff2):
    depth = w_ada.shape[0]
    bp, tp, _ = x_prompt.shape
    bs, ts, _ = x_sample.shape
    pos_p = np.arange(tp)
    pos_s = PAST_LEN + np.arange(ts)
    bdq = _block_diag_ones(SWA_Q, HEAD_DIM)
    bdk = _block_diag_ones(SWA_KV, HEAD_DIM)
    yp, ys = x_prompt, x_sample
    outs = [[] for _ in range(6)]
    for l in range(depth):
        mod = _modulation(c_sample, c_prompt, w_ada[l], b_ada[l])
        lw = _layer_weights(w_in[l], w_gate_up[l], b_gate[l], q_norm_w[l], k_norm_w[l],
                            w_out[l], w_ff1[l], w_ff2[l])
        gnw = gla_norm_w[l].reshape(1, GLA_DV)
        yp, ys, *new = _decoder_layers(yp, ys, mod, pos_p, pos_s, state_gla[l], cache_swa_k[l],
                                       cache_swa_v[l], lw, norm1_w[l], norm2_w[l], gnw, sinks[l], bdq, bdk)
        for lst, val in zip(outs, new):
            lst.append(val)
    return (yp, ys) + tuple(jnp.stack(o) for o in outs)
```

```python
import functools

import jax
import jax.numpy as jnp
import numpy as np
from jax import lax
from jax.experimental import pallas as pl
from jax.experimental.pallas import tpu as pltpu

f32 = jnp.float32
bf16 = jnp.bfloat16

D_MODEL = 1024
GLA_HEADS = 4
GLA_DK = 64
GLA_DV = 128
GLA_QK = GLA_HEADS * GLA_DK
GLA_WIDTH = GLA_HEADS * GLA_DV
GLA_GATE_RANK = 16
GLA_TAU = 16.0
LOG2E = 1.4426950408889634
HEAD_DIM = 64
N_Q_HEADS = 8
N_KV_HEADS = 2
SWA_Q = N_Q_HEADS * HEAD_DIM
SWA_KV = N_KV_HEADS * HEAD_DIM
SWA_KV2 = 2 * SWA_KV
WINDOW = 128
ROPE_THETA = 10000.0
PAST_LEN = 8192
D_FF = 4 * D_MODEL
EPS = 1e-6
LANES = 128
GLA_CHUNK = 128
PROMPT_GLA_CHUNK = 128
TOKEN_BLOCK = 1024
FUSED_TOKEN_BLOCK = 512
SAMPLE_ROWS = 256
PROMPT_INPROJ_GROUPS = 1
VMEM_LIMIT = 56 * 1024 * 1024

_SEG = {}
_off = 0
for _name, _w in (("gq", GLA_QK), ("gk", GLA_QK), ("gv", GLA_WIDTH), ("gr", GLA_WIDTH),
                  ("sq", SWA_Q), ("sk", SWA_KV), ("sv", SWA_KV), ("glr", LANES)):
    _SEG[_name] = (_off, _off + _w)
    _off += _w
IN_WIDTH_PADDED = _off


def _dot(a, b):
    return jnp.dot(a, b, preferred_element_type=f32)


def _dot_nt(a, b):
    return lax.dot_general(a, b, (((1,), (1,)), ((), ())), preferred_element_type=f32)


def _dot_tn(a, b):
    return lax.dot_general(a, b, (((0,), (0,)), ((), ())), preferred_element_type=f32)


def _sigmoid(x):
    return 1.0 / (1.0 + jnp.exp(-x))


MOD_W_SLABS = 4


def _mod_kernel(ca_ref, cb_ref, *refs):
    w_refs, b_ref, o_ref = refs[:-2], refs[-2], refs[-1]
    c = jnp.concatenate([ca_ref[...], cb_ref[...]], axis=0)
    s = (c * _sigmoid(c)).astype(bf16)
    kb = w_refs[0].shape[0]
    res = b_ref[...]
    for i, w_ref in enumerate(w_refs):
        res = res + _dot(s[:, i * kb:(i + 1) * kb], w_ref[...].astype(bf16))
    for r in range(res.shape[0]):
        o_ref[r] = res[r:r + 1, :]


def _modulation(c_a, c_b, w_ada, b_ada):
    m = c_a.shape[0] + c_b.shape[0]
    n = w_ada.shape[1]
    bn = 1536
    kb = D_MODEL // MOD_W_SLABS
    slabs = [pl.BlockSpec((kb, bn), lambda j, i=i: (i, j)) for i in range(MOD_W_SLABS)]
    return pl.pallas_call(
        _mod_kernel,
        grid=(n // bn,),
        in_specs=[pl.BlockSpec(c_a.shape, lambda j: (0, 0)), pl.BlockSpec(c_b.shape, lambda j: (0, 0))]
        + slabs + [pl.BlockSpec((1, bn), lambda j: (0, j))],
        out_specs=pl.BlockSpec((m, 1, bn), lambda j: (0, 0, j)),
        out_shape=jax.ShapeDtypeStruct((m, 1, n), f32),
        compiler_params=pltpu.CompilerParams(vmem_limit_bytes=VMEM_LIMIT),
        name="adaln_mod",
    )(c_a, c_b, *([w_ada] * MOD_W_SLABS), b_ada.reshape(1, n))


def _group_rms(x, bd_ref, w_ref):
    ssq = _dot((x * x).astype(bf16), bd_ref[...])
    return x * lax.rsqrt(ssq * (1.0 / HEAD_DIM) + EPS) * w_ref[...]


def _rope(x, cos, sin_signed, low_half):
    partner = jnp.where(low_half, pltpu.roll(x, LANES - 32, axis=1), pltpu.roll(x, 32, axis=1))
    return x * cos + partner * sin_signed


def _dup_heads(x, low_lanes):
    rolled = pltpu.roll(x, HEAD_DIM, axis=1)
    return jnp.where(low_lanes, x, rolled), jnp.where(low_lanes, rolled, x)


def _inproj_kernel(x_ref, sh_ref, sc_ref, n1_ref, win_ref, wgu_ref, bg_ref, qnw_ref, knw_ref,
                   bdq_ref, bdk_ref, cos_ref, sin_ref, *rest, groups, ncast, tail, tail_t):
    slabs, rest = rest[:ncast], rest[ncast:]
    gq_ref, gk_ref, gv_ref, gr_ref, la_ref, sq_ref, k2_ref, v2_ref, kk_ref, vk_ref = rest[:10]
    for src, dst in zip(slabs, rest[10:]):
        dst[...] = src[...].astype(dst.dtype)
    nb, t, d = x_ref.shape
    m = nb * t
    mg = m // groups
    lane = lax.broadcasted_iota(jnp.int32, (mg, LANES), 1)
    low_half = (lane & 32) == 0
    low_lanes = lane < HEAD_DIM
    for grp in range(groups):
        rows = slice(grp * mg, (grp + 1) * mg)
        if nb == 1:
            x = x_ref[:, rows, :]
            sc, sh = sc_ref[...], sh_ref[...]
        else:
            seqs = slice(grp * (nb // groups), (grp + 1) * (nb // groups))
            x = x_ref[seqs]
            sc, sh = sc_ref[seqs], sh_ref[seqs]
        ms = jnp.mean(x * x, axis=-1, keepdims=True)
        hn = x * lax.rsqrt(ms + EPS) * n1_ref[...]
        hn = hn * (1.0 + sc) + sh
        hb = hn.reshape(mg, d).astype(bf16)

        def seg(name):
            a, b = _SEG[name]
            return _dot(hb, win_ref[:, a:b])

        cos = cos_ref[rows, :]
        sin = sin_ref[rows, :]
        sq = _group_rms(seg("sq"), bdq_ref, qnw_ref)
        for c in range(SWA_Q // LANES):
            blk = _rope(sq[:, c * LANES:(c + 1) * LANES], cos, sin, low_half)
            sq_ref[rows, c * LANES:(c + 1) * LANES] = (blk * (LOG2E * HEAD_DIM ** -0.5)).astype(sq_ref.dtype)
        sk = _rope(_group_rms(seg("sk"), bdk_ref, knw_ref), cos, sin, low_half)
        sv = seg("sv")
        keep = (grp + 1) * mg - (m - tail)
        if keep > 0:
            keep = min(keep, mg)
            dst_rows = slice((grp + 1) * mg - keep - (m - tail), (grp + 1) * mg - (m - tail))
            if tail_t:
                assert keep == tail
                kk_ref[...] = sk[mg - keep:].T
                vk_ref[...] = sv[mg - keep:].T
            else:
                kk_ref[dst_rows, :] = sk[mg - keep:]
                vk_ref[dst_rows, :] = sv[mg - keep:]
        for src, dst in ((sk, k2_ref), (sv, v2_ref)):
            d0, d1 = _dup_heads(src, low_lanes)
            dst[rows, :LANES] = d0.astype(dst.dtype)
            dst[rows, LANES:] = d1.astype(dst.dtype)
        glr = seg("glr").astype(bf16)
        g = _dot(glr, wgu_ref[...]) + bg_ref[...]
        log_sig = jnp.minimum(g, 0.0) - jnp.log1p(jnp.exp(-jnp.abs(g)))
        la_ref[rows, :] = log_sig * (LOG2E / GLA_TAU)
        gq_ref[rows, :] = seg("gq") * (GLA_DK ** -0.5)
        gk_ref[rows, :] = seg("gk")
        gv_ref[rows, :] = seg("gv").astype(gv_ref.dtype)
        gr_ref[rows, :] = seg("gr").astype(gr_ref.dtype)


MOD_SHIFT1, MOD_SCALE1, MOD_GATE1, MOD_SHIFT2, MOD_SCALE2, MOD_GATE2 = range(6)


def _mod_spec(nb, row0, chunk, batch_block):
    return pl.BlockSpec((nb, 1, D_MODEL), lambda *ids: (row0 // nb + batch_block(*ids), 0, chunk))


def _inproj(x, mod, row0, n1, win, wgu, bg, qnw, knw, bdq, bdk, cos_t, sin_t, nb, tb, tail, act, groups,
            to_bf16=(), tail_t=False):
    bsz, t, d = x.shape
    m = nb * tb
    nt = t // tb
    grid = (bsz // nb, nt)
    tok = bsz * t
    nsteps = grid[0] * grid[1]
    slab_specs = [pl.BlockSpec((w.shape[0] // nsteps, w.shape[1]), lambda i, j: (i * nt + j, 0))
                  for w in to_bf16]

    def full(shape):
        return pl.BlockSpec(shape, lambda i, j: (0,) * len(shape))

    def out(width):
        return pl.BlockSpec((m, width), lambda i, j: (i * nt + j, 0))

    tail_blk = (SWA_KV, tail) if tail_t else (tail, SWA_KV)
    tail_spec = pl.BlockSpec(tail_blk, lambda i, j: (i, 0))
    tail_shape = jax.ShapeDtypeStruct(((bsz // nb) * tail_blk[0], tail_blk[1]), f32)
    outs = ((GLA_QK, f32), (GLA_QK, f32), (GLA_WIDTH, act), (GLA_WIDTH, act), (GLA_QK, f32),
            (SWA_Q, act), (SWA_KV2, bf16), (SWA_KV2, bf16))
    return pl.pallas_call(
        functools.partial(_inproj_kernel, groups=groups, ncast=len(to_bf16), tail=tail, tail_t=tail_t),
        grid=grid,
        in_specs=[pl.BlockSpec((nb, tb, d), lambda i, j: (i, j, 0)),
                  _mod_spec(nb, row0, MOD_SHIFT1, lambda i, j: i),
                  _mod_spec(nb, row0, MOD_SCALE1, lambda i, j: i),
                  full((1, 1, d)),
                  full((d, IN_WIDTH_PADDED)),
                  full((LANES, GLA_QK)),
                  full((1, GLA_QK)),
                  full((1, SWA_Q)),
                  full((1, SWA_KV)),
                  full((SWA_Q, SWA_Q)),
                  full((SWA_KV, SWA_KV)),
                  pl.BlockSpec((m, LANES), lambda i, j: (j, 0)),
                  pl.BlockSpec((m, LANES), lambda i, j: (j, 0))] + slab_specs,
        out_specs=[out(w) for w, _ in outs] + [tail_spec, tail_spec] + slab_specs,
        out_shape=[jax.ShapeDtypeStruct((tok, w), dt) for w, dt in outs]
        + [tail_shape] * 2
        + [jax.ShapeDtypeStruct(w.shape, bf16) for w in to_bf16],
        compiler_params=pltpu.CompilerParams(
            dimension_semantics=("parallel", "arbitrary"), vmem_limit_bytes=VMEM_LIMIT),
        name="inproj",
    )(x, mod, mod, n1, win, wgu, bg, qnw, knw, bdq, bdk, cos_t, sin_t, *to_bf16)


def _gla_constants(chunk, seq):
    t = np.arange(chunk)
    levels = []
    m = 1
    while m < seq:
        levels.append(m)
        m *= 2
    masks = [np.eye(chunk, dtype=bool)]
    for m in levels:
        upper = (t % (2 * m) >= m)[:, None]
        lower = (t % (2 * m) < m)[None, :]
        same = (t[:, None] // (2 * m)) == (t[None, :] // (2 * m))
        masks.append(same & upper & lower)
    tri = t[None, :] <= t[:, None]
    tiled = np.tile(np.stack(masks).astype(np.float32), (1, 1, GLA_HEADS))
    return tuple(levels), jnp.asarray(tri.astype(np.float32), bf16), jnp.asarray(tiled)


def _stack_heads(xb, lane_head, axis):
    zero = jnp.zeros_like(xb)
    return jnp.concatenate([jnp.where(lane_head == h, xb, zero) for h in range(GLA_HEADS)], axis=axis)


def _block_sums(la, bcum, m, row, rolls):
    c, n = la.shape
    if m == 1:
        return la, None
    if m < 8:
        def rolled(shift):
            if shift not in rolls:
                rolls[shift] = pltpu.roll(la, shift % c, axis=0)
            return rolls[shift]
        pos = row & (m - 1)
        pre = la
        suf = None
        for j in range(1, m):
            pre = pre + jnp.where(pos >= j, rolled(j), 0.0)
            term = jnp.where(pos < m - j, rolled(-j), 0.0)
            suf = term if suf is None else suf + term
        return pre, suf
    before, last = [], []
    for i in range(c // m):
        before.append(jnp.zeros((m, n), f32) if i == 0
                      else jnp.broadcast_to(bcum[i * m - 1:i * m], (m, n)))
        last.append(jnp.broadcast_to(bcum[(i + 1) * m - 1:(i + 1) * m], (m, n)))
    if len(before) == 1:
        return bcum - before[0], last[0] - bcum
    return bcum - jnp.concatenate(before, axis=0), jnp.concatenate(last, axis=0) - bcum


def _gla_scores(q_ref, k_ref, v_ref, la_ref, tri_ref, masks_ref, levels, seq):
    c = q_ref.shape[0]
    la = la_ref[...]
    hi = la.astype(bf16)
    lo = (la - hi.astype(f32)).astype(bf16)
    tri = tri_ref[...]
    bcum = _dot(tri, hi) + _dot(tri, lo)
    q = q_ref[...]
    k = k_ref[...]
    vb = v_ref[...].astype(bf16)
    row = lax.broadcasted_iota(jnp.int32, (c, GLA_QK), 0)
    row_head_t = lax.broadcasted_iota(jnp.int32, (GLA_QK, c), 0) >> 6
    rolls = {}
    attn = None
    for lvl, m in enumerate((0,) + levels):
        if m == 0:
            qt, kt = q, k
        else:
            pre, suf = _block_sums(la, bcum, m, row, rolls)
            qt = q * jnp.exp2(pre)
            kt = k if suf is None else k * jnp.exp2(suf)
        kt_t = kt.T.astype(bf16)
        r = _dot(qt.astype(bf16), _stack_heads(kt_t, row_head_t, 1))
        rm = r * masks_ref[lvl]
        attn = rm if attn is None else attn + rm
    a = attn.astype(bf16)
    pre, suf = _block_sums(la, bcum, seq, row, rolls)
    qf = (q * jnp.exp2(pre)).astype(bf16)
    kf = (k * jnp.exp2(suf)).astype(bf16)
    return a, qf, kf, vb, bcum, hi, lo


def _gla_values(a, vb):
    v_head = lax.broadcasted_iota(jnp.int32, vb.shape, 1) >> 7
    return _dot(a, _stack_heads(vb, v_head, 0))


def _gla_finish(o, r_ref, gnw, go_ref):
    for h in range(GLA_HEADS):
        sl = slice(h * GLA_DV, (h + 1) * GLA_DV)
        oh = o[:, sl]
        r = r_ref[:, sl].astype(f32)
        ms = jnp.mean(oh * oh, axis=-1, keepdims=True)
        go_ref[:, sl] = (oh * lax.rsqrt(ms + EPS) * gnw * (r * _sigmoid(r))).astype(go_ref.dtype)


def _gla_prompt_block(st, q_ref, k_ref, v_ref, la_ref, r_ref, tri_ref, masks_ref, gnw, go_ref, levels,
                      between=lambda: None):
    c = PROMPT_GLA_CHUNK
    nchunk = q_ref.shape[0] // c
    lane_head = lax.broadcasted_iota(jnp.int32, (GLA_DV, GLA_QK), 1) >> 6
    row_head = lax.broadcasted_iota(jnp.int32, (GLA_QK, GLA_DV), 0) >> 6
    chunks = []
    for ci in range(nchunk):
        rows = pl.ds(ci * c, c)
        chunks.append(_gla_scores(q_ref.at[rows], k_ref.at[rows], v_ref.at[rows], la_ref.at[rows],
                                  tri_ref, masks_ref, levels, c))
        between()
    intra, upds, decays = [], [], []
    for a, _, kf, vb, bcum, _, _ in chunks:
        intra.append(_gla_values(a, vb))
        full = _dot_tn(vb, kf)
        upd = None
        for h in range(GLA_HEADS):
            term = jnp.where(lane_head == h, full[h * GLA_DV:(h + 1) * GLA_DV], 0.0)
            upd = term if upd is None else upd + term
        upds.append(upd)
        decays.append(jnp.exp2(bcum[c - 1:c]))
    between()
    for ci in range(nchunk):
        rows = pl.ds(ci * c, c)
        sbd = _stack_heads(st.T.astype(bf16), row_head, 1)
        o = intra[ci] + _dot(chunks[ci][1], sbd)
        _gla_finish(o, r_ref.at[rows], gnw, go_ref.at[rows])
        st = decays[ci] * st + upds[ci]
    return st


def _gla_sample_kernel(q_ref, k_ref, v_ref, la_ref, r_ref, s0_ref, tri_ref, masks_ref, gnw_ref,
                       go_ref, sout_ref, *, levels, seq):
    c = q_ref.shape[0]
    a, qf, kf, vb, _, hi, lo = _gla_scores(q_ref, k_ref, v_ref, la_ref, tri_ref, masks_ref, levels, seq)
    o_intra = _gla_values(a, vb)
    row_head = lax.broadcasted_iota(jnp.int32, (GLA_QK, GLA_DV), 0) >> 6
    ones = jnp.ones((seq, GLA_DV), bf16)
    inter = []
    for b in range(c // seq):
        rows = slice(b * seq, (b + 1) * seq)
        s_old = s0_ref[b]
        sbd = _stack_heads(s_old.astype(bf16), row_head, 1)
        inter.append(_dot(qf[rows], sbd))
        p = _dot_tn(kf[rows], vb[rows])
        upd = None
        for h in range(GLA_HEADS):
            term = jnp.where(row_head == h, p[:, h * GLA_DV:(h + 1) * GLA_DV], 0.0)
            upd = term if upd is None else upd + term
        total = _dot_tn(hi[rows], ones) + _dot_tn(lo[rows], ones)
        sout_ref[b] = jnp.exp2(total) * s_old + upd
    o = o_intra + jnp.concatenate(inter, axis=0)
    _gla_finish(o, r_ref, gnw_ref[...], go_ref)


def _gla_sample(gq, gk, gv, la, gr, state, gnw, bsz, t):
    c = GLA_CHUNK
    nseq = c // t
    levels, tri, masks = _gla_constants(c, t)

    def tok(width):
        return pl.BlockSpec((c, width), lambda i: (i, 0))

    return pl.pallas_call(
        functools.partial(_gla_sample_kernel, levels=levels, seq=t),
        grid=(bsz // nseq,),
        in_specs=[tok(GLA_QK), tok(GLA_QK), tok(GLA_WIDTH), tok(GLA_QK), tok(GLA_WIDTH),
                  pl.BlockSpec((nseq, GLA_QK, GLA_DV), lambda i: (i, 0, 0)),
                  pl.BlockSpec(tri.shape, lambda i: (0, 0)),
                  pl.BlockSpec(masks.shape, lambda i: (0, 0, 0)),
                  pl.BlockSpec((1, GLA_DV), lambda i: (0, 0))],
        out_specs=[tok(GLA_WIDTH),
                   pl.BlockSpec((nseq, GLA_QK, GLA_DV), lambda i: (i, 0, 0))],
        out_shape=[jax.ShapeDtypeStruct((bsz * t, GLA_WIDTH), f32),
                   jax.ShapeDtypeStruct((bsz, GLA_QK, GLA_DV), f32)],
        compiler_params=pltpu.CompilerParams(
            dimension_semantics=("parallel",), vmem_limit_bytes=VMEM_LIMIT),
        name="gla_sample",
    )(gq, gk, gv, la, gr, state, tri, masks, gnw)


def _sink_attention(q_ref, rows, kdups, vdups, mask, sink_ref, o_ref, between=lambda: None):
    n = len(kdups)
    heads_per_group = N_Q_HEADS // N_KV_HEADS
    low_q = lax.broadcasted_iota(jnp.int32, (rows, LANES), 1) < HEAD_DIM
    scores = []
    for i in range(n):
        per_group = []
        for g in range(N_KV_HEADS):
            stack = []
            for p in range(heads_per_group // 2):
                pair = g * (heads_per_group // 2) + p
                qp = q_ref[i * rows:(i + 1) * rows, pair * LANES:(pair + 1) * LANES].astype(bf16)
                zero = jnp.zeros_like(qp)
                stack.append(jnp.where(low_q, qp, zero))
                stack.append(jnp.where(low_q, zero, qp))
            per_group.append(_dot_nt(jnp.concatenate(stack, axis=0), kdups[i][g]))
        scores.append(per_group)
    between()
    probs, inv = [], []
    for head in range(N_Q_HEADS):
        g, hh = divmod(head, heads_per_group)
        parts = [scores[i][g][hh * rows:(hh + 1) * rows] for i in range(n)]
        s = parts[0] if n == 1 else jnp.concatenate(parts, axis=0)
        sink = sink_ref[head] * LOG2E
        s = jnp.where(mask, s, -1e30)
        mx = jnp.maximum(jnp.max(s, axis=-1, keepdims=True), sink)
        p = jnp.exp2(s - mx)
        inv.append(1.0 / (jnp.sum(p, axis=-1, keepdims=True) + jnp.exp2(sink - mx)))
        probs.append(p.astype(bf16))
    between()
    outs = []
    for i in range(n):
        per_group = []
        for g in range(N_KV_HEADS):
            p_i = jnp.concatenate([probs[g * heads_per_group + hh][i * rows:(i + 1) * rows]
                                   for hh in range(heads_per_group)], axis=0)
            per_group.append(_dot(p_i, vdups[i][g]))
        outs.append(per_group)
    low_all = lax.broadcasted_iota(jnp.int32, (n * rows, LANES), 1) < HEAD_DIM
    for pair in range(N_Q_HEADS // 2):
        halves = []
        for head in (2 * pair, 2 * pair + 1):
            g, hh = divmod(head, heads_per_group)
            parts = [outs[i][g][hh * rows:(hh + 1) * rows] for i in range(n)]
            o = parts[0] if n == 1 else jnp.concatenate(parts, axis=0)
            halves.append(o * inv[head])
        o_ref[:, pair * LANES:(pair + 1) * LANES] = jnp.where(low_all, halves[0], halves[1]).astype(o_ref.dtype)


def _swa_prompt_blocks(q_ref, k_before, k2_ref, v_before, v2_ref, first, sink_ref, o_ref,
                       between=lambda: None):
    w = WINDOW
    nblk = q_ref.shape[0] // w

    def dup(before, ref):
        blocks = [before] + [ref[i * w:(i + 1) * w, :] for i in range(nblk)]
        return [[jnp.concatenate([blocks[i][:, g * LANES:(g + 1) * LANES],
                                  blocks[i + 1][:, g * LANES:(g + 1) * LANES]], axis=0)
                 for g in range(N_KV_HEADS)] for i in range(nblk)]

    row = lax.broadcasted_iota(jnp.int32, (nblk * w, 2 * w), 0)
    tk = lax.broadcasted_iota(jnp.int32, (nblk * w, 2 * w), 1)
    rel = tk - (row & (w - 1))
    first_key = jnp.where(row < w, jnp.where(first, w, 0), 0)
    mask = (rel > 0) & (rel <= w) & (tk >= first_key)
    _sink_attention(q_ref, w, dup(k_before, k2_ref), dup(v_before, v2_ref), mask, sink_ref, o_ref, between)


def _swa_sample_kernel(sink_ref, q_ref, kn_ref, vn_ref, pk_ref, pv_ref, o_ref, ko_ref, vo_ref, *, seq):
    nseq, _, _, w = pk_ref.shape
    rows = nseq * seq
    heads_per_group = N_Q_HEADS // N_KV_HEADS
    lane = lax.broadcasted_iota(jnp.int32, (rows, LANES), 1)
    r_id = lax.broadcasted_iota(jnp.int32, (rows, LANES), 0)
    low = lane < HEAD_DIM
    pos = r_id & (seq - 1)
    mask_old = lane > pos
    seq_shift = seq.bit_length() - 1
    mask_new = ((r_id >> seq_shift) == (lane >> seq_shift)) & ((lane & (seq - 1)) <= pos)
    kn = kn_ref[...]
    vn = vn_ref[...]
    kn_dup = [d.astype(bf16) for d in _dup_heads(kn, low)]
    vn_dup = [d.astype(bf16) for d in _dup_heads(vn, low)]

    pad = jnp.zeros((w - seq, LANES), f32)
    tail_lanes = lax.broadcasted_iota(jnp.int32, (HEAD_DIM, w), 1) >= w - seq
    kt_dup, vt_dup = [], []
    for b in range(nseq):
        tok = slice(b * seq, (b + 1) * seq)
        per_k, per_v = [], []
        for new, old_ref, out_ref, per in ((kn, pk_ref, ko_ref, per_k), (vn, pv_ref, vo_ref, per_v)):
            new_t = jnp.concatenate([pad, new[tok]], axis=0).T
            for g in range(N_KV_HEADS):
                old = old_ref[b, g]
                out_ref[b, g] = jnp.where(tail_lanes, new_t[g * HEAD_DIM:(g + 1) * HEAD_DIM],
                                          pltpu.roll(old, w - seq, axis=1))
                ob = old.astype(bf16)
                per.append(jnp.concatenate([ob, ob], axis=0))
        kt_dup.append(per_k)
        vt_dup.append(per_v)

    lhs = []
    for g in range(N_KV_HEADS):
        stack = []
        for p in range(heads_per_group // 2):
            pair = g * (heads_per_group // 2) + p
            qp = q_ref[:, pair * LANES:(pair + 1) * LANES].astype(bf16)
            zero = jnp.zeros_like(qp)
            stack.append(jnp.where(low, qp, zero))
            stack.append(jnp.where(low, zero, qp))
        lhs.append(jnp.concatenate(stack, axis=0))
    s_new = [_dot_nt(lhs[g], kn_dup[g]) for g in range(N_KV_HEADS)]
    s_old = []
    for b in range(nseq):
        per = []
        for g in range(N_KV_HEADS):
            qb = jnp.concatenate([lhs[g][hh * rows + b * seq:hh * rows + (b + 1) * seq]
                                  for hh in range(heads_per_group)], axis=0)
            per.append(_dot(qb, kt_dup[b][g]))
        s_old.append(per)

    p_old, p_new, inv = [], [], []
    for head in range(N_Q_HEADS):
        g, hh = divmod(head, heads_per_group)
        so = jnp.concatenate([s_old[b][g][hh * seq:(hh + 1) * seq] for b in range(nseq)], axis=0)
        so = jnp.where(mask_old, so, -1e30)
        sn = jnp.where(mask_new, s_new[g][hh * rows:(hh + 1) * rows], -1e30)
        sink = sink_ref[head] * LOG2E
        mx = jnp.maximum(jnp.maximum(jnp.max(so, axis=-1, keepdims=True),
                                     jnp.max(sn, axis=-1, keepdims=True)), sink)
        po = jnp.exp2(so - mx)
        pn = jnp.exp2(sn - mx)
        inv.append(1.0 / (jnp.sum(po, axis=-1, keepdims=True) + jnp.sum(pn, axis=-1, keepdims=True)
                          + jnp.exp2(sink - mx)))
        p_old.append(po.astype(bf16))
        p_new.append(pn.astype(bf16))

    o_new = [_dot(jnp.concatenate([p_new[g * heads_per_group + hh] for hh in range(heads_per_group)], axis=0),
                  vn_dup[g]) for g in range(N_KV_HEADS)]
    o_old = []
    for b in range(nseq):
        per = []
        for g in range(N_KV_HEADS):
            pb = jnp.concatenate([p_old[g * heads_per_group + hh][b * seq:(b + 1) * seq]
                                  for hh in range(heads_per_group)], axis=0)
            per.append(_dot_nt(pb, vt_dup[b][g]))
        o_old.append(per)

    for pair in range(N_Q_HEADS // 2):
        halves = []
        for head in (2 * pair, 2 * pair + 1):
            g, hh = divmod(head, heads_per_group)
            old = jnp.concatenate([o_old[b][g][hh * seq:(hh + 1) * seq] for b in range(nseq)], axis=0)
            halves.append((old + o_new[g][hh * rows:(hh + 1) * rows]) * inv[head])
        o_ref[:, pair * LANES:(pair + 1) * LANES] = jnp.where(low, halves[0], halves[1])


def _swa_sample(sq, sk, sv, past_kt, past_vt, sinks, bsz, t):
    nseq = SAMPLE_ROWS // 2 // t
    assert nseq * t == LANES
    tok = lambda width: pl.BlockSpec((nseq * t, width), lambda i, s: (i, 0))
    cache = pl.BlockSpec((nseq,) + past_kt.shape[1:], lambda i, s: (i, 0, 0, 0))
    return pl.pallas_call(
        functools.partial(_swa_sample_kernel, seq=t),
        grid_spec=pltpu.PrefetchScalarGridSpec(
            num_scalar_prefetch=1,
            grid=(bsz // nseq,),
            in_specs=[tok(SWA_Q), tok(SWA_KV), tok(SWA_KV), cache, cache],
            out_specs=[tok(SWA_Q), cache, cache]),
        out_shape=[jax.ShapeDtypeStruct((bsz * t, SWA_Q), f32),
                   jax.ShapeDtypeStruct(past_kt.shape, f32),
                   jax.ShapeDtypeStruct(past_vt.shape, f32)],
        compiler_params=pltpu.CompilerParams(
            dimension_semantics=("parallel",), vmem_limit_bytes=VMEM_LIMIT),
        name="swa_sample",
    )(sinks, sq, sk, sv, past_kt, past_vt)


FFN_CHUNK = 512


def _out_proj_norm(streams, n2, wout_ref):
    mix = [s[1] for s in streams]
    mix = mix[0] if len(mix) == 1 else jnp.concatenate(mix, axis=0)
    mixed = _dot(mix[:, :GLA_WIDTH], wout_ref[:GLA_WIDTH, :]) + _dot(mix[:, GLA_WIDTH:], wout_ref[GLA_WIDTH:, :])
    hs, hbs, r0 = [], [], 0
    for x, _, g1, sh2, sc2 in streams:
        nb, t, d = x.shape
        h = x + g1 * mixed[r0:r0 + nb * t].reshape(nb, t, d)
        ms = jnp.mean(h * h, axis=-1, keepdims=True)
        hn = h * lax.rsqrt(ms + EPS) * n2
        hn = hn * (1.0 + sc2) + sh2
        hs.append(h)
        hbs.append(hn.reshape(nb * t, d).astype(bf16))
        r0 += nb * t
    return hs, (hbs[0] if len(hbs) == 1 else jnp.concatenate(hbs, axis=0))


def _ffn_piece(hb, w1_ref, w2_ref, c, width):
    cols = slice(c * width, (c + 1) * width)
    a = jnp.maximum(_dot(hb, w1_ref[:, cols]), 0.0)
    return _dot((a * a).astype(bf16), w2_ref[cols, :])


def _resident(shape):
    return pl.BlockSpec(shape, lambda *_: (0,) * len(shape), pipeline_mode=pl.Buffered(1))


def _prompt_kernel(sink_ref,
                   q_ref, k_ref, v_ref, la_ref, r_ref, sq_ref, k2_ref, v2_ref, k2p_ref, v2p_ref,
                   tri_ref, masks_ref, gnw_ref,
                   x_ref, g1_ref, sh2_ref, sc2_ref, g2_ref,
                   xs_ref, gos_ref, sos_ref, g1s_ref, sh2s_ref, sc2s_ref, g2s_ref,
                   n2_ref, wout_ref, w1_ref, w2_ref,
                   y_ref, ys_ref, sout_ref, state_ref, mix_ref, *, levels, nt, nsteps):
    s = pl.program_id(0)
    slot = s % 2

    @pl.when(s == 0)
    def _():
        mix_ref[1] = jnp.zeros(mix_ref.shape[1:], mix_ref.dtype)
        state_ref[...] = jnp.zeros_like(state_ref)

    j = jnp.minimum(s, nsteps - 1) % nt
    st_in = state_ref[...]
    st = jnp.where(j == 0, 0.0, st_in)
    gnw = gnw_ref[...]
    out = mix_ref.at[slot]
    mix_s = jnp.concatenate([gos_ref[...], sos_ref[...]], axis=1).astype(bf16)
    (h, h_s), hb = _out_proj_norm(
        [(x_ref[...], mix_ref[1 - slot], g1_ref[...], sh2_ref[...], sc2_ref[...]),
         (xs_ref[...], mix_s, g1s_ref[...], sh2s_ref[...], sc2s_ref[...])], n2_ref[...], wout_ref)
    rows_p = h.shape[0] * h.shape[1]
    todo = list(range(D_FF // FFN_CHUNK))
    ff = []

    def ffn_piece():
        if todo:
            part = _ffn_piece(hb, w1_ref, w2_ref, todo.pop(0), FFN_CHUNK)
            ff[:] = [part if not ff else ff[0] + part]

    st = _gla_prompt_block(st, q_ref, k_ref, v_ref, la_ref, r_ref, tri_ref, masks_ref, gnw,
                           out.at[:, pl.ds(0, GLA_WIDTH)], levels, ffn_piece)
    _swa_prompt_blocks(sq_ref, k2p_ref[...], k2_ref, v2p_ref[...], v2_ref, j == 0, sink_ref,
                       out.at[:, pl.ds(GLA_WIDTH, SWA_Q)], ffn_piece)
    while todo:
        ffn_piece()
    st = jnp.where(s < nsteps, st, st_in)
    state_ref[...] = st
    sout_ref[0] = st.T
    y_ref[...] = h + g2_ref[...] * ff[0][:rows_p].reshape(h.shape)
    ys_ref[...] = h_s + g2s_ref[...] * ff[0][rows_p:].reshape(h_s.shape)


def _prompt_mix_ffn(x, gq, gk, gv, la, gr, sq, k2, v2, gnw, sinks, mod, row0, n2, wout, w1, w2,
                    xs, gos, sos, row0_s):
    bsz, t, d = x.shape
    tb = FUSED_TOKEN_BLOCK
    nt = t // tb
    nsteps = bsz * nt
    levels, tri, masks = _gla_constants(PROMPT_GLA_CHUNK, PROMPT_GLA_CHUNK)
    per_blk = tb // WINDOW
    bs, ts, _ = xs.shape
    nbs = bs // nsteps
    assert nbs * nsteps == bs and (nbs * ts) % 16 == 0

    def mix_blk(s):
        return jnp.minimum(s, nsteps - 1)

    def ffn_blk(s):
        return jnp.maximum(s - 1, 0)

    def tok(width):
        return pl.BlockSpec((tb, width), lambda s, _: (mix_blk(s), 0))

    prev = pl.BlockSpec((WINDOW, SWA_KV2), lambda s, _: (jnp.maximum(mix_blk(s) * per_blk - 1, 0), 0))
    xspec = pl.BlockSpec((1, tb, d), lambda s, _: (ffn_blk(s) // nt, ffn_blk(s) % nt, 0))
    ffn_chunks = (MOD_GATE1, MOD_SHIFT2, MOD_SCALE2, MOD_GATE2)
    mods = [_mod_spec(1, row0, chunk, lambda s, _: ffn_blk(s) // nt) for chunk in ffn_chunks]
    mods_s = [_mod_spec(nbs, row0_s, chunk, lambda s, _: ffn_blk(s)) for chunk in ffn_chunks]
    xs_spec = pl.BlockSpec((nbs, ts, d), lambda s, _: (ffn_blk(s), 0, 0))
    mix_s_spec = pl.BlockSpec((nbs * ts, GLA_WIDTH), lambda s, _: (ffn_blk(s), 0))
    return pl.pallas_call(
        functools.partial(_prompt_kernel, levels=levels, nt=nt, nsteps=nsteps),
        grid_spec=pltpu.PrefetchScalarGridSpec(
            num_scalar_prefetch=1,
            grid=(nsteps + 1,),
            in_specs=[tok(GLA_QK), tok(GLA_QK), tok(GLA_WIDTH), tok(GLA_QK), tok(GLA_WIDTH),
                      tok(SWA_Q), tok(SWA_KV2), tok(SWA_KV2), prev, prev,
                      _resident(tri.shape), _resident(masks.shape), _resident((1, GLA_DV)),
                      xspec, *mods, xs_spec, mix_s_spec, mix_s_spec, *mods_s, _resident((1, 1, d)),
                      _resident((d, d)), _resident((d, D_FF)), _resident((D_FF, d))],
            out_specs=[xspec, xs_spec,
                       pl.BlockSpec((1, GLA_QK, GLA_DV), lambda s, _: (mix_blk(s) // nt, 0, 0))],
            scratch_shapes=[pltpu.VMEM((GLA_DV, GLA_QK), f32),
                            pltpu.VMEM((2, tb, GLA_WIDTH + SWA_Q), bf16)]),
        out_shape=[jax.ShapeDtypeStruct((bsz, t, d), f32),
                   jax.ShapeDtypeStruct(xs.shape, f32),
                   jax.ShapeDtypeStruct((bsz, GLA_QK, GLA_DV), f32)],
        compiler_params=pltpu.CompilerParams(
            dimension_semantics=("arbitrary",), vmem_limit_bytes=VMEM_LIMIT),
        name="prompt_mix_ffn",
    )(sinks, gq, gk, gv, la, gr, sq, k2, v2, k2, v2, tri, masks, gnw,
      x, mod, mod, mod, mod, xs, gos, sos, mod, mod, mod, mod, n2, wout, w1, w2)


def _rope_tables(pos, copies):
    half = HEAD_DIM // 2
    inv = np.power(np.float32(ROPE_THETA), -np.arange(half, dtype=np.float32) * np.float32(2.0 / HEAD_DIM))
    ang = np.asarray(pos, np.float32)[:, None] * inv[None, :].astype(np.float32)
    cos = np.cos(ang).astype(np.float32)
    sin = np.sin(ang).astype(np.float32)
    reps = (copies, LANES // HEAD_DIM)
    return (jnp.asarray(np.tile(np.concatenate([cos, cos], axis=-1), reps)),
            jnp.asarray(np.tile(np.concatenate([-sin, sin], axis=-1), reps)))


def _block_diag_ones(n, blk):
    idx = np.arange(n) // blk
    return jnp.asarray((idx[:, None] == idx[None, :]).astype(np.float32), bf16)


def _layer_weights(w_in, w_gate_up, b_gate, q_norm_w, k_norm_w, w_out, w_ff1, w_ff2):
    splits = np.cumsum([GLA_QK, GLA_QK, GLA_WIDTH, GLA_WIDTH, GLA_GATE_RANK, SWA_Q, SWA_KV])
    gq, gk, gv, gr, glr, sq, sk, sv = jnp.split(w_in, [int(s) for s in splits], axis=1)
    glr = jnp.pad(glr, ((0, 0), (0, LANES - GLA_GATE_RANK)))
    win = jnp.concatenate([gq, gk, gv, gr, sq, sk, sv, glr], axis=1).astype(bf16)
    wgu = jnp.pad(w_gate_up, ((0, LANES - GLA_GATE_RANK), (0, 0))).astype(bf16)
    return dict(
        win=win, wgu=wgu, bg=b_gate.reshape(1, GLA_QK),
        qnw=jnp.tile(q_norm_w, N_Q_HEADS).reshape(1, SWA_Q),
        knw=jnp.tile(k_norm_w, N_KV_HEADS).reshape(1, SWA_KV),
        ffn_f32=(w_out, w_ff1, w_ff2))


def _project(x, mod, row0, pos, lw, n1, bdq, bdk, prompt):
    bsz, t, d = x.shape
    if prompt:
        nb, tb, tail, act, groups, cast = 1, TOKEN_BLOCK, WINDOW, bf16, PROMPT_INPROJ_GROUPS, lw["ffn_f32"]
    else:
        nb, tb, tail, act, groups, cast = SAMPLE_ROWS // t, t, SAMPLE_ROWS, f32, 1, ()
    cos_t, sin_t = _rope_tables(pos, nb)
    return _inproj(x, mod, row0, n1.reshape(1, 1, d), lw["win"], lw["wgu"], lw["bg"], lw["qnw"], lw["knw"],
                   bdq, bdk, cos_t, sin_t, nb, tb, tail, act, groups, cast, tail_t=prompt)


def _decoder_layers(xp, xs, mod, pos_p, pos_s, state, past_k, past_v, lw, n1, n2, gnw, sinks, bdq, bdk):
    bp, tp, d = xp.shape
    bs, ts, _ = xs.shape
    heads = (N_KV_HEADS, HEAD_DIM)
    gq, gk, gv, gr, la, sq, _, _, kk, vk = _project(xs, mod, 0, pos_s, lw, n1, bdq, bdk, False)
    go_s, state_s = _gla_sample(gq, gk, gv, la, gr, state.reshape(bs, GLA_QK, GLA_DV), gnw, bs, ts)
    so_s, kt, vt = _swa_sample(sq, kk, vk, past_k.transpose(0, 2, 3, 1), past_v.transpose(0, 2, 3, 1),
                               sinks, bs, ts)
    gq, gk, gv, gr, la, sq, k2, v2, kk, vk, *ffn_w = _project(xp, mod, bs, pos_p, lw, n1, bdq, bdk, True)
    yp, ys, s_t = _prompt_mix_ffn(xp, gq, gk, gv, la, gr, sq, k2, v2, gnw, sinks, mod, bs,
                                  n2.reshape(1, 1, d), *ffn_w, xs, go_s, so_s, 0)
    state_p = s_t.reshape(bp, GLA_HEADS, GLA_DK, GLA_DV)
    return (yp, ys, state_p,
            kk.reshape(bp, *heads, WINDOW).transpose(0, 3, 1, 2), vk.reshape(bp, *heads, WINDOW).transpose(0, 3, 1, 2),
            state_s.reshape(bs, GLA_HEADS, GLA_DK, GLA_DV),
            kt.transpose(0, 3, 1, 2), vt.transpose(0, 3, 1, 2))


def kernel(x_prompt, x_sample, state_gla, cache_swa_k, cache_swa_v, c_prompt, c_sample, w_ada, b_ada, norm1_w, norm2_w, w_in, w_gate_up, b_gate, gla_norm_w, q_norm_w, k_norm_w, sinks, w_out, w_ff1, w_ff2):
    depth = w_ada.shape[0]
    bp, tp, _ = x_prompt.shape
    bs, ts, _ = x_sample.shape
    pos_p = np.arange(tp)
    pos_s = PAST_LEN + np.arange(ts)
    bdq = _block_diag_ones(SWA_Q, HEAD_DIM)
    bdk = _block_diag_ones(SWA_KV, HEAD_DIM)
    yp, ys = x_prompt, x_sample
    outs = [[] for _ in range(6)]
    for l in range(depth):
        mod = _modulation(c_sample, c_prompt, w_ada[l], b_ada[l])
        lw = _layer_weights(w_in[l], w_gate_up[l], b_gate[l], q_norm_w[l], k_norm_w[l],
                            w_out[l], w_ff1[l], w_ff2[l])
        gnw = gla_norm_w[l].reshape(1, GLA_DV)
        yp, ys, *new = _decoder_layers(yp, ys, mod, pos_p, pos_s, state_gla[l], cache_swa_k[l],
                                       cache_swa_v[l], lw, norm1_w[l], norm2_w[l], gnw, sinks[l], bdq, bdk)
        for lst, val in zip(outs, new):
            lst.append(val)
    return (yp, ys) + tuple(jnp.stack(o) for o in outs)
```

```python
import functools

import jax
import jax.numpy as jnp
import numpy as np
from jax import lax
from jax.experimental import pallas as pl
from jax.experimental.pallas import tpu as pltpu

f32 = jnp.float32
bf16 = jnp.bfloat16

D_MODEL = 1024
GLA_HEADS = 4
GLA_DK = 64
GLA_DV = 128
GLA_QK = GLA_HEADS * GLA_DK
GLA_WIDTH = GLA_HEADS * GLA_DV
GLA_GATE_RANK = 16
GLA_TAU = 16.0
LOG2E = 1.4426950408889634
HEAD_DIM = 64
N_Q_HEADS = 8
N_KV_HEADS = 2
SWA_Q = N_Q_HEADS * HEAD_DIM
SWA_KV = N_KV_HEADS * HEAD_DIM
SWA_KV2 = 2 * SWA_KV
WINDOW = 128
ROPE_THETA = 10000.0
PAST_LEN = 8192
D_FF = 4 * D_MODEL
EPS = 1e-6
LANES = 128
GLA_CHUNK = 128
PROMPT_GLA_CHUNK = 128
TOKEN_BLOCK = 1024
FUSED_TOKEN_BLOCK = 512
SAMPLE_ROWS = 512
PROMPT_INPROJ_GROUPS = 2
VMEM_LIMIT = 56 * 1024 * 1024

_SEG = {}
_off = 0
for _name, _w in (("gq", GLA_QK), ("gk", GLA_QK), ("gv", GLA_WIDTH), ("gr", GLA_WIDTH),
                  ("sq", SWA_Q), ("sk", SWA_KV), ("sv", SWA_KV), ("glr", LANES)):
    _SEG[_name] = (_off, _off + _w)
    _off += _w
IN_WIDTH_PADDED = _off


def _dot(a, b):
    return jnp.dot(a, b, preferred_element_type=f32)


def _dot_nt(a, b):
    return lax.dot_general(a, b, (((1,), (1,)), ((), ())), preferred_element_type=f32)


def _dot_tn(a, b):
    return lax.dot_general(a, b, (((0,), (0,)), ((), ())), preferred_element_type=f32)


def _sigmoid(x):
    return 1.0 / (1.0 + jnp.exp(-x))


MOD_W_SLABS = 4


def _mod_kernel(ca_ref, cb_ref, *refs):
    w_refs, b_ref, o_ref = refs[:-2], refs[-2], refs[-1]
    c = jnp.concatenate([ca_ref[...], cb_ref[...]], axis=0)
    s = (c * _sigmoid(c)).astype(bf16)
    kb = w_refs[0].shape[0]
    res = b_ref[...]
    for i, w_ref in enumerate(w_refs):
        res = res + _dot(s[:, i * kb:(i + 1) * kb], w_ref[...].astype(bf16))
    for r in range(res.shape[0]):
        o_ref[r] = res[r:r + 1, :]


def _modulation(c_a, c_b, w_ada, b_ada):
    m = c_a.shape[0] + c_b.shape[0]
    n = w_ada.shape[1]
    bn = 1536
    kb = D_MODEL // MOD_W_SLABS
    slabs = [pl.BlockSpec((kb, bn), lambda j, i=i: (i, j)) for i in range(MOD_W_SLABS)]
    return pl.pallas_call(
        _mod_kernel,
        grid=(n // bn,),
        in_specs=[pl.BlockSpec(c_a.shape, lambda j: (0, 0)), pl.BlockSpec(c_b.shape, lambda j: (0, 0))]
        + slabs + [pl.BlockSpec((1, bn), lambda j: (0, j))],
        out_specs=pl.BlockSpec((m, 1, bn), lambda j: (0, 0, j)),
        out_shape=jax.ShapeDtypeStruct((m, 1, n), f32),
        compiler_params=pltpu.CompilerParams(vmem_limit_bytes=VMEM_LIMIT),
        name="adaln_mod",
    )(c_a, c_b, *([w_ada] * MOD_W_SLABS), b_ada.reshape(1, n))


def _group_rms(x, bd_ref, w_ref):
    ssq = _dot((x * x).astype(bf16), bd_ref[...])
    return x * lax.rsqrt(ssq * (1.0 / HEAD_DIM) + EPS) * w_ref[...]


def _rope(x, cos, sin_signed, low_half):
    partner = jnp.where(low_half, pltpu.roll(x, LANES - 32, axis=1), pltpu.roll(x, 32, axis=1))
    return x * cos + partner * sin_signed


def _dup_heads(x, low_lanes):
    rolled = pltpu.roll(x, HEAD_DIM, axis=1)
    return jnp.where(low_lanes, x, rolled), jnp.where(low_lanes, rolled, x)


def _inproj_kernel(x_ref, sh_ref, sc_ref, n1_ref, win_ref, wgu_ref, bg_ref, qnw_ref, knw_ref,
                   bdq_ref, bdk_ref, cos_ref, sin_ref, *rest, groups, ncast, tail, tail_t):
    slabs, rest = rest[:ncast], rest[ncast:]
    gq_ref, gk_ref, gv_ref, gr_ref, la_ref, sq_ref, k2_ref, v2_ref, kk_ref, vk_ref = rest[:10]
    for src, dst in zip(slabs, rest[10:]):
        dst[...] = src[...].astype(dst.dtype)
    nb, t, d = x_ref.shape
    m = nb * t
    mg = m // groups
    lane = lax.broadcasted_iota(jnp.int32, (mg, LANES), 1)
    low_half = (lane & 32) == 0
    low_lanes = lane < HEAD_DIM
    for grp in range(groups):
        rows = slice(grp * mg, (grp + 1) * mg)
        if nb == 1:
            x = x_ref[:, rows, :]
            sc, sh = sc_ref[...], sh_ref[...]
        else:
            seqs = slice(grp * (nb // groups), (grp + 1) * (nb // groups))
            x = x_ref[seqs]
            sc, sh = sc_ref[seqs], sh_ref[seqs]
        ms = jnp.mean(x * x, axis=-1, keepdims=True)
        hn = x * lax.rsqrt(ms + EPS) * n1_ref[...]
        hn = hn * (1.0 + sc) + sh
        hb = hn.reshape(mg, d).astype(bf16)

        def seg(name):
            a, b = _SEG[name]
            return _dot(hb, win_ref[:, a:b])

        cos = cos_ref[rows, :]
        sin = sin_ref[rows, :]
        sq = _group_rms(seg("sq"), bdq_ref, qnw_ref)
        for c in range(SWA_Q // LANES):
            blk = _rope(sq[:, c * LANES:(c + 1) * LANES], cos, sin, low_half)
            sq_ref[rows, c * LANES:(c + 1) * LANES] = (blk * (LOG2E * HEAD_DIM ** -0.5)).astype(sq_ref.dtype)
        sk = _rope(_group_rms(seg("sk"), bdk_ref, knw_ref), cos, sin, low_half)
        sv = seg("sv")
        keep = (grp + 1) * mg - (m - tail)
        if keep > 0:
            keep = min(keep, mg)
            dst_rows = slice((grp + 1) * mg - keep - (m - tail), (grp + 1) * mg - (m - tail))
            if tail_t:
                assert keep == tail
                kk_ref[...] = sk[mg - keep:].T
                vk_ref[...] = sv[mg - keep:].T
            else:
                kk_ref[dst_rows, :] = sk[mg - keep:]
                vk_ref[dst_rows, :] = sv[mg - keep:]
        for src, dst in ((sk, k2_ref), (sv, v2_ref)):
            d0, d1 = _dup_heads(src, low_lanes)
            dst[rows, :LANES] = d0.astype(dst.dtype)
            dst[rows, LANES:] = d1.astype(dst.dtype)
        glr = seg("glr").astype(bf16)
        g = _dot(glr, wgu_ref[...]) + bg_ref[...]
        log_sig = jnp.minimum(g, 0.0) - jnp.log1p(jnp.exp(-jnp.abs(g)))
        la_ref[rows, :] = log_sig * (LOG2E / GLA_TAU)
        gq_ref[rows, :] = seg("gq") * (GLA_DK ** -0.5)
        gk_ref[rows, :] = seg("gk")
        gv_ref[rows, :] = seg("gv").astype(gv_ref.dtype)
        gr_ref[rows, :] = seg("gr").astype(gr_ref.dtype)


MOD_SHIFT1, MOD_SCALE1, MOD_GATE1, MOD_SHIFT2, MOD_SCALE2, MOD_GATE2 = range(6)


def _mod_spec(nb, row0, chunk, batch_block):
    return pl.BlockSpec((nb, 1, D_MODEL), lambda *ids: (row0 // nb + batch_block(*ids), 0, chunk))


def _inproj(x, mod, row0, n1, win, wgu, bg, qnw, knw, bdq, bdk, cos_t, sin_t, nb, tb, tail, act, groups,
            to_bf16=(), tail_t=False):
    bsz, t, d = x.shape
    m = nb * tb
    nt = t // tb
    grid = (bsz // nb, nt)
    tok = bsz * t
    nsteps = grid[0] * grid[1]
    slab_specs = [pl.BlockSpec((w.shape[0] // nsteps, w.shape[1]), lambda i, j: (i * nt + j, 0))
                  for w in to_bf16]

    def full(shape):
        return pl.BlockSpec(shape, lambda i, j: (0,) * len(shape))

    def out(width):
        return pl.BlockSpec((m, width), lambda i, j: (i * nt + j, 0))

    tail_blk = (SWA_KV, tail) if tail_t else (tail, SWA_KV)
    tail_spec = pl.BlockSpec(tail_blk, lambda i, j: (i, 0))
    tail_shape = jax.ShapeDtypeStruct(((bsz // nb) * tail_blk[0], tail_blk[1]), f32)
    outs = ((GLA_QK, f32), (GLA_QK, f32), (GLA_WIDTH, act), (GLA_WIDTH, act), (GLA_QK, f32),
            (SWA_Q, act), (SWA_KV2, bf16), (SWA_KV2, bf16))
    return pl.pallas_call(
        functools.partial(_inproj_kernel, groups=groups, ncast=len(to_bf16), tail=tail, tail_t=tail_t),
        grid=grid,
        in_specs=[pl.BlockSpec((nb, tb, d), lambda i, j: (i, j, 0)),
                  _mod_spec(nb, row0, MOD_SHIFT1, lambda i, j: i),
                  _mod_spec(nb, row0, MOD_SCALE1, lambda i, j: i),
                  full((1, 1, d)),
                  full((d, IN_WIDTH_PADDED)),
                  full((LANES, GLA_QK)),
                  full((1, GLA_QK)),
                  full((1, SWA_Q)),
                  full((1, SWA_KV)),
                  full((SWA_Q, SWA_Q)),
                  full((SWA_KV, SWA_KV)),
                  pl.BlockSpec((m, LANES), lambda i, j: (j, 0)),
                  pl.BlockSpec((m, LANES), lambda i, j: (j, 0))] + slab_specs,
        out_specs=[out(w) for w, _ in outs] + [tail_spec, tail_spec] + slab_specs,
        out_shape=[jax.ShapeDtypeStruct((tok, w), dt) for w, dt in outs]
        + [tail_shape] * 2
        + [jax.ShapeDtypeStruct(w.shape, bf16) for w in to_bf16],
        compiler_params=pltpu.CompilerParams(
            dimension_semantics=("parallel", "arbitrary"), vmem_limit_bytes=VMEM_LIMIT),
        name="inproj",
    )(x, mod, mod, n1, win, wgu, bg, qnw, knw, bdq, bdk, cos_t, sin_t, *to_bf16)


def _gla_constants(chunk, seq):
    t = np.arange(chunk)
    levels = []
    m = 1
    while m < seq:
        levels.append(m)
        m *= 2
    masks = [np.eye(chunk, dtype=bool)]
    for m in levels:
        upper = (t % (2 * m) >= m)[:, None]
        lower = (t % (2 * m) < m)[None, :]
        same = (t[:, None] // (2 * m)) == (t[None, :] // (2 * m))
        masks.append(same & upper & lower)
    tri = t[None, :] <= t[:, None]
    tiled = np.tile(np.stack(masks).astype(np.float32), (1, 1, GLA_HEADS))
    return tuple(levels), jnp.asarray(tri.astype(np.float32), bf16), jnp.asarray(tiled)


def _stack_heads(xb, lane_head, axis):
    zero = jnp.zeros_like(xb)
    return jnp.concatenate([jnp.where(lane_head == h, xb, zero) for h in range(GLA_HEADS)], axis=axis)


def _block_sums(la, bcum, m, row, rolls):
    c, n = la.shape
    if m == 1:
        return la, None
    if m < 8:
        def rolled(shift):
            if shift not in rolls:
                rolls[shift] = pltpu.roll(la, shift % c, axis=0)
            return rolls[shift]
        pos = row & (m - 1)
        pre = la
        suf = None
        for j in range(1, m):
            pre = pre + jnp.where(pos >= j, rolled(j), 0.0)
            term = jnp.where(pos < m - j, rolled(-j), 0.0)
            suf = term if suf is None else suf + term
        return pre, suf
    before, last = [], []
    for i in range(c // m):
        before.append(jnp.zeros((m, n), f32) if i == 0
                      else jnp.broadcast_to(bcum[i * m - 1:i * m], (m, n)))
        last.append(jnp.broadcast_to(bcum[(i + 1) * m - 1:(i + 1) * m], (m, n)))
    if len(before) == 1:
        return bcum - before[0], last[0] - bcum
    return bcum - jnp.concatenate(before, axis=0), jnp.concatenate(last, axis=0) - bcum


def _gla_scores(q_ref, k_ref, v_ref, la_ref, tri_ref, masks_ref, levels, seq):
    c = q_ref.shape[0]
    la = la_ref[...]
    hi = la.astype(bf16)
    lo = (la - hi.astype(f32)).astype(bf16)
    tri = tri_ref[...]
    bcum = _dot(tri, hi) + _dot(tri, lo)
    q = q_ref[...]
    k = k_ref[...]
    vb = v_ref[...].astype(bf16)
    row = lax.broadcasted_iota(jnp.int32, (c, GLA_QK), 0)
    row_head_t = lax.broadcasted_iota(jnp.int32, (GLA_QK, c), 0) >> 6
    rolls = {}
    attn = None
    for lvl, m in enumerate((0,) + levels):
        if m == 0:
            qt, kt = q, k
        else:
            pre, suf = _block_sums(la, bcum, m, row, rolls)
            qt = q * jnp.exp2(pre)
            kt = k if suf is None else k * jnp.exp2(suf)
        kt_t = kt.T.astype(bf16)
        r = _dot(qt.astype(bf16), _stack_heads(kt_t, row_head_t, 1))
        rm = r * masks_ref[lvl]
        attn = rm if attn is None else attn + rm
    a = attn.astype(bf16)
    pre, suf = _block_sums(la, bcum, seq, row, rolls)
    qf = (q * jnp.exp2(pre)).astype(bf16)
    kf = (k * jnp.exp2(suf)).astype(bf16)
    return a, qf, kf, vb, bcum, hi, lo


def _gla_values(a, vb):
    v_head = lax.broadcasted_iota(jnp.int32, vb.shape, 1) >> 7
    return _dot(a, _stack_heads(vb, v_head, 0))


def _gla_finish(o, r_ref, gnw, go_ref):
    for h in range(GLA_HEADS):
        sl = slice(h * GLA_DV, (h + 1) * GLA_DV)
        oh = o[:, sl]
        r = r_ref[:, sl].astype(f32)
        ms = jnp.mean(oh * oh, axis=-1, keepdims=True)
        go_ref[:, sl] = (oh * lax.rsqrt(ms + EPS) * gnw * (r * _sigmoid(r))).astype(go_ref.dtype)


def _gla_prompt_block(st, q_ref, k_ref, v_ref, la_ref, r_ref, tri_ref, masks_ref, gnw, go_ref, levels,
                      between=lambda: None):
    c = PROMPT_GLA_CHUNK
    nchunk = q_ref.shape[0] // c
    lane_head = lax.broadcasted_iota(jnp.int32, (GLA_DV, GLA_QK), 1) >> 6
    row_head = lax.broadcasted_iota(jnp.int32, (GLA_QK, GLA_DV), 0) >> 6
    chunks = []
    for ci in range(nchunk):
        rows = pl.ds(ci * c, c)
        chunks.append(_gla_scores(q_ref.at[rows], k_ref.at[rows], v_ref.at[rows], la_ref.at[rows],
                                  tri_ref, masks_ref, levels, c))
        between()
    intra, upds, decays = [], [], []
    for a, _, kf, vb, bcum, _, _ in chunks:
        intra.append(_gla_values(a, vb))
        full = _dot_tn(vb, kf)
        upd = None
        for h in range(GLA_HEADS):
            term = jnp.where(lane_head == h, full[h * GLA_DV:(h + 1) * GLA_DV], 0.0)
            upd = term if upd is None else upd + term
        upds.append(upd)
        decays.append(jnp.exp2(bcum[c - 1:c]))
    between()
    for ci in range(nchunk):
        rows = pl.ds(ci * c, c)
        sbd = _stack_heads(st.T.astype(bf16), row_head, 1)
        o = intra[ci] + _dot(chunks[ci][1], sbd)
        _gla_finish(o, r_ref.at[rows], gnw, go_ref.at[rows])
        st = decays[ci] * st + upds[ci]
    return st


def _gla_sample_kernel(q_ref, k_ref, v_ref, la_ref, r_ref, s0_ref, tri_ref, masks_ref, gnw_ref,
                       go_ref, sout_ref, *, levels, seq):
    c = q_ref.shape[0]
    a, qf, kf, vb, _, hi, lo = _gla_scores(q_ref, k_ref, v_ref, la_ref, tri_ref, masks_ref, levels, seq)
    o_intra = _gla_values(a, vb)
    row_head = lax.broadcasted_iota(jnp.int32, (GLA_QK, GLA_DV), 0) >> 6
    ones = jnp.ones((seq, GLA_DV), bf16)
    inter = []
    for b in range(c // seq):
        rows = slice(b * seq, (b + 1) * seq)
        s_old = s0_ref[b]
        sbd = _stack_heads(s_old.astype(bf16), row_head, 1)
        inter.append(_dot(qf[rows], sbd))
        p = _dot_tn(kf[rows], vb[rows])
        upd = None
        for h in range(GLA_HEADS):
            term = jnp.where(row_head == h, p[:, h * GLA_DV:(h + 1) * GLA_DV], 0.0)
            upd = term if upd is None else upd + term
        total = _dot_tn(hi[rows], ones) + _dot_tn(lo[rows], ones)
        sout_ref[b] = jnp.exp2(total) * s_old + upd
    o = o_intra + jnp.concatenate(inter, axis=0)
    _gla_finish(o, r_ref, gnw_ref[...], go_ref)


def _gla_sample(gq, gk, gv, la, gr, state, gnw, bsz, t):
    c = GLA_CHUNK
    nseq = c // t
    levels, tri, masks = _gla_constants(c, t)

    def tok(width):
        return pl.BlockSpec((c, width), lambda i: (i, 0))

    return pl.pallas_call(
        functools.partial(_gla_sample_kernel, levels=levels, seq=t),
        grid=(bsz // nseq,),
        in_specs=[tok(GLA_QK), tok(GLA_QK), tok(GLA_WIDTH), tok(GLA_QK), tok(GLA_WIDTH),
                  pl.BlockSpec((nseq, GLA_QK, GLA_DV), lambda i: (i, 0, 0)),
                  pl.BlockSpec(tri.shape, lambda i: (0, 0)),
                  pl.BlockSpec(masks.shape, lambda i: (0, 0, 0)),
                  pl.BlockSpec((1, GLA_DV), lambda i: (0, 0))],
        out_specs=[tok(GLA_WIDTH),
                   pl.BlockSpec((nseq, GLA_QK, GLA_DV), lambda i: (i, 0, 0))],
        out_shape=[jax.ShapeDtypeStruct((bsz * t, GLA_WIDTH), f32),
                   jax.ShapeDtypeStruct((bsz, GLA_QK, GLA_DV), f32)],
        compiler_params=pltpu.CompilerParams(
            dimension_semantics=("parallel",), vmem_limit_bytes=VMEM_LIMIT),
        name="gla_sample",
    )(gq, gk, gv, la, gr, state, tri, masks, gnw)


def _sink_attention(q_ref, rows, kdups, vdups, mask, sink_ref, o_ref, between=lambda: None):
    n = len(kdups)
    heads_per_group = N_Q_HEADS // N_KV_HEADS
    low_q = lax.broadcasted_iota(jnp.int32, (rows, LANES), 1) < HEAD_DIM
    scores = []
    for i in range(n):
        per_group = []
        for g in range(N_KV_HEADS):
            stack = []
            for p in range(heads_per_group // 2):
                pair = g * (heads_per_group // 2) + p
                qp = q_ref[i * rows:(i + 1) * rows, pair * LANES:(pair + 1) * LANES].astype(bf16)
                zero = jnp.zeros_like(qp)
                stack.append(jnp.where(low_q, qp, zero))
                stack.append(jnp.where(low_q, zero, qp))
            per_group.append(_dot_nt(jnp.concatenate(stack, axis=0), kdups[i][g]))
        scores.append(per_group)
    between()
    probs, inv = [], []
    for head in range(N_Q_HEADS):
        g, hh = divmod(head, heads_per_group)
        parts = [scores[i][g][hh * rows:(hh + 1) * rows] for i in range(n)]
        s = parts[0] if n == 1 else jnp.concatenate(parts, axis=0)
        sink = sink_ref[head] * LOG2E
        s = jnp.where(mask, s, -1e30)
        mx = jnp.maximum(jnp.max(s, axis=-1, keepdims=True), sink)
        p = jnp.exp2(s - mx)
        inv.append(1.0 / (jnp.sum(p, axis=-1, keepdims=True) + jnp.exp2(sink - mx)))
        probs.append(p.astype(bf16))
    between()
    outs = []
    for i in range(n):
        per_group = []
        for g in range(N_KV_HEADS):
            p_i = jnp.concatenate([probs[g * heads_per_group + hh][i * rows:(i + 1) * rows]
                                   for hh in range(heads_per_group)], axis=0)
            per_group.append(_dot(p_i, vdups[i][g]))
        outs.append(per_group)
    low_all = lax.broadcasted_iota(jnp.int32, (n * rows, LANES), 1) < HEAD_DIM
    for pair in range(N_Q_HEADS // 2):
        halves = []
        for head in (2 * pair, 2 * pair + 1):
            g, hh = divmod(head, heads_per_group)
            parts = [outs[i][g][hh * rows:(hh + 1) * rows] for i in range(n)]
            o = parts[0] if n == 1 else jnp.concatenate(parts, axis=0)
            halves.append(o * inv[head])
        o_ref[:, pair * LANES:(pair + 1) * LANES] = jnp.where(low_all, halves[0], halves[1]).astype(o_ref.dtype)


def _swa_prompt_blocks(q_ref, k_before, k2_ref, v_before, v2_ref, first, sink_ref, o_ref,
                       between=lambda: None):
    w = WINDOW
    nblk = q_ref.shape[0] // w

    def dup(before, ref):
        blocks = [before] + [ref[i * w:(i + 1) * w, :] for i in range(nblk)]
        return [[jnp.concatenate([blocks[i][:, g * LANES:(g + 1) * LANES],
                                  blocks[i + 1][:, g * LANES:(g + 1) * LANES]], axis=0)
                 for g in range(N_KV_HEADS)] for i in range(nblk)]

    row = lax.broadcasted_iota(jnp.int32, (nblk * w, 2 * w), 0)
    tk = lax.broadcasted_iota(jnp.int32, (nblk * w, 2 * w), 1)
    rel = tk - (row & (w - 1))
    first_key = jnp.where(row < w, jnp.where(first, w, 0), 0)
    mask = (rel > 0) & (rel <= w) & (tk >= first_key)
    _sink_attention(q_ref, w, dup(k_before, k2_ref), dup(v_before, v2_ref), mask, sink_ref, o_ref, between)


def _swa_sample_kernel(sink_ref, q_ref, kn_ref, vn_ref, pk_ref, pv_ref, o_ref, ko_ref, vo_ref, *, seq):
    nseq, _, _, w = pk_ref.shape
    rows = nseq * seq
    heads_per_group = N_Q_HEADS // N_KV_HEADS
    lane = lax.broadcasted_iota(jnp.int32, (rows, LANES), 1)
    r_id = lax.broadcasted_iota(jnp.int32, (rows, LANES), 0)
    low = lane < HEAD_DIM
    pos = r_id & (seq - 1)
    mask_old = lane > pos
    seq_shift = seq.bit_length() - 1
    mask_new = ((r_id >> seq_shift) == (lane >> seq_shift)) & ((lane & (seq - 1)) <= pos)
    kn = kn_ref[...]
    vn = vn_ref[...]
    kn_dup = [d.astype(bf16) for d in _dup_heads(kn, low)]
    vn_dup = [d.astype(bf16) for d in _dup_heads(vn, low)]

    pad = jnp.zeros((w - seq, LANES), f32)
    tail_lanes = lax.broadcasted_iota(jnp.int32, (HEAD_DIM, w), 1) >= w - seq
    kt_dup, vt_dup = [], []
    for b in range(nseq):
        tok = slice(b * seq, (b + 1) * seq)
        per_k, per_v = [], []
        for new, old_ref, out_ref, per in ((kn, pk_ref, ko_ref, per_k), (vn, pv_ref, vo_ref, per_v)):
            new_t = jnp.concatenate([pad, new[tok]], axis=0).T
            for g in range(N_KV_HEADS):
                old = old_ref[b, g]
                out_ref[b, g] = jnp.where(tail_lanes, new_t[g * HEAD_DIM:(g + 1) * HEAD_DIM],
                                          pltpu.roll(old, w - seq, axis=1))
                ob = old.astype(bf16)
                per.append(jnp.concatenate([ob, ob], axis=0))
        kt_dup.append(per_k)
        vt_dup.append(per_v)

    lhs = []
    for g in range(N_KV_HEADS):
        stack = []
        for p in range(heads_per_group // 2):
            pair = g * (heads_per_group // 2) + p
            qp = q_ref[:, pair * LANES:(pair + 1) * LANES].astype(bf16)
            zero = jnp.zeros_like(qp)
            stack.append(jnp.where(low, qp, zero))
            stack.append(jnp.where(low, zero, qp))
        lhs.append(jnp.concatenate(stack, axis=0))
    s_new = [_dot_nt(lhs[g], kn_dup[g]) for g in range(N_KV_HEADS)]
    s_old = []
    for b in range(nseq):
        per = []
        for g in range(N_KV_HEADS):
            qb = jnp.concatenate([lhs[g][hh * rows + b * seq:hh * rows + (b + 1) * seq]
                                  for hh in range(heads_per_group)], axis=0)
            per.append(_dot(qb, kt_dup[b][g]))
        s_old.append(per)

    p_old, p_new, inv = [], [], []
    for head in range(N_Q_HEADS):
        g, hh = divmod(head, heads_per_group)
        so = jnp.concatenate([s_old[b][g][hh * seq:(hh + 1) * seq] for b in range(nseq)], axis=0)
        so = jnp.where(mask_old, so, -1e30)
        sn = jnp.where(mask_new, s_new[g][hh * rows:(hh + 1) * rows], -1e30)
        sink = sink_ref[head] * LOG2E
        mx = jnp.maximum(jnp.maximum(jnp.max(so, axis=-1, keepdims=True),
                                     jnp.max(sn, axis=-1, keepdims=True)), sink)
        po = jnp.exp2(so - mx)
        pn = jnp.exp2(sn - mx)
        inv.append(1.0 / (jnp.sum(po, axis=-1, keepdims=True) + jnp.sum(pn, axis=-1, keepdims=True)
                          + jnp.exp2(sink - mx)))
        p_old.append(po.astype(bf16))
        p_new.append(pn.astype(bf16))

    o_new = [_dot(jnp.concatenate([p_new[g * heads_per_group + hh] for hh in range(heads_per_group)], axis=0),
                  vn_dup[g]) for g in range(N_KV_HEADS)]
    o_old = []
    for b in range(nseq):
        per = []
        for g in range(N_KV_HEADS):
            pb = jnp.concatenate([p_old[g * heads_per_group + hh][b * seq:(b + 1) * seq]
                                  for hh in range(heads_per_group)], axis=0)
            per.append(_dot_nt(pb, vt_dup[b][g]))
        o_old.append(per)

    for pair in range(N_Q_HEADS // 2):
        halves = []
        for head in (2 * pair, 2 * pair + 1):
            g, hh = divmod(head, heads_per_group)
            old = jnp.concatenate([o_old[b][g][hh * seq:(hh + 1) * seq] for b in range(nseq)], axis=0)
            halves.append((old + o_new[g][hh * rows:(hh + 1) * rows]) * inv[head])
        o_ref[:, pair * LANES:(pair + 1) * LANES] = jnp.where(low, halves[0], halves[1])


def _swa_sample(sq, sk, sv, past_kt, past_vt, sinks, bsz, t):
    nseq = LANES // t
    tok = lambda width: pl.BlockSpec((nseq * t, width), lambda i, s: (i, 0))
    cache = pl.BlockSpec((nseq,) + past_kt.shape[1:], lambda i, s: (i, 0, 0, 0))
    return pl.pallas_call(
        functools.partial(_swa_sample_kernel, seq=t),
        grid_spec=pltpu.PrefetchScalarGridSpec(
            num_scalar_prefetch=1,
            grid=(bsz // nseq,),
            in_specs=[tok(SWA_Q), tok(SWA_KV), tok(SWA_KV), cache, cache],
            out_specs=[tok(SWA_Q), cache, cache]),
        out_shape=[jax.ShapeDtypeStruct((bsz * t, SWA_Q), f32),
                   jax.ShapeDtypeStruct(past_kt.shape, f32),
                   jax.ShapeDtypeStruct(past_vt.shape, f32)],
        compiler_params=pltpu.CompilerParams(
            dimension_semantics=("parallel",), vmem_limit_bytes=VMEM_LIMIT),
        name="swa_sample",
    )(sinks, sq, sk, sv, past_kt, past_vt)


FFN_CHUNK = 512


def _out_proj_norm(streams, n2, wout_ref):
    mix = [s[1] for s in streams]
    mix = mix[0] if len(mix) == 1 else jnp.concatenate(mix, axis=0)
    mixed = _dot(mix[:, :GLA_WIDTH], wout_ref[:GLA_WIDTH, :]) + _dot(mix[:, GLA_WIDTH:], wout_ref[GLA_WIDTH:, :])
    hs, hbs, r0 = [], [], 0
    for x, _, g1, sh2, sc2 in streams:
        nb, t, d = x.shape
        h = x + g1 * mixed[r0:r0 + nb * t].reshape(nb, t, d)
        ms = jnp.mean(h * h, axis=-1, keepdims=True)
        hn = h * lax.rsqrt(ms + EPS) * n2
        hn = hn * (1.0 + sc2) + sh2
        hs.append(h)
        hbs.append(hn.reshape(nb * t, d).astype(bf16))
        r0 += nb * t
    return hs, (hbs[0] if len(hbs) == 1 else jnp.concatenate(hbs, axis=0))


def _ffn_piece(hb, w1_ref, w2_ref, c, width):
    cols = slice(c * width, (c + 1) * width)
    a = jnp.maximum(_dot(hb, w1_ref[:, cols]), 0.0)
    return _dot((a * a).astype(bf16), w2_ref[cols, :])


def _resident(shape):
    return pl.BlockSpec(shape, lambda *_: (0,) * len(shape), pipeline_mode=pl.Buffered(1))


def _prompt_kernel(sink_ref,
                   q_ref, k_ref, v_ref, la_ref, r_ref, sq_ref, k2_ref, v2_ref, k2p_ref, v2p_ref,
                   tri_ref, masks_ref, gnw_ref,
                   x_ref, g1_ref, sh2_ref, sc2_ref, g2_ref,
                   xs_ref, gos_ref, sos_ref, g1s_ref, sh2s_ref, sc2s_ref, g2s_ref,
                   n2_ref, wout_ref, w1_ref, w2_ref,
                   y_ref, ys_ref, sout_ref, state_ref, mix_ref, *, levels, nt, nsteps):
    s = pl.program_id(0)
    slot = s % 2

    @pl.when(s == 0)
    def _():
        mix_ref[1] = jnp.zeros(mix_ref.shape[1:], mix_ref.dtype)
        state_ref[...] = jnp.zeros_like(state_ref)

    j = jnp.minimum(s, nsteps - 1) % nt
    st_in = state_ref[...]
    st = jnp.where(j == 0, 0.0, st_in)
    gnw = gnw_ref[...]
    out = mix_ref.at[slot]
    mix_s = jnp.concatenate([gos_ref[...], sos_ref[...]], axis=1).astype(bf16)
    (h, h_s), hb = _out_proj_norm(
        [(x_ref[...], mix_ref[1 - slot], g1_ref[...], sh2_ref[...], sc2_ref[...]),
         (xs_ref[...], mix_s, g1s_ref[...], sh2s_ref[...], sc2s_ref[...])], n2_ref[...], wout_ref)
    rows_p = h.shape[0] * h.shape[1]
    todo = list(range(D_FF // FFN_CHUNK))
    ff = []

    def ffn_piece():
        if todo:
            part = _ffn_piece(hb, w1_ref, w2_ref, todo.pop(0), FFN_CHUNK)
            ff[:] = [part if not ff else ff[0] + part]

    st = _gla_prompt_block(st, q_ref, k_ref, v_ref, la_ref, r_ref, tri_ref, masks_ref, gnw,
                           out.at[:, pl.ds(0, GLA_WIDTH)], levels, ffn_piece)
    _swa_prompt_blocks(sq_ref, k2p_ref[...], k2_ref, v2p_ref[...], v2_ref, j == 0, sink_ref,
                       out.at[:, pl.ds(GLA_WIDTH, SWA_Q)], ffn_piece)
    while todo:
        ffn_piece()
    st = jnp.where(s < nsteps, st, st_in)
    state_ref[...] = st
    sout_ref[0] = st.T
    y_ref[...] = h + g2_ref[...] * ff[0][:rows_p].reshape(h.shape)
    ys_ref[...] = h_s + g2s_ref[...] * ff[0][rows_p:].reshape(h_s.shape)


def _prompt_mix_ffn(x, gq, gk, gv, la, gr, sq, k2, v2, gnw, sinks, mod, row0, n2, wout, w1, w2,
                    xs, gos, sos, row0_s):
    bsz, t, d = x.shape
    tb = FUSED_TOKEN_BLOCK
    nt = t // tb
    nsteps = bsz * nt
    levels, tri, masks = _gla_constants(PROMPT_GLA_CHUNK, PROMPT_GLA_CHUNK)
    per_blk = tb // WINDOW
    bs, ts, _ = xs.shape
    nbs = bs // nsteps
    assert nbs * nsteps == bs and (nbs * ts) % 16 == 0

    def mix_blk(s):
        return jnp.minimum(s, nsteps - 1)

    def ffn_blk(s):
        return jnp.maximum(s - 1, 0)

    def tok(width):
        return pl.BlockSpec((tb, width), lambda s, _: (mix_blk(s), 0))

    prev = pl.BlockSpec((WINDOW, SWA_KV2), lambda s, _: (jnp.maximum(mix_blk(s) * per_blk - 1, 0), 0))
    xspec = pl.BlockSpec((1, tb, d), lambda s, _: (ffn_blk(s) // nt, ffn_blk(s) % nt, 0))
    ffn_chunks = (MOD_GATE1, MOD_SHIFT2, MOD_SCALE2, MOD_GATE2)
    mods = [_mod_spec(1, row0, chunk, lambda s, _: ffn_blk(s) // nt) for chunk in ffn_chunks]
    mods_s = [_mod_spec(nbs, row0_s, chunk, lambda s, _: ffn_blk(s)) for chunk in ffn_chunks]
    xs_spec = pl.BlockSpec((nbs, ts, d), lambda s, _: (ffn_blk(s), 0, 0))
    mix_s_spec = pl.BlockSpec((nbs * ts, GLA_WIDTH), lambda s, _: (ffn_blk(s), 0))
    return pl.pallas_call(
        functools.partial(_prompt_kernel, levels=levels, nt=nt, nsteps=nsteps),
        grid_spec=pltpu.PrefetchScalarGridSpec(
            num_scalar_prefetch=1,
            grid=(nsteps + 1,),
            in_specs=[tok(GLA_QK), tok(GLA_QK), tok(GLA_WIDTH), tok(GLA_QK), tok(GLA_WIDTH),
                      tok(SWA_Q), tok(SWA_KV2), tok(SWA_KV2), prev, prev,
                      _resident(tri.shape), _resident(masks.shape), _resident((1, GLA_DV)),
                      xspec, *mods, xs_spec, mix_s_spec, mix_s_spec, *mods_s, _resident((1, 1, d)),
                      _resident((d, d)), _resident((d, D_FF)), _resident((D_FF, d))],
            out_specs=[xspec, xs_spec,
                       pl.BlockSpec((1, GLA_QK, GLA_DV), lambda s, _: (mix_blk(s) // nt, 0, 0))],
            scratch_shapes=[pltpu.VMEM((GLA_DV, GLA_QK), f32),
                            pltpu.VMEM((2, tb, GLA_WIDTH + SWA_Q), bf16)]),
        out_shape=[jax.ShapeDtypeStruct((bsz, t, d), f32),
                   jax.ShapeDtypeStruct(xs.shape, f32),
                   jax.ShapeDtypeStruct((bsz, GLA_QK, GLA_DV), f32)],
        compiler_params=pltpu.CompilerParams(
            dimension_semantics=("arbitrary",), vmem_limit_bytes=VMEM_LIMIT),
        name="prompt_mix_ffn",
    )(sinks, gq, gk, gv, la, gr, sq, k2, v2, k2, v2, tri, masks, gnw,
      x, mod, mod, mod, mod, xs, gos, sos, mod, mod, mod, mod, n2, wout, w1, w2)


def _rope_tables(pos, copies):
    half = HEAD_DIM // 2
    inv = np.power(np.float32(ROPE_THETA), -np.arange(half, dtype=np.float32) * np.float32(2.0 / HEAD_DIM))
    ang = np.asarray(pos, np.float32)[:, None] * inv[None, :].astype(np.float32)
    cos = np.cos(ang).astype(np.float32)
    sin = np.sin(ang).astype(np.float32)
    reps = (copies, LANES // HEAD_DIM)
    return (jnp.asarray(np.tile(np.concatenate([cos, cos], axis=-1), reps)),
            jnp.asarray(np.tile(np.concatenate([-sin, sin], axis=-1), reps)))


def _block_diag_ones(n, blk):
    idx = np.arange(n) // blk
    return jnp.asarray((idx[:, None] == idx[None, :]).astype(np.float32), bf16)


def _layer_weights(w_in, w_gate_up, b_gate, q_norm_w, k_norm_w, w_out, w_ff1, w_ff2):
    splits = np.cumsum([GLA_QK, GLA_QK, GLA_WIDTH, GLA_WIDTH, GLA_GATE_RANK, SWA_Q, SWA_KV])
    gq, gk, gv, gr, glr, sq, sk, sv = jnp.split(w_in, [int(s) for s in splits], axis=1)
    glr = jnp.pad(glr, ((0, 0), (0, LANES - GLA_GATE_RANK)))
    win = jnp.concatenate([gq, gk, gv, gr, sq, sk, sv, glr], axis=1).astype(bf16)
    wgu = jnp.pad(w_gate_up, ((0, LANES - GLA_GATE_RANK), (0, 0))).astype(bf16)
    return dict(
        win=win, wgu=wgu, bg=b_gate.reshape(1, GLA_QK),
        qnw=jnp.tile(q_norm_w, N_Q_HEADS).reshape(1, SWA_Q),
        knw=jnp.tile(k_norm_w, N_KV_HEADS).reshape(1, SWA_KV),
        ffn_f32=(w_out, w_ff1, w_ff2))


def _project(x, mod, row0, pos, lw, n1, bdq, bdk, prompt):
    bsz, t, d = x.shape
    if prompt:
        nb, tb, tail, act, groups, cast = 1, TOKEN_BLOCK, WINDOW, bf16, PROMPT_INPROJ_GROUPS, lw["ffn_f32"]
    else:
        nb, tb, tail, act, groups, cast = SAMPLE_ROWS // t, t, SAMPLE_ROWS, f32, 1, ()
    cos_t, sin_t = _rope_tables(pos, nb)
    return _inproj(x, mod, row0, n1.reshape(1, 1, d), lw["win"], lw["wgu"], lw["bg"], lw["qnw"], lw["knw"],
                   bdq, bdk, cos_t, sin_t, nb, tb, tail, act, groups, cast, tail_t=prompt)


def _decoder_layers(xp, xs, mod, pos_p, pos_s, state, past_k, past_v, lw, n1, n2, gnw, sinks, bdq, bdk):
    bp, tp, d = xp.shape
    bs, ts, _ = xs.shape
    heads = (N_KV_HEADS, HEAD_DIM)
    gq, gk, gv, gr, la, sq, _, _, kk, vk = _project(xs, mod, 0, pos_s, lw, n1, bdq, bdk, False)
    go_s, state_s = _gla_sample(gq, gk, gv, la, gr, state.reshape(bs, GLA_QK, GLA_DV), gnw, bs, ts)
    so_s, kt, vt = _swa_sample(sq, kk, vk, past_k.transpose(0, 2, 3, 1), past_v.transpose(0, 2, 3, 1),
                               sinks, bs, ts)
    gq, gk, gv, gr, la, sq, k2, v2, kk, vk, *ffn_w = _project(xp, mod, bs, pos_p, lw, n1, bdq, bdk, True)
    yp, ys, s_t = _prompt_mix_ffn(xp, gq, gk, gv, la, gr, sq, k2, v2, gnw, sinks, mod, bs,
                                  n2.reshape(1, 1, d), *ffn_w, xs, go_s, so_s, 0)
    state_p = s_t.reshape(bp, GLA_HEADS, GLA_DK, GLA_DV)
    return (yp, ys, state_p,
            kk.reshape(bp, *heads, WINDOW).transpose(0, 3, 1, 2), vk.reshape(bp, *heads, WINDOW).transpose(0, 3, 1, 2),
            state_s.reshape(bs, GLA_HEADS, GLA_DK, GLA_DV),
            kt.transpose(0, 3, 1, 2), vt.transpose(0, 3, 1, 2))


def kernel(x_prompt, x_sample, state_gla, cache_swa_k, cache_swa_v, c_prompt, c_sample, w_ada, b_ada, norm1_w, norm2_w, w_in, w_gate_up, b_gate, gla_norm_w, q_norm_w, k_norm_w, sinks, w_out, w_ff1, w_ff2):
    depth = w_ada.shape[0]
    bp, tp, _ = x_prompt.shape
    bs, ts, _ = x_sample.shape
    pos_p = np.arange(tp)
    pos_s = PAST_LEN + np.arange(ts)
    bdq = _block_diag_ones(SWA_Q, HEAD_DIM)
    bdk = _block_diag_ones(SWA_KV, HEAD_DIM)
    yp, ys = x_prompt, x_sample
    outs = [[] for _ in range(6)]
    for l in range(depth):
        mod = _modulation(c_sample, c_prompt, w_ada[l], b_ada[l])
        lw = _layer_weights(w_in[l], w_gate_up[l], b_gate[l], q_norm_w[l], k_norm_w[l],
                            w_out[l], w_ff1[l], w_ff2[l])
        gnw = gla_norm_w[l].reshape(1, GLA_DV)
        yp, ys, *new = _decoder_layers(yp, ys, mod, pos_p, pos_s, state_gla[l], cache_swa_k[l],
                                       cache_swa_v[l], lw, norm1_w[l], norm2_w[l], gnw, sinks[l], bdq, bdk)
        for lst, val in zip(outs, new):
            lst.append(val)
    return (yp, ys) + tuple(jnp.stack(o) for o in outs)
```

```python
import functools

import jax
import jax.numpy as jnp
import numpy as np
from jax import lax
from jax.experimental import pallas as pl
from jax.experimental.pallas import tpu as pltpu

f32 = jnp.float32
bf16 = jnp.bfloat16

D_MODEL = 1024
GLA_HEADS = 4
GLA_DK = 64
GLA_DV = 128
GLA_QK = GLA_HEADS * GLA_DK
GLA_WIDTH = GLA_HEADS * GLA_DV
GLA_GATE_RANK = 16
GLA_TAU = 16.0
LOG2E = 1.4426950408889634
HEAD_DIM = 64
N_Q_HEADS = 8
N_KV_HEADS = 2
SWA_Q = N_Q_HEADS * HEAD_DIM
SWA_KV = N_KV_HEADS * HEAD_DIM
SWA_KV2 = 2 * SWA_KV
WINDOW = 128
ROPE_THETA = 10000.0
PAST_LEN = 8192
D_FF = 4 * D_MODEL
EPS = 1e-6
LANES = 128
GLA_CHUNK = 128
PROMPT_GLA_CHUNK = 128
TOKEN_BLOCK = 1024
FUSED_TOKEN_BLOCK = 512
SAMPLE_ROWS = 512
PROMPT_INPROJ_GROUPS = 2
VMEM_LIMIT = 56 * 1024 * 1024

_SEG = {}
_off = 0
for _name, _w in (("gq", GLA_QK), ("gk", GLA_QK), ("gv", GLA_WIDTH), ("gr", GLA_WIDTH),
                  ("sq", SWA_Q), ("sk", SWA_KV), ("sv", SWA_KV), ("glr", LANES)):
    _SEG[_name] = (_off, _off + _w)
    _off += _w
IN_WIDTH_PADDED = _off


def _dot(a, b):
    return jnp.dot(a, b, preferred_element_type=f32)


def _dot_nt(a, b):
    return lax.dot_general(a, b, (((1,), (1,)), ((), ())), preferred_element_type=f32)


def _dot_tn(a, b):
    return lax.dot_general(a, b, (((0,), (0,)), ((), ())), preferred_element_type=f32)


def _sigmoid(x):
    return 1.0 / (1.0 + jnp.exp(-x))


MOD_W_SLABS = 4


def _mod_kernel(ca_ref, cb_ref, *refs):
    w_refs, b_ref, o_ref = refs[:-2], refs[-2], refs[-1]
    c = jnp.concatenate([ca_ref[...], cb_ref[...]], axis=0)
    s = (c * _sigmoid(c)).astype(bf16)
    kb = w_refs[0].shape[0]
    res = b_ref[...]
    for i, w_ref in enumerate(w_refs):
        res = res + _dot(s[:, i * kb:(i + 1) * kb], w_ref[...].astype(bf16))
    for r in range(res.shape[0]):
        o_ref[r] = res[r:r + 1, :]


def _modulation(c_a, c_b, w_ada, b_ada):
    m = c_a.shape[0] + c_b.shape[0]
    n = w_ada.shape[1]
    bn = 1536
    kb = D_MODEL // MOD_W_SLABS
    slabs = [pl.BlockSpec((kb, bn), lambda j, i=i: (i, j)) for i in range(MOD_W_SLABS)]
    return pl.pallas_call(
        _mod_kernel,
        grid=(n // bn,),
        in_specs=[pl.BlockSpec(c_a.shape, lambda j: (0, 0)), pl.BlockSpec(c_b.shape, lambda j: (0, 0))]
        + slabs + [pl.BlockSpec((1, bn), lambda j: (0, j))],
        out_specs=pl.BlockSpec((m, 1, bn), lambda j: (0, 0, j)),
        out_shape=jax.ShapeDtypeStruct((m, 1, n), f32),
        compiler_params=pltpu.CompilerParams(vmem_limit_bytes=VMEM_LIMIT),
        name="adaln_mod",
    )(c_a, c_b, *([w_ada] * MOD_W_SLABS), b_ada.reshape(1, n))


def _group_rms(x, bd_ref, w_ref):
    ssq = _dot((x * x).astype(bf16), bd_ref[...])
    return x * lax.rsqrt(ssq * (1.0 / HEAD_DIM) + EPS) * w_ref[...]


def _rope(x, cos, sin_signed, low_half):
    partner = jnp.where(low_half, pltpu.roll(x, LANES - 32, axis=1), pltpu.roll(x, 32, axis=1))
    return x * cos + partner * sin_signed


def _dup_heads(x, low_lanes):
    rolled = pltpu.roll(x, HEAD_DIM, axis=1)
    return jnp.where(low_lanes, x, rolled), jnp.where(low_lanes, rolled, x)


def _inproj_kernel(x_ref, sh_ref, sc_ref, n1_ref, win_ref, wgu_ref, bg_ref, qnw_ref, knw_ref,
                   bdq_ref, bdk_ref, cos_ref, sin_ref, *rest, groups, ncast, tail, tail_t):
    slabs, rest = rest[:ncast], rest[ncast:]
    gq_ref, gk_ref, gv_ref, gr_ref, la_ref, sq_ref, k2_ref, v2_ref, kk_ref, vk_ref = rest[:10]
    for src, dst in zip(slabs, rest[10:]):
        dst[...] = src[...].astype(dst.dtype)
    nb, t, d = x_ref.shape
    m = nb * t
    mg = m // groups
    lane = lax.broadcasted_iota(jnp.int32, (mg, LANES), 1)
    low_half = (lane & 32) == 0
    low_lanes = lane < HEAD_DIM
    for grp in range(groups):
        rows = slice(grp * mg, (grp + 1) * mg)
        if nb == 1:
            x = x_ref[:, rows, :]
            sc, sh = sc_ref[...], sh_ref[...]
        else:
            seqs = slice(grp * (nb // groups), (grp + 1) * (nb // groups))
            x = x_ref[seqs]
            sc, sh = sc_ref[seqs], sh_ref[seqs]
        ms = jnp.mean(x * x, axis=-1, keepdims=True)
        hn = x * lax.rsqrt(ms + EPS) * n1_ref[...]
        hn = hn * (1.0 + sc) + sh
        hb = hn.reshape(mg, d).astype(bf16)

        def seg(name):
            a, b = _SEG[name]
            return _dot(hb, win_ref[:, a:b])

        cos = cos_ref[rows, :]
        sin = sin_ref[rows, :]
        sq = _group_rms(seg("sq"), bdq_ref, qnw_ref)
        for c in range(SWA_Q // LANES):
            blk = _rope(sq[:, c * LANES:(c + 1) * LANES], cos, sin, low_half)
            sq_ref[rows, c * LANES:(c + 1) * LANES] = (blk * (LOG2E * HEAD_DIM ** -0.5)).astype(sq_ref.dtype)
        sk = _rope(_group_rms(seg("sk"), bdk_ref, knw_ref), cos, sin, low_half)
        sv = seg("sv")
        keep = (grp + 1) * mg - (m - tail)
        if keep > 0:
            keep = min(keep, mg)
            dst_rows = slice((grp + 1) * mg - keep - (m - tail), (grp + 1) * mg - (m - tail))
            if tail_t:
                assert keep == tail
                kk_ref[...] = sk[mg - keep:].T
                vk_ref[...] = sv[mg - keep:].T
            else:
                kk_ref[dst_rows, :] = sk[mg - keep:]
                vk_ref[dst_rows, :] = sv[mg - keep:]
        for src, dst in ((sk, k2_ref), (sv, v2_ref)):
            d0, d1 = _dup_heads(src, low_lanes)
            dst[rows, :LANES] = d0.astype(dst.dtype)
            dst[rows, LANES:] = d1.astype(dst.dtype)
        glr = seg("glr").astype(bf16)
        g = _dot(glr, wgu_ref[...]) + bg_ref[...]
        log_sig = jnp.minimum(g, 0.0) - jnp.log1p(jnp.exp(-jnp.abs(g)))
        la_ref[rows, :] = log_sig * (LOG2E / GLA_TAU)
        gq_ref[rows, :] = seg("gq") * (GLA_DK ** -0.5)
        gk_ref[rows, :] = seg("gk")
        gv_ref[rows, :] = seg("gv").astype(gv_ref.dtype)
        gr_ref[rows, :] = seg("gr").astype(gr_ref.dtype)


MOD_SHIFT1, MOD_SCALE1, MOD_GATE1, MOD_SHIFT2, MOD_SCALE2, MOD_GATE2 = range(6)


def _mod_spec(nb, row0, chunk, batch_block):
    return pl.BlockSpec((nb, 1, D_MODEL), lambda *ids: (row0 // nb + batch_block(*ids), 0, chunk))


def _inproj(x, mod, row0, n1, win, wgu, bg, qnw, knw, bdq, bdk, cos_t, sin_t, nb, tb, tail, act, groups,
            to_bf16=(), tail_t=False):
    bsz, t, d = x.shape
    m = nb * tb
    nt = t // tb
    grid = (bsz // nb, nt)
    tok = bsz * t
    nsteps = grid[0] * grid[1]
    slab_specs = [pl.BlockSpec((w.shape[0] // nsteps, w.shape[1]), lambda i, j: (i * nt + j, 0))
                  for w in to_bf16]

    def full(shape):
        return pl.BlockSpec(shape, lambda i, j: (0,) * len(shape))

    def out(width):
        return pl.BlockSpec((m, width), lambda i, j: (i * nt + j, 0))

    tail_blk = (SWA_KV, tail) if tail_t else (tail, SWA_KV)
    tail_spec = pl.BlockSpec(tail_blk, lambda i, j: (i, 0))
    tail_shape = jax.ShapeDtypeStruct(((bsz // nb) * tail_blk[0], tail_blk[1]), f32)
    outs = ((GLA_QK, f32), (GLA_QK, f32), (GLA_WIDTH, act), (GLA_WIDTH, act), (GLA_QK, f32),
            (SWA_Q, act), (SWA_KV2, bf16), (SWA_KV2, bf16))
    return pl.pallas_call(
        functools.partial(_inproj_kernel, groups=groups, ncast=len(to_bf16), tail=tail, tail_t=tail_t),
        grid=grid,
        in_specs=[pl.BlockSpec((nb, tb, d), lambda i, j: (i, j, 0)),
                  _mod_spec(nb, row0, MOD_SHIFT1, lambda i, j: i),
                  _mod_spec(nb, row0, MOD_SCALE1, lambda i, j: i),
                  full((1, 1, d)),
                  full((d, IN_WIDTH_PADDED)),
                  full((LANES, GLA_QK)),
                  full((1, GLA_QK)),
                  full((1, SWA_Q)),
                  full((1, SWA_KV)),
                  full((SWA_Q, SWA_Q)),
                  full((SWA_KV, SWA_KV)),
                  pl.BlockSpec((m, LANES), lambda i, j: (j, 0)),
                  pl.BlockSpec((m, LANES), lambda i, j: (j, 0))] + slab_specs,
        out_specs=[out(w) for w, _ in outs] + [tail_spec, tail_spec] + slab_specs,
        out_shape=[jax.ShapeDtypeStruct((tok, w), dt) for w, dt in outs]
        + [tail_shape] * 2
        + [jax.ShapeDtypeStruct(w.shape, bf16) for w in to_bf16],
        compiler_params=pltpu.CompilerParams(
            dimension_semantics=("parallel", "arbitrary"), vmem_limit_bytes=VMEM_LIMIT),
        name="inproj",
    )(x, mod, mod, n1, win, wgu, bg, qnw, knw, bdq, bdk, cos_t, sin_t, *to_bf16)


def _gla_constants(chunk, seq):
    t = np.arange(chunk)
    levels = []
    m = 1
    while m < seq:
        levels.append(m)
        m *= 2
    masks = [np.eye(chunk, dtype=bool)]
    for m in levels:
        upper = (t % (2 * m) >= m)[:, None]
        lower = (t % (2 * m) < m)[None, :]
        same = (t[:, None] // (2 * m)) == (t[None, :] // (2 * m))
        masks.append(same & upper & lower)
    tri = t[None, :] <= t[:, None]
    tiled = np.tile(np.stack(masks).astype(np.float32), (1, 1, GLA_HEADS))
    return tuple(levels), jnp.asarray(tri.astype(np.float32), bf16), jnp.asarray(tiled)


def _stack_heads(xb, lane_head, axis):
    zero = jnp.zeros_like(xb)
    return jnp.concatenate([jnp.where(lane_head == h, xb, zero) for h in range(GLA_HEADS)], axis=axis)


def _block_sums(la, bcum, m, row, rolls):
    c, n = la.shape
    if m == 1:
        return la, None
    if m < 8:
        def rolled(shift):
            if shift not in rolls:
                rolls[shift] = pltpu.roll(la, shift % c, axis=0)
            return rolls[shift]
        pos = row & (m - 1)
        pre = la
        suf = None
        for j in range(1, m):
            pre = pre + jnp.where(pos >= j, rolled(j), 0.0)
            term = jnp.where(pos < m - j, rolled(-j), 0.0)
            suf = term if suf is None else suf + term
        return pre, suf
    before, last = [], []
    for i in range(c // m):
        before.append(jnp.zeros((m, n), f32) if i == 0
                      else jnp.broadcast_to(bcum[i * m - 1:i * m], (m, n)))
        last.append(jnp.broadcast_to(bcum[(i + 1) * m - 1:(i + 1) * m], (m, n)))
    if len(before) == 1:
        return bcum - before[0], last[0] - bcum
    return bcum - jnp.concatenate(before, axis=0), jnp.concatenate(last, axis=0) - bcum


def _gla_scores(q_ref, k_ref, v_ref, la_ref, tri_ref, masks_ref, levels, seq):
    c = q_ref.shape[0]
    la = la_ref[...]
    hi = la.astype(bf16)
    lo = (la - hi.astype(f32)).astype(bf16)
    tri = tri_ref[...]
    bcum = _dot(tri, hi) + _dot(tri, lo)
    q = q_ref[...]
    k = k_ref[...]
    vb = v_ref[...].astype(bf16)
    row = lax.broadcasted_iota(jnp.int32, (c, GLA_QK), 0)
    row_head_t = lax.broadcasted_iota(jnp.int32, (GLA_QK, c), 0) >> 6
    rolls = {}
    attn = None
    for lvl, m in enumerate((0,) + levels):
        if m == 0:
            qt, kt = q, k
        else:
            pre, suf = _block_sums(la, bcum, m, row, rolls)
            qt = q * jnp.exp2(pre)
            kt = k if suf is None else k * jnp.exp2(suf)
        kt_t = kt.T.astype(bf16)
        r = _dot(qt.astype(bf16), _stack_heads(kt_t, row_head_t, 1))
        rm = r * masks_ref[lvl]
        attn = rm if attn is None else attn + rm
    a = attn.astype(bf16)
    pre, suf = _block_sums(la, bcum, seq, row, rolls)
    qf = (q * jnp.exp2(pre)).astype(bf16)
    kf = (k * jnp.exp2(suf)).astype(bf16)
    return a, qf, kf, vb, bcum, hi, lo


def _gla_values(a, vb):
    c = a.shape[0]
    return jnp.concatenate([_dot(a[:, h * c:(h + 1) * c], vb[:, h * GLA_DV:(h + 1) * GLA_DV])
                            for h in range(GLA_HEADS)], axis=1)


def _gla_finish(o, r_ref, gnw, go_ref):
    for h in range(GLA_HEADS):
        sl = slice(h * GLA_DV, (h + 1) * GLA_DV)
        oh = o[:, sl]
        r = r_ref[:, sl].astype(f32)
        ms = jnp.mean(oh * oh, axis=-1, keepdims=True)
        go_ref[:, sl] = (oh * lax.rsqrt(ms + EPS) * gnw * (r * _sigmoid(r))).astype(go_ref.dtype)


def _gla_prompt_block(st, q_ref, k_ref, v_ref, la_ref, r_ref, tri_ref, masks_ref, gnw, go_ref, levels,
                      between=lambda: None):
    c = PROMPT_GLA_CHUNK
    nchunk = q_ref.shape[0] // c
    lane_head = lax.broadcasted_iota(jnp.int32, (GLA_DV, GLA_QK), 1) >> 6
    row_head = lax.broadcasted_iota(jnp.int32, (GLA_QK, GLA_DV), 0) >> 6
    chunks = []
    for ci in range(nchunk):
        rows = pl.ds(ci * c, c)
        chunks.append(_gla_scores(q_ref.at[rows], k_ref.at[rows], v_ref.at[rows], la_ref.at[rows],
                                  tri_ref, masks_ref, levels, c))
        between()
    intra, upds, decays = [], [], []
    for a, _, kf, vb, bcum, _, _ in chunks:
        intra.append(_gla_values(a, vb))
        full = _dot_tn(vb, kf)
        upd = None
        for h in range(GLA_HEADS):
            term = jnp.where(lane_head == h, full[h * GLA_DV:(h + 1) * GLA_DV], 0.0)
            upd = term if upd is None else upd + term
        upds.append(upd)
        decays.append(jnp.exp2(bcum[c - 1:c]))
    between()
    for ci in range(nchunk):
        rows = pl.ds(ci * c, c)
        sbd = _stack_heads(st.T.astype(bf16), row_head, 1)
        o = intra[ci] + _dot(chunks[ci][1], sbd)
        _gla_finish(o, r_ref.at[rows], gnw, go_ref.at[rows])
        st = decays[ci] * st + upds[ci]
    return st


def _gla_sample_kernel(q_ref, k_ref, v_ref, la_ref, r_ref, s0_ref, tri_ref, masks_ref, gnw_ref,
                       go_ref, sout_ref, *, levels, seq):
    c = q_ref.shape[0]
    a, qf, kf, vb, _, hi, lo = _gla_scores(q_ref, k_ref, v_ref, la_ref, tri_ref, masks_ref, levels, seq)
    o_intra = _gla_values(a, vb)
    row_head = lax.broadcasted_iota(jnp.int32, (GLA_QK, GLA_DV), 0) >> 6
    ones = jnp.ones((seq, GLA_DV), bf16)
    inter = []
    for b in range(c // seq):
        rows = slice(b * seq, (b + 1) * seq)
        s_old = s0_ref[b]
        sbd = _stack_heads(s_old.astype(bf16), row_head, 1)
        inter.append(_dot(qf[rows], sbd))
        p = _dot_tn(kf[rows], vb[rows])
        upd = None
        for h in range(GLA_HEADS):
            term = jnp.where(row_head == h, p[:, h * GLA_DV:(h + 1) * GLA_DV], 0.0)
            upd = term if upd is None else upd + term
        total = _dot_tn(hi[rows], ones) + _dot_tn(lo[rows], ones)
        sout_ref[b] = jnp.exp2(total) * s_old + upd
    o = o_intra + jnp.concatenate(inter, axis=0)
    _gla_finish(o, r_ref, gnw_ref[...], go_ref)


def _gla_sample(gq, gk, gv, la, gr, state, gnw, bsz, t):
    c = GLA_CHUNK
    nseq = c // t
    levels, tri, masks = _gla_constants(c, t)

    def tok(width):
        return pl.BlockSpec((c, width), lambda i: (i, 0))

    return pl.pallas_call(
        functools.partial(_gla_sample_kernel, levels=levels, seq=t),
        grid=(bsz // nseq,),
        in_specs=[tok(GLA_QK), tok(GLA_QK), tok(GLA_WIDTH), tok(GLA_QK), tok(GLA_WIDTH),
                  pl.BlockSpec((nseq, GLA_QK, GLA_DV), lambda i: (i, 0, 0)),
                  pl.BlockSpec(tri.shape, lambda i: (0, 0)),
                  pl.BlockSpec(masks.shape, lambda i: (0, 0, 0)),
                  pl.BlockSpec((1, GLA_DV), lambda i: (0, 0))],
        out_specs=[tok(GLA_WIDTH),
                   pl.BlockSpec((nseq, GLA_QK, GLA_DV), lambda i: (i, 0, 0))],
        out_shape=[jax.ShapeDtypeStruct((bsz * t, GLA_WIDTH), f32),
                   jax.ShapeDtypeStruct((bsz, GLA_QK, GLA_DV), f32)],
        compiler_params=pltpu.CompilerParams(
            dimension_semantics=("parallel",), vmem_limit_bytes=VMEM_LIMIT),
        name="gla_sample",
    )(gq, gk, gv, la, gr, state, tri, masks, gnw)


def _sink_attention(q_ref, rows, kdups, vdups, mask, sink_ref, o_ref, between=lambda: None):
    n = len(kdups)
    heads_per_group = N_Q_HEADS // N_KV_HEADS
    low_q = lax.broadcasted_iota(jnp.int32, (rows, LANES), 1) < HEAD_DIM
    scores = []
    for i in range(n):
        per_group = []
        for g in range(N_KV_HEADS):
            stack = []
            for p in range(heads_per_group // 2):
                pair = g * (heads_per_group // 2) + p
                qp = q_ref[i * rows:(i + 1) * rows, pair * LANES:(pair + 1) * LANES].astype(bf16)
                zero = jnp.zeros_like(qp)
                stack.append(jnp.where(low_q, qp, zero))
                stack.append(jnp.where(low_q, zero, qp))
            per_group.append(_dot_nt(jnp.concatenate(stack, axis=0), kdups[i][g]))
        scores.append(per_group)
    between()
    probs, inv = [], []
    for head in range(N_Q_HEADS):
        g, hh = divmod(head, heads_per_group)
        parts = [scores[i][g][hh * rows:(hh + 1) * rows] for i in range(n)]
        s = parts[0] if n == 1 else jnp.concatenate(parts, axis=0)
        sink = sink_ref[head] * LOG2E
        s = jnp.where(mask, s, -1e30)
        mx = jnp.maximum(jnp.max(s, axis=-1, keepdims=True), sink)
        p = jnp.exp2(s - mx)
        inv.append(1.0 / (jnp.sum(p, axis=-1, keepdims=True) + jnp.exp2(sink - mx)))
        probs.append(p.astype(bf16))
    between()
    outs = []
    for i in range(n):
        per_group = []
        for g in range(N_KV_HEADS):
            p_i = jnp.concatenate([probs[g * heads_per_group + hh][i * rows:(i + 1) * rows]
                                   for hh in range(heads_per_group)], axis=0)
            per_group.append(_dot(p_i, vdups[i][g]))
        outs.append(per_group)
    low_all = lax.broadcasted_iota(jnp.int32, (n * rows, LANES), 1) < HEAD_DIM
    for pair in range(N_Q_HEADS // 2):
        halves = []
        for head in (2 * pair, 2 * pair + 1):
            g, hh = divmod(head, heads_per_group)
            parts = [outs[i][g][hh * rows:(hh + 1) * rows] for i in range(n)]
            o = parts[0] if n == 1 else jnp.concatenate(parts, axis=0)
            halves.append(o * inv[head])
        o_ref[:, pair * LANES:(pair + 1) * LANES] = jnp.where(low_all, halves[0], halves[1]).astype(o_ref.dtype)


def _swa_prompt_blocks(q_ref, k_before, k2_ref, v_before, v2_ref, first, sink_ref, o_ref,
                       between=lambda: None):
    w = WINDOW
    nblk = q_ref.shape[0] // w

    def dup(before, ref):
        blocks = [before] + [ref[i * w:(i + 1) * w, :] for i in range(nblk)]
        return [[jnp.concatenate([blocks[i][:, g * LANES:(g + 1) * LANES],
                                  blocks[i + 1][:, g * LANES:(g + 1) * LANES]], axis=0)
                 for g in range(N_KV_HEADS)] for i in range(nblk)]

    row = lax.broadcasted_iota(jnp.int32, (nblk * w, 2 * w), 0)
    tk = lax.broadcasted_iota(jnp.int32, (nblk * w, 2 * w), 1)
    rel = tk - (row & (w - 1))
    first_key = jnp.where(row < w, jnp.where(first, w, 0), 0)
    mask = (rel > 0) & (rel <= w) & (tk >= first_key)
    _sink_attention(q_ref, w, dup(k_before, k2_ref), dup(v_before, v2_ref), mask, sink_ref, o_ref, between)


def _swa_sample_kernel(sink_ref, q_ref, kn_ref, vn_ref, pk_ref, pv_ref, o_ref, ko_ref, vo_ref, *, seq):
    nseq, _, _, w = pk_ref.shape
    rows = nseq * seq
    heads_per_group = N_Q_HEADS // N_KV_HEADS
    lane = lax.broadcasted_iota(jnp.int32, (rows, LANES), 1)
    r_id = lax.broadcasted_iota(jnp.int32, (rows, LANES), 0)
    low = lane < HEAD_DIM
    pos = r_id & (seq - 1)
    mask_old = lane > pos
    seq_shift = seq.bit_length() - 1
    mask_new = ((r_id >> seq_shift) == (lane >> seq_shift)) & ((lane & (seq - 1)) <= pos)
    kn = kn_ref[...]
    vn = vn_ref[...]
    kn_dup = [d.astype(bf16) for d in _dup_heads(kn, low)]
    vn_dup = [d.astype(bf16) for d in _dup_heads(vn, low)]

    pad = jnp.zeros((w - seq, LANES), f32)
    tail_lanes = lax.broadcasted_iota(jnp.int32, (HEAD_DIM, w), 1) >= w - seq
    kt_dup, vt_dup = [], []
    for b in range(nseq):
        tok = slice(b * seq, (b + 1) * seq)
        per_k, per_v = [], []
        for new, old_ref, out_ref, per in ((kn, pk_ref, ko_ref, per_k), (vn, pv_ref, vo_ref, per_v)):
            new_t = jnp.concatenate([pad, new[tok]], axis=0).T
            for g in range(N_KV_HEADS):
                old = old_ref[b, g]
                out_ref[b, g] = jnp.where(tail_lanes, new_t[g * HEAD_DIM:(g + 1) * HEAD_DIM],
                                          pltpu.roll(old, w - seq, axis=1))
                ob = old.astype(bf16)
                per.append(jnp.concatenate([ob, ob], axis=0))
        kt_dup.append(per_k)
        vt_dup.append(per_v)

    lhs = []
    for g in range(N_KV_HEADS):
        stack = []
        for p in range(heads_per_group // 2):
            pair = g * (heads_per_group // 2) + p
            qp = q_ref[:, pair * LANES:(pair + 1) * LANES].astype(bf16)
            zero = jnp.zeros_like(qp)
            stack.append(jnp.where(low, qp, zero))
            stack.append(jnp.where(low, zero, qp))
        lhs.append(jnp.concatenate(stack, axis=0))
    s_new = [_dot_nt(lhs[g], kn_dup[g]) for g in range(N_KV_HEADS)]
    s_old = []
    for b in range(nseq):
        per = []
        for g in range(N_KV_HEADS):
            qb = jnp.concatenate([lhs[g][hh * rows + b * seq:hh * rows + (b + 1) * seq]
                                  for hh in range(heads_per_group)], axis=0)
            per.append(_dot(qb, kt_dup[b][g]))
        s_old.append(per)

    p_old, p_new, inv = [], [], []
    for head in range(N_Q_HEADS):
        g, hh = divmod(head, heads_per_group)
        so = jnp.concatenate([s_old[b][g][hh * seq:(hh + 1) * seq] for b in range(nseq)], axis=0)
        so = jnp.where(mask_old, so, -1e30)
        sn = jnp.where(mask_new, s_new[g][hh * rows:(hh + 1) * rows], -1e30)
        sink = sink_ref[head] * LOG2E
        mx = jnp.maximum(jnp.maximum(jnp.max(so, axis=-1, keepdims=True),
                                     jnp.max(sn, axis=-1, keepdims=True)), sink)
        po = jnp.exp2(so - mx)
        pn = jnp.exp2(sn - mx)
        inv.append(1.0 / (jnp.sum(po, axis=-1, keepdims=True) + jnp.sum(pn, axis=-1, keepdims=True)
                          + jnp.exp2(sink - mx)))
        p_old.append(po.astype(bf16))
        p_new.append(pn.astype(bf16))

    o_new = [_dot(jnp.concatenate([p_new[g * heads_per_group + hh] for hh in range(heads_per_group)], axis=0),
                  vn_dup[g]) for g in range(N_KV_HEADS)]
    o_old = []
    for b in range(nseq):
        per = []
        for g in range(N_KV_HEADS):
            pb = jnp.concatenate([p_old[g * heads_per_group + hh][b * seq:(b + 1) * seq]
                                  for hh in range(heads_per_group)], axis=0)
            per.append(_dot_nt(pb, vt_dup[b][g]))
        o_old.append(per)

    for pair in range(N_Q_HEADS // 2):
        halves = []
        for head in (2 * pair, 2 * pair + 1):
            g, hh = divmod(head, heads_per_group)
            old = jnp.concatenate([o_old[b][g][hh * seq:(hh + 1) * seq] for b in range(nseq)], axis=0)
            halves.append((old + o_new[g][hh * rows:(hh + 1) * rows]) * inv[head])
        o_ref[:, pair * LANES:(pair + 1) * LANES] = jnp.where(low, halves[0], halves[1])


def _swa_sample(sq, sk, sv, past_kt, past_vt, sinks, bsz, t):
    nseq = LANES // t
    tok = lambda width: pl.BlockSpec((nseq * t, width), lambda i, s: (i, 0))
    cache = pl.BlockSpec((nseq,) + past_kt.shape[1:], lambda i, s: (i, 0, 0, 0))
    return pl.pallas_call(
        functools.partial(_swa_sample_kernel, seq=t),
        grid_spec=pltpu.PrefetchScalarGridSpec(
            num_scalar_prefetch=1,
            grid=(bsz // nseq,),
            in_specs=[tok(SWA_Q), tok(SWA_KV), tok(SWA_KV), cache, cache],
            out_specs=[tok(SWA_Q), cache, cache]),
        out_shape=[jax.ShapeDtypeStruct((bsz * t, SWA_Q), f32),
                   jax.ShapeDtypeStruct(past_kt.shape, f32),
                   jax.ShapeDtypeStruct(past_vt.shape, f32)],
        compiler_params=pltpu.CompilerParams(
            dimension_semantics=("parallel",), vmem_limit_bytes=VMEM_LIMIT),
        name="swa_sample",
    )(sinks, sq, sk, sv, past_kt, past_vt)


FFN_CHUNK = 512


def _out_proj_norm(streams, n2, wout_ref):
    mix = [s[1] for s in streams]
    mix = mix[0] if len(mix) == 1 else jnp.concatenate(mix, axis=0)
    mixed = _dot(mix[:, :GLA_WIDTH], wout_ref[:GLA_WIDTH, :]) + _dot(mix[:, GLA_WIDTH:], wout_ref[GLA_WIDTH:, :])
    hs, hbs, r0 = [], [], 0
    for x, _, g1, sh2, sc2 in streams:
        nb, t, d = x.shape
        h = x + g1 * mixed[r0:r0 + nb * t].reshape(nb, t, d)
        ms = jnp.mean(h * h, axis=-1, keepdims=True)
        hn = h * lax.rsqrt(ms + EPS) * n2
        hn = hn * (1.0 + sc2) + sh2
        hs.append(h)
        hbs.append(hn.reshape(nb * t, d).astype(bf16))
        r0 += nb * t
    return hs, (hbs[0] if len(hbs) == 1 else jnp.concatenate(hbs, axis=0))


def _ffn_piece(hb, w1_ref, w2_ref, c, width):
    cols = slice(c * width, (c + 1) * width)
    a = jnp.maximum(_dot(hb, w1_ref[:, cols]), 0.0)
    return _dot((a * a).astype(bf16), w2_ref[cols, :])


def _resident(shape):
    return pl.BlockSpec(shape, lambda *_: (0,) * len(shape), pipeline_mode=pl.Buffered(1))


def _prompt_kernel(sink_ref,
                   q_ref, k_ref, v_ref, la_ref, r_ref, sq_ref, k2_ref, v2_ref, k2p_ref, v2p_ref,
                   tri_ref, masks_ref, gnw_ref,
                   x_ref, g1_ref, sh2_ref, sc2_ref, g2_ref,
                   xs_ref, gos_ref, sos_ref, g1s_ref, sh2s_ref, sc2s_ref, g2s_ref,
                   n2_ref, wout_ref, w1_ref, w2_ref,
                   y_ref, ys_ref, sout_ref, state_ref, mix_ref, *, levels, nt, nsteps):
    s = pl.program_id(0)
    slot = s % 2

    @pl.when(s == 0)
    def _():
        mix_ref[1] = jnp.zeros(mix_ref.shape[1:], mix_ref.dtype)
        state_ref[...] = jnp.zeros_like(state_ref)

    j = jnp.minimum(s, nsteps - 1) % nt
    st_in = state_ref[...]
    st = jnp.where(j == 0, 0.0, st_in)
    gnw = gnw_ref[...]
    out = mix_ref.at[slot]
    mix_s = jnp.concatenate([gos_ref[...], sos_ref[...]], axis=1).astype(bf16)
    (h, h_s), hb = _out_proj_norm(
        [(x_ref[...], mix_ref[1 - slot], g1_ref[...], sh2_ref[...], sc2_ref[...]),
         (xs_ref[...], mix_s, g1s_ref[...], sh2s_ref[...], sc2s_ref[...])], n2_ref[...], wout_ref)
    rows_p = h.shape[0] * h.shape[1]
    todo = list(range(D_FF // FFN_CHUNK))
    ff = []

    def ffn_piece():
        if todo:
            part = _ffn_piece(hb, w1_ref, w2_ref, todo.pop(0), FFN_CHUNK)
            ff[:] = [part if not ff else ff[0] + part]

    st = _gla_prompt_block(st, q_ref, k_ref, v_ref, la_ref, r_ref, tri_ref, masks_ref, gnw,
                           out.at[:, pl.ds(0, GLA_WIDTH)], levels, ffn_piece)
    _swa_prompt_blocks(sq_ref, k2p_ref[...], k2_ref, v2p_ref[...], v2_ref, j == 0, sink_ref,
                       out.at[:, pl.ds(GLA_WIDTH, SWA_Q)], ffn_piece)
    while todo:
        ffn_piece()
    st = jnp.where(s < nsteps, st, st_in)
    state_ref[...] = st
    sout_ref[0] = st.T
    y_ref[...] = h + g2_ref[...] * ff[0][:rows_p].reshape(h.shape)
    ys_ref[...] = h_s + g2s_ref[...] * ff[0][rows_p:].reshape(h_s.shape)


def _prompt_mix_ffn(x, gq, gk, gv, la, gr, sq, k2, v2, gnw, sinks, mod, row0, n2, wout, w1, w2,
                    xs, gos, sos, row0_s):
    bsz, t, d = x.shape
    tb = FUSED_TOKEN_BLOCK
    nt = t // tb
    nsteps = bsz * nt
    levels, tri, masks = _gla_constants(PROMPT_GLA_CHUNK, PROMPT_GLA_CHUNK)
    per_blk = tb // WINDOW
    bs, ts, _ = xs.shape
    nbs = bs // nsteps
    assert nbs * nsteps == bs and (nbs * ts) % 16 == 0

    def mix_blk(s):
        return jnp.minimum(s, nsteps - 1)

    def ffn_blk(s):
        return jnp.maximum(s - 1, 0)

    def tok(width):
        return pl.BlockSpec((tb, width), lambda s, _: (mix_blk(s), 0))

    prev = pl.BlockSpec((WINDOW, SWA_KV2), lambda s, _: (jnp.maximum(mix_blk(s) * per_blk - 1, 0), 0))
    xspec = pl.BlockSpec((1, tb, d), lambda s, _: (ffn_blk(s) // nt, ffn_blk(s) % nt, 0))
    ffn_chunks = (MOD_GATE1, MOD_SHIFT2, MOD_SCALE2, MOD_GATE2)
    mods = [_mod_spec(1, row0, chunk, lambda s, _: ffn_blk(s) // nt) for chunk in ffn_chunks]
    mods_s = [_mod_spec(nbs, row0_s, chunk, lambda s, _: ffn_blk(s)) for chunk in ffn_chunks]
    xs_spec = pl.BlockSpec((nbs, ts, d), lambda s, _: (ffn_blk(s), 0, 0))
    mix_s_spec = pl.BlockSpec((nbs * ts, GLA_WIDTH), lambda s, _: (ffn_blk(s), 0))
    return pl.pallas_call(
        functools.partial(_prompt_kernel, levels=levels, nt=nt, nsteps=nsteps),
        grid_spec=pltpu.PrefetchScalarGridSpec(
            num_scalar_prefetch=1,
            grid=(nsteps + 1,),
            in_specs=[tok(GLA_QK), tok(GLA_QK), tok(GLA_WIDTH), tok(GLA_QK), tok(GLA_WIDTH),
                      tok(SWA_Q), tok(SWA_KV2), tok(SWA_KV2), prev, prev,
                      _resident(tri.shape), _resident(masks.shape), _resident((1, GLA_DV)),
                      xspec, *mods, xs_spec, mix_s_spec, mix_s_spec, *mods_s, _resident((1, 1, d)),
                      _resident((d, d)), _resident((d, D_FF)), _resident((D_FF, d))],
            out_specs=[xspec, xs_spec,
                       pl.BlockSpec((1, GLA_QK, GLA_DV), lambda s, _: (mix_blk(s) // nt, 0, 0))],
            scratch_shapes=[pltpu.VMEM((GLA_DV, GLA_QK), f32),
                            pltpu.VMEM((2, tb, GLA_WIDTH + SWA_Q), bf16)]),
        out_shape=[jax.ShapeDtypeStruct((bsz, t, d), f32),
                   jax.ShapeDtypeStruct(xs.shape, f32),
                   jax.ShapeDtypeStruct((bsz, GLA_QK, GLA_DV), f32)],
        compiler_params=pltpu.CompilerParams(
            dimension_semantics=("arbitrary",), vmem_limit_bytes=VMEM_LIMIT),
        name="prompt_mix_ffn",
    )(sinks, gq, gk, gv, la, gr, sq, k2, v2, k2, v2, tri, masks, gnw,
      x, mod, mod, mod, mod, xs, gos, sos, mod, mod, mod, mod, n2, wout, w1, w2)


def _rope_tables(pos, copies):
    half = HEAD_DIM // 2
    inv = np.power(np.float32(ROPE_THETA), -np.arange(half, dtype=np.float32) * np.float32(2.0 / HEAD_DIM))
    ang = np.asarray(pos, np.float32)[:, None] * inv[None, :].astype(np.float32)
    cos = np.cos(ang).astype(np.float32)
    sin = np.sin(ang).astype(np.float32)
    reps = (copies, LANES // HEAD_DIM)
    return (jnp.asarray(np.tile(np.concatenate([cos, cos], axis=-1), reps)),
            jnp.asarray(np.tile(np.concatenate([-sin, sin], axis=-1), reps)))


def _block_diag_ones(n, blk):
    idx = np.arange(n) // blk
    return jnp.asarray((idx[:, None] == idx[None, :]).astype(np.float32), bf16)


def _layer_weights(w_in, w_gate_up, b_gate, q_norm_w, k_norm_w, w_out, w_ff1, w_ff2):
    splits = np.cumsum([GLA_QK, GLA_QK, GLA_WIDTH, GLA_WIDTH, GLA_GATE_RANK, SWA_Q, SWA_KV])
    gq, gk, gv, gr, glr, sq, sk, sv = jnp.split(w_in, [int(s) for s in splits], axis=1)
    glr = jnp.pad(glr, ((0, 0), (0, LANES - GLA_GATE_RANK)))
    win = jnp.concatenate([gq, gk, gv, gr, sq, sk, sv, glr], axis=1).astype(bf16)
    wgu = jnp.pad(w_gate_up, ((0, LANES - GLA_GATE_RANK), (0, 0))).astype(bf16)
    return dict(
        win=win, wgu=wgu, bg=b_gate.reshape(1, GLA_QK),
        qnw=jnp.tile(q_norm_w, N_Q_HEADS).reshape(1, SWA_Q),
        knw=jnp.tile(k_norm_w, N_KV_HEADS).reshape(1, SWA_KV),
        ffn_f32=(w_out, w_ff1, w_ff2))


def _project(x, mod, row0, pos, lw, n1, bdq, bdk, prompt):
    bsz, t, d = x.shape
    if prompt:
        nb, tb, tail, act, groups, cast = 1, TOKEN_BLOCK, WINDOW, bf16, PROMPT_INPROJ_GROUPS, lw["ffn_f32"]
    else:
        nb, tb, tail, act, groups, cast = SAMPLE_ROWS // t, t, SAMPLE_ROWS, f32, 1, ()
    cos_t, sin_t = _rope_tables(pos, nb)
    return _inproj(x, mod, row0, n1.reshape(1, 1, d), lw["win"], lw["wgu"], lw["bg"], lw["qnw"], lw["knw"],
                   bdq, bdk, cos_t, sin_t, nb, tb, tail, act, groups, cast, tail_t=prompt)


def _decoder_layers(xp, xs, mod, pos_p, pos_s, state, past_k, past_v, lw, n1, n2, gnw, sinks, bdq, bdk):
    bp, tp, d = xp.shape
    bs, ts, _ = xs.shape
    heads = (N_KV_HEADS, HEAD_DIM)
    gq, gk, gv, gr, la, sq, _, _, kk, vk = _project(xs, mod, 0, pos_s, lw, n1, bdq, bdk, False)
    go_s, state_s = _gla_sample(gq, gk, gv, la, gr, state.reshape(bs, GLA_QK, GLA_DV), gnw, bs, ts)
    so_s, kt, vt = _swa_sample(sq, kk, vk, past_k.transpose(0, 2, 3, 1), past_v.transpose(0, 2, 3, 1),
                               sinks, bs, ts)
    gq, gk, gv, gr, la, sq, k2, v2, kk, vk, *ffn_w = _project(xp, mod, bs, pos_p, lw, n1, bdq, bdk, True)
    yp, ys, s_t = _prompt_mix_ffn(xp, gq, gk, gv, la, gr, sq, k2, v2, gnw, sinks, mod, bs,
                                  n2.reshape(1, 1, d), *ffn_w, xs, go_s, so_s, 0)
    state_p = s_t.reshape(bp, GLA_HEADS, GLA_DK, GLA_DV)
    return (yp, ys, state_p,
            kk.reshape(bp, *heads, WINDOW).transpose(0, 3, 1, 2), vk.reshape(bp, *heads, WINDOW).transpose(0, 3, 1, 2),
            state_s.reshape(bs, GLA_HEADS, GLA_DK, GLA_DV),
            kt.transpose(0, 3, 1, 2), vt.transpose(0, 3, 1, 2))


def kernel(x_prompt, x_sample, state_gla, cache_swa_k, cache_swa_v, c_prompt, c_sample, w_ada, b_ada, norm1_w, norm2_w, w_in, w_gate_up, b_gate, gla_norm_w, q_norm_w, k_norm_w, sinks, w_out, w_ff1, w_ff2):
    depth = w_ada.shape[0]
    bp, tp, _ = x_prompt.shape
    bs, ts, _ = x_sample.shape
    pos_p = np.arange(tp)
    pos_s = PAST_LEN + np.arange(ts)
    bdq = _block_diag_ones(SWA_Q, HEAD_DIM)
    bdk = _block_diag_ones(SWA_KV, HEAD_DIM)
    yp, ys = x_prompt, x_sample
    outs = [[] for _ in range(6)]
    for l in range(depth):
        mod = _modulation(c_sample, c_prompt, w_ada[l], b_ada[l])
        lw = _layer_weights(w_in[l], w_gate_up[l], b_gate[l], q_norm_w[l], k_norm_w[l],
                            w_out[l], w_ff1[l], w_ff2[l])
        gnw = gla_norm_w[l].reshape(1, GLA_DV)
        yp, ys, *new = _decoder_layers(yp, ys, mod, pos_p, pos_s, state_gla[l], cache_swa_k[l],
                                       cache_swa_v[l], lw, norm1_w[l], norm2_w[l], gnw, sinks[l], bdq, bdk)
        for lst, val in zip(outs, new):
            lst.append(val)
    return (yp, ys) + tuple(jnp.stack(o) for o in outs)
```

```python
import functools

import jax
import jax.numpy as jnp
import numpy as np
from jax import lax
from jax.experimental import pallas as pl
from jax.experimental.pallas import tpu as pltpu

f32 = jnp.float32
bf16 = jnp.bfloat16

D_MODEL = 1024
GLA_HEADS = 4
GLA_DK = 64
GLA_DV = 128
GLA_QK = GLA_HEADS * GLA_DK
GLA_WIDTH = GLA_HEADS * GLA_DV
GLA_GATE_RANK = 16
GLA_TAU = 16.0
LOG2E = 1.4426950408889634
HEAD_DIM = 64
N_Q_HEADS = 8
N_KV_HEADS = 2
SWA_Q = N_Q_HEADS * HEAD_DIM
SWA_KV = N_KV_HEADS * HEAD_DIM
SWA_KV2 = 2 * SWA_KV
WINDOW = 128
ROPE_THETA = 10000.0
PAST_LEN = 8192
D_FF = 4 * D_MODEL
EPS = 1e-6
LANES = 128
GLA_CHUNK = 128
PROMPT_GLA_CHUNK = 128
TOKEN_BLOCK = 1024
FUSED_TOKEN_BLOCK = 512
SAMPLE_ROWS = 512
PROMPT_INPROJ_GROUPS = 2
VMEM_LIMIT = 56 * 1024 * 1024

_SEG = {}
_off = 0
for _name, _w in (("gq", GLA_QK), ("gk", GLA_QK), ("gv", GLA_WIDTH), ("gr", GLA_WIDTH),
                  ("sq", SWA_Q), ("sk", SWA_KV), ("sv", SWA_KV), ("glr", LANES)):
    _SEG[_name] = (_off, _off + _w)
    _off += _w
IN_WIDTH_PADDED = _off


def _dot(a, b):
    return jnp.dot(a, b, preferred_element_type=f32)


def _dot_nt(a, b):
    return lax.dot_general(a, b, (((1,), (1,)), ((), ())), preferred_element_type=f32)


def _dot_tn(a, b):
    return lax.dot_general(a, b, (((0,), (0,)), ((), ())), preferred_element_type=f32)


def _sigmoid(x):
    return 1.0 / (1.0 + jnp.exp(-x))


MOD_W_SLABS = 4


def _mod_kernel(ca_ref, cb_ref, *refs):
    w_refs, b_ref, o_ref = refs[:-2], refs[-2], refs[-1]
    c = jnp.concatenate([ca_ref[...], cb_ref[...]], axis=0)
    s = (c * _sigmoid(c)).astype(bf16)
    kb = w_refs[0].shape[0]
    res = b_ref[...]
    for i, w_ref in enumerate(w_refs):
        res = res + _dot(s[:, i * kb:(i + 1) * kb], w_ref[...].astype(bf16))
    for r in range(res.shape[0]):
        o_ref[r] = res[r:r + 1, :]


def _modulation(c_a, c_b, w_ada, b_ada):
    m = c_a.shape[0] + c_b.shape[0]
    n = w_ada.shape[1]
    bn = 1536
    kb = D_MODEL // MOD_W_SLABS
    slabs = [pl.BlockSpec((kb, bn), lambda j, i=i: (i, j)) for i in range(MOD_W_SLABS)]
    return pl.pallas_call(
        _mod_kernel,
        grid=(n // bn,),
        in_specs=[pl.BlockSpec(c_a.shape, lambda j: (0, 0)), pl.BlockSpec(c_b.shape, lambda j: (0, 0))]
        + slabs + [pl.BlockSpec((1, bn), lambda j: (0, j))],
        out_specs=pl.BlockSpec((m, 1, bn), lambda j: (0, 0, j)),
        out_shape=jax.ShapeDtypeStruct((m, 1, n), f32),
        compiler_params=pltpu.CompilerParams(vmem_limit_bytes=VMEM_LIMIT),
        name="adaln_mod",
    )(c_a, c_b, *([w_ada] * MOD_W_SLABS), b_ada.reshape(1, n))


def _group_rms(x, bd_ref, w_ref):
    ssq = _dot((x * x).astype(bf16), bd_ref[...])
    return x * lax.rsqrt(ssq * (1.0 / HEAD_DIM) + EPS) * w_ref[...]


def _rope(x, cos, sin_signed, low_half):
    partner = jnp.where(low_half, pltpu.roll(x, LANES - 32, axis=1), pltpu.roll(x, 32, axis=1))
    return x * cos + partner * sin_signed


def _dup_heads(x, low_lanes):
    rolled = pltpu.roll(x, HEAD_DIM, axis=1)
    return jnp.where(low_lanes, x, rolled), jnp.where(low_lanes, rolled, x)


def _inproj_kernel(x_ref, sh_ref, sc_ref, n1_ref, win_ref, wgu_ref, bg_ref, qnw_ref, knw_ref,
                   bdq_ref, bdk_ref, cos_ref, sin_ref, *rest, groups, ncast, tail, tail_t):
    slabs, rest = rest[:ncast], rest[ncast:]
    gq_ref, gk_ref, gv_ref, gr_ref, la_ref, sq_ref, k2_ref, v2_ref, kk_ref, vk_ref = rest[:10]
    for src, dst in zip(slabs, rest[10:]):
        dst[...] = src[...].astype(dst.dtype)
    nb, t, d = x_ref.shape
    m = nb * t
    mg = m // groups
    lane = lax.broadcasted_iota(jnp.int32, (mg, LANES), 1)
    low_half = (lane & 32) == 0
    low_lanes = lane < HEAD_DIM
    for grp in range(groups):
        rows = slice(grp * mg, (grp + 1) * mg)
        if nb == 1:
            x = x_ref[:, rows, :]
            sc, sh = sc_ref[...], sh_ref[...]
        else:
            seqs = slice(grp * (nb // groups), (grp + 1) * (nb // groups))
            x = x_ref[seqs]
            sc, sh = sc_ref[seqs], sh_ref[seqs]
        ms = jnp.mean(x * x, axis=-1, keepdims=True)
        hn = x * lax.rsqrt(ms + EPS) * n1_ref[...]
        hn = hn * (1.0 + sc) + sh
        hb = hn.reshape(mg, d).astype(bf16)

        def seg(name):
            a, b = _SEG[name]
            return _dot(hb, win_ref[:, a:b])

        cos = cos_ref[rows, :]
        sin = sin_ref[rows, :]
        sq = _group_rms(seg("sq"), bdq_ref, qnw_ref)
        for c in range(SWA_Q // LANES):
            blk = _rope(sq[:, c * LANES:(c + 1) * LANES], cos, sin, low_half)
            sq_ref[rows, c * LANES:(c + 1) * LANES] = (blk * (LOG2E * HEAD_DIM ** -0.5)).astype(sq_ref.dtype)
        sk = _rope(_group_rms(seg("sk"), bdk_ref, knw_ref), cos, sin, low_half)
        sv = seg("sv")
        keep = (grp + 1) * mg - (m - tail)
        if keep > 0:
            keep = min(keep, mg)
            dst_rows = slice((grp + 1) * mg - keep - (m - tail), (grp + 1) * mg - (m - tail))
            if tail_t:
                assert keep == tail
                kk_ref[...] = sk[mg - keep:].T
                vk_ref[...] = sv[mg - keep:].T
            else:
                kk_ref[dst_rows, :] = sk[mg - keep:]
                vk_ref[dst_rows, :] = sv[mg - keep:]
        for src, dst in ((sk, k2_ref), (sv, v2_ref)):
            d0, d1 = _dup_heads(src, low_lanes)
            dst[rows, :LANES] = d0.astype(dst.dtype)
            dst[rows, LANES:] = d1.astype(dst.dtype)
        glr = seg("glr").astype(bf16)
        g = _dot(glr, wgu_ref[...]) + bg_ref[...]
        log_sig = jnp.minimum(g, 0.0) - jnp.log1p(jnp.exp(-jnp.abs(g)))
        la_ref[rows, :] = log_sig * (LOG2E / GLA_TAU)
        gq_ref[rows, :] = seg("gq") * (GLA_DK ** -0.5)
        gk_ref[rows, :] = seg("gk")
        gv_ref[rows, :] = seg("gv").astype(gv_ref.dtype)
        gr_ref[rows, :] = seg("gr").astype(gr_ref.dtype)


MOD_SHIFT1, MOD_SCALE1, MOD_GATE1, MOD_SHIFT2, MOD_SCALE2, MOD_GATE2 = range(6)


def _mod_spec(nb, row0, chunk, batch_block):
    return pl.BlockSpec((nb, 1, D_MODEL), lambda *ids: (row0 // nb + batch_block(*ids), 0, chunk))


def _inproj(x, mod, row0, n1, win, wgu, bg, qnw, knw, bdq, bdk, cos_t, sin_t, nb, tb, tail, act, groups,
            to_bf16=(), tail_t=False):
    bsz, t, d = x.shape
    m = nb * tb
    nt = t // tb
    grid = (bsz // nb, nt)
    tok = bsz * t
    nsteps = grid[0] * grid[1]
    slab_specs = [pl.BlockSpec((w.shape[0] // nsteps, w.shape[1]), lambda i, j: (i * nt + j, 0))
                  for w in to_bf16]

    def full(shape):
        return pl.BlockSpec(shape, lambda i, j: (0,) * len(shape))

    def out(width):
        return pl.BlockSpec((m, width), lambda i, j: (i * nt + j, 0))

    tail_blk = (SWA_KV, tail) if tail_t else (tail, SWA_KV)
    tail_spec = pl.BlockSpec(tail_blk, lambda i, j: (i, 0))
    tail_shape = jax.ShapeDtypeStruct(((bsz // nb) * tail_blk[0], tail_blk[1]), f32)
    outs = ((GLA_QK, f32), (GLA_QK, f32), (GLA_WIDTH, act), (GLA_WIDTH, act), (GLA_QK, f32),
            (SWA_Q, act), (SWA_KV2, bf16), (SWA_KV2, bf16))
    return pl.pallas_call(
        functools.partial(_inproj_kernel, groups=groups, ncast=len(to_bf16), tail=tail, tail_t=tail_t),
        grid=grid,
        in_specs=[pl.BlockSpec((nb, tb, d), lambda i, j: (i, j, 0)),
                  _mod_spec(nb, row0, MOD_SHIFT1, lambda i, j: i),
                  _mod_spec(nb, row0, MOD_SCALE1, lambda i, j: i),
                  full((1, 1, d)),
                  full((d, IN_WIDTH_PADDED)),
                  full((LANES, GLA_QK)),
                  full((1, GLA_QK)),
                  full((1, SWA_Q)),
                  full((1, SWA_KV)),
                  full((SWA_Q, SWA_Q)),
                  full((SWA_KV, SWA_KV)),
                  pl.BlockSpec((m, LANES), lambda i, j: (j, 0)),
                  pl.BlockSpec((m, LANES), lambda i, j: (j, 0))] + slab_specs,
        out_specs=[out(w) for w, _ in outs] + [tail_spec, tail_spec] + slab_specs,
        out_shape=[jax.ShapeDtypeStruct((tok, w), dt) for w, dt in outs]
        + [tail_shape] * 2
        + [jax.ShapeDtypeStruct(w.shape, bf16) for w in to_bf16],
        compiler_params=pltpu.CompilerParams(
            dimension_semantics=("parallel", "arbitrary"), vmem_limit_bytes=VMEM_LIMIT),
        name="inproj",
    )(x, mod, mod, n1, win, wgu, bg, qnw, knw, bdq, bdk, cos_t, sin_t, *to_bf16)


def _gla_constants(chunk, seq):
    t = np.arange(chunk)
    levels = []
    m = 1
    while m < seq:
        levels.append(m)
        m *= 2
    masks = [np.eye(chunk, dtype=bool)]
    for m in levels:
        upper = (t % (2 * m) >= m)[:, None]
        lower = (t % (2 * m) < m)[None, :]
        same = (t[:, None] // (2 * m)) == (t[None, :] // (2 * m))
        masks.append(same & upper & lower)
    tri = t[None, :] <= t[:, None]
    tiled = np.tile(np.stack(masks).astype(np.float32), (1, 1, GLA_HEADS))
    return tuple(levels), jnp.asarray(tri.astype(np.float32), bf16), jnp.asarray(tiled)


def _stack_heads(xb, lane_head, axis):
    zero = jnp.zeros_like(xb)
    return jnp.concatenate([jnp.where(lane_head == h, xb, zero) for h in range(GLA_HEADS)], axis=axis)


def _block_sums(la, bcum, m, row, rolls):
    c, n = la.shape
    if m == 1:
        return la, None
    if m < 8:
        def rolled(shift):
            if shift not in rolls:
                rolls[shift] = pltpu.roll(la, shift % c, axis=0)
            return rolls[shift]
        pos = row & (m - 1)
        pre = la
        suf = None
        for j in range(1, m):
            pre = pre + jnp.where(pos >= j, rolled(j), 0.0)
            term = jnp.where(pos < m - j, rolled(-j), 0.0)
            suf = term if suf is None else suf + term
        return pre, suf
    before, last = [], []
    for i in range(c // m):
        before.append(jnp.zeros((m, n), f32) if i == 0
                      else jnp.broadcast_to(bcum[i * m - 1:i * m], (m, n)))
        last.append(jnp.broadcast_to(bcum[(i + 1) * m - 1:(i + 1) * m], (m, n)))
    if len(before) == 1:
        return bcum - before[0], last[0] - bcum
    return bcum - jnp.concatenate(before, axis=0), jnp.concatenate(last, axis=0) - bcum


def _gla_scores(q_ref, k_ref, v_ref, la_ref, tri_ref, masks_ref, levels, seq):
    c = q_ref.shape[0]
    la = la_ref[...]
    hi = la.astype(bf16)
    lo = (la - hi.astype(f32)).astype(bf16)
    tri = tri_ref[...]
    bcum = _dot(tri, hi) + _dot(tri, lo)
    q = q_ref[...]
    k = k_ref[...]
    vb = v_ref[...].astype(bf16)
    row = lax.broadcasted_iota(jnp.int32, (c, GLA_QK), 0)
    row_head_t = lax.broadcasted_iota(jnp.int32, (GLA_QK, c), 0) >> 6
    rolls = {}
    attn = None
    for lvl, m in enumerate((0,) + levels):
        if m == 0:
            qt, kt = q, k
        else:
            pre, suf = _block_sums(la, bcum, m, row, rolls)
            qt = q * jnp.exp2(pre)
            kt = k if suf is None else k * jnp.exp2(suf)
        kt_t = kt.T.astype(bf16)
        r = _dot(qt.astype(bf16), _stack_heads(kt_t, row_head_t, 1))
        rm = r * masks_ref[lvl]
        attn = rm if attn is None else attn + rm
    a = attn.astype(bf16)
    pre, suf = _block_sums(la, bcum, seq, row, rolls)
    qf = (q * jnp.exp2(pre)).astype(bf16)
    kf = (k * jnp.exp2(suf)).astype(bf16)
    return a, qf, kf, vb, bcum, hi, lo


def _gla_values(a, vb):
    v_head = lax.broadcasted_iota(jnp.int32, vb.shape, 1) >> 7
    return _dot(a, _stack_heads(vb, v_head, 0))


def _gla_finish(o, r_ref, gnw, go_ref):
    for h in range(GLA_HEADS):
        sl = slice(h * GLA_DV, (h + 1) * GLA_DV)
        oh = o[:, sl]
        r = r_ref[:, sl].astype(f32)
        ms = jnp.mean(oh * oh, axis=-1, keepdims=True)
        go_ref[:, sl] = (oh * lax.rsqrt(ms + EPS) * gnw * (r * _sigmoid(r))).astype(go_ref.dtype)


def _gla_prompt_block(st, q_ref, k_ref, v_ref, la_ref, r_ref, tri_ref, masks_ref, gnw, go_ref, levels,
                      between=lambda: None):
    c = PROMPT_GLA_CHUNK
    nchunk = q_ref.shape[0] // c
    lane_head = lax.broadcasted_iota(jnp.int32, (GLA_DV, GLA_QK), 1) >> 6
    row_head = lax.broadcasted_iota(jnp.int32, (GLA_QK, GLA_DV), 0) >> 6
    chunks = []
    for ci in range(nchunk):
        rows = pl.ds(ci * c, c)
        chunks.append(_gla_scores(q_ref.at[rows], k_ref.at[rows], v_ref.at[rows], la_ref.at[rows],
                                  tri_ref, masks_ref, levels, c))
        between()
    intra, upds, decays = [], [], []
    for a, _, kf, vb, bcum, _, _ in chunks:
        intra.append(_gla_values(a, vb))
        full = _dot_tn(vb, kf)
        upd = None
        for h in range(GLA_HEADS):
            term = jnp.where(lane_head == h, full[h * GLA_DV:(h + 1) * GLA_DV], 0.0)
            upd = term if upd is None else upd + term
        upds.append(upd)
        decays.append(jnp.exp2(bcum[c - 1:c]))
    between()
    for ci in range(nchunk):
        rows = pl.ds(ci * c, c)
        sbd = _stack_heads(st.T.astype(bf16), row_head, 1)
        o = intra[ci] + _dot(chunks[ci][1], sbd)
        _gla_finish(o, r_ref.at[rows], gnw, go_ref.at[rows])
        st = decays[ci] * st + upds[ci]
    return st


def _gla_sample_kernel(q_ref, k_ref, v_ref, la_ref, r_ref, s0_ref, tri_ref, masks_ref, gnw_ref,
                       go_ref, sout_ref, *, levels, seq):
    c = q_ref.shape[0]
    a, qf, kf, vb, _, hi, lo = _gla_scores(q_ref, k_ref, v_ref, la_ref, tri_ref, masks_ref, levels, seq)
    o_intra = _gla_values(a, vb)
    row_head = lax.broadcasted_iota(jnp.int32, (GLA_QK, GLA_DV), 0) >> 6
    ones = jnp.ones((seq, GLA_DV), bf16)
    inter = []
    for b in range(c // seq):
        rows = slice(b * seq, (b + 1) * seq)
        s_old = s0_ref[b]
        sbd = _stack_heads(s_old.astype(bf16), row_head, 1)
        inter.append(_dot(qf[rows], sbd))
        p = _dot_tn(kf[rows], vb[rows])
        upd = None
        for h in range(GLA_HEADS):
            term = jnp.where(row_head == h, p[:, h * GLA_DV:(h + 1) * GLA_DV], 0.0)
            upd = term if upd is None else upd + term
        total = _dot_tn(hi[rows], ones) + _dot_tn(lo[rows], ones)
        sout_ref[b] = jnp.exp2(total) * s_old + upd
    o = o_intra + jnp.concatenate(inter, axis=0)
    _gla_finish(o, r_ref, gnw_ref[...], go_ref)


def _gla_sample(gq, gk, gv, la, gr, state, gnw, bsz, t):
    c = GLA_CHUNK
    nseq = c // t
    levels, tri, masks = _gla_constants(c, t)

    def tok(width):
        return pl.BlockSpec((c, width), lambda i: (i, 0))

    return pl.pallas_call(
        functools.partial(_gla_sample_kernel, levels=levels, seq=t),
        grid=(bsz // nseq,),
        in_specs=[tok(GLA_QK), tok(GLA_QK), tok(GLA_WIDTH), tok(GLA_QK), tok(GLA_WIDTH),
                  pl.BlockSpec((nseq, GLA_QK, GLA_DV), lambda i: (i, 0, 0)),
                  pl.BlockSpec(tri.shape, lambda i: (0, 0)),
                  pl.BlockSpec(masks.shape, lambda i: (0, 0, 0)),
                  pl.BlockSpec((1, GLA_DV), lambda i: (0, 0))],
        out_specs=[tok(GLA_WIDTH),
                   pl.BlockSpec((nseq, GLA_QK, GLA_DV), lambda i: (i, 0, 0))],
        out_shape=[jax.ShapeDtypeStruct((bsz * t, GLA_WIDTH), f32),
                   jax.ShapeDtypeStruct((bsz, GLA_QK, GLA_DV), f32)],
        compiler_params=pltpu.CompilerParams(
            dimension_semantics=("parallel",), vmem_limit_bytes=VMEM_LIMIT),
        name="gla_sample",
    )(gq, gk, gv, la, gr, state, tri, masks, gnw)


def _sink_attention(q_ref, rows, kdups, vdups, mask, sink_ref, o_ref, between=lambda: None):
    n = len(kdups)
    heads_per_group = N_Q_HEADS // N_KV_HEADS
    low_q = lax.broadcasted_iota(jnp.int32, (rows, LANES), 1) < HEAD_DIM
    scores = []
    for i in range(n):
        per_group = []
        for g in range(N_KV_HEADS):
            stack = []
            for p in range(heads_per_group // 2):
                pair = g * (heads_per_group // 2) + p
                qp = q_ref[i * rows:(i + 1) * rows, pair * LANES:(pair + 1) * LANES].astype(bf16)
                zero = jnp.zeros_like(qp)
                stack.append(jnp.where(low_q, qp, zero))
                stack.append(jnp.where(low_q, zero, qp))
            per_group.append(_dot_nt(jnp.concatenate(stack, axis=0), kdups[i][g]))
        scores.append(per_group)
    between()
    probs, inv = [], []
    for head in range(N_Q_HEADS):
        g, hh = divmod(head, heads_per_group)
        parts = [scores[i][g][hh * rows:(hh + 1) * rows] for i in range(n)]
        s = parts[0] if n == 1 else jnp.concatenate(parts, axis=0)
        sink = sink_ref[head] * LOG2E
        s = jnp.where(mask, s, -1e30)
        mx = jnp.maximum(jnp.max(s, axis=-1, keepdims=True), sink)
        p = jnp.exp2(s - mx)
        inv.append(1.0 / (jnp.sum(p, axis=-1, keepdims=True) + jnp.exp2(sink - mx)))
        probs.append(p.astype(bf16))
    between()
    outs = []
    for i in range(n):
        per_group = []
        for g in range(N_KV_HEADS):
            p_i = jnp.concatenate([probs[g * heads_per_group + hh][i * rows:(i + 1) * rows]
                                   for hh in range(heads_per_group)], axis=0)
            per_group.append(_dot(p_i, vdups[i][g]))
        outs.append(per_group)
    low_all = lax.broadcasted_iota(jnp.int32, (n * rows, LANES), 1) < HEAD_DIM
    for pair in range(N_Q_HEADS // 2):
        halves = []
        for head in (2 * pair, 2 * pair + 1):
            g, hh = divmod(head, heads_per_group)
            parts = [outs[i][g][hh * rows:(hh + 1) * rows] for i in range(n)]
            o = parts[0] if n == 1 else jnp.concatenate(parts, axis=0)
            halves.append(o * inv[head])
        o_ref[:, pair * LANES:(pair + 1) * LANES] = jnp.where(low_all, halves[0], halves[1]).astype(o_ref.dtype)


def _swa_prompt_blocks(q_ref, k_before, k2_ref, v_before, v2_ref, first, sink_ref, o_ref,
                       between=lambda: None):
    w = WINDOW
    nblk = q_ref.shape[0] // w

    def dup(before, ref):
        blocks = [before] + [ref[i * w:(i + 1) * w, :] for i in range(nblk)]
        return [[jnp.concatenate([blocks[i][:, g * LANES:(g + 1) * LANES],
                                  blocks[i + 1][:, g * LANES:(g + 1) * LANES]], axis=0)
                 for g in range(N_KV_HEADS)] for i in range(nblk)]

    row = lax.broadcasted_iota(jnp.int32, (nblk * w, 2 * w), 0)
    tk = lax.broadcasted_iota(jnp.int32, (nblk * w, 2 * w), 1)
    rel = tk - (row & (w - 1))
    first_key = jnp.where(row < w, jnp.where(first, w, 0), 0)
    mask = (rel > 0) & (rel <= w) & (tk >= first_key)
    _sink_attention(q_ref, w, dup(k_before, k2_ref), dup(v_before, v2_ref), mask, sink_ref, o_ref, between)


def _swa_sample_kernel(sink_ref, q_ref, kn_ref, vn_ref, pk_ref, pv_ref, o_ref, ko_ref, vo_ref, *, seq):
    nseq, _, _, w = pk_ref.shape
    rows = nseq * seq
    heads_per_group = N_Q_HEADS // N_KV_HEADS
    lane = lax.broadcasted_iota(jnp.int32, (rows, LANES), 1)
    r_id = lax.broadcasted_iota(jnp.int32, (rows, LANES), 0)
    low = lane < HEAD_DIM
    pos = r_id & (seq - 1)
    mask_old = lane > pos
    seq_shift = seq.bit_length() - 1
    mask_new = ((r_id >> seq_shift) == (lane >> seq_shift)) & ((lane & (seq - 1)) <= pos)
    kn = kn_ref[...]
    vn = vn_ref[...]
    kn_dup = [d.astype(bf16) for d in _dup_heads(kn, low)]
    vn_dup = [d.astype(bf16) for d in _dup_heads(vn, low)]

    pad = jnp.zeros((w - seq, LANES), f32)
    tail_lanes = lax.broadcasted_iota(jnp.int32, (HEAD_DIM, w), 1) >= w - seq
    kt_dup, vt_dup = [], []
    for b in range(nseq):
        tok = slice(b * seq, (b + 1) * seq)
        per_k, per_v = [], []
        for new, old_ref, out_ref, per in ((kn, pk_ref, ko_ref, per_k), (vn, pv_ref, vo_ref, per_v)):
            new_t = jnp.concatenate([pad, new[tok]], axis=0).T
            for g in range(N_KV_HEADS):
                old = old_ref[b, g]
                out_ref[b, g] = jnp.where(tail_lanes, new_t[g * HEAD_DIM:(g + 1) * HEAD_DIM],
                                          pltpu.roll(old, w - seq, axis=1))
                ob = old.astype(bf16)
                per.append(jnp.concatenate([ob, ob], axis=0))
        kt_dup.append(per_k)
        vt_dup.append(per_v)

    lhs = []
    for g in range(N_KV_HEADS):
        stack = []
        for p in range(heads_per_group // 2):
            pair = g * (heads_per_group // 2) + p
            qp = q_ref[:, pair * LANES:(pair + 1) * LANES].astype(bf16)
            zero = jnp.zeros_like(qp)
            stack.append(jnp.where(low, qp, zero))
            stack.append(jnp.where(low, zero, qp))
        lhs.append(jnp.concatenate(stack, axis=0))
    s_new = [_dot_nt(lhs[g], kn_dup[g]) for g in range(N_KV_HEADS)]
    s_old = []
    for b in range(nseq):
        per = []
        for g in range(N_KV_HEADS):
            qb = jnp.concatenate([lhs[g][hh * rows + b * seq:hh * rows + (b + 1) * seq]
                                  for hh in range(heads_per_group)], axis=0)
            per.append(_dot(qb, kt_dup[b][g]))
        s_old.append(per)

    p_old, p_new, inv = [], [], []
    for head in range(N_Q_HEADS):
        g, hh = divmod(head, heads_per_group)
        so = jnp.concatenate([s_old[b][g][hh * seq:(hh + 1) * seq] for b in range(nseq)], axis=0)
        so = jnp.where(mask_old, so, -1e30)
        sn = jnp.where(mask_new, s_new[g][hh * rows:(hh + 1) * rows], -1e30)
        sink = sink_ref[head] * LOG2E
        mx = jnp.maximum(jnp.maximum(jnp.max(so, axis=-1, keepdims=True),
                                     jnp.max(sn, axis=-1, keepdims=True)), sink)
        po = jnp.exp2(so - mx)
        pn = jnp.exp2(sn - mx)
        inv.append(1.0 / (jnp.sum(po, axis=-1, keepdims=True) + jnp.sum(pn, axis=-1, keepdims=True)
                          + jnp.exp2(sink - mx)))
        p_old.append(po.astype(bf16))
        p_new.append(pn.astype(bf16))

    o_new = [_dot(jnp.concatenate([p_new[g * heads_per_group + hh] for hh in range(heads_per_group)], axis=0),
                  vn_dup[g]) for g in range(N_KV_HEADS)]
    o_old = []
    for b in range(nseq):
        per = []
        for g in range(N_KV_HEADS):
            pb = jnp.concatenate([p_old[g * heads_per_group + hh][b * seq:(b + 1) * seq]
                                  for hh in range(heads_per_group)], axis=0)
            per.append(_dot_nt(pb, vt_dup[b][g]))
        o_old.append(per)

    for pair in range(N_Q_HEADS // 2):
        halves = []
        for head in (2 * pair, 2 * pair + 1):
            g, hh = divmod(head, heads_per_group)
            old = jnp.concatenate([o_old[b][g][hh * seq:(hh + 1) * seq] for b in range(nseq)], axis=0)
            halves.append((old + o_new[g][hh * rows:(hh + 1) * rows]) * inv[head])
        o_ref[:, pair * LANES:(pair + 1) * LANES] = jnp.where(low, halves[0], halves[1])


def _swa_sample(sq, sk, sv, past_kt, past_vt, sinks, bsz, t):
    nseq = LANES // t
    tok = lambda width: pl.BlockSpec((nseq * t, width), lambda i, s: (i, 0))
    cache = pl.BlockSpec((nseq,) + past_kt.shape[1:], lambda i, s: (i, 0, 0, 0))
    return pl.pallas_call(
        functools.partial(_swa_sample_kernel, seq=t),
        grid_spec=pltpu.PrefetchScalarGridSpec(
            num_scalar_prefetch=1,
            grid=(bsz // nseq,),
            in_specs=[tok(SWA_Q), tok(SWA_KV), tok(SWA_KV), cache, cache],
            out_specs=[tok(SWA_Q), cache, cache]),
        out_shape=[jax.ShapeDtypeStruct((bsz * t, SWA_Q), f32),
                   jax.ShapeDtypeStruct(past_kt.shape, f32),
                   jax.ShapeDtypeStruct(past_vt.shape, f32)],
        compiler_params=pltpu.CompilerParams(
            dimension_semantics=("parallel",), vmem_limit_bytes=VMEM_LIMIT),
        name="swa_sample",
    )(sinks, sq, sk, sv, past_kt, past_vt)


FFN_CHUNK = 512


def _out_proj_norm(streams, n2, wout_ref):
    mix = [s[1] for s in streams]
    mix = mix[0] if len(mix) == 1 else jnp.concatenate(mix, axis=0)
    mixed = _dot(mix[:, :GLA_WIDTH], wout_ref[:GLA_WIDTH, :]) + _dot(mix[:, GLA_WIDTH:], wout_ref[GLA_WIDTH:, :])
    hs, hbs, r0 = [], [], 0
    for x, _, g1, sh2, sc2 in streams:
        nb, t, d = x.shape
        h = x + g1 * mixed[r0:r0 + nb * t].reshape(nb, t, d)
        ms = jnp.mean(h * h, axis=-1, keepdims=True)
        hn = h * lax.rsqrt(ms + EPS) * n2
        hn = hn * (1.0 + sc2) + sh2
        hs.append(h)
        hbs.append(hn.reshape(nb * t, d).astype(bf16))
        r0 += nb * t
    return hs, (hbs[0] if len(hbs) == 1 else jnp.concatenate(hbs, axis=0))


def _ffn_piece(hb, w1_ref, w2_ref, c, width):
    cols = slice(c * width, (c + 1) * width)
    a = jnp.maximum(_dot(hb, w1_ref[:, cols]), 0.0)
    return _dot((a * a).astype(bf16), w2_ref[cols, :])


def _resident(shape):
    return pl.BlockSpec(shape, lambda *_: (0,) * len(shape), pipeline_mode=pl.Buffered(1))


def _prompt_kernel(sink_ref,
                   q_ref, k_ref, v_ref, la_ref, r_ref, sq_ref, k2_ref, v2_ref, k2p_ref, v2p_ref,
                   tri_ref, masks_ref, gnw_ref,
                   x_ref, g1_ref, sh2_ref, sc2_ref, g2_ref,
                   xs_ref, gos_ref, sos_ref, g1s_ref, sh2s_ref, sc2s_ref, g2s_ref,
                   n2_ref, wout_ref, w1_ref, w2_ref,
                   y_ref, ys_ref, sout_ref, state_ref, mix_ref, *, levels, nt, nsteps):
    s = pl.program_id(0)
    slot = s % 2

    @pl.when(s == 0)
    def _():
        mix_ref[1] = jnp.zeros(mix_ref.shape[1:], mix_ref.dtype)
        state_ref[...] = jnp.zeros_like(state_ref)

    j = jnp.minimum(s, nsteps - 1) % nt
    st_in = state_ref[...]
    st = jnp.where(j == 0, 0.0, st_in)
    gnw = gnw_ref[...]
    out = mix_ref.at[slot]
    mix_s = jnp.concatenate([gos_ref[...], sos_ref[...]], axis=1).astype(bf16)
    (h, h_s), hb = _out_proj_norm(
        [(x_ref[...], mix_ref[1 - slot], g1_ref[...], sh2_ref[...], sc2_ref[...]),
         (xs_ref[...], mix_s, g1s_ref[...], sh2s_ref[...], sc2s_ref[...])], n2_ref[...], wout_ref)
    rows_p = h.shape[0] * h.shape[1]
    todo = list(range(D_FF // FFN_CHUNK))
    ff = []

    def ffn_piece():
        if todo:
            part = _ffn_piece(hb, w1_ref, w2_ref, todo.pop(0), FFN_CHUNK)
            ff[:] = [part if not ff else ff[0] + part]

    st = _gla_prompt_block(st, q_ref, k_ref, v_ref, la_ref, r_ref, tri_ref, masks_ref, gnw,
                           out.at[:, pl.ds(0, GLA_WIDTH)], levels, ffn_piece)
    _swa_prompt_blocks(sq_ref, k2p_ref[...], k2_ref, v2p_ref[...], v2_ref, j == 0, sink_ref,
                       out.at[:, pl.ds(GLA_WIDTH, SWA_Q)], ffn_piece)
    while todo:
        ffn_piece()
    st = jnp.where(s < nsteps, st, st_in)
    state_ref[...] = st
    sout_ref[0] = st.T
    y_ref[...] = h + g2_ref[...] * ff[0][:rows_p].reshape(h.shape)
    ys_ref[...] = h_s + g2s_ref[...] * ff[0][rows_p:].reshape(h_s.shape)


def _prompt_mix_ffn(x, gq, gk, gv, la, gr, sq, k2, v2, gnw, sinks, mod, row0, n2, wout, w1, w2,
                    xs, gos, sos, row0_s):
    bsz, t, d = x.shape
    tb = FUSED_TOKEN_BLOCK
    nt = t // tb
    nsteps = bsz * nt
    levels, tri, masks = _gla_constants(PROMPT_GLA_CHUNK, PROMPT_GLA_CHUNK)
    per_blk = tb // WINDOW
    bs, ts, _ = xs.shape
    nbs = bs // nsteps
    assert nbs * nsteps == bs and (nbs * ts) % 16 == 0

    def mix_blk(s):
        return jnp.minimum(s, nsteps - 1)

    def ffn_blk(s):
        return jnp.maximum(s - 1, 0)

    def tok(width):
        return pl.BlockSpec((tb, width), lambda s, _: (mix_blk(s), 0))

    prev = pl.BlockSpec((WINDOW, SWA_KV2), lambda s, _: (jnp.maximum(mix_blk(s) * per_blk - 1, 0), 0))
    xspec = pl.BlockSpec((1, tb, d), lambda s, _: (ffn_blk(s) // nt, ffn_blk(s) % nt, 0))
    ffn_chunks = (MOD_GATE1, MOD_SHIFT2, MOD_SCALE2, MOD_GATE2)
    mods = [_mod_spec(1, row0, chunk, lambda s, _: ffn_blk(s) // nt) for chunk in ffn_chunks]
    mods_s = [_mod_spec(nbs, row0_s, chunk, lambda s, _: ffn_blk(s)) for chunk in ffn_chunks]
    xs_spec = pl.BlockSpec((nbs, ts, d), lambda s, _: (ffn_blk(s), 0, 0))
    mix_s_spec = pl.BlockSpec((nbs * ts, GLA_WIDTH), lambda s, _: (ffn_blk(s), 0))
    return pl.pallas_call(
        functools.partial(_prompt_kernel, levels=levels, nt=nt, nsteps=nsteps),
        grid_spec=pltpu.PrefetchScalarGridSpec(
            num_scalar_prefetch=1,
            grid=(nsteps + 1,),
            in_specs=[tok(GLA_QK), tok(GLA_QK), tok(GLA_WIDTH), tok(GLA_QK), tok(GLA_WIDTH),
                      tok(SWA_Q), tok(SWA_KV2), tok(SWA_KV2), prev, prev,
                      _resident(tri.shape), _resident(masks.shape), _resident((1, GLA_DV)),
                      xspec, *mods, xs_spec, mix_s_spec, mix_s_spec, *mods_s, _resident((1, 1, d)),
                      _resident((d, d)), _resident((d, D_FF)), _resident((D_FF, d))],
            out_specs=[xspec, xs_spec,
                       pl.BlockSpec((1, GLA_QK, GLA_DV), lambda s, _: (mix_blk(s) // nt, 0, 0))],
            scratch_shapes=[pltpu.VMEM((GLA_DV, GLA_QK), f32),
                            pltpu.VMEM((2, tb, GLA_WIDTH + SWA_Q), bf16)]),
        out_shape=[jax.ShapeDtypeStruct((bsz, t, d), f32),
                   jax.ShapeDtypeStruct(xs.shape, f32),
                   jax.ShapeDtypeStruct((bsz, GLA_QK, GLA_DV), f32)],
        compiler_params=pltpu.CompilerParams(
            dimension_semantics=("arbitrary",), vmem_limit_bytes=VMEM_LIMIT),
        name="prompt_mix_ffn",
    )(sinks, gq, gk, gv, la, gr, sq, k2, v2, k2, v2, tri, masks, gnw,
      x, mod, mod, mod, mod, xs, gos, sos, mod, mod, mod, mod, n2, wout, w1, w2)


def _rope_tables(pos, copies):
    half = HEAD_DIM // 2
    inv = np.power(np.float32(ROPE_THETA), -np.arange(half, dtype=np.float32) * np.float32(2.0 / HEAD_DIM))
    ang = np.asarray(pos, np.float32)[:, None] * inv[None, :].astype(np.float32)
    cos = np.cos(ang).astype(np.float32)
    sin = np.sin(ang).astype(np.float32)
    reps = (copies, LANES // HEAD_DIM)
    return (jnp.asarray(np.tile(np.concatenate([cos, cos], axis=-1), reps)),
            jnp.asarray(np.tile(np.concatenate([-sin, sin], axis=-1), reps)))


def _block_diag_ones(n, blk):
    idx = np.arange(n) // blk
    return jnp.asarray((idx[:, None] == idx[None, :]).astype(np.float32), bf16)


def _layer_weights(w_in, w_gate_up, b_gate, q_norm_w, k_norm_w, w_out, w_ff1, w_ff2):
    splits = np.cumsum([GLA_QK, GLA_QK, GLA_WIDTH, GLA_WIDTH, GLA_GATE_RANK, SWA_Q, SWA_KV])
    gq, gk, gv, gr, glr, sq, sk, sv = jnp.split(w_in, [int(s) for s in splits], axis=1)
    glr = jnp.pad(glr, ((0, 0), (0, LANES - GLA_GATE_RANK)))
    win = jnp.concatenate([gq, gk, gv, gr, sq, sk, sv, glr], axis=1).astype(bf16)
    wgu = jnp.pad(w_gate_up, ((0, LANES - GLA_GATE_RANK), (0, 0))).astype(bf16)
    return dict(
        win=win, wgu=wgu, bg=b_gate.reshape(1, GLA_QK),
        qnw=jnp.tile(q_norm_w, N_Q_HEADS).reshape(1, SWA_Q),
        knw=jnp.tile(k_norm_w, N_KV_HEADS).reshape(1, SWA_KV),
        ffn_f32=(w_out, w_ff1, w_ff2))


def _project(x, mod, row0, pos, lw, n1, bdq, bdk, prompt):
    bsz, t, d = x.shape
    if prompt:
        nb, tb, tail, act, groups, cast = 1, TOKEN_BLOCK, WINDOW, bf16, PROMPT_INPROJ_GROUPS, lw["ffn_f32"]
    else:
        nb, tb, tail, act, groups, cast = SAMPLE_ROWS // t, t, SAMPLE_ROWS, f32, 1, ()
    cos_t, sin_t = _rope_tables(pos, nb)
    return _inproj(x, mod, row0, n1.reshape(1, 1, d), lw["win"], lw["wgu"], lw["bg"], lw["qnw"], lw["knw"],
                   bdq, bdk, cos_t, sin_t, nb, tb, tail, act, groups, cast, tail_t=prompt)


def _decoder_layers(xp, xs, mod, pos_p, pos_s, state, past_k, past_v, lw, n1, n2, gnw, sinks, bdq, bdk):
    bp, tp, d = xp.shape
    bs, ts, _ = xs.shape
    heads = (N_KV_HEADS, HEAD_DIM)
    gq, gk, gv, gr, la, sq, _, _, kk, vk = _project(xs, mod, 0, pos_s, lw, n1, bdq, bdk, False)
    go_s, state_s = _gla_sample(gq, gk, gv, la, gr, state.reshape(bs, GLA_QK, GLA_DV), gnw, bs, ts)
    so_s, kt, vt = _swa_sample(sq, kk, vk, past_k.transpose(0, 2, 3, 1), past_v.transpose(0, 2, 3, 1),
                               sinks, bs, ts)
    gq, gk, gv, gr, la, sq, k2, v2, kk, vk, *ffn_w = _project(xp, mod, bs, pos_p, lw, n1, bdq, bdk, True)
    yp, ys, s_t = _prompt_mix_ffn(xp, gq, gk, gv, la, gr, sq, k2, v2, gnw, sinks, mod, bs,
                                  n2.reshape(1, 1, d), *ffn_w, xs, go_s, so_s, 0)
    state_p = s_t.reshape(bp, GLA_HEADS, GLA_DK, GLA_DV)
    return (yp, ys, state_p,
            kk.reshape(bp, *heads, WINDOW).transpose(0, 3, 1, 2), vk.reshape(bp, *heads, WINDOW).transpose(0, 3, 1, 2),
            state_s.reshape(bs, GLA_HEADS, GLA_DK, GLA_DV),
            kt.transpose(0, 3, 1, 2), vt.transpose(0, 3, 1, 2))


def kernel(x_prompt, x_sample, state_gla, cache_swa_k, cache_swa_v, c_prompt, c_sample, w_ada, b_ada, norm1_w, norm2_w, w_in, w_gate_up, b_gate, gla_norm_w, q_norm_w, k_norm_w, sinks, w_out, w_ff1, w_ff2):
    depth = w_ada.shape[0]
    bp, tp, _ = x_prompt.shape
    bs, ts, _ = x_sample.shape
    pos_p = np.arange(tp)
    pos_s = PAST_LEN + np.arange(ts)
    bdq = _block_diag_ones(SWA_Q, HEAD_DIM)
    bdk = _block_diag_ones(SWA_KV, HEAD_DIM)
    yp, ys = x_prompt, x_sample
    outs = [[] for _ in range(6)]
    for l in range(depth):
        mod = _modulation(c_sample, c_prompt, w_ada[l], b_ada[l])
        lw = _layer_weights(w_in[l], w_gate_up[l], b_gate[l], q_norm_w[l], k_norm_w[l],
                            w_out[l], w_ff1[l], w_ff2[l])
        gnw = gla_norm_w[l].reshape(1, GLA_DV)
        yp, ys, *new = _decoder_layers(yp, ys, mod, pos_p, pos_s, state_gla[l], cache_swa_k[l],
                                       cache_swa_v[l], lw, norm1_w[l], norm2_w[l], gnw, sinks[l], bdq, bdk)
        for lst, val in zip(outs, new):
            lst.append(val)
    return (yp, ys) + tuple(jnp.stack(o) for o in outs)
```

```python
import functools

import jax
import jax.numpy as jnp
import numpy as np
from jax import lax
from jax.experimental import pallas as pl
from jax.experimental.pallas import tpu as pltpu

f32 = jnp.float32
bf16 = jnp.bfloat16

D_MODEL = 1024
GLA_HEADS = 4
GLA_DK = 64
GLA_DV = 128
GLA_QK = GLA_HEADS * GLA_DK
GLA_WIDTH = GLA_HEADS * GLA_DV
GLA_GATE_RANK = 16
GLA_TAU = 16.0
LOG2E = 1.4426950408889634
HEAD_DIM = 64
N_Q_HEADS = 8
N_KV_HEADS = 2
SWA_Q = N_Q_HEADS * HEAD_DIM
SWA_KV = N_KV_HEADS * HEAD_DIM
SWA_KV2 = 2 * SWA_KV
WINDOW = 128
ROPE_THETA = 10000.0
PAST_LEN = 8192
D_FF = 4 * D_MODEL
EPS = 1e-6
LANES = 128
GLA_CHUNK = 128
PROMPT_GLA_CHUNK = 128
TOKEN_BLOCK = 1024
FUSED_TOKEN_BLOCK = 512
SAMPLE_ROWS = 512
PROMPT_INPROJ_GROUPS = 2
VMEM_LIMIT = 56 * 1024 * 1024

_SEG = {}
_off = 0
for _name, _w in (("gq", GLA_QK), ("gk", GLA_QK), ("gv", GLA_WIDTH), ("gr", GLA_WIDTH),
                  ("sq", SWA_Q), ("sk", SWA_KV), ("sv", SWA_KV), ("glr", LANES)):
    _SEG[_name] = (_off, _off + _w)
    _off += _w
IN_WIDTH_PADDED = _off


def _dot(a, b):
    return jnp.dot(a, b, preferred_element_type=f32)


def _dot_nt(a, b):
    return lax.dot_general(a, b, (((1,), (1,)), ((), ())), preferred_element_type=f32)


def _dot_tn(a, b):
    return lax.dot_general(a, b, (((0,), (0,)), ((), ())), preferred_element_type=f32)


def _sigmoid(x):
    return 1.0 / (1.0 + jnp.exp(-x))


MOD_W_SLABS = 4


def _mod_kernel(ca_ref, cb_ref, *refs):
    w_refs, b_ref, o_ref = refs[:-2], refs[-2], refs[-1]
    c = jnp.concatenate([ca_ref[...], cb_ref[...]], axis=0)
    s = (c * _sigmoid(c)).astype(bf16)
    kb = w_refs[0].shape[0]
    res = b_ref[...]
    for i, w_ref in enumerate(w_refs):
        res = res + _dot(s[:, i * kb:(i + 1) * kb], w_ref[...].astype(bf16))
    for r in range(res.shape[0]):
        o_ref[r] = res[r:r + 1, :]


def _modulation(c_a, c_b, w_ada, b_ada):
    m = c_a.shape[0] + c_b.shape[0]
    n = w_ada.shape[1]
    bn = 1536
    kb = D_MODEL // MOD_W_SLABS
    slabs = [pl.BlockSpec((kb, bn), lambda j, i=i: (i, j)) for i in range(MOD_W_SLABS)]
    return pl.pallas_call(
        _mod_kernel,
        grid=(n // bn,),
        in_specs=[pl.BlockSpec(c_a.shape, lambda j: (0, 0)), pl.BlockSpec(c_b.shape, lambda j: (0, 0))]
        + slabs + [pl.BlockSpec((1, bn), lambda j: (0, j))],
        out_specs=pl.BlockSpec((m, 1, bn), lambda j: (0, 0, j)),
        out_shape=jax.ShapeDtypeStruct((m, 1, n), f32),
        compiler_params=pltpu.CompilerParams(vmem_limit_bytes=VMEM_LIMIT),
        name="adaln_mod",
    )(c_a, c_b, *([w_ada] * MOD_W_SLABS), b_ada.reshape(1, n))


def _group_rms(x, bd_ref, w_ref):
    ssq = _dot((x * x).astype(bf16), bd_ref[...])
    return x * lax.rsqrt(ssq * (1.0 / HEAD_DIM) + EPS) * w_ref[...]


def _rope(x, cos, sin_signed, low_half):
    partner = jnp.where(low_half, pltpu.roll(x, LANES - 32, axis=1), pltpu.roll(x, 32, axis=1))
    return x * cos + partner * sin_signed


def _dup_heads(x, low_lanes):
    rolled = pltpu.roll(x, HEAD_DIM, axis=1)
    return jnp.where(low_lanes, x, rolled), jnp.where(low_lanes, rolled, x)


def _inproj_kernel(x_ref, sh_ref, sc_ref, n1_ref, win_ref, wgu_ref, bg_ref, qnw_ref, knw_ref,
                   bdq_ref, bdk_ref, cos_ref, sin_ref, *rest, groups, ncast, tail, tail_t):
    slabs, rest = rest[:ncast], rest[ncast:]
    gq_ref, gk_ref, gv_ref, gr_ref, la_ref, sq_ref, k2_ref, v2_ref, kk_ref, vk_ref = rest[:10]
    for src, dst in zip(slabs, rest[10:]):
        dst[...] = src[...].astype(dst.dtype)
    nb, t, d = x_ref.shape
    m = nb * t
    mg = m // groups
    lane = lax.broadcasted_iota(jnp.int32, (mg, LANES), 1)
    low_half = (lane & 32) == 0
    low_lanes = lane < HEAD_DIM
    for grp in range(groups):
        rows = slice(grp * mg, (grp + 1) * mg)
        if nb == 1:
            x = x_ref[:, rows, :]
            sc, sh = sc_ref[...], sh_ref[...]
        else:
            seqs = slice(grp * (nb // groups), (grp + 1) * (nb // groups))
            x = x_ref[seqs]
            sc, sh = sc_ref[seqs], sh_ref[seqs]
        ms = jnp.mean(x * x, axis=-1, keepdims=True)
        hn = x * lax.rsqrt(ms + EPS) * n1_ref[...]
        hn = hn * (1.0 + sc) + sh
        hb = hn.reshape(mg, d).astype(bf16)

        def seg(name):
            a, b = _SEG[name]
            return _dot(hb, win_ref[:, a:b])

        cos = cos_ref[rows, :]
        sin = sin_ref[rows, :]
        sq = _group_rms(seg("sq"), bdq_ref, qnw_ref)
        for c in range(SWA_Q // LANES):
            blk = _rope(sq[:, c * LANES:(c + 1) * LANES], cos, sin, low_half)
            sq_ref[rows, c * LANES:(c + 1) * LANES] = (blk * (LOG2E * HEAD_DIM ** -0.5)).astype(sq_ref.dtype)
        sk = _rope(_group_rms(seg("sk"), bdk_ref, knw_ref), cos, sin, low_half)
        sv = seg("sv")
        keep = (grp + 1) * mg - (m - tail)
        if keep > 0:
            keep = min(keep, mg)
            dst_rows = slice((grp + 1) * mg - keep - (m - tail), (grp + 1) * mg - (m - tail))
            if tail_t:
                assert keep == tail
                kk_ref[...] = sk[mg - keep:].T
                vk_ref[...] = sv[mg - keep:].T
            else:
                kk_ref[dst_rows, :] = sk[mg - keep:]
                vk_ref[dst_rows, :] = sv[mg - keep:]
        for src, dst in ((sk, k2_ref), (sv, v2_ref)):
            d0, d1 = _dup_heads(src, low_lanes)
            dst[rows, :LANES] = d0.astype(dst.dtype)
            dst[rows, LANES:] = d1.astype(dst.dtype)
        glr = seg("glr").astype(bf16)
        g = _dot(glr, wgu_ref[...]) + bg_ref[...]
        log_sig = jnp.minimum(g, 0.0) - jnp.log1p(jnp.exp(-jnp.abs(g)))
        la_ref[rows, :] = log_sig * (LOG2E / GLA_TAU)
        gq_ref[rows, :] = seg("gq") * (GLA_DK ** -0.5)
        gk_ref[rows, :] = seg("gk")
        gv_ref[rows, :] = seg("gv").astype(gv_ref.dtype)
        gr_ref[rows, :] = seg("gr").astype(gr_ref.dtype)


MOD_SHIFT1, MOD_SCALE1, MOD_GATE1, MOD_SHIFT2, MOD_SCALE2, MOD_GATE2 = range(6)


def _mod_spec(nb, row0, chunk, batch_block):
    return pl.BlockSpec((nb, 1, D_MODEL), lambda *ids: (row0 // nb + batch_block(*ids), 0, chunk))


def _inproj(x, mod, row0, n1, win, wgu, bg, qnw, knw, bdq, bdk, cos_t, sin_t, nb, tb, tail, act, groups,
            to_bf16=(), tail_t=False):
    bsz, t, d = x.shape
    m = nb * tb
    nt = t // tb
    grid = (bsz // nb, nt)
    tok = bsz * t
    nsteps = grid[0] * grid[1]
    slab_specs = [pl.BlockSpec((w.shape[0] // nsteps, w.shape[1]), lambda i, j: (i * nt + j, 0))
                  for w in to_bf16]

    def full(shape):
        return pl.BlockSpec(shape, lambda i, j: (0,) * len(shape))

    def out(width):
        return pl.BlockSpec((m, width), lambda i, j: (i * nt + j, 0))

    tail_blk = (SWA_KV, tail) if tail_t else (tail, SWA_KV)
    tail_spec = pl.BlockSpec(tail_blk, lambda i, j: (i, 0))
    tail_shape = jax.ShapeDtypeStruct(((bsz // nb) * tail_blk[0], tail_blk[1]), f32)
    outs = ((GLA_QK, f32), (GLA_QK, f32), (GLA_WIDTH, act), (GLA_WIDTH, act), (GLA_QK, f32),
            (SWA_Q, act), (SWA_KV2, bf16), (SWA_KV2, bf16))
    return pl.pallas_call(
        functools.partial(_inproj_kernel, groups=groups, ncast=len(to_bf16), tail=tail, tail_t=tail_t),
        grid=grid,
        in_specs=[pl.BlockSpec((nb, tb, d), lambda i, j: (i, j, 0)),
                  _mod_spec(nb, row0, MOD_SHIFT1, lambda i, j: i),
                  _mod_spec(nb, row0, MOD_SCALE1, lambda i, j: i),
                  full((1, 1, d)),
                  full((d, IN_WIDTH_PADDED)),
                  full((LANES, GLA_QK)),
                  full((1, GLA_QK)),
                  full((1, SWA_Q)),
                  full((1, SWA_KV)),
                  full((SWA_Q, SWA_Q)),
                  full((SWA_KV, SWA_KV)),
                  pl.BlockSpec((m, LANES), lambda i, j: (j, 0)),
                  pl.BlockSpec((m, LANES), lambda i, j: (j, 0))] + slab_specs,
        out_specs=[out(w) for w, _ in outs] + [tail_spec, tail_spec] + slab_specs,
        out_shape=[jax.ShapeDtypeStruct((tok, w), dt) for w, dt in outs]
        + [tail_shape] * 2
        + [jax.ShapeDtypeStruct(w.shape, bf16) for w in to_bf16],
        compiler_params=pltpu.CompilerParams(
            dimension_semantics=("parallel", "arbitrary"), vmem_limit_bytes=VMEM_LIMIT),
        name="inproj",
    )(x, mod, mod, n1, win, wgu, bg, qnw, knw, bdq, bdk, cos_t, sin_t, *to_bf16)


def _gla_constants(chunk, seq):
    t = np.arange(chunk)
    levels = []
    m = 1
    while m < seq:
        levels.append(m)
        m *= 2
    masks = [np.eye(chunk, dtype=bool)]
    for m in levels:
        upper = (t % (2 * m) >= m)[:, None]
        lower = (t % (2 * m) < m)[None, :]
        same = (t[:, None] // (2 * m)) == (t[None, :] // (2 * m))
        masks.append(same & upper & lower)
    tri = t[None, :] <= t[:, None]
    tiled = np.tile(np.stack(masks).astype(np.float32), (1, 1, GLA_HEADS))
    return tuple(levels), jnp.asarray(tri.astype(np.float32), bf16), jnp.asarray(tiled)


def _stack_heads(xb, lane_head, axis):
    zero = jnp.zeros_like(xb)
    return jnp.concatenate([jnp.where(lane_head == h, xb, zero) for h in range(GLA_HEADS)], axis=axis)


def _block_sums(la, bcum, m, row, rolls):
    c, n = la.shape
    if m == 1:
        return la, None
    if m < 8:
        def rolled(shift):
            if shift not in rolls:
                rolls[shift] = pltpu.roll(la, shift % c, axis=0)
            return rolls[shift]
        pos = row & (m - 1)
        pre = la
        suf = None
        for j in range(1, m):
            pre = pre + jnp.where(pos >= j, rolled(j), 0.0)
            term = jnp.where(pos < m - j, rolled(-j), 0.0)
            suf = term if suf is None else suf + term
        return pre, suf
    before, last = [], []
    for i in range(c // m):
        before.append(jnp.zeros((m, n), f32) if i == 0
                      else jnp.broadcast_to(bcum[i * m - 1:i * m], (m, n)))
        last.append(jnp.broadcast_to(bcum[(i + 1) * m - 1:(i + 1) * m], (m, n)))
    if len(before) == 1:
        return bcum - before[0], last[0] - bcum
    return bcum - jnp.concatenate(before, axis=0), jnp.concatenate(last, axis=0) - bcum


def _gla_scores(q_ref, k_ref, v_ref, la_ref, tri_ref, masks_ref, levels, seq):
    c = q_ref.shape[0]
    la = la_ref[...]
    hi = la.astype(bf16)
    lo = (la - hi.astype(f32)).astype(bf16)
    tri = tri_ref[...]
    bcum = _dot(tri, hi) + _dot(tri, lo)
    q = q_ref[...]
    k = k_ref[...]
    vb = v_ref[...].astype(bf16)
    row = lax.broadcasted_iota(jnp.int32, (c, GLA_QK), 0)
    row_head_t = lax.broadcasted_iota(jnp.int32, (GLA_QK, c), 0) >> 6
    rolls = {}
    assert levels[0] == 1
    q01 = jnp.concatenate([q.astype(bf16), (q * jnp.exp2(la)).astype(bf16)], axis=0)
    r = _dot(q01, _stack_heads(k.T.astype(bf16), row_head_t, 1))
    attn = r[:c] * masks_ref[0] + r[c:] * masks_ref[1]
    for lvl, m in enumerate(levels[1:], start=2):
        pre, suf = _block_sums(la, bcum, m, row, rolls)
        kt_t = (k * jnp.exp2(suf)).T.astype(bf16)
        r = _dot((q * jnp.exp2(pre)).astype(bf16), _stack_heads(kt_t, row_head_t, 1))
        attn = attn + r * masks_ref[lvl]
    a = attn.astype(bf16)
    pre, suf = _block_sums(la, bcum, seq, row, rolls)
    qf = (q * jnp.exp2(pre)).astype(bf16)
    kf = (k * jnp.exp2(suf)).astype(bf16)
    return a, qf, kf, vb, bcum, hi, lo


def _gla_values(a, vb):
    v_head = lax.broadcasted_iota(jnp.int32, vb.shape, 1) >> 7
    return _dot(a, _stack_heads(vb, v_head, 0))


def _gla_finish(o, r_ref, gnw, go_ref):
    for h in range(GLA_HEADS):
        sl = slice(h * GLA_DV, (h + 1) * GLA_DV)
        oh = o[:, sl]
        r = r_ref[:, sl].astype(f32)
        ms = jnp.mean(oh * oh, axis=-1, keepdims=True)
        go_ref[:, sl] = (oh * lax.rsqrt(ms + EPS) * gnw * (r * _sigmoid(r))).astype(go_ref.dtype)


def _gla_prompt_block(st, q_ref, k_ref, v_ref, la_ref, r_ref, tri_ref, masks_ref, gnw, go_ref, levels,
                      between=lambda: None):
    c = PROMPT_GLA_CHUNK
    nchunk = q_ref.shape[0] // c
    lane_head = lax.broadcasted_iota(jnp.int32, (GLA_DV, GLA_QK), 1) >> 6
    row_head = lax.broadcasted_iota(jnp.int32, (GLA_QK, GLA_DV), 0) >> 6
    chunks = []
    for ci in range(nchunk):
        rows = pl.ds(ci * c, c)
        chunks.append(_gla_scores(q_ref.at[rows], k_ref.at[rows], v_ref.at[rows], la_ref.at[rows],
                                  tri_ref, masks_ref, levels, c))
        between()
    intra, upds, decays = [], [], []
    for a, _, kf, vb, bcum, _, _ in chunks:
        intra.append(_gla_values(a, vb))
        full = _dot_tn(vb, kf)
        upd = None
        for h in range(GLA_HEADS):
            term = jnp.where(lane_head == h, full[h * GLA_DV:(h + 1) * GLA_DV], 0.0)
            upd = term if upd is None else upd + term
        upds.append(upd)
        decays.append(jnp.exp2(bcum[c - 1:c]))
    between()
    for ci in range(nchunk):
        rows = pl.ds(ci * c, c)
        sbd = _stack_heads(st.T.astype(bf16), row_head, 1)
        o = intra[ci] + _dot(chunks[ci][1], sbd)
        _gla_finish(o, r_ref.at[rows], gnw, go_ref.at[rows])
        st = decays[ci] * st + upds[ci]
    return st


def _gla_sample_kernel(q_ref, k_ref, v_ref, la_ref, r_ref, s0_ref, tri_ref, masks_ref, gnw_ref,
                       go_ref, sout_ref, *, levels, seq):
    c = q_ref.shape[0]
    a, qf, kf, vb, _, hi, lo = _gla_scores(q_ref, k_ref, v_ref, la_ref, tri_ref, masks_ref, levels, seq)
    o_intra = _gla_values(a, vb)
    row_head = lax.broadcasted_iota(jnp.int32, (GLA_QK, GLA_DV), 0) >> 6
    ones = jnp.ones((seq, GLA_DV), bf16)
    inter = []
    for b in range(c // seq):
        rows = slice(b * seq, (b + 1) * seq)
        s_old = s0_ref[b]
        sbd = _stack_heads(s_old.astype(bf16), row_head, 1)
        inter.append(_dot(qf[rows], sbd))
        p = _dot_tn(kf[rows], vb[rows])
        upd = None
        for h in range(GLA_HEADS):
            term = jnp.where(row_head == h, p[:, h * GLA_DV:(h + 1) * GLA_DV], 0.0)
            upd = term if upd is None else upd + term
        total = _dot_tn(hi[rows], ones) + _dot_tn(lo[rows], ones)
        sout_ref[b] = jnp.exp2(total) * s_old + upd
    o = o_intra + jnp.concatenate(inter, axis=0)
    _gla_finish(o, r_ref, gnw_ref[...], go_ref)


def _gla_sample(gq, gk, gv, la, gr, state, gnw, bsz, t):
    c = GLA_CHUNK
    nseq = c // t
    levels, tri, masks = _gla_constants(c, t)

    def tok(width):
        return pl.BlockSpec((c, width), lambda i: (i, 0))

    return pl.pallas_call(
        functools.partial(_gla_sample_kernel, levels=levels, seq=t),
        grid=(bsz // nseq,),
        in_specs=[tok(GLA_QK), tok(GLA_QK), tok(GLA_WIDTH), tok(GLA_QK), tok(GLA_WIDTH),
                  pl.BlockSpec((nseq, GLA_QK, GLA_DV), lambda i: (i, 0, 0)),
                  pl.BlockSpec(tri.shape, lambda i: (0, 0)),
                  pl.BlockSpec(masks.shape, lambda i: (0, 0, 0)),
                  pl.BlockSpec((1, GLA_DV), lambda i: (0, 0))],
        out_specs=[tok(GLA_WIDTH),
                   pl.BlockSpec((nseq, GLA_QK, GLA_DV), lambda i: (i, 0, 0))],
        out_shape=[jax.ShapeDtypeStruct((bsz * t, GLA_WIDTH), f32),
                   jax.ShapeDtypeStruct((bsz, GLA_QK, GLA_DV), f32)],
        compiler_params=pltpu.CompilerParams(
            dimension_semantics=("parallel",), vmem_limit_bytes=VMEM_LIMIT),
        name="gla_sample",
    )(gq, gk, gv, la, gr, state, tri, masks, gnw)


def _sink_attention(q_ref, rows, kdups, vdups, mask, sink_ref, o_ref, between=lambda: None):
    n = len(kdups)
    heads_per_group = N_Q_HEADS // N_KV_HEADS
    low_q = lax.broadcasted_iota(jnp.int32, (rows, LANES), 1) < HEAD_DIM
    scores = []
    for i in range(n):
        per_group = []
        for g in range(N_KV_HEADS):
            stack = []
            for p in range(heads_per_group // 2):
                pair = g * (heads_per_group // 2) + p
                qp = q_ref[i * rows:(i + 1) * rows, pair * LANES:(pair + 1) * LANES].astype(bf16)
                zero = jnp.zeros_like(qp)
                stack.append(jnp.where(low_q, qp, zero))
                stack.append(jnp.where(low_q, zero, qp))
            per_group.append(_dot_nt(jnp.concatenate(stack, axis=0), kdups[i][g]))
        scores.append(per_group)
    between()
    probs, inv = [], []
    for head in range(N_Q_HEADS):
        g, hh = divmod(head, heads_per_group)
        parts = [scores[i][g][hh * rows:(hh + 1) * rows] for i in range(n)]
        s = parts[0] if n == 1 else jnp.concatenate(parts, axis=0)
        sink = sink_ref[head] * LOG2E
        s = jnp.where(mask, s, -1e30)
        mx = jnp.maximum(jnp.max(s, axis=-1, keepdims=True), sink)
        p = jnp.exp2(s - mx)
        inv.append(1.0 / (jnp.sum(p, axis=-1, keepdims=True) + jnp.exp2(sink - mx)))
        probs.append(p.astype(bf16))
    between()
    outs = []
    for i in range(n):
        per_group = []
        for g in range(N_KV_HEADS):
            p_i = jnp.concatenate([probs[g * heads_per_group + hh][i * rows:(i + 1) * rows]
                                   for hh in range(heads_per_group)], axis=0)
            per_group.append(_dot(p_i, vdups[i][g]))
        outs.append(per_group)
    low_all = lax.broadcasted_iota(jnp.int32, (n * rows, LANES), 1) < HEAD_DIM
    for pair in range(N_Q_HEADS // 2):
        halves = []
        for head in (2 * pair, 2 * pair + 1):
            g, hh = divmod(head, heads_per_group)
            parts = [outs[i][g][hh * rows:(hh + 1) * rows] for i in range(n)]
            o = parts[0] if n == 1 else jnp.concatenate(parts, axis=0)
            halves.append(o * inv[head])
        o_ref[:, pair * LANES:(pair + 1) * LANES] = jnp.where(low_all, halves[0], halves[1]).astype(o_ref.dtype)


def _swa_prompt_blocks(q_ref, k_before, k2_ref, v_before, v2_ref, first, sink_ref, o_ref,
                       between=lambda: None):
    w = WINDOW
    nblk = q_ref.shape[0] // w

    def dup(before, ref):
        blocks = [before] + [ref[i * w:(i + 1) * w, :] for i in range(nblk)]
        return [[jnp.concatenate([blocks[i][:, g * LANES:(g + 1) * LANES],
                                  blocks[i + 1][:, g * LANES:(g + 1) * LANES]], axis=0)
                 for g in range(N_KV_HEADS)] for i in range(nblk)]

    row = lax.broadcasted_iota(jnp.int32, (nblk * w, 2 * w), 0)
    tk = lax.broadcasted_iota(jnp.int32, (nblk * w, 2 * w), 1)
    rel = tk - (row & (w - 1))
    first_key = jnp.where(row < w, jnp.where(first, w, 0), 0)
    mask = (rel > 0) & (rel <= w) & (tk >= first_key)
    _sink_attention(q_ref, w, dup(k_before, k2_ref), dup(v_before, v2_ref), mask, sink_ref, o_ref, between)


def _swa_sample_kernel(sink_ref, q_ref, kn_ref, vn_ref, pk_ref, pv_ref, o_ref, ko_ref, vo_ref, *, seq):
    nseq, _, _, w = pk_ref.shape
    rows = nseq * seq
    heads_per_group = N_Q_HEADS // N_KV_HEADS
    lane = lax.broadcasted_iota(jnp.int32, (rows, LANES), 1)
    r_id = lax.broadcasted_iota(jnp.int32, (rows, LANES), 0)
    low = lane < HEAD_DIM
    pos = r_id & (seq - 1)
    mask_old = lane > pos
    seq_shift = seq.bit_length() - 1
    mask_new = ((r_id >> seq_shift) == (lane >> seq_shift)) & ((lane & (seq - 1)) <= pos)
    kn = kn_ref[...]
    vn = vn_ref[...]
    kn_dup = [d.astype(bf16) for d in _dup_heads(kn, low)]
    vn_dup = [d.astype(bf16) for d in _dup_heads(vn, low)]

    pad = jnp.zeros((w - seq, LANES), f32)
    tail_lanes = lax.broadcasted_iota(jnp.int32, (HEAD_DIM, w), 1) >= w - seq
    kt_dup, vt_dup = [], []
    for b in range(nseq):
        tok = slice(b * seq, (b + 1) * seq)
        per_k, per_v = [], []
        for new, old_ref, out_ref, per in ((kn, pk_ref, ko_ref, per_k), (vn, pv_ref, vo_ref, per_v)):
            new_t = jnp.concatenate([pad, new[tok]], axis=0).T
            for g in range(N_KV_HEADS):
                old = old_ref[b, g]
                out_ref[b, g] = jnp.where(tail_lanes, new_t[g * HEAD_DIM:(g + 1) * HEAD_DIM],
                                          pltpu.roll(old, w - seq, axis=1))
                ob = old.astype(bf16)
                per.append(jnp.concatenate([ob, ob], axis=0))
        kt_dup.append(per_k)
        vt_dup.append(per_v)

    lhs = []
    for g in range(N_KV_HEADS):
        stack = []
        for p in range(heads_per_group // 2):
            pair = g * (heads_per_group // 2) + p
            qp = q_ref[:, pair * LANES:(pair + 1) * LANES].astype(bf16)
            zero = jnp.zeros_like(qp)
            stack.append(jnp.where(low, qp, zero))
            stack.append(jnp.where(low, zero, qp))
        lhs.append(jnp.concatenate(stack, axis=0))
    s_new = [_dot_nt(lhs[g], kn_dup[g]) for g in range(N_KV_HEADS)]
    s_old = []
    for b in range(nseq):
        per = []
        for g in range(N_KV_HEADS):
            qb = jnp.concatenate([lhs[g][hh * rows + b * seq:hh * rows + (b + 1) * seq]
                                  for hh in range(heads_per_group)], axis=0)
            per.append(_dot(qb, kt_dup[b][g]))
        s_old.append(per)

    p_old, p_new, inv = [], [], []
    for head in range(N_Q_HEADS):
        g, hh = divmod(head, heads_per_group)
        so = jnp.concatenate([s_old[b][g][hh * seq:(hh + 1) * seq] for b in range(nseq)], axis=0)
        so = jnp.where(mask_old, so, -1e30)
        sn = jnp.where(mask_new, s_new[g][hh * rows:(hh + 1) * rows], -1e30)
        sink = sink_ref[head] * LOG2E
        mx = jnp.maximum(jnp.maximum(jnp.max(so, axis=-1, keepdims=True),
                                     jnp.max(sn, axis=-1, keepdims=True)), sink)
        po = jnp.exp2(so - mx)
        pn = jnp.exp2(sn - mx)
        inv.append(1.0 / (jnp.sum(po, axis=-1, keepdims=True) + jnp.sum(pn, axis=-1, keepdims=True)
                          + jnp.exp2(sink - mx)))
        p_old.append(po.astype(bf16))
        p_new.append(pn.astype(bf16))

    o_new = [_dot(jnp.concatenate([p_new[g * heads_per_group + hh] for hh in range(heads_per_group)], axis=0),
                  vn_dup[g]) for g in range(N_KV_HEADS)]
    o_old = []
    for b in range(nseq):
        per = []
        for g in range(N_KV_HEADS):
            pb = jnp.concatenate([p_old[g * heads_per_group + hh][b * seq:(b + 1) * seq]
                                  for hh in range(heads_per_group)], axis=0)
            per.append(_dot_nt(pb, vt_dup[b][g]))
        o_old.append(per)

    for pair in range(N_Q_HEADS // 2):
        halves = []
        for head in (2 * pair, 2 * pair + 1):
            g, hh = divmod(head, heads_per_group)
            old = jnp.concatenate([o_old[b][g][hh * seq:(hh + 1) * seq] for b in range(nseq)], axis=0)
            halves.append((old + o_new[g][hh * rows:(hh + 1) * rows]) * inv[head])
        o_ref[:, pair * LANES:(pair + 1) * LANES] = jnp.where(low, halves[0], halves[1])


def _swa_sample(sq, sk, sv, past_kt, past_vt, sinks, bsz, t):
    nseq = LANES // t
    tok = lambda width: pl.BlockSpec((nseq * t, width), lambda i, s: (i, 0))
    cache = pl.BlockSpec((nseq,) + past_kt.shape[1:], lambda i, s: (i, 0, 0, 0))
    return pl.pallas_call(
        functools.partial(_swa_sample_kernel, seq=t),
        grid_spec=pltpu.PrefetchScalarGridSpec(
            num_scalar_prefetch=1,
            grid=(bsz // nseq,),
            in_specs=[tok(SWA_Q), tok(SWA_KV), tok(SWA_KV), cache, cache],
            out_specs=[tok(SWA_Q), cache, cache]),
        out_shape=[jax.ShapeDtypeStruct((bsz * t, SWA_Q), f32),
                   jax.ShapeDtypeStruct(past_kt.shape, f32),
                   jax.ShapeDtypeStruct(past_vt.shape, f32)],
        compiler_params=pltpu.CompilerParams(
            dimension_semantics=("parallel",), vmem_limit_bytes=VMEM_LIMIT),
        name="swa_sample",
    )(sinks, sq, sk, sv, past_kt, past_vt)


FFN_CHUNK = 512


def _out_proj_norm(streams, n2, wout_ref):
    mix = [s[1] for s in streams]
    mix = mix[0] if len(mix) == 1 else jnp.concatenate(mix, axis=0)
    mixed = _dot(mix[:, :GLA_WIDTH], wout_ref[:GLA_WIDTH, :]) + _dot(mix[:, GLA_WIDTH:], wout_ref[GLA_WIDTH:, :])
    hs, hbs, r0 = [], [], 0
    for x, _, g1, sh2, sc2 in streams:
        nb, t, d = x.shape
        h = x + g1 * mixed[r0:r0 + nb * t].reshape(nb, t, d)
        ms = jnp.mean(h * h, axis=-1, keepdims=True)
        hn = h * lax.rsqrt(ms + EPS) * n2
        hn = hn * (1.0 + sc2) + sh2
        hs.append(h)
        hbs.append(hn.reshape(nb * t, d).astype(bf16))
        r0 += nb * t
    return hs, (hbs[0] if len(hbs) == 1 else jnp.concatenate(hbs, axis=0))


def _ffn_piece(hb, w1_ref, w2_ref, c, width):
    cols = slice(c * width, (c + 1) * width)
    a = jnp.maximum(_dot(hb, w1_ref[:, cols]), 0.0)
    return _dot((a * a).astype(bf16), w2_ref[cols, :])


def _resident(shape):
    return pl.BlockSpec(shape, lambda *_: (0,) * len(shape), pipeline_mode=pl.Buffered(1))


def _prompt_kernel(sink_ref,
                   q_ref, k_ref, v_ref, la_ref, r_ref, sq_ref, k2_ref, v2_ref, k2p_ref, v2p_ref,
                   tri_ref, masks_ref, gnw_ref,
                   x_ref, g1_ref, sh2_ref, sc2_ref, g2_ref,
                   xs_ref, gos_ref, sos_ref, g1s_ref, sh2s_ref, sc2s_ref, g2s_ref,
                   n2_ref, wout_ref, w1_ref, w2_ref,
                   y_ref, ys_ref, sout_ref, state_ref, mix_ref, *, levels, nt, nsteps):
    s = pl.program_id(0)
    slot = s % 2

    @pl.when(s == 0)
    def _():
        mix_ref[1] = jnp.zeros(mix_ref.shape[1:], mix_ref.dtype)
        state_ref[...] = jnp.zeros_like(state_ref)

    j = jnp.minimum(s, nsteps - 1) % nt
    st_in = state_ref[...]
    st = jnp.where(j == 0, 0.0, st_in)
    gnw = gnw_ref[...]
    out = mix_ref.at[slot]
    mix_s = jnp.concatenate([gos_ref[...], sos_ref[...]], axis=1).astype(bf16)
    (h, h_s), hb = _out_proj_norm(
        [(x_ref[...], mix_ref[1 - slot], g1_ref[...], sh2_ref[...], sc2_ref[...]),
         (xs_ref[...], mix_s, g1s_ref[...], sh2s_ref[...], sc2s_ref[...])], n2_ref[...], wout_ref)
    rows_p = h.shape[0] * h.shape[1]
    todo = list(range(D_FF // FFN_CHUNK))
    ff = []

    def ffn_piece():
        if todo:
            part = _ffn_piece(hb, w1_ref, w2_ref, todo.pop(0), FFN_CHUNK)
            ff[:] = [part if not ff else ff[0] + part]

    st = _gla_prompt_block(st, q_ref, k_ref, v_ref, la_ref, r_ref, tri_ref, masks_ref, gnw,
                           out.at[:, pl.ds(0, GLA_WIDTH)], levels, ffn_piece)
    _swa_prompt_blocks(sq_ref, k2p_ref[...], k2_ref, v2p_ref[...], v2_ref, j == 0, sink_ref,
                       out.at[:, pl.ds(GLA_WIDTH, SWA_Q)], ffn_piece)
    while todo:
        ffn_piece()
    st = jnp.where(s < nsteps, st, st_in)
    state_ref[...] = st
    sout_ref[0] = st.T
    y_ref[...] = h + g2_ref[...] * ff[0][:rows_p].reshape(h.shape)
    ys_ref[...] = h_s + g2s_ref[...] * ff[0][rows_p:].reshape(h_s.shape)


def _prompt_mix_ffn(x, gq, gk, gv, la, gr, sq, k2, v2, gnw, sinks, mod, row0, n2, wout, w1, w2,
                    xs, gos, sos, row0_s):
    bsz, t, d = x.shape
    tb = FUSED_TOKEN_BLOCK
    nt = t // tb
    nsteps = bsz * nt
    levels, tri, masks = _gla_constants(PROMPT_GLA_CHUNK, PROMPT_GLA_CHUNK)
    per_blk = tb // WINDOW
    bs, ts, _ = xs.shape
    nbs = bs // nsteps
    assert nbs * nsteps == bs and (nbs * ts) % 16 == 0

    def mix_blk(s):
        return jnp.minimum(s, nsteps - 1)

    def ffn_blk(s):
        return jnp.maximum(s - 1, 0)

    def tok(width):
        return pl.BlockSpec((tb, width), lambda s, _: (mix_blk(s), 0))

    prev = pl.BlockSpec((WINDOW, SWA_KV2), lambda s, _: (jnp.maximum(mix_blk(s) * per_blk - 1, 0), 0))
    xspec = pl.BlockSpec((1, tb, d), lambda s, _: (ffn_blk(s) // nt, ffn_blk(s) % nt, 0))
    ffn_chunks = (MOD_GATE1, MOD_SHIFT2, MOD_SCALE2, MOD_GATE2)
    mods = [_mod_spec(1, row0, chunk, lambda s, _: ffn_blk(s) // nt) for chunk in ffn_chunks]
    mods_s = [_mod_spec(nbs, row0_s, chunk, lambda s, _: ffn_blk(s)) for chunk in ffn_chunks]
    xs_spec = pl.BlockSpec((nbs, ts, d), lambda s, _: (ffn_blk(s), 0, 0))
    mix_s_spec = pl.BlockSpec((nbs * ts, GLA_WIDTH), lambda s, _: (ffn_blk(s), 0))
    return pl.pallas_call(
        functools.partial(_prompt_kernel, levels=levels, nt=nt, nsteps=nsteps),
        grid_spec=pltpu.PrefetchScalarGridSpec(
            num_scalar_prefetch=1,
            grid=(nsteps + 1,),
            in_specs=[tok(GLA_QK), tok(GLA_QK), tok(GLA_WIDTH), tok(GLA_QK), tok(GLA_WIDTH),
                      tok(SWA_Q), tok(SWA_KV2), tok(SWA_KV2), prev, prev,
                      _resident(tri.shape), _resident(masks.shape), _resident((1, GLA_DV)),
                      xspec, *mods, xs_spec, mix_s_spec, mix_s_spec, *mods_s, _resident((1, 1, d)),
                      _resident((d, d)), _resident((d, D_FF)), _resident((D_FF, d))],
            out_specs=[xspec, xs_spec,
                       pl.BlockSpec((1, GLA_QK, GLA_DV), lambda s, _: (mix_blk(s) // nt, 0, 0))],
            scratch_shapes=[pltpu.VMEM((GLA_DV, GLA_QK), f32),
                            pltpu.VMEM((2, tb, GLA_WIDTH + SWA_Q), bf16)]),
        out_shape=[jax.ShapeDtypeStruct((bsz, t, d), f32),
                   jax.ShapeDtypeStruct(xs.shape, f32),
                   jax.ShapeDtypeStruct((bsz, GLA_QK, GLA_DV), f32)],
        compiler_params=pltpu.CompilerParams(
            dimension_semantics=("arbitrary",), vmem_limit_bytes=VMEM_LIMIT),
        name="prompt_mix_ffn",
    )(sinks, gq, gk, gv, la, gr, sq, k2, v2, k2, v2, tri, masks, gnw,
      x, mod, mod, mod, mod, xs, gos, sos, mod, mod, mod, mod, n2, wout, w1, w2)


def _rope_tables(pos, copies):
    half = HEAD_DIM // 2
    inv = np.power(np.float32(ROPE_THETA), -np.arange(half, dtype=np.float32) * np.float32(2.0 / HEAD_DIM))
    ang = np.asarray(pos, np.float32)[:, None] * inv[None, :].astype(np.float32)
    cos = np.cos(ang).astype(np.float32)
    sin = np.sin(ang).astype(np.float32)
    reps = (copies, LANES // HEAD_DIM)
    return (jnp.asarray(np.tile(np.concatenate([cos, cos], axis=-1), reps)),
            jnp.asarray(np.tile(np.concatenate([-sin, sin], axis=-1), reps)))


def _block_diag_ones(n, blk):
    idx = np.arange(n) // blk
    return jnp.asarray((idx[:, None] == idx[None, :]).astype(np.float32), bf16)


def _layer_weights(w_in, w_gate_up, b_gate, q_norm_w, k_norm_w, w_out, w_ff1, w_ff2):
    splits = np.cumsum([GLA_QK, GLA_QK, GLA_WIDTH, GLA_WIDTH, GLA_GATE_RANK, SWA_Q, SWA_KV])
    gq, gk, gv, gr, glr, sq, sk, sv = jnp.split(w_in, [int(s) for s in splits], axis=1)
    glr = jnp.pad(glr, ((0, 0), (0, LANES - GLA_GATE_RANK)))
    win = jnp.concatenate([gq, gk, gv, gr, sq, sk, sv, glr], axis=1).astype(bf16)
    wgu = jnp.pad(w_gate_up, ((0, LANES - GLA_GATE_RANK), (0, 0))).astype(bf16)
    return dict(
        win=win, wgu=wgu, bg=b_gate.reshape(1, GLA_QK),
        qnw=jnp.tile(q_norm_w, N_Q_HEADS).reshape(1, SWA_Q),
        knw=jnp.tile(k_norm_w, N_KV_HEADS).reshape(1, SWA_KV),
        ffn_f32=(w_out, w_ff1, w_ff2))


def _project(x, mod, row0, pos, lw, n1, bdq, bdk, prompt):
    bsz, t, d = x.shape
    if prompt:
        nb, tb, tail, act, groups, cast = 1, TOKEN_BLOCK, WINDOW, bf16, PROMPT_INPROJ_GROUPS, lw["ffn_f32"]
    else:
        nb, tb, tail, act, groups, cast = SAMPLE_ROWS // t, t, SAMPLE_ROWS, f32, 1, ()
    cos_t, sin_t = _rope_tables(pos, nb)
    return _inproj(x, mod, row0, n1.reshape(1, 1, d), lw["win"], lw["wgu"], lw["bg"], lw["qnw"], lw["knw"],
                   bdq, bdk, cos_t, sin_t, nb, tb, tail, act, groups, cast, tail_t=prompt)


def _decoder_layers(xp, xs, mod, pos_p, pos_s, state, past_k, past_v, lw, n1, n2, gnw, sinks, bdq, bdk):
    bp, tp, d = xp.shape
    bs, ts, _ = xs.shape
    heads = (N_KV_HEADS, HEAD_DIM)
    gq, gk, gv, gr, la, sq, _, _, kk, vk = _project(xs, mod, 0, pos_s, lw, n1, bdq, bdk, False)
    go_s, state_s = _gla_sample(gq, gk, gv, la, gr, state.reshape(bs, GLA_QK, GLA_DV), gnw, bs, ts)
    so_s, kt, vt = _swa_sample(sq, kk, vk, past_k.transpose(0, 2, 3, 1), past_v.transpose(0, 2, 3, 1),
                               sinks, bs, ts)
    gq, gk, gv, gr, la, sq, k2, v2, kk, vk, *ffn_w = _project(xp, mod, bs, pos_p, lw, n1, bdq, bdk, True)
    yp, ys, s_t = _prompt_mix_ffn(xp, gq, gk, gv, la, gr, sq, k2, v2, gnw, sinks, mod, bs,
                                  n2.reshape(1, 1, d), *ffn_w, xs, go_s, so_s, 0)
    state_p = s_t.reshape(bp, GLA_HEADS, GLA_DK, GLA_DV)
    return (yp, ys, state_p,
            kk.reshape(bp, *heads, WINDOW).transpose(0, 3, 1, 2), vk.reshape(bp, *heads, WINDOW).transpose(0, 3, 1, 2),
            state_s.reshape(bs, GLA_HEADS, GLA_DK, GLA_DV),
            kt.transpose(0, 3, 1, 2), vt.transpose(0, 3, 1, 2))


def kernel(x_prompt, x_sample, state_gla, cache_swa_k, cache_swa_v, c_prompt, c_sample, w_ada, b_ada, norm1_w, norm2_w, w_in, w_gate_up, b_gate, gla_norm_w, q_norm_w, k_norm_w, sinks, w_out, w_ff1, w_ff2):
    depth = w_ada.shape[0]
    bp, tp, _ = x_prompt.shape
    bs, ts, _ = x_sample.shape
    pos_p = np.arange(tp)
    pos_s = PAST_LEN + np.arange(ts)
    bdq = _block_diag_ones(SWA_Q, HEAD_DIM)
    bdk = _block_diag_ones(SWA_KV, HEAD_DIM)
    yp, ys = x_prompt, x_sample
    outs = [[] for _ in range(6)]
    for l in range(depth):
        mod = _modulation(c_sample, c_prompt, w_ada[l], b_ada[l])
        lw = _layer_weights(w_in[l], w_gate_up[l], b_gate[l], q_norm_w[l], k_norm_w[l],
                            w_out[l], w_ff1[l], w_ff2[l])
        gnw = gla_norm_w[l].reshape(1, GLA_DV)
        yp, ys, *new = _decoder_layers(yp, ys, mod, pos_p, pos_s, state_gla[l], cache_swa_k[l],
                                       cache_swa_v[l], lw, norm1_w[l], norm2_w[l], gnw, sinks[l], bdq, bdk)
        for lst, val in zip(outs, new):
            lst.append(val)
    return (yp, ys) + tuple(jnp.stack(o) for o in outs)
```

```python
import functools

import jax
import jax.numpy as jnp
import numpy as np
from jax import lax
from jax.experimental import pallas as pl
from jax.experimental.pallas import tpu as pltpu

f32 = jnp.float32
bf16 = jnp.bfloat16

D_MODEL = 1024
GLA_HEADS = 4
GLA_DK = 64
GLA_DV = 128
GLA_QK = GLA_HEADS * GLA_DK
GLA_WIDTH = GLA_HEADS * GLA_DV
GLA_GATE_RANK = 16
GLA_TAU = 16.0
LOG2E = 1.4426950408889634
HEAD_DIM = 64
N_Q_HEADS = 8
N_KV_HEADS = 2
SWA_Q = N_Q_HEADS * HEAD_DIM
SWA_KV = N_KV_HEADS * HEAD_DIM
SWA_KV2 = 2 * SWA_KV
WINDOW = 128
ROPE_THETA = 10000.0
PAST_LEN = 8192
D_FF = 4 * D_MODEL
EPS = 1e-6
LANES = 128
GLA_CHUNK = 256
PROMPT_GLA_CHUNK = 128
TOKEN_BLOCK = 1024
FUSED_TOKEN_BLOCK = 512
SAMPLE_ROWS = 512
PROMPT_INPROJ_GROUPS = 2
VMEM_LIMIT = 56 * 1024 * 1024

_SEG = {}
_off = 0
for _name, _w in (("gq", GLA_QK), ("gk", GLA_QK), ("gv", GLA_WIDTH), ("gr", GLA_WIDTH),
                  ("sq", SWA_Q), ("sk", SWA_KV), ("sv", SWA_KV), ("glr", LANES)):
    _SEG[_name] = (_off, _off + _w)
    _off += _w
IN_WIDTH_PADDED = _off


def _dot(a, b):
    return jnp.dot(a, b, preferred_element_type=f32)


def _dot_nt(a, b):
    return lax.dot_general(a, b, (((1,), (1,)), ((), ())), preferred_element_type=f32)


def _dot_tn(a, b):
    return lax.dot_general(a, b, (((0,), (0,)), ((), ())), preferred_element_type=f32)


def _sigmoid(x):
    return 1.0 / (1.0 + jnp.exp(-x))


MOD_W_SLABS = 4


def _mod_kernel(ca_ref, cb_ref, *refs):
    w_refs, b_ref, o_ref = refs[:-2], refs[-2], refs[-1]
    c = jnp.concatenate([ca_ref[...], cb_ref[...]], axis=0)
    s = (c * _sigmoid(c)).astype(bf16)
    kb = w_refs[0].shape[0]
    res = b_ref[...]
    for i, w_ref in enumerate(w_refs):
        res = res + _dot(s[:, i * kb:(i + 1) * kb], w_ref[...].astype(bf16))
    for r in range(res.shape[0]):
        o_ref[r] = res[r:r + 1, :]


def _modulation(c_a, c_b, w_ada, b_ada):
    m = c_a.shape[0] + c_b.shape[0]
    n = w_ada.shape[1]
    bn = 1536
    kb = D_MODEL // MOD_W_SLABS
    slabs = [pl.BlockSpec((kb, bn), lambda j, i=i: (i, j)) for i in range(MOD_W_SLABS)]
    return pl.pallas_call(
        _mod_kernel,
        grid=(n // bn,),
        in_specs=[pl.BlockSpec(c_a.shape, lambda j: (0, 0)), pl.BlockSpec(c_b.shape, lambda j: (0, 0))]
        + slabs + [pl.BlockSpec((1, bn), lambda j: (0, j))],
        out_specs=pl.BlockSpec((m, 1, bn), lambda j: (0, 0, j)),
        out_shape=jax.ShapeDtypeStruct((m, 1, n), f32),
        compiler_params=pltpu.CompilerParams(vmem_limit_bytes=VMEM_LIMIT),
        name="adaln_mod",
    )(c_a, c_b, *([w_ada] * MOD_W_SLABS), b_ada.reshape(1, n))


def _group_rms(x, bd_ref, w_ref):
    ssq = _dot((x * x).astype(bf16), bd_ref[...])
    return x * lax.rsqrt(ssq * (1.0 / HEAD_DIM) + EPS) * w_ref[...]


def _rope(x, cos, sin_signed, low_half):
    partner = jnp.where(low_half, pltpu.roll(x, LANES - 32, axis=1), pltpu.roll(x, 32, axis=1))
    return x * cos + partner * sin_signed


def _dup_heads(x, low_lanes):
    rolled = pltpu.roll(x, HEAD_DIM, axis=1)
    return jnp.where(low_lanes, x, rolled), jnp.where(low_lanes, rolled, x)


def _inproj_kernel(x_ref, sh_ref, sc_ref, n1_ref, win_ref, wgu_ref, bg_ref, qnw_ref, knw_ref,
                   bdq_ref, bdk_ref, cos_ref, sin_ref, *rest, groups, ncast, tail, tail_t):
    slabs, rest = rest[:ncast], rest[ncast:]
    gq_ref, gk_ref, gv_ref, gr_ref, la_ref, sq_ref, k2_ref, v2_ref, kk_ref, vk_ref = rest[:10]
    for src, dst in zip(slabs, rest[10:]):
        dst[...] = src[...].astype(dst.dtype)
    nb, t, d = x_ref.shape
    m = nb * t
    mg = m // groups
    lane = lax.broadcasted_iota(jnp.int32, (mg, LANES), 1)
    low_half = (lane & 32) == 0
    low_lanes = lane < HEAD_DIM
    for grp in range(groups):
        rows = slice(grp * mg, (grp + 1) * mg)
        if nb == 1:
            x = x_ref[:, rows, :]
            sc, sh = sc_ref[...], sh_ref[...]
        else:
            seqs = slice(grp * (nb // groups), (grp + 1) * (nb // groups))
            x = x_ref[seqs]
            sc, sh = sc_ref[seqs], sh_ref[seqs]
        ms = jnp.mean(x * x, axis=-1, keepdims=True)
        hn = x * lax.rsqrt(ms + EPS) * n1_ref[...]
        hn = hn * (1.0 + sc) + sh
        hb = hn.reshape(mg, d).astype(bf16)

        def seg(name):
            a, b = _SEG[name]
            return _dot(hb, win_ref[:, a:b])

        cos = cos_ref[rows, :]
        sin = sin_ref[rows, :]
        sq = _group_rms(seg("sq"), bdq_ref, qnw_ref)
        for c in range(SWA_Q // LANES):
            blk = _rope(sq[:, c * LANES:(c + 1) * LANES], cos, sin, low_half)
            sq_ref[rows, c * LANES:(c + 1) * LANES] = (blk * (LOG2E * HEAD_DIM ** -0.5)).astype(sq_ref.dtype)
        sk = _rope(_group_rms(seg("sk"), bdk_ref, knw_ref), cos, sin, low_half)
        sv = seg("sv")
        keep = (grp + 1) * mg - (m - tail)
        if keep > 0:
            keep = min(keep, mg)
            dst_rows = slice((grp + 1) * mg - keep - (m - tail), (grp + 1) * mg - (m - tail))
            if tail_t:
                assert keep == tail
                kk_ref[...] = sk[mg - keep:].T
                vk_ref[...] = sv[mg - keep:].T
            else:
                kk_ref[dst_rows, :] = sk[mg - keep:]
                vk_ref[dst_rows, :] = sv[mg - keep:]
        for src, dst in ((sk, k2_ref), (sv, v2_ref)):
            d0, d1 = _dup_heads(src, low_lanes)
            dst[rows, :LANES] = d0.astype(dst.dtype)
            dst[rows, LANES:] = d1.astype(dst.dtype)
        glr = seg("glr").astype(bf16)
        g = _dot(glr, wgu_ref[...]) + bg_ref[...]
        log_sig = jnp.minimum(g, 0.0) - jnp.log1p(jnp.exp(-jnp.abs(g)))
        la_ref[rows, :] = log_sig * (LOG2E / GLA_TAU)
        gq_ref[rows, :] = seg("gq") * (GLA_DK ** -0.5)
        gk_ref[rows, :] = seg("gk")
        gv_ref[rows, :] = seg("gv").astype(gv_ref.dtype)
        gr_ref[rows, :] = seg("gr").astype(gr_ref.dtype)


MOD_SHIFT1, MOD_SCALE1, MOD_GATE1, MOD_SHIFT2, MOD_SCALE2, MOD_GATE2 = range(6)


def _mod_spec(nb, row0, chunk, batch_block):
    return pl.BlockSpec((nb, 1, D_MODEL), lambda *ids: (row0 // nb + batch_block(*ids), 0, chunk))


def _inproj(x, mod, row0, n1, win, wgu, bg, qnw, knw, bdq, bdk, cos_t, sin_t, nb, tb, tail, act, groups,
            to_bf16=(), tail_t=False):
    bsz, t, d = x.shape
    m = nb * tb
    nt = t // tb
    grid = (bsz // nb, nt)
    tok = bsz * t
    nsteps = grid[0] * grid[1]
    slab_specs = [pl.BlockSpec((w.shape[0] // nsteps, w.shape[1]), lambda i, j: (i * nt + j, 0))
                  for w in to_bf16]

    def full(shape):
        return pl.BlockSpec(shape, lambda i, j: (0,) * len(shape))

    def out(width):
        return pl.BlockSpec((m, width), lambda i, j: (i * nt + j, 0))

    tail_blk = (SWA_KV, tail) if tail_t else (tail, SWA_KV)
    tail_spec = pl.BlockSpec(tail_blk, lambda i, j: (i, 0))
    tail_shape = jax.ShapeDtypeStruct(((bsz // nb) * tail_blk[0], tail_blk[1]), f32)
    outs = ((GLA_QK, f32), (GLA_QK, f32), (GLA_WIDTH, act), (GLA_WIDTH, act), (GLA_QK, f32),
            (SWA_Q, act), (SWA_KV2, bf16), (SWA_KV2, bf16))
    return pl.pallas_call(
        functools.partial(_inproj_kernel, groups=groups, ncast=len(to_bf16), tail=tail, tail_t=tail_t),
        grid=grid,
        in_specs=[pl.BlockSpec((nb, tb, d), lambda i, j: (i, j, 0)),
                  _mod_spec(nb, row0, MOD_SHIFT1, lambda i, j: i),
                  _mod_spec(nb, row0, MOD_SCALE1, lambda i, j: i),
                  full((1, 1, d)),
                  full((d, IN_WIDTH_PADDED)),
                  full((LANES, GLA_QK)),
                  full((1, GLA_QK)),
                  full((1, SWA_Q)),
                  full((1, SWA_KV)),
                  full((SWA_Q, SWA_Q)),
                  full((SWA_KV, SWA_KV)),
                  pl.BlockSpec((m, LANES), lambda i, j: (j, 0)),
                  pl.BlockSpec((m, LANES), lambda i, j: (j, 0))] + slab_specs,
        out_specs=[out(w) for w, _ in outs] + [tail_spec, tail_spec] + slab_specs,
        out_shape=[jax.ShapeDtypeStruct((tok, w), dt) for w, dt in outs]
        + [tail_shape] * 2
        + [jax.ShapeDtypeStruct(w.shape, bf16) for w in to_bf16],
        compiler_params=pltpu.CompilerParams(
            dimension_semantics=("parallel", "arbitrary"), vmem_limit_bytes=VMEM_LIMIT),
        name="inproj",
    )(x, mod, mod, n1, win, wgu, bg, qnw, knw, bdq, bdk, cos_t, sin_t, *to_bf16)


def _gla_constants(chunk, seq):
    t = np.arange(chunk)
    levels = []
    m = 1
    while m < seq:
        levels.append(m)
        m *= 2
    masks = [np.eye(chunk, dtype=bool)]
    for m in levels:
        upper = (t % (2 * m) >= m)[:, None]
        lower = (t % (2 * m) < m)[None, :]
        same = (t[:, None] // (2 * m)) == (t[None, :] // (2 * m))
        masks.append(same & upper & lower)
    tri = t[None, :] <= t[:, None]
    tiled = np.tile(np.stack(masks).astype(np.float32), (1, 1, GLA_HEADS))
    return tuple(levels), jnp.asarray(tri.astype(np.float32), bf16), jnp.asarray(tiled)


def _stack_heads(xb, lane_head, axis):
    zero = jnp.zeros_like(xb)
    return jnp.concatenate([jnp.where(lane_head == h, xb, zero) for h in range(GLA_HEADS)], axis=axis)


def _block_sums(la, bcum, m, row, rolls):
    c, n = la.shape
    if m == 1:
        return la, None
    if m < 8:
        def rolled(shift):
            if shift not in rolls:
                rolls[shift] = pltpu.roll(la, shift % c, axis=0)
            return rolls[shift]
        pos = row & (m - 1)
        pre = la
        suf = None
        for j in range(1, m):
            pre = pre + jnp.where(pos >= j, rolled(j), 0.0)
            term = jnp.where(pos < m - j, rolled(-j), 0.0)
            suf = term if suf is None else suf + term
        return pre, suf
    before, last = [], []
    for i in range(c // m):
        before.append(jnp.zeros((m, n), f32) if i == 0
                      else jnp.broadcast_to(bcum[i * m - 1:i * m], (m, n)))
        last.append(jnp.broadcast_to(bcum[(i + 1) * m - 1:(i + 1) * m], (m, n)))
    if len(before) == 1:
        return bcum - before[0], last[0] - bcum
    return bcum - jnp.concatenate(before, axis=0), jnp.concatenate(last, axis=0) - bcum


def _gla_scores(q_ref, k_ref, v_ref, la_ref, tri_ref, masks_ref, levels, seq):
    c = q_ref.shape[0]
    la = la_ref[...]
    hi = la.astype(bf16)
    lo = (la - hi.astype(f32)).astype(bf16)
    tri = tri_ref[...]
    bcum = _dot(tri, hi) + _dot(tri, lo)
    q = q_ref[...]
    k = k_ref[...]
    vb = v_ref[...].astype(bf16)
    row = lax.broadcasted_iota(jnp.int32, (c, GLA_QK), 0)
    row_head_t = lax.broadcasted_iota(jnp.int32, (GLA_QK, c), 0) >> 6
    rolls = {}
    assert levels[0] == 1
    q01 = jnp.concatenate([q.astype(bf16), (q * jnp.exp2(la)).astype(bf16)], axis=0)
    r = _dot(q01, _stack_heads(k.T.astype(bf16), row_head_t, 1))
    attn = r[:c] * masks_ref[0] + r[c:] * masks_ref[1]
    for lvl, m in enumerate(levels[1:], start=2):
        pre, suf = _block_sums(la, bcum, m, row, rolls)
        kt_t = (k * jnp.exp2(suf)).T.astype(bf16)
        r = _dot((q * jnp.exp2(pre)).astype(bf16), _stack_heads(kt_t, row_head_t, 1))
        attn = attn + r * masks_ref[lvl]
    a = attn.astype(bf16)
    pre, suf = _block_sums(la, bcum, seq, row, rolls)
    qf = (q * jnp.exp2(pre)).astype(bf16)
    kf = (k * jnp.exp2(suf)).astype(bf16)
    return a, qf, kf, vb, bcum, hi, lo


def _gla_values(a, vb):
    v_head = lax.broadcasted_iota(jnp.int32, vb.shape, 1) >> 7
    return _dot(a, _stack_heads(vb, v_head, 0))


def _gla_finish(o, r_ref, gnw, go_ref):
    for h in range(GLA_HEADS):
        sl = slice(h * GLA_DV, (h + 1) * GLA_DV)
        oh = o[:, sl]
        r = r_ref[:, sl].astype(f32)
        ms = jnp.mean(oh * oh, axis=-1, keepdims=True)
        go_ref[:, sl] = (oh * lax.rsqrt(ms + EPS) * gnw * (r * _sigmoid(r))).astype(go_ref.dtype)


def _gla_prompt_block(st, q_ref, k_ref, v_ref, la_ref, r_ref, tri_ref, masks_ref, gnw, go_ref, levels,
                      between=lambda: None):
    c = PROMPT_GLA_CHUNK
    nchunk = q_ref.shape[0] // c
    lane_head = lax.broadcasted_iota(jnp.int32, (GLA_DV, GLA_QK), 1) >> 6
    row_head = lax.broadcasted_iota(jnp.int32, (GLA_QK, GLA_DV), 0) >> 6
    chunks = []
    for ci in range(nchunk):
        rows = pl.ds(ci * c, c)
        chunks.append(_gla_scores(q_ref.at[rows], k_ref.at[rows], v_ref.at[rows], la_ref.at[rows],
                                  tri_ref, masks_ref, levels, c))
        between()
    intra, upds, decays = [], [], []
    for a, _, kf, vb, bcum, _, _ in chunks:
        intra.append(_gla_values(a, vb))
        full = _dot_tn(vb, kf)
        upd = None
        for h in range(GLA_HEADS):
            term = jnp.where(lane_head == h, full[h * GLA_DV:(h + 1) * GLA_DV], 0.0)
            upd = term if upd is None else upd + term
        upds.append(upd)
        decays.append(jnp.exp2(bcum[c - 1:c]))
    between()
    for ci in range(nchunk):
        rows = pl.ds(ci * c, c)
        sbd = _stack_heads(st.T.astype(bf16), row_head, 1)
        o = intra[ci] + _dot(chunks[ci][1], sbd)
        _gla_finish(o, r_ref.at[rows], gnw, go_ref.at[rows])
        st = decays[ci] * st + upds[ci]
    return st


def _gla_sample_kernel(q_ref, k_ref, v_ref, la_ref, r_ref, s0_ref, tri_ref, masks_ref, gnw_ref,
                       go_ref, sout_ref, *, levels, seq):
    c = q_ref.shape[0]
    a, qf, kf, vb, _, hi, lo = _gla_scores(q_ref, k_ref, v_ref, la_ref, tri_ref, masks_ref, levels, seq)
    o_intra = _gla_values(a, vb)
    row_head = lax.broadcasted_iota(jnp.int32, (GLA_QK, GLA_DV), 0) >> 6
    ones = jnp.ones((seq, GLA_DV), bf16)
    inter = []
    for b in range(c // seq):
        rows = slice(b * seq, (b + 1) * seq)
        s_old = s0_ref[b]
        sbd = _stack_heads(s_old.astype(bf16), row_head, 1)
        inter.append(_dot(qf[rows], sbd))
        p = _dot_tn(kf[rows], vb[rows])
        upd = None
        for h in range(GLA_HEADS):
            term = jnp.where(row_head == h, p[:, h * GLA_DV:(h + 1) * GLA_DV], 0.0)
            upd = term if upd is None else upd + term
        total = _dot_tn(hi[rows], ones) + _dot_tn(lo[rows], ones)
        sout_ref[b] = jnp.exp2(total) * s_old + upd
    o = o_intra + jnp.concatenate(inter, axis=0)
    _gla_finish(o, r_ref, gnw_ref[...], go_ref)


def _gla_sample(gq, gk, gv, la, gr, state, gnw, bsz, t):
    c = GLA_CHUNK
    nseq = c // t
    levels, tri, masks = _gla_constants(c, t)

    def tok(width):
        return pl.BlockSpec((c, width), lambda i: (i, 0))

    return pl.pallas_call(
        functools.partial(_gla_sample_kernel, levels=levels, seq=t),
        grid=(bsz // nseq,),
        in_specs=[tok(GLA_QK), tok(GLA_QK), tok(GLA_WIDTH), tok(GLA_QK), tok(GLA_WIDTH),
                  pl.BlockSpec((nseq, GLA_QK, GLA_DV), lambda i: (i, 0, 0)),
                  pl.BlockSpec(tri.shape, lambda i: (0, 0)),
                  pl.BlockSpec(masks.shape, lambda i: (0, 0, 0)),
                  pl.BlockSpec((1, GLA_DV), lambda i: (0, 0))],
        out_specs=[tok(GLA_WIDTH),
                   pl.BlockSpec((nseq, GLA_QK, GLA_DV), lambda i: (i, 0, 0))],
        out_shape=[jax.ShapeDtypeStruct((bsz * t, GLA_WIDTH), f32),
                   jax.ShapeDtypeStruct((bsz, GLA_QK, GLA_DV), f32)],
        compiler_params=pltpu.CompilerParams(
            dimension_semantics=("parallel",), vmem_limit_bytes=VMEM_LIMIT),
        name="gla_sample",
    )(gq, gk, gv, la, gr, state, tri, masks, gnw)


def _sink_attention(q_ref, rows, kdups, vdups, mask, sink_ref, o_ref, between=lambda: None):
    n = len(kdups)
    heads_per_group = N_Q_HEADS // N_KV_HEADS
    low_q = lax.broadcasted_iota(jnp.int32, (rows, LANES), 1) < HEAD_DIM
    scores = []
    for i in range(n):
        per_group = []
        for g in range(N_KV_HEADS):
            stack = []
            for p in range(heads_per_group // 2):
                pair = g * (heads_per_group // 2) + p
                qp = q_ref[i * rows:(i + 1) * rows, pair * LANES:(pair + 1) * LANES].astype(bf16)
                zero = jnp.zeros_like(qp)
                stack.append(jnp.where(low_q, qp, zero))
                stack.append(jnp.where(low_q, zero, qp))
            per_group.append(_dot_nt(jnp.concatenate(stack, axis=0), kdups[i][g]))
        scores.append(per_group)
    between()
    probs, inv = [], []
    for head in range(N_Q_HEADS):
        g, hh = divmod(head, heads_per_group)
        parts = [scores[i][g][hh * rows:(hh + 1) * rows] for i in range(n)]
        s = parts[0] if n == 1 else jnp.concatenate(parts, axis=0)
        sink = sink_ref[head] * LOG2E
        s = jnp.where(mask, s, -1e30)
        mx = jnp.maximum(jnp.max(s, axis=-1, keepdims=True), sink)
        p = jnp.exp2(s - mx)
        inv.append(1.0 / (jnp.sum(p, axis=-1, keepdims=True) + jnp.exp2(sink - mx)))
        probs.append(p.astype(bf16))
    between()
    outs = []
    for i in range(n):
        per_group = []
        for g in range(N_KV_HEADS):
            p_i = jnp.concatenate([probs[g * heads_per_group + hh][i * rows:(i + 1) * rows]
                                   for hh in range(heads_per_group)], axis=0)
            per_group.append(_dot(p_i, vdups[i][g]))
        outs.append(per_group)
    low_all = lax.broadcasted_iota(jnp.int32, (n * rows, LANES), 1) < HEAD_DIM
    for pair in range(N_Q_HEADS // 2):
        halves = []
        for head in (2 * pair, 2 * pair + 1):
            g, hh = divmod(head, heads_per_group)
            parts = [outs[i][g][hh * rows:(hh + 1) * rows] for i in range(n)]
            o = parts[0] if n == 1 else jnp.concatenate(parts, axis=0)
            halves.append(o * inv[head])
        o_ref[:, pair * LANES:(pair + 1) * LANES] = jnp.where(low_all, halves[0], halves[1]).astype(o_ref.dtype)


def _swa_prompt_blocks(q_ref, k_before, k2_ref, v_before, v2_ref, first, sink_ref, o_ref,
                       between=lambda: None):
    w = WINDOW
    nblk = q_ref.shape[0] // w

    def dup(before, ref):
        blocks = [before] + [ref[i * w:(i + 1) * w, :] for i in range(nblk)]
        return [[jnp.concatenate([blocks[i][:, g * LANES:(g + 1) * LANES],
                                  blocks[i + 1][:, g * LANES:(g + 1) * LANES]], axis=0)
                 for g in range(N_KV_HEADS)] for i in range(nblk)]

    row = lax.broadcasted_iota(jnp.int32, (nblk * w, 2 * w), 0)
    tk = lax.broadcasted_iota(jnp.int32, (nblk * w, 2 * w), 1)
    rel = tk - (row & (w - 1))
    first_key = jnp.where(row < w, jnp.where(first, w, 0), 0)
    mask = (rel > 0) & (rel <= w) & (tk >= first_key)
    _sink_attention(q_ref, w, dup(k_before, k2_ref), dup(v_before, v2_ref), mask, sink_ref, o_ref, between)


def _swa_sample_kernel(sink_ref, q_ref, kn_ref, vn_ref, pk_ref, pv_ref, o_ref, ko_ref, vo_ref, *, seq):
    nseq, _, _, w = pk_ref.shape
    rows = nseq * seq
    heads_per_group = N_Q_HEADS // N_KV_HEADS
    lane = lax.broadcasted_iota(jnp.int32, (rows, LANES), 1)
    r_id = lax.broadcasted_iota(jnp.int32, (rows, LANES), 0)
    low = lane < HEAD_DIM
    pos = r_id & (seq - 1)
    mask_old = lane > pos
    seq_shift = seq.bit_length() - 1
    mask_new = ((r_id >> seq_shift) == (lane >> seq_shift)) & ((lane & (seq - 1)) <= pos)
    kn = kn_ref[...]
    vn = vn_ref[...]
    kn_dup = [d.astype(bf16) for d in _dup_heads(kn, low)]
    vn_dup = [d.astype(bf16) for d in _dup_heads(vn, low)]

    pad = jnp.zeros((w - seq, LANES), f32)
    tail_lanes = lax.broadcasted_iota(jnp.int32, (HEAD_DIM, w), 1) >= w - seq
    kt_dup, vt_dup = [], []
    for b in range(nseq):
        tok = slice(b * seq, (b + 1) * seq)
        per_k, per_v = [], []
        for new, old_ref, out_ref, per in ((kn, pk_ref, ko_ref, per_k), (vn, pv_ref, vo_ref, per_v)):
            new_t = jnp.concatenate([pad, new[tok]], axis=0).T
            for g in range(N_KV_HEADS):
                old = old_ref[b, g]
                out_ref[b, g] = jnp.where(tail_lanes, new_t[g * HEAD_DIM:(g + 1) * HEAD_DIM],
                                          pltpu.roll(old, w - seq, axis=1))
                ob = old.astype(bf16)
                per.append(jnp.concatenate([ob, ob], axis=0))
        kt_dup.append(per_k)
        vt_dup.append(per_v)

    lhs = []
    for g in range(N_KV_HEADS):
        stack = []
        for p in range(heads_per_group // 2):
            pair = g * (heads_per_group // 2) + p
            qp = q_ref[:, pair * LANES:(pair + 1) * LANES].astype(bf16)
            zero = jnp.zeros_like(qp)
            stack.append(jnp.where(low, qp, zero))
            stack.append(jnp.where(low, zero, qp))
        lhs.append(jnp.concatenate(stack, axis=0))
    s_new = [_dot_nt(lhs[g], kn_dup[g]) for g in range(N_KV_HEADS)]
    s_old = []
    for b in range(nseq):
        per = []
        for g in range(N_KV_HEADS):
            qb = jnp.concatenate([lhs[g][hh * rows + b * seq:hh * rows + (b + 1) * seq]
                                  for hh in range(heads_per_group)], axis=0)
            per.append(_dot(qb, kt_dup[b][g]))
        s_old.append(per)

    p_old, p_new, inv = [], [], []
    for head in range(N_Q_HEADS):
        g, hh = divmod(head, heads_per_group)
        so = jnp.concatenate([s_old[b][g][hh * seq:(hh + 1) * seq] for b in range(nseq)], axis=0)
        so = jnp.where(mask_old, so, -1e30)
        sn = jnp.where(mask_new, s_new[g][hh * rows:(hh + 1) * rows], -1e30)
        sink = sink_ref[head] * LOG2E
        mx = jnp.maximum(jnp.maximum(jnp.max(so, axis=-1, keepdims=True),
                                     jnp.max(sn, axis=-1, keepdims=True)), sink)
        po = jnp.exp2(so - mx)
        pn = jnp.exp2(sn - mx)
        inv.append(1.0 / (jnp.sum(po, axis=-1, keepdims=True) + jnp.sum(pn, axis=-1, keepdims=True)
                          + jnp.exp2(sink - mx)))
        p_old.append(po.astype(bf16))
        p_new.append(pn.astype(bf16))

    o_new = [_dot(jnp.concatenate([p_new[g * heads_per_group + hh] for hh in range(heads_per_group)], axis=0),
                  vn_dup[g]) for g in range(N_KV_HEADS)]
    o_old = []
    for b in range(nseq):
        per = []
        for g in range(N_KV_HEADS):
            pb = jnp.concatenate([p_old[g * heads_per_group + hh][b * seq:(b + 1) * seq]
                                  for hh in range(heads_per_group)], axis=0)
            per.append(_dot_nt(pb, vt_dup[b][g]))
        o_old.append(per)

    for pair in range(N_Q_HEADS // 2):
        halves = []
        for head in (2 * pair, 2 * pair + 1):
            g, hh = divmod(head, heads_per_group)
            old = jnp.concatenate([o_old[b][g][hh * seq:(hh + 1) * seq] for b in range(nseq)], axis=0)
            halves.append((old + o_new[g][hh * rows:(hh + 1) * rows]) * inv[head])
        o_ref[:, pair * LANES:(pair + 1) * LANES] = jnp.where(low, halves[0], halves[1])


def _swa_sample(sq, sk, sv, past_kt, past_vt, sinks, bsz, t):
    nseq = LANES // t
    tok = lambda width: pl.BlockSpec((nseq * t, width), lambda i, s: (i, 0))
    cache = pl.BlockSpec((nseq,) + past_kt.shape[1:], lambda i, s: (i, 0, 0, 0))
    return pl.pallas_call(
        functools.partial(_swa_sample_kernel, seq=t),
        grid_spec=pltpu.PrefetchScalarGridSpec(
            num_scalar_prefetch=1,
            grid=(bsz // nseq,),
            in_specs=[tok(SWA_Q), tok(SWA_KV), tok(SWA_KV), cache, cache],
            out_specs=[tok(SWA_Q), cache, cache]),
        out_shape=[jax.ShapeDtypeStruct((bsz * t, SWA_Q), f32),
                   jax.ShapeDtypeStruct(past_kt.shape, f32),
                   jax.ShapeDtypeStruct(past_vt.shape, f32)],
        compiler_params=pltpu.CompilerParams(
            dimension_semantics=("parallel",), vmem_limit_bytes=VMEM_LIMIT),
        name="swa_sample",
    )(sinks, sq, sk, sv, past_kt, past_vt)


FFN_CHUNK = 512


def _out_proj_norm(streams, n2, wout_ref):
    mix = [s[1] for s in streams]
    mix = mix[0] if len(mix) == 1 else jnp.concatenate(mix, axis=0)
    mixed = _dot(mix[:, :GLA_WIDTH], wout_ref[:GLA_WIDTH, :]) + _dot(mix[:, GLA_WIDTH:], wout_ref[GLA_WIDTH:, :])
    hs, hbs, r0 = [], [], 0
    for x, _, g1, sh2, sc2 in streams:
        nb, t, d = x.shape
        h = x + g1 * mixed[r0:r0 + nb * t].reshape(nb, t, d)
        ms = jnp.mean(h * h, axis=-1, keepdims=True)
        hn = h * lax.rsqrt(ms + EPS) * n2
        hn = hn * (1.0 + sc2) + sh2
        hs.append(h)
        hbs.append(hn.reshape(nb * t, d).astype(bf16))
        r0 += nb * t
    return hs, (hbs[0] if len(hbs) == 1 else jnp.concatenate(hbs, axis=0))


def _ffn_piece(hb, w1_ref, w2_ref, c, width):
    cols = slice(c * width, (c + 1) * width)
    a = jnp.maximum(_dot(hb, w1_ref[:, cols]), 0.0)
    return _dot((a * a).astype(bf16), w2_ref[cols, :])


def _resident(shape):
    return pl.BlockSpec(shape, lambda *_: (0,) * len(shape), pipeline_mode=pl.Buffered(1))


def _prompt_kernel(sink_ref,
                   q_ref, k_ref, v_ref, la_ref, r_ref, sq_ref, k2_ref, v2_ref, k2p_ref, v2p_ref,
                   tri_ref, masks_ref, gnw_ref,
                   x_ref, g1_ref, sh2_ref, sc2_ref, g2_ref,
                   xs_ref, gos_ref, sos_ref, g1s_ref, sh2s_ref, sc2s_ref, g2s_ref,
                   n2_ref, wout_ref, w1_ref, w2_ref,
                   y_ref, ys_ref, sout_ref, state_ref, mix_ref, *, levels, nt, nsteps):
    s = pl.program_id(0)
    slot = s % 2

    @pl.when(s == 0)
    def _():
        mix_ref[1] = jnp.zeros(mix_ref.shape[1:], mix_ref.dtype)
        state_ref[...] = jnp.zeros_like(state_ref)

    j = jnp.minimum(s, nsteps - 1) % nt
    st_in = state_ref[...]
    st = jnp.where(j == 0, 0.0, st_in)
    gnw = gnw_ref[...]
    out = mix_ref.at[slot]
    mix_s = jnp.concatenate([gos_ref[...], sos_ref[...]], axis=1).astype(bf16)
    (h, h_s), hb = _out_proj_norm(
        [(x_ref[...], mix_ref[1 - slot], g1_ref[...], sh2_ref[...], sc2_ref[...]),
         (xs_ref[...], mix_s, g1s_ref[...], sh2s_ref[...], sc2s_ref[...])], n2_ref[...], wout_ref)
    rows_p = h.shape[0] * h.shape[1]
    todo = list(range(D_FF // FFN_CHUNK))
    ff = []

    def ffn_piece():
        if todo:
            part = _ffn_piece(hb, w1_ref, w2_ref, todo.pop(0), FFN_CHUNK)
            ff[:] = [part if not ff else ff[0] + part]

    st = _gla_prompt_block(st, q_ref, k_ref, v_ref, la_ref, r_ref, tri_ref, masks_ref, gnw,
                           out.at[:, pl.ds(0, GLA_WIDTH)], levels, ffn_piece)
    _swa_prompt_blocks(sq_ref, k2p_ref[...], k2_ref, v2p_ref[...], v2_ref, j == 0, sink_ref,
                       out.at[:, pl.ds(GLA_WIDTH, SWA_Q)], ffn_piece)
    while todo:
        ffn_piece()
    st = jnp.where(s < nsteps, st, st_in)
    state_ref[...] = st
    sout_ref[0] = st.T
    y_ref[...] = h + g2_ref[...] * ff[0][:rows_p].reshape(h.shape)
    ys_ref[...] = h_s + g2s_ref[...] * ff[0][rows_p:].reshape(h_s.shape)


def _prompt_mix_ffn(x, gq, gk, gv, la, gr, sq, k2, v2, gnw, sinks, mod, row0, n2, wout, w1, w2,
                    xs, gos, sos, row0_s):
    bsz, t, d = x.shape
    tb = FUSED_TOKEN_BLOCK
    nt = t // tb
    nsteps = bsz * nt
    levels, tri, masks = _gla_constants(PROMPT_GLA_CHUNK, PROMPT_GLA_CHUNK)
    per_blk = tb // WINDOW
    bs, ts, _ = xs.shape
    nbs = bs // nsteps
    assert nbs * nsteps == bs and (nbs * ts) % 16 == 0

    def mix_blk(s):
        return jnp.minimum(s, nsteps - 1)

    def ffn_blk(s):
        return jnp.maximum(s - 1, 0)

    def tok(width):
        return pl.BlockSpec((tb, width), lambda s, _: (mix_blk(s), 0))

    prev = pl.BlockSpec((WINDOW, SWA_KV2), lambda s, _: (jnp.maximum(mix_blk(s) * per_blk - 1, 0), 0))
    xspec = pl.BlockSpec((1, tb, d), lambda s, _: (ffn_blk(s) // nt, ffn_blk(s) % nt, 0))
    ffn_chunks = (MOD_GATE1, MOD_SHIFT2, MOD_SCALE2, MOD_GATE2)
    mods = [_mod_spec(1, row0, chunk, lambda s, _: ffn_blk(s) // nt) for chunk in ffn_chunks]
    mods_s = [_mod_spec(nbs, row0_s, chunk, lambda s, _: ffn_blk(s)) for chunk in ffn_chunks]
    xs_spec = pl.BlockSpec((nbs, ts, d), lambda s, _: (ffn_blk(s), 0, 0))
    mix_s_spec = pl.BlockSpec((nbs * ts, GLA_WIDTH), lambda s, _: (ffn_blk(s), 0))
    return pl.pallas_call(
        functools.partial(_prompt_kernel, levels=levels, nt=nt, nsteps=nsteps),
        grid_spec=pltpu.PrefetchScalarGridSpec(
            num_scalar_prefetch=1,
            grid=(nsteps + 1,),
            in_specs=[tok(GLA_QK), tok(GLA_QK), tok(GLA_WIDTH), tok(GLA_QK), tok(GLA_WIDTH),
                      tok(SWA_Q), tok(SWA_KV2), tok(SWA_KV2), prev, prev,
                      _resident(tri.shape), _resident(masks.shape), _resident((1, GLA_DV)),
                      xspec, *mods, xs_spec, mix_s_spec, mix_s_spec, *mods_s, _resident((1, 1, d)),
                      _resident((d, d)), _resident((d, D_FF)), _resident((D_FF, d))],
            out_specs=[xspec, xs_spec,
                       pl.BlockSpec((1, GLA_QK, GLA_DV), lambda s, _: (mix_blk(s) // nt, 0, 0))],
            scratch_shapes=[pltpu.VMEM((GLA_DV, GLA_QK), f32),
                            pltpu.VMEM((2, tb, GLA_WIDTH + SWA_Q), bf16)]),
        out_shape=[jax.ShapeDtypeStruct((bsz, t, d), f32),
                   jax.ShapeDtypeStruct(xs.shape, f32),
                   jax.ShapeDtypeStruct((bsz, GLA_QK, GLA_DV), f32)],
        compiler_params=pltpu.CompilerParams(
            dimension_semantics=("arbitrary",), vmem_limit_bytes=VMEM_LIMIT),
        name="prompt_mix_ffn",
    )(sinks, gq, gk, gv, la, gr, sq, k2, v2, k2, v2, tri, masks, gnw,
      x, mod, mod, mod, mod, xs, gos, sos, mod, mod, mod, mod, n2, wout, w1, w2)


def _rope_tables(pos, copies):
    half = HEAD_DIM // 2
    inv = np.power(np.float32(ROPE_THETA), -np.arange(half, dtype=np.float32) * np.float32(2.0 / HEAD_DIM))
    ang = np.asarray(pos, np.float32)[:, None] * inv[None, :].astype(np.float32)
    cos = np.cos(ang).astype(np.float32)
    sin = np.sin(ang).astype(np.float32)
    reps = (copies, LANES // HEAD_DIM)
    return (jnp.asarray(np.tile(np.concatenate([cos, cos], axis=-1), reps)),
            jnp.asarray(np.tile(np.concatenate([-sin, sin], axis=-1), reps)))


def _block_diag_ones(n, blk):
    idx = np.arange(n) // blk
    return jnp.asarray((idx[:, None] == idx[None, :]).astype(np.float32), bf16)


def _layer_weights(w_in, w_gate_up, b_gate, q_norm_w, k_norm_w, w_out, w_ff1, w_ff2):
    splits = np.cumsum([GLA_QK, GLA_QK, GLA_WIDTH, GLA_WIDTH, GLA_GATE_RANK, SWA_Q, SWA_KV])
    gq, gk, gv, gr, glr, sq, sk, sv = jnp.split(w_in, [int(s) for s in splits], axis=1)
    glr = jnp.pad(glr, ((0, 0), (0, LANES - GLA_GATE_RANK)))
    win = jnp.concatenate([gq, gk, gv, gr, sq, sk, sv, glr], axis=1).astype(bf16)
    wgu = jnp.pad(w_gate_up, ((0, LANES - GLA_GATE_RANK), (0, 0))).astype(bf16)
    return dict(
        win=win, wgu=wgu, bg=b_gate.reshape(1, GLA_QK),
        qnw=jnp.tile(q_norm_w, N_Q_HEADS).reshape(1, SWA_Q),
        knw=jnp.tile(k_norm_w, N_KV_HEADS).reshape(1, SWA_KV),
        ffn_f32=(w_out, w_ff1, w_ff2))


def _project(x, mod, row0, pos, lw, n1, bdq, bdk, prompt):
    bsz, t, d = x.shape
    if prompt:
        nb, tb, tail, act, groups, cast = 1, TOKEN_BLOCK, WINDOW, bf16, PROMPT_INPROJ_GROUPS, lw["ffn_f32"]
    else:
        nb, tb, tail, act, groups, cast = SAMPLE_ROWS // t, t, SAMPLE_ROWS, f32, 1, ()
    cos_t, sin_t = _rope_tables(pos, nb)
    return _inproj(x, mod, row0, n1.reshape(1, 1, d), lw["win"], lw["wgu"], lw["bg"], lw["qnw"], lw["knw"],
                   bdq, bdk, cos_t, sin_t, nb, tb, tail, act, groups, cast, tail_t=prompt)


def _decoder_layers(xp, xs, mod, pos_p, pos_s, state, past_k, past_v, lw, n1, n2, gnw, sinks, bdq, bdk):
    bp, tp, d = xp.shape
    bs, ts, _ = xs.shape
    heads = (N_KV_HEADS, HEAD_DIM)
    gq, gk, gv, gr, la, sq, _, _, kk, vk = _project(xs, mod, 0, pos_s, lw, n1, bdq, bdk, False)
    go_s, state_s = _gla_sample(gq, gk, gv, la, gr, state.reshape(bs, GLA_QK, GLA_DV), gnw, bs, ts)
    so_s, kt, vt = _swa_sample(sq, kk, vk, past_k.transpose(0, 2, 3, 1), past_v.transpose(0, 2, 3, 1),
                               sinks, bs, ts)
    gq, gk, gv, gr, la, sq, k2, v2, kk, vk, *ffn_w = _project(xp, mod, bs, pos_p, lw, n1, bdq, bdk, True)
    yp, ys, s_t = _prompt_mix_ffn(xp, gq, gk, gv, la, gr, sq, k2, v2, gnw, sinks, mod, bs,
                                  n2.reshape(1, 1, d), *ffn_w, xs, go_s, so_s, 0)
    state_p = s_t.reshape(bp, GLA_HEADS, GLA_DK, GLA_DV)
    return (yp, ys, state_p,
            kk.reshape(bp, *heads, WINDOW).transpose(0, 3, 1, 2), vk.reshape(bp, *heads, WINDOW).transpose(0, 3, 1, 2),
            state_s.reshape(bs, GLA_HEADS, GLA_DK, GLA_DV),
            kt.transpose(0, 3, 1, 2), vt.transpose(0, 3, 1, 2))


def kernel(x_prompt, x_sample, state_gla, cache_swa_k, cache_swa_v, c_prompt, c_sample, w_ada, b_ada, norm1_w, norm2_w, w_in, w_gate_up, b_gate, gla_norm_w, q_norm_w, k_norm_w, sinks, w_out, w_ff1, w_ff2):
    depth = w_ada.shape[0]
    bp, tp, _ = x_prompt.shape
    bs, ts, _ = x_sample.shape
    pos_p = np.arange(tp)
    pos_s = PAST_LEN + np.arange(ts)
    bdq = _block_diag_ones(SWA_Q, HEAD_DIM)
    bdk = _block_diag_ones(SWA_KV, HEAD_DIM)
    yp, ys = x_prompt, x_sample
    outs = [[] for _ in range(6)]
    for l in range(depth):
        mod = _modulation(c_sample, c_prompt, w_ada[l], b_ada[l])
        lw = _layer_weights(w_in[l], w_gate_up[l], b_gate[l], q_norm_w[l], k_norm_w[l],
                            w_out[l], w_ff1[l], w_ff2[l])
        gnw = gla_norm_w[l].reshape(1, GLA_DV)
        yp, ys, *new = _decoder_layers(yp, ys, mod, pos_p, pos_s, state_gla[l], cache_swa_k[l],
                                       cache_swa_v[l], lw, norm1_w[l], norm2_w[l], gnw, sinks[l], bdq, bdk)
        for lst, val in zip(outs, new):
            lst.append(val)
    return (yp, ys) + tuple(jnp.stack(o) for o in outs)
```
